```python
import jax, jax.numpy as jnp
from jax import lax
import numpy as np

D_MODEL = 2048
BATCH = 8
SEQ = 8192
DEPTH = 2

ATT_HEAD_DIM = 64
ATT_Q_HEADS = 16
ATT_KV_HEADS = 4
ATT_GROUP = ATT_Q_HEADS // ATT_KV_HEADS
WINDOW = 128
ATT_BLOCK = WINDOW
ATT_WIDTH = ATT_Q_HEADS * ATT_HEAD_DIM
KV_WIDTH = ATT_KV_HEADS * ATT_HEAD_DIM

POOL_WINDOWS = (2, 4, 8, 16)
POOL_GROUPS = len(POOL_WINDOWS)
POOL_WIDTH = D_MODEL // 2
POOL_GROUP_DIM = POOL_WIDTH // POOL_GROUPS

D_INNER = D_MODEL
SSM_HEAD_DIM = 64
SSM_HEADS = D_INNER // SSM_HEAD_DIM
SSM_GROUPS = 4
HEADS_PER_GROUP = SSM_HEADS // SSM_GROUPS
D_STATE = 128
CONV_K = 4
CHUNK = 128
CONV_CH = D_INNER + 2 * SSM_GROUPS * D_STATE

N_BRANCH = 3
D_FF = -(-8 * D_MODEL // (3 * 256)) * 256
EPS = 1e-6

IN_WIDTHS = (ATT_WIDTH, KV_WIDTH, KV_WIDTH, POOL_WIDTH, D_INNER, CONV_CH, SSM_HEADS, N_BRANCH * D_MODEL)
IN_COLS = sum(IN_WIDTHS)

kernel_name = 'hybrid_gated_swa_pool_ssd_block'


def split_points(widths):
    pts, acc = [], 0
    for w in widths[:-1]:
        acc += w
        pts.append(acc)
    return pts


def rmsnorm(x, w):
    xf = x.astype(jnp.float32)
    y = xf * lax.rsqrt(jnp.mean(xf * xf, axis=-1, keepdims=True) + EPS)
    return (y * w.astype(jnp.float32)).astype(x.dtype)


def sink_window_attention(q, k, v, sink):
    b, l = q.shape[0], q.shape[1]
    nb = l // ATT_BLOCK
    qb = q.reshape(b, nb, ATT_BLOCK, ATT_KV_HEADS, ATT_GROUP, ATT_HEAD_DIM)
    kb = k.reshape(b, nb, ATT_BLOCK, ATT_KV_HEADS, ATT_HEAD_DIM)
    vb = v.reshape(b, nb, ATT_BLOCK, ATT_KV_HEADS, ATT_HEAD_DIM)

    def with_prev(t):
        prev = jnp.pad(t[:, :-1], ((0, 0), (1, 0), (0, 0), (0, 0), (0, 0)))
        return jnp.concatenate([prev, t], axis=2)

    kk, vv = with_prev(kb), with_prev(vb)
    scores = jnp.einsum('bnqkgd,bnskd->bnkgqs', qb, kk).astype(jnp.float32) * (ATT_HEAD_DIM ** -0.5)
    blk = jnp.arange(nb)[:, None, None]
    qpos = blk * ATT_BLOCK + jnp.arange(ATT_BLOCK)[None, :, None]
    kpos = (blk - 1) * ATT_BLOCK + jnp.arange(2 * ATT_BLOCK)[None, None, :]
    diff = qpos - kpos
    mask = (diff >= 0) & (diff < WINDOW) & (kpos >= 0)
    scores = jnp.where(mask[None, :, None, None], scores, -jnp.inf)
    sink_l = sink.astype(jnp.float32).reshape(ATT_KV_HEADS, ATT_GROUP)[None, None, :, :, None, None]
    sink_l = jnp.broadcast_to(sink_l, scores.shape[:-1] + (1,))
    probs = jax.nn.softmax(jnp.concatenate([scores, sink_l], axis=-1), axis=-1)[..., :-1]
    out = jnp.einsum('bnkgqs,bnskd->bnqkgd', probs.astype(v.dtype), vv)
    return out.reshape(b, l, ATT_WIDTH)


def multiscale_pool(u, pool_w, pool_scale):
    b, l, _ = u.shape
    ug = u.reshape(b, l, POOL_GROUPS, POOL_GROUP_DIM).astype(jnp.float32)
    cs = jnp.cumsum(ug, axis=1)
    t = jnp.arange(l)
    means = []
    for gi, w in enumerate(POOL_WINDOWS):
        c = cs[:, :, gi]
        shifted = jnp.pad(c, ((0, 0), (w, 0), (0, 0)))[:, :l]
        cnt = jnp.minimum(t + 1, w).astype(jnp.float32)[None, :, None]
        means.append((c - shifted) / cnt)
    mixed = (jnp.stack(means, axis=2) - ug).astype(u.dtype)
    y = jnp.einsum('blgc,gcd->blgd', mixed, pool_w).reshape(b, l, POOL_WIDTH)
    return y * pool_scale


def ssd_mixer(z, xbc, dt_raw, conv_w, conv_b, dt_bias, a_log, d_skip, norm_w):
    b, l, _ = xbc.shape
    xbc = lax.conv_general_dilated(xbc, conv_w[:, None, :].astype(xbc.dtype), window_strides=(1,),
                                   padding=[(CONV_K - 1, 0)], dimension_numbers=('NWC', 'WIO', 'NWC'),
                                   feature_group_count=CONV_CH) + conv_b
    xbc = jax.nn.silu(xbc)
    nc = l // CHUNK
    xs = xbc[..., :D_INNER].astype(jnp.float32).reshape(b, nc, CHUNK, SSM_GROUPS, HEADS_PER_GROUP, SSM_HEAD_DIM)
    bm = xbc[..., D_INNER:D_INNER + SSM_GROUPS * D_STATE].astype(jnp.float32).reshape(b, nc, CHUNK, SSM_GROUPS, D_STATE)
    cm = xbc[..., D_INNER + SSM_GROUPS * D_STATE:].astype(jnp.float32).reshape(b, nc, CHUNK, SSM_GROUPS, D_STATE)
    dt = jax.nn.softplus(dt_raw.astype(jnp.float32) + dt_bias.astype(jnp.float32))
    dt = dt.reshape(b, nc, CHUNK, SSM_GROUPS, HEADS_PER_GROUP)
    a = -jnp.exp(a_log.astype(jnp.float32)).reshape(SSM_GROUPS, HEADS_PER_GROUP)
    a_cs = jnp.cumsum(dt * a, axis=2)
    xdt = xs * dt[..., None]
    seg = a_cs[:, :, :, None] - a_cs[:, :, None, :]
    causal = jnp.tril(jnp.ones((CHUNK, CHUNK), dtype=bool))
    decay = jnp.exp(jnp.where(causal[:, :, None, None], seg, -jnp.inf))
    cb = jnp.einsum('bclgn,bcsgn->bclsg', cm, bm)
    y_diag = jnp.einsum('bclsg,bclsge,bcsgep->bclgep', cb, decay, xdt)
    decay_to_end = jnp.exp(a_cs[:, :, -1:] - a_cs)
    states = jnp.einsum('bcsgn,bcsge,bcsgep->bcgepn', bm, decay_to_end, xdt)
    chunk_decay = jnp.exp(a_cs[:, :, -1])

    def step(h, inp):
        st, dec = inp
        return h * dec[..., None, None] + st, h

    h0 = jnp.zeros((b, SSM_GROUPS, HEADS_PER_GROUP, SSM_HEAD_DIM, D_STATE), jnp.float32)
    _, prev = lax.scan(step, h0, (jnp.moveaxis(states, 1, 0), jnp.moveaxis(chunk_decay, 1, 0)))
    prev = jnp.moveaxis(prev, 0, 1)
    y_off = jnp.einsum('bclgn,bcgepn,bclge->bclgep', cm, prev, jnp.exp(a_cs))
    y = y_diag + y_off + d_skip.astype(jnp.float32).reshape(SSM_GROUPS, HEADS_PER_GROUP)[:, :, None] * xs
    y = y.reshape(b, l, D_INNER)
    g = (y * jax.nn.silu(z.astype(jnp.float32))).reshape(b, l, SSM_GROUPS, D_INNER // SSM_GROUPS)
    g = g * lax.rsqrt(jnp.mean(g * g, axis=-1, keepdims=True) + EPS)
    return (g.reshape(b, l, D_INNER) * norm_w.astype(jnp.float32)).astype(z.dtype)


def _fwd_setup_inputs(seed: int = 0) -> dict:
    key = jax.random.key(seed)
    ks = jax.random.split(key, 24)
    f32 = jnp.float32

    def nrm(k, shape, scale):
        return jax.random.normal(k, shape, f32) * scale

    dt0 = jnp.exp(jax.random.uniform(ks[6], (DEPTH, SSM_HEADS), f32, np.log(1e-3), np.log(1e-1)))
    dt_bias = dt0 + jnp.log(-jnp.expm1(-dt0))
    a_log = jnp.log(jax.random.uniform(ks[7], (DEPTH, SSM_HEADS), f32, 1.0, 16.0))
    return {
        'x': nrm(ks[0], (BATCH, SEQ, D_MODEL), 1.0),
        'ln1_w': 1.0 + nrm(ks[1], (DEPTH, D_MODEL), 0.05),
        'w_in': nrm(ks[2], (DEPTH, D_MODEL, IN_COLS), D_MODEL ** -0.5),
        'attn_sink': nrm(ks[3], (DEPTH, ATT_Q_HEADS), 0.5),
        'conv_w': nrm(ks[4], (DEPTH, CONV_K, CONV_CH), CONV_K ** -0.5),
        'conv_b': nrm(ks[5], (DEPTH, CONV_CH), 0.02),
        'dt_bias': dt_bias,
        'a_log': a_log,
        'd_skip': 1.0 + nrm(ks[8], (DEPTH, SSM_HEADS), 0.1),
        'ssm_norm_w': 1.0 + nrm(ks[9], (DEPTH, D_INNER), 0.05),
        'pool_w': nrm(ks[10], (DEPTH, POOL_GROUPS, POOL_GROUP_DIM, POOL_GROUP_DIM), POOL_GROUP_DIM ** -0.5),
        'pool_scale': 1.0 + nrm(ks[11], (DEPTH, POOL_WIDTH), 0.1),
        'w_attn_br': nrm(ks[12], (DEPTH, ATT_WIDTH, D_MODEL), ATT_WIDTH ** -0.5),
        'w_pool_br': nrm(ks[13], (DEPTH, POOL_WIDTH, D_MODEL), POOL_WIDTH ** -0.5),
        'w_ssm_br': nrm(ks[14], (DEPTH, D_INNER, D_MODEL), D_INNER ** -0.5),
        'w_out': nrm(ks[15], (DEPTH, D_MODEL, D_MODEL), D_MODEL ** -0.5),
        'ln2_w': 1.0 + nrm(ks[16], (DEPTH, D_MODEL), 0.05),
        'w_gate_up': nrm(ks[17], (DEPTH, D_MODEL, 2 * D_FF), D_MODEL ** -0.5),
        'w_down': nrm(ks[18], (DEPTH, D_FF, D_MODEL), D_FF ** -0.5),
        'final_w': 1.0 + nrm(ks[19], (D_MODEL,), 0.05),
    }


def _fwd_reference(x, ln1_w, w_in, attn_sink, conv_w, conv_b, dt_bias, a_log, d_skip, ssm_norm_w,
              pool_w, pool_scale, w_attn_br, w_pool_br, w_ssm_br, w_out, ln2_w, w_gate_up,
              w_down, final_w):
    b, l, _ = x.shape
    pts = split_points(IN_WIDTHS)
    for i in range(DEPTH):
        h = rmsnorm(x, ln1_w[i])
        proj = h @ w_in[i]
        q, k, v, u, z, xbc, dt_raw, gate_logits = jnp.split(proj, pts, axis=-1)
        att = sink_window_attention(q.reshape(b, l, ATT_Q_HEADS, ATT_HEAD_DIM),
                                    k.reshape(b, l, ATT_KV_HEADS, ATT_HEAD_DIM),
                                    v.reshape(b, l, ATT_KV_HEADS, ATT_HEAD_DIM), attn_sink[i])
        pool = multiscale_pool(u, pool_w[i], pool_scale[i])
        ssm = ssd_mixer(z, xbc, dt_raw, conv_w[i], conv_b[i], dt_bias[i], a_log[i], d_skip[i], ssm_norm_w[i])
        gates = jax.nn.sigmoid(gate_logits.astype(jnp.float32)).astype(x.dtype).reshape(b, l, N_BRANCH, D_MODEL)
        merged = (gates[:, :, 0] * (att @ w_attn_br[i])
                  + gates[:, :, 1] * (pool @ w_pool_br[i])
                  + gates[:, :, 2] * (ssm @ w_ssm_br[i]))
        x = x + merged @ w_out[i]
        h2 = rmsnorm(x, ln2_w[i])
        gu = h2 @ w_gate_up[i]
        x = x + (jax.nn.silu(gu[..., :D_FF]) * gu[..., D_FF:]) @ w_down[i]
    return rmsnorm(x, final_w)


import jax as _jax
import jax.numpy as _jnp

TWIN_FORMAT = 'train_step'
FWD_PARAMS = ['x', 'ln1_w', 'w_in', 'attn_sink', 'conv_w', 'conv_b', 'dt_bias', 'a_log', 'd_skip', 'ssm_norm_w', 'pool_w', 'pool_scale', 'w_attn_br', 'w_pool_br', 'w_ssm_br', 'w_out', 'ln2_w', 'w_gate_up', 'w_down', 'final_w']
TWIN_WEIGHTS = ['ln1_w', 'w_in', 'attn_sink', 'conv_w', 'conv_b', 'dt_bias', 'a_log', 'd_skip', 'ssm_norm_w', 'pool_w', 'pool_scale', 'w_attn_br', 'w_pool_br', 'w_ssm_br', 'w_out', 'ln2_w', 'w_gate_up', 'w_down', 'final_w']
TWIN_DIFF_INPUT = 'x'
TWIN_INPUTS = ['x', 'ln1_w', 'w_in', 'attn_sink', 'conv_w', 'conv_b', 'dt_bias', 'a_log', 'd_skip', 'ssm_norm_w', 'pool_w', 'pool_scale', 'w_attn_br', 'w_pool_br', 'w_ssm_br', 'w_out', 'ln2_w', 'w_gate_up', 'w_down', 'final_w', 'loss_target', 'm_ln1_w', 'm_w_in', 'm_attn_sink', 'm_conv_w', 'm_conv_b', 'm_dt_bias', 'm_a_log', 'm_d_skip', 'm_ssm_norm_w', 'm_pool_w', 'm_pool_scale', 'm_w_attn_br', 'm_w_pool_br', 'm_w_ssm_br', 'm_w_out', 'm_ln2_w', 'm_w_gate_up', 'm_w_down', 'm_final_w', 'v_ln1_w', 'v_w_in', 'v_attn_sink', 'v_conv_w', 'v_conv_b', 'v_dt_bias', 'v_a_log', 'v_d_skip', 'v_ssm_norm_w', 'v_pool_w', 'v_pool_scale', 'v_w_attn_br', 'v_w_pool_br', 'v_w_ssm_br', 'v_w_out', 'v_ln2_w', 'v_w_gate_up', 'v_w_down', 'v_final_w']
TWIN_OUTPUTS = ['loss', 'grad_x', 'grad_ln1_w', 'grad_w_in', 'grad_attn_sink', 'grad_conv_w', 'grad_conv_b', 'grad_dt_bias', 'grad_a_log', 'grad_d_skip', 'grad_ssm_norm_w', 'grad_pool_w', 'grad_pool_scale', 'grad_w_attn_br', 'grad_w_pool_br', 'grad_w_ssm_br', 'grad_w_out', 'grad_ln2_w', 'grad_w_gate_up', 'grad_w_down', 'grad_final_w', 'delta_ln1_w', 'delta_w_in', 'delta_attn_sink', 'delta_conv_w', 'delta_conv_b', 'delta_dt_bias', 'delta_a_log', 'delta_d_skip', 'delta_ssm_norm_w', 'delta_pool_w', 'delta_pool_scale', 'delta_w_attn_br', 'delta_w_pool_br', 'delta_w_ssm_br', 'delta_w_out', 'delta_ln2_w', 'delta_w_gate_up', 'delta_w_down', 'delta_final_w', 'new_m_ln1_w', 'new_m_w_in', 'new_m_attn_sink', 'new_m_conv_w', 'new_m_conv_b', 'new_m_dt_bias', 'new_m_a_log', 'new_m_d_skip', 'new_m_ssm_norm_w', 'new_m_pool_w', 'new_m_pool_scale', 'new_m_w_attn_br', 'new_m_w_pool_br', 'new_m_w_ssm_br', 'new_m_w_out', 'new_m_ln2_w', 'new_m_w_gate_up', 'new_m_w_down', 'new_m_final_w', 'new_v_ln1_w', 'new_v_w_in', 'new_v_attn_sink', 'new_v_conv_w', 'new_v_conv_b', 'new_v_dt_bias', 'new_v_a_log', 'new_v_d_skip', 'new_v_ssm_norm_w', 'new_v_pool_w', 'new_v_pool_scale', 'new_v_w_attn_br', 'new_v_w_pool_br', 'new_v_w_ssm_br', 'new_v_w_out', 'new_v_ln2_w', 'new_v_w_gate_up', 'new_v_w_down', 'new_v_final_w']
TWIN_LEAF_KINDS = {'loss': 'loss', 'grad_x': 'grad_x', 'grad_ln1_w': 'grad_w', 'grad_w_in': 'grad_w', 'grad_attn_sink': 'grad_w', 'grad_conv_w': 'grad_w', 'grad_conv_b': 'grad_w', 'grad_dt_bias': 'grad_w', 'grad_a_log': 'grad_w', 'grad_d_skip': 'grad_w', 'grad_ssm_norm_w': 'grad_w', 'grad_pool_w': 'grad_w', 'grad_pool_scale': 'grad_w', 'grad_w_attn_br': 'grad_w', 'grad_w_pool_br': 'grad_w', 'grad_w_ssm_br': 'grad_w', 'grad_w_out': 'grad_w', 'grad_ln2_w': 'grad_w', 'grad_w_gate_up': 'grad_w', 'grad_w_down': 'grad_w', 'grad_final_w': 'grad_w', 'delta_ln1_w': 'delta_w', 'delta_w_in': 'delta_w', 'delta_attn_sink': 'delta_w', 'delta_conv_w': 'delta_w', 'delta_conv_b': 'delta_w', 'delta_dt_bias': 'delta_w', 'delta_a_log': 'delta_w', 'delta_d_skip': 'delta_w', 'delta_ssm_norm_w': 'delta_w', 'delta_pool_w': 'delta_w', 'delta_pool_scale': 'delta_w', 'delta_w_attn_br': 'delta_w', 'delta_w_pool_br': 'delta_w', 'delta_w_ssm_br': 'delta_w', 'delta_w_out': 'delta_w', 'delta_ln2_w': 'delta_w', 'delta_w_gate_up': 'delta_w', 'delta_w_down': 'delta_w', 'delta_final_w': 'delta_w', 'new_m_ln1_w': 'new_m', 'new_m_w_in': 'new_m', 'new_m_attn_sink': 'new_m', 'new_m_conv_w': 'new_m', 'new_m_conv_b': 'new_m', 'new_m_dt_bias': 'new_m', 'new_m_a_log': 'new_m', 'new_m_d_skip': 'new_m', 'new_m_ssm_norm_w': 'new_m', 'new_m_pool_w': 'new_m', 'new_m_pool_scale': 'new_m', 'new_m_w_attn_br': 'new_m', 'new_m_w_pool_br': 'new_m', 'new_m_w_ssm_br': 'new_m', 'new_m_w_out': 'new_m', 'new_m_ln2_w': 'new_m', 'new_m_w_gate_up': 'new_m', 'new_m_w_down': 'new_m', 'new_m_final_w': 'new_m', 'new_v_ln1_w': 'new_v', 'new_v_w_in': 'new_v', 'new_v_attn_sink': 'new_v', 'new_v_conv_w': 'new_v', 'new_v_conv_b': 'new_v', 'new_v_dt_bias': 'new_v', 'new_v_a_log': 'new_v', 'new_v_d_skip': 'new_v', 'new_v_ssm_norm_w': 'new_v', 'new_v_pool_w': 'new_v', 'new_v_pool_scale': 'new_v', 'new_v_w_attn_br': 'new_v', 'new_v_w_pool_br': 'new_v', 'new_v_w_ssm_br': 'new_v', 'new_v_w_out': 'new_v', 'new_v_ln2_w': 'new_v', 'new_v_w_gate_up': 'new_v', 'new_v_w_down': 'new_v', 'new_v_final_w': 'new_v'}


def _forward(args):
    return _fwd_reference(*[args[k] for k in FWD_PARAMS])


def _output_shape():
    def fwd():
        inp = _fwd_setup_inputs(0)
        return _fwd_reference(*[inp[k] for k in FWD_PARAMS])
    out = _jax.eval_shape(fwd)
    return out.shape, out.dtype

N_MICROBATCH = 1
ADAM_LR = 0.001
ADAM_B1 = 0.9
ADAM_B2 = 0.999
ADAM_EPS = 1e-08
ADAM_WD = 0.01
ADAM_STEP = 10
PER_EXAMPLE_BATCH_AXIS = {'x': 0, 'loss_target': 0}
SHARED_INPUTS = []
_WEIGHT_DTYPES = {'ln1_w': _jnp.float32, 'w_in': _jnp.float32, 'attn_sink': _jnp.float32, 'conv_w': _jnp.float32, 'conv_b': _jnp.float32, 'dt_bias': _jnp.float32, 'a_log': _jnp.float32, 'd_skip': _jnp.float32, 'ssm_norm_w': _jnp.float32, 'pool_w': _jnp.float32, 'pool_scale': _jnp.float32, 'w_attn_br': _jnp.float32, 'w_pool_br': _jnp.float32, 'w_ssm_br': _jnp.float32, 'w_out': _jnp.float32, 'ln2_w': _jnp.float32, 'w_gate_up': _jnp.float32, 'w_down': _jnp.float32, 'final_w': _jnp.float32}
MOMENT_SCALE = {'ln1_w': 1.115289e-01, 'w_in': 4.276691e-02, 'attn_sink': 1.547721e-02, 'conv_w': 5.188006e-02, 'conv_b': 8.155737e-02, 'dt_bias': 1.365940e-01, 'a_log': 4.790965e-01, 'd_skip': 3.260099e-01, 'ssm_norm_w': 6.027357e-02, 'pool_w': 7.601787e-02, 'pool_scale': 7.631869e-02, 'w_attn_br': 1.274120e-02, 'w_pool_br': 5.386231e-02, 'w_ssm_br': 6.094843e-02, 'w_out': 8.160791e-02, 'ln2_w': 8.014433e-02, 'w_gate_up': 3.398211e-02, 'w_down': 5.571806e-02, 'final_w': 3.200821e+01}


def _to_microbatches(a, axis):
    t = _jnp.moveaxis(a, axis, 0)
    t = t.reshape((N_MICROBATCH, t.shape[0] // N_MICROBATCH) + t.shape[1:])
    return _jnp.moveaxis(t, 1, axis + 1)


def setup_inputs(seed: int = 0) -> dict:
    inp = _fwd_setup_inputs(seed)
    key = _jax.random.fold_in(_jax.random.key(seed), 7919)
    shape, _ = _output_shape()
    out = dict(inp)
    out["loss_target"] = _jax.random.normal(_jax.random.fold_in(key, 0), shape, _jnp.float32)
    for i, name in enumerate(TWIN_WEIGHTS):
        w = inp[name].astype(_jnp.float32)
        if MOMENT_SCALE is None:
            s = _jnp.sqrt(_jnp.mean(_jnp.square(w)) + 1e-30)
        else:
            s = MOMENT_SCALE[name]
        km, kv = _jax.random.split(_jax.random.fold_in(key, i + 1))
        out[name] = w
        out["m_" + name] = s * _jax.random.normal(km, w.shape, _jnp.float32)
        out["v_" + name] = (s * s) * _jax.random.uniform(kv, w.shape, _jnp.float32, 0.5, 1.5)
    if N_MICROBATCH > 1:
        for name, axis in PER_EXAMPLE_BATCH_AXIS.items():
            out[name] = _to_microbatches(out[name], axis)
    return {'x': out['x'], 'ln1_w': out['ln1_w'], 'w_in': out['w_in'], 'attn_sink': out['attn_sink'], 'conv_w': out['conv_w'], 'conv_b': out['conv_b'], 'dt_bias': out['dt_bias'], 'a_log': out['a_log'], 'd_skip': out['d_skip'], 'ssm_norm_w': out['ssm_norm_w'], 'pool_w': out['pool_w'], 'pool_scale': out['pool_scale'], 'w_attn_br': out['w_attn_br'], 'w_pool_br': out['w_pool_br'], 'w_ssm_br': out['w_ssm_br'], 'w_out': out['w_out'], 'ln2_w': out['ln2_w'], 'w_gate_up': out['w_gate_up'], 'w_down': out['w_down'], 'final_w': out['final_w'], 'loss_target': out['loss_target'], 'm_ln1_w': out['m_ln1_w'], 'm_w_in': out['m_w_in'], 'm_attn_sink': out['m_attn_sink'], 'm_conv_w': out['m_conv_w'], 'm_conv_b': out['m_conv_b'], 'm_dt_bias': out['m_dt_bias'], 'm_a_log': out['m_a_log'], 'm_d_skip': out['m_d_skip'], 'm_ssm_norm_w': out['m_ssm_norm_w'], 'm_pool_w': out['m_pool_w'], 'm_pool_scale': out['m_pool_scale'], 'm_w_attn_br': out['m_w_attn_br'], 'm_w_pool_br': out['m_w_pool_br'], 'm_w_ssm_br': out['m_w_ssm_br'], 'm_w_out': out['m_w_out'], 'm_ln2_w': out['m_ln2_w'], 'm_w_gate_up': out['m_w_gate_up'], 'm_w_down': out['m_w_down'], 'm_final_w': out['m_final_w'], 'v_ln1_w': out['v_ln1_w'], 'v_w_in': out['v_w_in'], 'v_attn_sink': out['v_attn_sink'], 'v_conv_w': out['v_conv_w'], 'v_conv_b': out['v_conv_b'], 'v_dt_bias': out['v_dt_bias'], 'v_a_log': out['v_a_log'], 'v_d_skip': out['v_d_skip'], 'v_ssm_norm_w': out['v_ssm_norm_w'], 'v_pool_w': out['v_pool_w'], 'v_pool_scale': out['v_pool_scale'], 'v_w_attn_br': out['v_w_attn_br'], 'v_w_pool_br': out['v_w_pool_br'], 'v_w_ssm_br': out['v_w_ssm_br'], 'v_w_out': out['v_w_out'], 'v_ln2_w': out['v_ln2_w'], 'v_w_gate_up': out['v_w_gate_up'], 'v_w_down': out['v_w_down'], 'v_final_w': out['v_final_w']}


def _loss(weights, diff, rest, loss_target):
    with _jax.named_scope("forward"):
        args = {**rest, TWIN_DIFF_INPUT: diff, **{k: w.astype(_WEIGHT_DTYPES[k]) for k, w in weights.items()}}
        y = _forward(args)
    with _jax.named_scope("loss_head"):
        err = _jnp.square(y.astype(_jnp.float32) - loss_target)
        return 0.5 * _jnp.sum(_jnp.mean(err, axis=-1)) if err.ndim else 0.5 * err


def _adamw(w, g, m, v):
    m = ADAM_B1 * m + (1.0 - ADAM_B1) * g
    v = ADAM_B2 * v + (1.0 - ADAM_B2) * _jnp.square(g)
    m_hat = m / (1.0 - ADAM_B1 ** ADAM_STEP)
    v_hat = v / (1.0 - ADAM_B2 ** ADAM_STEP)
    delta = -ADAM_LR * (m_hat / (_jnp.sqrt(v_hat) + ADAM_EPS) + ADAM_WD * w)
    return delta, m, v


def reference(x, ln1_w, w_in, attn_sink, conv_w, conv_b, dt_bias, a_log, d_skip, ssm_norm_w, pool_w, pool_scale, w_attn_br, w_pool_br, w_ssm_br, w_out, ln2_w, w_gate_up, w_down, final_w, loss_target, m_ln1_w, m_w_in, m_attn_sink, m_conv_w, m_conv_b, m_dt_bias, m_a_log, m_d_skip, m_ssm_norm_w, m_pool_w, m_pool_scale, m_w_attn_br, m_w_pool_br, m_w_ssm_br, m_w_out, m_ln2_w, m_w_gate_up, m_w_down, m_final_w, v_ln1_w, v_w_in, v_attn_sink, v_conv_w, v_conv_b, v_dt_bias, v_a_log, v_d_skip, v_ssm_norm_w, v_pool_w, v_pool_scale, v_w_attn_br, v_w_pool_br, v_w_ssm_br, v_w_out, v_ln2_w, v_w_gate_up, v_w_down, v_final_w):
    given = dict(x=x, ln1_w=ln1_w, w_in=w_in, attn_sink=attn_sink, conv_w=conv_w, conv_b=conv_b, dt_bias=dt_bias, a_log=a_log, d_skip=d_skip, ssm_norm_w=ssm_norm_w, pool_w=pool_w, pool_scale=pool_scale, w_attn_br=w_attn_br, w_pool_br=w_pool_br, w_ssm_br=w_ssm_br, w_out=w_out, ln2_w=ln2_w, w_gate_up=w_gate_up, w_down=w_down, final_w=final_w, loss_target=loss_target, m_ln1_w=m_ln1_w, m_w_in=m_w_in, m_attn_sink=m_attn_sink, m_conv_w=m_conv_w, m_conv_b=m_conv_b, m_dt_bias=m_dt_bias, m_a_log=m_a_log, m_d_skip=m_d_skip, m_ssm_norm_w=m_ssm_norm_w, m_pool_w=m_pool_w, m_pool_scale=m_pool_scale, m_w_attn_br=m_w_attn_br, m_w_pool_br=m_w_pool_br, m_w_ssm_br=m_w_ssm_br, m_w_out=m_w_out, m_ln2_w=m_ln2_w, m_w_gate_up=m_w_gate_up, m_w_down=m_w_down, m_final_w=m_final_w, v_ln1_w=v_ln1_w, v_w_in=v_w_in, v_attn_sink=v_attn_sink, v_conv_w=v_conv_w, v_conv_b=v_conv_b, v_dt_bias=v_dt_bias, v_a_log=v_a_log, v_d_skip=v_d_skip, v_ssm_norm_w=v_ssm_norm_w, v_pool_w=v_pool_w, v_pool_scale=v_pool_scale, v_w_attn_br=v_w_attn_br, v_w_pool_br=v_w_pool_br, v_w_ssm_br=v_w_ssm_br, v_w_out=v_w_out, v_ln2_w=v_ln2_w, v_w_gate_up=v_w_gate_up, v_w_down=v_w_down, v_final_w=v_final_w)
    weights = {n: given[n] for n in TWIN_WEIGHTS}
    shared = {n: given[n] for n in SHARED_INPUTS}
    per_example = {n: given[n] for n in ['x']}
    grad_fn = _jax.value_and_grad(_loss, argnums=(0, 1))

    def one_microbatch(ex, loss_target):
        ex = dict(ex)
        diff = ex.pop(TWIN_DIFF_INPUT)
        return grad_fn(weights, diff, {**shared, **ex}, loss_target)

    if N_MICROBATCH == 1:
        loss, (grad_w, grad_x) = one_microbatch(per_example, given["loss_target"])
    else:
        def body(carry, xs):
            loss_sum, grad_sum = carry
            l_k, (gw_k, gx_k) = one_microbatch(xs[0], xs[1])
            with _jax.named_scope("update"):
                return (loss_sum + l_k, _jax.tree.map(_jnp.add, grad_sum, gw_k)), gx_k

        init = (_jnp.zeros((), _jnp.float32), _jax.tree.map(_jnp.zeros_like, weights))
        (loss, grad_w), grad_x = _jax.lax.scan(body, init, (per_example, given["loss_target"]))
    with _jax.named_scope("update"):
        delta_w, new_m, new_v = {}, {}, {}
        for n in TWIN_WEIGHTS:
            delta_w[n], new_m[n], new_v[n] = _adamw(weights[n], grad_w[n], given["m_" + n], given["v_" + n])
    return (loss, grad_x, *[grad_w[n] for n in TWIN_WEIGHTS], *[delta_w[n] for n in TWIN_WEIGHTS],
            *[new_m[n] for n in TWIN_WEIGHTS], *[new_v[n] for n in TWIN_WEIGHTS])
```

```python
import functools

import jax
import jax.numpy as jnp
from jax import lax
from jax.experimental import pallas as pl
from jax.experimental.pallas import tpu as pltpu

F32 = jnp.float32
MXU = jnp.bfloat16
VMEM_LIMIT = 56 * 1024 * 1024
EPS = 1e-6
NEG = -1e30

ADAM_LR, ADAM_B1, ADAM_B2, ADAM_EPS, ADAM_WD, ADAM_STEP = 0.001, 0.9, 0.999, 1e-08, 0.01, 10


class Cfg:
    def __init__(self, d_model=2048, seq=8192, depth=2, q_heads=16, kv_heads=4, head_dim=64,
                 ssm_head_dim=64, ssm_groups=4, d_state=128):
        self.D, self.T, self.depth = d_model, seq, depth
        self.hd, self.qh, self.kvh = head_dim, q_heads, kv_heads
        self.AW, self.KW = q_heads * head_dim, kv_heads * head_dim
        self.blk = 128
        self.PW = d_model // 2
        self.PG = self.PW // 4
        self.DI = d_model
        self.P = ssm_head_dim
        self.H = self.DI // self.P
        self.G = ssm_groups
        self.HG = self.H // self.G
        self.N = d_state
        self.CC = self.DI + 2 * self.G * self.N
        self.F = -(-8 * d_model // (3 * 256)) * 256
        self.in_widths = (self.AW, self.KW, self.KW, self.PW, self.DI, self.CC, self.H, 3 * d_model)
        self.in_cols = sum(self.in_widths)


C = Cfg()
POOL_WINDOWS = (2, 4, 8, 16)


def _pick(n, cands):
    for c in cands:
        if n % c == 0:
            return c
    return n


def _params(*sem):
    return pltpu.CompilerParams(dimension_semantics=sem, vmem_limit_bytes=VMEM_LIMIT)


def _sigmoid(x):
    return 1.0 / (1.0 + jnp.exp(-x))


def _dot(a, b, dims):
    return lax.dot_general(a.astype(MXU), b.astype(MXU), (dims, ((), ())), preferred_element_type=F32)


def _dot_nn(a, b):
    return _dot(a, b, ((1,), (0,)))


def _dot_nt(a, b):
    return _dot(a, b, ((1,), (1,)))


def _dot_tn(a, b):
    return _dot(a, b, ((0,), (0,)))


def matmul(a, b, *, name, ta=False, tb=False, out_dtype=F32, add=None):
    M, K = (a.shape[1], a.shape[0]) if ta else a.shape
    N = b.shape[0] if tb else b.shape[1]
    assert K == (b.shape[1] if tb else b.shape[0]), (a.shape, b.shape)
    tm = _pick(M, (1024, 512, 256, 128))
    if ta:
        tn = _pick(N, (1024, 512, 256, 128))
        tk = _pick(K, (1024, 512, 256, 128))
    else:
        tn = _pick(N, (512, 256, 128))
        tk = K if K <= 2048 else _pick(K, (512, 256, 128))
    nk = K // tk
    a_spec = pl.BlockSpec((tk, tm), lambda i, j, k: (k, i)) if ta else pl.BlockSpec((tm, tk), lambda i, j, k: (i, k))
    b_spec = pl.BlockSpec((tn, tk), lambda i, j, k: (j, k)) if tb else pl.BlockSpec((tk, tn), lambda i, j, k: (k, j))
    o_spec = pl.BlockSpec((tm, tn), lambda i, j, k: (i, j))
    dims = ((0 if ta else 1,), (1 if tb else 0,))
    has_add = add is not None

    def body(*refs):
        a_ref, b_ref = refs[0], refs[1]
        add_ref = refs[2] if has_add else None
        o_ref = refs[3] if has_add else refs[2]
        part = _dot(a_ref[...], b_ref[...], dims)

        def finish(total):
            if has_add:
                total = total + add_ref[...].astype(F32)
            o_ref[...] = total.astype(o_ref.dtype)

        if nk == 1:
            finish(part)
        else:
            acc_ref = refs[-1]
            k = pl.program_id(2)

            @pl.when(k == 0)
            def _():
                acc_ref[...] = part

            @pl.when(k > 0)
            def _():
                acc_ref[...] += part

            @pl.when(k == nk - 1)
            def _():
                finish(acc_ref[...])

    in_specs = [a_spec, b_spec] + ([o_spec] if has_add else [])
    args = (a, b) + ((add,) if has_add else ())
    return pl.pallas_call(
        body, name=name, grid=(M // tm, N // tn, nk), in_specs=in_specs, out_specs=o_spec,
        out_shape=jax.ShapeDtypeStruct((M, N), out_dtype),
        scratch_shapes=[pltpu.VMEM((tm, tn), F32)] if nk > 1 else [],
        compiler_params=_params("parallel", "parallel", "arbitrary"),
    )(*args)


def _row_tile(t):
    return _pick(t, (256, 128, 64, 32, 16, 8))


def rms_fwd(x, w, *, name):
    T, D = x.shape
    tr = _row_tile(T)

    def body(x_ref, w_ref, o_ref):
        xv = x_ref[...]
        r = lax.rsqrt(jnp.mean(xv * xv, axis=-1, keepdims=True) + EPS)
        o_ref[...] = (xv * r * w_ref[...]).astype(o_ref.dtype)

    row = pl.BlockSpec((tr, D), lambda i: (i, 0))
    return pl.pallas_call(
        body, name=name, grid=(T // tr,), in_specs=[row, pl.BlockSpec((1, D), lambda i: (0, 0))], out_specs=row,
        out_shape=jax.ShapeDtypeStruct((T, D), MXU), compiler_params=_params("parallel"),
    )(x, w)


def rms_bwd(x, w, dh, dres, *, name):
    T, D = x.shape
    tr = _row_tile(T)

    def body(x_ref, w_ref, dh_ref, dres_ref, dx_ref, dxm_ref, dw_ref):
        xv = x_ref[...]
        dhv = dh_ref[...].astype(F32)
        r = lax.rsqrt(jnp.mean(xv * xv, axis=-1, keepdims=True) + EPS)
        g = dhv * w_ref[...]
        dot = jnp.mean(g * xv, axis=-1, keepdims=True)
        dx = dres_ref[...] + r * g - xv * (r * r * r * dot)
        dx_ref[...] = dx
        dxm_ref[...] = dx.astype(dxm_ref.dtype)
        part = jnp.sum(dhv * xv * r, axis=0, keepdims=True)

        @pl.when(pl.program_id(0) == 0)
        def _():
            dw_ref[...] = part

        @pl.when(pl.program_id(0) > 0)
        def _():
            dw_ref[...] += part

    row = pl.BlockSpec((tr, D), lambda i: (i, 0))
    vec = pl.BlockSpec((1, D), lambda i: (0, 0))
    return pl.pallas_call(
        body, name=name, grid=(T // tr,), in_specs=[row, vec, row, row], out_specs=[row, row, vec],
        out_shape=[jax.ShapeDtypeStruct((T, D), F32), jax.ShapeDtypeStruct((T, D), MXU),
                   jax.ShapeDtypeStruct((1, D), F32)],
        compiler_params=_params("arbitrary"),
    )(x, w, dh, dres)


def final_loss(x, w, target, *, name):
    T, D = x.shape
    tr = _row_tile(T)

    def body(x_ref, w_ref, t_ref, loss_ref, dx_ref, dxm_ref, dw_ref):
        xv = x_ref[...]
        wv = w_ref[...]
        r = lax.rsqrt(jnp.mean(xv * xv, axis=-1, keepdims=True) + EPS)
        err = xv * r * wv - t_ref[...]
        lpart = 0.5 * jnp.sum(jnp.mean(err * err, axis=-1, keepdims=True), axis=0, keepdims=True)
        dy = err * (1.0 / D)
        g = dy * wv
        dot = jnp.mean(g * xv, axis=-1, keepdims=True)
        dx = r * g - xv * (r * r * r * dot)
        dx_ref[...] = dx
        dxm_ref[...] = dx.astype(dxm_ref.dtype)
        part = jnp.sum(dy * xv * r, axis=0, keepdims=True)

        @pl.when(pl.program_id(0) == 0)
        def _():
            dw_ref[...] = part
            loss_ref[...] = lpart

        @pl.when(pl.program_id(0) > 0)
        def _():
            dw_ref[...] += part
            loss_ref[...] += lpart

    row = pl.BlockSpec((tr, D), lambda i: (i, 0))
    vec = pl.BlockSpec((1, D), lambda i: (0, 0))
    one = pl.BlockSpec((1, 1), lambda i: (0, 0))
    return pl.pallas_call(
        body, name=name, grid=(T // tr,), in_specs=[row, vec, row], out_specs=[one, row, row, vec],
        out_shape=[jax.ShapeDtypeStruct((1, 1), F32), jax.ShapeDtypeStruct((T, D), F32),
                   jax.ShapeDtypeStruct((T, D), MXU), jax.ShapeDtypeStruct((1, D), F32)],
        compiler_params=_params("arbitrary"),
    )(x, w, target)


def swiglu_fwd(gu, *, name):
    T, F2 = gu.shape
    F = F2 // 2
    tr = _row_tile(T)

    def body(g_ref, u_ref, o_ref):
        g = g_ref[...]
        o_ref[...] = (g * _sigmoid(g) * u_ref[...]).astype(o_ref.dtype)

    return pl.pallas_call(
        body, name=name, grid=(T // tr,),
        in_specs=[pl.BlockSpec((tr, F), lambda i: (i, 0)), pl.BlockSpec((tr, F), lambda i: (i, 1))],
        out_specs=pl.BlockSpec((tr, F), lambda i: (i, 0)),
        out_shape=jax.ShapeDtypeStruct((T, F), MXU), compiler_params=_params("parallel"),
    )(gu, gu)


def swiglu_bwd(gu, dact, *, name):
    T, F2 = gu.shape
    F = F2 // 2
    tr = _row_tile(T)

    def body(g_ref, u_ref, d_ref, o_ref):
        g = g_ref[...]
        d = d_ref[...]
        s = _sigmoid(g)
        o_ref[:, :F] = (d * u_ref[...] * s * (1.0 + g * (1.0 - s))).astype(o_ref.dtype)
        o_ref[:, F:] = (d * g * s).astype(o_ref.dtype)

    lo = pl.BlockSpec((tr, F), lambda i: (i, 0))
    hi = pl.BlockSpec((tr, F), lambda i: (i, 1))
    return pl.pallas_call(
        body, name=name, grid=(T // tr,), in_specs=[lo, hi, lo], out_specs=pl.BlockSpec((tr, F2), lambda i: (i, 0)),
        out_shape=jax.ShapeDtypeStruct((T, F2), MXU), compiler_params=_params("parallel"),
    )(gu, gu, dact)


def merge_fwd(gl, abr, pbr, sbr, *, name):
    T, D = abr.shape
    tr = _row_tile(T)

    def body(g0, g1, g2, a_ref, p_ref, s_ref, o_ref):
        m = _sigmoid(g0[...]) * a_ref[...] + _sigmoid(g1[...]) * p_ref[...] + _sigmoid(g2[...]) * s_ref[...]
        o_ref[...] = m.astype(o_ref.dtype)

    row = pl.BlockSpec((tr, D), lambda i: (i, 0))
    gs = [pl.BlockSpec((tr, D), lambda i, j=j: (i, j)) for j in range(3)]
    return pl.pallas_call(
        body, name=name, grid=(T // tr,), in_specs=gs + [row, row, row], out_specs=row,
        out_shape=jax.ShapeDtypeStruct((T, D), MXU), compiler_params=_params("parallel"),
    )(gl, gl, gl, abr, pbr, sbr)


def merge_bwd(gl, abr, pbr, sbr, dm, *, name):
    T, D = abr.shape
    tr = _row_tile(T)

    def body(g0, g1, g2, a_ref, p_ref, s_ref, dm_ref, da_ref, dp_ref, ds_ref, dg_ref):
        d = dm_ref[...]
        for j, (g_ref, b_ref, db_ref) in enumerate(((g0, a_ref, da_ref), (g1, p_ref, dp_ref), (g2, s_ref, ds_ref))):
            s = _sigmoid(g_ref[...])
            db_ref[...] = (d * s).astype(db_ref.dtype)
            dg_ref[:, j * D:(j + 1) * D] = (d * b_ref[...] * s * (1.0 - s)).astype(dg_ref.dtype)

    row = pl.BlockSpec((tr, D), lambda i: (i, 0))
    gs = [pl.BlockSpec((tr, D), lambda i, j=j: (i, j)) for j in range(3)]
    return pl.pallas_call(
        body, name=name, grid=(T // tr,), in_specs=gs + [row] * 4,
        out_specs=[row] * 3 + [pl.BlockSpec((tr, 3 * D), lambda i: (i, 0))],
        out_shape=[jax.ShapeDtypeStruct((T, D), MXU)] * 3 + [jax.ShapeDtypeStruct((T, 3 * D), MXU)],
        compiler_params=_params("parallel"),
    )(gl, gl, gl, abr, pbr, sbr, dm)


def gnorm_fwd(y, z, w, *, name):
    T, DI = y.shape
    gw = DI // C.G
    tr = _row_tile(T)

    def body(y_ref, z_ref, w_ref, o_ref):
        for g in range(C.G):
            sl = slice(g * gw, (g + 1) * gw)
            zz = z_ref[:, sl]
            v = y_ref[:, sl] * (zz * _sigmoid(zz))
            r = lax.rsqrt(jnp.mean(v * v, axis=-1, keepdims=True) + EPS)
            o_ref[:, sl] = (v * r * w_ref[:, sl]).astype(o_ref.dtype)

    row = pl.BlockSpec((tr, DI), lambda i: (i, 0))
    return pl.pallas_call(
        body, name=name, grid=(T // tr,), in_specs=[row, row, pl.BlockSpec((1, DI), lambda i: (0, 0))],
        out_specs=row, out_shape=jax.ShapeDtypeStruct((T, DI), MXU), compiler_params=_params("parallel"),
    )(y, z, w)


def gnorm_bwd(y, z, w, do, *, name):
    T, DI = y.shape
    gw = DI // C.G
    tr = _row_tile(T)

    def body(y_ref, z_ref, w_ref, do_ref, dy_ref, dz_ref, dw_ref):
        first = pl.program_id(0) == 0
        for g in range(C.G):
            sl = slice(g * gw, (g + 1) * gw)
            zz = z_ref[:, sl]
            yy = y_ref[:, sl]
            s = _sigmoid(zz)
            sz = zz * s
            v = yy * sz
            r = lax.rsqrt(jnp.mean(v * v, axis=-1, keepdims=True) + EPS)
            dov = do_ref[:, sl]
            gg = dov * w_ref[:, sl]
            dot = jnp.mean(gg * v, axis=-1, keepdims=True)
            dv = r * gg - v * (r * r * r * dot)
            dy_ref[:, sl] = dv * sz
            dz_ref[:, sl] = (dv * yy * s * (1.0 + zz * (1.0 - s))).astype(dz_ref.dtype)
            part = jnp.sum(dov * v * r, axis=0, keepdims=True)

            @pl.when(first)
            def _():
                dw_ref[:, sl] = part

            @pl.when(jnp.logical_not(first))
            def _():
                dw_ref[:, sl] += part

    row = pl.BlockSpec((tr, DI), lambda i: (i, 0))
    vec = pl.BlockSpec((1, DI), lambda i: (0, 0))
    return pl.pallas_call(
        body, name=name, grid=(T // tr,), in_specs=[row, row, vec, row], out_specs=[row, row, vec],
        out_shape=[jax.ShapeDtypeStruct((T, DI), F32), jax.ShapeDtypeStruct((T, DI), MXU),
                   jax.ShapeDtypeStruct((1, DI), F32)],
        compiler_params=_params("arbitrary"),
    )(y, z, w, do)


def adamw(w, g, m, v, *, name):
    shape = w.shape
    cols = shape[-1]
    rows = w.size // cols
    w2, g2, m2, v2 = (t.reshape(rows, cols) for t in (w, g, m, v))
    tr = rows if rows * cols * 4 <= (2 << 20) else _pick(rows, (512, 256, 128, 64, 32, 16, 8))
    while tr * cols * 4 > (2 << 20) and tr % 16 == 0:
        tr //= 2
    c1 = 1.0 - ADAM_B1 ** ADAM_STEP
    c2 = 1.0 - ADAM_B2 ** ADAM_STEP

    def body(w_ref, g_ref, m_ref, v_ref, d_ref, nm_ref, nv_ref):
        gv = g_ref[...]
        nm = ADAM_B1 * m_ref[...] + (1.0 - ADAM_B1) * gv
        nv = ADAM_B2 * v_ref[...] + (1.0 - ADAM_B2) * (gv * gv)
        d_ref[...] = -ADAM_LR * ((nm / c1) / (jnp.sqrt(nv / c2) + ADAM_EPS) + ADAM_WD * w_ref[...])
        nm_ref[...] = nm
        nv_ref[...] = nv

    row = pl.BlockSpec((tr, cols), lambda i: (i, 0))
    outs = pl.pallas_call(
        body, name=name, grid=(rows // tr,), in_specs=[row] * 4, out_specs=[row] * 3,
        out_shape=[jax.ShapeDtypeStruct((rows, cols), F32)] * 3, compiler_params=_params("parallel"),
    )(w2, g2, m2, v2)
    return tuple(o.reshape(shape) for o in outs)


def _attn_masks(i):
    B = C.blk
    row = lax.broadcasted_iota(jnp.int32, (B, 2 * B), 0)
    col = lax.broadcasted_iota(jnp.int32, (B, 2 * B), 1)
    diff = row + B - col
    return (diff >= 0) & (diff < B) & ((col >= B) | (i > 0))


def _kv2(prev_ref, cur_ref, m, lo):
    sl = slice(m * 128, (m + 1) * 128)
    slab = jnp.concatenate([prev_ref[:, sl], cur_ref[:, sl]], axis=0).astype(F32)
    rolled = pltpu.roll(slab, 64, axis=1)
    return jnp.where(lo, slab, rolled), jnp.where(lo, rolled, slab)


def _attn_probs(qh, k2, mask, sk):
    s = _dot_nt(qh, k2) * (C.hd ** -0.5)
    s = jnp.where(mask, s, NEG)
    mx = jnp.maximum(jnp.max(s, axis=-1, keepdims=True), sk)
    p = jnp.exp(s - mx)
    esk = jnp.exp(sk - mx)
    inv = 1.0 / (jnp.sum(p, axis=-1, keepdims=True) + esk)
    return p * inv, esk * inv


def attn_fwd(qkv, sink, *, name):
    T = qkv.shape[0]
    B, AW, KW = C.blk, C.AW, C.KW
    assert C.hd == 64 and KW % 128 == 0 and (C.qh // C.kvh) % 2 == 0
    nb = T // B
    kb = AW // KW
    ppk = C.qh // C.kvh // 2

    def body(q_ref, kc_ref, kp_ref, vc_ref, vp_ref, sink_ref, o_ref):
        i = pl.program_id(0)
        lo = lax.broadcasted_iota(jnp.int32, (1, 128), 1) < 64
        mask = _attn_masks(i)
        for m in range(KW // 128):
            k2s = _kv2(kp_ref, kc_ref, m, lo)
            v2s = _kv2(vp_ref, vc_ref, m, lo)
            for par in range(2):
                for pr in range(ppk):
                    pair = (2 * m + par) * ppk + pr
                    sl = slice(pair * 128, (pair + 1) * 128)
                    qp = q_ref[:, sl].astype(F32)
                    outs = []
                    for half in range(2):
                        h = 2 * pair + half
                        qh = jnp.where(lo if half == 0 else jnp.logical_not(lo), qp, 0.0)
                        p, _ = _attn_probs(qh, k2s[par], mask, sink_ref[:, h:h + 1])
                        outs.append(_dot_nn(p, v2s[par]))
                    o_ref[:, sl] = jnp.where(lo, outs[0], outs[1]).astype(o_ref.dtype)

    prev = lambda i: jnp.maximum(i - 1, 0)
    return pl.pallas_call(
        body, name=name, grid=(nb,),
        in_specs=[pl.BlockSpec((B, AW), lambda i: (i, 0)),
                  pl.BlockSpec((B, KW), lambda i: (i, kb)), pl.BlockSpec((B, KW), lambda i: (prev(i), kb)),
                  pl.BlockSpec((B, KW), lambda i: (i, kb + 1)), pl.BlockSpec((B, KW), lambda i: (prev(i), kb + 1)),
                  pl.BlockSpec((1, C.qh), lambda i: (0, 0))],
        out_specs=pl.BlockSpec((B, AW), lambda i: (i, 0)),
        out_shape=jax.ShapeDtypeStruct((T, AW), MXU), compiler_params=_params("parallel"),
    )(qkv, qkv, qkv, qkv, qkv, sink)


def attn_bwd(qkv, sink, dout, *, name):
    T = qkv.shape[0]
    B, AW, KW = C.blk, C.AW, C.KW
    nb = T // B
    kb = AW // KW
    ppk = C.qh // C.kvh // 2
    scale = C.hd ** -0.5

    def body(q_ref, kc_ref, kp_ref, vc_ref, vp_ref, sink_ref, do_ref, dq_ref, dk_ref, dv_ref, ds_ref, ck_ref, cv_ref):
        i = pl.program_id(0)

        @pl.when(i == 0)
        def _():
            ck_ref[...] = jnp.zeros_like(ck_ref)
            cv_ref[...] = jnp.zeros_like(cv_ref)
            ds_ref[...] = jnp.zeros_like(ds_ref)

        @pl.when(i < nb)
        def _():
            lane = lax.broadcasted_iota(jnp.int32, (1, 128), 1)
            lo = lane < 64
            hlane = lax.broadcasted_iota(jnp.int32, (1, C.qh), 1)
            mask = _attn_masks(i)
            dsink = jnp.zeros((1, C.qh), F32)
            for m in range(KW // 128):
                ksl = slice(m * 128, (m + 1) * 128)
                k2s = _kv2(kp_ref, kc_ref, m, lo)
                v2s = _kv2(vp_ref, vc_ref, m, lo)
                folded = []
                for par in range(2):
                    dk2 = jnp.zeros((2 * B, 128), F32)
                    dv2 = jnp.zeros((2 * B, 128), F32)
                    for pr in range(ppk):
                        pair = (2 * m + par) * ppk + pr
                        sl = slice(pair * 128, (pair + 1) * 128)
                        qp = q_ref[:, sl].astype(F32)
                        dop = do_ref[:, sl].astype(F32)
                        dqs = []
                        for half in range(2):
                            h = 2 * pair + half
                            hm = lo if half == 0 else jnp.logical_not(lo)
                            qh = jnp.where(hm, qp, 0.0)
                            doh = jnp.where(hm, dop, 0.0)
                            p, psink = _attn_probs(qh, k2s[par], mask, sink_ref[:, h:h + 1])
                            dp = _dot_nt(doh, v2s[par])
                            delta = jnp.sum(p * dp, axis=-1, keepdims=True)
                            dsc = p * (dp - delta)
                            dqs.append(_dot_nn(dsc, k2s[par]) * scale)
                            dk2 = dk2 + _dot_tn(dsc, qh) * scale
                            dv2 = dv2 + _dot_tn(p, doh)
                            dsh = -jnp.sum(psink * delta, axis=0, keepdims=True)
                            dsink = dsink + jnp.where(hlane == h, dsh, 0.0)
                        dq_ref[:, sl] = jnp.where(lo, dqs[0], dqs[1]).astype(dq_ref.dtype)
                    folded.append((dk2 + pltpu.roll(dk2, 64, axis=1), dv2 + pltpu.roll(dv2, 64, axis=1)))
                dks = jnp.where(lo, folded[0][0], folded[1][0])
                dvs = jnp.where(lo, folded[0][1], folded[1][1])
                dk_ref[:, ksl] = (ck_ref[:, ksl] + dks[:B]).astype(dk_ref.dtype)
                dv_ref[:, ksl] = (cv_ref[:, ksl] + dvs[:B]).astype(dv_ref.dtype)
                ck_ref[:, ksl] = dks[B:]
                cv_ref[:, ksl] = dvs[B:]
            ds_ref[...] += dsink

        @pl.when(i == nb)
        def _():
            dk_ref[...] = ck_ref[...].astype(dk_ref.dtype)
            dv_ref[...] = cv_ref[...].astype(dv_ref.dtype)

    cur = lambda i: jnp.minimum(i, nb - 1)
    prev = lambda i: jnp.maximum(jnp.minimum(i, nb - 1) - 1, 0)
    out = lambda i: jnp.maximum(i - 1, 0)
    return pl.pallas_call(
        body, name=name, grid=(nb + 1,),
        in_specs=[pl.BlockSpec((B, AW), lambda i: (cur(i), 0)),
                  pl.BlockSpec((B, KW), lambda i: (cur(i), kb)), pl.BlockSpec((B, KW), lambda i: (prev(i), kb)),
                  pl.BlockSpec((B, KW), lambda i: (cur(i), kb + 1)), pl.BlockSpec((B, KW), lambda i: (prev(i), kb + 1)),
                  pl.BlockSpec((1, C.qh), lambda i: (0, 0)),
                  pl.BlockSpec((B, AW), lambda i: (cur(i), 0))],
        out_specs=[pl.BlockSpec((B, AW), lambda i: (cur(i), 0)),
                   pl.BlockSpec((B, KW), lambda i: (out(i), 0)), pl.BlockSpec((B, KW), lambda i: (out(i), 0)),
                   pl.BlockSpec((1, C.qh), lambda i: (0, 0))],
        out_shape=[jax.ShapeDtypeStruct((T, AW), MXU), jax.ShapeDtypeStruct((T, KW), MXU),
                   jax.ShapeDtypeStruct((T, KW), MXU), jax.ShapeDtypeStruct((1, C.qh), F32)],
        scratch_shapes=[pltpu.VMEM((B, KW), F32), pltpu.VMEM((B, KW), F32)],
        compiler_params=_params("arbitrary"),
    )(qkv, qkv, qkv, qkv, qkv, sink, dout)


POOL_HALO = 16


def _window_sum(e, w, n, forward):
    s, k = e, 1
    while k < w:
        s = s + pltpu.roll(s, (n - k) if forward else k, axis=0)
        k *= 2
    return s


def pool_fwd(u, pw, ps, *, name):
    T, PW = u.shape
    PG = C.PG
    tT = _pick(T, (256, 128))
    hb = tT // POOL_HALO

    def body(u_ref, uh_ref, pw_ref, ps_ref, o_ref, mx_ref):
        i = pl.program_id(0)
        halo = jnp.where(i > 0, uh_ref[...], 0.0)
        ext = jnp.concatenate([halo, u_ref[...]], axis=0)
        t = i * tT + lax.broadcasted_iota(jnp.int32, (tT, 1), 0)
        for g, w in enumerate(POOL_WINDOWS):
            sl = slice(g * PG, (g + 1) * PG)
            s = _window_sum(ext[:, sl], w, tT + POOL_HALO, False)[POOL_HALO:]
            cnt = jnp.minimum(t + 1, w).astype(F32)
            mixed = (s / cnt - u_ref[:, sl]).astype(MXU)
            mx_ref[:, sl] = mixed.astype(mx_ref.dtype)
            o_ref[:, sl] = (_dot_nn(mixed, pw_ref[g]) * ps_ref[:, sl]).astype(o_ref.dtype)

    row = pl.BlockSpec((tT, PW), lambda i: (i, 0))
    return pl.pallas_call(
        body, name=name, grid=(T // tT,),
        in_specs=[row, pl.BlockSpec((POOL_HALO, PW), lambda i: (jnp.maximum(i * hb - 1, 0), 0)),
                  pl.BlockSpec((4, PG, PG), lambda i: (0, 0, 0)), pl.BlockSpec((1, PW), lambda i: (0, 0))],
        out_specs=[row, row], out_shape=[jax.ShapeDtypeStruct((T, PW), MXU)] * 2,
        compiler_params=_params("parallel"),
    )(u, u, pw, ps)


def pool_bwd(dpool, mixed, pw, ps, *, name):
    T, PW = dpool.shape
    PG = C.PG
    tT = _pick(T, (256, 128))
    hb = tT // POOL_HALO
    n = T // tT
    rows = tT + POOL_HALO

    def body(dp_ref, dph_ref, mx_ref, pw_ref, ps_ref, du_ref, dpw_ref, dps_ref):
        i = pl.program_id(0)

        @pl.when(i == 0)
        def _():
            dpw_ref[...] = jnp.zeros_like(dpw_ref)
            dps_ref[...] = jnp.zeros_like(dps_ref)

        halo = jnp.where(i < n - 1, dph_ref[...], 0.0)
        dext = jnp.concatenate([dp_ref[...], halo], axis=0)
        t = i * tT + lax.broadcasted_iota(jnp.int32, (rows, 1), 0)
        for g, w in enumerate(POOL_WINDOWS):
            sl = slice(g * PG, (g + 1) * PG)
            dyg = dext[:, sl] * ps_ref[:, sl]
            dmix = _dot_nt(dyg, pw_ref[g])
            cnt = jnp.minimum(t + 1, w).astype(F32)
            s = _window_sum(dmix / cnt, w, rows, True)
            du_ref[:, sl] = (s[:tT] - dmix[:tT]).astype(du_ref.dtype)
            mb = mx_ref[:, sl]
            dpw_ref[g] += _dot_tn(mb, dyg[:tT])
            dps_ref[:, sl] += jnp.sum(dp_ref[:, sl] * _dot_nn(mb, pw_ref[g]), axis=0, keepdims=True)

    row = pl.BlockSpec((tT, PW), lambda i: (i, 0))
    return pl.pallas_call(
        body, name=name, grid=(n,),
        in_specs=[row, pl.BlockSpec((POOL_HALO, PW), lambda i: (jnp.minimum((i + 1) * hb, T // POOL_HALO - 1), 0)), row,
                  pl.BlockSpec((4, PG, PG), lambda i: (0, 0, 0)), pl.BlockSpec((1, PW), lambda i: (0, 0))],
        out_specs=[row, pl.BlockSpec((4, PG, PG), lambda i: (0, 0, 0)), pl.BlockSpec((1, PW), lambda i: (0, 0))],
        out_shape=[jax.ShapeDtypeStruct((T, PW), MXU), jax.ShapeDtypeStruct((4, PG, PG), F32),
                   jax.ShapeDtypeStruct((1, PW), F32)],
        compiler_params=_params("arbitrary"),
    )(dpool, dpool, mixed, pw, ps)


CONV_HALO = 8
CONV_K = 4


def _conv(x, cw_ref, cb_ref, sl):
    acc = cb_ref[:, sl] + cw_ref[CONV_K - 1:CONV_K, sl] * x
    for k in range(CONV_K - 1):
        acc = acc + cw_ref[k:k + 1, sl] * pltpu.roll(x, CONV_K - 1 - k, axis=0)
    return acc


def conv_fwd(xbc, cw, cb, *, name):
    T, CC = xbc.shape
    tT = _pick(T, (256, 128))
    hb = tT // CONV_HALO
    cs = _pick(CC, (512, 256, 128))

    def body(x_ref, xh_ref, cw_ref, cb_ref, o_ref):
        i = pl.program_id(0)
        for c0 in range(0, CC, cs):
            sl = slice(c0, c0 + cs)
            ext = jnp.concatenate([jnp.where(i > 0, xh_ref[:, sl], 0.0), x_ref[:, sl]], axis=0)
            xc = _conv(ext, cw_ref, cb_ref, sl)[CONV_HALO:]
            o_ref[:, sl] = xc * _sigmoid(xc)

    row = pl.BlockSpec((tT, CC), lambda i: (i, 0))
    return pl.pallas_call(
        body, name=name, grid=(T // tT,),
        in_specs=[row, pl.BlockSpec((CONV_HALO, CC), lambda i: (jnp.maximum(i * hb - 1, 0), 0)),
                  pl.BlockSpec((CONV_K, CC), lambda i: (0, 0)), pl.BlockSpec((1, CC), lambda i: (0, 0))],
        out_specs=row, out_shape=jax.ShapeDtypeStruct((T, CC), F32), compiler_params=_params("parallel"),
    )(xbc, xbc, cw, cb)


def conv_bwd(xbc, cw, cb, dxa, *, name):
    T, CC = xbc.shape
    tT = _pick(T, (256, 128))
    hb = tT // CONV_HALO
    n = T // tT
    rows = tT + 2 * CONV_HALO
    cs = _pick(CC, (512, 256, 128))

    def body(x_ref, xp_ref, xn_ref, cw_ref, cb_ref, d_ref, dn_ref, dx_ref, dcw_ref, dcb_ref):
        i = pl.program_id(0)

        @pl.when(i == 0)
        def _():
            dcw_ref[...] = jnp.zeros_like(dcw_ref)
            dcb_ref[...] = jnp.zeros_like(dcb_ref)

        r = lax.broadcasted_iota(jnp.int32, (rows, 1), 0)
        own = (r >= CONV_HALO) & (r < tT + CONV_HALO)
        for c0 in range(0, CC, cs):
            sl = slice(c0, c0 + cs)
            x = jnp.concatenate([jnp.where(i > 0, xp_ref[:, sl], 0.0), x_ref[:, sl],
                                 jnp.where(i < n - 1, xn_ref[:, sl], 0.0)], axis=0)
            da = jnp.concatenate([jnp.zeros((CONV_HALO, cs), F32), d_ref[:, sl],
                                  jnp.where(i < n - 1, dn_ref[:, sl], 0.0)], axis=0)
            xc = _conv(x, cw_ref, cb_ref, sl)
            sg = _sigmoid(xc)
            dxc = da * sg * (1.0 + xc * (1.0 - sg))
            acc = cw_ref[CONV_K - 1:CONV_K, sl] * dxc
            for k in range(CONV_K - 1):
                acc = acc + cw_ref[k:k + 1, sl] * pltpu.roll(dxc, rows - (CONV_K - 1 - k), axis=0)
            dx_ref[:, sl] = acc[CONV_HALO:tT + CONV_HALO].astype(dx_ref.dtype)
            down = jnp.where(own, dxc, 0.0)
            dcb_ref[:, sl] += jnp.sum(down, axis=0, keepdims=True)
            dcw_ref[CONV_K - 1:CONV_K, sl] += jnp.sum(down * x, axis=0, keepdims=True)
            for k in range(CONV_K - 1):
                dcw_ref[k:k + 1, sl] += jnp.sum(down * pltpu.roll(x, CONV_K - 1 - k, axis=0), axis=0, keepdims=True)

    row = pl.BlockSpec((tT, CC), lambda i: (i, 0))
    prev = pl.BlockSpec((CONV_HALO, CC), lambda i: (jnp.maximum(i * hb - 1, 0), 0))
    nxt = pl.BlockSpec((CONV_HALO, CC), lambda i: (jnp.minimum((i + 1) * hb, T // CONV_HALO - 1), 0))
    return pl.pallas_call(
        body, name=name, grid=(n,),
        in_specs=[row, prev, nxt, pl.BlockSpec((CONV_K, CC), lambda i: (0, 0)), pl.BlockSpec((1, CC), lambda i: (0, 0)),
                  row, nxt],
        out_specs=[row, pl.BlockSpec((CONV_K, CC), lambda i: (0, 0)), pl.BlockSpec((1, CC), lambda i: (0, 0))],
        out_shape=[jax.ShapeDtypeStruct((T, CC), MXU), jax.ShapeDtypeStruct((CONV_K, CC), F32),
                   jax.ShapeDtypeStruct((1, CC), F32)],
        compiler_params=_params("arbitrary"),
    )(xbc, xbc, xbc, cw, cb, dxa, dxa)


def _softplus(x):
    return jnp.maximum(x, 0.0) + jnp.log(1.0 + jnp.exp(-jnp.abs(x)))


def _dot_exact(a, b):
    return lax.dot_general(a, b, (((1,), (0,)), ((), ())), precision=lax.Precision.HIGHEST, preferred_element_type=F32)


def ssd_prep(raw_t, bias, alog, *, name):
    H, T = raw_t.shape
    B = C.blk
    tc = _pick(T, (4 * B, 2 * B, B))

    def body(r_ref, b_ref, al_ref, dt_ref, acs_ref):
        dt = _softplus(r_ref[...] + b_ref[...])
        dt_ref[...] = dt
        dta = dt * (-jnp.exp(al_ref[...]))
        upper = (lax.broadcasted_iota(jnp.int32, (B, B), 0) <= lax.broadcasted_iota(jnp.int32, (B, B), 1)).astype(F32)
        for j in range(tc // B):
            acs_ref[:, j * B:(j + 1) * B] = _dot_exact(dta[:, j * B:(j + 1) * B], upper)

    blk = pl.BlockSpec((H, tc), lambda i: (0, i))
    vec = pl.BlockSpec((H, 1), lambda i: (0, 0))
    return pl.pallas_call(
        body, name=name, grid=(T // tc,), in_specs=[blk, vec, vec], out_specs=[blk, blk],
        out_shape=[jax.ShapeDtypeStruct((H, T), F32)] * 2, compiler_params=_params("parallel"),
    )(raw_t, bias, alog)


def _pair(lo, arr, h0, rows=slice(None)):
    return jnp.where(lo, arr[rows, h0:h0 + 1], arr[rows, h0 + 1:h0 + 2])


def _decay(acs, acs_t, h, causal):
    return jnp.exp(jnp.where(causal, acs[:, h:h + 1] - acs_t[h:h + 1, :], NEG))


def ssd_fwd(xa, dt, acs, acs_t, dskip, *, name):
    T = xa.shape[0]
    B, DI, G, N, HG, P, H = C.blk, C.DI, C.G, C.N, C.HG, C.P, C.H
    assert P == 64 and HG % 2 == 0
    nc = T // B
    W = HG * P

    def body(xa_ref, dt_ref, acs_ref, acst_ref, ds_ref, y_ref, hp_ref, h_scr):
        @pl.when(pl.program_id(0) == 0)
        def _():
            h_scr[...] = jnp.zeros_like(h_scr)

        lo = lax.broadcasted_iota(jnp.int32, (1, 128), 1) < 64
        causal = lax.broadcasted_iota(jnp.int32, (B, B), 0) >= lax.broadcasted_iota(jnp.int32, (B, B), 1)
        dt, acs, acs_t, dsk = dt_ref[...], acs_ref[...], acst_ref[...], ds_ref[...]
        for g in range(G):
            bg = xa_ref[:, DI + g * N:DI + (g + 1) * N]
            cg = xa_ref[:, DI + (G + g) * N:DI + (G + g + 1) * N]
            cb = _dot_nt(cg, bg)
            hg = h_scr[g]
            hp_ref[0, g * N:(g + 1) * N, :] = hg
            yoff = _dot_nn(cg, hg)
            xws, decs = [], []
            for j in range(HG // 2):
                h0 = g * HG + 2 * j
                xsl = slice(h0 * P, (h0 + 2) * P)
                xp = xa_ref[:, xsl]
                ap = _pair(lo, acs, h0)
                alast = _pair(lo, acs, h0, slice(B - 1, B))
                xdt = xp * _pair(lo, dt, h0)
                ys = [_dot_nn(cb * _decay(acs, acs_t, h0 + half, causal), xdt) for half in range(2)]
                y_ref[:, xsl] = (jnp.where(lo, ys[0], ys[1]) + yoff[:, 2 * j * P:(2 * j + 2) * P] * jnp.exp(ap)
                                 + _pair(lo, dsk, h0) * xp)
                xws.append(xdt * jnp.exp(alast - ap))
                decs.append(jnp.exp(alast))
            h_scr[g] = hg * jnp.concatenate(decs, axis=1) + _dot_tn(bg, jnp.concatenate(xws, axis=1))

    tok = lambda w: pl.BlockSpec((B, w), lambda c: (c, 0))
    return pl.pallas_call(
        body, name=name, grid=(nc,),
        in_specs=[tok(C.CC), tok(H), tok(H), pl.BlockSpec((H, B), lambda c: (0, c)), pl.BlockSpec((1, H), lambda c: (0, 0))],
        out_specs=[tok(DI), pl.BlockSpec((1, G * N, W), lambda c: (c, 0, 0))],
        out_shape=[jax.ShapeDtypeStruct((T, DI), F32), jax.ShapeDtypeStruct((nc, G * N, W), F32)],
        scratch_shapes=[pltpu.VMEM((G, N, W), F32)],
        compiler_params=_params("arbitrary"),
    )(xa, dt, acs, acs_t, dskip)


def ssd_bwd(xa, dt, acs, acs_t, dskip, hprev, dy, *, name):
    T = xa.shape[0]
    B, DI, G, N, HG, P, H = C.blk, C.DI, C.G, C.N, C.HG, C.P, C.H
    nc = T // B
    W = HG * P

    def body(xa_ref, dt_ref, acs_ref, acst_ref, ds_ref, hp_ref, dy_ref, dxa_ref, ddt_ref, dacs_ref, dd_ref, dh_scr):
        @pl.when(pl.program_id(0) == 0)
        def _():
            dh_scr[...] = jnp.zeros_like(dh_scr)
            dd_ref[...] = jnp.zeros_like(dd_ref)

        lo = lax.broadcasted_iota(jnp.int32, (1, 128), 1) < 64
        hi = jnp.logical_not(lo)
        causal = lax.broadcasted_iota(jnp.int32, (B, B), 0) >= lax.broadcasted_iota(jnp.int32, (B, B), 1)
        hlane = lax.broadcasted_iota(jnp.int32, (1, H), 1)
        hsub = lax.broadcasted_iota(jnp.int32, (B, 1), 0)
        lastrow = lax.broadcasted_iota(jnp.int32, (B, 1), 0) == B - 1
        dt, acs, acs_t, dsk = dt_ref[...], acs_ref[...], acst_ref[...], ds_ref[...]
        d_acs = jnp.zeros((B, H), F32)
        d_acs_t = jnp.zeros((B, B), F32)
        d_dt = jnp.zeros((B, H), F32)
        d_d = jnp.zeros((1, H), F32)

        def rsum(v):
            return (jnp.sum(jnp.where(lo, v, 0.0), axis=1, keepdims=True),
                    jnp.sum(jnp.where(hi, v, 0.0), axis=1, keepdims=True))

        for g in range(G):
            bsl = slice(DI + g * N, DI + (g + 1) * N)
            csl = slice(DI + (G + g) * N, DI + (G + g + 1) * N)
            bg, cg = xa_ref[:, bsl], xa_ref[:, csl]
            cb = _dot_nt(cg, bg)
            hg = hp_ref[0, g * N:(g + 1) * N, :]
            dhg = dh_scr[g]
            yoff = _dot_nn(cg, hg)
            bds = _dot_nn(bg, dhg)
            d_cb = jnp.zeros((B, B), F32)
            dyes, xws, decs = [], [], []
            for j in range(HG // 2):
                h0 = g * HG + 2 * j
                xsl = slice(h0 * P, (h0 + 2) * P)
                psl = slice(2 * j * P, (2 * j + 2) * P)
                xp, dyp = xa_ref[:, xsl], dy_ref[:, xsl]
                dtp = _pair(lo, dt, h0)
                ap = _pair(lo, acs, h0)
                alast = _pair(lo, acs, h0, slice(B - 1, B))
                ea, ew, el = jnp.exp(ap), jnp.exp(alast - ap), jnp.exp(alast)
                xdt = xp * dtp
                halves = []
                for half in range(2):
                    h = h0 + half
                    lm = _decay(acs, acs_t, h, causal)
                    m = cb * lm
                    d_m = _dot_nt(jnp.where(lo if half == 0 else hi, dyp, 0.0), xdt)
                    d_cb = d_cb + d_m * lm
                    wgt = d_m * m
                    d_acs = d_acs + jnp.where(hlane == h, jnp.sum(wgt, axis=1, keepdims=True), 0.0)
                    d_acs_t = d_acs_t + jnp.where(hsub == h, jnp.sum(wgt, axis=0, keepdims=True), 0.0)
                    halves.append(_dot_tn(m, dyp))
                bdp = bds[:, psl]
                dxdt = jnp.where(lo, halves[0], halves[1]) + ew * bdp
                dxa_ref[:, xsl] = dtp * dxdt + _pair(lo, dsk, h0) * dyp
                xw = xdt * ew
                terms_dt = rsum(dxdt * xp)
                terms_dd = rsum(dyp * xp)
                terms_off = rsum(dyp * (ea * yoff[:, psl]))
                terms_e = rsum(xw * bdp)
                terms_h = rsum(hg[:, psl] * dhg[:, psl])
                for half in range(2):
                    h = h0 + half
                    sel = hlane == h
                    d_dt = d_dt + jnp.where(sel, terms_dt[half], 0.0)
                    d_d = d_d + jnp.where(sel, jnp.sum(terms_dd[half], axis=0, keepdims=True), 0.0)
                    e_last = jnp.sum(jnp.where(lo if half == 0 else hi, el, 0.0), axis=1, keepdims=True) * (1.0 / P)
                    d_last = (jnp.sum(terms_e[half], axis=0, keepdims=True)
                              + e_last * jnp.sum(terms_h[half], axis=0, keepdims=True))
                    d_acs = d_acs + jnp.where(sel, terms_off[half] - terms_e[half] + jnp.where(lastrow, d_last, 0.0), 0.0)
                dyes.append(dyp * ea)
                xws.append(xw)
                decs.append(el)
            dye = jnp.concatenate(dyes, axis=1)
            xwc = jnp.concatenate(xws, axis=1)
            dxa_ref[:, csl] = _dot_nn(d_cb, bg) + _dot_nt(dye, hg)
            dxa_ref[:, bsl] = _dot_tn(d_cb, cg) + _dot_nt(xwc, dhg)
            dh_scr[g] = dhg * jnp.concatenate(decs, axis=1) + _dot_tn(cg, dye)
        ddt_ref[...] = d_dt
        dacs_ref[...] = d_acs - d_acs_t.T[:, :H]
        dd_ref[...] += d_d

    rev = lambda w: pl.BlockSpec((B, w), lambda c: (nc - 1 - c, 0))
    vec = pl.BlockSpec((1, H), lambda c: (0, 0))
    return pl.pallas_call(
        body, name=name, grid=(nc,),
        in_specs=[rev(C.CC), rev(H), rev(H), pl.BlockSpec((H, B), lambda c: (0, nc - 1 - c)), vec,
                  pl.BlockSpec((1, G * N, W), lambda c: (nc - 1 - c, 0, 0)), rev(DI)],
        out_specs=[rev(C.CC), rev(H), rev(H), vec],
        out_shape=[jax.ShapeDtypeStruct((T, C.CC), F32), jax.ShapeDtypeStruct((T, H), F32),
                   jax.ShapeDtypeStruct((T, H), F32), jax.ShapeDtypeStruct((1, H), F32)],
        scratch_shapes=[pltpu.VMEM((G, N, W), F32)],
        compiler_params=_params("arbitrary"),
    )(xa, dt, acs, acs_t, dskip, hprev, dy)


def ssd_post(ddt, dacs, dt, raw, bias, alog, *, name):
    T, H = ddt.shape
    B = C.blk
    tc = _pick(T, (4 * B, 2 * B, B))

    def body(ddt_ref, dacs_ref, dt_ref, raw_ref, b_ref, al_ref, draw_ref, db_ref, dal_ref):
        @pl.when(pl.program_id(0) == 0)
        def _():
            db_ref[...] = jnp.zeros_like(db_ref)
            dal_ref[...] = jnp.zeros_like(dal_ref)

        a = -jnp.exp(al_ref[...])
        lower = (lax.broadcasted_iota(jnp.int32, (B, B), 0) <= lax.broadcasted_iota(jnp.int32, (B, B), 1)).astype(F32)
        for j in range(tc // B):
            sl = slice(j * B, (j + 1) * B)
            rc = _dot_exact(lower, dacs_ref[sl, :])
            dtv = dt_ref[sl, :]
            draw = (ddt_ref[sl, :] + a * rc) * _sigmoid(raw_ref[sl, :] + b_ref[...])
            draw_ref[sl, :] = draw
            db_ref[...] += jnp.sum(draw, axis=0, keepdims=True)
            dal_ref[...] += jnp.sum(dtv * rc, axis=0, keepdims=True) * a

    blk = pl.BlockSpec((tc, H), lambda i: (i, 0))
    vec = pl.BlockSpec((1, H), lambda i: (0, 0))
    return pl.pallas_call(
        body, name=name, grid=(T // tc,), in_specs=[blk, blk, blk, blk, vec, vec], out_specs=[blk, vec, vec],
        out_shape=[jax.ShapeDtypeStruct((T, H), F32), jax.ShapeDtypeStruct((1, H), F32), jax.ShapeDtypeStruct((1, H), F32)],
        compiler_params=_params("arbitrary"),
    )(ddt, dacs, dt, raw, bias, alog)


MESH = pl.DeviceIdType.MESH
PACK_W = 1024
PACK_ROWS = 1024
ANY = pl.BlockSpec(memory_space=pl.ANY)


def _place():
    return lax.axis_index("x"), lax.axis_index("y"), lax.axis_index("c")


def _other_chips(x, y):
    return [(1 - x, y), (x, 1 - y), (1 - x, 1 - y)]


def gather_shards(p, *, name):
    R, W = p.shape
    Rh = R // 2

    def body(p_ref, g_ref, send_sems, recv_sems, local_sem):
        x, y, c = _place()
        s = 2 * x + y
        chips = _other_chips(x, y)
        mine = pl.ds(pl.multiple_of(c * Rh, 32), Rh)
        theirs = pl.ds(pl.multiple_of((1 - c) * Rh, 32), Rh)
        own = pltpu.make_async_copy(p_ref, g_ref.at[s], local_sem)
        own.start()

        def copy(k, src, dst, to):
            return pltpu.make_async_remote_copy(src_ref=src, dst_ref=dst, send_sem=send_sems.at[k],
                                                recv_sem=recv_sems.at[k], device_id=to, device_id_type=MESH)

        out = [copy(j, p_ref.at[mine], g_ref.at[s, mine], (px, py, c)) for j, (px, py) in enumerate(chips)]
        for cp in out:
            cp.start()
        passed = []
        for j, (px, py) in enumerate(chips):
            sj = 2 * px + py
            copy(j, p_ref.at[mine], g_ref.at[sj, mine], (px, py, c)).wait_recv()
            fw = copy(3 + j, g_ref.at[sj, mine], g_ref.at[sj, mine], (x, y, 1 - c))
            fw.start()
            passed.append(fw)
        for j, (px, py) in enumerate(chips):
            sj = 2 * px + py
            copy(3 + j, g_ref.at[sj, theirs], g_ref.at[sj, theirs], (x, y, 1 - c)).wait_recv()
        for cp in out + passed:
            cp.wait_send()
        own.wait()

    return pl.pallas_call(
        body, name=name, in_specs=[ANY], out_specs=ANY, out_shape=jax.ShapeDtypeStruct((4, R, W), p.dtype),
        scratch_shapes=[pltpu.SemaphoreType.DMA((6,)), pltpu.SemaphoreType.DMA((6,)), pltpu.SemaphoreType.DMA],
    )(p)


def swap_halves(g, *, name):
    _, R, W = g.shape
    Rh = R // 2

    def body(g_ref, r_ref, send_sem, recv_sem):
        x, y, c = _place()
        theirs = pl.ds(pl.multiple_of((1 - c) * Rh, 32), Rh)
        cp = pltpu.make_async_remote_copy(src_ref=g_ref.at[:, theirs], dst_ref=r_ref, send_sem=send_sem, recv_sem=recv_sem,
                                          device_id=(x, y, 1 - c), device_id_type=MESH)
        cp.start()
        cp.wait()

    return pl.pallas_call(
        body, name=name, in_specs=[ANY], out_specs=ANY, out_shape=jax.ShapeDtypeStruct((4, Rh, W), g.dtype),
        scratch_shapes=[pltpu.SemaphoreType.DMA, pltpu.SemaphoreType.DMA],
    )(g)


def scatter_chips(a, *, name):
    _, Rh, W = a.shape

    def body(a_ref, r_ref, send_sems, recv_sems):
        x, y, c = _place()
        out = []
        for j, (px, py) in enumerate(_other_chips(x, y)):
            cp = pltpu.make_async_remote_copy(src_ref=a_ref.at[2 * px + py], dst_ref=r_ref.at[j], send_sem=send_sems.at[j],
                                              recv_sem=recv_sems.at[j], device_id=(px, py, c), device_id_type=MESH)
            cp.start()
            out.append(cp)
        for cp in out:
            cp.wait()

    return pl.pallas_call(
        body, name=name, in_specs=[ANY], out_specs=ANY, out_shape=jax.ShapeDtypeStruct((3, Rh, W), a.dtype),
        scratch_shapes=[pltpu.SemaphoreType.DMA((3,)), pltpu.SemaphoreType.DMA((3,))],
    )(a)


def join_halves(r, *, name):
    Rh, W = r.shape

    def body(r_ref, o_ref, send_sem, recv_sem, local_sem):
        x, y, c = _place()
        mine = pl.ds(pl.multiple_of(c * Rh, 32), Rh)
        own = pltpu.make_async_copy(r_ref, o_ref.at[mine], local_sem)
        own.start()
        cp = pltpu.make_async_remote_copy(src_ref=r_ref, dst_ref=o_ref.at[mine], send_sem=send_sem, recv_sem=recv_sem,
                                          device_id=(x, y, 1 - c), device_id_type=MESH)
        cp.start()
        theirs = pl.ds(pl.multiple_of((1 - c) * Rh, 32), Rh)
        pltpu.make_async_remote_copy(src_ref=r_ref, dst_ref=o_ref.at[theirs], send_sem=send_sem, recv_sem=recv_sem,
                                     device_id=(x, y, 1 - c), device_id_type=MESH).wait_recv()
        cp.wait_send()
        own.wait()

    return pl.pallas_call(
        body, name=name, in_specs=[ANY], out_specs=ANY, out_shape=jax.ShapeDtypeStruct((2 * Rh, W), r.dtype),
        scratch_shapes=[pltpu.SemaphoreType.DMA, pltpu.SemaphoreType.DMA, pltpu.SemaphoreType.DMA],
    )(r)


def gather_all(v, *, name, total):
    rows, W = v.shape

    def body(v_ref, o_ref, *scr):
        buf = scr[0] if total else o_ref
        send_sems, recv_sems = scr[-2], scr[-1]
        x, y, c = _place()
        me = 4 * x + 2 * y + c
        flips = [(k >> 2 & 1, k >> 1 & 1, k & 1) for k in range(1, 8)]
        peers = [((1 - x) if fx else x, (1 - y) if fy else y, (1 - c) if fc else c) for fx, fy, fc in flips]
        out = []
        for k, peer in enumerate(peers):
            cp = pltpu.make_async_remote_copy(src_ref=v_ref, dst_ref=buf.at[me], send_sem=send_sems.at[k],
                                              recv_sem=recv_sems.at[k], device_id=peer, device_id_type=MESH)
            cp.start()
            out.append(cp)
        buf[me] = v_ref[...]
        for k, (px, py, pc) in enumerate(peers):
            pltpu.make_async_remote_copy(src_ref=v_ref, dst_ref=buf.at[4 * px + 2 * py + pc], send_sem=send_sems.at[k],
                                         recv_sem=recv_sems.at[k], device_id=(px, py, pc), device_id_type=MESH).wait_recv()
        for cp in out:
            cp.wait_send()
        if total:
            acc = buf[0]
            for d in range(1, 8):
                acc = acc + buf[d]
            o_ref[...] = acc

    vm = pl.BlockSpec(memory_space=pltpu.VMEM)
    return pl.pallas_call(
        body, name=name, in_specs=[vm], out_specs=vm,
        out_shape=jax.ShapeDtypeStruct((rows, W) if total else (8, rows, W), F32),
        scratch_shapes=([pltpu.VMEM((8, rows, W), F32)] if total else [])
        + [pltpu.SemaphoreType.DMA((7,)), pltpu.SemaphoreType.DMA((7,))],
    )(v)


def pair_sum(g, r, c, *, name):
    _, R, W = g.shape
    Rh = R // 2
    tr = _pick(Rh, (512, 256, 128, 64, 32))
    nb = Rh // tr

    def body(c_ref, g_ref, r_ref, o_ref):
        o_ref[...] = (g_ref[...].astype(F32) + r_ref[...].astype(F32)).astype(o_ref.dtype)

    grid_spec = pltpu.PrefetchScalarGridSpec(
        num_scalar_prefetch=1, grid=(4, nb),
        in_specs=[pl.BlockSpec((1, tr, W), lambda s, i, c_ref: (s, c_ref[0] * nb + i, 0)),
                  pl.BlockSpec((1, tr, W), lambda s, i, c_ref: (s, i, 0))],
        out_specs=pl.BlockSpec((1, tr, W), lambda s, i, c_ref: (s, i, 0)))
    return pl.pallas_call(
        body, name=name, grid_spec=grid_spec, out_shape=jax.ShapeDtypeStruct((4, Rh, W), g.dtype),
        compiler_params=_params("parallel", "parallel"),
    )(c, g, r)


def chip_sum(g, r1, r2, sc, *, name):
    _, R, W = g.shape
    Rh = R // 2
    tr = _pick(Rh, (512, 256, 128, 64, 32))
    nb = Rh // tr

    def body(sc_ref, g_ref, r1_ref, a_ref, b_ref, c_ref, o_ref):
        acc = g_ref[0].astype(F32) + r1_ref[0].astype(F32)
        for ref in (a_ref, b_ref, c_ref):
            acc = acc + ref[0].astype(F32)
        o_ref[...] = acc

    grid_spec = pltpu.PrefetchScalarGridSpec(
        num_scalar_prefetch=1, grid=(nb,),
        in_specs=[pl.BlockSpec((1, tr, W), lambda i, sc_ref: (sc_ref[0], sc_ref[1] * nb + i, 0)),
                  pl.BlockSpec((1, tr, W), lambda i, sc_ref: (sc_ref[0], i, 0))]
        + [pl.BlockSpec((1, tr, W), lambda i, sc_ref, j=j: (j, i, 0)) for j in range(3)],
        out_specs=pl.BlockSpec((tr, W), lambda i, sc_ref: (i, 0)))
    return pl.pallas_call(
        body, name=name, grid_spec=grid_spec, out_shape=jax.ShapeDtypeStruct((Rh, W), F32),
        compiler_params=_params("parallel"),
    )(sc, g, r1, r2, r2, r2)


BIG = (("w_in", 2), ("pool_w", 2), ("w_attn_br", 2), ("w_pool_br", 2), ("w_ssm_br", 1), ("w_out", 1),
       ("w_gate_up", 2), ("w_down", 1))


def _pack_rows(n):
    r = -(-n // PACK_W)
    return -(-r // PACK_ROWS) * PACK_ROWS


def _pack(flats):
    flat = jnp.concatenate(flats, axis=1)
    rows = _pack_rows(flat.shape[1])
    flat = jnp.pad(flat, ((0, 0), (0, rows * PACK_W - flat.shape[1])))
    return flat.reshape(flat.shape[0], rows, PACK_W)


def _unpack(buf, shapes):
    lead = buf.shape[0]
    flat = buf.reshape(lead, -1)
    out, off = [], 0
    for shp in shapes:
        n = 1
        for d in shp:
            n *= d
        out.append(flat[:, off:off + n].reshape((lead,) + tuple(shp)))
        off += n
    return out


def _join(piece, axis):
    t = jnp.moveaxis(piece, 0, axis)
    shp = t.shape
    return t.reshape(shp[:axis] + (shp[axis] * shp[axis + 1],) + shp[axis + 2:])


def _split(full, axis):
    shp = full.shape
    t = full.reshape(shp[:axis] + (4, shp[axis] // 4) + shp[axis + 1:])
    return jnp.moveaxis(t, axis, 0).reshape(4, -1)


def _in_segments(w_in):
    aw, kw, _, pw, di, cc, h, gd = C.in_widths
    o = [0, aw + 2 * kw]
    for wdt in (pw, di, cc, h, gd):
        o.append(o[-1] + wdt)
    segs = [w_in[:, o[i]:o[i + 1]] for i in range(6)]
    segs[4] = jnp.pad(segs[4], ((0, 0), (0, 128 - h)))
    return segs


def _layer_fwd(x, p, tag):
    nm = lambda s: f"{s}_{tag}"
    H = C.H
    h = rms_fwd(x, p["ln1_w"], name=nm("rms1"))
    wq, wu, wz, wx, wd, wg = p["in_segs"]
    qkv = matmul(h, wq, name=nm("mm_qkv"), out_dtype=MXU)
    u = matmul(h, wu, name=nm("mm_u"))
    z = matmul(h, wz, name=nm("mm_z"))
    xbc = matmul(h, wx, name=nm("mm_xbc"))
    dtp = matmul(h, wd, name=nm("mm_dt"))
    gl = matmul(h, wg, name=nm("mm_gate"))
    att = attn_fwd(qkv, p["attn_sink"], name=nm("attn_fwd"))
    pool, mixed = pool_fwd(u, p["pool_w"], p["pool_scale"], name=nm("pool_fwd"))
    xa = conv_fwd(xbc, p["conv_w"], p["conv_b"], name=nm("conv_fwd"))
    raw = dtp[:, :H]
    dt_t, acs_t = ssd_prep(raw.T, p["dt_bias"].T, p["a_log"].T, name=nm("ssd_prep"))
    dt, acs = dt_t.T, acs_t.T
    y, hprev = ssd_fwd(xa, dt, acs, acs_t, p["d_skip"], name=nm("ssd_fwd"))
    ssm = gnorm_fwd(y, z, p["ssm_norm_w"], name=nm("gnorm_fwd"))
    abr = matmul(att, p["w_attn_br"], name=nm("mm_abr"))
    pbr = matmul(pool, p["w_pool_br"], name=nm("mm_pbr"))
    sbr = matmul(ssm, p["w_ssm_br"], name=nm("mm_sbr"))
    merged = merge_fwd(gl, abr, pbr, sbr, name=nm("merge_fwd"))
    xm = matmul(merged, p["w_out"], add=x, name=nm("mm_out"))
    h2 = rms_fwd(xm, p["ln2_w"], name=nm("rms2"))
    gu = matmul(h2, p["w_gate_up"], name=nm("mm_gu"))
    act = swiglu_fwd(gu, name=nm("swiglu_fwd"))
    xo = matmul(act, p["w_down"], add=xm, name=nm("mm_down"))
    saved = dict(x=x, h=h, qkv=qkv, z=z, xbc=xbc, raw=raw, gl=gl, att=att, pool=pool, mixed=mixed, xa=xa, dt=dt, acs=acs,
                 acs_t=acs_t, y=y, hprev=hprev, ssm=ssm, abr=abr, pbr=pbr, sbr=sbr, merged=merged, xm=xm, h2=h2, gu=gu,
                 act=act)
    return xo, saved


def _layer_bwd(dxo, dxo_m, p, s, tag):
    nm = lambda t: f"{t}_{tag}"
    H = C.H
    g = {}
    g["w_down"] = matmul(s["act"], dxo_m, ta=True, name=nm("mmg_down"))
    dact = matmul(dxo_m, p["w_down"], tb=True, name=nm("mmb_down"))
    dgu = swiglu_bwd(s["gu"], dact, name=nm("swiglu_bwd"))
    g["w_gate_up"] = matmul(s["h2"], dgu, ta=True, name=nm("mmg_gu"))
    dh2 = matmul(dgu, p["w_gate_up"], tb=True, name=nm("mmb_gu"))
    dxm, dxm_m, g["ln2_w"] = rms_bwd(s["xm"], p["ln2_w"], dh2, dxo, name=nm("rms2_bwd"))
    g["w_out"] = matmul(s["merged"], dxm_m, ta=True, name=nm("mmg_out"))
    dmerged = matmul(dxm_m, p["w_out"], tb=True, name=nm("mmb_out"))
    dabr, dpbr, dsbr, dgl = merge_bwd(s["gl"], s["abr"], s["pbr"], s["sbr"], dmerged, name=nm("merge_bwd"))
    g["w_attn_br"] = matmul(s["att"], dabr, ta=True, name=nm("mmg_abr"))
    g["w_pool_br"] = matmul(s["pool"], dpbr, ta=True, name=nm("mmg_pbr"))
    g["w_ssm_br"] = matmul(s["ssm"], dsbr, ta=True, name=nm("mmg_sbr"))
    datt = matmul(dabr, p["w_attn_br"], tb=True, name=nm("mmb_abr"))
    dpool = matmul(dpbr, p["w_pool_br"], tb=True, name=nm("mmb_pbr"))
    dssm = matmul(dsbr, p["w_ssm_br"], tb=True, name=nm("mmb_sbr"))
    dq, dk, dv, g["attn_sink"] = attn_bwd(s["qkv"], p["attn_sink"], datt, name=nm("attn_bwd"))
    dqkv = jnp.concatenate([dq, dk, dv], axis=1)
    du, g["pool_w"], g["pool_scale"] = pool_bwd(dpool, s["mixed"], p["pool_w"], p["pool_scale"], name=nm("pool_bwd"))
    dy, dz, g["ssm_norm_w"] = gnorm_bwd(s["y"], s["z"], p["ssm_norm_w"], dssm, name=nm("gnorm_bwd"))
    dxa, ddt, dacs, g["d_skip"] = ssd_bwd(s["xa"], s["dt"], s["acs"], s["acs_t"], p["d_skip"], s["hprev"], dy,
                                          name=nm("ssd_bwd"))
    draw, g["dt_bias"], g["a_log"] = ssd_post(ddt, dacs, s["dt"], s["raw"], p["dt_bias"], p["a_log"], name=nm("ssd_post"))
    dxbc, g["conv_w"], g["conv_b"] = conv_bwd(s["xbc"], p["conv_w"], p["conv_b"], dxa, name=nm("conv_bwd"))
    ddtp = jnp.pad(draw, ((0, 0), (0, 128 - H))).astype(MXU)
    dsegs = (dqkv, du, dz, dxbc, ddtp, dgl)
    dh = None
    gsegs = []
    for k, (d, w) in enumerate(zip(dsegs, p["in_segs"])):
        dh = matmul(d, w, tb=True, add=dh, name=nm(f"mmb_in{k}"))
        gsegs.append(matmul(s["h"], d, ta=True, name=nm(f"mmg_in{k}")))
    gsegs[4] = gsegs[4][:, :H]
    g["w_in"] = jnp.concatenate(gsegs, axis=1)
    dx, dx_m, g["ln1_w"] = rms_bwd(s["x"], p["ln1_w"], dh, dxm, name=nm("rms1_bwd"))
    return dx, dx_m, g


SMALL = ("ln1_w", "attn_sink", "conv_w", "conv_b", "dt_bias", "a_log", "d_skip", "ssm_norm_w", "pool_scale", "ln2_w")
WEIGHTS = ("ln1_w", "w_in", "attn_sink", "conv_w", "conv_b", "dt_bias", "a_log", "d_skip", "ssm_norm_w", "pool_w",
           "pool_scale", "w_attn_br", "w_pool_br", "w_ssm_br", "w_out", "ln2_w", "w_gate_up", "w_down", "final_w")


def _step(x, loss_target, w, m, v):
    depth = C.depth
    xi, yi, ci = _place()
    shard = 2 * xi + yi
    big_axes = dict(BIG)

    packed = _pack([w[n].astype(MXU).reshape(1, -1) for n, _ in BIG])[0]
    gathered = gather_shards(packed, name="gather_weights")
    pieces = _unpack(gathered, [w[n].shape for n, _ in BIG])
    full = {n: _join(pc, ax) for (n, ax), pc in zip(BIG, pieces)}
    cw_all = gather_all(_pack_small([w["conv_w"].reshape(1, -1)]), name="gather_conv_w", total=False)
    cw_shards = cw_all[0::2].reshape(4, -1)[:, :w["conv_w"].size].reshape((4,) + w["conv_w"].shape)
    full["conv_w"] = _join(cw_shards, 2)

    layers = []
    for i in range(depth):
        p = {n: full[n][i] for n in full}
        p["in_segs"] = _in_segments(p.pop("w_in"))
        for n in SMALL:
            if n != "conv_w":
                p[n] = w[n][i][None]
        layers.append(p)

    xs = x[0]
    saved = []
    for i in range(depth):
        xs, sv = _layer_fwd(xs, layers[i], f"l{i}")
        saved.append(sv)
    loss_part, dx, dx_m, g_final = final_loss(xs, w["final_w"][None], loss_target[0], name="final_loss")
    grads = [None] * depth
    for i in reversed(range(depth)):
        dx, dx_m, grads[i] = _layer_bwd(dx, dx_m, layers[i], saved[i], f"l{i}")

    gfull = {n: jnp.stack([grads[i][n] for i in range(depth)]) for n, _ in BIG}
    gp = _pack([_split(gfull[n], big_axes[n]).astype(MXU) for n, _ in BIG])
    cvec = ci.astype(jnp.int32).reshape(1)
    sib = swap_halves(gp, name="swap_halves")
    chip = pair_sum(gp, sib, cvec, name="pair_sum")
    others = scatter_chips(chip, name="scatter_chips")
    mine = chip_sum(gp, sib, others, jnp.stack([shard, ci]).astype(jnp.int32), name="chip_sum")
    reduced = join_halves(mine, name="join_halves")
    gshards = _unpack(reduced[None], [w[n].shape for n, _ in BIG])
    gout = {n: gs[0] for (n, _), gs in zip(BIG, gshards)}

    small = [jnp.stack([grads[i][n].reshape(-1) for i in range(depth)]).reshape(1, -1) for n in SMALL]
    small += [g_final.reshape(1, -1), loss_part.reshape(1, -1)]
    tot = gather_all(_pack_small(small), name="sum_small", total=True)
    parts = _unpack_small(tot, [t.shape[1] for t in small])
    for n, t in zip(SMALL, parts):
        if n == "conv_w":
            cols = w[n].shape[2]
            gout[n] = lax.dynamic_slice_in_dim(t.reshape(depth, CONV_K, -1), shard * cols, cols, axis=2)
        else:
            gout[n] = t.reshape(w[n].shape)
    gout["final_w"] = parts[-2].reshape(w["final_w"].shape)
    loss = parts[-1].reshape(())

    upd = {n: adamw(w[n], gout[n].reshape(w[n].shape), m[n], v[n], name=f"adamw_{n}") for n in WEIGHTS}
    return (loss, dx[None], *[gout[n].reshape(w[n].shape) for n in WEIGHTS], *[upd[n][0] for n in WEIGHTS],
            *[upd[n][1] for n in WEIGHTS], *[upd[n][2] for n in WEIGHTS])


def _pack_small(parts):
    flat = jnp.concatenate(parts, axis=1)
    rows = -(-flat.shape[1] // PACK_W)
    rows = -(-rows // 8) * 8
    return jnp.pad(flat, ((0, 0), (0, rows * PACK_W - flat.shape[1]))).reshape(rows, PACK_W)


def _unpack_small(buf, sizes):
    flat = buf.reshape(-1)
    out, off = [], 0
    for n in sizes:
        out.append(flat[off:off + n])
        off += n
    return out


def kernel(x, ln1_w, w_in, attn_sink, conv_w, conv_b, dt_bias, a_log, d_skip, ssm_norm_w, pool_w, pool_scale, w_attn_br, w_pool_br, w_ssm_br, w_out, ln2_w, w_gate_up, w_down, final_w, loss_target, m_ln1_w, m_w_in, m_attn_sink, m_conv_w, m_conv_b, m_dt_bias, m_a_log, m_d_skip, m_ssm_norm_w, m_pool_w, m_pool_scale, m_w_attn_br, m_w_pool_br, m_w_ssm_br, m_w_out, m_ln2_w, m_w_gate_up, m_w_down, m_final_w, v_ln1_w, v_w_in, v_attn_sink, v_conv_w, v_conv_b, v_dt_bias, v_a_log, v_d_skip, v_ssm_norm_w, v_pool_w, v_pool_scale, v_w_attn_br, v_w_pool_br, v_w_ssm_br, v_w_out, v_ln2_w, v_w_gate_up, v_w_down, v_final_w):
    w = dict(ln1_w=ln1_w, w_in=w_in, attn_sink=attn_sink, conv_w=conv_w, conv_b=conv_b, dt_bias=dt_bias, a_log=a_log,
             d_skip=d_skip, ssm_norm_w=ssm_norm_w, pool_w=pool_w, pool_scale=pool_scale, w_attn_br=w_attn_br,
             w_pool_br=w_pool_br, w_ssm_br=w_ssm_br, w_out=w_out, ln2_w=ln2_w, w_gate_up=w_gate_up, w_down=w_down,
             final_w=final_w)
    m = dict(ln1_w=m_ln1_w, w_in=m_w_in, attn_sink=m_attn_sink, conv_w=m_conv_w, conv_b=m_conv_b, dt_bias=m_dt_bias,
             a_log=m_a_log, d_skip=m_d_skip, ssm_norm_w=m_ssm_norm_w, pool_w=m_pool_w, pool_scale=m_pool_scale,
             w_attn_br=m_w_attn_br, w_pool_br=m_w_pool_br, w_ssm_br=m_w_ssm_br, w_out=m_w_out, ln2_w=m_ln2_w,
             w_gate_up=m_w_gate_up, w_down=m_w_down, final_w=m_final_w)
    v = dict(ln1_w=v_ln1_w, w_in=v_w_in, attn_sink=v_attn_sink, conv_w=v_conv_w, conv_b=v_conv_b, dt_bias=v_dt_bias,
             a_log=v_a_log, d_skip=v_d_skip, ssm_norm_w=v_ssm_norm_w, pool_w=v_pool_w, pool_scale=v_pool_scale,
             w_attn_br=v_w_attn_br, w_pool_br=v_w_pool_br, w_ssm_br=v_w_ssm_br, w_out=v_w_out, ln2_w=v_ln2_w,
             w_gate_up=v_w_gate_up, w_down=v_w_down, final_w=v_final_w)
    return _step(x, loss_target, w, m, v)
```

```python
import functools

import jax
import jax.numpy as jnp
from jax import lax
from jax.experimental import pallas as pl
from jax.experimental.pallas import tpu as pltpu

F32 = jnp.float32
MXU = jnp.bfloat16
VMEM_LIMIT = 56 * 1024 * 1024
EPS = 1e-6
NEG = -1e30

ADAM_LR, ADAM_B1, ADAM_B2, ADAM_EPS, ADAM_WD, ADAM_STEP = 0.001, 0.9, 0.999, 1e-08, 0.01, 10


class Cfg:
    def __init__(self, d_model=2048, seq=8192, depth=2, q_heads=16, kv_heads=4, head_dim=64,
                 ssm_head_dim=64, ssm_groups=4, d_state=128):
        self.D, self.T, self.depth = d_model, seq, depth
        self.hd, self.qh, self.kvh = head_dim, q_heads, kv_heads
        self.AW, self.KW = q_heads * head_dim, kv_heads * head_dim
        self.blk = 128
        self.PW = d_model // 2
        self.PG = self.PW // 4
        self.DI = d_model
        self.P = ssm_head_dim
        self.H = self.DI // self.P
        self.G = ssm_groups
        self.HG = self.H // self.G
        self.N = d_state
        self.CC = self.DI + 2 * self.G * self.N
        self.F = -(-8 * d_model // (3 * 256)) * 256
        self.in_widths = (self.AW, self.KW, self.KW, self.PW, self.DI, self.CC, self.H, 3 * d_model)
        self.in_cols = sum(self.in_widths)


C = Cfg()
POOL_WINDOWS = (2, 4, 8, 16)


def _pick(n, cands):
    for c in cands:
        if n % c == 0:
            return c
    return n


def _params(*sem):
    return pltpu.CompilerParams(dimension_semantics=sem, vmem_limit_bytes=VMEM_LIMIT)


def _sigmoid(x):
    return 1.0 / (1.0 + jnp.exp(-x))


def _dot(a, b, dims):
    return lax.dot_general(a.astype(MXU), b.astype(MXU), (dims, ((), ())), preferred_element_type=F32)


def _dot_nn(a, b):
    return _dot(a, b, ((1,), (0,)))


def _dot_nt(a, b):
    return _dot(a, b, ((1,), (1,)))


def _dot_tn(a, b):
    return _dot(a, b, ((0,), (0,)))


MM_VMEM = 40 * 1024 * 1024
LANES = 128


def _tile(n, cap):
    best = None
    for d in range(LANES, min(n, cap) + 1, LANES):
        if n % d == 0:
            best = d
    return n if best is None else best


def matmul(a, b, *, name, ta=False, tb=False, out_dtype=F32, add=None, layer=None, b_cols=False, out_cols=False):
    M, K = (a.shape[1], a.shape[0]) if ta else a.shape
    rows, cols = b.shape[-2], (4 if b_cols else 1) * b.shape[-1]
    N, bk = (rows, cols) if tb else (cols, rows)
    assert K == bk, (a.shape, b.shape)
    n_unit = N // 4 if (out_cols or (b_cols and not tb)) else N
    k_unit = K // 4 if (b_cols and tb) else K
    tm = _tile(M, 1024)
    tk = _tile(k_unit, 1024 if ta else 2816)
    tn = _tile(n_unit, 1408 if ta else 2048)
    sa, sb, so = a.dtype.itemsize, b.dtype.itemsize, jnp.dtype(out_dtype).itemsize

    def need(tk, tn):
        acc = tm * tn * 4 if K > tk else 0
        return 2 * (tm * tk * sa + tk * tn * sb + tm * tn * so) + acc + (2 * tm * tn * add.dtype.itemsize if add is not None else 0)

    while need(tk, tn) > MM_VMEM:
        if tk > 1024 or tn <= 512:
            tk = _tile(k_unit, tk - LANES)
        else:
            tn = _tile(n_unit, tn - LANES)
    nk = K // tk
    a_spec = pl.BlockSpec((tk, tm), lambda i, j, k: (k, i)) if ta else pl.BlockSpec((tm, tk), lambda i, j, k: (i, k))
    if b_cols and tb:
        per = k_unit // tk
        b_spec = pl.BlockSpec((None, None, tn, tk), lambda i, j, k: (k // per, layer, j, k % per))
    elif b_cols:
        per = n_unit // tn
        b_spec = pl.BlockSpec((None, None, tk, tn), lambda i, j, k: (j // per, layer, k, j % per))
    elif layer is not None:
        b_spec = (pl.BlockSpec((None, tn, tk), lambda i, j, k: (layer, j, k)) if tb
                  else pl.BlockSpec((None, tk, tn), lambda i, j, k: (layer, k, j)))
    else:
        b_spec = pl.BlockSpec((tn, tk), lambda i, j, k: (j, k)) if tb else pl.BlockSpec((tk, tn), lambda i, j, k: (k, j))
    if out_cols:
        per = n_unit // tn
        o_spec = pl.BlockSpec((None, tm, tn), lambda i, j, k: (j // per, i, j % per))
        out_shape = jax.ShapeDtypeStruct((4, M, N // 4), out_dtype)
    else:
        o_spec = pl.BlockSpec((tm, tn), lambda i, j, k: (i, j))
        out_shape = jax.ShapeDtypeStruct((M, N), out_dtype)
    dims = ((0 if ta else 1,), (1 if tb else 0,))
    has_add = add is not None

    def body(*refs):
        a_ref, b_ref = refs[0], refs[1]
        add_ref = refs[2] if has_add else None
        o_ref = refs[3] if has_add else refs[2]
        part = _dot(a_ref[...], b_ref[...], dims)

        def finish(total):
            if has_add:
                total = total + add_ref[...].astype(F32)
            o_ref[...] = total.astype(o_ref.dtype)

        if nk == 1:
            finish(part)
        else:
            acc_ref = refs[-1]
            k = pl.program_id(2)

            @pl.when(k == 0)
            def _():
                acc_ref[...] = part

            @pl.when(k > 0)
            def _():
                acc_ref[...] += part

            @pl.when(k == nk - 1)
            def _():
                finish(acc_ref[...])

    in_specs = [a_spec, b_spec] + ([o_spec] if has_add else [])
    args = (a, b) + ((add,) if has_add else ())
    return pl.pallas_call(
        body, name=name, grid=(M // tm, N // tn, nk), in_specs=in_specs, out_specs=o_spec, out_shape=out_shape,
        scratch_shapes=[pltpu.VMEM((tm, tn), F32)] if nk > 1 else [],
        compiler_params=_params("parallel", "parallel", "arbitrary"),
    )(*args)


def _row_tile(t):
    return _pick(t, (256, 128, 64, 32, 16, 8))


def rms_fwd(x, w, *, name):
    T, D = x.shape
    tr = _row_tile(T)

    def body(x_ref, w_ref, o_ref):
        xv = x_ref[...]
        r = lax.rsqrt(jnp.mean(xv * xv, axis=-1, keepdims=True) + EPS)
        o_ref[...] = (xv * r * w_ref[...]).astype(o_ref.dtype)

    row = pl.BlockSpec((tr, D), lambda i: (i, 0))
    return pl.pallas_call(
        body, name=name, grid=(T // tr,), in_specs=[row, pl.BlockSpec((1, D), lambda i: (0, 0))], out_specs=row,
        out_shape=jax.ShapeDtypeStruct((T, D), MXU), compiler_params=_params("parallel"),
    )(x, w)


def rms_bwd(x, w, dh, dres, *, name):
    T, D = x.shape
    tr = _row_tile(T)

    def body(x_ref, w_ref, dh_ref, dres_ref, dx_ref, dxm_ref, dw_ref):
        xv = x_ref[...]
        dhv = dh_ref[...].astype(F32)
        r = lax.rsqrt(jnp.mean(xv * xv, axis=-1, keepdims=True) + EPS)
        g = dhv * w_ref[...]
        dot = jnp.mean(g * xv, axis=-1, keepdims=True)
        dx = dres_ref[...] + r * g - xv * (r * r * r * dot)
        dx_ref[...] = dx
        dxm_ref[...] = dx.astype(dxm_ref.dtype)
        part = jnp.sum(dhv * xv * r, axis=0, keepdims=True)

        @pl.when(pl.program_id(0) == 0)
        def _():
            dw_ref[...] = part

        @pl.when(pl.program_id(0) > 0)
        def _():
            dw_ref[...] += part

    row = pl.BlockSpec((tr, D), lambda i: (i, 0))
    vec = pl.BlockSpec((1, D), lambda i: (0, 0))
    return pl.pallas_call(
        body, name=name, grid=(T // tr,), in_specs=[row, vec, row, row], out_specs=[row, row, vec],
        out_shape=[jax.ShapeDtypeStruct((T, D), F32), jax.ShapeDtypeStruct((T, D), MXU),
                   jax.ShapeDtypeStruct((1, D), F32)],
        compiler_params=_params("arbitrary"),
    )(x, w, dh, dres)


def final_loss(x, w, target, *, name):
    T, D = x.shape
    tr = _row_tile(T)

    def body(x_ref, w_ref, t_ref, loss_ref, dx_ref, dxm_ref, dw_ref):
        xv = x_ref[...]
        wv = w_ref[...]
        r = lax.rsqrt(jnp.mean(xv * xv, axis=-1, keepdims=True) + EPS)
        err = xv * r * wv - t_ref[...]
        lpart = 0.5 * jnp.sum(jnp.mean(err * err, axis=-1, keepdims=True), axis=0, keepdims=True)
        dy = err * (1.0 / D)
        g = dy * wv
        dot = jnp.mean(g * xv, axis=-1, keepdims=True)
        dx = r * g - xv * (r * r * r * dot)
        dx_ref[...] = dx
        dxm_ref[...] = dx.astype(dxm_ref.dtype)
        part = jnp.sum(dy * xv * r, axis=0, keepdims=True)

        @pl.when(pl.program_id(0) == 0)
        def _():
            dw_ref[...] = part
            loss_ref[...] = lpart

        @pl.when(pl.program_id(0) > 0)
        def _():
            dw_ref[...] += part
            loss_ref[...] += lpart

    row = pl.BlockSpec((tr, D), lambda i: (i, 0))
    vec = pl.BlockSpec((1, D), lambda i: (0, 0))
    one = pl.BlockSpec((1, 1), lambda i: (0, 0))
    return pl.pallas_call(
        body, name=name, grid=(T // tr,), in_specs=[row, vec, row], out_specs=[one, row, row, vec],
        out_shape=[jax.ShapeDtypeStruct((1, 1), F32), jax.ShapeDtypeStruct((T, D), F32),
                   jax.ShapeDtypeStruct((T, D), MXU), jax.ShapeDtypeStruct((1, D), F32)],
        compiler_params=_params("arbitrary"),
    )(x, w, target)


def swiglu_fwd(gu, *, name):
    T, F2 = gu.shape
    F = F2 // 2
    tr = _row_tile(T)

    def body(g_ref, u_ref, o_ref):
        g = g_ref[...]
        o_ref[...] = (g * _sigmoid(g) * u_ref[...]).astype(o_ref.dtype)

    return pl.pallas_call(
        body, name=name, grid=(T // tr,),
        in_specs=[pl.BlockSpec((tr, F), lambda i: (i, 0)), pl.BlockSpec((tr, F), lambda i: (i, 1))],
        out_specs=pl.BlockSpec((tr, F), lambda i: (i, 0)),
        out_shape=jax.ShapeDtypeStruct((T, F), MXU), compiler_params=_params("parallel"),
    )(gu, gu)


def swiglu_bwd(gu, dact, *, name):
    T, F2 = gu.shape
    F = F2 // 2
    tr = _row_tile(T)

    def body(g_ref, u_ref, d_ref, o_ref):
        g = g_ref[...]
        d = d_ref[...]
        s = _sigmoid(g)
        o_ref[:, :F] = (d * u_ref[...] * s * (1.0 + g * (1.0 - s))).astype(o_ref.dtype)
        o_ref[:, F:] = (d * g * s).astype(o_ref.dtype)

    lo = pl.BlockSpec((tr, F), lambda i: (i, 0))
    hi = pl.BlockSpec((tr, F), lambda i: (i, 1))
    return pl.pallas_call(
        body, name=name, grid=(T // tr,), in_specs=[lo, hi, lo], out_specs=pl.BlockSpec((tr, F2), lambda i: (i, 0)),
        out_shape=jax.ShapeDtypeStruct((T, F2), MXU), compiler_params=_params("parallel"),
    )(gu, gu, dact)


def merge_fwd(gl, abr, pbr, sbr, *, name):
    T, D = abr.shape
    tr = _row_tile(T)

    def body(g0, g1, g2, a_ref, p_ref, s_ref, o_ref):
        m = _sigmoid(g0[...]) * a_ref[...] + _sigmoid(g1[...]) * p_ref[...] + _sigmoid(g2[...]) * s_ref[...]
        o_ref[...] = m.astype(o_ref.dtype)

    row = pl.BlockSpec((tr, D), lambda i: (i, 0))
    gs = [pl.BlockSpec((tr, D), lambda i, j=j: (i, j)) for j in range(3)]
    return pl.pallas_call(
        body, name=name, grid=(T // tr,), in_specs=gs + [row, row, row], out_specs=row,
        out_shape=jax.ShapeDtypeStruct((T, D), MXU), compiler_params=_params("parallel"),
    )(gl, gl, gl, abr, pbr, sbr)


def merge_bwd(gl, abr, pbr, sbr, dm, *, name):
    T, D = abr.shape
    tr = _row_tile(T)

    def body(g0, g1, g2, a_ref, p_ref, s_ref, dm_ref, da_ref, dp_ref, ds_ref, dg_ref):
        d = dm_ref[...]
        for j, (g_ref, b_ref, db_ref) in enumerate(((g0, a_ref, da_ref), (g1, p_ref, dp_ref), (g2, s_ref, ds_ref))):
            s = _sigmoid(g_ref[...])
            db_ref[...] = (d * s).astype(db_ref.dtype)
            dg_ref[:, j * D:(j + 1) * D] = (d * b_ref[...] * s * (1.0 - s)).astype(dg_ref.dtype)

    row = pl.BlockSpec((tr, D), lambda i: (i, 0))
    gs = [pl.BlockSpec((tr, D), lambda i, j=j: (i, j)) for j in range(3)]
    return pl.pallas_call(
        body, name=name, grid=(T // tr,), in_specs=gs + [row] * 4,
        out_specs=[row] * 3 + [pl.BlockSpec((tr, 3 * D), lambda i: (i, 0))],
        out_shape=[jax.ShapeDtypeStruct((T, D), MXU)] * 3 + [jax.ShapeDtypeStruct((T, 3 * D), MXU)],
        compiler_params=_params("parallel"),
    )(gl, gl, gl, abr, pbr, sbr, dm)


def gnorm_fwd(y, z, w, *, name):
    T, DI = y.shape
    gw = DI // C.G
    tr = _row_tile(T)

    def body(y_ref, z_ref, w_ref, o_ref):
        for g in range(C.G):
            sl = slice(g * gw, (g + 1) * gw)
            zz = z_ref[:, sl]
            v = y_ref[:, sl] * (zz * _sigmoid(zz))
            r = lax.rsqrt(jnp.mean(v * v, axis=-1, keepdims=True) + EPS)
            o_ref[:, sl] = (v * r * w_ref[:, sl]).astype(o_ref.dtype)

    row = pl.BlockSpec((tr, DI), lambda i: (i, 0))
    return pl.pallas_call(
        body, name=name, grid=(T // tr,), in_specs=[row, row, pl.BlockSpec((1, DI), lambda i: (0, 0))],
        out_specs=row, out_shape=jax.ShapeDtypeStruct((T, DI), MXU), compiler_params=_params("parallel"),
    )(y, z, w)


def gnorm_bwd(y, z, w, do, *, name):
    T, DI = y.shape
    gw = DI // C.G
    tr = _row_tile(T)

    def body(y_ref, z_ref, w_ref, do_ref, dy_ref, dz_ref, dw_ref):
        first = pl.program_id(0) == 0
        for g in range(C.G):
            sl = slice(g * gw, (g + 1) * gw)
            zz = z_ref[:, sl]
            yy = y_ref[:, sl]
            s = _sigmoid(zz)
            sz = zz * s
            v = yy * sz
            r = lax.rsqrt(jnp.mean(v * v, axis=-1, keepdims=True) + EPS)
            dov = do_ref[:, sl]
            gg = dov * w_ref[:, sl]
            dot = jnp.mean(gg * v, axis=-1, keepdims=True)
            dv = r * gg - v * (r * r * r * dot)
            dy_ref[:, sl] = dv * sz
            dz_ref[:, sl] = (dv * yy * s * (1.0 + zz * (1.0 - s))).astype(dz_ref.dtype)
            part = jnp.sum(dov * v * r, axis=0, keepdims=True)

            @pl.when(first)
            def _():
                dw_ref[:, sl] = part

            @pl.when(jnp.logical_not(first))
            def _():
                dw_ref[:, sl] += part

    row = pl.BlockSpec((tr, DI), lambda i: (i, 0))
    vec = pl.BlockSpec((1, DI), lambda i: (0, 0))
    return pl.pallas_call(
        body, name=name, grid=(T // tr,), in_specs=[row, row, vec, row], out_specs=[row, row, vec],
        out_shape=[jax.ShapeDtypeStruct((T, DI), F32), jax.ShapeDtypeStruct((T, DI), MXU),
                   jax.ShapeDtypeStruct((1, DI), F32)],
        compiler_params=_params("arbitrary"),
    )(y, z, w, do)


def adamw(w, g, m, v, *, name):
    shape = w.shape
    cols = shape[-1]
    rows = w.size // cols
    w2, g2, m2, v2 = (t.reshape(rows, cols) for t in (w, g, m, v))
    tr = rows if rows * cols * 4 <= (2 << 20) else _pick(rows, (512, 256, 128, 64, 32, 16, 8))
    while tr * cols * 4 > (2 << 20) and tr % 16 == 0:
        tr //= 2
    c1 = 1.0 - ADAM_B1 ** ADAM_STEP
    c2 = 1.0 - ADAM_B2 ** ADAM_STEP

    def body(w_ref, g_ref, m_ref, v_ref, d_ref, nm_ref, nv_ref):
        gv = g_ref[...]
        nm = ADAM_B1 * m_ref[...] + (1.0 - ADAM_B1) * gv
        nv = ADAM_B2 * v_ref[...] + (1.0 - ADAM_B2) * (gv * gv)
        d_ref[...] = -ADAM_LR * ((nm / c1) / (jnp.sqrt(nv / c2) + ADAM_EPS) + ADAM_WD * w_ref[...])
        nm_ref[...] = nm
        nv_ref[...] = nv

    row = pl.BlockSpec((tr, cols), lambda i: (i, 0))
    outs = pl.pallas_call(
        body, name=name, grid=(rows // tr,), in_specs=[row] * 4, out_specs=[row] * 3,
        out_shape=[jax.ShapeDtypeStruct((rows, cols), F32)] * 3, compiler_params=_params("parallel"),
    )(w2, g2, m2, v2)
    return tuple(o.reshape(shape) for o in outs)


def adamw_layers(w, g_layers, m, v, *, name):
    shape = w.shape
    L = shape[0]
    rows, cols = g_layers[0].shape
    w3, m3, v3 = (t.reshape(L, rows, cols) for t in (w, m, v))
    tr = _pick(rows, (512, 256, 128, 64, 32, 16, 8))
    while tr * cols * 4 > (2 << 20) and tr % 16 == 0:
        tr //= 2
    c1 = 1.0 - ADAM_B1 ** ADAM_STEP
    c2 = 1.0 - ADAM_B2 ** ADAM_STEP

    def body(*refs):
        w_ref, m_ref, v_ref = refs[0], refs[1], refs[2]
        g_refs = refs[3:3 + L]
        go_ref, d_ref, nm_ref, nv_ref = refs[3 + L:]
        layer = pl.program_id(0)
        gv = g_refs[0][...]
        for k in range(1, L):
            gv = jnp.where(layer == k, g_refs[k][...], gv)
        nm = ADAM_B1 * m_ref[...] + (1.0 - ADAM_B1) * gv
        nv = ADAM_B2 * v_ref[...] + (1.0 - ADAM_B2) * (gv * gv)
        d_ref[...] = -ADAM_LR * ((nm / c1) / (jnp.sqrt(nv / c2) + ADAM_EPS) + ADAM_WD * w_ref[...])
        go_ref[...] = gv
        nm_ref[...] = nm
        nv_ref[...] = nv

    blk = pl.BlockSpec((None, tr, cols), lambda l, i: (l, i, 0))
    gblk = pl.BlockSpec((tr, cols), lambda l, i: (i, 0))
    outs = pl.pallas_call(
        body, name=name, grid=(L, rows // tr), in_specs=[blk] * 3 + [gblk] * L, out_specs=[blk] * 4,
        out_shape=[jax.ShapeDtypeStruct((L, rows, cols), F32)] * 4, compiler_params=_params("parallel", "parallel"),
    )(w3, m3, v3, *g_layers)
    return tuple(o.reshape(shape) for o in outs)


def _attn_masks(i):
    B = C.blk
    row = lax.broadcasted_iota(jnp.int32, (B, 2 * B), 0)
    col = lax.broadcasted_iota(jnp.int32, (B, 2 * B), 1)
    diff = row + B - col
    return (diff >= 0) & (diff < B) & ((col >= B) | (i > 0))


def _kv2(prev_ref, cur_ref, m, lo):
    sl = slice(m * 128, (m + 1) * 128)
    slab = jnp.concatenate([prev_ref[:, sl], cur_ref[:, sl]], axis=0).astype(F32)
    rolled = pltpu.roll(slab, 64, axis=1)
    return jnp.where(lo, slab, rolled), jnp.where(lo, rolled, slab)


def _attn_probs(qh, k2, mask, sk):
    s = _dot_nt(qh, k2) * (C.hd ** -0.5)
    s = jnp.where(mask, s, NEG)
    mx = jnp.maximum(jnp.max(s, axis=-1, keepdims=True), sk)
    p = jnp.exp(s - mx)
    esk = jnp.exp(sk - mx)
    inv = 1.0 / (jnp.sum(p, axis=-1, keepdims=True) + esk)
    return p * inv, esk * inv


def attn_fwd(qkv, sink, *, name):
    T = qkv.shape[0]
    B, AW, KW = C.blk, C.AW, C.KW
    assert C.hd == 64 and KW % 128 == 0 and (C.qh // C.kvh) % 2 == 0
    nb = T // B
    kb = AW // KW
    ppk = C.qh // C.kvh // 2

    def body(q_ref, kc_ref, kp_ref, vc_ref, vp_ref, sink_ref, o_ref):
        i = pl.program_id(0)
        lo = lax.broadcasted_iota(jnp.int32, (1, 128), 1) < 64
        mask = _attn_masks(i)
        for m in range(KW // 128):
            k2s = _kv2(kp_ref, kc_ref, m, lo)
            v2s = _kv2(vp_ref, vc_ref, m, lo)
            for par in range(2):
                for pr in range(ppk):
                    pair = (2 * m + par) * ppk + pr
                    sl = slice(pair * 128, (pair + 1) * 128)
                    qp = q_ref[:, sl].astype(F32)
                    outs = []
                    for half in range(2):
                        h = 2 * pair + half
                        qh = jnp.where(lo if half == 0 else jnp.logical_not(lo), qp, 0.0)
                        p, _ = _attn_probs(qh, k2s[par], mask, sink_ref[:, h:h + 1])
                        outs.append(_dot_nn(p, v2s[par]))
                    o_ref[:, sl] = jnp.where(lo, outs[0], outs[1]).astype(o_ref.dtype)

    prev = lambda i: jnp.maximum(i - 1, 0)
    return pl.pallas_call(
        body, name=name, grid=(nb,),
        in_specs=[pl.BlockSpec((B, AW), lambda i: (i, 0)),
                  pl.BlockSpec((B, KW), lambda i: (i, kb)), pl.BlockSpec((B, KW), lambda i: (prev(i), kb)),
                  pl.BlockSpec((B, KW), lambda i: (i, kb + 1)), pl.BlockSpec((B, KW), lambda i: (prev(i), kb + 1)),
                  pl.BlockSpec((1, C.qh), lambda i: (0, 0))],
        out_specs=pl.BlockSpec((B, AW), lambda i: (i, 0)),
        out_shape=jax.ShapeDtypeStruct((T, AW), MXU), compiler_params=_params("parallel"),
    )(qkv, qkv, qkv, qkv, qkv, sink)


def attn_bwd(qkv, sink, dout, *, name):
    T = qkv.shape[0]
    B, AW, KW = C.blk, C.AW, C.KW
    nb = T // B
    kb = AW // KW
    ppk = C.qh // C.kvh // 2
    scale = C.hd ** -0.5

    def body(q_ref, kc_ref, kp_ref, vc_ref, vp_ref, sink_ref, do_ref, dq_ref, dk_ref, dv_ref, ds_ref, ck_ref, cv_ref):
        i = pl.program_id(0)

        @pl.when(i == 0)
        def _():
            ck_ref[...] = jnp.zeros_like(ck_ref)
            cv_ref[...] = jnp.zeros_like(cv_ref)
            ds_ref[...] = jnp.zeros_like(ds_ref)

        @pl.when(i < nb)
        def _():
            lane = lax.broadcasted_iota(jnp.int32, (1, 128), 1)
            lo = lane < 64
            hlane = lax.broadcasted_iota(jnp.int32, (1, C.qh), 1)
            mask = _attn_masks(i)
            dsink = jnp.zeros((1, C.qh), F32)
            for m in range(KW // 128):
                ksl = slice(m * 128, (m + 1) * 128)
                k2s = _kv2(kp_ref, kc_ref, m, lo)
                v2s = _kv2(vp_ref, vc_ref, m, lo)
                folded = []
                for par in range(2):
                    dk2 = jnp.zeros((2 * B, 128), F32)
                    dv2 = jnp.zeros((2 * B, 128), F32)
                    for pr in range(ppk):
                        pair = (2 * m + par) * ppk + pr
                        sl = slice(pair * 128, (pair + 1) * 128)
                        qp = q_ref[:, sl].astype(F32)
                        dop = do_ref[:, sl].astype(F32)
                        dqs = []
                        for half in range(2):
                            h = 2 * pair + half
                            hm = lo if half == 0 else jnp.logical_not(lo)
                            qh = jnp.where(hm, qp, 0.0)
                            doh = jnp.where(hm, dop, 0.0)
                            p, psink = _attn_probs(qh, k2s[par], mask, sink_ref[:, h:h + 1])
                            dp = _dot_nt(doh, v2s[par])
                            delta = jnp.sum(p * dp, axis=-1, keepdims=True)
                            dsc = p * (dp - delta)
                            dqs.append(_dot_nn(dsc, k2s[par]) * scale)
                            dk2 = dk2 + _dot_tn(dsc, qh) * scale
                            dv2 = dv2 + _dot_tn(p, doh)
                            dsh = -jnp.sum(psink * delta, axis=0, keepdims=True)
                            dsink = dsink + jnp.where(hlane == h, dsh, 0.0)
                        dq_ref[:, sl] = jnp.where(lo, dqs[0], dqs[1]).astype(dq_ref.dtype)
                    folded.append((dk2 + pltpu.roll(dk2, 64, axis=1), dv2 + pltpu.roll(dv2, 64, axis=1)))
                dks = jnp.where(lo, folded[0][0], folded[1][0])
                dvs = jnp.where(lo, folded[0][1], folded[1][1])
                dk_ref[:, ksl] = (ck_ref[:, ksl] + dks[:B]).astype(dk_ref.dtype)
                dv_ref[:, ksl] = (cv_ref[:, ksl] + dvs[:B]).astype(dv_ref.dtype)
                ck_ref[:, ksl] = dks[B:]
                cv_ref[:, ksl] = dvs[B:]
            ds_ref[...] += dsink

        @pl.when(i == nb)
        def _():
            dk_ref[...] = ck_ref[...].astype(dk_ref.dtype)
            dv_ref[...] = cv_ref[...].astype(dv_ref.dtype)

    cur = lambda i: jnp.minimum(i, nb - 1)
    prev = lambda i: jnp.maximum(jnp.minimum(i, nb - 1) - 1, 0)
    out = lambda i: jnp.maximum(i - 1, 0)
    return pl.pallas_call(
        body, name=name, grid=(nb + 1,),
        in_specs=[pl.BlockSpec((B, AW), lambda i: (cur(i), 0)),
                  pl.BlockSpec((B, KW), lambda i: (cur(i), kb)), pl.BlockSpec((B, KW), lambda i: (prev(i), kb)),
                  pl.BlockSpec((B, KW), lambda i: (cur(i), kb + 1)), pl.BlockSpec((B, KW), lambda i: (prev(i), kb + 1)),
                  pl.BlockSpec((1, C.qh), lambda i: (0, 0)),
                  pl.BlockSpec((B, AW), lambda i: (cur(i), 0))],
        out_specs=[pl.BlockSpec((B, AW), lambda i: (cur(i), 0)),
                   pl.BlockSpec((B, KW), lambda i: (out(i), 0)), pl.BlockSpec((B, KW), lambda i: (out(i), 0)),
                   pl.BlockSpec((1, C.qh), lambda i: (0, 0))],
        out_shape=[jax.ShapeDtypeStruct((T, AW), MXU), jax.ShapeDtypeStruct((T, KW), MXU),
                   jax.ShapeDtypeStruct((T, KW), MXU), jax.ShapeDtypeStruct((1, C.qh), F32)],
        scratch_shapes=[pltpu.VMEM((B, KW), F32), pltpu.VMEM((B, KW), F32)],
        compiler_params=_params("arbitrary"),
    )(qkv, qkv, qkv, qkv, qkv, sink, dout)


POOL_HALO = 16


def _window_sum(e, w, n, forward):
    s, k = e, 1
    while k < w:
        s = s + pltpu.roll(s, (n - k) if forward else k, axis=0)
        k *= 2
    return s


def pool_fwd(u, pw, ps, *, name):
    T, PW = u.shape
    PG = C.PG
    tT = _pick(T, (256, 128))
    hb = tT // POOL_HALO

    def body(u_ref, uh_ref, pw_ref, ps_ref, o_ref, mx_ref):
        i = pl.program_id(0)
        halo = jnp.where(i > 0, uh_ref[...], 0.0)
        ext = jnp.concatenate([halo, u_ref[...]], axis=0)
        t = i * tT + lax.broadcasted_iota(jnp.int32, (tT, 1), 0)
        for g, w in enumerate(POOL_WINDOWS):
            sl = slice(g * PG, (g + 1) * PG)
            s = _window_sum(ext[:, sl], w, tT + POOL_HALO, False)[POOL_HALO:]
            cnt = jnp.minimum(t + 1, w).astype(F32)
            mixed = (s / cnt - u_ref[:, sl]).astype(MXU)
            mx_ref[:, sl] = mixed.astype(mx_ref.dtype)
            o_ref[:, sl] = (_dot_nn(mixed, pw_ref[g]) * ps_ref[:, sl]).astype(o_ref.dtype)

    row = pl.BlockSpec((tT, PW), lambda i: (i, 0))
    return pl.pallas_call(
        body, name=name, grid=(T // tT,),
        in_specs=[row, pl.BlockSpec((POOL_HALO, PW), lambda i: (jnp.maximum(i * hb - 1, 0), 0)),
                  pl.BlockSpec((4, PG, PG), lambda i: (0, 0, 0)), pl.BlockSpec((1, PW), lambda i: (0, 0))],
        out_specs=[row, row], out_shape=[jax.ShapeDtypeStruct((T, PW), MXU)] * 2,
        compiler_params=_params("parallel"),
    )(u, u, pw, ps)


def pool_bwd(dpool, mixed, pw, ps, *, name):
    T, PW = dpool.shape
    PG = C.PG
    tT = _pick(T, (256, 128))
    hb = tT // POOL_HALO
    n = T // tT
    rows = tT + POOL_HALO

    def body(dp_ref, dph_ref, mx_ref, pw_ref, ps_ref, du_ref, dpw_ref, dps_ref):
        i = pl.program_id(0)

        @pl.when(i == 0)
        def _():
            dpw_ref[...] = jnp.zeros_like(dpw_ref)
            dps_ref[...] = jnp.zeros_like(dps_ref)

        halo = jnp.where(i < n - 1, dph_ref[...], 0.0)
        dext = jnp.concatenate([dp_ref[...], halo], axis=0)
        t = i * tT + lax.broadcasted_iota(jnp.int32, (rows, 1), 0)
        for g, w in enumerate(POOL_WINDOWS):
            sl = slice(g * PG, (g + 1) * PG)
            dyg = dext[:, sl] * ps_ref[:, sl]
            dmix = _dot_nt(dyg, pw_ref[g])
            cnt = jnp.minimum(t + 1, w).astype(F32)
            s = _window_sum(dmix / cnt, w, rows, True)
            du_ref[:, sl] = (s[:tT] - dmix[:tT]).astype(du_ref.dtype)
            mb = mx_ref[:, sl]
            dpw_ref[g] += _dot_tn(mb, dyg[:tT])
            dps_ref[:, sl] += jnp.sum(dp_ref[:, sl] * _dot_nn(mb, pw_ref[g]), axis=0, keepdims=True)

    row = pl.BlockSpec((tT, PW), lambda i: (i, 0))
    return pl.pallas_call(
        body, name=name, grid=(n,),
        in_specs=[row, pl.BlockSpec((POOL_HALO, PW), lambda i: (jnp.minimum((i + 1) * hb, T // POOL_HALO - 1), 0)), row,
                  pl.BlockSpec((4, PG, PG), lambda i: (0, 0, 0)), pl.BlockSpec((1, PW), lambda i: (0, 0))],
        out_specs=[row, pl.BlockSpec((4, PG, PG), lambda i: (0, 0, 0)), pl.BlockSpec((1, PW), lambda i: (0, 0))],
        out_shape=[jax.ShapeDtypeStruct((T, PW), MXU), jax.ShapeDtypeStruct((4, PG, PG), F32),
                   jax.ShapeDtypeStruct((1, PW), F32)],
        compiler_params=_params("arbitrary"),
    )(dpool, dpool, mixed, pw, ps)


CONV_HALO = 8
CONV_K = 4


def _conv(x, cw_ref, cb_ref, sl):
    acc = cb_ref[:, sl] + cw_ref[CONV_K - 1:CONV_K, sl] * x
    for k in range(CONV_K - 1):
        acc = acc + cw_ref[k:k + 1, sl] * pltpu.roll(x, CONV_K - 1 - k, axis=0)
    return acc


def conv_fwd(xbc, cw, cb, *, name):
    T, CC = xbc.shape
    tT = _pick(T, (256, 128))
    hb = tT // CONV_HALO
    cs = _pick(CC, (512, 256, 128))

    def body(x_ref, xh_ref, cw_ref, cb_ref, o_ref):
        i = pl.program_id(0)
        for c0 in range(0, CC, cs):
            sl = slice(c0, c0 + cs)
            ext = jnp.concatenate([jnp.where(i > 0, xh_ref[:, sl], 0.0), x_ref[:, sl]], axis=0)
            xc = _conv(ext, cw_ref, cb_ref, sl)[CONV_HALO:]
            o_ref[:, sl] = xc * _sigmoid(xc)

    row = pl.BlockSpec((tT, CC), lambda i: (i, 0))
    return pl.pallas_call(
        body, name=name, grid=(T // tT,),
        in_specs=[row, pl.BlockSpec((CONV_HALO, CC), lambda i: (jnp.maximum(i * hb - 1, 0), 0)),
                  pl.BlockSpec((CONV_K, CC), lambda i: (0, 0)), pl.BlockSpec((1, CC), lambda i: (0, 0))],
        out_specs=row, out_shape=jax.ShapeDtypeStruct((T, CC), F32), compiler_params=_params("parallel"),
    )(xbc, xbc, cw, cb)


def conv_bwd(xbc, cw, cb, dxa, *, name):
    T, CC = xbc.shape
    tT = _pick(T, (256, 128))
    hb = tT // CONV_HALO
    n = T // tT
    rows = tT + 2 * CONV_HALO
    cs = _pick(CC, (512, 256, 128))

    def body(x_ref, xp_ref, xn_ref, cw_ref, cb_ref, d_ref, dn_ref, dx_ref, dcw_ref, dcb_ref):
        i = pl.program_id(0)

        @pl.when(i == 0)
        def _():
            dcw_ref[...] = jnp.zeros_like(dcw_ref)
            dcb_ref[...] = jnp.zeros_like(dcb_ref)

        r = lax.broadcasted_iota(jnp.int32, (rows, 1), 0)
        own = (r >= CONV_HALO) & (r < tT + CONV_HALO)
        for c0 in range(0, CC, cs):
            sl = slice(c0, c0 + cs)
            x = jnp.concatenate([jnp.where(i > 0, xp_ref[:, sl], 0.0), x_ref[:, sl],
                                 jnp.where(i < n - 1, xn_ref[:, sl], 0.0)], axis=0)
            da = jnp.concatenate([jnp.zeros((CONV_HALO, cs), F32), d_ref[:, sl],
                                  jnp.where(i < n - 1, dn_ref[:, sl], 0.0)], axis=0)
            xc = _conv(x, cw_ref, cb_ref, sl)
            sg = _sigmoid(xc)
            dxc = da * sg * (1.0 + xc * (1.0 - sg))
            acc = cw_ref[CONV_K - 1:CONV_K, sl] * dxc
            for k in range(CONV_K - 1):
                acc = acc + cw_ref[k:k + 1, sl] * pltpu.roll(dxc, rows - (CONV_K - 1 - k), axis=0)
            dx_ref[:, sl] = acc[CONV_HALO:tT + CONV_HALO].astype(dx_ref.dtype)
            down = jnp.where(own, dxc, 0.0)
            dcb_ref[:, sl] += jnp.sum(down, axis=0, keepdims=True)
            dcw_ref[CONV_K - 1:CONV_K, sl] += jnp.sum(down * x, axis=0, keepdims=True)
            for k in range(CONV_K - 1):
                dcw_ref[k:k + 1, sl] += jnp.sum(down * pltpu.roll(x, CONV_K - 1 - k, axis=0), axis=0, keepdims=True)

    row = pl.BlockSpec((tT, CC), lambda i: (i, 0))
    prev = pl.BlockSpec((CONV_HALO, CC), lambda i: (jnp.maximum(i * hb - 1, 0), 0))
    nxt = pl.BlockSpec((CONV_HALO, CC), lambda i: (jnp.minimum((i + 1) * hb, T // CONV_HALO - 1), 0))
    return pl.pallas_call(
        body, name=name, grid=(n,),
        in_specs=[row, prev, nxt, pl.BlockSpec((CONV_K, CC), lambda i: (0, 0)), pl.BlockSpec((1, CC), lambda i: (0, 0)),
                  row, nxt],
        out_specs=[row, pl.BlockSpec((CONV_K, CC), lambda i: (0, 0)), pl.BlockSpec((1, CC), lambda i: (0, 0))],
        out_shape=[jax.ShapeDtypeStruct((T, CC), MXU), jax.ShapeDtypeStruct((CONV_K, CC), F32),
                   jax.ShapeDtypeStruct((1, CC), F32)],
        compiler_params=_params("arbitrary"),
    )(xbc, xbc, xbc, cw, cb, dxa, dxa)


def _softplus(x):
    return jnp.maximum(x, 0.0) + jnp.log(1.0 + jnp.exp(-jnp.abs(x)))


def _dot_exact(a, b):
    return lax.dot_general(a, b, (((1,), (0,)), ((), ())), precision=lax.Precision.HIGHEST, preferred_element_type=F32)


def ssd_prep(raw_t, bias, alog, *, name):
    H, T = raw_t.shape
    B = C.blk
    tc = _pick(T, (4 * B, 2 * B, B))

    def body(r_ref, b_ref, al_ref, dt_ref, acs_ref):
        dt = _softplus(r_ref[...] + b_ref[...])
        dt_ref[...] = dt
        dta = dt * (-jnp.exp(al_ref[...]))
        upper = (lax.broadcasted_iota(jnp.int32, (B, B), 0) <= lax.broadcasted_iota(jnp.int32, (B, B), 1)).astype(F32)
        for j in range(tc // B):
            acs_ref[:, j * B:(j + 1) * B] = _dot_exact(dta[:, j * B:(j + 1) * B], upper)

    blk = pl.BlockSpec((H, tc), lambda i: (0, i))
    vec = pl.BlockSpec((H, 1), lambda i: (0, 0))
    return pl.pallas_call(
        body, name=name, grid=(T // tc,), in_specs=[blk, vec, vec], out_specs=[blk, blk],
        out_shape=[jax.ShapeDtypeStruct((H, T), F32)] * 2, compiler_params=_params("parallel"),
    )(raw_t, bias, alog)


def _pair(lo, arr, h0, rows=slice(None)):
    return jnp.where(lo, arr[rows, h0:h0 + 1], arr[rows, h0 + 1:h0 + 2])


def _decay(acs, acs_t, h, causal):
    return jnp.exp(jnp.where(causal, acs[:, h:h + 1] - acs_t[h:h + 1, :], NEG))


def ssd_fwd(xa, dt, acs, acs_t, dskip, *, name):
    T = xa.shape[0]
    B, DI, G, N, HG, P, H = C.blk, C.DI, C.G, C.N, C.HG, C.P, C.H
    assert P == 64 and HG % 2 == 0
    nc = T // B
    W = HG * P

    def body(xa_ref, dt_ref, acs_ref, acst_ref, ds_ref, y_ref, hp_ref, h_scr):
        @pl.when(pl.program_id(0) == 0)
        def _():
            h_scr[...] = jnp.zeros_like(h_scr)

        lo = lax.broadcasted_iota(jnp.int32, (1, 128), 1) < 64
        causal = lax.broadcasted_iota(jnp.int32, (B, B), 0) >= lax.broadcasted_iota(jnp.int32, (B, B), 1)
        dt, acs, acs_t, dsk = dt_ref[...], acs_ref[...], acst_ref[...], ds_ref[...]
        for g in range(G):
            bg = xa_ref[:, DI + g * N:DI + (g + 1) * N]
            cg = xa_ref[:, DI + (G + g) * N:DI + (G + g + 1) * N]
            cb = _dot_nt(cg, bg)
            hg = h_scr[g]
            hp_ref[0, g * N:(g + 1) * N, :] = hg
            yoff = _dot_nn(cg, hg)
            xws, decs = [], []
            for j in range(HG // 2):
                h0 = g * HG + 2 * j
                xsl = slice(h0 * P, (h0 + 2) * P)
                xp = xa_ref[:, xsl]
                ap = _pair(lo, acs, h0)
                alast = _pair(lo, acs, h0, slice(B - 1, B))
                xdt = xp * _pair(lo, dt, h0)
                ys = [_dot_nn(cb * _decay(acs, acs_t, h0 + half, causal), xdt) for half in range(2)]
                y_ref[:, xsl] = (jnp.where(lo, ys[0], ys[1]) + yoff[:, 2 * j * P:(2 * j + 2) * P] * jnp.exp(ap)
                                 + _pair(lo, dsk, h0) * xp)
                xws.append(xdt * jnp.exp(alast - ap))
                decs.append(jnp.exp(alast))
            h_scr[g] = hg * jnp.concatenate(decs, axis=1) + _dot_tn(bg, jnp.concatenate(xws, axis=1))

    tok = lambda w: pl.BlockSpec((B, w), lambda c: (c, 0))
    return pl.pallas_call(
        body, name=name, grid=(nc,),
        in_specs=[tok(C.CC), tok(H), tok(H), pl.BlockSpec((H, B), lambda c: (0, c)), pl.BlockSpec((1, H), lambda c: (0, 0))],
        out_specs=[tok(DI), pl.BlockSpec((1, G * N, W), lambda c: (c, 0, 0))],
        out_shape=[jax.ShapeDtypeStruct((T, DI), F32), jax.ShapeDtypeStruct((nc, G * N, W), F32)],
        scratch_shapes=[pltpu.VMEM((G, N, W), F32)],
        compiler_params=_params("arbitrary"),
    )(xa, dt, acs, acs_t, dskip)


def ssd_bwd(xa, dt, acs, acs_t, dskip, hprev, dy, *, name):
    T = xa.shape[0]
    B, DI, G, N, HG, P, H = C.blk, C.DI, C.G, C.N, C.HG, C.P, C.H
    nc = T // B
    W = HG * P

    def body(xa_ref, dt_ref, acs_ref, acst_ref, ds_ref, hp_ref, dy_ref, dxa_ref, ddt_ref, dacs_ref, dd_ref, dh_scr):
        @pl.when(pl.program_id(0) == 0)
        def _():
            dh_scr[...] = jnp.zeros_like(dh_scr)
            dd_ref[...] = jnp.zeros_like(dd_ref)

        lo = lax.broadcasted_iota(jnp.int32, (1, 128), 1) < 64
        hi = jnp.logical_not(lo)
        causal = lax.broadcasted_iota(jnp.int32, (B, B), 0) >= lax.broadcasted_iota(jnp.int32, (B, B), 1)
        hlane = lax.broadcasted_iota(jnp.int32, (1, H), 1)
        hsub = lax.broadcasted_iota(jnp.int32, (B, 1), 0)
        lastrow = lax.broadcasted_iota(jnp.int32, (B, 1), 0) == B - 1
        dt, acs, acs_t, dsk = dt_ref[...], acs_ref[...], acst_ref[...], ds_ref[...]
        d_acs = jnp.zeros((B, H), F32)
        d_acs_t = jnp.zeros((B, B), F32)
        d_dt = jnp.zeros((B, H), F32)
        d_d = jnp.zeros((1, H), F32)

        def rsum(v):
            return (jnp.sum(jnp.where(lo, v, 0.0), axis=1, keepdims=True),
                    jnp.sum(jnp.where(hi, v, 0.0), axis=1, keepdims=True))

        for g in range(G):
            bsl = slice(DI + g * N, DI + (g + 1) * N)
            csl = slice(DI + (G + g) * N, DI + (G + g + 1) * N)
            bg, cg = xa_ref[:, bsl], xa_ref[:, csl]
            cb = _dot_nt(cg, bg)
            hg = hp_ref[0, g * N:(g + 1) * N, :]
            dhg = dh_scr[g]
            yoff = _dot_nn(cg, hg)
            bds = _dot_nn(bg, dhg)
            d_cb = jnp.zeros((B, B), F32)
            dyes, xws, decs = [], [], []
            for j in range(HG // 2):
                h0 = g * HG + 2 * j
                xsl = slice(h0 * P, (h0 + 2) * P)
                psl = slice(2 * j * P, (2 * j + 2) * P)
                xp, dyp = xa_ref[:, xsl], dy_ref[:, xsl]
                dtp = _pair(lo, dt, h0)
                ap = _pair(lo, acs, h0)
                alast = _pair(lo, acs, h0, slice(B - 1, B))
                ea, ew, el = jnp.exp(ap), jnp.exp(alast - ap), jnp.exp(alast)
                xdt = xp * dtp
                halves = []
                for half in range(2):
                    h = h0 + half
                    lm = _decay(acs, acs_t, h, causal)
                    m = cb * lm
                    d_m = _dot_nt(jnp.where(lo if half == 0 else hi, dyp, 0.0), xdt)
                    d_cb = d_cb + d_m * lm
                    wgt = d_m * m
                    d_acs = d_acs + jnp.where(hlane == h, jnp.sum(wgt, axis=1, keepdims=True), 0.0)
                    d_acs_t = d_acs_t + jnp.where(hsub == h, jnp.sum(wgt, axis=0, keepdims=True), 0.0)
                    halves.append(_dot_tn(m, dyp))
                bdp = bds[:, psl]
                dxdt = jnp.where(lo, halves[0], halves[1]) + ew * bdp
                dxa_ref[:, xsl] = dtp * dxdt + _pair(lo, dsk, h0) * dyp
                xw = xdt * ew
                terms_dt = rsum(dxdt * xp)
                terms_dd = rsum(dyp * xp)
                terms_off = rsum(dyp * (ea * yoff[:, psl]))
                terms_e = rsum(xw * bdp)
                terms_h = rsum(hg[:, psl] * dhg[:, psl])
                for half in range(2):
                    h = h0 + half
                    sel = hlane == h
                    d_dt = d_dt + jnp.where(sel, terms_dt[half], 0.0)
                    d_d = d_d + jnp.where(sel, jnp.sum(terms_dd[half], axis=0, keepdims=True), 0.0)
                    e_last = jnp.sum(jnp.where(lo if half == 0 else hi, el, 0.0), axis=1, keepdims=True) * (1.0 / P)
                    d_last = (jnp.sum(terms_e[half], axis=0, keepdims=True)
                              + e_last * jnp.sum(terms_h[half], axis=0, keepdims=True))
                    d_acs = d_acs + jnp.where(sel, terms_off[half] - terms_e[half] + jnp.where(lastrow, d_last, 0.0), 0.0)
                dyes.append(dyp * ea)
                xws.append(xw)
                decs.append(el)
            dye = jnp.concatenate(dyes, axis=1)
            xwc = jnp.concatenate(xws, axis=1)
            dxa_ref[:, csl] = _dot_nn(d_cb, bg) + _dot_nt(dye, hg)
            dxa_ref[:, bsl] = _dot_tn(d_cb, cg) + _dot_nt(xwc, dhg)
            dh_scr[g] = dhg * jnp.concatenate(decs, axis=1) + _dot_tn(cg, dye)
        ddt_ref[...] = d_dt
        dacs_ref[...] = d_acs - d_acs_t.T[:, :H]
        dd_ref[...] += d_d

    rev = lambda w: pl.BlockSpec((B, w), lambda c: (nc - 1 - c, 0))
    vec = pl.BlockSpec((1, H), lambda c: (0, 0))
    return pl.pallas_call(
        body, name=name, grid=(nc,),
        in_specs=[rev(C.CC), rev(H), rev(H), pl.BlockSpec((H, B), lambda c: (0, nc - 1 - c)), vec,
                  pl.BlockSpec((1, G * N, W), lambda c: (nc - 1 - c, 0, 0)), rev(DI)],
        out_specs=[rev(C.CC), rev(H), rev(H), vec],
        out_shape=[jax.ShapeDtypeStruct((T, C.CC), F32), jax.ShapeDtypeStruct((T, H), F32),
                   jax.ShapeDtypeStruct((T, H), F32), jax.ShapeDtypeStruct((1, H), F32)],
        scratch_shapes=[pltpu.VMEM((G, N, W), F32)],
        compiler_params=_params("arbitrary"),
    )(xa, dt, acs, acs_t, dskip, hprev, dy)


def ssd_post(ddt, dacs, dt, raw, bias, alog, *, name):
    T, H = ddt.shape
    B = C.blk
    tc = _pick(T, (4 * B, 2 * B, B))

    def body(ddt_ref, dacs_ref, dt_ref, raw_ref, b_ref, al_ref, draw_ref, db_ref, dal_ref):
        @pl.when(pl.program_id(0) == 0)
        def _():
            db_ref[...] = jnp.zeros_like(db_ref)
            dal_ref[...] = jnp.zeros_like(dal_ref)

        a = -jnp.exp(al_ref[...])
        lower = (lax.broadcasted_iota(jnp.int32, (B, B), 0) <= lax.broadcasted_iota(jnp.int32, (B, B), 1)).astype(F32)
        for j in range(tc // B):
            sl = slice(j * B, (j + 1) * B)
            rc = _dot_exact(lower, dacs_ref[sl, :])
            dtv = dt_ref[sl, :]
            draw = (ddt_ref[sl, :] + a * rc) * _sigmoid(raw_ref[sl, :] + b_ref[...])
            draw_ref[sl, :] = draw
            db_ref[...] += jnp.sum(draw, axis=0, keepdims=True)
            dal_ref[...] += jnp.sum(dtv * rc, axis=0, keepdims=True) * a

    blk = pl.BlockSpec((tc, H), lambda i: (i, 0))
    vec = pl.BlockSpec((1, H), lambda i: (0, 0))
    return pl.pallas_call(
        body, name=name, grid=(T // tc,), in_specs=[blk, blk, blk, blk, vec, vec], out_specs=[blk, vec, vec],
        out_shape=[jax.ShapeDtypeStruct((T, H), F32), jax.ShapeDtypeStruct((1, H), F32), jax.ShapeDtypeStruct((1, H), F32)],
        compiler_params=_params("arbitrary"),
    )(ddt, dacs, dt, raw, bias, alog)


MESH = pl.DeviceIdType.MESH
PACK_W = 1024
ANY = pl.BlockSpec(memory_space=pl.ANY)


def _place():
    return lax.axis_index("x"), lax.axis_index("y"), lax.axis_index("c")


def _other_chips(x, y):
    return [(1 - x, y), (x, 1 - y), (1 - x, 1 - y)]


def _remote(src, dst, send_sems, recv_sems, k, to):
    return pltpu.make_async_remote_copy(src_ref=src, dst_ref=dst, send_sem=send_sems.at[k], recv_sem=recv_sems.at[k],
                                        device_id=to, device_id_type=MESH)


def _half(c, rows):
    return pl.ds(pl.multiple_of(c * (rows // 2), 16), rows // 2)


def gather_weights(shards, rowwise, *, name):
    n = len(shards)

    def body(*refs):
        ps, gs = refs[:n], refs[n:2 * n]
        send_sems, recv_sems = refs[2 * n:]
        x, y, c = _place()
        s = 2 * x + y
        sib = (x, y, 1 - c)
        chips = _other_chips(x, y)

        def dst(w, shard, layer=None):
            if rowwise[w]:
                return gs[w].at[:, shard] if layer is None else gs[w].at[layer, shard]
            return gs[w].at[shard] if layer is None else gs[w].at[shard, layer]

        copy = functools.partial(_remote, send_sems=send_sems, recv_sems=recv_sems)
        started = []
        for w in range(n):
            started.append(copy(ps[w], dst(w, s), k=w, to=sib))
        for w in range(n):
            for j, (px, py) in enumerate(chips):
                started.append(copy(ps[w].at[c], dst(w, s, c), k=n + 3 * w + j, to=(px, py, c)))
        for cp in started:
            cp.start()
        passed = []
        for w in range(n):
            for j, (px, py) in enumerate(chips):
                there = dst(w, 2 * px + py, c)
                copy(ps[w].at[c], there, k=n + 3 * w + j, to=(px, py, c)).wait_recv()
                fw = copy(there, there, k=4 * n + 3 * w + j, to=sib)
                fw.start()
                passed.append(fw)
        for w in range(n):
            copy(ps[w], dst(w, s), k=w, to=sib).wait_recv()
            for j, (px, py) in enumerate(chips):
                there = dst(w, 2 * px + py, 1 - c)
                copy(there, there, k=4 * n + 3 * w + j, to=sib).wait_recv()
        for cp in started + passed:
            cp.wait_send()

    def full(p, rw):
        L = p.shape[0]
        return jax.ShapeDtypeStruct((L, 4) + p.shape[1:] if rw else (4,) + p.shape, p.dtype)

    return pl.pallas_call(
        body, name=name, in_specs=[ANY] * n, out_specs=[ANY] * n, out_shape=[full(p, rw) for p, rw in zip(shards, rowwise)],
        scratch_shapes=[pltpu.SemaphoreType.DMA((7 * n,)), pltpu.SemaphoreType.DMA((7 * n,))],
    )(*shards)


def swap_halves(items, *, name):
    n = len(items)

    def body(*refs):
        gs, rs = refs[:n], refs[n:2 * n]
        send_sems, recv_sems = refs[2 * n:]
        x, y, c = _place()
        cps = [_remote(g.at[:, _half(1 - c, g.shape[1])], r, send_sems, recv_sems, k, (x, y, 1 - c))
               for k, (g, r) in enumerate(zip(gs, rs))]
        for cp in cps:
            cp.start()
        for cp in cps:
            cp.wait()

    return pl.pallas_call(
        body, name=name, in_specs=[ANY] * n, out_specs=[ANY] * n,
        out_shape=[jax.ShapeDtypeStruct((4, g.shape[1] // 2, g.shape[2]), g.dtype) for g in items],
        scratch_shapes=[pltpu.SemaphoreType.DMA((n,)), pltpu.SemaphoreType.DMA((n,))],
    )(*items)


def scatter_chips(items, *, name):
    n = len(items)

    def body(*refs):
        as_, rs = refs[:n], refs[n:2 * n]
        send_sems, recv_sems = refs[2 * n:]
        x, y, c = _place()
        cps = [_remote(a.at[2 * px + py], r.at[j], send_sems, recv_sems, 3 * k + j, (px, py, c))
               for k, (a, r) in enumerate(zip(as_, rs)) for j, (px, py) in enumerate(_other_chips(x, y))]
        for cp in cps:
            cp.start()
        for cp in cps:
            cp.wait()

    return pl.pallas_call(
        body, name=name, in_specs=[ANY] * n, out_specs=[ANY] * n,
        out_shape=[jax.ShapeDtypeStruct((3,) + a.shape[1:], a.dtype) for a in items],
        scratch_shapes=[pltpu.SemaphoreType.DMA((3 * n,)), pltpu.SemaphoreType.DMA((3 * n,))],
    )(*items)


def join_halves(items, *, name):
    n = len(items)

    def body(*refs):
        rs, outs = refs[:n], refs[n:2 * n]
        send_sems, recv_sems = refs[2 * n:]
        x, y, c = _place()
        sib = (x, y, 1 - c)
        cps = []
        for k, (r, o) in enumerate(zip(rs, outs)):
            mine = _half(c, r.shape[0])
            cps.append(_remote(r.at[mine], o.at[mine], send_sems, recv_sems, k, sib))
            cps[-1].start()
        for k, (r, o) in enumerate(zip(rs, outs)):
            theirs = _half(1 - c, r.shape[0])
            _remote(r.at[theirs], o.at[theirs], send_sems, recv_sems, k, sib).wait_recv()
        for cp in cps:
            cp.wait_send()

    return pl.pallas_call(
        body, name=name, in_specs=[ANY] * n, out_specs=[ANY] * n,
        out_shape=[jax.ShapeDtypeStruct(r.shape, r.dtype) for r in items],
        input_output_aliases={k: k for k in range(n)},
        scratch_shapes=[pltpu.SemaphoreType.DMA((n,)), pltpu.SemaphoreType.DMA((n,))],
    )(*items)


def gather_all(v, *, name, total):
    rows, W = v.shape

    def body(v_ref, o_ref, *scr):
        buf = scr[0] if total else o_ref
        send_sems, recv_sems = scr[-2], scr[-1]
        x, y, c = _place()
        me = 4 * x + 2 * y + c
        flips = [(k >> 2 & 1, k >> 1 & 1, k & 1) for k in range(1, 8)]
        peers = [((1 - x) if fx else x, (1 - y) if fy else y, (1 - c) if fc else c) for fx, fy, fc in flips]
        out = []
        for k, peer in enumerate(peers):
            cp = pltpu.make_async_remote_copy(src_ref=v_ref, dst_ref=buf.at[me], send_sem=send_sems.at[k],
                                              recv_sem=recv_sems.at[k], device_id=peer, device_id_type=MESH)
            cp.start()
            out.append(cp)
        buf[me] = v_ref[...]
        for k, (px, py, pc) in enumerate(peers):
            pltpu.make_async_remote_copy(src_ref=v_ref, dst_ref=buf.at[4 * px + 2 * py + pc], send_sem=send_sems.at[k],
                                         recv_sem=recv_sems.at[k], device_id=(px, py, pc), device_id_type=MESH).wait_recv()
        for cp in out:
            cp.wait_send()
        if total:
            acc = buf[0]
            for d in range(1, 8):
                acc = acc + buf[d]
            o_ref[...] = acc

    vm = pl.BlockSpec(memory_space=pltpu.VMEM)
    return pl.pallas_call(
        body, name=name, in_specs=[vm], out_specs=vm,
        out_shape=jax.ShapeDtypeStruct((rows, W) if total else (8, rows, W), F32),
        scratch_shapes=([pltpu.VMEM((8, rows, W), F32)] if total else [])
        + [pltpu.SemaphoreType.DMA((7,)), pltpu.SemaphoreType.DMA((7,))],
    )(v)


def pair_sum(g, r, c, *, name):
    _, R, W = g.shape
    Rh = R // 2
    tr = _pick(Rh, (512, 256, 128, 64, 32, 16))
    nb = Rh // tr

    def body(c_ref, g_ref, r_ref, o_ref):
        o_ref[...] = (g_ref[...].astype(F32) + r_ref[...].astype(F32)).astype(o_ref.dtype)

    grid_spec = pltpu.PrefetchScalarGridSpec(
        num_scalar_prefetch=1, grid=(4, nb),
        in_specs=[pl.BlockSpec((1, tr, W), lambda s, i, c_ref: (s, c_ref[0] * nb + i, 0)),
                  pl.BlockSpec((1, tr, W), lambda s, i, c_ref: (s, i, 0))],
        out_specs=pl.BlockSpec((1, tr, W), lambda s, i, c_ref: (s, i, 0)))
    return pl.pallas_call(
        body, name=name, grid_spec=grid_spec, out_shape=jax.ShapeDtypeStruct((4, Rh, W), g.dtype),
        compiler_params=_params("parallel", "parallel"),
    )(c, g, r)


def chip_sum(g, r1, r2, sc, *, name):
    _, R, W = g.shape
    Rh = R // 2
    tr = _pick(Rh, (512, 256, 128, 64, 32, 16))
    nb = Rh // tr

    def body(sc_ref, g_ref, r1_ref, a_ref, b_ref, c_ref, o_ref):
        acc = g_ref[0].astype(F32) + r1_ref[0].astype(F32)
        for ref in (a_ref, b_ref, c_ref):
            acc = acc + ref[0].astype(F32)
        o_ref[...] = acc

    grid_spec = pltpu.PrefetchScalarGridSpec(
        num_scalar_prefetch=1, grid=(nb,),
        in_specs=[pl.BlockSpec((1, tr, W), lambda i, sc_ref: (sc_ref[0], sc_ref[1] * nb + i, 0)),
                  pl.BlockSpec((1, tr, W), lambda i, sc_ref: (sc_ref[0], i, 0))]
        + [pl.BlockSpec((1, tr, W), lambda i, sc_ref, j=j: (j, i, 0)) for j in range(3)],
        out_specs=pl.BlockSpec((tr, W), lambda i, sc_ref: (sc_ref[1] * nb + i, 0)))
    return pl.pallas_call(
        body, name=name, grid_spec=grid_spec, out_shape=jax.ShapeDtypeStruct((R, W), F32),
        compiler_params=_params("parallel"),
    )(sc, g, r1, r2, r2, r2)


BIG = (("w_in", False), ("pool_w", False), ("w_attn_br", False), ("w_pool_br", False), ("w_ssm_br", True),
       ("w_out", True), ("w_gate_up", False), ("w_down", True))


def _join(piece, axis):
    t = jnp.moveaxis(piece, 0, axis)
    shp = t.shape
    return t.reshape(shp[:axis] + (shp[axis] * shp[axis + 1],) + shp[axis + 2:])


def _seg_bounds():
    aw, kw, _, pw, di, cc, h, gd = C.in_widths
    o = [0, aw + 2 * kw]
    for wdt in (pw, di, cc, h, gd):
        o.append(o[-1] + wdt)
    return o


def _in_segments(blocks):
    w_in = jnp.concatenate([blocks[s] for s in range(4)], axis=1)
    o = _seg_bounds()
    segs = [w_in[:, o[i]:o[i + 1]] for i in range(6)]
    segs[4] = jnp.pad(segs[4], ((0, 0), (0, 128 - C.H)))
    return segs


def _layer_fwd(x, p, tag):
    nm = lambda s: f"{s}_{tag}"
    H = C.H
    h = rms_fwd(x, p["ln1_w"], name=nm("rms1"))
    wq, wu, wz, wx, wd, wg = p["in_segs"]
    qkv = matmul(h, wq, name=nm("mm_qkv"), out_dtype=MXU)
    u = matmul(h, wu, name=nm("mm_u"))
    z = matmul(h, wz, name=nm("mm_z"))
    xbc = matmul(h, wx, name=nm("mm_xbc"))
    dtp = matmul(h, wd, name=nm("mm_dt"))
    gl = matmul(h, wg, name=nm("mm_gate"))
    att = attn_fwd(qkv, p["attn_sink"], name=nm("attn_fwd"))
    pool, mixed = pool_fwd(u, p["pool_w"], p["pool_scale"], name=nm("pool_fwd"))
    xa = conv_fwd(xbc, p["conv_w"], p["conv_b"], name=nm("conv_fwd"))
    raw = dtp[:, :H]
    dt_t, acs_t = ssd_prep(raw.T, p["dt_bias"].T, p["a_log"].T, name=nm("ssd_prep"))
    dt, acs = dt_t.T, acs_t.T
    y, hprev = ssd_fwd(xa, dt, acs, acs_t, p["d_skip"], name=nm("ssd_fwd"))
    ssm = gnorm_fwd(y, z, p["ssm_norm_w"], name=nm("gnorm_fwd"))
    abr = matmul(att, name=nm("mm_abr"), **p["w_attn_br"])
    pbr = matmul(pool, name=nm("mm_pbr"), **p["w_pool_br"])
    sbr = matmul(ssm, name=nm("mm_sbr"), **p["w_ssm_br"])
    merged = merge_fwd(gl, abr, pbr, sbr, name=nm("merge_fwd"))
    xm = matmul(merged, add=x, name=nm("mm_out"), **p["w_out"])
    h2 = rms_fwd(xm, p["ln2_w"], name=nm("rms2"))
    gu = matmul(h2, name=nm("mm_gu"), **p["w_gate_up"])
    act = swiglu_fwd(gu, name=nm("swiglu_fwd"))
    xo = matmul(act, add=xm, name=nm("mm_down"), **p["w_down"])
    saved = dict(x=x, h=h, qkv=qkv, z=z, xbc=xbc, raw=raw, gl=gl, att=att, pool=pool, mixed=mixed, xa=xa, dt=dt, acs=acs,
                 acs_t=acs_t, y=y, hprev=hprev, ssm=ssm, abr=abr, pbr=pbr, sbr=sbr, merged=merged, xm=xm, h2=h2, gu=gu,
                 act=act)
    return xo, saved


def _layer_bwd(dxo, dxo_m, p, s, tag):
    nm = lambda t: f"{t}_{tag}"
    H = C.H
    g = {}

    def rows4(t):
        return t.reshape(4, t.shape[0] // 4, t.shape[1])

    g["w_down"] = rows4(matmul(s["act"], dxo_m, ta=True, out_dtype=MXU, name=nm("mmg_down")))
    dact = matmul(dxo_m, tb=True, name=nm("mmb_down"), **p["w_down"])
    dgu = swiglu_bwd(s["gu"], dact, name=nm("swiglu_bwd"))
    g["w_gate_up"] = matmul(s["h2"], dgu, ta=True, out_dtype=MXU, out_cols=True, name=nm("mmg_gu"))
    dh2 = matmul(dgu, tb=True, name=nm("mmb_gu"), **p["w_gate_up"])
    dxm, dxm_m, g["ln2_w"] = rms_bwd(s["xm"], p["ln2_w"], dh2, dxo, name=nm("rms2_bwd"))
    g["w_out"] = rows4(matmul(s["merged"], dxm_m, ta=True, out_dtype=MXU, name=nm("mmg_out")))
    dmerged = matmul(dxm_m, tb=True, name=nm("mmb_out"), **p["w_out"])
    dabr, dpbr, dsbr, dgl = merge_bwd(s["gl"], s["abr"], s["pbr"], s["sbr"], dmerged, name=nm("merge_bwd"))
    g["w_attn_br"] = matmul(s["att"], dabr, ta=True, out_dtype=MXU, out_cols=True, name=nm("mmg_abr"))
    g["w_pool_br"] = matmul(s["pool"], dpbr, ta=True, out_dtype=MXU, out_cols=True, name=nm("mmg_pbr"))
    g["w_ssm_br"] = rows4(matmul(s["ssm"], dsbr, ta=True, out_dtype=MXU, name=nm("mmg_sbr")))
    datt = matmul(dabr, tb=True, name=nm("mmb_abr"), **p["w_attn_br"])
    dpool = matmul(dpbr, tb=True, name=nm("mmb_pbr"), **p["w_pool_br"])
    dssm = matmul(dsbr, tb=True, name=nm("mmb_sbr"), **p["w_ssm_br"])
    dq, dk, dv, g["attn_sink"] = attn_bwd(s["qkv"], p["attn_sink"], datt, name=nm("attn_bwd"))
    dqkv = jnp.concatenate([dq, dk, dv], axis=1)
    du, dpw, g["pool_scale"] = pool_bwd(dpool, s["mixed"], p["pool_w"], p["pool_scale"], name=nm("pool_bwd"))
    pg = dpw.shape[1] // 4
    g["pool_w"] = jnp.moveaxis(dpw.reshape(4, 4, pg, dpw.shape[2]), 1, 0).reshape(4, 4 * pg, dpw.shape[2]).astype(MXU)
    dy, dz, g["ssm_norm_w"] = gnorm_bwd(s["y"], s["z"], p["ssm_norm_w"], dssm, name=nm("gnorm_bwd"))
    dxa, ddt, dacs, g["d_skip"] = ssd_bwd(s["xa"], s["dt"], s["acs"], s["acs_t"], p["d_skip"], s["hprev"], dy,
                                          name=nm("ssd_bwd"))
    draw, g["dt_bias"], g["a_log"] = ssd_post(ddt, dacs, s["dt"], s["raw"], p["dt_bias"], p["a_log"], name=nm("ssd_post"))
    dxbc, g["conv_w"], g["conv_b"] = conv_bwd(s["xbc"], p["conv_w"], p["conv_b"], dxa, name=nm("conv_bwd"))
    ddtp = jnp.pad(draw, ((0, 0), (0, 128 - H))).astype(MXU)
    dsegs = (dqkv, du, dz, dxbc, ddtp, dgl)
    dh = None
    gsegs = []
    for k, (d, w) in enumerate(zip(dsegs, p["in_segs"])):
        dh = matmul(d, w, tb=True, add=dh, name=nm(f"mmb_in{k}"))
        gsegs.append(matmul(s["h"], d, ta=True, out_dtype=MXU, name=nm(f"mmg_in{k}")))
    gsegs[4] = gsegs[4][:, :H]
    g_in = jnp.concatenate(gsegs, axis=1)
    nc = g_in.shape[1] // 4
    g["w_in"] = jnp.stack([g_in[:, k * nc:(k + 1) * nc] for k in range(4)])
    dx, dx_m, g["ln1_w"] = rms_bwd(s["x"], p["ln1_w"], dh, dxm, name=nm("rms1_bwd"))
    return dx, dx_m, g


SMALL = ("ln1_w", "attn_sink", "conv_w", "conv_b", "dt_bias", "a_log", "d_skip", "ssm_norm_w", "pool_scale", "ln2_w")
WEIGHTS = ("ln1_w", "w_in", "attn_sink", "conv_w", "conv_b", "dt_bias", "a_log", "d_skip", "ssm_norm_w", "pool_w",
           "pool_scale", "w_attn_br", "w_pool_br", "w_ssm_br", "w_out", "ln2_w", "w_gate_up", "w_down", "final_w")


def _step(x, loss_target, w, m, v):
    depth = C.depth
    xi, yi, ci = _place()
    shard = 2 * xi + yi

    gathered = gather_weights([w[n].astype(MXU) for n, _ in BIG], [rw for _, rw in BIG], name="gather_weights")
    full = dict(zip([n for n, _ in BIG], gathered))
    cw_all = gather_all(_pack_small([w["conv_w"].reshape(1, -1)]), name="gather_conv_w", total=False)
    cw_shards = cw_all[0::2].reshape(4, -1)[:, :w["conv_w"].size].reshape((4,) + w["conv_w"].shape)
    conv_w = _join(cw_shards, 2)

    layers = []
    for i in range(depth):
        p = {n: w[n][i][None] for n in SMALL if n != "conv_w"}
        p["conv_w"] = conv_w[i]
        p["in_segs"] = _in_segments(full["w_in"][:, i])
        pw = full["pool_w"][:, i]
        p["pool_w"] = _join(pw, 1)
        for n, rw in BIG[2:]:
            if rw:
                t = full[n]
                p[n] = dict(b=t.reshape(t.shape[0], 4 * t.shape[2], t.shape[3]), layer=i)
            else:
                p[n] = dict(b=full[n], layer=i, b_cols=True)
        layers.append(p)

    xs = x[0]
    saved = []
    for i in range(depth):
        xs, sv = _layer_fwd(xs, layers[i], f"l{i}")
        saved.append(sv)
    loss_part, dx, dx_m, g_final = final_loss(xs, w["final_w"][None], loss_target[0], name="final_loss")
    grads = [None] * depth
    for i in reversed(range(depth)):
        dx, dx_m, grads[i] = _layer_bwd(dx, dx_m, layers[i], saved[i], f"l{i}")

    keys = [(i, n) for i in range(depth) for n, _ in BIG]
    items = [grads[i][n] for i, n in keys]
    cvec = ci.astype(jnp.int32).reshape(1)
    scvec = jnp.stack([shard, ci]).astype(jnp.int32)
    sib = swap_halves(items, name="swap_halves")
    chip = [pair_sum(g, r, cvec, name=f"pair_sum_{n}_l{i}") for (i, n), g, r in zip(keys, items, sib)]
    others = scatter_chips(chip, name="scatter_chips")
    mine = [chip_sum(g, r, o, scvec, name=f"chip_sum_{n}_l{i}") for (i, n), g, r, o in zip(keys, items, sib, others)]
    reduced = dict(zip(keys, join_halves(mine, name="join_halves")))
    gout = {}

    small = [jnp.stack([grads[i][n].reshape(-1) for i in range(depth)]).reshape(1, -1) for n in SMALL]
    small += [g_final.reshape(1, -1), loss_part.reshape(1, -1)]
    tot = gather_all(_pack_small(small), name="sum_small", total=True)
    parts = _unpack_small(tot, [t.shape[1] for t in small])
    for n, t in zip(SMALL, parts):
        if n == "conv_w":
            cols = w[n].shape[2]
            gout[n] = lax.dynamic_slice_in_dim(t.reshape(depth, CONV_K, -1), shard * cols, cols, axis=2)
        else:
            gout[n] = t.reshape(w[n].shape)
    gout["final_w"] = parts[-2].reshape(w["final_w"].shape)
    loss = parts[-1].reshape(())

    upd = {}
    for n in WEIGHTS:
        if n in gout:
            upd[n] = (gout[n].reshape(w[n].shape),) + adamw(w[n], gout[n].reshape(w[n].shape), m[n], v[n], name=f"adamw_{n}")
        else:
            upd[n] = adamw_layers(w[n], [reduced[(i, n)] for i in range(depth)], m[n], v[n], name=f"adamw_{n}")
    return (loss, dx[None], *[upd[n][0] for n in WEIGHTS], *[upd[n][1] for n in WEIGHTS],
            *[upd[n][2] for n in WEIGHTS], *[upd[n][3] for n in WEIGHTS])


def _pack_small(parts):
    flat = jnp.concatenate(parts, axis=1)
    rows = -(-flat.shape[1] // PACK_W)
    rows = -(-rows // 8) * 8
    return jnp.pad(flat, ((0, 0), (0, rows * PACK_W - flat.shape[1]))).reshape(rows, PACK_W)


def _unpack_small(buf, sizes):
    flat = buf.reshape(-1)
    out, off = [], 0
    for n in sizes:
        out.append(flat[off:off + n])
        off += n
    return out


def kernel(x, ln1_w, w_in, attn_sink, conv_w, conv_b, dt_bias, a_log, d_skip, ssm_norm_w, pool_w, pool_scale, w_attn_br, w_pool_br, w_ssm_br, w_out, ln2_w, w_gate_up, w_down, final_w, loss_target, m_ln1_w, m_w_in, m_attn_sink, m_conv_w, m_conv_b, m_dt_bias, m_a_log, m_d_skip, m_ssm_norm_w, m_pool_w, m_pool_scale, m_w_attn_br, m_w_pool_br, m_w_ssm_br, m_w_out, m_ln2_w, m_w_gate_up, m_w_down, m_final_w, v_ln1_w, v_w_in, v_attn_sink, v_conv_w, v_conv_b, v_dt_bias, v_a_log, v_d_skip, v_ssm_norm_w, v_pool_w, v_pool_scale, v_w_attn_br, v_w_pool_br, v_w_ssm_br, v_w_out, v_ln2_w, v_w_gate_up, v_w_down, v_final_w):
    w = dict(ln1_w=ln1_w, w_in=w_in, attn_sink=attn_sink, conv_w=conv_w, conv_b=conv_b, dt_bias=dt_bias, a_log=a_log,
             d_skip=d_skip, ssm_norm_w=ssm_norm_w, pool_w=pool_w, pool_scale=pool_scale, w_attn_br=w_attn_br,
             w_pool_br=w_pool_br, w_ssm_br=w_ssm_br, w_out=w_out, ln2_w=ln2_w, w_gate_up=w_gate_up, w_down=w_down,
             final_w=final_w)
    m = dict(ln1_w=m_ln1_w, w_in=m_w_in, attn_sink=m_attn_sink, conv_w=m_conv_w, conv_b=m_conv_b, dt_bias=m_dt_bias,
             a_log=m_a_log, d_skip=m_d_skip, ssm_norm_w=m_ssm_norm_w, pool_w=m_pool_w, pool_scale=m_pool_scale,
             w_attn_br=m_w_attn_br, w_pool_br=m_w_pool_br, w_ssm_br=m_w_ssm_br, w_out=m_w_out, ln2_w=m_ln2_w,
             w_gate_up=m_w_gate_up, w_down=m_w_down, final_w=m_final_w)
    v = dict(ln1_w=v_ln1_w, w_in=v_w_in, attn_sink=v_attn_sink, conv_w=v_conv_w, conv_b=v_conv_b, dt_bias=v_dt_bias,
             a_log=v_a_log, d_skip=v_d_skip, ssm_norm_w=v_ssm_norm_w, pool_w=v_pool_w, pool_scale=v_pool_scale,
             w_attn_br=v_w_attn_br, w_pool_br=v_w_pool_br, w_ssm_br=v_w_ssm_br, w_out=v_w_out, ln2_w=v_ln2_w,
             w_gate_up=v_w_gate_up, w_down=v_w_down, final_w=v_final_w)
    return _step(x, loss_target, w, m, v)
```

```python
import functools

import jax
import jax.numpy as jnp
from jax import lax
from jax.experimental import pallas as pl
from jax.experimental.pallas import tpu as pltpu

F32 = jnp.float32
MXU = jnp.bfloat16
VMEM_LIMIT = 56 * 1024 * 1024
EPS = 1e-6
NEG = -1e30

ADAM_LR, ADAM_B1, ADAM_B2, ADAM_EPS, ADAM_WD, ADAM_STEP = 0.001, 0.9, 0.999, 1e-08, 0.01, 10


class Cfg:
    def __init__(self, d_model=2048, seq=8192, depth=2, q_heads=16, kv_heads=4, head_dim=64,
                 ssm_head_dim=64, ssm_groups=4, d_state=128):
        self.D, self.T, self.depth = d_model, seq, depth
        self.hd, self.qh, self.kvh = head_dim, q_heads, kv_heads
        self.AW, self.KW = q_heads * head_dim, kv_heads * head_dim
        self.blk = 128
        self.PW = d_model // 2
        self.PG = self.PW // 4
        self.DI = d_model
        self.P = ssm_head_dim
        self.H = self.DI // self.P
        self.G = ssm_groups
        self.HG = self.H // self.G
        self.N = d_state
        self.CC = self.DI + 2 * self.G * self.N
        self.F = -(-8 * d_model // (3 * 256)) * 256
        self.in_widths = (self.AW, self.KW, self.KW, self.PW, self.DI, self.CC, self.H, 3 * d_model)
        self.in_cols = sum(self.in_widths)


C = Cfg()
POOL_WINDOWS = (2, 4, 8, 16)


def _pick(n, cands):
    for c in cands:
        if n % c == 0:
            return c
    return n


def _params(*sem):
    return pltpu.CompilerParams(dimension_semantics=sem, vmem_limit_bytes=VMEM_LIMIT)


def _sigmoid(x):
    return 1.0 / (1.0 + jnp.exp(-x))


def _dot(a, b, dims):
    return lax.dot_general(a.astype(MXU), b.astype(MXU), (dims, ((), ())), preferred_element_type=F32)


def _dot_nn(a, b):
    return _dot(a, b, ((1,), (0,)))


def _dot_nt(a, b):
    return _dot(a, b, ((1,), (1,)))


def _dot_tn(a, b):
    return _dot(a, b, ((0,), (0,)))


MM_VMEM = 40 * 1024 * 1024
LANES = 128


def _tile(n, cap):
    best = None
    for d in range(LANES, min(n, cap) + 1, LANES):
        if n % d == 0:
            best = d
    return n if best is None else best


def _divisors(n, cap):
    ds = [d for d in range(LANES, min(n, cap) + 1, LANES) if n % d == 0]
    return ds or [n]


HBM_RATE, MXU_RATE, ACC_RATE, STEP_COST = 3.0e12, 0.9e15, 1.2e13, 0.35e-6
MXU_WIDTH = 256


def _choose_tiles(M, N, K, n_unit, k_unit, sa, sb, so, sadd, ta):
    best = None
    for tm in sorted({_tile(M, 1024), _tile(M, 512)}):
        for tk in _divisors(k_unit, 4096):
            for tn in _divisors(n_unit, 2048):
                nk = K // tk
                vmem = 2 * (tm * tk * sa + tk * tn * sb + tm * tn * (so + sadd)) + (tm * tn * 4 if nk > 1 else 0)
                if vmem > MM_VMEM:
                    continue
                steps = (M // tm) * (N // tn) * nk
                a_bytes = M * K * sa * (N // tn if (nk > 1 or ta) else 1)
                b_bytes = K * N * sb * (M // tm)
                hbm = (a_bytes + b_bytes + M * N * (so + sadd)) / HBM_RATE
                fill = (tn / (-(-tn // MXU_WIDTH) * MXU_WIDTH)) * (tk / (-(-tk // MXU_WIDTH) * MXU_WIDTH))
                mxu = 2.0 * M * N * K / (MXU_RATE * fill)
                acc = steps * tm * tn * 8 / ACC_RATE if nk > 1 else 0.0
                cost = max(hbm, mxu) + acc + steps * STEP_COST
                if best is None or cost < best[0]:
                    best = (cost, tm, tn, tk)
    assert best is not None, (M, N, K)
    return best[1:]


def matmul(a, b, *, name, ta=False, tb=False, out_dtype=F32, add=None, layer=None, b_cols=False, out_cols=False):
    M, K = (a.shape[1], a.shape[0]) if ta else a.shape
    rows, cols = b.shape[-2], (4 if b_cols else 1) * b.shape[-1]
    N, bk = (rows, cols) if tb else (cols, rows)
    assert K == bk, (a.shape, b.shape)
    n_unit = N // 4 if (out_cols or (b_cols and not tb)) else N
    k_unit = K // 4 if (b_cols and tb) else K
    sa, sb, so = a.dtype.itemsize, b.dtype.itemsize, jnp.dtype(out_dtype).itemsize
    sadd = add.dtype.itemsize if add is not None else 0
    tm, tn, tk = _choose_tiles(M, N, K, n_unit, k_unit, sa, sb, so, sadd, ta)
    nk = K // tk
    a_spec = pl.BlockSpec((tk, tm), lambda i, j, k: (k, i)) if ta else pl.BlockSpec((tm, tk), lambda i, j, k: (i, k))
    if b_cols and tb:
        per = k_unit // tk
        b_spec = pl.BlockSpec((None, None, tn, tk), lambda i, j, k: (k // per, layer, j, k % per))
    elif b_cols:
        per = n_unit // tn
        b_spec = pl.BlockSpec((None, None, tk, tn), lambda i, j, k: (j // per, layer, k, j % per))
    elif layer is not None:
        b_spec = (pl.BlockSpec((None, tn, tk), lambda i, j, k: (layer, j, k)) if tb
                  else pl.BlockSpec((None, tk, tn), lambda i, j, k: (layer, k, j)))
    else:
        b_spec = pl.BlockSpec((tn, tk), lambda i, j, k: (j, k)) if tb else pl.BlockSpec((tk, tn), lambda i, j, k: (k, j))
    if out_cols:
        per = n_unit // tn
        o_spec = pl.BlockSpec((None, tm, tn), lambda i, j, k: (j // per, i, j % per))
        out_shape = jax.ShapeDtypeStruct((4, M, N // 4), out_dtype)
    else:
        o_spec = pl.BlockSpec((tm, tn), lambda i, j, k: (i, j))
        out_shape = jax.ShapeDtypeStruct((M, N), out_dtype)
    dims = ((0 if ta else 1,), (1 if tb else 0,))
    has_add = add is not None

    def body(*refs):
        a_ref, b_ref = refs[0], refs[1]
        add_ref = refs[2] if has_add else None
        o_ref = refs[3] if has_add else refs[2]
        part = _dot(a_ref[...], b_ref[...], dims)

        def finish(total):
            if has_add:
                total = total + add_ref[...].astype(F32)
            o_ref[...] = total.astype(o_ref.dtype)

        if nk == 1:
            finish(part)
        else:
            acc_ref = refs[-1]
            k = pl.program_id(2)

            @pl.when(k == 0)
            def _():
                acc_ref[...] = part

            @pl.when(k > 0)
            def _():
                acc_ref[...] += part

            @pl.when(k == nk - 1)
            def _():
                finish(acc_ref[...])

    in_specs = [a_spec, b_spec] + ([o_spec] if has_add else [])
    args = (a, b) + ((add,) if has_add else ())
    return pl.pallas_call(
        body, name=name, grid=(M // tm, N // tn, nk), in_specs=in_specs, out_specs=o_spec, out_shape=out_shape,
        scratch_shapes=[pltpu.VMEM((tm, tn), F32)] if nk > 1 else [],
        compiler_params=_params("parallel", "parallel", "arbitrary"),
    )(*args)


def _row_tile(t):
    return _pick(t, (256, 128, 64, 32, 16, 8))


def rms_fwd(x, w, *, name):
    T, D = x.shape
    tr = _row_tile(T)

    def body(x_ref, w_ref, o_ref):
        xv = x_ref[...]
        r = lax.rsqrt(jnp.mean(xv * xv, axis=-1, keepdims=True) + EPS)
        o_ref[...] = (xv * r * w_ref[...]).astype(o_ref.dtype)

    row = pl.BlockSpec((tr, D), lambda i: (i, 0))
    return pl.pallas_call(
        body, name=name, grid=(T // tr,), in_specs=[row, pl.BlockSpec((1, D), lambda i: (0, 0))], out_specs=row,
        out_shape=jax.ShapeDtypeStruct((T, D), MXU), compiler_params=_params("parallel"),
    )(x, w)


def rms_bwd(x, w, dh, dres, *, name):
    T, D = x.shape
    tr = _row_tile(T)

    def body(x_ref, w_ref, dh_ref, dres_ref, dx_ref, dxm_ref, dw_ref):
        xv = x_ref[...]
        dhv = dh_ref[...].astype(F32)
        r = lax.rsqrt(jnp.mean(xv * xv, axis=-1, keepdims=True) + EPS)
        g = dhv * w_ref[...]
        dot = jnp.mean(g * xv, axis=-1, keepdims=True)
        dx = dres_ref[...] + r * g - xv * (r * r * r * dot)
        dx_ref[...] = dx
        dxm_ref[...] = dx.astype(dxm_ref.dtype)
        part = jnp.sum(dhv * xv * r, axis=0, keepdims=True)

        @pl.when(pl.program_id(0) == 0)
        def _():
            dw_ref[...] = part

        @pl.when(pl.program_id(0) > 0)
        def _():
            dw_ref[...] += part

    row = pl.BlockSpec((tr, D), lambda i: (i, 0))
    vec = pl.BlockSpec((1, D), lambda i: (0, 0))
    return pl.pallas_call(
        body, name=name, grid=(T // tr,), in_specs=[row, vec, row, row], out_specs=[row, row, vec],
        out_shape=[jax.ShapeDtypeStruct((T, D), F32), jax.ShapeDtypeStruct((T, D), MXU),
                   jax.ShapeDtypeStruct((1, D), F32)],
        compiler_params=_params("arbitrary"),
    )(x, w, dh, dres)


def final_loss(x, w, target, *, name):
    T, D = x.shape
    tr = _row_tile(T)

    def body(x_ref, w_ref, t_ref, loss_ref, dx_ref, dxm_ref, dw_ref):
        xv = x_ref[...]
        wv = w_ref[...]
        r = lax.rsqrt(jnp.mean(xv * xv, axis=-1, keepdims=True) + EPS)
        err = xv * r * wv - t_ref[...]
        lpart = 0.5 * jnp.sum(jnp.mean(err * err, axis=-1, keepdims=True), axis=0, keepdims=True)
        dy = err * (1.0 / D)
        g = dy * wv
        dot = jnp.mean(g * xv, axis=-1, keepdims=True)
        dx = r * g - xv * (r * r * r * dot)
        dx_ref[...] = dx
        dxm_ref[...] = dx.astype(dxm_ref.dtype)
        part = jnp.sum(dy * xv * r, axis=0, keepdims=True)

        @pl.when(pl.program_id(0) == 0)
        def _():
            dw_ref[...] = part
            loss_ref[...] = lpart

        @pl.when(pl.program_id(0) > 0)
        def _():
            dw_ref[...] += part
            loss_ref[...] += lpart

    row = pl.BlockSpec((tr, D), lambda i: (i, 0))
    vec = pl.BlockSpec((1, D), lambda i: (0, 0))
    one = pl.BlockSpec((1, 1), lambda i: (0, 0))
    return pl.pallas_call(
        body, name=name, grid=(T // tr,), in_specs=[row, vec, row], out_specs=[one, row, row, vec],
        out_shape=[jax.ShapeDtypeStruct((1, 1), F32), jax.ShapeDtypeStruct((T, D), F32),
                   jax.ShapeDtypeStruct((T, D), MXU), jax.ShapeDtypeStruct((1, D), F32)],
        compiler_params=_params("arbitrary"),
    )(x, w, target)


def swiglu_fwd(gu, *, name):
    T, F2 = gu.shape
    F = F2 // 2
    tr = _row_tile(T)

    def body(g_ref, u_ref, o_ref):
        g = g_ref[...]
        o_ref[...] = (g * _sigmoid(g) * u_ref[...]).astype(o_ref.dtype)

    return pl.pallas_call(
        body, name=name, grid=(T // tr,),
        in_specs=[pl.BlockSpec((tr, F), lambda i: (i, 0)), pl.BlockSpec((tr, F), lambda i: (i, 1))],
        out_specs=pl.BlockSpec((tr, F), lambda i: (i, 0)),
        out_shape=jax.ShapeDtypeStruct((T, F), MXU), compiler_params=_params("parallel"),
    )(gu, gu)


def swiglu_bwd(gu, dact, *, name):
    T, F2 = gu.shape
    F = F2 // 2
    tr = _row_tile(T)

    def body(g_ref, u_ref, d_ref, o_ref):
        g = g_ref[...]
        d = d_ref[...]
        s = _sigmoid(g)
        o_ref[:, :F] = (d * u_ref[...] * s * (1.0 + g * (1.0 - s))).astype(o_ref.dtype)
        o_ref[:, F:] = (d * g * s).astype(o_ref.dtype)

    lo = pl.BlockSpec((tr, F), lambda i: (i, 0))
    hi = pl.BlockSpec((tr, F), lambda i: (i, 1))
    return pl.pallas_call(
        body, name=name, grid=(T // tr,), in_specs=[lo, hi, lo], out_specs=pl.BlockSpec((tr, F2), lambda i: (i, 0)),
        out_shape=jax.ShapeDtypeStruct((T, F2), MXU), compiler_params=_params("parallel"),
    )(gu, gu, dact)


def merge_fwd(gl, abr, pbr, sbr, *, name):
    T, D = abr.shape
    tr = _row_tile(T)

    def body(g0, g1, g2, a_ref, p_ref, s_ref, o_ref):
        m = _sigmoid(g0[...]) * a_ref[...] + _sigmoid(g1[...]) * p_ref[...] + _sigmoid(g2[...]) * s_ref[...]
        o_ref[...] = m.astype(o_ref.dtype)

    row = pl.BlockSpec((tr, D), lambda i: (i, 0))
    gs = [pl.BlockSpec((tr, D), lambda i, j=j: (i, j)) for j in range(3)]
    return pl.pallas_call(
        body, name=name, grid=(T // tr,), in_specs=gs + [row, row, row], out_specs=row,
        out_shape=jax.ShapeDtypeStruct((T, D), MXU), compiler_params=_params("parallel"),
    )(gl, gl, gl, abr, pbr, sbr)


def merge_bwd(gl, abr, pbr, sbr, dm, *, name):
    T, D = abr.shape
    tr = _row_tile(T)

    def body(g0, g1, g2, a_ref, p_ref, s_ref, dm_ref, da_ref, dp_ref, ds_ref, dg_ref):
        d = dm_ref[...]
        for j, (g_ref, b_ref, db_ref) in enumerate(((g0, a_ref, da_ref), (g1, p_ref, dp_ref), (g2, s_ref, ds_ref))):
            s = _sigmoid(g_ref[...])
            db_ref[...] = (d * s).astype(db_ref.dtype)
            dg_ref[:, j * D:(j + 1) * D] = (d * b_ref[...] * s * (1.0 - s)).astype(dg_ref.dtype)

    row = pl.BlockSpec((tr, D), lambda i: (i, 0))
    gs = [pl.BlockSpec((tr, D), lambda i, j=j: (i, j)) for j in range(3)]
    return pl.pallas_call(
        body, name=name, grid=(T // tr,), in_specs=gs + [row] * 4,
        out_specs=[row] * 3 + [pl.BlockSpec((tr, 3 * D), lambda i: (i, 0))],
        out_shape=[jax.ShapeDtypeStruct((T, D), MXU)] * 3 + [jax.ShapeDtypeStruct((T, 3 * D), MXU)],
        compiler_params=_params("parallel"),
    )(gl, gl, gl, abr, pbr, sbr, dm)


def gnorm_fwd(y, z, w, *, name):
    T, DI = y.shape
    gw = DI // C.G
    tr = _row_tile(T)

    def body(y_ref, z_ref, w_ref, o_ref):
        for g in range(C.G):
            sl = slice(g * gw, (g + 1) * gw)
            zz = z_ref[:, sl]
            v = y_ref[:, sl] * (zz * _sigmoid(zz))
            r = lax.rsqrt(jnp.mean(v * v, axis=-1, keepdims=True) + EPS)
            o_ref[:, sl] = (v * r * w_ref[:, sl]).astype(o_ref.dtype)

    row = pl.BlockSpec((tr, DI), lambda i: (i, 0))
    return pl.pallas_call(
        body, name=name, grid=(T // tr,), in_specs=[row, row, pl.BlockSpec((1, DI), lambda i: (0, 0))],
        out_specs=row, out_shape=jax.ShapeDtypeStruct((T, DI), MXU), compiler_params=_params("parallel"),
    )(y, z, w)


def gnorm_bwd(y, z, w, do, *, name):
    T, DI = y.shape
    gw = DI // C.G
    tr = _row_tile(T)

    def body(y_ref, z_ref, w_ref, do_ref, dy_ref, dz_ref, dw_ref):
        first = pl.program_id(0) == 0
        for g in range(C.G):
            sl = slice(g * gw, (g + 1) * gw)
            zz = z_ref[:, sl]
            yy = y_ref[:, sl]
            s = _sigmoid(zz)
            sz = zz * s
            v = yy * sz
            r = lax.rsqrt(jnp.mean(v * v, axis=-1, keepdims=True) + EPS)
            dov = do_ref[:, sl]
            gg = dov * w_ref[:, sl]
            dot = jnp.mean(gg * v, axis=-1, keepdims=True)
            dv = r * gg - v * (r * r * r * dot)
            dy_ref[:, sl] = dv * sz
            dz_ref[:, sl] = (dv * yy * s * (1.0 + zz * (1.0 - s))).astype(dz_ref.dtype)
            part = jnp.sum(dov * v * r, axis=0, keepdims=True)

            @pl.when(first)
            def _():
                dw_ref[:, sl] = part

            @pl.when(jnp.logical_not(first))
            def _():
                dw_ref[:, sl] += part

    row = pl.BlockSpec((tr, DI), lambda i: (i, 0))
    vec = pl.BlockSpec((1, DI), lambda i: (0, 0))
    return pl.pallas_call(
        body, name=name, grid=(T // tr,), in_specs=[row, row, vec, row], out_specs=[row, row, vec],
        out_shape=[jax.ShapeDtypeStruct((T, DI), F32), jax.ShapeDtypeStruct((T, DI), MXU),
                   jax.ShapeDtypeStruct((1, DI), F32)],
        compiler_params=_params("arbitrary"),
    )(y, z, w, do)


def adamw(w, g, m, v, *, name):
    shape = w.shape
    cols = shape[-1]
    rows = w.size // cols
    w2, g2, m2, v2 = (t.reshape(rows, cols) for t in (w, g, m, v))
    tr = rows if rows * cols * 4 <= (2 << 20) else _pick(rows, (512, 256, 128, 64, 32, 16, 8))
    while tr * cols * 4 > (2 << 20) and tr % 16 == 0:
        tr //= 2
    c1 = 1.0 - ADAM_B1 ** ADAM_STEP
    c2 = 1.0 - ADAM_B2 ** ADAM_STEP

    def body(w_ref, g_ref, m_ref, v_ref, d_ref, nm_ref, nv_ref):
        gv = g_ref[...]
        nm = ADAM_B1 * m_ref[...] + (1.0 - ADAM_B1) * gv
        nv = ADAM_B2 * v_ref[...] + (1.0 - ADAM_B2) * (gv * gv)
        d_ref[...] = -ADAM_LR * ((nm / c1) / (jnp.sqrt(nv / c2) + ADAM_EPS) + ADAM_WD * w_ref[...])
        nm_ref[...] = nm
        nv_ref[...] = nv

    row = pl.BlockSpec((tr, cols), lambda i: (i, 0))
    outs = pl.pallas_call(
        body, name=name, grid=(rows // tr,), in_specs=[row] * 4, out_specs=[row] * 3,
        out_shape=[jax.ShapeDtypeStruct((rows, cols), F32)] * 3, compiler_params=_params("parallel"),
    )(w2, g2, m2, v2)
    return tuple(o.reshape(shape) for o in outs)


def adamw_layers(w, g_layers, m, v, *, name):
    shape = w.shape
    L = shape[0]
    rows, cols = g_layers[0].shape
    w3, m3, v3 = (t.reshape(L, rows, cols) for t in (w, m, v))
    tr = _pick(rows, (512, 256, 128, 64, 32, 16, 8))
    while tr * cols * 4 > (2 << 20) and tr % 16 == 0:
        tr //= 2
    c1 = 1.0 - ADAM_B1 ** ADAM_STEP
    c2 = 1.0 - ADAM_B2 ** ADAM_STEP

    def body(*refs):
        w_ref, m_ref, v_ref = refs[0], refs[1], refs[2]
        g_refs = refs[3:3 + L]
        go_ref, d_ref, nm_ref, nv_ref = refs[3 + L:]
        layer = pl.program_id(0)
        gv = g_refs[0][...]
        for k in range(1, L):
            gv = jnp.where(layer == k, g_refs[k][...], gv)
        nm = ADAM_B1 * m_ref[...] + (1.0 - ADAM_B1) * gv
        nv = ADAM_B2 * v_ref[...] + (1.0 - ADAM_B2) * (gv * gv)
        d_ref[...] = -ADAM_LR * ((nm / c1) / (jnp.sqrt(nv / c2) + ADAM_EPS) + ADAM_WD * w_ref[...])
        go_ref[...] = gv
        nm_ref[...] = nm
        nv_ref[...] = nv

    blk = pl.BlockSpec((None, tr, cols), lambda l, i: (l, i, 0))
    gblk = pl.BlockSpec((tr, cols), lambda l, i: (i, 0))
    outs = pl.pallas_call(
        body, name=name, grid=(L, rows // tr), in_specs=[blk] * 3 + [gblk] * L, out_specs=[blk] * 4,
        out_shape=[jax.ShapeDtypeStruct((L, rows, cols), F32)] * 4, compiler_params=_params("parallel", "parallel"),
    )(w3, m3, v3, *g_layers)
    return tuple(o.reshape(shape) for o in outs)


def _attn_masks(i, heads):
    B = C.blk
    row = lax.broadcasted_iota(jnp.int32, (heads * B, 2 * B), 0) & (B - 1)
    col = lax.broadcasted_iota(jnp.int32, (heads * B, 2 * B), 1)
    diff = row + B - col
    return (diff >= 0) & (diff < B) & ((col >= B) | (i > 0))


def _stack_heads(ref, kvh, ppk, lo):
    parts = []
    for pr in range(ppk):
        pair = kvh * ppk + pr
        qp = ref[:, pair * 128:(pair + 1) * 128].astype(F32)
        parts += [jnp.where(lo, qp, 0.0), jnp.where(lo, 0.0, qp)]
    return jnp.concatenate(parts, axis=0)


def _unstack_heads(ref, val, kvh, ppk, lo):
    B = C.blk
    for pr in range(ppk):
        pair = kvh * ppk + pr
        ref[:, pair * 128:(pair + 1) * 128] = jnp.where(lo, val[2 * pr * B:(2 * pr + 1) * B],
                                                        val[(2 * pr + 1) * B:(2 * pr + 2) * B]).astype(ref.dtype)


def _sink_column(sink_ref, kvh, ppk):
    B = C.blk
    h0 = kvh * 2 * ppk
    return jnp.concatenate([jnp.broadcast_to(sink_ref[:, h:h + 1], (B, 1)) for h in range(h0, h0 + 2 * ppk)], axis=0)


def _kv2(prev_ref, cur_ref, m, lo):
    sl = slice(m * 128, (m + 1) * 128)
    slab = jnp.concatenate([prev_ref[:, sl], cur_ref[:, sl]], axis=0).astype(F32)
    rolled = pltpu.roll(slab, 64, axis=1)
    return jnp.where(lo, slab, rolled), jnp.where(lo, rolled, slab)


def _attn_probs(qh, k2, mask, sk):
    s = _dot_nt(qh, k2) * (C.hd ** -0.5)
    s = jnp.where(mask, s, NEG)
    mx = jnp.maximum(jnp.max(s, axis=-1, keepdims=True), sk)
    p = jnp.exp(s - mx)
    esk = jnp.exp(sk - mx)
    inv = 1.0 / (jnp.sum(p, axis=-1, keepdims=True) + esk)
    return p * inv, esk * inv


def attn_fwd(qkv, sink, *, name):
    T = qkv.shape[0]
    B, AW, KW = C.blk, C.AW, C.KW
    assert C.hd == 64 and KW % 128 == 0 and (C.qh // C.kvh) % 2 == 0
    nb = T // B
    kb = AW // KW
    ppk = C.qh // C.kvh // 2

    def body(q_ref, kc_ref, kp_ref, vc_ref, vp_ref, sink_ref, o_ref):
        i = pl.program_id(0)
        lo = lax.broadcasted_iota(jnp.int32, (1, 128), 1) < 64
        mask = _attn_masks(i, 2 * ppk)
        for m in range(KW // 128):
            k2s = _kv2(kp_ref, kc_ref, m, lo)
            v2s = _kv2(vp_ref, vc_ref, m, lo)
            for par in range(2):
                kvh = 2 * m + par
                q4 = _stack_heads(q_ref, kvh, ppk, lo)
                p, _ = _attn_probs(q4, k2s[par], mask, _sink_column(sink_ref, kvh, ppk))
                _unstack_heads(o_ref, _dot_nn(p, v2s[par]), kvh, ppk, lo)

    prev = lambda i: jnp.maximum(i - 1, 0)
    return pl.pallas_call(
        body, name=name, grid=(nb,),
        in_specs=[pl.BlockSpec((B, AW), lambda i: (i, 0)),
                  pl.BlockSpec((B, KW), lambda i: (i, kb)), pl.BlockSpec((B, KW), lambda i: (prev(i), kb)),
                  pl.BlockSpec((B, KW), lambda i: (i, kb + 1)), pl.BlockSpec((B, KW), lambda i: (prev(i), kb + 1)),
                  pl.BlockSpec((1, C.qh), lambda i: (0, 0))],
        out_specs=pl.BlockSpec((B, AW), lambda i: (i, 0)),
        out_shape=jax.ShapeDtypeStruct((T, AW), MXU), compiler_params=_params("parallel"),
    )(qkv, qkv, qkv, qkv, qkv, sink)


def attn_bwd(qkv, sink, dout, *, name):
    T = qkv.shape[0]
    B, AW, KW = C.blk, C.AW, C.KW
    nb = T // B
    kb = AW // KW
    ppk = C.qh // C.kvh // 2
    scale = C.hd ** -0.5

    def body(q_ref, kc_ref, kp_ref, vc_ref, vp_ref, sink_ref, do_ref, dq_ref, dk_ref, dv_ref, ds_ref, ck_ref, cv_ref):
        i = pl.program_id(0)

        @pl.when(i == 0)
        def _():
            ck_ref[...] = jnp.zeros_like(ck_ref)
            cv_ref[...] = jnp.zeros_like(cv_ref)
            ds_ref[...] = jnp.zeros_like(ds_ref)

        @pl.when(i < nb)
        def _():
            lane = lax.broadcasted_iota(jnp.int32, (1, 128), 1)
            lo = lane < 64
            hlane = lax.broadcasted_iota(jnp.int32, (1, C.qh), 1)
            mask = _attn_masks(i, 2 * ppk)
            dsink = jnp.zeros((1, C.qh), F32)
            for m in range(KW // 128):
                ksl = slice(m * 128, (m + 1) * 128)
                k2s = _kv2(kp_ref, kc_ref, m, lo)
                v2s = _kv2(vp_ref, vc_ref, m, lo)
                folded = []
                for par in range(2):
                    kvh = 2 * m + par
                    q4 = _stack_heads(q_ref, kvh, ppk, lo)
                    do4 = _stack_heads(do_ref, kvh, ppk, lo)
                    p, psink = _attn_probs(q4, k2s[par], mask, _sink_column(sink_ref, kvh, ppk))
                    dp = _dot_nt(do4, v2s[par])
                    delta = jnp.sum(p * dp, axis=-1, keepdims=True)
                    dsc = p * (dp - delta)
                    _unstack_heads(dq_ref, _dot_nn(dsc, k2s[par]) * scale, kvh, ppk, lo)
                    dk2 = _dot_tn(dsc, q4) * scale
                    dv2 = _dot_tn(p, do4)
                    pd = psink * delta
                    for k in range(2 * ppk):
                        dsh = -jnp.sum(pd[k * B:(k + 1) * B], axis=0, keepdims=True)
                        dsink = dsink + jnp.where(hlane == kvh * 2 * ppk + k, dsh, 0.0)
                    folded.append((dk2 + pltpu.roll(dk2, 64, axis=1), dv2 + pltpu.roll(dv2, 64, axis=1)))
                dks = jnp.where(lo, folded[0][0], folded[1][0])
                dvs = jnp.where(lo, folded[0][1], folded[1][1])
                dk_ref[:, ksl] = (ck_ref[:, ksl] + dks[:B]).astype(dk_ref.dtype)
                dv_ref[:, ksl] = (cv_ref[:, ksl] + dvs[:B]).astype(dv_ref.dtype)
                ck_ref[:, ksl] = dks[B:]
                cv_ref[:, ksl] = dvs[B:]
            ds_ref[...] += dsink

        @pl.when(i == nb)
        def _():
            dk_ref[...] = ck_ref[...].astype(dk_ref.dtype)
            dv_ref[...] = cv_ref[...].astype(dv_ref.dtype)

    cur = lambda i: jnp.minimum(i, nb - 1)
    prev = lambda i: jnp.maximum(jnp.minimum(i, nb - 1) - 1, 0)
    out = lambda i: jnp.maximum(i - 1, 0)
    return pl.pallas_call(
        body, name=name, grid=(nb + 1,),
        in_specs=[pl.BlockSpec((B, AW), lambda i: (cur(i), 0)),
                  pl.BlockSpec((B, KW), lambda i: (cur(i), kb)), pl.BlockSpec((B, KW), lambda i: (prev(i), kb)),
                  pl.BlockSpec((B, KW), lambda i: (cur(i), kb + 1)), pl.BlockSpec((B, KW), lambda i: (prev(i), kb + 1)),
                  pl.BlockSpec((1, C.qh), lambda i: (0, 0)),
                  pl.BlockSpec((B, AW), lambda i: (cur(i), 0))],
        out_specs=[pl.BlockSpec((B, AW), lambda i: (cur(i), 0)),
                   pl.BlockSpec((B, KW), lambda i: (out(i), 0)), pl.BlockSpec((B, KW), lambda i: (out(i), 0)),
                   pl.BlockSpec((1, C.qh), lambda i: (0, 0))],
        out_shape=[jax.ShapeDtypeStruct((T, AW), MXU), jax.ShapeDtypeStruct((T, KW), MXU),
                   jax.ShapeDtypeStruct((T, KW), MXU), jax.ShapeDtypeStruct((1, C.qh), F32)],
        scratch_shapes=[pltpu.VMEM((B, KW), F32), pltpu.VMEM((B, KW), F32)],
        compiler_params=_params("arbitrary"),
    )(qkv, qkv, qkv, qkv, qkv, sink, dout)


POOL_HALO = 16


def _window_sum(e, w, n, forward):
    s, k = e, 1
    while k < w:
        s = s + pltpu.roll(s, (n - k) if forward else k, axis=0)
        k *= 2
    return s


def pool_fwd(u, pw, ps, *, name):
    T, PW = u.shape
    PG = C.PG
    tT = _pick(T, (256, 128))
    hb = tT // POOL_HALO

    def body(u_ref, uh_ref, pw_ref, ps_ref, o_ref, mx_ref):
        i = pl.program_id(0)
        halo = jnp.where(i > 0, uh_ref[...], 0.0)
        ext = jnp.concatenate([halo, u_ref[...]], axis=0)
        t = i * tT + lax.broadcasted_iota(jnp.int32, (tT, 1), 0)
        for g, w in enumerate(POOL_WINDOWS):
            sl = slice(g * PG, (g + 1) * PG)
            s = _window_sum(ext[:, sl], w, tT + POOL_HALO, False)[POOL_HALO:]
            cnt = jnp.minimum(t + 1, w).astype(F32)
            mixed = (s / cnt - u_ref[:, sl]).astype(MXU)
            mx_ref[:, sl] = mixed.astype(mx_ref.dtype)
            o_ref[:, sl] = (_dot_nn(mixed, pw_ref[g]) * ps_ref[:, sl]).astype(o_ref.dtype)

    row = pl.BlockSpec((tT, PW), lambda i: (i, 0))
    return pl.pallas_call(
        body, name=name, grid=(T // tT,),
        in_specs=[row, pl.BlockSpec((POOL_HALO, PW), lambda i: (jnp.maximum(i * hb - 1, 0), 0)),
                  pl.BlockSpec((4, PG, PG), lambda i: (0, 0, 0)), pl.BlockSpec((1, PW), lambda i: (0, 0))],
        out_specs=[row, row], out_shape=[jax.ShapeDtypeStruct((T, PW), MXU)] * 2,
        compiler_params=_params("parallel"),
    )(u, u, pw, ps)


def pool_bwd(dpool, mixed, pw, ps, *, name):
    T, PW = dpool.shape
    PG = C.PG
    tT = _pick(T, (256, 128))
    hb = tT // POOL_HALO
    n = T // tT
    rows = tT + POOL_HALO

    def body(dp_ref, dph_ref, mx_ref, pw_ref, ps_ref, du_ref, dpw_ref, dps_ref):
        i = pl.program_id(0)

        @pl.when(i == 0)
        def _():
            dpw_ref[...] = jnp.zeros_like(dpw_ref)
            dps_ref[...] = jnp.zeros_like(dps_ref)

        halo = jnp.where(i < n - 1, dph_ref[...], 0.0)
        dext = jnp.concatenate([dp_ref[...], halo], axis=0)
        t = i * tT + lax.broadcasted_iota(jnp.int32, (rows, 1), 0)
        for g, w in enumerate(POOL_WINDOWS):
            sl = slice(g * PG, (g + 1) * PG)
            dyg = dext[:, sl] * ps_ref[:, sl]
            dmix = _dot_nt(dyg, pw_ref[g])
            cnt = jnp.minimum(t + 1, w).astype(F32)
            s = _window_sum(dmix / cnt, w, rows, True)
            du_ref[:, sl] = (s[:tT] - dmix[:tT]).astype(du_ref.dtype)
            mb = mx_ref[:, sl]
            dpw_ref[g] += _dot_tn(mb, dyg[:tT])
            dps_ref[:, sl] += jnp.sum(dp_ref[:, sl] * _dot_nn(mb, pw_ref[g]), axis=0, keepdims=True)

    row = pl.BlockSpec((tT, PW), lambda i: (i, 0))
    return pl.pallas_call(
        body, name=name, grid=(n,),
        in_specs=[row, pl.BlockSpec((POOL_HALO, PW), lambda i: (jnp.minimum((i + 1) * hb, T // POOL_HALO - 1), 0)), row,
                  pl.BlockSpec((4, PG, PG), lambda i: (0, 0, 0)), pl.BlockSpec((1, PW), lambda i: (0, 0))],
        out_specs=[row, pl.BlockSpec((4, PG, PG), lambda i: (0, 0, 0)), pl.BlockSpec((1, PW), lambda i: (0, 0))],
        out_shape=[jax.ShapeDtypeStruct((T, PW), MXU), jax.ShapeDtypeStruct((4, PG, PG), F32),
                   jax.ShapeDtypeStruct((1, PW), F32)],
        compiler_params=_params("arbitrary"),
    )(dpool, dpool, mixed, pw, ps)


CONV_HALO = 8
CONV_K = 4


def _conv(x, cw_ref, cb_ref, sl):
    acc = cb_ref[:, sl] + cw_ref[CONV_K - 1:CONV_K, sl] * x
    for k in range(CONV_K - 1):
        acc = acc + cw_ref[k:k + 1, sl] * pltpu.roll(x, CONV_K - 1 - k, axis=0)
    return acc


def conv_fwd(xbc, cw, cb, *, name):
    T, CC = xbc.shape
    tT = _pick(T, (256, 128))
    hb = tT // CONV_HALO
    cs = _pick(CC, (512, 256, 128))

    def body(x_ref, xh_ref, cw_ref, cb_ref, o_ref):
        i = pl.program_id(0)
        for c0 in range(0, CC, cs):
            sl = slice(c0, c0 + cs)
            ext = jnp.concatenate([jnp.where(i > 0, xh_ref[:, sl], 0.0), x_ref[:, sl]], axis=0)
            xc = _conv(ext, cw_ref, cb_ref, sl)[CONV_HALO:]
            o_ref[:, sl] = xc * _sigmoid(xc)

    row = pl.BlockSpec((tT, CC), lambda i: (i, 0))
    return pl.pallas_call(
        body, name=name, grid=(T // tT,),
        in_specs=[row, pl.BlockSpec((CONV_HALO, CC), lambda i: (jnp.maximum(i * hb - 1, 0), 0)),
                  pl.BlockSpec((CONV_K, CC), lambda i: (0, 0)), pl.BlockSpec((1, CC), lambda i: (0, 0))],
        out_specs=row, out_shape=jax.ShapeDtypeStruct((T, CC), F32), compiler_params=_params("parallel"),
    )(xbc, xbc, cw, cb)


def conv_bwd(xbc, cw, cb, dxa, *, name):
    T, CC = xbc.shape
    tT = _pick(T, (256, 128))
    hb = tT // CONV_HALO
    n = T // tT
    rows = tT + 2 * CONV_HALO
    cs = _pick(CC, (512, 256, 128))

    def body(x_ref, xp_ref, xn_ref, cw_ref, cb_ref, d_ref, dn_ref, dx_ref, dcw_ref, dcb_ref):
        i = pl.program_id(0)

        @pl.when(i == 0)
        def _():
            dcw_ref[...] = jnp.zeros_like(dcw_ref)
            dcb_ref[...] = jnp.zeros_like(dcb_ref)

        r = lax.broadcasted_iota(jnp.int32, (rows, 1), 0)
        own = (r >= CONV_HALO) & (r < tT + CONV_HALO)
        for c0 in range(0, CC, cs):
            sl = slice(c0, c0 + cs)
            x = jnp.concatenate([jnp.where(i > 0, xp_ref[:, sl], 0.0), x_ref[:, sl],
                                 jnp.where(i < n - 1, xn_ref[:, sl], 0.0)], axis=0)
            da = jnp.concatenate([jnp.zeros((CONV_HALO, cs), F32), d_ref[:, sl],
                                  jnp.where(i < n - 1, dn_ref[:, sl], 0.0)], axis=0)
            xc = _conv(x, cw_ref, cb_ref, sl)
            sg = _sigmoid(xc)
            dxc = da * sg * (1.0 + xc * (1.0 - sg))
            acc = cw_ref[CONV_K - 1:CONV_K, sl] * dxc
            for k in range(CONV_K - 1):
                acc = acc + cw_ref[k:k + 1, sl] * pltpu.roll(dxc, rows - (CONV_K - 1 - k), axis=0)
            dx_ref[:, sl] = acc[CONV_HALO:tT + CONV_HALO].astype(dx_ref.dtype)
            down = jnp.where(own, dxc, 0.0)
            dcb_ref[:, sl] += jnp.sum(down, axis=0, keepdims=True)
            dcw_ref[CONV_K - 1:CONV_K, sl] += jnp.sum(down * x, axis=0, keepdims=True)
            for k in range(CONV_K - 1):
                dcw_ref[k:k + 1, sl] += jnp.sum(down * pltpu.roll(x, CONV_K - 1 - k, axis=0), axis=0, keepdims=True)

    row = pl.BlockSpec((tT, CC), lambda i: (i, 0))
    prev = pl.BlockSpec((CONV_HALO, CC), lambda i: (jnp.maximum(i * hb - 1, 0), 0))
    nxt = pl.BlockSpec((CONV_HALO, CC), lambda i: (jnp.minimum((i + 1) * hb, T // CONV_HALO - 1), 0))
    return pl.pallas_call(
        body, name=name, grid=(n,),
        in_specs=[row, prev, nxt, pl.BlockSpec((CONV_K, CC), lambda i: (0, 0)), pl.BlockSpec((1, CC), lambda i: (0, 0)),
                  row, nxt],
        out_specs=[row, pl.BlockSpec((CONV_K, CC), lambda i: (0, 0)), pl.BlockSpec((1, CC), lambda i: (0, 0))],
        out_shape=[jax.ShapeDtypeStruct((T, CC), MXU), jax.ShapeDtypeStruct((CONV_K, CC), F32),
                   jax.ShapeDtypeStruct((1, CC), F32)],
        compiler_params=_params("arbitrary"),
    )(xbc, xbc, xbc, cw, cb, dxa, dxa)


def _softplus(x):
    return jnp.maximum(x, 0.0) + jnp.log(1.0 + jnp.exp(-jnp.abs(x)))


def _dot_exact(a, b):
    return lax.dot_general(a, b, (((1,), (0,)), ((), ())), precision=lax.Precision.HIGHEST, preferred_element_type=F32)


def ssd_prep(raw_t, bias, alog, *, name):
    H, T = raw_t.shape
    B = C.blk
    tc = _pick(T, (4 * B, 2 * B, B))

    def body(r_ref, b_ref, al_ref, dt_ref, acs_ref):
        dt = _softplus(r_ref[...] + b_ref[...])
        dt_ref[...] = dt
        dta = dt * (-jnp.exp(al_ref[...]))
        upper = (lax.broadcasted_iota(jnp.int32, (B, B), 0) <= lax.broadcasted_iota(jnp.int32, (B, B), 1)).astype(F32)
        for j in range(tc // B):
            acs_ref[:, j * B:(j + 1) * B] = _dot_exact(dta[:, j * B:(j + 1) * B], upper)

    blk = pl.BlockSpec((H, tc), lambda i: (0, i))
    vec = pl.BlockSpec((H, 1), lambda i: (0, 0))
    return pl.pallas_call(
        body, name=name, grid=(T // tc,), in_specs=[blk, vec, vec], out_specs=[blk, blk],
        out_shape=[jax.ShapeDtypeStruct((H, T), F32)] * 2, compiler_params=_params("parallel"),
    )(raw_t, bias, alog)


def _pair(lo, arr, h0, rows=slice(None)):
    return jnp.where(lo, arr[rows, h0:h0 + 1], arr[rows, h0 + 1:h0 + 2])


def _decay(acs, acs_t, h, causal):
    return jnp.exp(jnp.where(causal, acs[:, h:h + 1] - acs_t[h:h + 1, :], NEG))


def ssd_fwd(xa, dt, acs, acs_t, dskip, *, name):
    T = xa.shape[0]
    B, DI, G, N, HG, P, H = C.blk, C.DI, C.G, C.N, C.HG, C.P, C.H
    assert P == 64 and HG % 2 == 0
    nc = T // B
    W = HG * P

    def body(xa_ref, dt_ref, acs_ref, acst_ref, ds_ref, y_ref, hp_ref, h_scr):
        @pl.when(pl.program_id(0) == 0)
        def _():
            h_scr[...] = jnp.zeros_like(h_scr)

        lo = lax.broadcasted_iota(jnp.int32, (1, 128), 1) < 64
        causal = lax.broadcasted_iota(jnp.int32, (B, B), 0) >= lax.broadcasted_iota(jnp.int32, (B, B), 1)
        dt, acs, acs_t, dsk = dt_ref[...], acs_ref[...], acst_ref[...], ds_ref[...]
        for g in range(G):
            bg = xa_ref[:, DI + g * N:DI + (g + 1) * N]
            cg = xa_ref[:, DI + (G + g) * N:DI + (G + g + 1) * N]
            cb = _dot_nt(cg, bg)
            hg = h_scr[g]
            hp_ref[0, g * N:(g + 1) * N, :] = hg
            yoff = _dot_nn(cg, hg)
            xws, decs = [], []
            for j in range(HG // 2):
                h0 = g * HG + 2 * j
                xsl = slice(h0 * P, (h0 + 2) * P)
                xp = xa_ref[:, xsl]
                ap = _pair(lo, acs, h0)
                alast = _pair(lo, acs, h0, slice(B - 1, B))
                xdt = xp * _pair(lo, dt, h0)
                ys = [_dot_nn(cb * _decay(acs, acs_t, h0 + half, causal), xdt) for half in range(2)]
                y_ref[:, xsl] = (jnp.where(lo, ys[0], ys[1]) + yoff[:, 2 * j * P:(2 * j + 2) * P] * jnp.exp(ap)
                                 + _pair(lo, dsk, h0) * xp)
                xws.append(xdt * jnp.exp(alast - ap))
                decs.append(jnp.exp(alast))
            h_scr[g] = hg * jnp.concatenate(decs, axis=1) + _dot_tn(bg, jnp.concatenate(xws, axis=1))

    tok = lambda w: pl.BlockSpec((B, w), lambda c: (c, 0))
    return pl.pallas_call(
        body, name=name, grid=(nc,),
        in_specs=[tok(C.CC), tok(H), tok(H), pl.BlockSpec((H, B), lambda c: (0, c)), pl.BlockSpec((1, H), lambda c: (0, 0))],
        out_specs=[tok(DI), pl.BlockSpec((1, G * N, W), lambda c: (c, 0, 0))],
        out_shape=[jax.ShapeDtypeStruct((T, DI), F32), jax.ShapeDtypeStruct((nc, G * N, W), F32)],
        scratch_shapes=[pltpu.VMEM((G, N, W), F32)],
        compiler_params=_params("arbitrary"),
    )(xa, dt, acs, acs_t, dskip)


def ssd_bwd(xa, dt, acs, acs_t, dskip, hprev, dy, *, name):
    T = xa.shape[0]
    B, DI, G, N, HG, P, H = C.blk, C.DI, C.G, C.N, C.HG, C.P, C.H
    nc = T // B
    W = HG * P

    def body(xa_ref, dt_ref, acs_ref, acst_ref, ds_ref, hp_ref, dy_ref, dxa_ref, ddt_ref, dacs_ref, dd_ref, dh_scr):
        @pl.when(pl.program_id(0) == 0)
        def _():
            dh_scr[...] = jnp.zeros_like(dh_scr)
            dd_ref[...] = jnp.zeros_like(dd_ref)

        lo = lax.broadcasted_iota(jnp.int32, (1, 128), 1) < 64
        hi = jnp.logical_not(lo)
        causal = lax.broadcasted_iota(jnp.int32, (B, B), 0) >= lax.broadcasted_iota(jnp.int32, (B, B), 1)
        hlane = lax.broadcasted_iota(jnp.int32, (1, H), 1)
        hsub = lax.broadcasted_iota(jnp.int32, (B, 1), 0)
        lastrow = lax.broadcasted_iota(jnp.int32, (B, 1), 0) == B - 1
        dt, acs, acs_t, dsk = dt_ref[...], acs_ref[...], acst_ref[...], ds_ref[...]
        d_acs = jnp.zeros((B, H), F32)
        d_acs_t = jnp.zeros((B, B), F32)
        d_dt = jnp.zeros((B, H), F32)
        d_d = jnp.zeros((1, H), F32)

        def rsum(v):
            return (jnp.sum(jnp.where(lo, v, 0.0), axis=1, keepdims=True),
                    jnp.sum(jnp.where(hi, v, 0.0), axis=1, keepdims=True))

        for g in range(G):
            bsl = slice(DI + g * N, DI + (g + 1) * N)
            csl = slice(DI + (G + g) * N, DI + (G + g + 1) * N)
            bg, cg = xa_ref[:, bsl], xa_ref[:, csl]
            cb = _dot_nt(cg, bg)
            hg = hp_ref[0, g * N:(g + 1) * N, :]
            dhg = dh_scr[g]
            yoff = _dot_nn(cg, hg)
            bds = _dot_nn(bg, dhg)
            d_cb = jnp.zeros((B, B), F32)
            dyes, xws, decs = [], [], []
            for j in range(HG // 2):
                h0 = g * HG + 2 * j
                xsl = slice(h0 * P, (h0 + 2) * P)
                psl = slice(2 * j * P, (2 * j + 2) * P)
                xp, dyp = xa_ref[:, xsl], dy_ref[:, xsl]
                dtp = _pair(lo, dt, h0)
                ap = _pair(lo, acs, h0)
                alast = _pair(lo, acs, h0, slice(B - 1, B))
                ea, ew, el = jnp.exp(ap), jnp.exp(alast - ap), jnp.exp(alast)
                xdt = xp * dtp
                halves = []
                for half in range(2):
                    h = h0 + half
                    lm = _decay(acs, acs_t, h, causal)
                    m = cb * lm
                    d_m = _dot_nt(jnp.where(lo if half == 0 else hi, dyp, 0.0), xdt)
                    d_cb = d_cb + d_m * lm
                    wgt = d_m * m
                    d_acs = d_acs + jnp.where(hlane == h, jnp.sum(wgt, axis=1, keepdims=True), 0.0)
                    d_acs_t = d_acs_t + jnp.where(hsub == h, jnp.sum(wgt, axis=0, keepdims=True), 0.0)
                    halves.append(_dot_tn(m, dyp))
                bdp = bds[:, psl]
                dxdt = jnp.where(lo, halves[0], halves[1]) + ew * bdp
                dxa_ref[:, xsl] = dtp * dxdt + _pair(lo, dsk, h0) * dyp
                xw = xdt * ew
                terms_dt = rsum(dxdt * xp)
                terms_dd = rsum(dyp * xp)
                terms_off = rsum(dyp * (ea * yoff[:, psl]))
                terms_e = rsum(xw * bdp)
                terms_h = rsum(hg[:, psl] * dhg[:, psl])
                for half in range(2):
                    h = h0 + half
                    sel = hlane == h
                    d_dt = d_dt + jnp.where(sel, terms_dt[half], 0.0)
                    d_d = d_d + jnp.where(sel, jnp.sum(terms_dd[half], axis=0, keepdims=True), 0.0)
                    e_last = jnp.sum(jnp.where(lo if half == 0 else hi, el, 0.0), axis=1, keepdims=True) * (1.0 / P)
                    d_last = (jnp.sum(terms_e[half], axis=0, keepdims=True)
                              + e_last * jnp.sum(terms_h[half], axis=0, keepdims=True))
                    d_acs = d_acs + jnp.where(sel, terms_off[half] - terms_e[half] + jnp.where(lastrow, d_last, 0.0), 0.0)
                dyes.append(dyp * ea)
                xws.append(xw)
                decs.append(el)
            dye = jnp.concatenate(dyes, axis=1)
            xwc = jnp.concatenate(xws, axis=1)
            dxa_ref[:, csl] = _dot_nn(d_cb, bg) + _dot_nt(dye, hg)
            dxa_ref[:, bsl] = _dot_tn(d_cb, cg) + _dot_nt(xwc, dhg)
            dh_scr[g] = dhg * jnp.concatenate(decs, axis=1) + _dot_tn(cg, dye)
        ddt_ref[...] = d_dt
        dacs_ref[...] = d_acs - d_acs_t.T[:, :H]
        dd_ref[...] += d_d

    rev = lambda w: pl.BlockSpec((B, w), lambda c: (nc - 1 - c, 0))
    vec = pl.BlockSpec((1, H), lambda c: (0, 0))
    return pl.pallas_call(
        body, name=name, grid=(nc,),
        in_specs=[rev(C.CC), rev(H), rev(H), pl.BlockSpec((H, B), lambda c: (0, nc - 1 - c)), vec,
                  pl.BlockSpec((1, G * N, W), lambda c: (nc - 1 - c, 0, 0)), rev(DI)],
        out_specs=[rev(C.CC), rev(H), rev(H), vec],
        out_shape=[jax.ShapeDtypeStruct((T, C.CC), F32), jax.ShapeDtypeStruct((T, H), F32),
                   jax.ShapeDtypeStruct((T, H), F32), jax.ShapeDtypeStruct((1, H), F32)],
        scratch_shapes=[pltpu.VMEM((G, N, W), F32)],
        compiler_params=_params("arbitrary"),
    )(xa, dt, acs, acs_t, dskip, hprev, dy)


def ssd_post(ddt, dacs, dt, raw, bias, alog, *, name):
    T, H = ddt.shape
    B = C.blk
    tc = _pick(T, (4 * B, 2 * B, B))

    def body(ddt_ref, dacs_ref, dt_ref, raw_ref, b_ref, al_ref, draw_ref, db_ref, dal_ref):
        @pl.when(pl.program_id(0) == 0)
        def _():
            db_ref[...] = jnp.zeros_like(db_ref)
            dal_ref[...] = jnp.zeros_like(dal_ref)

        a = -jnp.exp(al_ref[...])
        lower = (lax.broadcasted_iota(jnp.int32, (B, B), 0) <= lax.broadcasted_iota(jnp.int32, (B, B), 1)).astype(F32)
        for j in range(tc // B):
            sl = slice(j * B, (j + 1) * B)
            rc = _dot_exact(lower, dacs_ref[sl, :])
            dtv = dt_ref[sl, :]
            draw = (ddt_ref[sl, :] + a * rc) * _sigmoid(raw_ref[sl, :] + b_ref[...])
            draw_ref[sl, :] = draw
            db_ref[...] += jnp.sum(draw, axis=0, keepdims=True)
            dal_ref[...] += jnp.sum(dtv * rc, axis=0, keepdims=True) * a

    blk = pl.BlockSpec((tc, H), lambda i: (i, 0))
    vec = pl.BlockSpec((1, H), lambda i: (0, 0))
    return pl.pallas_call(
        body, name=name, grid=(T // tc,), in_specs=[blk, blk, blk, blk, vec, vec], out_specs=[blk, vec, vec],
        out_shape=[jax.ShapeDtypeStruct((T, H), F32), jax.ShapeDtypeStruct((1, H), F32), jax.ShapeDtypeStruct((1, H), F32)],
        compiler_params=_params("arbitrary"),
    )(ddt, dacs, dt, raw, bias, alog)


MESH = pl.DeviceIdType.MESH
PACK_W = 1024
ANY = pl.BlockSpec(memory_space=pl.ANY)


def _place():
    return lax.axis_index("x"), lax.axis_index("y"), lax.axis_index("c")


def _other_chips(x, y):
    return [(1 - x, y), (x, 1 - y), (1 - x, 1 - y)]


def _remote(src, dst, send_sems, recv_sems, k, to):
    return pltpu.make_async_remote_copy(src_ref=src, dst_ref=dst, send_sem=send_sems.at[k], recv_sem=recv_sems.at[k],
                                        device_id=to, device_id_type=MESH)


def _half(c, rows):
    return pl.ds(pl.multiple_of(c * (rows // 2), 16), rows // 2)


def gather_weights(shards, rowwise, *, name):
    n = len(shards)

    def body(*refs):
        ps, gs = refs[:n], refs[n:2 * n]
        send_sems, recv_sems = refs[2 * n:]
        x, y, c = _place()
        s = 2 * x + y
        sib = (x, y, 1 - c)
        chips = _other_chips(x, y)

        def dst(w, shard, layer=None):
            if rowwise[w]:
                return gs[w].at[:, shard] if layer is None else gs[w].at[layer, shard]
            return gs[w].at[shard] if layer is None else gs[w].at[shard, layer]

        copy = functools.partial(_remote, send_sems=send_sems, recv_sems=recv_sems)
        started = []
        for w in range(n):
            started.append(copy(ps[w], dst(w, s), k=w, to=sib))
        for w in range(n):
            for j, (px, py) in enumerate(chips):
                started.append(copy(ps[w].at[c], dst(w, s, c), k=n + 3 * w + j, to=(px, py, c)))
        for cp in started:
            cp.start()
        passed = []
        for w in range(n):
            for j, (px, py) in enumerate(chips):
                there = dst(w, 2 * px + py, c)
                copy(ps[w].at[c], there, k=n + 3 * w + j, to=(px, py, c)).wait_recv()
                fw = copy(there, there, k=4 * n + 3 * w + j, to=sib)
                fw.start()
                passed.append(fw)
        for w in range(n):
            copy(ps[w], dst(w, s), k=w, to=sib).wait_recv()
            for j, (px, py) in enumerate(chips):
                there = dst(w, 2 * px + py, 1 - c)
                copy(there, there, k=4 * n + 3 * w + j, to=sib).wait_recv()
        for cp in started + passed:
            cp.wait_send()

    def full(p, rw):
        L = p.shape[0]
        return jax.ShapeDtypeStruct((L, 4) + p.shape[1:] if rw else (4,) + p.shape, p.dtype)

    return pl.pallas_call(
        body, name=name, in_specs=[ANY] * n, out_specs=[ANY] * n, out_shape=[full(p, rw) for p, rw in zip(shards, rowwise)],
        scratch_shapes=[pltpu.SemaphoreType.DMA((7 * n,)), pltpu.SemaphoreType.DMA((7 * n,))],
    )(*shards)


def swap_halves(items, *, name):
    n = len(items)

    def body(*refs):
        gs, rs = refs[:n], refs[n:2 * n]
        send_sems, recv_sems = refs[2 * n:]
        x, y, c = _place()
        cps = [_remote(g.at[:, _half(1 - c, g.shape[1])], r, send_sems, recv_sems, k, (x, y, 1 - c))
               for k, (g, r) in enumerate(zip(gs, rs))]
        for cp in cps:
            cp.start()
        for cp in cps:
            cp.wait()

    return pl.pallas_call(
        body, name=name, in_specs=[ANY] * n, out_specs=[ANY] * n,
        out_shape=[jax.ShapeDtypeStruct((4, g.shape[1] // 2, g.shape[2]), g.dtype) for g in items],
        scratch_shapes=[pltpu.SemaphoreType.DMA((n,)), pltpu.SemaphoreType.DMA((n,))],
    )(*items)


def scatter_chips(items, *, name):
    n = len(items)

    def body(*refs):
        as_, rs = refs[:n], refs[n:2 * n]
        send_sems, recv_sems = refs[2 * n:]
        x, y, c = _place()
        cps = [_remote(a.at[2 * px + py], r.at[j], send_sems, recv_sems, 3 * k + j, (px, py, c))
               for k, (a, r) in enumerate(zip(as_, rs)) for j, (px, py) in enumerate(_other_chips(x, y))]
        for cp in cps:
            cp.start()
        for cp in cps:
            cp.wait()

    return pl.pallas_call(
        body, name=name, in_specs=[ANY] * n, out_specs=[ANY] * n,
        out_shape=[jax.ShapeDtypeStruct((3,) + a.shape[1:], a.dtype) for a in items],
        scratch_shapes=[pltpu.SemaphoreType.DMA((3 * n,)), pltpu.SemaphoreType.DMA((3 * n,))],
    )(*items)


def join_halves(items, *, name):
    n = len(items)

    def body(*refs):
        rs, outs = refs[:n], refs[n:2 * n]
        send_sems, recv_sems = refs[2 * n:]
        x, y, c = _place()
        sib = (x, y, 1 - c)
        cps = []
        for k, (r, o) in enumerate(zip(rs, outs)):
            mine = _half(c, r.shape[0])
            cps.append(_remote(r.at[mine], o.at[mine], send_sems, recv_sems, k, sib))
            cps[-1].start()
        for k, (r, o) in enumerate(zip(rs, outs)):
            theirs = _half(1 - c, r.shape[0])
            _remote(r.at[theirs], o.at[theirs], send_sems, recv_sems, k, sib).wait_recv()
        for cp in cps:
            cp.wait_send()

    return pl.pallas_call(
        body, name=name, in_specs=[ANY] * n, out_specs=[ANY] * n,
        out_shape=[jax.ShapeDtypeStruct(r.shape, r.dtype) for r in items],
        input_output_aliases={k: k for k in range(n)},
        scratch_shapes=[pltpu.SemaphoreType.DMA((n,)), pltpu.SemaphoreType.DMA((n,))],
    )(*items)


def gather_all(v, *, name, total):
    rows, W = v.shape

    def body(v_ref, o_ref, *scr):
        buf = scr[0] if total else o_ref
        send_sems, recv_sems = scr[-2], scr[-1]
        x, y, c = _place()
        me = 4 * x + 2 * y + c
        flips = [(k >> 2 & 1, k >> 1 & 1, k & 1) for k in range(1, 8)]
        peers = [((1 - x) if fx else x, (1 - y) if fy else y, (1 - c) if fc else c) for fx, fy, fc in flips]
        out = []
        for k, peer in enumerate(peers):
            cp = pltpu.make_async_remote_copy(src_ref=v_ref, dst_ref=buf.at[me], send_sem=send_sems.at[k],
                                              recv_sem=recv_sems.at[k], device_id=peer, device_id_type=MESH)
            cp.start()
            out.append(cp)
        buf[me] = v_ref[...]
        for k, (px, py, pc) in enumerate(peers):
            pltpu.make_async_remote_copy(src_ref=v_ref, dst_ref=buf.at[4 * px + 2 * py + pc], send_sem=send_sems.at[k],
                                         recv_sem=recv_sems.at[k], device_id=(px, py, pc), device_id_type=MESH).wait_recv()
        for cp in out:
            cp.wait_send()
        if total:
            acc = buf[0]
            for d in range(1, 8):
                acc = acc + buf[d]
            o_ref[...] = acc

    vm = pl.BlockSpec(memory_space=pltpu.VMEM)
    return pl.pallas_call(
        body, name=name, in_specs=[vm], out_specs=vm,
        out_shape=jax.ShapeDtypeStruct((rows, W) if total else (8, rows, W), F32),
        scratch_shapes=([pltpu.VMEM((8, rows, W), F32)] if total else [])
        + [pltpu.SemaphoreType.DMA((7,)), pltpu.SemaphoreType.DMA((7,))],
    )(v)


def pair_sum(g, r, *, name):
    _, R, W = g.shape
    Rh = R // 2
    tr = _pick(Rh, (512, 256, 128, 64, 32, 16))
    nb = Rh // tr

    def body(g_ref, r_ref, o_ref):
        o_ref[...] = (g_ref[...].astype(F32) + r_ref[...].astype(F32)).astype(o_ref.dtype)

    return pl.pallas_call(
        body, name=name, grid=(4, nb),
        in_specs=[pl.BlockSpec((1, tr, W), lambda s, i: (s, lax.axis_index("c") * nb + i, 0)),
                  pl.BlockSpec((1, tr, W), lambda s, i: (s, i, 0))],
        out_specs=pl.BlockSpec((1, tr, W), lambda s, i: (s, i, 0)),
        out_shape=jax.ShapeDtypeStruct((4, Rh, W), g.dtype), compiler_params=_params("parallel", "parallel"),
    )(g, r)


def chip_sum(g, r1, r2, *, name):
    _, R, W = g.shape
    Rh = R // 2
    tr = _pick(Rh, (512, 256, 128, 64, 32, 16))
    nb = Rh // tr

    def body(g_ref, r1_ref, a_ref, b_ref, c_ref, o_ref):
        acc = g_ref[0].astype(F32) + r1_ref[0].astype(F32)
        for ref in (a_ref, b_ref, c_ref):
            acc = acc + ref[0].astype(F32)
        o_ref[...] = acc

    shard = lambda: 2 * lax.axis_index("x") + lax.axis_index("y")
    half = lambda i: lax.axis_index("c") * nb + i
    return pl.pallas_call(
        body, name=name, grid=(nb,),
        in_specs=[pl.BlockSpec((1, tr, W), lambda i: (shard(), half(i), 0)),
                  pl.BlockSpec((1, tr, W), lambda i: (shard(), i, 0))]
        + [pl.BlockSpec((1, tr, W), lambda i, j=j: (j, i, 0)) for j in range(3)],
        out_specs=pl.BlockSpec((tr, W), lambda i: (half(i), 0)),
        out_shape=jax.ShapeDtypeStruct((R, W), F32), compiler_params=_params("parallel"),
    )(g, r1, r2, r2, r2)


BIG = (("w_in", False), ("pool_w", False), ("w_attn_br", False), ("w_pool_br", False), ("w_ssm_br", True),
       ("w_out", True), ("w_gate_up", False), ("w_down", True))


def _join(piece, axis):
    t = jnp.moveaxis(piece, 0, axis)
    shp = t.shape
    return t.reshape(shp[:axis] + (shp[axis] * shp[axis + 1],) + shp[axis + 2:])


def _seg_bounds():
    aw, kw, _, pw, di, cc, h, gd = C.in_widths
    o = [0, aw + 2 * kw]
    for wdt in (pw, di, cc, h, gd):
        o.append(o[-1] + wdt)
    return o


IN_PAD = 640


def _in_segments(blocks):
    w_in = jnp.concatenate([blocks[s] for s in range(4)], axis=1)
    o = _seg_bounds()
    segs = [w_in[:, o[i]:o[i + 1]] for i in range(6)]
    segs[4] = jnp.pad(segs[4], ((0, 0), (0, 128 - C.H)))
    width = sum(t.shape[1] for t in segs)
    pad = -width % IN_PAD
    return segs, jnp.concatenate(segs + [jnp.zeros((w_in.shape[0], pad), w_in.dtype)], axis=1)


def _layer_fwd(x, p, tag):
    nm = lambda s: f"{s}_{tag}"
    H = C.H
    h = rms_fwd(x, p["ln1_w"], name=nm("rms1"))
    wq, wu, wz, wx, wd, wg = p["in_segs"]
    qkv = matmul(h, wq, name=nm("mm_qkv"), out_dtype=MXU)
    u = matmul(h, wu, name=nm("mm_u"))
    z = matmul(h, wz, name=nm("mm_z"))
    xbc = matmul(h, wx, name=nm("mm_xbc"))
    dtp = matmul(h, wd, name=nm("mm_dt"))
    gl = matmul(h, wg, name=nm("mm_gate"))
    att = attn_fwd(qkv, p["attn_sink"], name=nm("attn_fwd"))
    pool, mixed = pool_fwd(u, p["pool_w"], p["pool_scale"], name=nm("pool_fwd"))
    xa = conv_fwd(xbc, p["conv_w"], p["conv_b"], name=nm("conv_fwd"))
    raw = dtp[:, :H]
    dt_t, acs_t = ssd_prep(raw.T, p["dt_bias"].T, p["a_log"].T, name=nm("ssd_prep"))
    dt, acs = dt_t.T, acs_t.T
    y, hprev = ssd_fwd(xa, dt, acs, acs_t, p["d_skip"], name=nm("ssd_fwd"))
    ssm = gnorm_fwd(y, z, p["ssm_norm_w"], name=nm("gnorm_fwd"))
    abr = matmul(att, name=nm("mm_abr"), **p["w_attn_br"])
    pbr = matmul(pool, name=nm("mm_pbr"), **p["w_pool_br"])
    sbr = matmul(ssm, name=nm("mm_sbr"), **p["w_ssm_br"])
    merged = merge_fwd(gl, abr, pbr, sbr, name=nm("merge_fwd"))
    xm = matmul(merged, add=x, name=nm("mm_out"), **p["w_out"])
    h2 = rms_fwd(xm, p["ln2_w"], name=nm("rms2"))
    gu = matmul(h2, name=nm("mm_gu"), **p["w_gate_up"])
    act = swiglu_fwd(gu, name=nm("swiglu_fwd"))
    xo = matmul(act, add=xm, name=nm("mm_down"), **p["w_down"])
    saved = dict(x=x, h=h, qkv=qkv, z=z, xbc=xbc, raw=raw, gl=gl, att=att, pool=pool, mixed=mixed, xa=xa, dt=dt, acs=acs,
                 acs_t=acs_t, y=y, hprev=hprev, ssm=ssm, abr=abr, pbr=pbr, sbr=sbr, merged=merged, xm=xm, h2=h2, gu=gu,
                 act=act)
    return xo, saved


def _layer_bwd(dxo, dxo_m, p, s, tag):
    nm = lambda t: f"{t}_{tag}"
    H = C.H
    g = {}

    def rows4(t):
        return t.reshape(4, t.shape[0] // 4, t.shape[1])

    g["w_down"] = rows4(matmul(s["act"], dxo_m, ta=True, out_dtype=MXU, name=nm("mmg_down")))
    dact = matmul(dxo_m, tb=True, name=nm("mmb_down"), **p["w_down"])
    dgu = swiglu_bwd(s["gu"], dact, name=nm("swiglu_bwd"))
    g["w_gate_up"] = matmul(s["h2"], dgu, ta=True, out_dtype=MXU, out_cols=True, name=nm("mmg_gu"))
    dh2 = matmul(dgu, tb=True, name=nm("mmb_gu"), **p["w_gate_up"])
    dxm, dxm_m, g["ln2_w"] = rms_bwd(s["xm"], p["ln2_w"], dh2, dxo, name=nm("rms2_bwd"))
    g["w_out"] = rows4(matmul(s["merged"], dxm_m, ta=True, out_dtype=MXU, name=nm("mmg_out")))
    dmerged = matmul(dxm_m, tb=True, name=nm("mmb_out"), **p["w_out"])
    dabr, dpbr, dsbr, dgl = merge_bwd(s["gl"], s["abr"], s["pbr"], s["sbr"], dmerged, name=nm("merge_bwd"))
    g["w_attn_br"] = matmul(s["att"], dabr, ta=True, out_dtype=MXU, out_cols=True, name=nm("mmg_abr"))
    g["w_pool_br"] = matmul(s["pool"], dpbr, ta=True, out_dtype=MXU, out_cols=True, name=nm("mmg_pbr"))
    g["w_ssm_br"] = rows4(matmul(s["ssm"], dsbr, ta=True, out_dtype=MXU, name=nm("mmg_sbr")))
    datt = matmul(dabr, tb=True, name=nm("mmb_abr"), **p["w_attn_br"])
    dpool = matmul(dpbr, tb=True, name=nm("mmb_pbr"), **p["w_pool_br"])
    dssm = matmul(dsbr, tb=True, name=nm("mmb_sbr"), **p["w_ssm_br"])
    dq, dk, dv, g["attn_sink"] = attn_bwd(s["qkv"], p["attn_sink"], datt, name=nm("attn_bwd"))
    du, dpw, g["pool_scale"] = pool_bwd(dpool, s["mixed"], p["pool_w"], p["pool_scale"], name=nm("pool_bwd"))
    pg = dpw.shape[1] // 4
    g["pool_w"] = jnp.moveaxis(dpw.reshape(4, 4, pg, dpw.shape[2]), 1, 0).reshape(4, 4 * pg, dpw.shape[2]).astype(MXU)
    dy, dz, g["ssm_norm_w"] = gnorm_bwd(s["y"], s["z"], p["ssm_norm_w"], dssm, name=nm("gnorm_bwd"))
    dxa, ddt, dacs, g["d_skip"] = ssd_bwd(s["xa"], s["dt"], s["acs"], s["acs_t"], p["d_skip"], s["hprev"], dy,
                                          name=nm("ssd_bwd"))
    draw, g["dt_bias"], g["a_log"] = ssd_post(ddt, dacs, s["dt"], s["raw"], p["dt_bias"], p["a_log"], name=nm("ssd_post"))
    dxbc, g["conv_w"], g["conv_b"] = conv_bwd(s["xbc"], p["conv_w"], p["conv_b"], dxa, name=nm("conv_bwd"))
    w_all = p["in_all"]
    ddtp = jnp.pad(draw, ((0, 0), (0, 128 - H))).astype(MXU)
    parts = [dq, dk, dv, du, dz, dxbc, ddtp, dgl]
    used = sum(t.shape[1] for t in parts)
    dproj = jnp.concatenate(parts + [jnp.zeros((dq.shape[0], w_all.shape[1] - used), MXU)], axis=1)
    dh = matmul(dproj, w_all, tb=True, name=nm("mmb_in"))
    g_all = matmul(s["h"], dproj, ta=True, out_dtype=MXU, name=nm("mmg_in"))
    o = _seg_bounds()
    dt0 = o[4]
    g_in = jnp.concatenate([g_all[:, :dt0 + H], g_all[:, dt0 + 128:dt0 + 128 + (o[6] - o[5])]], axis=1)
    nc = g_in.shape[1] // 4
    g["w_in"] = jnp.stack([g_in[:, k * nc:(k + 1) * nc] for k in range(4)])
    dx, dx_m, g["ln1_w"] = rms_bwd(s["x"], p["ln1_w"], dh, dxm, name=nm("rms1_bwd"))
    return dx, dx_m, g


SMALL = ("ln1_w", "attn_sink", "conv_w", "conv_b", "dt_bias", "a_log", "d_skip", "ssm_norm_w", "pool_scale", "ln2_w")
WEIGHTS = ("ln1_w", "w_in", "attn_sink", "conv_w", "conv_b", "dt_bias", "a_log", "d_skip", "ssm_norm_w", "pool_w",
           "pool_scale", "w_attn_br", "w_pool_br", "w_ssm_br", "w_out", "ln2_w", "w_gate_up", "w_down", "final_w")


def _step(x, loss_target, w, m, v):
    depth = C.depth
    xi, yi, ci = _place()
    shard = 2 * xi + yi

    gathered = gather_weights([w[n].astype(MXU) for n, _ in BIG], [rw for _, rw in BIG], name="gather_weights")
    full = dict(zip([n for n, _ in BIG], gathered))
    cw_all = gather_all(_pack_small([w["conv_w"].reshape(1, -1)]), name="gather_conv_w", total=False)
    cw_shards = cw_all[0::2].reshape(4, -1)[:, :w["conv_w"].size].reshape((4,) + w["conv_w"].shape)
    conv_w = _join(cw_shards, 2)

    layers = []
    for i in range(depth):
        p = {n: w[n][i][None] for n in SMALL if n != "conv_w"}
        p["conv_w"] = conv_w[i]
        p["in_segs"], p["in_all"] = _in_segments(full["w_in"][:, i])
        pw = full["pool_w"][:, i]
        p["pool_w"] = _join(pw, 1)
        for n, rw in BIG[2:]:
            if rw:
                t = full[n]
                p[n] = dict(b=t.reshape(t.shape[0], 4 * t.shape[2], t.shape[3]), layer=i)
            else:
                p[n] = dict(b=full[n], layer=i, b_cols=True)
        layers.append(p)

    xs = x[0]
    saved = []
    for i in range(depth):
        xs, sv = _layer_fwd(xs, layers[i], f"l{i}")
        saved.append(sv)
    loss_part, dx, dx_m, g_final = final_loss(xs, w["final_w"][None], loss_target[0], name="final_loss")
    grads = [None] * depth
    for i in reversed(range(depth)):
        dx, dx_m, grads[i] = _layer_bwd(dx, dx_m, layers[i], saved[i], f"l{i}")

    keys = [(i, n) for i in range(depth) for n, _ in BIG]
    items = [grads[i][n] for i, n in keys]
    sib = swap_halves(items, name="swap_halves")
    chip = [pair_sum(g, r, name=f"pair_sum_{n}_l{i}") for (i, n), g, r in zip(keys, items, sib)]
    others = scatter_chips(chip, name="scatter_chips")
    mine = [chip_sum(g, r, o, name=f"chip_sum_{n}_l{i}") for (i, n), g, r, o in zip(keys, items, sib, others)]
    reduced = dict(zip(keys, join_halves(mine, name="join_halves")))
    gout = {}

    small = [jnp.stack([grads[i][n].reshape(-1) for i in range(depth)]).reshape(1, -1) for n in SMALL]
    small += [g_final.reshape(1, -1), loss_part.reshape(1, -1)]
    tot = gather_all(_pack_small(small), name="sum_small", total=True)
    parts = _unpack_small(tot, [t.shape[1] for t in small])
    for n, t in zip(SMALL, parts):
        if n == "conv_w":
            cols = w[n].shape[2]
            gout[n] = lax.dynamic_slice_in_dim(t.reshape(depth, CONV_K, -1), shard * cols, cols, axis=2)
        else:
            gout[n] = t.reshape(w[n].shape)
    gout["final_w"] = parts[-2].reshape(w["final_w"].shape)
    loss = parts[-1].reshape(())

    upd = {}
    for n in WEIGHTS:
        if n in gout:
            upd[n] = (gout[n].reshape(w[n].shape),) + adamw(w[n], gout[n].reshape(w[n].shape), m[n], v[n], name=f"adamw_{n}")
        else:
            upd[n] = adamw_layers(w[n], [reduced[(i, n)] for i in range(depth)], m[n], v[n], name=f"adamw_{n}")
    return (loss, dx[None], *[upd[n][0] for n in WEIGHTS], *[upd[n][1] for n in WEIGHTS],
            *[upd[n][2] for n in WEIGHTS], *[upd[n][3] for n in WEIGHTS])


def _pack_small(parts):
    flat = jnp.concatenate(parts, axis=1)
    rows = -(-flat.shape[1] // PACK_W)
    rows = -(-rows // 8) * 8
    return jnp.pad(flat, ((0, 0), (0, rows * PACK_W - flat.shape[1]))).reshape(rows, PACK_W)


def _unpack_small(buf, sizes):
    flat = buf.reshape(-1)
    out, off = [], 0
    for n in sizes:
        out.append(flat[off:off + n])
        off += n
    return out


def kernel(x, ln1_w, w_in, attn_sink, conv_w, conv_b, dt_bias, a_log, d_skip, ssm_norm_w, pool_w, pool_scale, w_attn_br, w_pool_br, w_ssm_br, w_out, ln2_w, w_gate_up, w_down, final_w, loss_target, m_ln1_w, m_w_in, m_attn_sink, m_conv_w, m_conv_b, m_dt_bias, m_a_log, m_d_skip, m_ssm_norm_w, m_pool_w, m_pool_scale, m_w_attn_br, m_w_pool_br, m_w_ssm_br, m_w_out, m_ln2_w, m_w_gate_up, m_w_down, m_final_w, v_ln1_w, v_w_in, v_attn_sink, v_conv_w, v_conv_b, v_dt_bias, v_a_log, v_d_skip, v_ssm_norm_w, v_pool_w, v_pool_scale, v_w_attn_br, v_w_pool_br, v_w_ssm_br, v_w_out, v_ln2_w, v_w_gate_up, v_w_down, v_final_w):
    w = dict(ln1_w=ln1_w, w_in=w_in, attn_sink=attn_sink, conv_w=conv_w, conv_b=conv_b, dt_bias=dt_bias, a_log=a_log,
             d_skip=d_skip, ssm_norm_w=ssm_norm_w, pool_w=pool_w, pool_scale=pool_scale, w_attn_br=w_attn_br,
             w_pool_br=w_pool_br, w_ssm_br=w_ssm_br, w_out=w_out, ln2_w=ln2_w, w_gate_up=w_gate_up, w_down=w_down,
             final_w=final_w)
    m = dict(ln1_w=m_ln1_w, w_in=m_w_in, attn_sink=m_attn_sink, conv_w=m_conv_w, conv_b=m_conv_b, dt_bias=m_dt_bias,
             a_log=m_a_log, d_skip=m_d_skip, ssm_norm_w=m_ssm_norm_w, pool_w=m_pool_w, pool_scale=m_pool_scale,
             w_attn_br=m_w_attn_br, w_pool_br=m_w_pool_br, w_ssm_br=m_w_ssm_br, w_out=m_w_out, ln2_w=m_ln2_w,
             w_gate_up=m_w_gate_up, w_down=m_w_down, final_w=m_final_w)
    v = dict(ln1_w=v_ln1_w, w_in=v_w_in, attn_sink=v_attn_sink, conv_w=v_conv_w, conv_b=v_conv_b, dt_bias=v_dt_bias,
             a_log=v_a_log, d_skip=v_d_skip, ssm_norm_w=v_ssm_norm_w, pool_w=v_pool_w, pool_scale=v_pool_scale,
             w_attn_br=v_w_attn_br, w_pool_br=v_w_pool_br, w_ssm_br=v_w_ssm_br, w_out=v_w_out, ln2_w=v_ln2_w,
             w_gate_up=v_w_gate_up, w_down=v_w_down, final_w=v_final_w)
    return _step(x, loss_target, w, m, v)
```

```python
import functools

import jax
import jax.numpy as jnp
from jax import lax
from jax.experimental import pallas as pl
from jax.experimental.pallas import tpu as pltpu

F32 = jnp.float32
MXU = jnp.bfloat16
VMEM_LIMIT = 56 * 1024 * 1024
EPS = 1e-6
NEG = -1e30

ADAM_LR, ADAM_B1, ADAM_B2, ADAM_EPS, ADAM_WD, ADAM_STEP = 0.001, 0.9, 0.999, 1e-08, 0.01, 10


class Cfg:
    def __init__(self, d_model=2048, seq=8192, depth=2, q_heads=16, kv_heads=4, head_dim=64,
                 ssm_head_dim=64, ssm_groups=4, d_state=128):
        self.D, self.T, self.depth = d_model, seq, depth
        self.hd, self.qh, self.kvh = head_dim, q_heads, kv_heads
        self.AW, self.KW = q_heads * head_dim, kv_heads * head_dim
        self.blk = 128
        self.PW = d_model // 2
        self.PG = self.PW // 4
        self.DI = d_model
        self.P = ssm_head_dim
        self.H = self.DI // self.P
        self.G = ssm_groups
        self.HG = self.H // self.G
        self.N = d_state
        self.CC = self.DI + 2 * self.G * self.N
        self.F = -(-8 * d_model // (3 * 256)) * 256
        self.in_widths = (self.AW, self.KW, self.KW, self.PW, self.DI, self.CC, self.H, 3 * d_model)
        self.in_cols = sum(self.in_widths)


C = Cfg()
POOL_WINDOWS = (2, 4, 8, 16)


def _pick(n, cands):
    for c in cands:
        if n % c == 0:
            return c
    return n


def _params(*sem):
    return pltpu.CompilerParams(dimension_semantics=sem, vmem_limit_bytes=VMEM_LIMIT)


def _sigmoid(x):
    return 1.0 / (1.0 + jnp.exp(-x))


def _dot(a, b, dims):
    return lax.dot_general(a.astype(MXU), b.astype(MXU), (dims, ((), ())), preferred_element_type=F32)


def _dot_nn(a, b):
    return _dot(a, b, ((1,), (0,)))


def _dot_nt(a, b):
    return _dot(a, b, ((1,), (1,)))


def _dot_tn(a, b):
    return _dot(a, b, ((0,), (0,)))


MM_VMEM = 40 * 1024 * 1024
LANES = 128


def _tile(n, cap):
    best = None
    for d in range(LANES, min(n, cap) + 1, LANES):
        if n % d == 0:
            best = d
    return n if best is None else best


def _divisors(n, cap):
    ds = [d for d in range(LANES, min(n, cap) + 1, LANES) if n % d == 0]
    return ds or [n]


HBM_RATE, MXU_RATE, ACC_RATE, STEP_COST = 3.0e12, 0.9e15, 1.2e13, 0.35e-6
MXU_WIDTH = 256


def _choose_tiles(M, N, K, n_unit, k_unit, sa, sb, so, sadd, ta):
    best = None
    for tm in sorted({_tile(M, 1024), _tile(M, 512)}):
        for tk in _divisors(k_unit, 4096):
            for tn in _divisors(n_unit, 2048):
                nk = K // tk
                vmem = 2 * (tm * tk * sa + tk * tn * sb + tm * tn * (so + sadd)) + (tm * tn * 4 if nk > 1 else 0)
                if vmem > MM_VMEM:
                    continue
                steps = (M // tm) * (N // tn) * nk
                a_bytes = M * K * sa * (N // tn if (nk > 1 or ta) else 1)
                b_bytes = K * N * sb * (M // tm)
                hbm = (a_bytes + b_bytes + M * N * (so + sadd)) / HBM_RATE
                fill = (tn / (-(-tn // MXU_WIDTH) * MXU_WIDTH)) * (tk / (-(-tk // MXU_WIDTH) * MXU_WIDTH))
                mxu = 2.0 * M * N * K / (MXU_RATE * fill)
                acc = steps * tm * tn * 8 / ACC_RATE if nk > 1 else 0.0
                cost = max(hbm, mxu) + acc + steps * STEP_COST
                if best is None or cost < best[0]:
                    best = (cost, tm, tn, tk)
    assert best is not None, (M, N, K)
    return best[1:]


def matmul(a, b, *, name, ta=False, tb=False, out_dtype=F32, add=None, layer=None, b_cols=False, out_cols=False):
    M, K = (a.shape[1], a.shape[0]) if ta else a.shape
    rows, cols = b.shape[-2], (4 if b_cols else 1) * b.shape[-1]
    N, bk = (rows, cols) if tb else (cols, rows)
    assert K == bk, (a.shape, b.shape)
    n_unit = N // 4 if (out_cols or (b_cols and not tb)) else N
    k_unit = K // 4 if (b_cols and tb) else K
    sa, sb, so = a.dtype.itemsize, b.dtype.itemsize, jnp.dtype(out_dtype).itemsize
    sadd = add.dtype.itemsize if add is not None else 0
    tm, tn, tk = _choose_tiles(M, N, K, n_unit, k_unit, sa, sb, so, sadd, ta)
    nk = K // tk
    a_spec = pl.BlockSpec((tk, tm), lambda i, j, k: (k, i)) if ta else pl.BlockSpec((tm, tk), lambda i, j, k: (i, k))
    if b_cols and tb:
        per = k_unit // tk
        b_spec = pl.BlockSpec((None, None, tn, tk), lambda i, j, k: (k // per, layer, j, k % per))
    elif b_cols:
        per = n_unit // tn
        b_spec = pl.BlockSpec((None, None, tk, tn), lambda i, j, k: (j // per, layer, k, j % per))
    elif layer is not None:
        b_spec = (pl.BlockSpec((None, tn, tk), lambda i, j, k: (layer, j, k)) if tb
                  else pl.BlockSpec((None, tk, tn), lambda i, j, k: (layer, k, j)))
    else:
        b_spec = pl.BlockSpec((tn, tk), lambda i, j, k: (j, k)) if tb else pl.BlockSpec((tk, tn), lambda i, j, k: (k, j))
    if out_cols:
        per = n_unit // tn
        o_spec = pl.BlockSpec((None, tm, tn), lambda i, j, k: (j // per, i, j % per))
        out_shape = jax.ShapeDtypeStruct((4, M, N // 4), out_dtype)
    else:
        o_spec = pl.BlockSpec((tm, tn), lambda i, j, k: (i, j))
        out_shape = jax.ShapeDtypeStruct((M, N), out_dtype)
    dims = ((0 if ta else 1,), (1 if tb else 0,))
    has_add = add is not None

    def body(*refs):
        a_ref, b_ref = refs[0], refs[1]
        add_ref = refs[2] if has_add else None
        o_ref = refs[3] if has_add else refs[2]
        part = _dot(a_ref[...], b_ref[...], dims)

        def finish(total):
            if has_add:
                total = total + add_ref[...].astype(F32)
            o_ref[...] = total.astype(o_ref.dtype)

        if nk == 1:
            finish(part)
        else:
            acc_ref = refs[-1]
            k = pl.program_id(2)

            @pl.when(k == 0)
            def _():
                acc_ref[...] = part

            @pl.when(k > 0)
            def _():
                acc_ref[...] += part

            @pl.when(k == nk - 1)
            def _():
                finish(acc_ref[...])

    in_specs = [a_spec, b_spec] + ([o_spec] if has_add else [])
    args = (a, b) + ((add,) if has_add else ())
    return pl.pallas_call(
        body, name=name, grid=(M // tm, N // tn, nk), in_specs=in_specs, out_specs=o_spec, out_shape=out_shape,
        scratch_shapes=[pltpu.VMEM((tm, tn), F32)] if nk > 1 else [],
        compiler_params=_params("parallel", "parallel", "arbitrary"),
    )(*args)


def _row_tile(t):
    return _pick(t, (256, 128, 64, 32, 16, 8))


def rms_fwd(x, w, *, name):
    T, D = x.shape
    tr = _row_tile(T)

    def body(x_ref, w_ref, o_ref):
        xv = x_ref[...]
        r = lax.rsqrt(jnp.mean(xv * xv, axis=-1, keepdims=True) + EPS)
        o_ref[...] = (xv * r * w_ref[...]).astype(o_ref.dtype)

    row = pl.BlockSpec((tr, D), lambda i: (i, 0))
    return pl.pallas_call(
        body, name=name, grid=(T // tr,), in_specs=[row, pl.BlockSpec((1, D), lambda i: (0, 0))], out_specs=row,
        out_shape=jax.ShapeDtypeStruct((T, D), MXU), compiler_params=_params("parallel"),
    )(x, w)


def rms_bwd(x, w, dh, dres, *, name):
    T, D = x.shape
    tr = _row_tile(T)

    def body(x_ref, w_ref, dh_ref, dres_ref, dx_ref, dxm_ref, dw_ref):
        xv = x_ref[...]
        dhv = dh_ref[...].astype(F32)
        r = lax.rsqrt(jnp.mean(xv * xv, axis=-1, keepdims=True) + EPS)
        g = dhv * w_ref[...]
        dot = jnp.mean(g * xv, axis=-1, keepdims=True)
        dx = dres_ref[...] + r * g - xv * (r * r * r * dot)
        dx_ref[...] = dx
        dxm_ref[...] = dx.astype(dxm_ref.dtype)
        part = jnp.sum(dhv * xv * r, axis=0, keepdims=True)

        @pl.when(pl.program_id(0) == 0)
        def _():
            dw_ref[...] = part

        @pl.when(pl.program_id(0) > 0)
        def _():
            dw_ref[...] += part

    row = pl.BlockSpec((tr, D), lambda i: (i, 0))
    vec = pl.BlockSpec((1, D), lambda i: (0, 0))
    return pl.pallas_call(
        body, name=name, grid=(T // tr,), in_specs=[row, vec, row, row], out_specs=[row, row, vec],
        out_shape=[jax.ShapeDtypeStruct((T, D), F32), jax.ShapeDtypeStruct((T, D), MXU),
                   jax.ShapeDtypeStruct((1, D), F32)],
        compiler_params=_params("arbitrary"),
    )(x, w, dh, dres)


def final_loss(x, w, target, *, name):
    T, D = x.shape
    tr = _row_tile(T)

    def body(x_ref, w_ref, t_ref, loss_ref, dx_ref, dxm_ref, dw_ref):
        xv = x_ref[...]
        wv = w_ref[...]
        r = lax.rsqrt(jnp.mean(xv * xv, axis=-1, keepdims=True) + EPS)
        err = xv * r * wv - t_ref[...]
        lpart = 0.5 * jnp.sum(jnp.mean(err * err, axis=-1, keepdims=True), axis=0, keepdims=True)
        dy = err * (1.0 / D)
        g = dy * wv
        dot = jnp.mean(g * xv, axis=-1, keepdims=True)
        dx = r * g - xv * (r * r * r * dot)
        dx_ref[...] = dx
        dxm_ref[...] = dx.astype(dxm_ref.dtype)
        part = jnp.sum(dy * xv * r, axis=0, keepdims=True)

        @pl.when(pl.program_id(0) == 0)
        def _():
            dw_ref[...] = part
            loss_ref[...] = lpart

        @pl.when(pl.program_id(0) > 0)
        def _():
            dw_ref[...] += part
            loss_ref[...] += lpart

    row = pl.BlockSpec((tr, D), lambda i: (i, 0))
    vec = pl.BlockSpec((1, D), lambda i: (0, 0))
    one = pl.BlockSpec((1, 1), lambda i: (0, 0))
    return pl.pallas_call(
        body, name=name, grid=(T // tr,), in_specs=[row, vec, row], out_specs=[one, row, row, vec],
        out_shape=[jax.ShapeDtypeStruct((1, 1), F32), jax.ShapeDtypeStruct((T, D), F32),
                   jax.ShapeDtypeStruct((T, D), MXU), jax.ShapeDtypeStruct((1, D), F32)],
        compiler_params=_params("arbitrary"),
    )(x, w, target)


def swiglu_fwd(gu, *, name):
    T, F2 = gu.shape
    F = F2 // 2
    tr = _row_tile(T)

    def body(g_ref, u_ref, o_ref):
        g = g_ref[...].astype(F32)
        o_ref[...] = (g * _sigmoid(g) * u_ref[...].astype(F32)).astype(o_ref.dtype)

    return pl.pallas_call(
        body, name=name, grid=(T // tr,),
        in_specs=[pl.BlockSpec((tr, F), lambda i: (i, 0)), pl.BlockSpec((tr, F), lambda i: (i, 1))],
        out_specs=pl.BlockSpec((tr, F), lambda i: (i, 0)),
        out_shape=jax.ShapeDtypeStruct((T, F), MXU), compiler_params=_params("parallel"),
    )(gu, gu)


def swiglu_bwd(gu, dact, *, name):
    T, F2 = gu.shape
    F = F2 // 2
    tr = _row_tile(T)

    def body(g_ref, u_ref, d_ref, o_ref):
        g = g_ref[...].astype(F32)
        d = d_ref[...].astype(F32)
        s = _sigmoid(g)
        o_ref[:, :F] = (d * u_ref[...].astype(F32) * s * (1.0 + g * (1.0 - s))).astype(o_ref.dtype)
        o_ref[:, F:] = (d * g * s).astype(o_ref.dtype)

    lo = pl.BlockSpec((tr, F), lambda i: (i, 0))
    hi = pl.BlockSpec((tr, F), lambda i: (i, 1))
    return pl.pallas_call(
        body, name=name, grid=(T // tr,), in_specs=[lo, hi, lo], out_specs=pl.BlockSpec((tr, F2), lambda i: (i, 0)),
        out_shape=jax.ShapeDtypeStruct((T, F2), MXU), compiler_params=_params("parallel"),
    )(gu, gu, dact)


def merge_fwd(gl, abr, pbr, sbr, *, name):
    T, D = abr.shape
    tr = _row_tile(T)

    def body(g0, g1, g2, a_ref, p_ref, s_ref, o_ref):
        m = sum(_sigmoid(g[...].astype(F32)) * b[...].astype(F32) for g, b in ((g0, a_ref), (g1, p_ref), (g2, s_ref)))
        o_ref[...] = m.astype(o_ref.dtype)

    row = pl.BlockSpec((tr, D), lambda i: (i, 0))
    gs = [pl.BlockSpec((tr, D), lambda i, j=j: (i, j)) for j in range(3)]
    return pl.pallas_call(
        body, name=name, grid=(T // tr,), in_specs=gs + [row, row, row], out_specs=row,
        out_shape=jax.ShapeDtypeStruct((T, D), MXU), compiler_params=_params("parallel"),
    )(gl, gl, gl, abr, pbr, sbr)


def merge_bwd(gl, abr, pbr, sbr, dm, *, name):
    T, D = abr.shape
    tr = _row_tile(T)

    def body(g0, g1, g2, a_ref, p_ref, s_ref, dm_ref, da_ref, dp_ref, ds_ref, dg_ref):
        d = dm_ref[...].astype(F32)
        for j, (g_ref, b_ref, db_ref) in enumerate(((g0, a_ref, da_ref), (g1, p_ref, dp_ref), (g2, s_ref, ds_ref))):
            s = _sigmoid(g_ref[...].astype(F32))
            db_ref[...] = (d * s).astype(db_ref.dtype)
            dg_ref[:, j * D:(j + 1) * D] = (d * b_ref[...].astype(F32) * s * (1.0 - s)).astype(dg_ref.dtype)

    row = pl.BlockSpec((tr, D), lambda i: (i, 0))
    gs = [pl.BlockSpec((tr, D), lambda i, j=j: (i, j)) for j in range(3)]
    return pl.pallas_call(
        body, name=name, grid=(T // tr,), in_specs=gs + [row] * 4,
        out_specs=[row] * 3 + [pl.BlockSpec((tr, 3 * D), lambda i: (i, 0))],
        out_shape=[jax.ShapeDtypeStruct((T, D), MXU)] * 3 + [jax.ShapeDtypeStruct((T, 3 * D), MXU)],
        compiler_params=_params("parallel"),
    )(gl, gl, gl, abr, pbr, sbr, dm)


def gnorm_fwd(y, z, w, *, name):
    T, DI = y.shape
    gw = DI // C.G
    tr = _row_tile(T)

    def body(y_ref, z_ref, w_ref, o_ref):
        for g in range(C.G):
            sl = slice(g * gw, (g + 1) * gw)
            zz = z_ref[:, sl]
            v = y_ref[:, sl] * (zz * _sigmoid(zz))
            r = lax.rsqrt(jnp.mean(v * v, axis=-1, keepdims=True) + EPS)
            o_ref[:, sl] = (v * r * w_ref[:, sl]).astype(o_ref.dtype)

    row = pl.BlockSpec((tr, DI), lambda i: (i, 0))
    return pl.pallas_call(
        body, name=name, grid=(T // tr,), in_specs=[row, row, pl.BlockSpec((1, DI), lambda i: (0, 0))],
        out_specs=row, out_shape=jax.ShapeDtypeStruct((T, DI), MXU), compiler_params=_params("parallel"),
    )(y, z, w)


def gnorm_bwd(y, z, w, do, *, name):
    T, DI = y.shape
    gw = DI // C.G
    tr = _row_tile(T)

    def body(y_ref, z_ref, w_ref, do_ref, dy_ref, dz_ref, dw_ref):
        first = pl.program_id(0) == 0
        for g in range(C.G):
            sl = slice(g * gw, (g + 1) * gw)
            zz = z_ref[:, sl]
            yy = y_ref[:, sl]
            s = _sigmoid(zz)
            sz = zz * s
            v = yy * sz
            r = lax.rsqrt(jnp.mean(v * v, axis=-1, keepdims=True) + EPS)
            dov = do_ref[:, sl].astype(F32)
            gg = dov * w_ref[:, sl]
            dot = jnp.mean(gg * v, axis=-1, keepdims=True)
            dv = r * gg - v * (r * r * r * dot)
            dy_ref[:, sl] = dv * sz
            dz_ref[:, sl] = (dv * yy * s * (1.0 + zz * (1.0 - s))).astype(dz_ref.dtype)
            part = jnp.sum(dov * v * r, axis=0, keepdims=True)

            @pl.when(first)
            def _():
                dw_ref[:, sl] = part

            @pl.when(jnp.logical_not(first))
            def _():
                dw_ref[:, sl] += part

    row = pl.BlockSpec((tr, DI), lambda i: (i, 0))
    vec = pl.BlockSpec((1, DI), lambda i: (0, 0))
    return pl.pallas_call(
        body, name=name, grid=(T // tr,), in_specs=[row, row, vec, row], out_specs=[row, row, vec],
        out_shape=[jax.ShapeDtypeStruct((T, DI), F32), jax.ShapeDtypeStruct((T, DI), MXU),
                   jax.ShapeDtypeStruct((1, DI), F32)],
        compiler_params=_params("arbitrary"),
    )(y, z, w, do)


def adamw(w, g, m, v, *, name):
    shape = w.shape
    cols = shape[-1]
    rows = w.size // cols
    w2, g2, m2, v2 = (t.reshape(rows, cols) for t in (w, g, m, v))
    tr = rows if rows * cols * 4 <= (2 << 20) else _pick(rows, (512, 256, 128, 64, 32, 16, 8))
    while tr * cols * 4 > (2 << 20) and tr % 16 == 0:
        tr //= 2
    c1 = 1.0 - ADAM_B1 ** ADAM_STEP
    c2 = 1.0 - ADAM_B2 ** ADAM_STEP

    def body(w_ref, g_ref, m_ref, v_ref, d_ref, nm_ref, nv_ref):
        gv = g_ref[...]
        nm = ADAM_B1 * m_ref[...] + (1.0 - ADAM_B1) * gv
        nv = ADAM_B2 * v_ref[...] + (1.0 - ADAM_B2) * (gv * gv)
        d_ref[...] = -ADAM_LR * ((nm / c1) / (jnp.sqrt(nv / c2) + ADAM_EPS) + ADAM_WD * w_ref[...])
        nm_ref[...] = nm
        nv_ref[...] = nv

    row = pl.BlockSpec((tr, cols), lambda i: (i, 0))
    outs = pl.pallas_call(
        body, name=name, grid=(rows // tr,), in_specs=[row] * 4, out_specs=[row] * 3,
        out_shape=[jax.ShapeDtypeStruct((rows, cols), F32)] * 3, compiler_params=_params("parallel"),
    )(w2, g2, m2, v2)
    return tuple(o.reshape(shape) for o in outs)


def adamw_layers(w, g_layers, m, v, *, name):
    shape = w.shape
    L = shape[0]
    rows, cols = g_layers[0].shape
    w3, m3, v3 = (t.reshape(L, rows, cols) for t in (w, m, v))
    tr = _pick(rows, (512, 256, 128, 64, 32, 16, 8))
    while tr * cols * 4 > (2 << 20) and tr % 16 == 0:
        tr //= 2
    c1 = 1.0 - ADAM_B1 ** ADAM_STEP
    c2 = 1.0 - ADAM_B2 ** ADAM_STEP

    def body(*refs):
        w_ref, m_ref, v_ref = refs[0], refs[1], refs[2]
        g_refs = refs[3:3 + L]
        go_ref, d_ref, nm_ref, nv_ref = refs[3 + L:]
        layer = pl.program_id(0)
        gv = g_refs[0][...]
        for k in range(1, L):
            gv = jnp.where(layer == k, g_refs[k][...], gv)
        nm = ADAM_B1 * m_ref[...] + (1.0 - ADAM_B1) * gv
        nv = ADAM_B2 * v_ref[...] + (1.0 - ADAM_B2) * (gv * gv)
        d_ref[...] = -ADAM_LR * ((nm / c1) / (jnp.sqrt(nv / c2) + ADAM_EPS) + ADAM_WD * w_ref[...])
        go_ref[...] = gv
        nm_ref[...] = nm
        nv_ref[...] = nv

    blk = pl.BlockSpec((None, tr, cols), lambda l, i: (l, i, 0))
    gblks = [pl.BlockSpec((tr, cols), lambda l, i, k=k: (jnp.where(l == k, i, 0), 0)) for k in range(L)]
    outs = pl.pallas_call(
        body, name=name, grid=(L, rows // tr), in_specs=[blk] * 3 + gblks, out_specs=[blk] * 4,
        out_shape=[jax.ShapeDtypeStruct((L, rows, cols), F32)] * 4, compiler_params=_params("parallel", "parallel"),
    )(w3, m3, v3, *g_layers)
    return tuple(o.reshape(shape) for o in outs)


def _attn_masks(i, heads):
    B = C.blk
    row = lax.broadcasted_iota(jnp.int32, (heads * B, 2 * B), 0) & (B - 1)
    col = lax.broadcasted_iota(jnp.int32, (heads * B, 2 * B), 1)
    diff = row + B - col
    return (diff >= 0) & (diff < B) & ((col >= B) | (i > 0))


def _stack_heads(ref, kvh, ppk, lo):
    parts = []
    for pr in range(ppk):
        pair = kvh * ppk + pr
        qp = ref[:, pair * 128:(pair + 1) * 128].astype(F32)
        parts += [jnp.where(lo, qp, 0.0), jnp.where(lo, 0.0, qp)]
    return jnp.concatenate(parts, axis=0)


def _unstack_heads(ref, val, kvh, ppk, lo):
    B = C.blk
    for pr in range(ppk):
        pair = kvh * ppk + pr
        ref[:, pair * 128:(pair + 1) * 128] = jnp.where(lo, val[2 * pr * B:(2 * pr + 1) * B],
                                                        val[(2 * pr + 1) * B:(2 * pr + 2) * B]).astype(ref.dtype)


def _sink_scores(sink_ref, kvh, ppk):
    B = C.blk
    h0 = kvh * 2 * ppk
    return jnp.concatenate([jnp.full((B, 2 * B), sink_ref[0, h], F32) for h in range(h0, h0 + 2 * ppk)], axis=0)


def _kv2(prev_ref, cur_ref, m, lo):
    sl = slice(m * 128, (m + 1) * 128)
    slab = jnp.concatenate([prev_ref[:, sl], cur_ref[:, sl]], axis=0).astype(F32)
    slab = jnp.where(lax.broadcasted_iota(jnp.int32, (slab.shape[0], 1), 0) == 0, 0.0, slab)
    rolled = pltpu.roll(slab, 64, axis=1)
    return jnp.where(lo, slab, rolled), jnp.where(lo, rolled, slab)


def _attn_weights(q4, k2, mask, sink_scores):
    s = _dot_nt(q4, k2) * (C.hd ** -0.5)
    col0 = lax.broadcasted_iota(jnp.int32, (1, s.shape[1]), 1) == 0
    s = jnp.where(col0, sink_scores, jnp.where(mask, s, NEG))
    return jnp.exp(s - jnp.max(s, axis=-1, keepdims=True))


def _row_sums(p):
    return lax.dot_general(p, jnp.ones((p.shape[1], 128), F32), (((1,), (0,)), ((), ())),
                           precision=lax.Precision.HIGHEST, preferred_element_type=F32)


def attn_fwd(qkv, sink, *, name):
    T = qkv.shape[0]
    B, AW, KW = C.blk, C.AW, C.KW
    assert C.hd == 64 and KW % 128 == 0 and (C.qh // C.kvh) % 2 == 0
    nb = T // B
    kb = AW // KW
    ppk = C.qh // C.kvh // 2

    def body(q_ref, kc_ref, kp_ref, vc_ref, vp_ref, sink_ref, o_ref):
        i = pl.program_id(0)
        lo = lax.broadcasted_iota(jnp.int32, (1, 128), 1) < 64
        mask = _attn_masks(i, 2 * ppk)
        for m in range(KW // 128):
            k2s = _kv2(kp_ref, kc_ref, m, lo)
            v2s = _kv2(vp_ref, vc_ref, m, lo)
            for par in range(2):
                kvh = 2 * m + par
                q4 = _stack_heads(q_ref, kvh, ppk, lo)
                p = _attn_weights(q4, k2s[par], mask, _sink_scores(sink_ref, kvh, ppk))
                _unstack_heads(o_ref, _dot_nn(p, v2s[par]) / _row_sums(p), kvh, ppk, lo)

    prev = lambda i: jnp.maximum(i - 1, 0)
    return pl.pallas_call(
        body, name=name, grid=(nb,),
        in_specs=[pl.BlockSpec((B, AW), lambda i: (i, 0)),
                  pl.BlockSpec((B, KW), lambda i: (i, kb)), pl.BlockSpec((B, KW), lambda i: (prev(i), kb)),
                  pl.BlockSpec((B, KW), lambda i: (i, kb + 1)), pl.BlockSpec((B, KW), lambda i: (prev(i), kb + 1)),
                  pl.BlockSpec(memory_space=pltpu.SMEM)],
        out_specs=pl.BlockSpec((B, AW), lambda i: (i, 0)),
        out_shape=jax.ShapeDtypeStruct((T, AW), MXU), compiler_params=_params("parallel"),
    )(qkv, qkv, qkv, qkv, qkv, sink)


def attn_bwd(qkv, sink, dout, *, name):
    T = qkv.shape[0]
    B, AW, KW = C.blk, C.AW, C.KW
    nb = T // B
    kb = AW // KW
    ppk = C.qh // C.kvh // 2
    scale = C.hd ** -0.5

    def body(q_ref, kc_ref, kp_ref, vc_ref, vp_ref, sink_ref, do_ref, dq_ref, dk_ref, dv_ref, ds_ref, ck_ref, cv_ref):
        i = pl.program_id(0)

        @pl.when(i == 0)
        def _():
            ck_ref[...] = jnp.zeros_like(ck_ref)
            cv_ref[...] = jnp.zeros_like(cv_ref)
            ds_ref[...] = jnp.zeros_like(ds_ref)

        @pl.when(i < nb)
        def _():
            lane = lax.broadcasted_iota(jnp.int32, (1, 128), 1)
            lo = lane < 64
            hlane = lax.broadcasted_iota(jnp.int32, (1, C.qh), 1)
            mask = _attn_masks(i, 2 * ppk)
            dsink = jnp.zeros((1, C.qh), F32)
            for m in range(KW // 128):
                ksl = slice(m * 128, (m + 1) * 128)
                k2s = _kv2(kp_ref, kc_ref, m, lo)
                v2s = _kv2(vp_ref, vc_ref, m, lo)
                folded = []
                for par in range(2):
                    kvh = 2 * m + par
                    q4 = _stack_heads(q_ref, kvh, ppk, lo)
                    do4 = _stack_heads(do_ref, kvh, ppk, lo)
                    p = _attn_weights(q4, k2s[par], mask, _sink_scores(sink_ref, kvh, ppk))
                    inv = 1.0 / _row_sums(p)
                    p = p * jnp.concatenate([inv, inv], axis=1)
                    dp = _dot_nt(do4, v2s[par])
                    delta = _row_sums(p * dp)
                    dsc = p * (dp - jnp.concatenate([delta, delta], axis=1))
                    _unstack_heads(dq_ref, _dot_nn(dsc, k2s[par]) * scale, kvh, ppk, lo)
                    dk2 = _dot_tn(dsc, q4) * scale
                    dv2 = _dot_tn(p, do4)
                    for k in range(2 * ppk):
                        dsh = jnp.sum(dsc[k * B:(k + 1) * B, :128], axis=0, keepdims=True)[:, :1]
                        dsink = dsink + jnp.where(hlane == kvh * 2 * ppk + k, dsh, 0.0)
                    folded.append((dk2 + pltpu.roll(dk2, 64, axis=1), dv2 + pltpu.roll(dv2, 64, axis=1)))
                row0 = lax.broadcasted_iota(jnp.int32, (2 * B, 1), 0) == 0
                dks = jnp.where(row0, 0.0, jnp.where(lo, folded[0][0], folded[1][0]))
                dvs = jnp.where(row0, 0.0, jnp.where(lo, folded[0][1], folded[1][1]))
                dk_ref[:, ksl] = (ck_ref[:, ksl] + dks[:B]).astype(dk_ref.dtype)
                dv_ref[:, ksl] = (cv_ref[:, ksl] + dvs[:B]).astype(dv_ref.dtype)
                ck_ref[:, ksl] = dks[B:]
                cv_ref[:, ksl] = dvs[B:]
            ds_ref[...] += dsink

        @pl.when(i == nb)
        def _():
            dk_ref[...] = ck_ref[...].astype(dk_ref.dtype)
            dv_ref[...] = cv_ref[...].astype(dv_ref.dtype)

    cur = lambda i: jnp.minimum(i, nb - 1)
    prev = lambda i: jnp.maximum(jnp.minimum(i, nb - 1) - 1, 0)
    out = lambda i: jnp.maximum(i - 1, 0)
    return pl.pallas_call(
        body, name=name, grid=(nb + 1,),
        in_specs=[pl.BlockSpec((B, AW), lambda i: (cur(i), 0)),
                  pl.BlockSpec((B, KW), lambda i: (cur(i), kb)), pl.BlockSpec((B, KW), lambda i: (prev(i), kb)),
                  pl.BlockSpec((B, KW), lambda i: (cur(i), kb + 1)), pl.BlockSpec((B, KW), lambda i: (prev(i), kb + 1)),
                  pl.BlockSpec(memory_space=pltpu.SMEM),
                  pl.BlockSpec((B, AW), lambda i: (cur(i), 0))],
        out_specs=[pl.BlockSpec((B, AW), lambda i: (cur(i), 0)),
                   pl.BlockSpec((B, KW), lambda i: (out(i), 0)), pl.BlockSpec((B, KW), lambda i: (out(i), 0)),
                   pl.BlockSpec((1, C.qh), lambda i: (0, 0))],
        out_shape=[jax.ShapeDtypeStruct((T, AW), MXU), jax.ShapeDtypeStruct((T, KW), MXU),
                   jax.ShapeDtypeStruct((T, KW), MXU), jax.ShapeDtypeStruct((1, C.qh), F32)],
        scratch_shapes=[pltpu.VMEM((B, KW), F32), pltpu.VMEM((B, KW), F32)],
        compiler_params=_params("arbitrary"),
    )(qkv, qkv, qkv, qkv, qkv, sink, dout)


POOL_HALO = 16


def _window_sum(e, w, n, forward):
    s, k = e, 1
    while k < w:
        s = s + pltpu.roll(s, (n - k) if forward else k, axis=0)
        k *= 2
    return s


def pool_fwd(u, pw, ps, *, name):
    T, PW = u.shape
    PG = C.PG
    tT = _pick(T, (256, 128))
    hb = tT // POOL_HALO

    def body(u_ref, uh_ref, pw_ref, ps_ref, o_ref, mx_ref):
        i = pl.program_id(0)
        halo = jnp.where(i > 0, uh_ref[...], 0.0)
        ext = jnp.concatenate([halo, u_ref[...]], axis=0)
        t = i * tT + lax.broadcasted_iota(jnp.int32, (tT, 1), 0)
        for g, w in enumerate(POOL_WINDOWS):
            sl = slice(g * PG, (g + 1) * PG)
            s = _window_sum(ext[:, sl], w, tT + POOL_HALO, False)[POOL_HALO:]
            cnt = jnp.minimum(t + 1, w).astype(F32)
            mixed = (s / cnt - u_ref[:, sl]).astype(MXU)
            mx_ref[:, sl] = mixed.astype(mx_ref.dtype)
            o_ref[:, sl] = (_dot_nn(mixed, pw_ref[g]) * ps_ref[:, sl]).astype(o_ref.dtype)

    row = pl.BlockSpec((tT, PW), lambda i: (i, 0))
    return pl.pallas_call(
        body, name=name, grid=(T // tT,),
        in_specs=[row, pl.BlockSpec((POOL_HALO, PW), lambda i: (jnp.maximum(i * hb - 1, 0), 0)),
                  pl.BlockSpec((4, PG, PG), lambda i: (0, 0, 0)), pl.BlockSpec((1, PW), lambda i: (0, 0))],
        out_specs=[row, row], out_shape=[jax.ShapeDtypeStruct((T, PW), MXU)] * 2,
        compiler_params=_params("parallel"),
    )(u, u, pw, ps)


def pool_bwd(dpool, mixed, pw, ps, *, name):
    T, PW = dpool.shape
    PG = C.PG
    tT = _pick(T, (256, 128))
    hb = tT // POOL_HALO
    n = T // tT
    rows = tT + POOL_HALO

    def body(dp_ref, dph_ref, mx_ref, pw_ref, ps_ref, du_ref, dpw_ref, dps_ref):
        i = pl.program_id(0)

        @pl.when(i == 0)
        def _():
            dpw_ref[...] = jnp.zeros_like(dpw_ref)
            dps_ref[...] = jnp.zeros_like(dps_ref)

        halo = jnp.where(i < n - 1, dph_ref[...].astype(F32), 0.0)
        dext = jnp.concatenate([dp_ref[...].astype(F32), halo], axis=0)
        t = i * tT + lax.broadcasted_iota(jnp.int32, (rows, 1), 0)
        for g, w in enumerate(POOL_WINDOWS):
            sl = slice(g * PG, (g + 1) * PG)
            dyg = dext[:, sl] * ps_ref[:, sl]
            dmix = _dot_nt(dyg, pw_ref[g])
            cnt = jnp.minimum(t + 1, w).astype(F32)
            s = _window_sum(dmix / cnt, w, rows, True)
            du_ref[:, sl] = (s[:tT] - dmix[:tT]).astype(du_ref.dtype)
            mb = mx_ref[:, sl]
            dpw_ref[g] += _dot_tn(mb, dyg[:tT])
            dps_ref[:, sl] += jnp.sum(dext[:tT, sl] * _dot_nn(mb, pw_ref[g]), axis=0, keepdims=True)

    row = pl.BlockSpec((tT, PW), lambda i: (i, 0))
    return pl.pallas_call(
        body, name=name, grid=(n,),
        in_specs=[row, pl.BlockSpec((POOL_HALO, PW), lambda i: (jnp.minimum((i + 1) * hb, T // POOL_HALO - 1), 0)), row,
                  pl.BlockSpec((4, PG, PG), lambda i: (0, 0, 0)), pl.BlockSpec((1, PW), lambda i: (0, 0))],
        out_specs=[row, pl.BlockSpec((4, PG, PG), lambda i: (0, 0, 0)), pl.BlockSpec((1, PW), lambda i: (0, 0))],
        out_shape=[jax.ShapeDtypeStruct((T, PW), MXU), jax.ShapeDtypeStruct((4, PG, PG), F32),
                   jax.ShapeDtypeStruct((1, PW), F32)],
        compiler_params=_params("arbitrary"),
    )(dpool, dpool, mixed, pw, ps)


CONV_HALO = 8
CONV_K = 4


def _conv(x, cw_ref, cb_ref, sl):
    acc = cb_ref[:, sl] + cw_ref[CONV_K - 1:CONV_K, sl] * x
    for k in range(CONV_K - 1):
        acc = acc + cw_ref[k:k + 1, sl] * pltpu.roll(x, CONV_K - 1 - k, axis=0)
    return acc


def conv_fwd(xbc, cw, cb, *, name):
    T, CC = xbc.shape
    tT = _pick(T, (256, 128))
    hb = tT // CONV_HALO
    cs = _pick(CC, (512, 256, 128))

    def body(x_ref, xh_ref, cw_ref, cb_ref, o_ref):
        i = pl.program_id(0)
        for c0 in range(0, CC, cs):
            sl = slice(c0, c0 + cs)
            ext = jnp.concatenate([jnp.where(i > 0, xh_ref[:, sl], 0.0), x_ref[:, sl]], axis=0)
            xc = _conv(ext, cw_ref, cb_ref, sl)[CONV_HALO:]
            o_ref[:, sl] = xc * _sigmoid(xc)

    row = pl.BlockSpec((tT, CC), lambda i: (i, 0))
    return pl.pallas_call(
        body, name=name, grid=(T // tT,),
        in_specs=[row, pl.BlockSpec((CONV_HALO, CC), lambda i: (jnp.maximum(i * hb - 1, 0), 0)),
                  pl.BlockSpec((CONV_K, CC), lambda i: (0, 0)), pl.BlockSpec((1, CC), lambda i: (0, 0))],
        out_specs=row, out_shape=jax.ShapeDtypeStruct((T, CC), F32), compiler_params=_params("parallel"),
    )(xbc, xbc, cw, cb)


def conv_bwd(xbc, cw, cb, dxa, *, name):
    T, CC = xbc.shape
    tT = _pick(T, (256, 128))
    hb = tT // CONV_HALO
    n = T // tT
    rows = tT + 2 * CONV_HALO
    cs = _pick(CC, (512, 256, 128))

    def body(x_ref, xp_ref, xn_ref, cw_ref, cb_ref, d_ref, dn_ref, dx_ref, dcw_ref, dcb_ref):
        i = pl.program_id(0)

        @pl.when(i == 0)
        def _():
            dcw_ref[...] = jnp.zeros_like(dcw_ref)
            dcb_ref[...] = jnp.zeros_like(dcb_ref)

        r = lax.broadcasted_iota(jnp.int32, (rows, 1), 0)
        own = (r >= CONV_HALO) & (r < tT + CONV_HALO)
        for c0 in range(0, CC, cs):
            sl = slice(c0, c0 + cs)
            x = jnp.concatenate([jnp.where(i > 0, xp_ref[:, sl], 0.0), x_ref[:, sl],
                                 jnp.where(i < n - 1, xn_ref[:, sl], 0.0)], axis=0)
            da = jnp.concatenate([jnp.zeros((CONV_HALO, cs), F32), d_ref[:, sl],
                                  jnp.where(i < n - 1, dn_ref[:, sl], 0.0)], axis=0)
            xc = _conv(x, cw_ref, cb_ref, sl)
            sg = _sigmoid(xc)
            dxc = da * sg * (1.0 + xc * (1.0 - sg))
            acc = cw_ref[CONV_K - 1:CONV_K, sl] * dxc
            for k in range(CONV_K - 1):
                acc = acc + cw_ref[k:k + 1, sl] * pltpu.roll(dxc, rows - (CONV_K - 1 - k), axis=0)
            dx_ref[:, sl] = acc[CONV_HALO:tT + CONV_HALO].astype(dx_ref.dtype)
            down = jnp.where(own, dxc, 0.0)
            dcb_ref[:, sl] += jnp.sum(down, axis=0, keepdims=True)
            dcw_ref[CONV_K - 1:CONV_K, sl] += jnp.sum(down * x, axis=0, keepdims=True)
            for k in range(CONV_K - 1):
                dcw_ref[k:k + 1, sl] += jnp.sum(down * pltpu.roll(x, CONV_K - 1 - k, axis=0), axis=0, keepdims=True)

    row = pl.BlockSpec((tT, CC), lambda i: (i, 0))
    prev = pl.BlockSpec((CONV_HALO, CC), lambda i: (jnp.maximum(i * hb - 1, 0), 0))
    nxt = pl.BlockSpec((CONV_HALO, CC), lambda i: (jnp.minimum((i + 1) * hb, T // CONV_HALO - 1), 0))
    return pl.pallas_call(
        body, name=name, grid=(n,),
        in_specs=[row, prev, nxt, pl.BlockSpec((CONV_K, CC), lambda i: (0, 0)), pl.BlockSpec((1, CC), lambda i: (0, 0)),
                  row, nxt],
        out_specs=[row, pl.BlockSpec((CONV_K, CC), lambda i: (0, 0)), pl.BlockSpec((1, CC), lambda i: (0, 0))],
        out_shape=[jax.ShapeDtypeStruct((T, CC), MXU), jax.ShapeDtypeStruct((CONV_K, CC), F32),
                   jax.ShapeDtypeStruct((1, CC), F32)],
        compiler_params=_params("arbitrary"),
    )(xbc, xbc, xbc, cw, cb, dxa, dxa)


def _softplus(x):
    return jnp.maximum(x, 0.0) + jnp.log(1.0 + jnp.exp(-jnp.abs(x)))


def _dot_exact(a, b):
    return lax.dot_general(a, b, (((1,), (0,)), ((), ())), precision=lax.Precision.HIGHEST, preferred_element_type=F32)


def ssd_prep(raw_t, bias, alog, *, name):
    H, T = raw_t.shape
    B = C.blk
    tc = _pick(T, (4 * B, 2 * B, B))

    def body(r_ref, b_ref, al_ref, dt_ref, acs_ref):
        dt = _softplus(r_ref[...] + b_ref[...])
        dt_ref[...] = dt
        dta = dt * (-jnp.exp(al_ref[...]))
        upper = (lax.broadcasted_iota(jnp.int32, (B, B), 0) <= lax.broadcasted_iota(jnp.int32, (B, B), 1)).astype(F32)
        for j in range(tc // B):
            acs_ref[:, j * B:(j + 1) * B] = _dot_exact(dta[:, j * B:(j + 1) * B], upper)

    blk = pl.BlockSpec((H, tc), lambda i: (0, i))
    vec = pl.BlockSpec((H, 1), lambda i: (0, 0))
    return pl.pallas_call(
        body, name=name, grid=(T // tc,), in_specs=[blk, vec, vec], out_specs=[blk, blk],
        out_shape=[jax.ShapeDtypeStruct((H, T), F32)] * 2, compiler_params=_params("parallel"),
    )(raw_t, bias, alog)


def _pair(lo, arr, h0, rows=slice(None)):
    return jnp.where(lo, arr[rows, h0:h0 + 1], arr[rows, h0 + 1:h0 + 2])


def _decay(acs, acs_t, h, causal):
    return jnp.exp(jnp.where(causal, acs[:, h:h + 1] - acs_t[h:h + 1, :], NEG))


def ssd_fwd(xa, dt, acs, acs_t, dskip, *, name):
    T = xa.shape[0]
    B, DI, G, N, HG, P, H = C.blk, C.DI, C.G, C.N, C.HG, C.P, C.H
    assert P == 64 and HG % 2 == 0
    nc = T // B
    W = HG * P

    def body(xa_ref, dt_ref, acs_ref, acst_ref, ds_ref, y_ref, hp_ref, h_scr):
        @pl.when(pl.program_id(0) == 0)
        def _():
            h_scr[...] = jnp.zeros_like(h_scr)

        lo = lax.broadcasted_iota(jnp.int32, (1, 128), 1) < 64
        causal = lax.broadcasted_iota(jnp.int32, (B, B), 0) >= lax.broadcasted_iota(jnp.int32, (B, B), 1)
        dt, acs, acs_t, dsk = dt_ref[...], acs_ref[...], acst_ref[...], ds_ref[...]
        for g in range(G):
            bg = xa_ref[:, DI + g * N:DI + (g + 1) * N]
            cg = xa_ref[:, DI + (G + g) * N:DI + (G + g + 1) * N]
            cb = _dot_nt(cg, bg)
            hg = h_scr[g]
            hp_ref[0, g * N:(g + 1) * N, :] = hg
            yoff = _dot_nn(cg, hg)
            xws, decs = [], []
            for j in range(HG // 2):
                h0 = g * HG + 2 * j
                xsl = slice(h0 * P, (h0 + 2) * P)
                xp = xa_ref[:, xsl]
                ap = _pair(lo, acs, h0)
                alast = _pair(lo, acs, h0, slice(B - 1, B))
                xdt = xp * _pair(lo, dt, h0)
                ys = [_dot_nn(cb * _decay(acs, acs_t, h0 + half, causal), xdt) for half in range(2)]
                y_ref[:, xsl] = (jnp.where(lo, ys[0], ys[1]) + yoff[:, 2 * j * P:(2 * j + 2) * P] * jnp.exp(ap)
                                 + _pair(lo, dsk, h0) * xp)
                xws.append(xdt * jnp.exp(alast - ap))
                decs.append(jnp.exp(alast))
            h_scr[g] = hg * jnp.concatenate(decs, axis=1) + _dot_tn(bg, jnp.concatenate(xws, axis=1))

    tok = lambda w: pl.BlockSpec((B, w), lambda c: (c, 0))
    return pl.pallas_call(
        body, name=name, grid=(nc,),
        in_specs=[tok(C.CC), tok(H), tok(H), pl.BlockSpec((H, B), lambda c: (0, c)), pl.BlockSpec((1, H), lambda c: (0, 0))],
        out_specs=[tok(DI), pl.BlockSpec((1, G * N, W), lambda c: (c, 0, 0))],
        out_shape=[jax.ShapeDtypeStruct((T, DI), F32), jax.ShapeDtypeStruct((nc, G * N, W), F32)],
        scratch_shapes=[pltpu.VMEM((G, N, W), F32)],
        compiler_params=_params("arbitrary"),
    )(xa, dt, acs, acs_t, dskip)


def ssd_bwd(xa, dt, acs, acs_t, dskip, hprev, dy, *, name):
    T = xa.shape[0]
    B, DI, G, N, HG, P, H = C.blk, C.DI, C.G, C.N, C.HG, C.P, C.H
    nc = T // B
    W = HG * P

    def body(xa_ref, dt_ref, acs_ref, acst_ref, ds_ref, hp_ref, dy_ref, dxa_ref, ddt_ref, dacs_ref, dd_ref, dh_scr):
        @pl.when(pl.program_id(0) == 0)
        def _():
            dh_scr[...] = jnp.zeros_like(dh_scr)
            dd_ref[...] = jnp.zeros_like(dd_ref)

        lo = lax.broadcasted_iota(jnp.int32, (1, 128), 1) < 64
        hi = jnp.logical_not(lo)
        causal = lax.broadcasted_iota(jnp.int32, (B, B), 0) >= lax.broadcasted_iota(jnp.int32, (B, B), 1)
        hlane = lax.broadcasted_iota(jnp.int32, (1, H), 1)
        hsub = lax.broadcasted_iota(jnp.int32, (B, 1), 0)
        lastrow = lax.broadcasted_iota(jnp.int32, (B, 1), 0) == B - 1
        dt, acs, acs_t, dsk = dt_ref[...], acs_ref[...], acst_ref[...], ds_ref[...]
        d_acs = jnp.zeros((B, H), F32)
        d_acs_t = jnp.zeros((B, B), F32)
        d_dt = jnp.zeros((B, H), F32)
        d_d = jnp.zeros((1, H), F32)

        def rsum(v):
            return (jnp.sum(jnp.where(lo, v, 0.0), axis=1, keepdims=True),
                    jnp.sum(jnp.where(hi, v, 0.0), axis=1, keepdims=True))

        for g in range(G):
            bsl = slice(DI + g * N, DI + (g + 1) * N)
            csl = slice(DI + (G + g) * N, DI + (G + g + 1) * N)
            bg, cg = xa_ref[:, bsl], xa_ref[:, csl]
            cb = _dot_nt(cg, bg)
            hg = hp_ref[0, g * N:(g + 1) * N, :]
            dhg = dh_scr[g]
            yoff = _dot_nn(cg, hg)
            bds = _dot_nn(bg, dhg)
            d_cb = jnp.zeros((B, B), F32)
            dyes, xws, decs = [], [], []
            for j in range(HG // 2):
                h0 = g * HG + 2 * j
                xsl = slice(h0 * P, (h0 + 2) * P)
                psl = slice(2 * j * P, (2 * j + 2) * P)
                xp, dyp = xa_ref[:, xsl], dy_ref[:, xsl]
                dtp = _pair(lo, dt, h0)
                ap = _pair(lo, acs, h0)
                alast = _pair(lo, acs, h0, slice(B - 1, B))
                ea, ew, el = jnp.exp(ap), jnp.exp(alast - ap), jnp.exp(alast)
                xdt = xp * dtp
                halves = []
                for half in range(2):
                    h = h0 + half
                    lm = _decay(acs, acs_t, h, causal)
                    m = cb * lm
                    d_m = _dot_nt(jnp.where(lo if half == 0 else hi, dyp, 0.0), xdt)
                    d_cb = d_cb + d_m * lm
                    wgt = d_m * m
                    d_acs = d_acs + jnp.where(hlane == h, jnp.sum(wgt, axis=1, keepdims=True), 0.0)
                    d_acs_t = d_acs_t + jnp.where(hsub == h, jnp.sum(wgt, axis=0, keepdims=True), 0.0)
                    halves.append(_dot_tn(m, dyp))
                bdp = bds[:, psl]
                dxdt = jnp.where(lo, halves[0], halves[1]) + ew * bdp
                dxa_ref[:, xsl] = dtp * dxdt + _pair(lo, dsk, h0) * dyp
                xw = xdt * ew
                terms_dt = rsum(dxdt * xp)
                terms_dd = rsum(dyp * xp)
                terms_off = rsum(dyp * (ea * yoff[:, psl]))
                terms_e = rsum(xw * bdp)
                terms_h = rsum(hg[:, psl] * dhg[:, psl])
                for half in range(2):
                    h = h0 + half
                    sel = hlane == h
                    d_dt = d_dt + jnp.where(sel, terms_dt[half], 0.0)
                    d_d = d_d + jnp.where(sel, jnp.sum(terms_dd[half], axis=0, keepdims=True), 0.0)
                    e_last = jnp.sum(jnp.where(lo if half == 0 else hi, el, 0.0), axis=1, keepdims=True) * (1.0 / P)
                    d_last = (jnp.sum(terms_e[half], axis=0, keepdims=True)
                              + e_last * jnp.sum(terms_h[half], axis=0, keepdims=True))
                    d_acs = d_acs + jnp.where(sel, terms_off[half] - terms_e[half] + jnp.where(lastrow, d_last, 0.0), 0.0)
                dyes.append(dyp * ea)
                xws.append(xw)
                decs.append(el)
            dye = jnp.concatenate(dyes, axis=1)
            xwc = jnp.concatenate(xws, axis=1)
            dxa_ref[:, csl] = _dot_nn(d_cb, bg) + _dot_nt(dye, hg)
            dxa_ref[:, bsl] = _dot_tn(d_cb, cg) + _dot_nt(xwc, dhg)
            dh_scr[g] = dhg * jnp.concatenate(decs, axis=1) + _dot_tn(cg, dye)
        ddt_ref[...] = d_dt
        dacs_ref[...] = d_acs - d_acs_t.T[:, :H]
        dd_ref[...] += d_d

    rev = lambda w: pl.BlockSpec((B, w), lambda c: (nc - 1 - c, 0))
    vec = pl.BlockSpec((1, H), lambda c: (0, 0))
    return pl.pallas_call(
        body, name=name, grid=(nc,),
        in_specs=[rev(C.CC), rev(H), rev(H), pl.BlockSpec((H, B), lambda c: (0, nc - 1 - c)), vec,
                  pl.BlockSpec((1, G * N, W), lambda c: (nc - 1 - c, 0, 0)), rev(DI)],
        out_specs=[rev(C.CC), rev(H), rev(H), vec],
        out_shape=[jax.ShapeDtypeStruct((T, C.CC), F32), jax.ShapeDtypeStruct((T, H), F32),
                   jax.ShapeDtypeStruct((T, H), F32), jax.ShapeDtypeStruct((1, H), F32)],
        scratch_shapes=[pltpu.VMEM((G, N, W), F32)],
        compiler_params=_params("arbitrary"),
    )(xa, dt, acs, acs_t, dskip, hprev, dy)


def ssd_post(ddt, dacs, dt, raw, bias, alog, *, name):
    T, H = ddt.shape
    B = C.blk
    tc = _pick(T, (4 * B, 2 * B, B))

    def body(ddt_ref, dacs_ref, dt_ref, raw_ref, b_ref, al_ref, draw_ref, db_ref, dal_ref):
        @pl.when(pl.program_id(0) == 0)
        def _():
            db_ref[...] = jnp.zeros_like(db_ref)
            dal_ref[...] = jnp.zeros_like(dal_ref)

        a = -jnp.exp(al_ref[...])
        lower = (lax.broadcasted_iota(jnp.int32, (B, B), 0) <= lax.broadcasted_iota(jnp.int32, (B, B), 1)).astype(F32)
        for j in range(tc // B):
            sl = slice(j * B, (j + 1) * B)
            rc = _dot_exact(lower, dacs_ref[sl, :])
            dtv = dt_ref[sl, :]
            draw = (ddt_ref[sl, :] + a * rc) * _sigmoid(raw_ref[sl, :] + b_ref[...])
            draw_ref[sl, :] = draw
            db_ref[...] += jnp.sum(draw, axis=0, keepdims=True)
            dal_ref[...] += jnp.sum(dtv * rc, axis=0, keepdims=True) * a

    blk = pl.BlockSpec((tc, H), lambda i: (i, 0))
    vec = pl.BlockSpec((1, H), lambda i: (0, 0))
    return pl.pallas_call(
        body, name=name, grid=(T // tc,), in_specs=[blk, blk, blk, blk, vec, vec], out_specs=[blk, vec, vec],
        out_shape=[jax.ShapeDtypeStruct((T, H), F32), jax.ShapeDtypeStruct((1, H), F32), jax.ShapeDtypeStruct((1, H), F32)],
        compiler_params=_params("arbitrary"),
    )(ddt, dacs, dt, raw, bias, alog)


MESH = pl.DeviceIdType.MESH
PACK_W = 1024
ANY = pl.BlockSpec(memory_space=pl.ANY)


def _place():
    return lax.axis_index("x"), lax.axis_index("y"), lax.axis_index("c")


def _other_chips(x, y):
    return [(1 - x, y), (x, 1 - y), (1 - x, 1 - y)]


def _remote(src, dst, send_sems, recv_sems, k, to):
    return pltpu.make_async_remote_copy(src_ref=src, dst_ref=dst, send_sem=send_sems.at[k], recv_sem=recv_sems.at[k],
                                        device_id=to, device_id_type=MESH)


def _half(c, rows):
    return pl.ds(pl.multiple_of(c * (rows // 2), 16), rows // 2)


def gather_weights(shards, rowwise, *, name):
    n = len(shards)

    def body(*refs):
        ps, gs = refs[:n], refs[n:2 * n]
        send_sems, recv_sems = refs[2 * n:]
        x, y, c = _place()
        s = 2 * x + y
        sib = (x, y, 1 - c)
        chips = _other_chips(x, y)

        def dst(w, shard, layer=None):
            if rowwise[w]:
                return gs[w].at[:, shard] if layer is None else gs[w].at[layer, shard]
            return gs[w].at[shard] if layer is None else gs[w].at[shard, layer]

        copy = functools.partial(_remote, send_sems=send_sems, recv_sems=recv_sems)
        started = []
        for w in range(n):
            started.append(copy(ps[w], dst(w, s), k=w, to=sib))
        for w in range(n):
            for j, (px, py) in enumerate(chips):
                started.append(copy(ps[w].at[c], dst(w, s, c), k=n + 3 * w + j, to=(px, py, c)))
        for cp in started:
            cp.start()
        passed = []
        for w in range(n):
            for j, (px, py) in enumerate(chips):
                there = dst(w, 2 * px + py, c)
                copy(ps[w].at[c], there, k=n + 3 * w + j, to=(px, py, c)).wait_recv()
                fw = copy(there, there, k=4 * n + 3 * w + j, to=sib)
                fw.start()
                passed.append(fw)
        for w in range(n):
            copy(ps[w], dst(w, s), k=w, to=sib).wait_recv()
            for j, (px, py) in enumerate(chips):
                there = dst(w, 2 * px + py, 1 - c)
                copy(there, there, k=4 * n + 3 * w + j, to=sib).wait_recv()
        for cp in started + passed:
            cp.wait_send()

    def full(p, rw):
        L = p.shape[0]
        return jax.ShapeDtypeStruct((L, 4) + p.shape[1:] if rw else (4,) + p.shape, p.dtype)

    return pl.pallas_call(
        body, name=name, in_specs=[ANY] * n, out_specs=[ANY] * n, out_shape=[full(p, rw) for p, rw in zip(shards, rowwise)],
        scratch_shapes=[pltpu.SemaphoreType.DMA((7 * n,)), pltpu.SemaphoreType.DMA((7 * n,))],
    )(*shards)


def swap_halves(items, *, name):
    n = len(items)

    def body(*refs):
        gs, rs = refs[:n], refs[n:2 * n]
        send_sems, recv_sems = refs[2 * n:]
        x, y, c = _place()
        cps = [_remote(g.at[:, _half(1 - c, g.shape[1])], r, send_sems, recv_sems, k, (x, y, 1 - c))
               for k, (g, r) in enumerate(zip(gs, rs))]
        for cp in cps:
            cp.start()
        for cp in cps:
            cp.wait()

    return pl.pallas_call(
        body, name=name, in_specs=[ANY] * n, out_specs=[ANY] * n,
        out_shape=[jax.ShapeDtypeStruct((4, g.shape[1] // 2, g.shape[2]), g.dtype) for g in items],
        scratch_shapes=[pltpu.SemaphoreType.DMA((n,)), pltpu.SemaphoreType.DMA((n,))],
    )(*items)


def scatter_chips(items, *, name):
    n = len(items)

    def body(*refs):
        as_, rs = refs[:n], refs[n:2 * n]
        send_sems, recv_sems = refs[2 * n:]
        x, y, c = _place()
        cps = [_remote(a.at[2 * px + py], r.at[j], send_sems, recv_sems, 3 * k + j, (px, py, c))
               for k, (a, r) in enumerate(zip(as_, rs)) for j, (px, py) in enumerate(_other_chips(x, y))]
        for cp in cps:
            cp.start()
        for cp in cps:
            cp.wait()

    return pl.pallas_call(
        body, name=name, in_specs=[ANY] * n, out_specs=[ANY] * n,
        out_shape=[jax.ShapeDtypeStruct((3,) + a.shape[1:], a.dtype) for a in items],
        scratch_shapes=[pltpu.SemaphoreType.DMA((3 * n,)), pltpu.SemaphoreType.DMA((3 * n,))],
    )(*items)


def join_halves(items, *, name):
    n = len(items)

    def body(*refs):
        rs, outs = refs[:n], refs[n:2 * n]
        send_sems, recv_sems = refs[2 * n:]
        x, y, c = _place()
        sib = (x, y, 1 - c)
        cps = []
        for k, (r, o) in enumerate(zip(rs, outs)):
            mine = _half(c, r.shape[0])
            cps.append(_remote(r.at[mine], o.at[mine], send_sems, recv_sems, k, sib))
            cps[-1].start()
        for k, (r, o) in enumerate(zip(rs, outs)):
            theirs = _half(1 - c, r.shape[0])
            _remote(r.at[theirs], o.at[theirs], send_sems, recv_sems, k, sib).wait_recv()
        for cp in cps:
            cp.wait_send()

    return pl.pallas_call(
        body, name=name, in_specs=[ANY] * n, out_specs=[ANY] * n,
        out_shape=[jax.ShapeDtypeStruct(r.shape, r.dtype) for r in items],
        input_output_aliases={k: k for k in range(n)},
        scratch_shapes=[pltpu.SemaphoreType.DMA((n,)), pltpu.SemaphoreType.DMA((n,))],
    )(*items)


def gather_all(v, *, name, total):
    rows, W = v.shape

    def body(v_ref, o_ref, *scr):
        buf = scr[0] if total else o_ref
        send_sems, recv_sems = scr[-2], scr[-1]
        x, y, c = _place()
        me = 4 * x + 2 * y + c
        flips = [(k >> 2 & 1, k >> 1 & 1, k & 1) for k in range(1, 8)]
        peers = [((1 - x) if fx else x, (1 - y) if fy else y, (1 - c) if fc else c) for fx, fy, fc in flips]
        out = []
        for k, peer in enumerate(peers):
            cp = pltpu.make_async_remote_copy(src_ref=v_ref, dst_ref=buf.at[me], send_sem=send_sems.at[k],
                                              recv_sem=recv_sems.at[k], device_id=peer, device_id_type=MESH)
            cp.start()
            out.append(cp)
        buf[me] = v_ref[...]
        for k, (px, py, pc) in enumerate(peers):
            pltpu.make_async_remote_copy(src_ref=v_ref, dst_ref=buf.at[4 * px + 2 * py + pc], send_sem=send_sems.at[k],
                                         recv_sem=recv_sems.at[k], device_id=(px, py, pc), device_id_type=MESH).wait_recv()
        for cp in out:
            cp.wait_send()
        if total:
            acc = buf[0]
            for d in range(1, 8):
                acc = acc + buf[d]
            o_ref[...] = acc

    vm = pl.BlockSpec(memory_space=pltpu.VMEM)
    return pl.pallas_call(
        body, name=name, in_specs=[vm], out_specs=vm,
        out_shape=jax.ShapeDtypeStruct((rows, W) if total else (8, rows, W), F32),
        scratch_shapes=([pltpu.VMEM((8, rows, W), F32)] if total else [])
        + [pltpu.SemaphoreType.DMA((7,)), pltpu.SemaphoreType.DMA((7,))],
    )(v)


def pair_sum(g, r, *, name):
    _, R, W = g.shape
    Rh = R // 2
    tr = _pick(Rh, (512, 256, 128, 64, 32, 16))
    nb = Rh // tr

    def body(g_ref, r_ref, o_ref):
        o_ref[...] = (g_ref[...].astype(F32) + r_ref[...].astype(F32)).astype(o_ref.dtype)

    return pl.pallas_call(
        body, name=name, grid=(4, nb),
        in_specs=[pl.BlockSpec((1, tr, W), lambda s, i: (s, lax.axis_index("c") * nb + i, 0)),
                  pl.BlockSpec((1, tr, W), lambda s, i: (s, i, 0))],
        out_specs=pl.BlockSpec((1, tr, W), lambda s, i: (s, i, 0)),
        out_shape=jax.ShapeDtypeStruct((4, Rh, W), g.dtype), compiler_params=_params("parallel", "parallel"),
    )(g, r)


def chip_sum(g, r1, r2, *, name):
    _, R, W = g.shape
    Rh = R // 2
    tr = _pick(Rh, (512, 256, 128, 64, 32, 16))
    nb = Rh // tr

    def body(g_ref, r1_ref, a_ref, b_ref, c_ref, o_ref):
        acc = g_ref[0].astype(F32) + r1_ref[0].astype(F32)
        for ref in (a_ref, b_ref, c_ref):
            acc = acc + ref[0].astype(F32)
        o_ref[...] = acc

    shard = lambda: 2 * lax.axis_index("x") + lax.axis_index("y")
    half = lambda i: lax.axis_index("c") * nb + i
    return pl.pallas_call(
        body, name=name, grid=(nb,),
        in_specs=[pl.BlockSpec((1, tr, W), lambda i: (shard(), half(i), 0)),
                  pl.BlockSpec((1, tr, W), lambda i: (shard(), i, 0))]
        + [pl.BlockSpec((1, tr, W), lambda i, j=j: (j, i, 0)) for j in range(3)],
        out_specs=pl.BlockSpec((tr, W), lambda i: (half(i), 0)),
        out_shape=jax.ShapeDtypeStruct((R, W), F32), compiler_params=_params("parallel"),
    )(g, r1, r2, r2, r2)


BIG = (("w_in", False), ("pool_w", False), ("w_attn_br", False), ("w_pool_br", False), ("w_ssm_br", True),
       ("w_out", True), ("w_gate_up", False), ("w_down", True))


def _join(piece, axis):
    t = jnp.moveaxis(piece, 0, axis)
    shp = t.shape
    return t.reshape(shp[:axis] + (shp[axis] * shp[axis + 1],) + shp[axis + 2:])


def _seg_bounds():
    aw, kw, _, pw, di, cc, h, gd = C.in_widths
    o = [0, aw + 2 * kw]
    for wdt in (pw, di, cc, h, gd):
        o.append(o[-1] + wdt)
    return o


IN_PAD = 640


def _in_segments(blocks):
    w_in = jnp.concatenate([blocks[s] for s in range(4)], axis=1)
    o = _seg_bounds()
    segs = [w_in[:, o[i]:o[i + 1]] for i in range(6)]
    segs[4] = jnp.pad(segs[4], ((0, 0), (0, 128 - C.H)))
    width = sum(t.shape[1] for t in segs)
    pad = -width % IN_PAD
    return segs, jnp.concatenate(segs + [jnp.zeros((w_in.shape[0], pad), w_in.dtype)], axis=1)


def _layer_fwd(x, p, tag):
    nm = lambda s: f"{s}_{tag}"
    H = C.H
    h = rms_fwd(x, p["ln1_w"], name=nm("rms1"))
    wq, wu, wz, wx, wd, wg = p["in_segs"]
    qkv = matmul(h, wq, name=nm("mm_qkv"), out_dtype=MXU)
    u = matmul(h, wu, name=nm("mm_u"))
    z = matmul(h, wz, name=nm("mm_z"))
    xbc = matmul(h, wx, name=nm("mm_xbc"))
    dtp = matmul(h, wd, name=nm("mm_dt"))
    gl = matmul(h, wg, name=nm("mm_gate"), out_dtype=MXU)
    att = attn_fwd(qkv, p["attn_sink"], name=nm("attn_fwd"))
    pool, mixed = pool_fwd(u, p["pool_w"], p["pool_scale"], name=nm("pool_fwd"))
    xa = conv_fwd(xbc, p["conv_w"], p["conv_b"], name=nm("conv_fwd"))
    raw = dtp[:, :H]
    dt_t, acs_t = ssd_prep(raw.T, p["dt_bias"].T, p["a_log"].T, name=nm("ssd_prep"))
    dt, acs = dt_t.T, acs_t.T
    y, hprev = ssd_fwd(xa, dt, acs, acs_t, p["d_skip"], name=nm("ssd_fwd"))
    ssm = gnorm_fwd(y, z, p["ssm_norm_w"], name=nm("gnorm_fwd"))
    abr = matmul(att, name=nm("mm_abr"), out_dtype=MXU, **p["w_attn_br"])
    pbr = matmul(pool, name=nm("mm_pbr"), out_dtype=MXU, **p["w_pool_br"])
    sbr = matmul(ssm, name=nm("mm_sbr"), out_dtype=MXU, **p["w_ssm_br"])
    merged = merge_fwd(gl, abr, pbr, sbr, name=nm("merge_fwd"))
    xm = matmul(merged, add=x, name=nm("mm_out"), **p["w_out"])
    h2 = rms_fwd(xm, p["ln2_w"], name=nm("rms2"))
    gu = matmul(h2, name=nm("mm_gu"), out_dtype=MXU, **p["w_gate_up"])
    act = swiglu_fwd(gu, name=nm("swiglu_fwd"))
    xo = matmul(act, add=xm, name=nm("mm_down"), **p["w_down"])
    saved = dict(x=x, h=h, qkv=qkv, z=z, xbc=xbc, raw=raw, gl=gl, att=att, pool=pool, mixed=mixed, xa=xa, dt=dt, acs=acs,
                 acs_t=acs_t, y=y, hprev=hprev, ssm=ssm, abr=abr, pbr=pbr, sbr=sbr, merged=merged, xm=xm, h2=h2, gu=gu,
                 act=act)
    return xo, saved


def _layer_bwd(dxo, dxo_m, p, s, tag):
    nm = lambda t: f"{t}_{tag}"
    H = C.H
    g = {}

    def rows4(t):
        return t.reshape(4, t.shape[0] // 4, t.shape[1])

    g["w_down"] = rows4(matmul(s["act"], dxo_m, ta=True, out_dtype=MXU, name=nm("mmg_down")))
    dact = matmul(dxo_m, tb=True, out_dtype=MXU, name=nm("mmb_down"), **p["w_down"])
    dgu = swiglu_bwd(s["gu"], dact, name=nm("swiglu_bwd"))
    g["w_gate_up"] = matmul(s["h2"], dgu, ta=True, out_dtype=MXU, out_cols=True, name=nm("mmg_gu"))
    dh2 = matmul(dgu, tb=True, name=nm("mmb_gu"), **p["w_gate_up"])
    dxm, dxm_m, g["ln2_w"] = rms_bwd(s["xm"], p["ln2_w"], dh2, dxo, name=nm("rms2_bwd"))
    g["w_out"] = rows4(matmul(s["merged"], dxm_m, ta=True, out_dtype=MXU, name=nm("mmg_out")))
    dmerged = matmul(dxm_m, tb=True, out_dtype=MXU, name=nm("mmb_out"), **p["w_out"])
    dabr, dpbr, dsbr, dgl = merge_bwd(s["gl"], s["abr"], s["pbr"], s["sbr"], dmerged, name=nm("merge_bwd"))
    g["w_attn_br"] = matmul(s["att"], dabr, ta=True, out_dtype=MXU, out_cols=True, name=nm("mmg_abr"))
    g["w_pool_br"] = matmul(s["pool"], dpbr, ta=True, out_dtype=MXU, out_cols=True, name=nm("mmg_pbr"))
    g["w_ssm_br"] = rows4(matmul(s["ssm"], dsbr, ta=True, out_dtype=MXU, name=nm("mmg_sbr")))
    datt = matmul(dabr, tb=True, out_dtype=MXU, name=nm("mmb_abr"), **p["w_attn_br"])
    dpool = matmul(dpbr, tb=True, out_dtype=MXU, name=nm("mmb_pbr"), **p["w_pool_br"])
    dssm = matmul(dsbr, tb=True, out_dtype=MXU, name=nm("mmb_sbr"), **p["w_ssm_br"])
    dq, dk, dv, g["attn_sink"] = attn_bwd(s["qkv"], p["attn_sink"], datt, name=nm("attn_bwd"))
    du, dpw, g["pool_scale"] = pool_bwd(dpool, s["mixed"], p["pool_w"], p["pool_scale"], name=nm("pool_bwd"))
    pg = dpw.shape[1] // 4
    g["pool_w"] = jnp.moveaxis(dpw.reshape(4, 4, pg, dpw.shape[2]), 1, 0).reshape(4, 4 * pg, dpw.shape[2]).astype(MXU)
    dy, dz, g["ssm_norm_w"] = gnorm_bwd(s["y"], s["z"], p["ssm_norm_w"], dssm, name=nm("gnorm_bwd"))
    dxa, ddt, dacs, g["d_skip"] = ssd_bwd(s["xa"], s["dt"], s["acs"], s["acs_t"], p["d_skip"], s["hprev"], dy,
                                          name=nm("ssd_bwd"))
    draw, g["dt_bias"], g["a_log"] = ssd_post(ddt, dacs, s["dt"], s["raw"], p["dt_bias"], p["a_log"], name=nm("ssd_post"))
    dxbc, g["conv_w"], g["conv_b"] = conv_bwd(s["xbc"], p["conv_w"], p["conv_b"], dxa, name=nm("conv_bwd"))
    w_all = p["in_all"]
    ddtp = jnp.pad(draw, ((0, 0), (0, 128 - H))).astype(MXU)
    parts = [dq, dk, dv, du, dz, dxbc, ddtp, dgl]
    used = sum(t.shape[1] for t in parts)
    dproj = jnp.concatenate(parts + [jnp.zeros((dq.shape[0], w_all.shape[1] - used), MXU)], axis=1)
    dh = matmul(dproj, w_all, tb=True, name=nm("mmb_in"))
    g_all = matmul(s["h"], dproj, ta=True, out_dtype=MXU, name=nm("mmg_in"))
    o = _seg_bounds()
    dt0 = o[4]
    g_in = jnp.concatenate([g_all[:, :dt0 + H], g_all[:, dt0 + 128:dt0 + 128 + (o[6] - o[5])]], axis=1)
    nc = g_in.shape[1] // 4
    g["w_in"] = jnp.stack([g_in[:, k * nc:(k + 1) * nc] for k in range(4)])
    dx, dx_m, g["ln1_w"] = rms_bwd(s["x"], p["ln1_w"], dh, dxm, name=nm("rms1_bwd"))
    return dx, dx_m, g


SMALL = ("ln1_w", "attn_sink", "conv_w", "conv_b", "dt_bias", "a_log", "d_skip", "ssm_norm_w", "pool_scale", "ln2_w")
WEIGHTS = ("ln1_w", "w_in", "attn_sink", "conv_w", "conv_b", "dt_bias", "a_log", "d_skip", "ssm_norm_w", "pool_w",
           "pool_scale", "w_attn_br", "w_pool_br", "w_ssm_br", "w_out", "ln2_w", "w_gate_up", "w_down", "final_w")


def _step(x, loss_target, w, m, v):
    depth = C.depth
    xi, yi, ci = _place()
    shard = 2 * xi + yi

    gathered = gather_weights([w[n].astype(MXU) for n, _ in BIG], [rw for _, rw in BIG], name="gather_weights")
    full = dict(zip([n for n, _ in BIG], gathered))
    cw_all = gather_all(_pack_small([w["conv_w"].reshape(1, -1)]), name="gather_conv_w", total=False)
    cw_shards = cw_all[0::2].reshape(4, -1)[:, :w["conv_w"].size].reshape((4,) + w["conv_w"].shape)
    conv_w = _join(cw_shards, 2)

    layers = []
    for i in range(depth):
        p = {n: w[n][i][None] for n in SMALL if n != "conv_w"}
        p["conv_w"] = conv_w[i]
        p["in_segs"], p["in_all"] = _in_segments(full["w_in"][:, i])
        pw = full["pool_w"][:, i]
        p["pool_w"] = _join(pw, 1)
        for n, rw in BIG[2:]:
            if rw:
                t = full[n]
                p[n] = dict(b=t.reshape(t.shape[0], 4 * t.shape[2], t.shape[3]), layer=i)
            else:
                p[n] = dict(b=full[n], layer=i, b_cols=True)
        layers.append(p)

    xs = x[0]
    saved = []
    for i in range(depth):
        xs, sv = _layer_fwd(xs, layers[i], f"l{i}")
        saved.append(sv)
    loss_part, dx, dx_m, g_final = final_loss(xs, w["final_w"][None], loss_target[0], name="final_loss")
    grads = [None] * depth
    for i in reversed(range(depth)):
        dx, dx_m, grads[i] = _layer_bwd(dx, dx_m, layers[i], saved[i], f"l{i}")

    keys = [(i, n) for i in range(depth) for n, _ in BIG]
    items = [grads[i][n] for i, n in keys]
    sib = swap_halves(items, name="swap_halves")
    chip = [pair_sum(g, r, name=f"pair_sum_{n}_l{i}") for (i, n), g, r in zip(keys, items, sib)]
    others = scatter_chips(chip, name="scatter_chips")
    mine = [chip_sum(g, r, o, name=f"chip_sum_{n}_l{i}") for (i, n), g, r, o in zip(keys, items, sib, others)]
    reduced = dict(zip(keys, join_halves(mine, name="join_halves")))
    gout = {}

    small = [jnp.stack([grads[i][n].reshape(-1) for i in range(depth)]).reshape(1, -1) for n in SMALL]
    small += [g_final.reshape(1, -1), loss_part.reshape(1, -1)]
    tot = gather_all(_pack_small(small), name="sum_small", total=True)
    parts = _unpack_small(tot, [t.shape[1] for t in small])
    for n, t in zip(SMALL, parts):
        if n == "conv_w":
            cols = w[n].shape[2]
            gout[n] = lax.dynamic_slice_in_dim(t.reshape(depth, CONV_K, -1), shard * cols, cols, axis=2)
        else:
            gout[n] = t.reshape(w[n].shape)
    gout["final_w"] = parts[-2].reshape(w["final_w"].shape)
    loss = parts[-1].reshape(())

    upd = {}
    for n in WEIGHTS:
        if n in gout:
            upd[n] = (gout[n].reshape(w[n].shape),) + adamw(w[n], gout[n].reshape(w[n].shape), m[n], v[n], name=f"adamw_{n}")
        else:
            upd[n] = adamw_layers(w[n], [reduced[(i, n)] for i in range(depth)], m[n], v[n], name=f"adamw_{n}")
    return (loss, dx[None], *[upd[n][0] for n in WEIGHTS], *[upd[n][1] for n in WEIGHTS],
            *[upd[n][2] for n in WEIGHTS], *[upd[n][3] for n in WEIGHTS])


def _pack_small(parts):
    flat = jnp.concatenate(parts, axis=1)
    rows = -(-flat.shape[1] // PACK_W)
    rows = -(-rows // 8) * 8
    return jnp.pad(flat, ((0, 0), (0, rows * PACK_W - flat.shape[1]))).reshape(rows, PACK_W)


def _unpack_small(buf, sizes):
    flat = buf.reshape(-1)
    out, off = [], 0
    for n in sizes:
        out.append(flat[off:off + n])
        off += n
    return out


def kernel(x, ln1_w, w_in, attn_sink, conv_w, conv_b, dt_bias, a_log, d_skip, ssm_norm_w, pool_w, pool_scale, w_attn_br, w_pool_br, w_ssm_br, w_out, ln2_w, w_gate_up, w_down, final_w, loss_target, m_ln1_w, m_w_in, m_attn_sink, m_conv_w, m_conv_b, m_dt_bias, m_a_log, m_d_skip, m_ssm_norm_w, m_pool_w, m_pool_scale, m_w_attn_br, m_w_pool_br, m_w_ssm_br, m_w_out, m_ln2_w, m_w_gate_up, m_w_down, m_final_w, v_ln1_w, v_w_in, v_attn_sink, v_conv_w, v_conv_b, v_dt_bias, v_a_log, v_d_skip, v_ssm_norm_w, v_pool_w, v_pool_scale, v_w_attn_br, v_w_pool_br, v_w_ssm_br, v_w_out, v_ln2_w, v_w_gate_up, v_w_down, v_final_w):
    w = dict(ln1_w=ln1_w, w_in=w_in, attn_sink=attn_sink, conv_w=conv_w, conv_b=conv_b, dt_bias=dt_bias, a_log=a_log,
             d_skip=d_skip, ssm_norm_w=ssm_norm_w, pool_w=pool_w, pool_scale=pool_scale, w_attn_br=w_attn_br,
             w_pool_br=w_pool_br, w_ssm_br=w_ssm_br, w_out=w_out, ln2_w=ln2_w, w_gate_up=w_gate_up, w_down=w_down,
             final_w=final_w)
    m = dict(ln1_w=m_ln1_w, w_in=m_w_in, attn_sink=m_attn_sink, conv_w=m_conv_w, conv_b=m_conv_b, dt_bias=m_dt_bias,
             a_log=m_a_log, d_skip=m_d_skip, ssm_norm_w=m_ssm_norm_w, pool_w=m_pool_w, pool_scale=m_pool_scale,
             w_attn_br=m_w_attn_br, w_pool_br=m_w_pool_br, w_ssm_br=m_w_ssm_br, w_out=m_w_out, ln2_w=m_ln2_w,
             w_gate_up=m_w_gate_up, w_down=m_w_down, final_w=m_final_w)
    v = dict(ln1_w=v_ln1_w, w_in=v_w_in, attn_sink=v_attn_sink, conv_w=v_conv_w, conv_b=v_conv_b, dt_bias=v_dt_bias,
             a_log=v_a_log, d_skip=v_d_skip, ssm_norm_w=v_ssm_norm_w, pool_w=v_pool_w, pool_scale=v_pool_scale,
             w_attn_br=v_w_attn_br, w_pool_br=v_w_pool_br, w_ssm_br=v_w_ssm_br, w_out=v_w_out, ln2_w=v_ln2_w,
             w_gate_up=v_w_gate_up, w_down=v_w_down, final_w=v_final_w)
    return _step(x, loss_target, w, m, v)
```

```python
import functools

import jax
import jax.numpy as jnp
from jax import lax
from jax.experimental import pallas as pl
from jax.experimental.pallas import tpu as pltpu

F32 = jnp.float32
MXU = jnp.bfloat16
VMEM_LIMIT = 56 * 1024 * 1024
EPS = 1e-6
NEG = -1e30

ADAM_LR, ADAM_B1, ADAM_B2, ADAM_EPS, ADAM_WD, ADAM_STEP = 0.001, 0.9, 0.999, 1e-08, 0.01, 10


class Cfg:
    def __init__(self, d_model=2048, seq=8192, depth=2, q_heads=16, kv_heads=4, head_dim=64,
                 ssm_head_dim=64, ssm_groups=4, d_state=128):
        self.D, self.T, self.depth = d_model, seq, depth
        self.hd, self.qh, self.kvh = head_dim, q_heads, kv_heads
        self.AW, self.KW = q_heads * head_dim, kv_heads * head_dim
        self.blk = 128
        self.PW = d_model // 2
        self.PG = self.PW // 4
        self.DI = d_model
        self.P = ssm_head_dim
        self.H = self.DI // self.P
        self.G = ssm_groups
        self.HG = self.H // self.G
        self.N = d_state
        self.CC = self.DI + 2 * self.G * self.N
        self.F = -(-8 * d_model // (3 * 256)) * 256
        self.in_widths = (self.AW, self.KW, self.KW, self.PW, self.DI, self.CC, self.H, 3 * d_model)
        self.in_cols = sum(self.in_widths)


C = Cfg()
POOL_WINDOWS = (2, 4, 8, 16)


def _pick(n, cands):
    for c in cands:
        if n % c == 0:
            return c
    return n


def _params(*sem):
    return pltpu.CompilerParams(dimension_semantics=sem, vmem_limit_bytes=VMEM_LIMIT)


def _sigmoid(x):
    return 1.0 / (1.0 + jnp.exp(-x))


def _dot(a, b, dims):
    return lax.dot_general(a.astype(MXU), b.astype(MXU), (dims, ((), ())), preferred_element_type=F32)


def _dot_nn(a, b):
    return _dot(a, b, ((1,), (0,)))


def _dot_nt(a, b):
    return _dot(a, b, ((1,), (1,)))


def _dot_tn(a, b):
    return _dot(a, b, ((0,), (0,)))


MM_VMEM = 40 * 1024 * 1024
LANES = 128


def _tile(n, cap):
    best = None
    for d in range(LANES, min(n, cap) + 1, LANES):
        if n % d == 0:
            best = d
    return n if best is None else best


def _divisors(n, cap):
    ds = [d for d in range(LANES, min(n, cap) + 1, LANES) if n % d == 0]
    return ds or [n]


HBM_RATE, MXU_RATE, ACC_RATE, STEP_COST = 3.0e12, 0.9e15, 1.2e13, 0.35e-6
MXU_WIDTH = 256


def _choose_tiles(M, N, K, n_unit, k_unit, sa, sb, so, sadd, ta):
    best = None
    for tm in sorted({_tile(M, 1024), _tile(M, 512)}):
        for tk in _divisors(k_unit, 4096):
            for tn in _divisors(n_unit, 2048):
                nk = K // tk
                vmem = 2 * (tm * tk * sa + tk * tn * sb + tm * tn * (so + sadd)) + (tm * tn * 4 if nk > 1 else 0)
                if vmem > MM_VMEM:
                    continue
                steps = (M // tm) * (N // tn) * nk
                a_bytes = M * K * sa * (N // tn if (nk > 1 or ta) else 1)
                b_bytes = K * N * sb * (M // tm)
                hbm = (a_bytes + b_bytes + M * N * (so + sadd)) / HBM_RATE
                fill = (tn / (-(-tn // MXU_WIDTH) * MXU_WIDTH)) * (tk / (-(-tk // MXU_WIDTH) * MXU_WIDTH))
                mxu = 2.0 * M * N * K / (MXU_RATE * fill)
                acc = steps * tm * tn * 8 / ACC_RATE if nk > 1 else 0.0
                cost = max(hbm, mxu) + acc + steps * STEP_COST
                if best is None or cost < best[0]:
                    best = (cost, tm, tn, tk)
    assert best is not None, (M, N, K)
    return best[1:]


def matmul(a, b, *, name, ta=False, tb=False, out_dtype=F32, add=None, layer=None, b_cols=False, out_cols=False):
    M, K = (a.shape[1], a.shape[0]) if ta else a.shape
    rows, cols = b.shape[-2], (4 if b_cols else 1) * b.shape[-1]
    N, bk = (rows, cols) if tb else (cols, rows)
    assert K == bk, (a.shape, b.shape)
    n_unit = N // 4 if (out_cols or (b_cols and not tb)) else N
    k_unit = K // 4 if (b_cols and tb) else K
    sa, sb, so = a.dtype.itemsize, b.dtype.itemsize, jnp.dtype(out_dtype).itemsize
    sadd = add.dtype.itemsize if add is not None else 0
    tm, tn, tk = _choose_tiles(M, N, K, n_unit, k_unit, sa, sb, so, sadd, ta)
    nk = K // tk
    a_spec = pl.BlockSpec((tk, tm), lambda i, j, k: (k, i)) if ta else pl.BlockSpec((tm, tk), lambda i, j, k: (i, k))
    if b_cols and tb:
        per = k_unit // tk
        b_spec = pl.BlockSpec((None, tn, tk), lambda i, j, k: (k // per, j, k % per))
    elif b_cols:
        per = n_unit // tn
        b_spec = pl.BlockSpec((None, tk, tn), lambda i, j, k: (j // per, k, j % per))
    elif layer is not None:
        b_spec = (pl.BlockSpec((None, tn, tk), lambda i, j, k: (layer, j, k)) if tb
                  else pl.BlockSpec((None, tk, tn), lambda i, j, k: (layer, k, j)))
    else:
        b_spec = pl.BlockSpec((tn, tk), lambda i, j, k: (j, k)) if tb else pl.BlockSpec((tk, tn), lambda i, j, k: (k, j))
    if out_cols:
        per = n_unit // tn
        o_spec = pl.BlockSpec((None, tm, tn), lambda i, j, k: (j // per, i, j % per))
        out_shape = jax.ShapeDtypeStruct((4, M, N // 4), out_dtype)
    else:
        o_spec = pl.BlockSpec((tm, tn), lambda i, j, k: (i, j))
        out_shape = jax.ShapeDtypeStruct((M, N), out_dtype)
    dims = ((0 if ta else 1,), (1 if tb else 0,))
    has_add = add is not None

    def body(*refs):
        a_ref, b_ref = refs[0], refs[1]
        add_ref = refs[2] if has_add else None
        o_ref = refs[3] if has_add else refs[2]
        part = _dot(a_ref[...], b_ref[...], dims)

        def finish(total):
            if has_add:
                total = total + add_ref[...].astype(F32)
            o_ref[...] = total.astype(o_ref.dtype)

        if nk == 1:
            finish(part)
        else:
            acc_ref = refs[-1]
            k = pl.program_id(2)

            @pl.when(k == 0)
            def _():
                acc_ref[...] = part

            @pl.when(k > 0)
            def _():
                acc_ref[...] += part

            @pl.when(k == nk - 1)
            def _():
                finish(acc_ref[...])

    in_specs = [a_spec, b_spec] + ([o_spec] if has_add else [])
    args = (a, b) + ((add,) if has_add else ())
    return pl.pallas_call(
        body, name=name, grid=(M // tm, N // tn, nk), in_specs=in_specs, out_specs=o_spec, out_shape=out_shape,
        scratch_shapes=[pltpu.VMEM((tm, tn), F32)] if nk > 1 else [],
        compiler_params=_params("parallel", "parallel", "arbitrary"),
    )(*args)


def _row_tile(t):
    return _pick(t, (256, 128, 64, 32, 16, 8))


def rms_fwd(x, w, *, name):
    T, D = x.shape
    tr = _row_tile(T)

    def body(x_ref, w_ref, o_ref):
        xv = x_ref[...]
        r = lax.rsqrt(jnp.mean(xv * xv, axis=-1, keepdims=True) + EPS)
        o_ref[...] = (xv * r * w_ref[...]).astype(o_ref.dtype)

    row = pl.BlockSpec((tr, D), lambda i: (i, 0))
    return pl.pallas_call(
        body, name=name, grid=(T // tr,), in_specs=[row, pl.BlockSpec((1, D), lambda i: (0, 0))], out_specs=row,
        out_shape=jax.ShapeDtypeStruct((T, D), MXU), compiler_params=_params("parallel"),
    )(x, w)


def rms_bwd(x, w, dh, dres, *, name):
    T, D = x.shape
    tr = _row_tile(T)

    def body(x_ref, w_ref, dh_ref, dres_ref, dx_ref, dxm_ref, dw_ref):
        xv = x_ref[...]
        dhv = dh_ref[...].astype(F32)
        r = lax.rsqrt(jnp.mean(xv * xv, axis=-1, keepdims=True) + EPS)
        g = dhv * w_ref[...]
        dot = jnp.mean(g * xv, axis=-1, keepdims=True)
        dx = dres_ref[...] + r * g - xv * (r * r * r * dot)
        dx_ref[...] = dx
        dxm_ref[...] = dx.astype(dxm_ref.dtype)
        part = jnp.sum(dhv * xv * r, axis=0, keepdims=True)

        @pl.when(pl.program_id(0) == 0)
        def _():
            dw_ref[...] = part

        @pl.when(pl.program_id(0) > 0)
        def _():
            dw_ref[...] += part

    row = pl.BlockSpec((tr, D), lambda i: (i, 0))
    vec = pl.BlockSpec((1, D), lambda i: (0, 0))
    return pl.pallas_call(
        body, name=name, grid=(T // tr,), in_specs=[row, vec, row, row], out_specs=[row, row, vec],
        out_shape=[jax.ShapeDtypeStruct((T, D), F32), jax.ShapeDtypeStruct((T, D), MXU),
                   jax.ShapeDtypeStruct((1, D), F32)],
        compiler_params=_params("arbitrary"),
    )(x, w, dh, dres)


def final_loss(x, w, target, *, name):
    T, D = x.shape
    tr = _row_tile(T)

    def body(x_ref, w_ref, t_ref, loss_ref, dx_ref, dxm_ref, dw_ref):
        xv = x_ref[...]
        wv = w_ref[...]
        r = lax.rsqrt(jnp.mean(xv * xv, axis=-1, keepdims=True) + EPS)
        err = xv * r * wv - t_ref[...]
        lpart = 0.5 * jnp.sum(jnp.mean(err * err, axis=-1, keepdims=True), axis=0, keepdims=True)
        dy = err * (1.0 / D)
        g = dy * wv
        dot = jnp.mean(g * xv, axis=-1, keepdims=True)
        dx = r * g - xv * (r * r * r * dot)
        dx_ref[...] = dx
        dxm_ref[...] = dx.astype(dxm_ref.dtype)
        part = jnp.sum(dy * xv * r, axis=0, keepdims=True)

        @pl.when(pl.program_id(0) == 0)
        def _():
            dw_ref[...] = part
            loss_ref[...] = lpart

        @pl.when(pl.program_id(0) > 0)
        def _():
            dw_ref[...] += part
            loss_ref[...] += lpart

    row = pl.BlockSpec((tr, D), lambda i: (i, 0))
    vec = pl.BlockSpec((1, D), lambda i: (0, 0))
    one = pl.BlockSpec((1, 1), lambda i: (0, 0))
    return pl.pallas_call(
        body, name=name, grid=(T // tr,), in_specs=[row, vec, row], out_specs=[one, row, row, vec],
        out_shape=[jax.ShapeDtypeStruct((1, 1), F32), jax.ShapeDtypeStruct((T, D), F32),
                   jax.ShapeDtypeStruct((T, D), MXU), jax.ShapeDtypeStruct((1, D), F32)],
        compiler_params=_params("arbitrary"),
    )(x, w, target)


def swiglu_fwd(gu, *, name):
    T, F2 = gu.shape
    F = F2 // 2
    tr = _row_tile(T)

    def body(g_ref, u_ref, o_ref):
        g = g_ref[...].astype(F32)
        o_ref[...] = (g * _sigmoid(g) * u_ref[...].astype(F32)).astype(o_ref.dtype)

    return pl.pallas_call(
        body, name=name, grid=(T // tr,),
        in_specs=[pl.BlockSpec((tr, F), lambda i: (i, 0)), pl.BlockSpec((tr, F), lambda i: (i, 1))],
        out_specs=pl.BlockSpec((tr, F), lambda i: (i, 0)),
        out_shape=jax.ShapeDtypeStruct((T, F), MXU), compiler_params=_params("parallel"),
    )(gu, gu)


def swiglu_bwd(gu, dact, *, name):
    T, F2 = gu.shape
    F = F2 // 2
    tr = _row_tile(T)

    def body(g_ref, u_ref, d_ref, o_ref):
        g = g_ref[...].astype(F32)
        d = d_ref[...].astype(F32)
        s = _sigmoid(g)
        o_ref[:, :F] = (d * u_ref[...].astype(F32) * s * (1.0 + g * (1.0 - s))).astype(o_ref.dtype)
        o_ref[:, F:] = (d * g * s).astype(o_ref.dtype)

    lo = pl.BlockSpec((tr, F), lambda i: (i, 0))
    hi = pl.BlockSpec((tr, F), lambda i: (i, 1))
    return pl.pallas_call(
        body, name=name, grid=(T // tr,), in_specs=[lo, hi, lo], out_specs=pl.BlockSpec((tr, F2), lambda i: (i, 0)),
        out_shape=jax.ShapeDtypeStruct((T, F2), MXU), compiler_params=_params("parallel"),
    )(gu, gu, dact)


def merge_fwd(gl, abr, pbr, sbr, *, name):
    T, D = abr.shape
    tr = _row_tile(T)

    def body(g0, g1, g2, a_ref, p_ref, s_ref, o_ref):
        m = sum(_sigmoid(g[...].astype(F32)) * b[...].astype(F32) for g, b in ((g0, a_ref), (g1, p_ref), (g2, s_ref)))
        o_ref[...] = m.astype(o_ref.dtype)

    row = pl.BlockSpec((tr, D), lambda i: (i, 0))
    gs = [pl.BlockSpec((tr, D), lambda i, j=j: (i, j)) for j in range(3)]
    return pl.pallas_call(
        body, name=name, grid=(T // tr,), in_specs=gs + [row, row, row], out_specs=row,
        out_shape=jax.ShapeDtypeStruct((T, D), MXU), compiler_params=_params("parallel"),
    )(gl, gl, gl, abr, pbr, sbr)


def merge_bwd(gl, abr, pbr, sbr, dm, *, name):
    T, D = abr.shape
    tr = _row_tile(T)

    def body(g0, g1, g2, a_ref, p_ref, s_ref, dm_ref, da_ref, dp_ref, ds_ref, dg_ref):
        d = dm_ref[...].astype(F32)
        for j, (g_ref, b_ref, db_ref) in enumerate(((g0, a_ref, da_ref), (g1, p_ref, dp_ref), (g2, s_ref, ds_ref))):
            s = _sigmoid(g_ref[...].astype(F32))
            db_ref[...] = (d * s).astype(db_ref.dtype)
            dg_ref[:, j * D:(j + 1) * D] = (d * b_ref[...].astype(F32) * s * (1.0 - s)).astype(dg_ref.dtype)

    row = pl.BlockSpec((tr, D), lambda i: (i, 0))
    gs = [pl.BlockSpec((tr, D), lambda i, j=j: (i, j)) for j in range(3)]
    return pl.pallas_call(
        body, name=name, grid=(T // tr,), in_specs=gs + [row] * 4,
        out_specs=[row] * 3 + [pl.BlockSpec((tr, 3 * D), lambda i: (i, 0))],
        out_shape=[jax.ShapeDtypeStruct((T, D), MXU)] * 3 + [jax.ShapeDtypeStruct((T, 3 * D), MXU)],
        compiler_params=_params("parallel"),
    )(gl, gl, gl, abr, pbr, sbr, dm)


def gnorm_fwd(y, z, w, *, name):
    T, DI = y.shape
    gw = DI // C.G
    tr = _row_tile(T)

    def body(y_ref, z_ref, w_ref, o_ref):
        for g in range(C.G):
            sl = slice(g * gw, (g + 1) * gw)
            zz = z_ref[:, sl]
            v = y_ref[:, sl] * (zz * _sigmoid(zz))
            r = lax.rsqrt(jnp.mean(v * v, axis=-1, keepdims=True) + EPS)
            o_ref[:, sl] = (v * r * w_ref[:, sl]).astype(o_ref.dtype)

    row = pl.BlockSpec((tr, DI), lambda i: (i, 0))
    return pl.pallas_call(
        body, name=name, grid=(T // tr,), in_specs=[row, row, pl.BlockSpec((1, DI), lambda i: (0, 0))],
        out_specs=row, out_shape=jax.ShapeDtypeStruct((T, DI), MXU), compiler_params=_params("parallel"),
    )(y, z, w)


def gnorm_bwd(y, z, w, do, *, name):
    T, DI = y.shape
    gw = DI // C.G
    tr = _row_tile(T)

    def body(y_ref, z_ref, w_ref, do_ref, dy_ref, dz_ref, dw_ref):
        first = pl.program_id(0) == 0
        for g in range(C.G):
            sl = slice(g * gw, (g + 1) * gw)
            zz = z_ref[:, sl]
            yy = y_ref[:, sl]
            s = _sigmoid(zz)
            sz = zz * s
            v = yy * sz
            r = lax.rsqrt(jnp.mean(v * v, axis=-1, keepdims=True) + EPS)
            dov = do_ref[:, sl].astype(F32)
            gg = dov * w_ref[:, sl]
            dot = jnp.mean(gg * v, axis=-1, keepdims=True)
            dv = r * gg - v * (r * r * r * dot)
            dy_ref[:, sl] = dv * sz
            dz_ref[:, sl] = (dv * yy * s * (1.0 + zz * (1.0 - s))).astype(dz_ref.dtype)
            part = jnp.sum(dov * v * r, axis=0, keepdims=True)

            @pl.when(first)
            def _():
                dw_ref[:, sl] = part

            @pl.when(jnp.logical_not(first))
            def _():
                dw_ref[:, sl] += part

    row = pl.BlockSpec((tr, DI), lambda i: (i, 0))
    vec = pl.BlockSpec((1, DI), lambda i: (0, 0))
    return pl.pallas_call(
        body, name=name, grid=(T // tr,), in_specs=[row, row, vec, row], out_specs=[row, row, vec],
        out_shape=[jax.ShapeDtypeStruct((T, DI), F32), jax.ShapeDtypeStruct((T, DI), MXU),
                   jax.ShapeDtypeStruct((1, DI), F32)],
        compiler_params=_params("arbitrary"),
    )(y, z, w, do)


def adamw(w, g, m, v, *, name):
    shape = w.shape
    cols = shape[-1]
    rows = w.size // cols
    w2, g2, m2, v2 = (t.reshape(rows, cols) for t in (w, g, m, v))
    tr = rows if rows * cols * 4 <= (2 << 20) else _pick(rows, (512, 256, 128, 64, 32, 16, 8))
    while tr * cols * 4 > (2 << 20) and tr % 16 == 0:
        tr //= 2
    c1 = 1.0 - ADAM_B1 ** ADAM_STEP
    c2 = 1.0 - ADAM_B2 ** ADAM_STEP

    def body(w_ref, g_ref, m_ref, v_ref, d_ref, nm_ref, nv_ref):
        gv = g_ref[...]
        nm = ADAM_B1 * m_ref[...] + (1.0 - ADAM_B1) * gv
        nv = ADAM_B2 * v_ref[...] + (1.0 - ADAM_B2) * (gv * gv)
        d_ref[...] = -ADAM_LR * ((nm / c1) / (jnp.sqrt(nv / c2) + ADAM_EPS) + ADAM_WD * w_ref[...])
        nm_ref[...] = nm
        nv_ref[...] = nv

    row = pl.BlockSpec((tr, cols), lambda i: (i, 0))
    outs = pl.pallas_call(
        body, name=name, grid=(rows // tr,), in_specs=[row] * 4, out_specs=[row] * 3,
        out_shape=[jax.ShapeDtypeStruct((rows, cols), F32)] * 3, compiler_params=_params("parallel"),
    )(w2, g2, m2, v2)
    return tuple(o.reshape(shape) for o in outs)


def adamw_layers(w, g_layers, m, v, *, name):
    shape = w.shape
    L = shape[0]
    rows, cols = g_layers[0].shape
    w3, m3, v3 = (t.reshape(L, rows, cols) for t in (w, m, v))
    tr = _pick(rows, (512, 256, 128, 64, 32, 16, 8))
    while tr * cols * 4 > (2 << 20) and tr % 16 == 0:
        tr //= 2
    c1 = 1.0 - ADAM_B1 ** ADAM_STEP
    c2 = 1.0 - ADAM_B2 ** ADAM_STEP

    def body(*refs):
        w_ref, m_ref, v_ref = refs[0], refs[1], refs[2]
        g_refs = refs[3:3 + L]
        go_ref, d_ref, nm_ref, nv_ref = refs[3 + L:]
        layer = pl.program_id(0)
        gv = g_refs[0][...]
        for k in range(1, L):
            gv = jnp.where(layer == k, g_refs[k][...], gv)
        nm = ADAM_B1 * m_ref[...] + (1.0 - ADAM_B1) * gv
        nv = ADAM_B2 * v_ref[...] + (1.0 - ADAM_B2) * (gv * gv)
        d_ref[...] = -ADAM_LR * ((nm / c1) / (jnp.sqrt(nv / c2) + ADAM_EPS) + ADAM_WD * w_ref[...])
        go_ref[...] = gv
        nm_ref[...] = nm
        nv_ref[...] = nv

    blk = pl.BlockSpec((None, tr, cols), lambda l, i: (l, i, 0))
    gblks = [pl.BlockSpec((tr, cols), lambda l, i, k=k: (jnp.where(l == k, i, 0), 0)) for k in range(L)]
    outs = pl.pallas_call(
        body, name=name, grid=(L, rows // tr), in_specs=[blk] * 3 + gblks, out_specs=[blk] * 4,
        out_shape=[jax.ShapeDtypeStruct((L, rows, cols), F32)] * 4, compiler_params=_params("parallel", "parallel"),
    )(w3, m3, v3, *g_layers)
    return tuple(o.reshape(shape) for o in outs)


def _attn_masks(i, heads):
    B = C.blk
    row = lax.broadcasted_iota(jnp.int32, (heads * B, 2 * B), 0) & (B - 1)
    col = lax.broadcasted_iota(jnp.int32, (heads * B, 2 * B), 1)
    diff = row + B - col
    return (diff >= 0) & (diff < B) & ((col >= B) | (i > 0))


def _stack_heads(ref, kvh, ppk, lo):
    parts = []
    for pr in range(ppk):
        pair = kvh * ppk + pr
        qp = ref[:, pair * 128:(pair + 1) * 128].astype(F32)
        parts += [jnp.where(lo, qp, 0.0), jnp.where(lo, 0.0, qp)]
    return jnp.concatenate(parts, axis=0)


def _unstack_heads(ref, val, kvh, ppk, lo):
    B = C.blk
    for pr in range(ppk):
        pair = kvh * ppk + pr
        ref[:, pair * 128:(pair + 1) * 128] = jnp.where(lo, val[2 * pr * B:(2 * pr + 1) * B],
                                                        val[(2 * pr + 1) * B:(2 * pr + 2) * B]).astype(ref.dtype)


def _sink_scores(sink_ref, kvh, ppk):
    B = C.blk
    h0 = kvh * 2 * ppk
    return jnp.concatenate([jnp.full((B, 2 * B), sink_ref[0, h], F32) for h in range(h0, h0 + 2 * ppk)], axis=0)


def _kv2(prev_ref, cur_ref, m, lo):
    sl = slice(m * 128, (m + 1) * 128)
    slab = jnp.concatenate([prev_ref[:, sl], cur_ref[:, sl]], axis=0).astype(F32)
    slab = jnp.where(lax.broadcasted_iota(jnp.int32, (slab.shape[0], 1), 0) == 0, 0.0, slab)
    rolled = pltpu.roll(slab, 64, axis=1)
    return jnp.where(lo, slab, rolled), jnp.where(lo, rolled, slab)


def _attn_weights(q4, k2, mask, sink_scores):
    s = _dot_nt(q4, k2) * (C.hd ** -0.5)
    col0 = lax.broadcasted_iota(jnp.int32, (1, s.shape[1]), 1) == 0
    s = jnp.where(col0, sink_scores, jnp.where(mask, s, NEG))
    return jnp.exp(s - jnp.max(s, axis=-1, keepdims=True))


def _row_sums(p):
    return lax.dot_general(p, jnp.ones((p.shape[1], 128), F32), (((1,), (0,)), ((), ())),
                           precision=lax.Precision.HIGHEST, preferred_element_type=F32)


def attn_fwd(qkv, sink, *, name):
    T = qkv.shape[0]
    B, AW, KW = C.blk, C.AW, C.KW
    assert C.hd == 64 and KW % 128 == 0 and (C.qh // C.kvh) % 2 == 0
    nb = T // B
    kb = AW // KW
    ppk = C.qh // C.kvh // 2

    def body(q_ref, kc_ref, kp_ref, vc_ref, vp_ref, sink_ref, o_ref):
        i = pl.program_id(0)
        lo = lax.broadcasted_iota(jnp.int32, (1, 128), 1) < 64
        mask = _attn_masks(i, 2 * ppk)
        for m in range(KW // 128):
            k2s = _kv2(kp_ref, kc_ref, m, lo)
            v2s = _kv2(vp_ref, vc_ref, m, lo)
            for par in range(2):
                kvh = 2 * m + par
                q4 = _stack_heads(q_ref, kvh, ppk, lo)
                p = _attn_weights(q4, k2s[par], mask, _sink_scores(sink_ref, kvh, ppk))
                _unstack_heads(o_ref, _dot_nn(p, v2s[par]) / _row_sums(p), kvh, ppk, lo)

    prev = lambda i: jnp.maximum(i - 1, 0)
    return pl.pallas_call(
        body, name=name, grid=(nb,),
        in_specs=[pl.BlockSpec((B, AW), lambda i: (i, 0)),
                  pl.BlockSpec((B, KW), lambda i: (i, kb)), pl.BlockSpec((B, KW), lambda i: (prev(i), kb)),
                  pl.BlockSpec((B, KW), lambda i: (i, kb + 1)), pl.BlockSpec((B, KW), lambda i: (prev(i), kb + 1)),
                  pl.BlockSpec(memory_space=pltpu.SMEM)],
        out_specs=pl.BlockSpec((B, AW), lambda i: (i, 0)),
        out_shape=jax.ShapeDtypeStruct((T, AW), MXU), compiler_params=_params("parallel"),
    )(qkv, qkv, qkv, qkv, qkv, sink)


def attn_bwd(qkv, sink, dout, *, name):
    T = qkv.shape[0]
    B, AW, KW = C.blk, C.AW, C.KW
    nb = T // B
    kb = AW // KW
    ppk = C.qh // C.kvh // 2
    scale = C.hd ** -0.5

    def body(q_ref, kc_ref, kp_ref, vc_ref, vp_ref, sink_ref, do_ref, dq_ref, dk_ref, dv_ref, ds_ref, ck_ref, cv_ref):
        i = pl.program_id(0)

        @pl.when(i == 0)
        def _():
            ck_ref[...] = jnp.zeros_like(ck_ref)
            cv_ref[...] = jnp.zeros_like(cv_ref)
            ds_ref[...] = jnp.zeros_like(ds_ref)

        @pl.when(i < nb)
        def _():
            lane = lax.broadcasted_iota(jnp.int32, (1, 128), 1)
            lo = lane < 64
            hlane = lax.broadcasted_iota(jnp.int32, (1, C.qh), 1)
            mask = _attn_masks(i, 2 * ppk)
            dsink = jnp.zeros((1, C.qh), F32)
            for m in range(KW // 128):
                ksl = slice(m * 128, (m + 1) * 128)
                k2s = _kv2(kp_ref, kc_ref, m, lo)
                v2s = _kv2(vp_ref, vc_ref, m, lo)
                folded = []
                for par in range(2):
                    kvh = 2 * m + par
                    q4 = _stack_heads(q_ref, kvh, ppk, lo)
                    do4 = _stack_heads(do_ref, kvh, ppk, lo)
                    p = _attn_weights(q4, k2s[par], mask, _sink_scores(sink_ref, kvh, ppk))
                    inv = 1.0 / _row_sums(p)
                    p = p * jnp.concatenate([inv, inv], axis=1)
                    dp = _dot_nt(do4, v2s[par])
                    delta = _row_sums(p * dp)
                    dsc = p * (dp - jnp.concatenate([delta, delta], axis=1))
                    _unstack_heads(dq_ref, _dot_nn(dsc, k2s[par]) * scale, kvh, ppk, lo)
                    dk2 = _dot_tn(dsc, q4) * scale
                    dv2 = _dot_tn(p, do4)
                    for k in range(2 * ppk):
                        dsh = jnp.sum(dsc[k * B:(k + 1) * B, :128], axis=0, keepdims=True)[:, :1]
                        dsink = dsink + jnp.where(hlane == kvh * 2 * ppk + k, dsh, 0.0)
                    folded.append((dk2 + pltpu.roll(dk2, 64, axis=1), dv2 + pltpu.roll(dv2, 64, axis=1)))
                row0 = lax.broadcasted_iota(jnp.int32, (2 * B, 1), 0) == 0
                dks = jnp.where(row0, 0.0, jnp.where(lo, folded[0][0], folded[1][0]))
                dvs = jnp.where(row0, 0.0, jnp.where(lo, folded[0][1], folded[1][1]))
                dk_ref[:, ksl] = (ck_ref[:, ksl] + dks[:B]).astype(dk_ref.dtype)
                dv_ref[:, ksl] = (cv_ref[:, ksl] + dvs[:B]).astype(dv_ref.dtype)
                ck_ref[:, ksl] = dks[B:]
                cv_ref[:, ksl] = dvs[B:]
            ds_ref[...] += dsink

        @pl.when(i == nb)
        def _():
            dk_ref[...] = ck_ref[...].astype(dk_ref.dtype)
            dv_ref[...] = cv_ref[...].astype(dv_ref.dtype)

    cur = lambda i: jnp.minimum(i, nb - 1)
    prev = lambda i: jnp.maximum(jnp.minimum(i, nb - 1) - 1, 0)
    out = lambda i: jnp.maximum(i - 1, 0)
    return pl.pallas_call(
        body, name=name, grid=(nb + 1,),
        in_specs=[pl.BlockSpec((B, AW), lambda i: (cur(i), 0)),
                  pl.BlockSpec((B, KW), lambda i: (cur(i), kb)), pl.BlockSpec((B, KW), lambda i: (prev(i), kb)),
                  pl.BlockSpec((B, KW), lambda i: (cur(i), kb + 1)), pl.BlockSpec((B, KW), lambda i: (prev(i), kb + 1)),
                  pl.BlockSpec(memory_space=pltpu.SMEM),
                  pl.BlockSpec((B, AW), lambda i: (cur(i), 0))],
        out_specs=[pl.BlockSpec((B, AW), lambda i: (cur(i), 0)),
                   pl.BlockSpec((B, KW), lambda i: (out(i), 0)), pl.BlockSpec((B, KW), lambda i: (out(i), 0)),
                   pl.BlockSpec((1, C.qh), lambda i: (0, 0))],
        out_shape=[jax.ShapeDtypeStruct((T, AW), MXU), jax.ShapeDtypeStruct((T, KW), MXU),
                   jax.ShapeDtypeStruct((T, KW), MXU), jax.ShapeDtypeStruct((1, C.qh), F32)],
        scratch_shapes=[pltpu.VMEM((B, KW), F32), pltpu.VMEM((B, KW), F32)],
        compiler_params=_params("arbitrary"),
    )(qkv, qkv, qkv, qkv, qkv, sink, dout)


POOL_HALO = 16


def _window_sum(e, w, n, forward):
    s, k = e, 1
    while k < w:
        s = s + pltpu.roll(s, (n - k) if forward else k, axis=0)
        k *= 2
    return s


def pool_fwd(u, pw, ps, *, name):
    T, PW = u.shape
    PG = C.PG
    tT = _pick(T, (256, 128))
    hb = tT // POOL_HALO

    def body(u_ref, uh_ref, pw_ref, ps_ref, o_ref, mx_ref):
        i = pl.program_id(0)
        halo = jnp.where(i > 0, uh_ref[...], 0.0)
        ext = jnp.concatenate([halo, u_ref[...]], axis=0)
        t = i * tT + lax.broadcasted_iota(jnp.int32, (tT, 1), 0)
        for g, w in enumerate(POOL_WINDOWS):
            sl = slice(g * PG, (g + 1) * PG)
            s = _window_sum(ext[:, sl], w, tT + POOL_HALO, False)[POOL_HALO:]
            cnt = jnp.minimum(t + 1, w).astype(F32)
            mixed = (s / cnt - u_ref[:, sl]).astype(MXU)
            mx_ref[:, sl] = mixed.astype(mx_ref.dtype)
            o_ref[:, sl] = (_dot_nn(mixed, pw_ref[g]) * ps_ref[:, sl]).astype(o_ref.dtype)

    row = pl.BlockSpec((tT, PW), lambda i: (i, 0))
    return pl.pallas_call(
        body, name=name, grid=(T // tT,),
        in_specs=[row, pl.BlockSpec((POOL_HALO, PW), lambda i: (jnp.maximum(i * hb - 1, 0), 0)),
                  pl.BlockSpec((4, PG, PG), lambda i: (0, 0, 0)), pl.BlockSpec((1, PW), lambda i: (0, 0))],
        out_specs=[row, row], out_shape=[jax.ShapeDtypeStruct((T, PW), MXU)] * 2,
        compiler_params=_params("parallel"),
    )(u, u, pw, ps)


def pool_bwd(dpool, mixed, pw, ps, *, name):
    T, PW = dpool.shape
    PG = C.PG
    tT = _pick(T, (256, 128))
    hb = tT // POOL_HALO
    n = T // tT
    rows = tT + POOL_HALO

    def body(dp_ref, dph_ref, mx_ref, pw_ref, ps_ref, du_ref, dpw_ref, dps_ref):
        i = pl.program_id(0)

        @pl.when(i == 0)
        def _():
            dpw_ref[...] = jnp.zeros_like(dpw_ref)
            dps_ref[...] = jnp.zeros_like(dps_ref)

        halo = jnp.where(i < n - 1, dph_ref[...].astype(F32), 0.0)
        dext = jnp.concatenate([dp_ref[...].astype(F32), halo], axis=0)
        t = i * tT + lax.broadcasted_iota(jnp.int32, (rows, 1), 0)
        for g, w in enumerate(POOL_WINDOWS):
            sl = slice(g * PG, (g + 1) * PG)
            dyg = dext[:, sl] * ps_ref[:, sl]
            dmix = _dot_nt(dyg, pw_ref[g])
            cnt = jnp.minimum(t + 1, w).astype(F32)
            s = _window_sum(dmix / cnt, w, rows, True)
            du_ref[:, sl] = (s[:tT] - dmix[:tT]).astype(du_ref.dtype)
            mb = mx_ref[:, sl]
            dpw_ref[g] += _dot_tn(mb, dyg[:tT])
            dps_ref[:, sl] += jnp.sum(dext[:tT, sl] * _dot_nn(mb, pw_ref[g]), axis=0, keepdims=True)

    row = pl.BlockSpec((tT, PW), lambda i: (i, 0))
    return pl.pallas_call(
        body, name=name, grid=(n,),
        in_specs=[row, pl.BlockSpec((POOL_HALO, PW), lambda i: (jnp.minimum((i + 1) * hb, T // POOL_HALO - 1), 0)), row,
                  pl.BlockSpec((4, PG, PG), lambda i: (0, 0, 0)), pl.BlockSpec((1, PW), lambda i: (0, 0))],
        out_specs=[row, pl.BlockSpec((4, PG, PG), lambda i: (0, 0, 0)), pl.BlockSpec((1, PW), lambda i: (0, 0))],
        out_shape=[jax.ShapeDtypeStruct((T, PW), MXU), jax.ShapeDtypeStruct((4, PG, PG), F32),
                   jax.ShapeDtypeStruct((1, PW), F32)],
        compiler_params=_params("arbitrary"),
    )(dpool, dpool, mixed, pw, ps)


CONV_HALO = 8
CONV_K = 4


def _conv(x, cw_ref, cb_ref, sl):
    acc = cb_ref[:, sl] + cw_ref[CONV_K - 1:CONV_K, sl] * x
    for k in range(CONV_K - 1):
        acc = acc + cw_ref[k:k + 1, sl] * pltpu.roll(x, CONV_K - 1 - k, axis=0)
    return acc


def conv_fwd(xbc, cw, cb, *, name):
    T, CC = xbc.shape
    tT = _pick(T, (256, 128))
    hb = tT // CONV_HALO
    cs = _pick(CC, (512, 256, 128))

    def body(x_ref, xh_ref, cw_ref, cb_ref, o_ref):
        i = pl.program_id(0)
        for c0 in range(0, CC, cs):
            sl = slice(c0, c0 + cs)
            ext = jnp.concatenate([jnp.where(i > 0, xh_ref[:, sl], 0.0), x_ref[:, sl]], axis=0)
            xc = _conv(ext, cw_ref, cb_ref, sl)[CONV_HALO:]
            o_ref[:, sl] = xc * _sigmoid(xc)

    row = pl.BlockSpec((tT, CC), lambda i: (i, 0))
    return pl.pallas_call(
        body, name=name, grid=(T // tT,),
        in_specs=[row, pl.BlockSpec((CONV_HALO, CC), lambda i: (jnp.maximum(i * hb - 1, 0), 0)),
                  pl.BlockSpec((CONV_K, CC), lambda i: (0, 0)), pl.BlockSpec((1, CC), lambda i: (0, 0))],
        out_specs=row, out_shape=jax.ShapeDtypeStruct((T, CC), F32), compiler_params=_params("parallel"),
    )(xbc, xbc, cw, cb)


def conv_bwd(xbc, cw, cb, dxa, *, name):
    T, CC = xbc.shape
    tT = _pick(T, (256, 128))
    hb = tT // CONV_HALO
    n = T // tT
    rows = tT + 2 * CONV_HALO
    cs = _pick(CC, (512, 256, 128))

    def body(x_ref, xp_ref, xn_ref, cw_ref, cb_ref, d_ref, dn_ref, dx_ref, dcw_ref, dcb_ref):
        i = pl.program_id(0)

        @pl.when(i == 0)
        def _():
            dcw_ref[...] = jnp.zeros_like(dcw_ref)
            dcb_ref[...] = jnp.zeros_like(dcb_ref)

        r = lax.broadcasted_iota(jnp.int32, (rows, 1), 0)
        own = (r >= CONV_HALO) & (r < tT + CONV_HALO)
        for c0 in range(0, CC, cs):
            sl = slice(c0, c0 + cs)
            x = jnp.concatenate([jnp.where(i > 0, xp_ref[:, sl], 0.0), x_ref[:, sl],
                                 jnp.where(i < n - 1, xn_ref[:, sl], 0.0)], axis=0)
            da = jnp.concatenate([jnp.zeros((CONV_HALO, cs), F32), d_ref[:, sl],
                                  jnp.where(i < n - 1, dn_ref[:, sl], 0.0)], axis=0)
            xc = _conv(x, cw_ref, cb_ref, sl)
            sg = _sigmoid(xc)
            dxc = da * sg * (1.0 + xc * (1.0 - sg))
            acc = cw_ref[CONV_K - 1:CONV_K, sl] * dxc
            for k in range(CONV_K - 1):
                acc = acc + cw_ref[k:k + 1, sl] * pltpu.roll(dxc, rows - (CONV_K - 1 - k), axis=0)
            dx_ref[:, sl] = acc[CONV_HALO:tT + CONV_HALO].astype(dx_ref.dtype)
            down = jnp.where(own, dxc, 0.0)
            dcb_ref[:, sl] += jnp.sum(down, axis=0, keepdims=True)
            dcw_ref[CONV_K - 1:CONV_K, sl] += jnp.sum(down * x, axis=0, keepdims=True)
            for k in range(CONV_K - 1):
                dcw_ref[k:k + 1, sl] += jnp.sum(down * pltpu.roll(x, CONV_K - 1 - k, axis=0), axis=0, keepdims=True)

    row = pl.BlockSpec((tT, CC), lambda i: (i, 0))
    prev = pl.BlockSpec((CONV_HALO, CC), lambda i: (jnp.maximum(i * hb - 1, 0), 0))
    nxt = pl.BlockSpec((CONV_HALO, CC), lambda i: (jnp.minimum((i + 1) * hb, T // CONV_HALO - 1), 0))
    return pl.pallas_call(
        body, name=name, grid=(n,),
        in_specs=[row, prev, nxt, pl.BlockSpec((CONV_K, CC), lambda i: (0, 0)), pl.BlockSpec((1, CC), lambda i: (0, 0)),
                  row, nxt],
        out_specs=[row, pl.BlockSpec((CONV_K, CC), lambda i: (0, 0)), pl.BlockSpec((1, CC), lambda i: (0, 0))],
        out_shape=[jax.ShapeDtypeStruct((T, CC), MXU), jax.ShapeDtypeStruct((CONV_K, CC), F32),
                   jax.ShapeDtypeStruct((1, CC), F32)],
        compiler_params=_params("arbitrary"),
    )(xbc, xbc, xbc, cw, cb, dxa, dxa)


def _softplus(x):
    return jnp.maximum(x, 0.0) + jnp.log(1.0 + jnp.exp(-jnp.abs(x)))


def _dot_exact(a, b):
    return lax.dot_general(a, b, (((1,), (0,)), ((), ())), precision=lax.Precision.HIGHEST, preferred_element_type=F32)


def ssd_prep(raw_t, bias, alog, *, name):
    H, T = raw_t.shape
    B = C.blk
    tc = _pick(T, (4 * B, 2 * B, B))

    def body(r_ref, b_ref, al_ref, dt_ref, acs_ref):
        dt = _softplus(r_ref[...] + b_ref[...])
        dt_ref[...] = dt
        dta = dt * (-jnp.exp(al_ref[...]))
        upper = (lax.broadcasted_iota(jnp.int32, (B, B), 0) <= lax.broadcasted_iota(jnp.int32, (B, B), 1)).astype(F32)
        for j in range(tc // B):
            acs_ref[:, j * B:(j + 1) * B] = _dot_exact(dta[:, j * B:(j + 1) * B], upper)

    blk = pl.BlockSpec((H, tc), lambda i: (0, i))
    vec = pl.BlockSpec((H, 1), lambda i: (0, 0))
    return pl.pallas_call(
        body, name=name, grid=(T // tc,), in_specs=[blk, vec, vec], out_specs=[blk, blk],
        out_shape=[jax.ShapeDtypeStruct((H, T), F32)] * 2, compiler_params=_params("parallel"),
    )(raw_t, bias, alog)


def _pair(lo, arr, h0, rows=slice(None)):
    return jnp.where(lo, arr[rows, h0:h0 + 1], arr[rows, h0 + 1:h0 + 2])


def _decay(acs, acs_t, h, causal):
    return jnp.exp(jnp.where(causal, acs[:, h:h + 1] - acs_t[h:h + 1, :], NEG))


def ssd_fwd(xa, dt, acs, acs_t, dskip, *, name):
    T = xa.shape[0]
    B, DI, G, N, HG, P, H = C.blk, C.DI, C.G, C.N, C.HG, C.P, C.H
    assert P == 64 and HG % 2 == 0
    nc = T // B
    W = HG * P

    def body(xa_ref, dt_ref, acs_ref, acst_ref, ds_ref, y_ref, hp_ref, h_scr):
        @pl.when(pl.program_id(0) == 0)
        def _():
            h_scr[...] = jnp.zeros_like(h_scr)

        lo = lax.broadcasted_iota(jnp.int32, (1, 128), 1) < 64
        causal = lax.broadcasted_iota(jnp.int32, (B, B), 0) >= lax.broadcasted_iota(jnp.int32, (B, B), 1)
        dt, acs, acs_t, dsk = dt_ref[...], acs_ref[...], acst_ref[...], ds_ref[...]
        for g in range(G):
            bg = xa_ref[:, DI + g * N:DI + (g + 1) * N]
            cg = xa_ref[:, DI + (G + g) * N:DI + (G + g + 1) * N]
            cb = _dot_nt(cg, bg)
            hg = h_scr[g]
            hp_ref[0, g * N:(g + 1) * N, :] = hg
            yoff = _dot_nn(cg, hg)
            xws, decs = [], []
            for j in range(HG // 2):
                h0 = g * HG + 2 * j
                xsl = slice(h0 * P, (h0 + 2) * P)
                xp = xa_ref[:, xsl]
                ap = _pair(lo, acs, h0)
                alast = _pair(lo, acs, h0, slice(B - 1, B))
                xdt = xp * _pair(lo, dt, h0)
                ys = [_dot_nn(cb * _decay(acs, acs_t, h0 + half, causal), xdt) for half in range(2)]
                y_ref[:, xsl] = (jnp.where(lo, ys[0], ys[1]) + yoff[:, 2 * j * P:(2 * j + 2) * P] * jnp.exp(ap)
                                 + _pair(lo, dsk, h0) * xp)
                xws.append(xdt * jnp.exp(alast - ap))
                decs.append(jnp.exp(alast))
            h_scr[g] = hg * jnp.concatenate(decs, axis=1) + _dot_tn(bg, jnp.concatenate(xws, axis=1))

    tok = lambda w: pl.BlockSpec((B, w), lambda c: (c, 0))
    return pl.pallas_call(
        body, name=name, grid=(nc,),
        in_specs=[tok(C.CC), tok(H), tok(H), pl.BlockSpec((H, B), lambda c: (0, c)), pl.BlockSpec((1, H), lambda c: (0, 0))],
        out_specs=[tok(DI), pl.BlockSpec((1, G * N, W), lambda c: (c, 0, 0))],
        out_shape=[jax.ShapeDtypeStruct((T, DI), F32), jax.ShapeDtypeStruct((nc, G * N, W), F32)],
        scratch_shapes=[pltpu.VMEM((G, N, W), F32)],
        compiler_params=_params("arbitrary"),
    )(xa, dt, acs, acs_t, dskip)


def ssd_bwd(xa, dt, acs, acs_t, dskip, hprev, dy, *, name):
    T = xa.shape[0]
    B, DI, G, N, HG, P, H = C.blk, C.DI, C.G, C.N, C.HG, C.P, C.H
    nc = T // B
    W = HG * P

    def body(xa_ref, dt_ref, acs_ref, acst_ref, ds_ref, hp_ref, dy_ref, dxa_ref, ddt_ref, dacs_ref, dd_ref, dh_scr):
        @pl.when(pl.program_id(0) == 0)
        def _():
            dh_scr[...] = jnp.zeros_like(dh_scr)
            dd_ref[...] = jnp.zeros_like(dd_ref)

        lo = lax.broadcasted_iota(jnp.int32, (1, 128), 1) < 64
        hi = jnp.logical_not(lo)
        causal = lax.broadcasted_iota(jnp.int32, (B, B), 0) >= lax.broadcasted_iota(jnp.int32, (B, B), 1)
        hlane = lax.broadcasted_iota(jnp.int32, (1, H), 1)
        hsub = lax.broadcasted_iota(jnp.int32, (B, 1), 0)
        lastrow = lax.broadcasted_iota(jnp.int32, (B, 1), 0) == B - 1
        dt, acs, acs_t, dsk = dt_ref[...], acs_ref[...], acst_ref[...], ds_ref[...]
        d_acs = jnp.zeros((B, H), F32)
        d_acs_t = jnp.zeros((B, B), F32)
        d_dt = jnp.zeros((B, H), F32)
        d_d = jnp.zeros((1, H), F32)

        def rsum(v):
            return (jnp.sum(jnp.where(lo, v, 0.0), axis=1, keepdims=True),
                    jnp.sum(jnp.where(hi, v, 0.0), axis=1, keepdims=True))

        for g in range(G):
            bsl = slice(DI + g * N, DI + (g + 1) * N)
            csl = slice(DI + (G + g) * N, DI + (G + g + 1) * N)
            bg, cg = xa_ref[:, bsl], xa_ref[:, csl]
            cb = _dot_nt(cg, bg)
            hg = hp_ref[0, g * N:(g + 1) * N, :]
            dhg = dh_scr[g]
            yoff = _dot_nn(cg, hg)
            bds = _dot_nn(bg, dhg)
            d_cb = jnp.zeros((B, B), F32)
            dyes, xws, decs = [], [], []
            for j in range(HG // 2):
                h0 = g * HG + 2 * j
                xsl = slice(h0 * P, (h0 + 2) * P)
                psl = slice(2 * j * P, (2 * j + 2) * P)
                xp, dyp = xa_ref[:, xsl], dy_ref[:, xsl]
                dtp = _pair(lo, dt, h0)
                ap = _pair(lo, acs, h0)
                alast = _pair(lo, acs, h0, slice(B - 1, B))
                ea, ew, el = jnp.exp(ap), jnp.exp(alast - ap), jnp.exp(alast)
                xdt = xp * dtp
                halves = []
                for half in range(2):
                    h = h0 + half
                    lm = _decay(acs, acs_t, h, causal)
                    m = cb * lm
                    d_m = _dot_nt(jnp.where(lo if half == 0 else hi, dyp, 0.0), xdt)
                    d_cb = d_cb + d_m * lm
                    wgt = d_m * m
                    d_acs = d_acs + jnp.where(hlane == h, jnp.sum(wgt, axis=1, keepdims=True), 0.0)
                    d_acs_t = d_acs_t + jnp.where(hsub == h, jnp.sum(wgt, axis=0, keepdims=True), 0.0)
                    halves.append(_dot_tn(m, dyp))
                bdp = bds[:, psl]
                dxdt = jnp.where(lo, halves[0], halves[1]) + ew * bdp
                dxa_ref[:, xsl] = dtp * dxdt + _pair(lo, dsk, h0) * dyp
                xw = xdt * ew
                terms_dt = rsum(dxdt * xp)
                terms_dd = rsum(dyp * xp)
                terms_off = rsum(dyp * (ea * yoff[:, psl]))
                terms_e = rsum(xw * bdp)
                terms_h = rsum(hg[:, psl] * dhg[:, psl])
                for half in range(2):
                    h = h0 + half
                    sel = hlane == h
                    d_dt = d_dt + jnp.where(sel, terms_dt[half], 0.0)
                    d_d = d_d + jnp.where(sel, jnp.sum(terms_dd[half], axis=0, keepdims=True), 0.0)
                    e_last = jnp.sum(jnp.where(lo if half == 0 else hi, el, 0.0), axis=1, keepdims=True) * (1.0 / P)
                    d_last = (jnp.sum(terms_e[half], axis=0, keepdims=True)
                              + e_last * jnp.sum(terms_h[half], axis=0, keepdims=True))
                    d_acs = d_acs + jnp.where(sel, terms_off[half] - terms_e[half] + jnp.where(lastrow, d_last, 0.0), 0.0)
                dyes.append(dyp * ea)
                xws.append(xw)
                decs.append(el)
            dye = jnp.concatenate(dyes, axis=1)
            xwc = jnp.concatenate(xws, axis=1)
            dxa_ref[:, csl] = _dot_nn(d_cb, bg) + _dot_nt(dye, hg)
            dxa_ref[:, bsl] = _dot_tn(d_cb, cg) + _dot_nt(xwc, dhg)
            dh_scr[g] = dhg * jnp.concatenate(decs, axis=1) + _dot_tn(cg, dye)
        ddt_ref[...] = d_dt
        dacs_ref[...] = d_acs - d_acs_t.T[:, :H]
        dd_ref[...] += d_d

    rev = lambda w: pl.BlockSpec((B, w), lambda c: (nc - 1 - c, 0))
    vec = pl.BlockSpec((1, H), lambda c: (0, 0))
    return pl.pallas_call(
        body, name=name, grid=(nc,),
        in_specs=[rev(C.CC), rev(H), rev(H), pl.BlockSpec((H, B), lambda c: (0, nc - 1 - c)), vec,
                  pl.BlockSpec((1, G * N, W), lambda c: (nc - 1 - c, 0, 0)), rev(DI)],
        out_specs=[rev(C.CC), rev(H), rev(H), vec],
        out_shape=[jax.ShapeDtypeStruct((T, C.CC), F32), jax.ShapeDtypeStruct((T, H), F32),
                   jax.ShapeDtypeStruct((T, H), F32), jax.ShapeDtypeStruct((1, H), F32)],
        scratch_shapes=[pltpu.VMEM((G, N, W), F32)],
        compiler_params=_params("arbitrary"),
    )(xa, dt, acs, acs_t, dskip, hprev, dy)


def ssd_post(ddt, dacs, dt, raw, bias, alog, *, name):
    T, H = ddt.shape
    B = C.blk
    tc = _pick(T, (4 * B, 2 * B, B))

    def body(ddt_ref, dacs_ref, dt_ref, raw_ref, b_ref, al_ref, draw_ref, db_ref, dal_ref):
        @pl.when(pl.program_id(0) == 0)
        def _():
            db_ref[...] = jnp.zeros_like(db_ref)
            dal_ref[...] = jnp.zeros_like(dal_ref)

        a = -jnp.exp(al_ref[...])
        lower = (lax.broadcasted_iota(jnp.int32, (B, B), 0) <= lax.broadcasted_iota(jnp.int32, (B, B), 1)).astype(F32)
        for j in range(tc // B):
            sl = slice(j * B, (j + 1) * B)
            rc = _dot_exact(lower, dacs_ref[sl, :])
            dtv = dt_ref[sl, :]
            draw = (ddt_ref[sl, :] + a * rc) * _sigmoid(raw_ref[sl, :] + b_ref[...])
            draw_ref[sl, :] = draw
            db_ref[...] += jnp.sum(draw, axis=0, keepdims=True)
            dal_ref[...] += jnp.sum(dtv * rc, axis=0, keepdims=True) * a

    blk = pl.BlockSpec((tc, H), lambda i: (i, 0))
    vec = pl.BlockSpec((1, H), lambda i: (0, 0))
    return pl.pallas_call(
        body, name=name, grid=(T // tc,), in_specs=[blk, blk, blk, blk, vec, vec], out_specs=[blk, vec, vec],
        out_shape=[jax.ShapeDtypeStruct((T, H), F32), jax.ShapeDtypeStruct((1, H), F32), jax.ShapeDtypeStruct((1, H), F32)],
        compiler_params=_params("arbitrary"),
    )(ddt, dacs, dt, raw, bias, alog)


MESH = pl.DeviceIdType.MESH
PACK_W = 1024
ANY = pl.BlockSpec(memory_space=pl.ANY)


def _place():
    return lax.axis_index("x"), lax.axis_index("y"), lax.axis_index("c")


def _other_chips(x, y):
    return [(1 - x, y), (x, 1 - y), (1 - x, 1 - y)]


def _remote(src, dst, send_sems, recv_sems, k, to):
    return pltpu.make_async_remote_copy(src_ref=src, dst_ref=dst, send_sem=send_sems.at[k], recv_sem=recv_sems.at[k],
                                        device_id=to, device_id_type=MESH)


def _half(c, rows):
    return pl.ds(pl.multiple_of(c * (rows // 2), 16), rows // 2)


def gather_layer(shards, *, name):
    n = len(shards)

    def body(*refs):
        ps, gs = refs[:n], refs[n:2 * n]
        send_sems, recv_sems = refs[2 * n:]
        x, y, c = _place()
        s = 2 * x + y
        sib = (x, y, 1 - c)
        chips = _other_chips(x, y)
        copy = functools.partial(_remote, send_sems=send_sems, recv_sems=recv_sems)
        started = []
        for w in range(n):
            started.append(copy(ps[w], gs[w].at[s], k=w, to=sib))
        for w in range(n):
            mine = _half(c, ps[w].shape[0])
            for j, (px, py) in enumerate(chips):
                started.append(copy(ps[w].at[mine], gs[w].at[s, mine], k=n + 3 * w + j, to=(px, py, c)))
        for cp in started:
            cp.start()
        passed = []
        for w in range(n):
            mine = _half(c, ps[w].shape[0])
            for j, (px, py) in enumerate(chips):
                there = gs[w].at[2 * px + py, mine]
                copy(there, there, k=n + 3 * w + j, to=(px, py, c)).wait_recv()
                fw = copy(there, there, k=4 * n + 3 * w + j, to=sib)
                fw.start()
                passed.append(fw)
        for w in range(n):
            copy(ps[w], gs[w].at[s], k=w, to=sib).wait_recv()
            theirs = _half(1 - c, ps[w].shape[0])
            for j, (px, py) in enumerate(chips):
                there = gs[w].at[2 * px + py, theirs]
                copy(there, there, k=4 * n + 3 * w + j, to=sib).wait_recv()
        for cp in started + passed:
            cp.wait_send()

    return pl.pallas_call(
        body, name=name, in_specs=[ANY] * n, out_specs=[ANY] * n,
        out_shape=[jax.ShapeDtypeStruct((4,) + p.shape, p.dtype) for p in shards],
        scratch_shapes=[pltpu.SemaphoreType.DMA((7 * n,)), pltpu.SemaphoreType.DMA((7 * n,))],
    )(*shards)


HBM = pl.BlockSpec(memory_space=pltpu.HBM)
SEMS = pl.BlockSpec(memory_space=pltpu.SEMAPHORE)


def gather_plan(x, y, c, ps, gs):
    s = 2 * x + y
    sends, lands = [], []
    for p, g in zip(ps, gs):
        rows = p.shape[0]
        sends.append((p, g.at[s], (x, y, 1 - c)))
        lands.append(g.at[s])
        for px, py in _other_chips(x, y):
            for pc in (c, 1 - c):
                sends.append((p.at[_half(c, rows)], g.at[s, _half(c, rows)], (px, py, pc)))
                lands.append(g.at[2 * px + py, _half(pc, rows)])
    return sends, lands


def scatter_plan(x, y, c, ps, gs):
    sends, lands = [], []
    for p, g in zip(ps, gs):
        for j, (px, py) in enumerate(_other_chips(x, y)):
            sends.append((p.at[2 * px + py], g.at[j], (px, py, c)))
            lands.append(g.at[j])
    return sends, lands


def _hbm(a):
    return pltpu.with_memory_space_constraint(a, pltpu.HBM)


def exchange_start(srcs, land_shapes, plan, ncopies, *, name):
    n = len(srcs)

    def body(*refs):
        ps, gs = refs[:n], refs[n:2 * n]
        send_sems, recv_sems = refs[2 * n], refs[2 * n + 1]
        token = refs[-1]
        sends, _ = plan(*_place(), ps, gs)
        for k, (src, dst, to) in enumerate(sends):
            _remote(src, dst, send_sems, recv_sems, k, to).start()
        token[...] = jnp.zeros_like(token)

    lands = [_hbm(lax.empty(s.shape, s.dtype)) for s in land_shapes]
    outs = pl.pallas_call(
        body, name=name,
        out_shape=[pltpu.SemaphoreType.DMA((ncopies,)), pltpu.SemaphoreType.DMA((ncopies,))]
        + [pltpu.HBM(a.shape, a.dtype) for a in srcs] + [pltpu.HBM(s.shape, s.dtype) for s in land_shapes]
        + [jax.ShapeDtypeStruct((8, 128), F32)],
        in_specs=[HBM] * (2 * n), out_specs=[SEMS, SEMS] + [HBM] * (2 * n) + [pl.BlockSpec(memory_space=pltpu.VMEM)],
        input_output_aliases={k: 2 + k for k in range(2 * n)},
        compiler_params=pltpu.CompilerParams(has_side_effects=pltpu.SideEffectType.DATAFLOW_SIDE_EFFECTING),
    )(*[_hbm(a) for a in srcs], *lands)
    return outs[0], outs[1], list(outs[2:2 + n]), list(outs[2 + n:2 + 2 * n]), outs[-1]


def exchange_wait(send_sems, recv_sems, srcs, lands, plan, after, *, name):
    n = len(srcs)

    def body(*refs):
        ps, gs = refs[:n], refs[n:2 * n]
        send_sems, recv_sems = refs[2 * n], refs[2 * n + 1]
        sends, arrivals = plan(*_place(), ps, gs)
        for k, ((src, dst, to), land) in enumerate(zip(sends, arrivals)):
            _remote(src, dst, send_sems, recv_sems, k, to).wait_send()
            _remote(land, land, send_sems, recv_sems, k, to).wait_recv()

    outs = pl.pallas_call(
        body, name=name,
        out_shape=[pltpu.HBM(a.shape, a.dtype) for a in srcs] + [pltpu.HBM(a.shape, a.dtype) for a in lands],
        in_specs=[HBM] * (2 * n) + [SEMS, SEMS, ANY], out_specs=[HBM] * (2 * n),
        input_output_aliases={k: k for k in range(2 * n)},
        compiler_params=pltpu.CompilerParams(has_side_effects=pltpu.SideEffectType.DATAFLOW_SIDE_EFFECTING),
    )(*srcs, *lands, send_sems, recv_sems, after)
    return list(outs[n:])


def swap_halves(items, *, name):
    n = len(items)

    def body(*refs):
        gs, rs = refs[:n], refs[n:2 * n]
        send_sems, recv_sems = refs[2 * n:]
        x, y, c = _place()
        cps = [_remote(g.at[:, _half(1 - c, g.shape[1])], r, send_sems, recv_sems, k, (x, y, 1 - c))
               for k, (g, r) in enumerate(zip(gs, rs))]
        for cp in cps:
            cp.start()
        for cp in cps:
            cp.wait()

    return pl.pallas_call(
        body, name=name, in_specs=[ANY] * n, out_specs=[ANY] * n,
        out_shape=[jax.ShapeDtypeStruct((4, g.shape[1] // 2, g.shape[2]), g.dtype) for g in items],
        scratch_shapes=[pltpu.SemaphoreType.DMA((n,)), pltpu.SemaphoreType.DMA((n,))],
    )(*items)


def scatter_chips(items, *, name):
    n = len(items)

    def body(*refs):
        as_, rs = refs[:n], refs[n:2 * n]
        send_sems, recv_sems = refs[2 * n:]
        x, y, c = _place()
        cps = [_remote(a.at[2 * px + py], r.at[j], send_sems, recv_sems, 3 * k + j, (px, py, c))
               for k, (a, r) in enumerate(zip(as_, rs)) for j, (px, py) in enumerate(_other_chips(x, y))]
        for cp in cps:
            cp.start()
        for cp in cps:
            cp.wait()

    return pl.pallas_call(
        body, name=name, in_specs=[ANY] * n, out_specs=[ANY] * n,
        out_shape=[jax.ShapeDtypeStruct((3,) + a.shape[1:], a.dtype) for a in items],
        scratch_shapes=[pltpu.SemaphoreType.DMA((3 * n,)), pltpu.SemaphoreType.DMA((3 * n,))],
    )(*items)


def join_halves(items, *, name):
    n = len(items)

    def body(*refs):
        rs, outs = refs[:n], refs[n:2 * n]
        send_sems, recv_sems = refs[2 * n:]
        x, y, c = _place()
        sib = (x, y, 1 - c)
        cps = []
        for k, (r, o) in enumerate(zip(rs, outs)):
            mine = _half(c, r.shape[0])
            cps.append(_remote(r.at[mine], o.at[mine], send_sems, recv_sems, k, sib))
            cps[-1].start()
        for k, (r, o) in enumerate(zip(rs, outs)):
            theirs = _half(1 - c, r.shape[0])
            _remote(r.at[theirs], o.at[theirs], send_sems, recv_sems, k, sib).wait_recv()
        for cp in cps:
            cp.wait_send()

    return pl.pallas_call(
        body, name=name, in_specs=[ANY] * n, out_specs=[ANY] * n,
        out_shape=[jax.ShapeDtypeStruct(r.shape, r.dtype) for r in items],
        input_output_aliases={k: k for k in range(n)},
        scratch_shapes=[pltpu.SemaphoreType.DMA((n,)), pltpu.SemaphoreType.DMA((n,))],
    )(*items)


def gather_all(v, *, name, total):
    rows, W = v.shape

    def body(v_ref, o_ref, *scr):
        buf = scr[0] if total else o_ref
        send_sems, recv_sems = scr[-2], scr[-1]
        x, y, c = _place()
        me = 4 * x + 2 * y + c
        flips = [(k >> 2 & 1, k >> 1 & 1, k & 1) for k in range(1, 8)]
        peers = [((1 - x) if fx else x, (1 - y) if fy else y, (1 - c) if fc else c) for fx, fy, fc in flips]
        out = []
        for k, peer in enumerate(peers):
            cp = pltpu.make_async_remote_copy(src_ref=v_ref, dst_ref=buf.at[me], send_sem=send_sems.at[k],
                                              recv_sem=recv_sems.at[k], device_id=peer, device_id_type=MESH)
            cp.start()
            out.append(cp)
        buf[me] = v_ref[...]
        for k, (px, py, pc) in enumerate(peers):
            pltpu.make_async_remote_copy(src_ref=v_ref, dst_ref=buf.at[4 * px + 2 * py + pc], send_sem=send_sems.at[k],
                                         recv_sem=recv_sems.at[k], device_id=(px, py, pc), device_id_type=MESH).wait_recv()
        for cp in out:
            cp.wait_send()
        if total:
            acc = buf[0]
            for d in range(1, 8):
                acc = acc + buf[d]
            o_ref[...] = acc

    vm = pl.BlockSpec(memory_space=pltpu.VMEM)
    return pl.pallas_call(
        body, name=name, in_specs=[vm], out_specs=vm,
        out_shape=jax.ShapeDtypeStruct((rows, W) if total else (8, rows, W), F32),
        scratch_shapes=([pltpu.VMEM((8, rows, W), F32)] if total else [])
        + [pltpu.SemaphoreType.DMA((7,)), pltpu.SemaphoreType.DMA((7,))],
    )(v)


def pair_sum(g, r, *, name):
    _, R, W = g.shape
    Rh = R // 2
    tr = _pick(Rh, (512, 256, 128, 64, 32, 16))
    nb = Rh // tr

    def body(g_ref, r_ref, o_ref):
        o_ref[...] = (g_ref[...].astype(F32) + r_ref[...].astype(F32)).astype(o_ref.dtype)

    return pl.pallas_call(
        body, name=name, grid=(4, nb),
        in_specs=[pl.BlockSpec((1, tr, W), lambda s, i: (s, lax.axis_index("c") * nb + i, 0)),
                  pl.BlockSpec((1, tr, W), lambda s, i: (s, i, 0))],
        out_specs=pl.BlockSpec((1, tr, W), lambda s, i: (s, i, 0)),
        out_shape=jax.ShapeDtypeStruct((4, Rh, W), g.dtype), compiler_params=_params("parallel", "parallel"),
    )(g, r)


def chip_sum(g, r1, r2, *, name):
    _, R, W = g.shape
    Rh = R // 2
    tr = _pick(Rh, (512, 256, 128, 64, 32, 16))
    nb = Rh // tr

    def body(g_ref, r1_ref, a_ref, b_ref, c_ref, o_ref):
        acc = g_ref[0].astype(F32) + r1_ref[0].astype(F32)
        for ref in (a_ref, b_ref, c_ref):
            acc = acc + ref[0].astype(F32)
        o_ref[...] = acc

    shard = lambda: 2 * lax.axis_index("x") + lax.axis_index("y")
    half = lambda i: lax.axis_index("c") * nb + i
    return pl.pallas_call(
        body, name=name, grid=(nb,),
        in_specs=[pl.BlockSpec((1, tr, W), lambda i: (shard(), half(i), 0)),
                  pl.BlockSpec((1, tr, W), lambda i: (shard(), i, 0))]
        + [pl.BlockSpec((1, tr, W), lambda i, j=j: (j, i, 0)) for j in range(3)],
        out_specs=pl.BlockSpec((tr, W), lambda i: (half(i), 0)),
        out_shape=jax.ShapeDtypeStruct((R, W), F32), compiler_params=_params("parallel"),
    )(g, r1, r2, r2, r2)


BIG = (("w_in", False), ("pool_w", False), ("w_attn_br", False), ("w_pool_br", False), ("w_ssm_br", True),
       ("w_out", True), ("w_gate_up", False), ("w_down", True))


def _join(piece, axis):
    t = jnp.moveaxis(piece, 0, axis)
    shp = t.shape
    return t.reshape(shp[:axis] + (shp[axis] * shp[axis + 1],) + shp[axis + 2:])


def _seg_bounds():
    aw, kw, _, pw, di, cc, h, gd = C.in_widths
    o = [0, aw + 2 * kw]
    for wdt in (pw, di, cc, h, gd):
        o.append(o[-1] + wdt)
    return o


IN_PAD = 640


def _in_segments(blocks):
    w_in = jnp.concatenate([blocks[s] for s in range(4)], axis=1)
    o = _seg_bounds()
    segs = [w_in[:, o[i]:o[i + 1]] for i in range(6)]
    segs[4] = jnp.pad(segs[4], ((0, 0), (0, 128 - C.H)))
    width = sum(t.shape[1] for t in segs)
    pad = -width % IN_PAD
    return segs, jnp.concatenate(segs + [jnp.zeros((w_in.shape[0], pad), w_in.dtype)], axis=1)


def _layer_fwd(x, p, tag):
    nm = lambda s: f"{s}_{tag}"
    H = C.H
    h = rms_fwd(x, p["ln1_w"], name=nm("rms1"))
    wq, wu, wz, wx, wd, wg = p["in_segs"]
    qkv = matmul(h, wq, name=nm("mm_qkv"), out_dtype=MXU)
    u = matmul(h, wu, name=nm("mm_u"))
    z = matmul(h, wz, name=nm("mm_z"))
    xbc = matmul(h, wx, name=nm("mm_xbc"))
    dtp = matmul(h, wd, name=nm("mm_dt"))
    gl = matmul(h, wg, name=nm("mm_gate"), out_dtype=MXU)
    att = attn_fwd(qkv, p["attn_sink"], name=nm("attn_fwd"))
    pool, mixed = pool_fwd(u, p["pool_w"], p["pool_scale"], name=nm("pool_fwd"))
    xa = conv_fwd(xbc, p["conv_w"], p["conv_b"], name=nm("conv_fwd"))
    raw = dtp[:, :H]
    dt_t, acs_t = ssd_prep(raw.T, p["dt_bias"].T, p["a_log"].T, name=nm("ssd_prep"))
    dt, acs = dt_t.T, acs_t.T
    y, hprev = ssd_fwd(xa, dt, acs, acs_t, p["d_skip"], name=nm("ssd_fwd"))
    ssm = gnorm_fwd(y, z, p["ssm_norm_w"], name=nm("gnorm_fwd"))
    abr = matmul(att, name=nm("mm_abr"), out_dtype=MXU, **p["w_attn_br"])
    pbr = matmul(pool, name=nm("mm_pbr"), out_dtype=MXU, **p["w_pool_br"])
    sbr = matmul(ssm, name=nm("mm_sbr"), out_dtype=MXU, **p["w_ssm_br"])
    merged = merge_fwd(gl, abr, pbr, sbr, name=nm("merge_fwd"))
    xm = matmul(merged, add=x, name=nm("mm_out"), **p["w_out"])
    h2 = rms_fwd(xm, p["ln2_w"], name=nm("rms2"))
    gu = matmul(h2, name=nm("mm_gu"), out_dtype=MXU, **p["w_gate_up"])
    act = swiglu_fwd(gu, name=nm("swiglu_fwd"))
    xo = matmul(act, add=xm, name=nm("mm_down"), **p["w_down"])
    saved = dict(x=x, h=h, qkv=qkv, z=z, xbc=xbc, raw=raw, gl=gl, att=att, pool=pool, mixed=mixed, xa=xa, dt=dt, acs=acs,
                 acs_t=acs_t, y=y, hprev=hprev, ssm=ssm, abr=abr, pbr=pbr, sbr=sbr, merged=merged, xm=xm, h2=h2, gu=gu,
                 act=act)
    return xo, saved


def _layer_bwd(dxo, dxo_m, p, s, tag):
    nm = lambda t: f"{t}_{tag}"
    H = C.H
    g = {}

    def rows4(t):
        return t.reshape(4, t.shape[0] // 4, t.shape[1])

    g["w_down"] = rows4(matmul(s["act"], dxo_m, ta=True, out_dtype=MXU, name=nm("mmg_down")))
    dact = matmul(dxo_m, tb=True, out_dtype=MXU, name=nm("mmb_down"), **p["w_down"])
    dgu = swiglu_bwd(s["gu"], dact, name=nm("swiglu_bwd"))
    g["w_gate_up"] = matmul(s["h2"], dgu, ta=True, out_dtype=MXU, out_cols=True, name=nm("mmg_gu"))
    dh2 = matmul(dgu, tb=True, name=nm("mmb_gu"), **p["w_gate_up"])
    dxm, dxm_m, g["ln2_w"] = rms_bwd(s["xm"], p["ln2_w"], dh2, dxo, name=nm("rms2_bwd"))
    g["w_out"] = rows4(matmul(s["merged"], dxm_m, ta=True, out_dtype=MXU, name=nm("mmg_out")))
    dmerged = matmul(dxm_m, tb=True, out_dtype=MXU, name=nm("mmb_out"), **p["w_out"])
    dabr, dpbr, dsbr, dgl = merge_bwd(s["gl"], s["abr"], s["pbr"], s["sbr"], dmerged, name=nm("merge_bwd"))
    g["w_attn_br"] = matmul(s["att"], dabr, ta=True, out_dtype=MXU, out_cols=True, name=nm("mmg_abr"))
    g["w_pool_br"] = matmul(s["pool"], dpbr, ta=True, out_dtype=MXU, out_cols=True, name=nm("mmg_pbr"))
    g["w_ssm_br"] = rows4(matmul(s["ssm"], dsbr, ta=True, out_dtype=MXU, name=nm("mmg_sbr")))
    datt = matmul(dabr, tb=True, out_dtype=MXU, name=nm("mmb_abr"), **p["w_attn_br"])
    dpool = matmul(dpbr, tb=True, out_dtype=MXU, name=nm("mmb_pbr"), **p["w_pool_br"])
    dssm = matmul(dsbr, tb=True, out_dtype=MXU, name=nm("mmb_sbr"), **p["w_ssm_br"])
    dq, dk, dv, g["attn_sink"] = attn_bwd(s["qkv"], p["attn_sink"], datt, name=nm("attn_bwd"))
    du, dpw, g["pool_scale"] = pool_bwd(dpool, s["mixed"], p["pool_w"], p["pool_scale"], name=nm("pool_bwd"))
    pg = dpw.shape[1] // 4
    g["pool_w"] = jnp.moveaxis(dpw.reshape(4, 4, pg, dpw.shape[2]), 1, 0).reshape(4, 4 * pg, dpw.shape[2]).astype(MXU)
    dy, dz, g["ssm_norm_w"] = gnorm_bwd(s["y"], s["z"], p["ssm_norm_w"], dssm, name=nm("gnorm_bwd"))
    dxa, ddt, dacs, g["d_skip"] = ssd_bwd(s["xa"], s["dt"], s["acs"], s["acs_t"], p["d_skip"], s["hprev"], dy,
                                          name=nm("ssd_bwd"))
    draw, g["dt_bias"], g["a_log"] = ssd_post(ddt, dacs, s["dt"], s["raw"], p["dt_bias"], p["a_log"], name=nm("ssd_post"))
    dxbc, g["conv_w"], g["conv_b"] = conv_bwd(s["xbc"], p["conv_w"], p["conv_b"], dxa, name=nm("conv_bwd"))
    w_all = p["in_all"]
    ddtp = jnp.pad(draw, ((0, 0), (0, 128 - H))).astype(MXU)
    parts = [dq, dk, dv, du, dz, dxbc, ddtp, dgl]
    used = sum(t.shape[1] for t in parts)
    dproj = jnp.concatenate(parts + [jnp.zeros((dq.shape[0], w_all.shape[1] - used), MXU)], axis=1)
    dh = matmul(dproj, w_all, tb=True, name=nm("mmb_in"))
    g_all = matmul(s["h"], dproj, ta=True, out_dtype=MXU, name=nm("mmg_in"))
    o = _seg_bounds()
    dt0 = o[4]
    g_in = jnp.concatenate([g_all[:, :dt0 + H], g_all[:, dt0 + 128:dt0 + 128 + (o[6] - o[5])]], axis=1)
    nc = g_in.shape[1] // 4
    g["w_in"] = jnp.stack([g_in[:, k * nc:(k + 1) * nc] for k in range(4)])
    dx, dx_m, g["ln1_w"] = rms_bwd(s["x"], p["ln1_w"], dh, dxm, name=nm("rms1_bwd"))
    return dx, dx_m, g


SMALL = ("ln1_w", "attn_sink", "conv_w", "conv_b", "dt_bias", "a_log", "d_skip", "ssm_norm_w", "pool_scale", "ln2_w")
WEIGHTS = ("ln1_w", "w_in", "attn_sink", "conv_w", "conv_b", "dt_bias", "a_log", "d_skip", "ssm_norm_w", "pool_w",
           "pool_scale", "w_attn_br", "w_pool_br", "w_ssm_br", "w_out", "ln2_w", "w_gate_up", "w_down", "final_w")


def _step(x, loss_target, w, m, v):
    depth = C.depth
    xi, yi, ci = _place()
    shard = 2 * xi + yi

    names = [n for n, _ in BIG]
    nbig = len(names)

    cw_all = gather_all(_pack_small([w["conv_w"].reshape(1, -1)]), name="gather_conv_w", total=False)
    cw_shards = cw_all[0::2].reshape(4, -1)[:, :w["conv_w"].size].reshape((4,) + w["conv_w"].shape)
    conv_w = _join(cw_shards, 2)

    def shards(i):
        return [w[n][i].astype(MXU).reshape(-1, w[n].shape[-1]) for n in names]

    def params(i, blocks):
        full = dict(zip(names, blocks))
        p = {n: w[n][i][None] for n in SMALL if n != "conv_w"}
        p["conv_w"] = conv_w[i]
        p["in_segs"], p["in_all"] = _in_segments(full["w_in"])
        pw = full["pool_w"]
        p["pool_w"] = _join(pw.reshape(4, 4, pw.shape[1] // 4, pw.shape[2]), 1)
        for n, rw in BIG[2:]:
            t = full[n]
            p[n] = dict(b=t.reshape(4 * t.shape[1], t.shape[2])) if rw else dict(b=t, b_cols=True)
        return p

    xs = x[0]
    layers, saved = [], []
    blocks = gather_layer(shards(0), name="gather_l0")
    for i in range(depth):
        nxt = None
        if i + 1 < depth:
            src = shards(i + 1)
            nxt = exchange_start(src, [jax.ShapeDtypeStruct((4,) + a.shape, a.dtype) for a in src], gather_plan,
                                 7 * nbig, name=f"gather_l{i + 1}_start")
            xs = xs + nxt[4][0, 0]
        layers.append(params(i, blocks))
        xs, sv = _layer_fwd(xs, layers[i], f"l{i}")
        saved.append(sv)
        if nxt is not None:
            blocks = exchange_wait(nxt[0], nxt[1], nxt[2], nxt[3], gather_plan, xs, name=f"gather_l{i + 1}_wait")
    loss_part, dx, dx_m, g_final = final_loss(xs, w["final_w"][None], loss_target[0], name="final_loss")

    grads, sib, others = [None] * depth, [None] * depth, [None] * depth
    pend = None
    for i in reversed(range(depth)):
        if pend is not None:
            dx_m = dx_m + pend[4][0, 0].astype(dx_m.dtype)
        dx, dx_m, grads[i] = _layer_bwd(dx, dx_m, layers[i], saved[i], f"l{i}")
        if pend is not None:
            others[i + 1] = exchange_wait(pend[0], pend[1], pend[2], pend[3], scatter_plan, dx, name=f"scatter_l{i + 1}_wait")
        items = [grads[i][n] for n in names]
        sib[i] = swap_halves(items, name=f"swap_halves_l{i}")
        chip = [pair_sum(g, r, name=f"pair_sum_{n}_l{i}") for n, g, r in zip(names, items, sib[i])]
        if i > 0:
            pend = exchange_start(chip, [jax.ShapeDtypeStruct((3,) + a.shape[1:], a.dtype) for a in chip], scatter_plan,
                                  3 * nbig, name=f"scatter_l{i}_start")
        else:
            others[i] = scatter_chips(chip, name=f"scatter_l{i}")
    keys = [(i, n) for i in range(depth) for n in names]
    mine = [chip_sum(grads[i][n], sib[i][k], others[i][k], name=f"chip_sum_{n}_l{i}")
            for i in range(depth) for k, n in enumerate(names)]
    reduced = dict(zip(keys, join_halves(mine, name="join_halves")))
    gout = {}

    small = [jnp.stack([grads[i][n].reshape(-1) for i in range(depth)]).reshape(1, -1) for n in SMALL]
    small += [g_final.reshape(1, -1), loss_part.reshape(1, -1)]
    tot = gather_all(_pack_small(small), name="sum_small", total=True)
    parts = _unpack_small(tot, [t.shape[1] for t in small])
    for n, t in zip(SMALL, parts):
        if n == "conv_w":
            cols = w[n].shape[2]
            gout[n] = lax.dynamic_slice_in_dim(t.reshape(depth, CONV_K, -1), shard * cols, cols, axis=2)
        else:
            gout[n] = t.reshape(w[n].shape)
    gout["final_w"] = parts[-2].reshape(w["final_w"].shape)
    loss = parts[-1].reshape(())

    upd = {}
    for n in WEIGHTS:
        if n in gout:
            upd[n] = (gout[n].reshape(w[n].shape),) + adamw(w[n], gout[n].reshape(w[n].shape), m[n], v[n], name=f"adamw_{n}")
        else:
            upd[n] = adamw_layers(w[n], [reduced[(i, n)] for i in range(depth)], m[n], v[n], name=f"adamw_{n}")
    return (loss, dx[None], *[upd[n][0] for n in WEIGHTS], *[upd[n][1] for n in WEIGHTS],
            *[upd[n][2] for n in WEIGHTS], *[upd[n][3] for n in WEIGHTS])


def _pack_small(parts):
    flat = jnp.concatenate(parts, axis=1)
    rows = -(-flat.shape[1] // PACK_W)
    rows = -(-rows // 8) * 8
    return jnp.pad(flat, ((0, 0), (0, rows * PACK_W - flat.shape[1]))).reshape(rows, PACK_W)


def _unpack_small(buf, sizes):
    flat = buf.reshape(-1)
    out, off = [], 0
    for n in sizes:
        out.append(flat[off:off + n])
        off += n
    return out


def kernel(x, ln1_w, w_in, attn_sink, conv_w, conv_b, dt_bias, a_log, d_skip, ssm_norm_w, pool_w, pool_scale, w_attn_br, w_pool_br, w_ssm_br, w_out, ln2_w, w_gate_up, w_down, final_w, loss_target, m_ln1_w, m_w_in, m_attn_sink, m_conv_w, m_conv_b, m_dt_bias, m_a_log, m_d_skip, m_ssm_norm_w, m_pool_w, m_pool_scale, m_w_attn_br, m_w_pool_br, m_w_ssm_br, m_w_out, m_ln2_w, m_w_gate_up, m_w_down, m_final_w, v_ln1_w, v_w_in, v_attn_sink, v_conv_w, v_conv_b, v_dt_bias, v_a_log, v_d_skip, v_ssm_norm_w, v_pool_w, v_pool_scale, v_w_attn_br, v_w_pool_br, v_w_ssm_br, v_w_out, v_ln2_w, v_w_gate_up, v_w_down, v_final_w):
    w = dict(ln1_w=ln1_w, w_in=w_in, attn_sink=attn_sink, conv_w=conv_w, conv_b=conv_b, dt_bias=dt_bias, a_log=a_log,
             d_skip=d_skip, ssm_norm_w=ssm_norm_w, pool_w=pool_w, pool_scale=pool_scale, w_attn_br=w_attn_br,
             w_pool_br=w_pool_br, w_ssm_br=w_ssm_br, w_out=w_out, ln2_w=ln2_w, w_gate_up=w_gate_up, w_down=w_down,
             final_w=final_w)
    m = dict(ln1_w=m_ln1_w, w_in=m_w_in, attn_sink=m_attn_sink, conv_w=m_conv_w, conv_b=m_conv_b, dt_bias=m_dt_bias,
             a_log=m_a_log, d_skip=m_d_skip, ssm_norm_w=m_ssm_norm_w, pool_w=m_pool_w, pool_scale=m_pool_scale,
             w_attn_br=m_w_attn_br, w_pool_br=m_w_pool_br, w_ssm_br=m_w_ssm_br, w_out=m_w_out, ln2_w=m_ln2_w,
             w_gate_up=m_w_gate_up, w_down=m_w_down, final_w=m_final_w)
    v = dict(ln1_w=v_ln1_w, w_in=v_w_in, attn_sink=v_attn_sink, conv_w=v_conv_w, conv_b=v_conv_b, dt_bias=v_dt_bias,
             a_log=v_a_log, d_skip=v_d_skip, ssm_norm_w=v_ssm_norm_w, pool_w=v_pool_w, pool_scale=v_pool_scale,
             w_attn_br=v_w_attn_br, w_pool_br=v_w_pool_br, w_ssm_br=v_w_ssm_br, w_out=v_w_out, ln2_w=v_ln2_w,
             w_gate_up=v_w_gate_up, w_down=v_w_down, final_w=v_final_w)
    return _step(x, loss_target, w, m, v)
```

```python
import functools

import jax
import jax.numpy as jnp
from jax import lax
from jax.experimental import pallas as pl
from jax.experimental.pallas import tpu as pltpu

F32 = jnp.float32
MXU = jnp.bfloat16
VMEM_LIMIT = 56 * 1024 * 1024
EPS = 1e-6
NEG = -1e30

ADAM_LR, ADAM_B1, ADAM_B2, ADAM_EPS, ADAM_WD, ADAM_STEP = 0.001, 0.9, 0.999, 1e-08, 0.01, 10


class Cfg:
    def __init__(self, d_model=2048, seq=8192, depth=2, q_heads=16, kv_heads=4, head_dim=64,
                 ssm_head_dim=64, ssm_groups=4, d_state=128):
        self.D, self.T, self.depth = d_model, seq, depth
        self.hd, self.qh, self.kvh = head_dim, q_heads, kv_heads
        self.AW, self.KW = q_heads * head_dim, kv_heads * head_dim
        self.blk = 128
        self.PW = d_model // 2
        self.PG = self.PW // 4
        self.DI = d_model
        self.P = ssm_head_dim
        self.H = self.DI // self.P
        self.G = ssm_groups
        self.HG = self.H // self.G
        self.N = d_state
        self.CC = self.DI + 2 * self.G * self.N
        self.F = -(-8 * d_model // (3 * 256)) * 256
        self.in_widths = (self.AW, self.KW, self.KW, self.PW, self.DI, self.CC, self.H, 3 * d_model)
        self.in_cols = sum(self.in_widths)


C = Cfg()
POOL_WINDOWS = (2, 4, 8, 16)


def _pick(n, cands):
    for c in cands:
        if n % c == 0:
            return c
    return n


def _params(*sem):
    return pltpu.CompilerParams(dimension_semantics=sem, vmem_limit_bytes=VMEM_LIMIT)


def _sigmoid(x):
    return 1.0 / (1.0 + jnp.exp(-x))


def _dot(a, b, dims):
    return lax.dot_general(a.astype(MXU), b.astype(MXU), (dims, ((), ())), preferred_element_type=F32)


def _dot_nn(a, b):
    return _dot(a, b, ((1,), (0,)))


def _dot_nt(a, b):
    return _dot(a, b, ((1,), (1,)))


def _dot_tn(a, b):
    return _dot(a, b, ((0,), (0,)))


MM_VMEM = 40 * 1024 * 1024
LANES = 128


def _tile(n, cap):
    best = None
    for d in range(LANES, min(n, cap) + 1, LANES):
        if n % d == 0:
            best = d
    return n if best is None else best


def _divisors(n, cap):
    ds = [d for d in range(LANES, min(n, cap) + 1, LANES) if n % d == 0]
    return ds or [n]


HBM_RATE, MXU_RATE, ACC_RATE, STEP_COST = 3.0e12, 0.9e15, 1.2e13, 0.35e-6
MXU_WIDTH = 256


def _choose_tiles(M, N, K, n_unit, k_unit, sa, sb, so, sadd, ta):
    best = None
    for tm in sorted({_tile(M, 1024), _tile(M, 512)}):
        for tk in _divisors(k_unit, 4096):
            for tn in _divisors(n_unit, 2048):
                nk = K // tk
                vmem = 2 * (tm * tk * sa + tk * tn * sb + tm * tn * (so + sadd)) + (tm * tn * 4 if nk > 1 else 0)
                if vmem > MM_VMEM:
                    continue
                steps = (M // tm) * (N // tn) * nk
                a_bytes = M * K * sa * (N // tn if (nk > 1 or ta) else 1)
                b_bytes = K * N * sb * (M // tm)
                hbm = (a_bytes + b_bytes + M * N * (so + sadd)) / HBM_RATE
                fill = (tn / (-(-tn // MXU_WIDTH) * MXU_WIDTH)) * (tk / (-(-tk // MXU_WIDTH) * MXU_WIDTH))
                mxu = 2.0 * M * N * K / (MXU_RATE * fill)
                acc = steps * tm * tn * 8 / ACC_RATE if nk > 1 else 0.0
                cost = max(hbm, mxu) + acc + steps * STEP_COST
                if best is None or cost < best[0]:
                    best = (cost, tm, tn, tk)
    assert best is not None, (M, N, K)
    return best[1:]


def matmul(a, b, *, name, ta=False, tb=False, out_dtype=F32, add=None, layer=None, b_cols=False, out_cols=False):
    M, K = (a.shape[1], a.shape[0]) if ta else a.shape
    rows, cols = b.shape[-2], (4 if b_cols else 1) * b.shape[-1]
    N, bk = (rows, cols) if tb else (cols, rows)
    assert K == bk, (a.shape, b.shape)
    n_unit = N // 4 if (out_cols or (b_cols and not tb)) else N
    k_unit = K // 4 if (b_cols and tb) else K
    sa, sb, so = a.dtype.itemsize, b.dtype.itemsize, jnp.dtype(out_dtype).itemsize
    sadd = add.dtype.itemsize if add is not None else 0
    tm, tn, tk = _choose_tiles(M, N, K, n_unit, k_unit, sa, sb, so, sadd, ta)
    nk = K // tk
    a_spec = pl.BlockSpec((tk, tm), lambda i, j, k: (k, i)) if ta else pl.BlockSpec((tm, tk), lambda i, j, k: (i, k))
    if b_cols and tb:
        per = k_unit // tk
        b_spec = pl.BlockSpec((None, tn, tk), lambda i, j, k: (k // per, j, k % per))
    elif b_cols:
        per = n_unit // tn
        b_spec = pl.BlockSpec((None, tk, tn), lambda i, j, k: (j // per, k, j % per))
    elif layer is not None:
        b_spec = (pl.BlockSpec((None, tn, tk), lambda i, j, k: (layer, j, k)) if tb
                  else pl.BlockSpec((None, tk, tn), lambda i, j, k: (layer, k, j)))
    else:
        b_spec = pl.BlockSpec((tn, tk), lambda i, j, k: (j, k)) if tb else pl.BlockSpec((tk, tn), lambda i, j, k: (k, j))
    if out_cols:
        per = n_unit // tn
        o_spec = pl.BlockSpec((None, tm, tn), lambda i, j, k: (j // per, i, j % per))
        out_shape = jax.ShapeDtypeStruct((4, M, N // 4), out_dtype)
    else:
        o_spec = pl.BlockSpec((tm, tn), lambda i, j, k: (i, j))
        out_shape = jax.ShapeDtypeStruct((M, N), out_dtype)
    dims = ((0 if ta else 1,), (1 if tb else 0,))
    has_add = add is not None

    def body(*refs):
        a_ref, b_ref = refs[0], refs[1]
        add_ref = refs[2] if has_add else None
        o_ref = refs[3] if has_add else refs[2]
        part = _dot(a_ref[...], b_ref[...], dims)

        def finish(total):
            if has_add:
                total = total + add_ref[...].astype(F32)
            o_ref[...] = total.astype(o_ref.dtype)

        if nk == 1:
            finish(part)
        else:
            acc_ref = refs[-1]
            k = pl.program_id(2)

            @pl.when(k == 0)
            def _():
                acc_ref[...] = part

            @pl.when(k > 0)
            def _():
                acc_ref[...] += part

            @pl.when(k == nk - 1)
            def _():
                finish(acc_ref[...])

    in_specs = [a_spec, b_spec] + ([o_spec] if has_add else [])
    args = (a, b) + ((add,) if has_add else ())
    return pl.pallas_call(
        body, name=name, grid=(M // tm, N // tn, nk), in_specs=in_specs, out_specs=o_spec, out_shape=out_shape,
        scratch_shapes=[pltpu.VMEM((tm, tn), F32)] if nk > 1 else [],
        compiler_params=_params("parallel", "parallel", "arbitrary"),
    )(*args)


def _row_tile(t):
    return _pick(t, (256, 128, 64, 32, 16, 8))


def rms_fwd(x, w, *, name):
    T, D = x.shape
    tr = _row_tile(T)

    def body(x_ref, w_ref, o_ref):
        xv = x_ref[...]
        r = lax.rsqrt(jnp.mean(xv * xv, axis=-1, keepdims=True) + EPS)
        o_ref[...] = (xv * r * w_ref[...]).astype(o_ref.dtype)

    row = pl.BlockSpec((tr, D), lambda i: (i, 0))
    return pl.pallas_call(
        body, name=name, grid=(T // tr,), in_specs=[row, pl.BlockSpec((1, D), lambda i: (0, 0))], out_specs=row,
        out_shape=jax.ShapeDtypeStruct((T, D), MXU), compiler_params=_params("parallel"),
    )(x, w)


def rms_bwd(x, w, dh, dres, *, name):
    T, D = x.shape
    tr = _row_tile(T)

    def body(x_ref, w_ref, dh_ref, dres_ref, dx_ref, dxm_ref, dw_ref):
        xv = x_ref[...]
        dhv = dh_ref[...].astype(F32)
        r = lax.rsqrt(jnp.mean(xv * xv, axis=-1, keepdims=True) + EPS)
        g = dhv * w_ref[...]
        dot = jnp.mean(g * xv, axis=-1, keepdims=True)
        dx = dres_ref[...] + r * g - xv * (r * r * r * dot)
        dx_ref[...] = dx
        dxm_ref[...] = dx.astype(dxm_ref.dtype)
        part = jnp.sum(dhv * xv * r, axis=0, keepdims=True)

        @pl.when(pl.program_id(0) == 0)
        def _():
            dw_ref[...] = part

        @pl.when(pl.program_id(0) > 0)
        def _():
            dw_ref[...] += part

    row = pl.BlockSpec((tr, D), lambda i: (i, 0))
    vec = pl.BlockSpec((1, D), lambda i: (0, 0))
    return pl.pallas_call(
        body, name=name, grid=(T // tr,), in_specs=[row, vec, row, row], out_specs=[row, row, vec],
        out_shape=[jax.ShapeDtypeStruct((T, D), F32), jax.ShapeDtypeStruct((T, D), MXU),
                   jax.ShapeDtypeStruct((1, D), F32)],
        compiler_params=_params("arbitrary"),
    )(x, w, dh, dres)


def final_loss(x, w, target, *, name):
    T, D = x.shape
    tr = _row_tile(T)

    def body(x_ref, w_ref, t_ref, loss_ref, dx_ref, dxm_ref, dw_ref):
        xv = x_ref[...]
        wv = w_ref[...]
        r = lax.rsqrt(jnp.mean(xv * xv, axis=-1, keepdims=True) + EPS)
        err = xv * r * wv - t_ref[...]
        lpart = 0.5 * jnp.sum(jnp.mean(err * err, axis=-1, keepdims=True), axis=0, keepdims=True)
        dy = err * (1.0 / D)
        g = dy * wv
        dot = jnp.mean(g * xv, axis=-1, keepdims=True)
        dx = r * g - xv * (r * r * r * dot)
        dx_ref[...] = dx
        dxm_ref[...] = dx.astype(dxm_ref.dtype)
        part = jnp.sum(dy * xv * r, axis=0, keepdims=True)

        @pl.when(pl.program_id(0) == 0)
        def _():
            dw_ref[...] = part
            loss_ref[...] = lpart

        @pl.when(pl.program_id(0) > 0)
        def _():
            dw_ref[...] += part
            loss_ref[...] += lpart

    row = pl.BlockSpec((tr, D), lambda i: (i, 0))
    vec = pl.BlockSpec((1, D), lambda i: (0, 0))
    one = pl.BlockSpec((1, 1), lambda i: (0, 0))
    return pl.pallas_call(
        body, name=name, grid=(T // tr,), in_specs=[row, vec, row], out_specs=[one, row, row, vec],
        out_shape=[jax.ShapeDtypeStruct((1, 1), F32), jax.ShapeDtypeStruct((T, D), F32),
                   jax.ShapeDtypeStruct((T, D), MXU), jax.ShapeDtypeStruct((1, D), F32)],
        compiler_params=_params("arbitrary"),
    )(x, w, target)


def swiglu_fwd(gu, *, name):
    T, F2 = gu.shape
    F = F2 // 2
    tr = _row_tile(T)

    def body(g_ref, u_ref, o_ref):
        g = g_ref[...].astype(F32)
        o_ref[...] = (g * _sigmoid(g) * u_ref[...].astype(F32)).astype(o_ref.dtype)

    return pl.pallas_call(
        body, name=name, grid=(T // tr,),
        in_specs=[pl.BlockSpec((tr, F), lambda i: (i, 0)), pl.BlockSpec((tr, F), lambda i: (i, 1))],
        out_specs=pl.BlockSpec((tr, F), lambda i: (i, 0)),
        out_shape=jax.ShapeDtypeStruct((T, F), MXU), compiler_params=_params("parallel"),
    )(gu, gu)


def swiglu_bwd(gu, dact, *, name):
    T, F2 = gu.shape
    F = F2 // 2
    tr = _row_tile(T)

    def body(g_ref, u_ref, d_ref, o_ref):
        g = g_ref[...].astype(F32)
        d = d_ref[...].astype(F32)
        s = _sigmoid(g)
        o_ref[:, :F] = (d * u_ref[...].astype(F32) * s * (1.0 + g * (1.0 - s))).astype(o_ref.dtype)
        o_ref[:, F:] = (d * g * s).astype(o_ref.dtype)

    lo = pl.BlockSpec((tr, F), lambda i: (i, 0))
    hi = pl.BlockSpec((tr, F), lambda i: (i, 1))
    return pl.pallas_call(
        body, name=name, grid=(T // tr,), in_specs=[lo, hi, lo], out_specs=pl.BlockSpec((tr, F2), lambda i: (i, 0)),
        out_shape=jax.ShapeDtypeStruct((T, F2), MXU), compiler_params=_params("parallel"),
    )(gu, gu, dact)


def merge_fwd(gl, abr, pbr, sbr, *, name):
    T, D = abr.shape
    tr = _row_tile(T)

    def body(g0, g1, g2, a_ref, p_ref, s_ref, o_ref):
        m = sum(_sigmoid(g[...].astype(F32)) * b[...].astype(F32) for g, b in ((g0, a_ref), (g1, p_ref), (g2, s_ref)))
        o_ref[...] = m.astype(o_ref.dtype)

    row = pl.BlockSpec((tr, D), lambda i: (i, 0))
    gs = [pl.BlockSpec((tr, D), lambda i, j=j: (i, j)) for j in range(3)]
    return pl.pallas_call(
        body, name=name, grid=(T // tr,), in_specs=gs + [row, row, row], out_specs=row,
        out_shape=jax.ShapeDtypeStruct((T, D), MXU), compiler_params=_params("parallel"),
    )(gl, gl, gl, abr, pbr, sbr)


def merge_bwd(gl, abr, pbr, sbr, dm, *, name):
    T, D = abr.shape
    tr = _row_tile(T)

    def body(g0, g1, g2, a_ref, p_ref, s_ref, dm_ref, da_ref, dp_ref, ds_ref, dg_ref):
        d = dm_ref[...].astype(F32)
        for j, (g_ref, b_ref, db_ref) in enumerate(((g0, a_ref, da_ref), (g1, p_ref, dp_ref), (g2, s_ref, ds_ref))):
            s = _sigmoid(g_ref[...].astype(F32))
            db_ref[...] = (d * s).astype(db_ref.dtype)
            dg_ref[:, j * D:(j + 1) * D] = (d * b_ref[...].astype(F32) * s * (1.0 - s)).astype(dg_ref.dtype)

    row = pl.BlockSpec((tr, D), lambda i: (i, 0))
    gs = [pl.BlockSpec((tr, D), lambda i, j=j: (i, j)) for j in range(3)]
    return pl.pallas_call(
        body, name=name, grid=(T // tr,), in_specs=gs + [row] * 4,
        out_specs=[row] * 3 + [pl.BlockSpec((tr, 3 * D), lambda i: (i, 0))],
        out_shape=[jax.ShapeDtypeStruct((T, D), MXU)] * 3 + [jax.ShapeDtypeStruct((T, 3 * D), MXU)],
        compiler_params=_params("parallel"),
    )(gl, gl, gl, abr, pbr, sbr, dm)


def gnorm_fwd(y, z, w, *, name):
    T, DI = y.shape
    gw = DI // C.G
    tr = _row_tile(T)

    def body(y_ref, z_ref, w_ref, o_ref):
        for g in range(C.G):
            sl = slice(g * gw, (g + 1) * gw)
            zz = z_ref[:, sl]
            v = y_ref[:, sl] * (zz * _sigmoid(zz))
            r = lax.rsqrt(jnp.mean(v * v, axis=-1, keepdims=True) + EPS)
            o_ref[:, sl] = (v * r * w_ref[:, sl]).astype(o_ref.dtype)

    row = pl.BlockSpec((tr, DI), lambda i: (i, 0))
    return pl.pallas_call(
        body, name=name, grid=(T // tr,), in_specs=[row, row, pl.BlockSpec((1, DI), lambda i: (0, 0))],
        out_specs=row, out_shape=jax.ShapeDtypeStruct((T, DI), MXU), compiler_params=_params("parallel"),
    )(y, z, w)


def gnorm_bwd(y, z, w, do, *, name):
    T, DI = y.shape
    gw = DI // C.G
    tr = _row_tile(T)

    def body(y_ref, z_ref, w_ref, do_ref, dy_ref, dz_ref, dw_ref):
        first = pl.program_id(0) == 0
        for g in range(C.G):
            sl = slice(g * gw, (g + 1) * gw)
            zz = z_ref[:, sl]
            yy = y_ref[:, sl]
            s = _sigmoid(zz)
            sz = zz * s
            v = yy * sz
            r = lax.rsqrt(jnp.mean(v * v, axis=-1, keepdims=True) + EPS)
            dov = do_ref[:, sl].astype(F32)
            gg = dov * w_ref[:, sl]
            dot = jnp.mean(gg * v, axis=-1, keepdims=True)
            dv = r * gg - v * (r * r * r * dot)
            dy_ref[:, sl] = dv * sz
            dz_ref[:, sl] = (dv * yy * s * (1.0 + zz * (1.0 - s))).astype(dz_ref.dtype)
            part = jnp.sum(dov * v * r, axis=0, keepdims=True)

            @pl.when(first)
            def _():
                dw_ref[:, sl] = part

            @pl.when(jnp.logical_not(first))
            def _():
                dw_ref[:, sl] += part

    row = pl.BlockSpec((tr, DI), lambda i: (i, 0))
    vec = pl.BlockSpec((1, DI), lambda i: (0, 0))
    return pl.pallas_call(
        body, name=name, grid=(T // tr,), in_specs=[row, row, vec, row], out_specs=[row, row, vec],
        out_shape=[jax.ShapeDtypeStruct((T, DI), F32), jax.ShapeDtypeStruct((T, DI), MXU),
                   jax.ShapeDtypeStruct((1, DI), F32)],
        compiler_params=_params("arbitrary"),
    )(y, z, w, do)


def adamw(w, g, m, v, *, name):
    shape = w.shape
    cols = shape[-1]
    rows = w.size // cols
    w2, g2, m2, v2 = (t.reshape(rows, cols) for t in (w, g, m, v))
    tr = rows if rows * cols * 4 <= (2 << 20) else _pick(rows, (512, 256, 128, 64, 32, 16, 8))
    while tr * cols * 4 > (2 << 20) and tr % 16 == 0:
        tr //= 2
    c1 = 1.0 - ADAM_B1 ** ADAM_STEP
    c2 = 1.0 - ADAM_B2 ** ADAM_STEP

    def body(w_ref, g_ref, m_ref, v_ref, d_ref, nm_ref, nv_ref):
        gv = g_ref[...]
        nm = ADAM_B1 * m_ref[...] + (1.0 - ADAM_B1) * gv
        nv = ADAM_B2 * v_ref[...] + (1.0 - ADAM_B2) * (gv * gv)
        d_ref[...] = -ADAM_LR * ((nm / c1) / (jnp.sqrt(nv / c2) + ADAM_EPS) + ADAM_WD * w_ref[...])
        nm_ref[...] = nm
        nv_ref[...] = nv

    row = pl.BlockSpec((tr, cols), lambda i: (i, 0))
    outs = pl.pallas_call(
        body, name=name, grid=(rows // tr,), in_specs=[row] * 4, out_specs=[row] * 3,
        out_shape=[jax.ShapeDtypeStruct((rows, cols), F32)] * 3, compiler_params=_params("parallel"),
    )(w2, g2, m2, v2)
    return tuple(o.reshape(shape) for o in outs)


def adamw_layers(w, g_layers, m, v, *, name):
    shape = w.shape
    L = shape[0]
    rows, cols = g_layers[0].shape
    w3, m3, v3 = (t.reshape(L, rows, cols) for t in (w, m, v))
    tr = _pick(rows, (512, 256, 128, 64, 32, 16, 8))
    while tr * cols * 4 > (2 << 20) and tr % 16 == 0:
        tr //= 2
    c1 = 1.0 - ADAM_B1 ** ADAM_STEP
    c2 = 1.0 - ADAM_B2 ** ADAM_STEP

    def body(*refs):
        w_ref, m_ref, v_ref = refs[0], refs[1], refs[2]
        g_refs = refs[3:3 + L]
        go_ref, d_ref, nm_ref, nv_ref = refs[3 + L:]
        layer = pl.program_id(0)
        gv = g_refs[0][...]
        for k in range(1, L):
            gv = jnp.where(layer == k, g_refs[k][...], gv)
        nm = ADAM_B1 * m_ref[...] + (1.0 - ADAM_B1) * gv
        nv = ADAM_B2 * v_ref[...] + (1.0 - ADAM_B2) * (gv * gv)
        d_ref[...] = -ADAM_LR * ((nm / c1) / (jnp.sqrt(nv / c2) + ADAM_EPS) + ADAM_WD * w_ref[...])
        go_ref[...] = gv
        nm_ref[...] = nm
        nv_ref[...] = nv

    blk = pl.BlockSpec((None, tr, cols), lambda l, i: (l, i, 0))
    gblks = [pl.BlockSpec((tr, cols), lambda l, i, k=k: (jnp.where(l == k, i, 0), 0)) for k in range(L)]
    outs = pl.pallas_call(
        body, name=name, grid=(L, rows // tr), in_specs=[blk] * 3 + gblks, out_specs=[blk] * 4,
        out_shape=[jax.ShapeDtypeStruct((L, rows, cols), F32)] * 4, compiler_params=_params("parallel", "parallel"),
    )(w3, m3, v3, *g_layers)
    return tuple(o.reshape(shape) for o in outs)


def _attn_masks(i, heads):
    B = C.blk
    row = lax.broadcasted_iota(jnp.int32, (heads * B, 2 * B), 0) & (B - 1)
    col = lax.broadcasted_iota(jnp.int32, (heads * B, 2 * B), 1)
    diff = row + B - col
    return (diff >= 0) & (diff < B) & ((col >= B) | (i > 0))


def _stack_heads(ref, kvh, ppk, lo):
    parts = []
    for pr in range(ppk):
        pair = kvh * ppk + pr
        qp = ref[:, pair * 128:(pair + 1) * 128].astype(F32)
        parts += [jnp.where(lo, qp, 0.0), jnp.where(lo, 0.0, qp)]
    return jnp.concatenate(parts, axis=0)


def _unstack_heads(ref, val, kvh, ppk, lo):
    B = C.blk
    for pr in range(ppk):
        pair = kvh * ppk + pr
        ref[:, pair * 128:(pair + 1) * 128] = jnp.where(lo, val[2 * pr * B:(2 * pr + 1) * B],
                                                        val[(2 * pr + 1) * B:(2 * pr + 2) * B]).astype(ref.dtype)


def _sink_scores(sink_ref, kvh, ppk):
    B = C.blk
    h0 = kvh * 2 * ppk
    return jnp.concatenate([jnp.full((B, 2 * B), sink_ref[0, h], F32) for h in range(h0, h0 + 2 * ppk)], axis=0)


def _kv2(prev_ref, cur_ref, m, lo):
    sl = slice(m * 128, (m + 1) * 128)
    slab = jnp.concatenate([prev_ref[:, sl], cur_ref[:, sl]], axis=0).astype(F32)
    slab = jnp.where(lax.broadcasted_iota(jnp.int32, (slab.shape[0], 1), 0) == 0, 0.0, slab)
    rolled = pltpu.roll(slab, 64, axis=1)
    return jnp.where(lo, slab, rolled), jnp.where(lo, rolled, slab)


def _attn_weights(q4, k2, mask, sink_scores):
    s = _dot_nt(q4, k2) * (C.hd ** -0.5)
    col0 = lax.broadcasted_iota(jnp.int32, (1, s.shape[1]), 1) == 0
    s = jnp.where(col0, sink_scores, jnp.where(mask, s, NEG))
    return jnp.exp(s - jnp.max(s, axis=-1, keepdims=True))


def _row_sums(p):
    return lax.dot_general(p, jnp.ones((p.shape[1], 128), F32), (((1,), (0,)), ((), ())),
                           precision=lax.Precision.HIGHEST, preferred_element_type=F32)


def attn_fwd(qkv, sink, *, name):
    T = qkv.shape[0]
    B, AW, KW = C.blk, C.AW, C.KW
    assert C.hd == 64 and KW % 128 == 0 and (C.qh // C.kvh) % 2 == 0
    nb = T // B
    kb = AW // KW
    ppk = C.qh // C.kvh // 2

    def body(q_ref, kc_ref, kp_ref, vc_ref, vp_ref, sink_ref, o_ref):
        i = pl.program_id(0)
        lo = lax.broadcasted_iota(jnp.int32, (1, 128), 1) < 64
        mask = _attn_masks(i, 2 * ppk)
        for m in range(KW // 128):
            k2s = _kv2(kp_ref, kc_ref, m, lo)
            v2s = _kv2(vp_ref, vc_ref, m, lo)
            for par in range(2):
                kvh = 2 * m + par
                q4 = _stack_heads(q_ref, kvh, ppk, lo)
                p = _attn_weights(q4, k2s[par], mask, _sink_scores(sink_ref, kvh, ppk))
                _unstack_heads(o_ref, _dot_nn(p, v2s[par]) / _row_sums(p), kvh, ppk, lo)

    prev = lambda i: jnp.maximum(i - 1, 0)
    return pl.pallas_call(
        body, name=name, grid=(nb,),
        in_specs=[pl.BlockSpec((B, AW), lambda i: (i, 0)),
                  pl.BlockSpec((B, KW), lambda i: (i, kb)), pl.BlockSpec((B, KW), lambda i: (prev(i), kb)),
                  pl.BlockSpec((B, KW), lambda i: (i, kb + 1)), pl.BlockSpec((B, KW), lambda i: (prev(i), kb + 1)),
                  pl.BlockSpec(memory_space=pltpu.SMEM)],
        out_specs=pl.BlockSpec((B, AW), lambda i: (i, 0)),
        out_shape=jax.ShapeDtypeStruct((T, AW), MXU), compiler_params=_params("parallel"),
    )(qkv, qkv, qkv, qkv, qkv, sink)


def attn_bwd(qkv, sink, dout, *, name):
    T = qkv.shape[0]
    B, AW, KW = C.blk, C.AW, C.KW
    nb = T // B
    kb = AW // KW
    ppk = C.qh // C.kvh // 2
    scale = C.hd ** -0.5

    def body(q_ref, kc_ref, kp_ref, vc_ref, vp_ref, sink_ref, do_ref, dq_ref, dk_ref, dv_ref, ds_ref, ck_ref, cv_ref):
        i = pl.program_id(0)

        @pl.when(i == 0)
        def _():
            ck_ref[...] = jnp.zeros_like(ck_ref)
            cv_ref[...] = jnp.zeros_like(cv_ref)
            ds_ref[...] = jnp.zeros_like(ds_ref)

        @pl.when(i < nb)
        def _():
            lane = lax.broadcasted_iota(jnp.int32, (1, 128), 1)
            lo = lane < 64
            hlane = lax.broadcasted_iota(jnp.int32, (1, C.qh), 1)
            mask = _attn_masks(i, 2 * ppk)
            dsink = jnp.zeros((1, C.qh), F32)
            for m in range(KW // 128):
                ksl = slice(m * 128, (m + 1) * 128)
                k2s = _kv2(kp_ref, kc_ref, m, lo)
                v2s = _kv2(vp_ref, vc_ref, m, lo)
                folded = []
                for par in range(2):
                    kvh = 2 * m + par
                    q4 = _stack_heads(q_ref, kvh, ppk, lo)
                    do4 = _stack_heads(do_ref, kvh, ppk, lo)
                    p = _attn_weights(q4, k2s[par], mask, _sink_scores(sink_ref, kvh, ppk))
                    inv = 1.0 / _row_sums(p)
                    p = p * jnp.concatenate([inv, inv], axis=1)
                    dp = _dot_nt(do4, v2s[par])
                    delta = _row_sums(p * dp)
                    dsc = p * (dp - jnp.concatenate([delta, delta], axis=1))
                    _unstack_heads(dq_ref, _dot_nn(dsc, k2s[par]) * scale, kvh, ppk, lo)
                    dk2 = _dot_tn(dsc, q4) * scale
                    dv2 = _dot_tn(p, do4)
                    for k in range(2 * ppk):
                        dsh = jnp.sum(dsc[k * B:(k + 1) * B, :128], axis=0, keepdims=True)[:, :1]
                        dsink = dsink + jnp.where(hlane == kvh * 2 * ppk + k, dsh, 0.0)
                    folded.append((dk2 + pltpu.roll(dk2, 64, axis=1), dv2 + pltpu.roll(dv2, 64, axis=1)))
                row0 = lax.broadcasted_iota(jnp.int32, (2 * B, 1), 0) == 0
                dks = jnp.where(row0, 0.0, jnp.where(lo, folded[0][0], folded[1][0]))
                dvs = jnp.where(row0, 0.0, jnp.where(lo, folded[0][1], folded[1][1]))
                dk_ref[:, ksl] = (ck_ref[:, ksl] + dks[:B]).astype(dk_ref.dtype)
                dv_ref[:, ksl] = (cv_ref[:, ksl] + dvs[:B]).astype(dv_ref.dtype)
                ck_ref[:, ksl] = dks[B:]
                cv_ref[:, ksl] = dvs[B:]
            ds_ref[...] += dsink

        @pl.when(i == nb)
        def _():
            dk_ref[...] = ck_ref[...].astype(dk_ref.dtype)
            dv_ref[...] = cv_ref[...].astype(dv_ref.dtype)

    cur = lambda i: jnp.minimum(i, nb - 1)
    prev = lambda i: jnp.maximum(jnp.minimum(i, nb - 1) - 1, 0)
    out = lambda i: jnp.maximum(i - 1, 0)
    return pl.pallas_call(
        body, name=name, grid=(nb + 1,),
        in_specs=[pl.BlockSpec((B, AW), lambda i: (cur(i), 0)),
                  pl.BlockSpec((B, KW), lambda i: (cur(i), kb)), pl.BlockSpec((B, KW), lambda i: (prev(i), kb)),
                  pl.BlockSpec((B, KW), lambda i: (cur(i), kb + 1)), pl.BlockSpec((B, KW), lambda i: (prev(i), kb + 1)),
                  pl.BlockSpec(memory_space=pltpu.SMEM),
                  pl.BlockSpec((B, AW), lambda i: (cur(i), 0))],
        out_specs=[pl.BlockSpec((B, AW), lambda i: (cur(i), 0)),
                   pl.BlockSpec((B, KW), lambda i: (out(i), 0)), pl.BlockSpec((B, KW), lambda i: (out(i), 0)),
                   pl.BlockSpec((1, C.qh), lambda i: (0, 0))],
        out_shape=[jax.ShapeDtypeStruct((T, AW), MXU), jax.ShapeDtypeStruct((T, KW), MXU),
                   jax.ShapeDtypeStruct((T, KW), MXU), jax.ShapeDtypeStruct((1, C.qh), F32)],
        scratch_shapes=[pltpu.VMEM((B, KW), F32), pltpu.VMEM((B, KW), F32)],
        compiler_params=_params("arbitrary"),
    )(qkv, qkv, qkv, qkv, qkv, sink, dout)


POOL_HALO = 16


def _window_sum(e, w, n, forward):
    s, k = e, 1
    while k < w:
        s = s + pltpu.roll(s, (n - k) if forward else k, axis=0)
        k *= 2
    return s


def pool_fwd(u, pw, ps, *, name):
    T, PW = u.shape
    PG = C.PG
    tT = _pick(T, (256, 128))
    hb = tT // POOL_HALO

    def body(u_ref, uh_ref, pw_ref, ps_ref, o_ref, mx_ref):
        i = pl.program_id(0)
        halo = jnp.where(i > 0, uh_ref[...], 0.0)
        ext = jnp.concatenate([halo, u_ref[...]], axis=0)
        t = i * tT + lax.broadcasted_iota(jnp.int32, (tT, 1), 0)
        for g, w in enumerate(POOL_WINDOWS):
            sl = slice(g * PG, (g + 1) * PG)
            s = _window_sum(ext[:, sl], w, tT + POOL_HALO, False)[POOL_HALO:]
            cnt = jnp.minimum(t + 1, w).astype(F32)
            mixed = (s / cnt - u_ref[:, sl]).astype(MXU)
            mx_ref[:, sl] = mixed.astype(mx_ref.dtype)
            o_ref[:, sl] = (_dot_nn(mixed, pw_ref[g]) * ps_ref[:, sl]).astype(o_ref.dtype)

    row = pl.BlockSpec((tT, PW), lambda i: (i, 0))
    return pl.pallas_call(
        body, name=name, grid=(T // tT,),
        in_specs=[row, pl.BlockSpec((POOL_HALO, PW), lambda i: (jnp.maximum(i * hb - 1, 0), 0)),
                  pl.BlockSpec((4, PG, PG), lambda i: (0, 0, 0)), pl.BlockSpec((1, PW), lambda i: (0, 0))],
        out_specs=[row, row], out_shape=[jax.ShapeDtypeStruct((T, PW), MXU)] * 2,
        compiler_params=_params("parallel"),
    )(u, u, pw, ps)


def pool_bwd(dpool, mixed, pw, ps, *, name):
    T, PW = dpool.shape
    PG = C.PG
    tT = _pick(T, (256, 128))
    hb = tT // POOL_HALO
    n = T // tT
    rows = tT + POOL_HALO

    def body(dp_ref, dph_ref, mx_ref, pw_ref, ps_ref, du_ref, dpw_ref, dps_ref):
        i = pl.program_id(0)

        @pl.when(i == 0)
        def _():
            dpw_ref[...] = jnp.zeros_like(dpw_ref)
            dps_ref[...] = jnp.zeros_like(dps_ref)

        halo = jnp.where(i < n - 1, dph_ref[...].astype(F32), 0.0)
        dext = jnp.concatenate([dp_ref[...].astype(F32), halo], axis=0)
        t = i * tT + lax.broadcasted_iota(jnp.int32, (rows, 1), 0)
        for g, w in enumerate(POOL_WINDOWS):
            sl = slice(g * PG, (g + 1) * PG)
            dyg = dext[:, sl] * ps_ref[:, sl]
            dmix = _dot_nt(dyg, pw_ref[g])
            cnt = jnp.minimum(t + 1, w).astype(F32)
            s = _window_sum(dmix / cnt, w, rows, True)
            du_ref[:, sl] = (s[:tT] - dmix[:tT]).astype(du_ref.dtype)
            mb = mx_ref[:, sl]
            dpw_ref[g] += _dot_tn(mb, dyg[:tT])
            dps_ref[:, sl] += jnp.sum(dext[:tT, sl] * _dot_nn(mb, pw_ref[g]), axis=0, keepdims=True)

    row = pl.BlockSpec((tT, PW), lambda i: (i, 0))
    return pl.pallas_call(
        body, name=name, grid=(n,),
        in_specs=[row, pl.BlockSpec((POOL_HALO, PW), lambda i: (jnp.minimum((i + 1) * hb, T // POOL_HALO - 1), 0)), row,
                  pl.BlockSpec((4, PG, PG), lambda i: (0, 0, 0)), pl.BlockSpec((1, PW), lambda i: (0, 0))],
        out_specs=[row, pl.BlockSpec((4, PG, PG), lambda i: (0, 0, 0)), pl.BlockSpec((1, PW), lambda i: (0, 0))],
        out_shape=[jax.ShapeDtypeStruct((T, PW), MXU), jax.ShapeDtypeStruct((4, PG, PG), F32),
                   jax.ShapeDtypeStruct((1, PW), F32)],
        compiler_params=_params("arbitrary"),
    )(dpool, dpool, mixed, pw, ps)


CONV_HALO = 8
CONV_K = 4
CONV_TAP_ROWS = 16


def _conv(x, cw_ref, cb_ref, sl):
    acc = cb_ref[:, sl] + cw_ref[CONV_K - 1:CONV_K, sl] * x
    for k in range(CONV_K - 1):
        acc = acc + cw_ref[k:k + 1, sl] * pltpu.roll(x, CONV_K - 1 - k, axis=0)
    return acc


def conv_fwd(xbc, cw, cb, *, name):
    T, CC = xbc.shape
    tT = _pick(T, (256, 128))
    hb = tT // CONV_HALO
    cs = _pick(CC, (512, 256, 128))

    def body(x_ref, xh_ref, cw_ref, cb_ref, o_ref):
        i = pl.program_id(0)
        for c0 in range(0, CC, cs):
            sl = slice(c0, c0 + cs)
            ext = jnp.concatenate([jnp.where(i > 0, xh_ref[:, sl], 0.0), x_ref[:, sl]], axis=0)
            xc = _conv(ext, cw_ref, cb_ref, sl)[CONV_HALO:]
            o_ref[:, sl] = xc * _sigmoid(xc)

    row = pl.BlockSpec((tT, CC), lambda i: (i, 0))
    return pl.pallas_call(
        body, name=name, grid=(T // tT,),
        in_specs=[row, pl.BlockSpec((CONV_HALO, CC), lambda i: (jnp.maximum(i * hb - 1, 0), 0)),
                  pl.BlockSpec((CONV_K, CC), lambda i: (0, 0)), pl.BlockSpec((1, CC), lambda i: (0, 0))],
        out_specs=row, out_shape=jax.ShapeDtypeStruct((T, CC), F32), compiler_params=_params("parallel"),
    )(xbc, xbc, cw, cb)


def conv_bwd(xbc, cw, cb, dxa, *, name):
    T, CC = xbc.shape
    tT = _pick(T, (256, 128))
    hb = tT // CONV_HALO
    n = T // tT
    rows = tT + 2 * CONV_HALO
    cs = _pick(CC, (512, 256, 128))

    def body(x_ref, xp_ref, xn_ref, cw_ref, cb_ref, d_ref, dn_ref, dx_ref, dcw_ref, dcb_ref):
        i = pl.program_id(0)

        @pl.when(i == 0)
        def _():
            dcw_ref[...] = jnp.zeros_like(dcw_ref)
            dcb_ref[...] = jnp.zeros_like(dcb_ref)

        r = lax.broadcasted_iota(jnp.int32, (rows, 1), 0)
        own = (r >= CONV_HALO) & (r < tT + CONV_HALO)
        for c0 in range(0, CC, cs):
            sl = slice(c0, c0 + cs)
            x = jnp.concatenate([jnp.where(i > 0, xp_ref[:, sl], 0.0), x_ref[:, sl],
                                 jnp.where(i < n - 1, xn_ref[:, sl], 0.0)], axis=0)
            da = jnp.concatenate([jnp.zeros((CONV_HALO, cs), F32), d_ref[:, sl],
                                  jnp.where(i < n - 1, dn_ref[:, sl], 0.0)], axis=0)
            xc = _conv(x, cw_ref, cb_ref, sl)
            sg = _sigmoid(xc)
            dxc = da * sg * (1.0 + xc * (1.0 - sg))
            acc = cw_ref[CONV_K - 1:CONV_K, sl] * dxc
            for k in range(CONV_K - 1):
                acc = acc + cw_ref[k:k + 1, sl] * pltpu.roll(dxc, rows - (CONV_K - 1 - k), axis=0)
            dx_ref[:, sl] = acc[CONV_HALO:tT + CONV_HALO].astype(dx_ref.dtype)
            down = jnp.where(own, dxc, 0.0)
            dcb_ref[:, sl] += jnp.sum(down, axis=0, keepdims=True)
            dcw_ref[CONV_K - 1:CONV_K, sl] += jnp.sum(down * x, axis=0, keepdims=True)
            for k in range(CONV_K - 1):
                dcw_ref[k:k + 1, sl] += jnp.sum(down * pltpu.roll(x, CONV_K - 1 - k, axis=0), axis=0, keepdims=True)

    row = pl.BlockSpec((tT, CC), lambda i: (i, 0))
    prev = pl.BlockSpec((CONV_HALO, CC), lambda i: (jnp.maximum(i * hb - 1, 0), 0))
    nxt = pl.BlockSpec((CONV_HALO, CC), lambda i: (jnp.minimum((i + 1) * hb, T // CONV_HALO - 1), 0))
    return pl.pallas_call(
        body, name=name, grid=(n,),
        in_specs=[row, prev, nxt, pl.BlockSpec((CONV_K, CC), lambda i: (0, 0)), pl.BlockSpec((1, CC), lambda i: (0, 0)),
                  row, nxt],
        out_specs=[row, pl.BlockSpec((CONV_K, CC), lambda i: (0, 0)), pl.BlockSpec((1, CC), lambda i: (0, 0))],
        out_shape=[jax.ShapeDtypeStruct((T, CC), MXU), jax.ShapeDtypeStruct((CONV_K, CC), F32),
                   jax.ShapeDtypeStruct((1, CC), F32)],
        compiler_params=_params("arbitrary"),
    )(xbc, xbc, xbc, cw, cb, dxa, dxa)


def _softplus(x):
    return jnp.maximum(x, 0.0) + jnp.log(1.0 + jnp.exp(-jnp.abs(x)))


def _dot_exact(a, b):
    return lax.dot_general(a, b, (((1,), (0,)), ((), ())), precision=lax.Precision.HIGHEST, preferred_element_type=F32)


def ssd_prep(raw_t, bias, alog, *, name):
    H, T = raw_t.shape
    B = C.blk
    tc = _pick(T, (4 * B, 2 * B, B))

    def body(r_ref, b_ref, al_ref, dt_ref, acs_ref):
        dt = _softplus(r_ref[...] + b_ref[...])
        dt_ref[...] = dt
        dta = dt * (-jnp.exp(al_ref[...]))
        upper = (lax.broadcasted_iota(jnp.int32, (B, B), 0) <= lax.broadcasted_iota(jnp.int32, (B, B), 1)).astype(F32)
        for j in range(tc // B):
            acs_ref[:, j * B:(j + 1) * B] = _dot_exact(dta[:, j * B:(j + 1) * B], upper)

    blk = pl.BlockSpec((H, tc), lambda i: (0, i))
    vec = pl.BlockSpec((H, 1), lambda i: (0, 0))
    return pl.pallas_call(
        body, name=name, grid=(T // tc,), in_specs=[blk, vec, vec], out_specs=[blk, blk],
        out_shape=[jax.ShapeDtypeStruct((H, T), F32)] * 2, compiler_params=_params("parallel"),
    )(raw_t, bias, alog)


def _pair(lo, arr, h0, rows=slice(None)):
    return jnp.where(lo, arr[rows, h0:h0 + 1], arr[rows, h0 + 1:h0 + 2])


def _decay(acs, acs_t, h, causal):
    return jnp.exp(jnp.where(causal, acs[:, h:h + 1] - acs_t[h:h + 1, :], NEG))


def ssd_fwd(xa, dt, acs, acs_t, dskip, *, name):
    T = xa.shape[0]
    B, DI, G, N, HG, P, H = C.blk, C.DI, C.G, C.N, C.HG, C.P, C.H
    assert P == 64 and HG % 2 == 0
    nc = T // B
    W = HG * P

    def body(xa_ref, dt_ref, acs_ref, acst_ref, ds_ref, y_ref, hp_ref, h_scr):
        @pl.when(pl.program_id(0) == 0)
        def _():
            h_scr[...] = jnp.zeros_like(h_scr)

        lo = lax.broadcasted_iota(jnp.int32, (1, 128), 1) < 64
        causal = lax.broadcasted_iota(jnp.int32, (B, B), 0) >= lax.broadcasted_iota(jnp.int32, (B, B), 1)
        dt, acs, acs_t, dsk = dt_ref[...], acs_ref[...], acst_ref[...], ds_ref[...]
        for g in range(G):
            bg = xa_ref[:, DI + g * N:DI + (g + 1) * N]
            cg = xa_ref[:, DI + (G + g) * N:DI + (G + g + 1) * N]
            cb = _dot_nt(cg, bg)
            hg = h_scr[g]
            hp_ref[0, g * N:(g + 1) * N, :] = hg
            yoff = _dot_nn(cg, hg)
            xws, decs = [], []
            for j in range(HG // 2):
                h0 = g * HG + 2 * j
                xsl = slice(h0 * P, (h0 + 2) * P)
                xp = xa_ref[:, xsl]
                ap = _pair(lo, acs, h0)
                alast = _pair(lo, acs, h0, slice(B - 1, B))
                xdt = xp * _pair(lo, dt, h0)
                ys = [_dot_nn(cb * _decay(acs, acs_t, h0 + half, causal), xdt) for half in range(2)]
                y_ref[:, xsl] = (jnp.where(lo, ys[0], ys[1]) + yoff[:, 2 * j * P:(2 * j + 2) * P] * jnp.exp(ap)
                                 + _pair(lo, dsk, h0) * xp)
                xws.append(xdt * jnp.exp(alast - ap))
                decs.append(jnp.exp(alast))
            h_scr[g] = hg * jnp.concatenate(decs, axis=1) + _dot_tn(bg, jnp.concatenate(xws, axis=1))

    tok = lambda w: pl.BlockSpec((B, w), lambda c: (c, 0))
    return pl.pallas_call(
        body, name=name, grid=(nc,),
        in_specs=[tok(C.CC), tok(H), tok(H), pl.BlockSpec((H, B), lambda c: (0, c)), pl.BlockSpec((1, H), lambda c: (0, 0))],
        out_specs=[tok(DI), pl.BlockSpec((1, G * N, W), lambda c: (c, 0, 0))],
        out_shape=[jax.ShapeDtypeStruct((T, DI), F32), jax.ShapeDtypeStruct((nc, G * N, W), F32)],
        scratch_shapes=[pltpu.VMEM((G, N, W), F32)],
        compiler_params=_params("arbitrary"),
    )(xa, dt, acs, acs_t, dskip)


def ssd_bwd(xa, dt, acs, acs_t, dskip, hprev, dy, *, name):
    T = xa.shape[0]
    B, DI, G, N, HG, P, H = C.blk, C.DI, C.G, C.N, C.HG, C.P, C.H
    nc = T // B
    W = HG * P

    def body(xa_ref, dt_ref, acs_ref, acst_ref, ds_ref, hp_ref, dy_ref, dxa_ref, ddt_ref, dacs_ref, dd_ref, dh_scr):
        @pl.when(pl.program_id(0) == 0)
        def _():
            dh_scr[...] = jnp.zeros_like(dh_scr)
            dd_ref[...] = jnp.zeros_like(dd_ref)

        lo = lax.broadcasted_iota(jnp.int32, (1, 128), 1) < 64
        hi = jnp.logical_not(lo)
        causal = lax.broadcasted_iota(jnp.int32, (B, B), 0) >= lax.broadcasted_iota(jnp.int32, (B, B), 1)
        hlane = lax.broadcasted_iota(jnp.int32, (1, H), 1)
        hsub = lax.broadcasted_iota(jnp.int32, (B, 1), 0)
        lastrow = lax.broadcasted_iota(jnp.int32, (B, 1), 0) == B - 1
        dt, acs, acs_t, dsk = dt_ref[...], acs_ref[...], acst_ref[...], ds_ref[...]
        d_acs = jnp.zeros((B, H), F32)
        d_acs_t = jnp.zeros((B, B), F32)
        d_dt = jnp.zeros((B, H), F32)
        d_d = jnp.zeros((1, H), F32)

        def rsum(v):
            return (jnp.sum(jnp.where(lo, v, 0.0), axis=1, keepdims=True),
                    jnp.sum(jnp.where(hi, v, 0.0), axis=1, keepdims=True))

        for g in range(G):
            bsl = slice(DI + g * N, DI + (g + 1) * N)
            csl = slice(DI + (G + g) * N, DI + (G + g + 1) * N)
            bg, cg = xa_ref[:, bsl], xa_ref[:, csl]
            cb = _dot_nt(cg, bg)
            hg = hp_ref[0, g * N:(g + 1) * N, :]
            dhg = dh_scr[g]
            yoff = _dot_nn(cg, hg)
            bds = _dot_nn(bg, dhg)
            d_cb = jnp.zeros((B, B), F32)
            dyes, xws, decs = [], [], []
            for j in range(HG // 2):
                h0 = g * HG + 2 * j
                xsl = slice(h0 * P, (h0 + 2) * P)
                psl = slice(2 * j * P, (2 * j + 2) * P)
                xp, dyp = xa_ref[:, xsl], dy_ref[:, xsl]
                dtp = _pair(lo, dt, h0)
                ap = _pair(lo, acs, h0)
                alast = _pair(lo, acs, h0, slice(B - 1, B))
                ea, ew, el = jnp.exp(ap), jnp.exp(alast - ap), jnp.exp(alast)
                xdt = xp * dtp
                halves = []
                for half in range(2):
                    h = h0 + half
                    lm = _decay(acs, acs_t, h, causal)
                    m = cb * lm
                    d_m = _dot_nt(jnp.where(lo if half == 0 else hi, dyp, 0.0), xdt)
                    d_cb = d_cb + d_m * lm
                    wgt = d_m * m
                    d_acs = d_acs + jnp.where(hlane == h, jnp.sum(wgt, axis=1, keepdims=True), 0.0)
                    d_acs_t = d_acs_t + jnp.where(hsub == h, jnp.sum(wgt, axis=0, keepdims=True), 0.0)
                    halves.append(_dot_tn(m, dyp))
                bdp = bds[:, psl]
                dxdt = jnp.where(lo, halves[0], halves[1]) + ew * bdp
                dxa_ref[:, xsl] = dtp * dxdt + _pair(lo, dsk, h0) * dyp
                xw = xdt * ew
                terms_dt = rsum(dxdt * xp)
                terms_dd = rsum(dyp * xp)
                terms_off = rsum(dyp * (ea * yoff[:, psl]))
                terms_e = rsum(xw * bdp)
                terms_h = rsum(hg[:, psl] * dhg[:, psl])
                for half in range(2):
                    h = h0 + half
                    sel = hlane == h
                    d_dt = d_dt + jnp.where(sel, terms_dt[half], 0.0)
                    d_d = d_d + jnp.where(sel, jnp.sum(terms_dd[half], axis=0, keepdims=True), 0.0)
                    e_last = jnp.sum(jnp.where(lo if half == 0 else hi, el, 0.0), axis=1, keepdims=True) * (1.0 / P)
                    d_last = (jnp.sum(terms_e[half], axis=0, keepdims=True)
                              + e_last * jnp.sum(terms_h[half], axis=0, keepdims=True))
                    d_acs = d_acs + jnp.where(sel, terms_off[half] - terms_e[half] + jnp.where(lastrow, d_last, 0.0), 0.0)
                dyes.append(dyp * ea)
                xws.append(xw)
                decs.append(el)
            dye = jnp.concatenate(dyes, axis=1)
            xwc = jnp.concatenate(xws, axis=1)
            dxa_ref[:, csl] = _dot_nn(d_cb, bg) + _dot_nt(dye, hg)
            dxa_ref[:, bsl] = _dot_tn(d_cb, cg) + _dot_nt(xwc, dhg)
            dh_scr[g] = dhg * jnp.concatenate(decs, axis=1) + _dot_tn(cg, dye)
        ddt_ref[...] = d_dt
        dacs_ref[...] = d_acs - d_acs_t.T[:, :H]
        dd_ref[...] += d_d

    rev = lambda w: pl.BlockSpec((B, w), lambda c: (nc - 1 - c, 0))
    vec = pl.BlockSpec((1, H), lambda c: (0, 0))
    return pl.pallas_call(
        body, name=name, grid=(nc,),
        in_specs=[rev(C.CC), rev(H), rev(H), pl.BlockSpec((H, B), lambda c: (0, nc - 1 - c)), vec,
                  pl.BlockSpec((1, G * N, W), lambda c: (nc - 1 - c, 0, 0)), rev(DI)],
        out_specs=[rev(C.CC), rev(H), rev(H), vec],
        out_shape=[jax.ShapeDtypeStruct((T, C.CC), F32), jax.ShapeDtypeStruct((T, H), F32),
                   jax.ShapeDtypeStruct((T, H), F32), jax.ShapeDtypeStruct((1, H), F32)],
        scratch_shapes=[pltpu.VMEM((G, N, W), F32)],
        compiler_params=_params("arbitrary"),
    )(xa, dt, acs, acs_t, dskip, hprev, dy)


def ssd_post(ddt, dacs, dt, raw, bias, alog, *, name):
    T, H = ddt.shape
    B = C.blk
    tc = _pick(T, (4 * B, 2 * B, B))

    def body(ddt_ref, dacs_ref, dt_ref, raw_ref, b_ref, al_ref, draw_ref, db_ref, dal_ref):
        @pl.when(pl.program_id(0) == 0)
        def _():
            db_ref[...] = jnp.zeros_like(db_ref)
            dal_ref[...] = jnp.zeros_like(dal_ref)

        a = -jnp.exp(al_ref[...])
        lower = (lax.broadcasted_iota(jnp.int32, (B, B), 0) <= lax.broadcasted_iota(jnp.int32, (B, B), 1)).astype(F32)
        for j in range(tc // B):
            sl = slice(j * B, (j + 1) * B)
            rc = _dot_exact(lower, dacs_ref[sl, :])
            dtv = dt_ref[sl, :]
            draw = (ddt_ref[sl, :] + a * rc) * _sigmoid(raw_ref[sl, :] + b_ref[...])
            draw_ref[sl, :] = draw
            db_ref[...] += jnp.sum(draw, axis=0, keepdims=True)
            dal_ref[...] += jnp.sum(dtv * rc, axis=0, keepdims=True) * a

    blk = pl.BlockSpec((tc, H), lambda i: (i, 0))
    vec = pl.BlockSpec((1, H), lambda i: (0, 0))
    return pl.pallas_call(
        body, name=name, grid=(T // tc,), in_specs=[blk, blk, blk, blk, vec, vec], out_specs=[blk, vec, vec],
        out_shape=[jax.ShapeDtypeStruct((T, H), F32), jax.ShapeDtypeStruct((1, H), F32), jax.ShapeDtypeStruct((1, H), F32)],
        compiler_params=_params("arbitrary"),
    )(ddt, dacs, dt, raw, bias, alog)


MESH = pl.DeviceIdType.MESH
PACK_W = 1024
ANY = pl.BlockSpec(memory_space=pl.ANY)


def _place():
    return lax.axis_index("x"), lax.axis_index("y"), lax.axis_index("c")


def _other_chips(x, y):
    return [(1 - x, y), (x, 1 - y), (1 - x, 1 - y)]


def _remote(src, dst, send_sems, recv_sems, k, to):
    return pltpu.make_async_remote_copy(src_ref=src, dst_ref=dst, send_sem=send_sems.at[k], recv_sem=recv_sems.at[k],
                                        device_id=to, device_id_type=MESH)


def _half(c, rows):
    rh = rows // 2
    return pl.ds(pl.multiple_of(c * rh, 16 if rh % 16 == 0 else 8), rh)


def gather_layer(shards, *, name):
    n = len(shards)

    def body(*refs):
        ps, gs = refs[:n], refs[n:2 * n]
        send_sems, recv_sems = refs[2 * n:]
        x, y, c = _place()
        s = 2 * x + y
        sib = (x, y, 1 - c)
        chips = _other_chips(x, y)
        copy = functools.partial(_remote, send_sems=send_sems, recv_sems=recv_sems)
        started = []
        for w in range(n):
            started.append(copy(ps[w], gs[w].at[s], k=w, to=sib))
        for w in range(n):
            mine = _half(c, ps[w].shape[0])
            for j, (px, py) in enumerate(chips):
                started.append(copy(ps[w].at[mine], gs[w].at[s, mine], k=n + 3 * w + j, to=(px, py, c)))
        for cp in started:
            cp.start()
        passed = []
        for w in range(n):
            mine = _half(c, ps[w].shape[0])
            for j, (px, py) in enumerate(chips):
                there = gs[w].at[2 * px + py, mine]
                copy(there, there, k=n + 3 * w + j, to=(px, py, c)).wait_recv()
                fw = copy(there, there, k=4 * n + 3 * w + j, to=sib)
                fw.start()
                passed.append(fw)
        for w in range(n):
            copy(ps[w], gs[w].at[s], k=w, to=sib).wait_recv()
            theirs = _half(1 - c, ps[w].shape[0])
            for j, (px, py) in enumerate(chips):
                there = gs[w].at[2 * px + py, theirs]
                copy(there, there, k=4 * n + 3 * w + j, to=sib).wait_recv()
        for cp in started + passed:
            cp.wait_send()

    return pl.pallas_call(
        body, name=name, in_specs=[ANY] * n, out_specs=[ANY] * n,
        out_shape=[jax.ShapeDtypeStruct((4,) + p.shape, p.dtype) for p in shards],
        scratch_shapes=[pltpu.SemaphoreType.DMA((7 * n,)), pltpu.SemaphoreType.DMA((7 * n,))],
    )(*shards)


HBM = pl.BlockSpec(memory_space=pltpu.HBM)
SEMS = pl.BlockSpec(memory_space=pltpu.SEMAPHORE)


def gather_plan(x, y, c, ps, gs):
    s = 2 * x + y
    sends, lands = [], []
    for p, g in zip(ps, gs):
        rows = p.shape[0]
        sends.append((p, g.at[s], (x, y, 1 - c)))
        lands.append(g.at[s])
        for px, py in _other_chips(x, y):
            for pc in (c, 1 - c):
                sends.append((p.at[_half(c, rows)], g.at[s, _half(c, rows)], (px, py, pc)))
                lands.append(g.at[2 * px + py, _half(pc, rows)])
    return sends, lands


def scatter_plan(x, y, c, ps, gs):
    sends, lands = [], []
    for p, g in zip(ps, gs):
        for j, (px, py) in enumerate(_other_chips(x, y)):
            sends.append((p.at[2 * px + py], g.at[j], (px, py, c)))
            lands.append(g.at[j])
    return sends, lands


def _hbm(a):
    return pltpu.with_memory_space_constraint(a, pltpu.HBM)


def exchange_start(srcs, land_shapes, plan, ncopies, *, name):
    n = len(srcs)

    def body(*refs):
        ps, gs = refs[:n], refs[n:2 * n]
        send_sems, recv_sems = refs[2 * n], refs[2 * n + 1]
        token = refs[-1]
        sends, _ = plan(*_place(), ps, gs)
        for k, (src, dst, to) in enumerate(sends):
            _remote(src, dst, send_sems, recv_sems, k, to).start()
        token[...] = jnp.zeros_like(token)

    lands = [_hbm(lax.empty(s.shape, s.dtype)) for s in land_shapes]
    outs = pl.pallas_call(
        body, name=name,
        out_shape=[pltpu.SemaphoreType.DMA((ncopies,)), pltpu.SemaphoreType.DMA((ncopies,))]
        + [pltpu.HBM(a.shape, a.dtype) for a in srcs] + [pltpu.HBM(s.shape, s.dtype) for s in land_shapes]
        + [jax.ShapeDtypeStruct((8, 128), F32)],
        in_specs=[HBM] * (2 * n), out_specs=[SEMS, SEMS] + [HBM] * (2 * n) + [pl.BlockSpec(memory_space=pltpu.VMEM)],
        input_output_aliases={k: 2 + k for k in range(2 * n)},
        compiler_params=pltpu.CompilerParams(has_side_effects=pltpu.SideEffectType.DATAFLOW_SIDE_EFFECTING),
    )(*[_hbm(a) for a in srcs], *lands)
    return outs[0], outs[1], list(outs[2:2 + n]), list(outs[2 + n:2 + 2 * n]), outs[-1]


def exchange_wait(send_sems, recv_sems, srcs, lands, plan, after, *, name):
    n = len(srcs)

    def body(*refs):
        ps, gs = refs[:n], refs[n:2 * n]
        send_sems, recv_sems = refs[2 * n], refs[2 * n + 1]
        sends, arrivals = plan(*_place(), ps, gs)
        for k, ((src, dst, to), land) in enumerate(zip(sends, arrivals)):
            _remote(src, dst, send_sems, recv_sems, k, to).wait_send()
            _remote(land, land, send_sems, recv_sems, k, to).wait_recv()

    outs = pl.pallas_call(
        body, name=name,
        out_shape=[pltpu.HBM(a.shape, a.dtype) for a in srcs] + [pltpu.HBM(a.shape, a.dtype) for a in lands],
        in_specs=[HBM] * (2 * n) + [SEMS, SEMS, ANY], out_specs=[HBM] * (2 * n),
        input_output_aliases={k: k for k in range(2 * n)},
        compiler_params=pltpu.CompilerParams(has_side_effects=pltpu.SideEffectType.DATAFLOW_SIDE_EFFECTING),
    )(*srcs, *lands, send_sems, recv_sems, after)
    return list(outs[n:])


def swap_halves(items, *, name):
    n = len(items)

    def body(*refs):
        gs, rs = refs[:n], refs[n:2 * n]
        send_sems, recv_sems = refs[2 * n:]
        x, y, c = _place()
        cps = [_remote(g.at[:, _half(1 - c, g.shape[1])], r, send_sems, recv_sems, k, (x, y, 1 - c))
               for k, (g, r) in enumerate(zip(gs, rs))]
        for cp in cps:
            cp.start()
        for cp in cps:
            cp.wait()

    return pl.pallas_call(
        body, name=name, in_specs=[ANY] * n, out_specs=[ANY] * n,
        out_shape=[jax.ShapeDtypeStruct((4, g.shape[1] // 2, g.shape[2]), g.dtype) for g in items],
        scratch_shapes=[pltpu.SemaphoreType.DMA((n,)), pltpu.SemaphoreType.DMA((n,))],
    )(*items)


def scatter_chips(items, *, name):
    n = len(items)

    def body(*refs):
        as_, rs = refs[:n], refs[n:2 * n]
        send_sems, recv_sems = refs[2 * n:]
        x, y, c = _place()
        cps = [_remote(a.at[2 * px + py], r.at[j], send_sems, recv_sems, 3 * k + j, (px, py, c))
               for k, (a, r) in enumerate(zip(as_, rs)) for j, (px, py) in enumerate(_other_chips(x, y))]
        for cp in cps:
            cp.start()
        for cp in cps:
            cp.wait()

    return pl.pallas_call(
        body, name=name, in_specs=[ANY] * n, out_specs=[ANY] * n,
        out_shape=[jax.ShapeDtypeStruct((3,) + a.shape[1:], a.dtype) for a in items],
        scratch_shapes=[pltpu.SemaphoreType.DMA((3 * n,)), pltpu.SemaphoreType.DMA((3 * n,))],
    )(*items)


def join_halves(items, *, name):
    n = len(items)

    def body(*refs):
        rs, outs = refs[:n], refs[n:2 * n]
        send_sems, recv_sems = refs[2 * n:]
        x, y, c = _place()
        sib = (x, y, 1 - c)
        cps = []
        for k, (r, o) in enumerate(zip(rs, outs)):
            mine = _half(c, r.shape[0])
            cps.append(_remote(r.at[mine], o.at[mine], send_sems, recv_sems, k, sib))
            cps[-1].start()
        for k, (r, o) in enumerate(zip(rs, outs)):
            theirs = _half(1 - c, r.shape[0])
            _remote(r.at[theirs], o.at[theirs], send_sems, recv_sems, k, sib).wait_recv()
        for cp in cps:
            cp.wait_send()

    return pl.pallas_call(
        body, name=name, in_specs=[ANY] * n, out_specs=[ANY] * n,
        out_shape=[jax.ShapeDtypeStruct(r.shape, r.dtype) for r in items],
        input_output_aliases={k: k for k in range(n)},
        scratch_shapes=[pltpu.SemaphoreType.DMA((n,)), pltpu.SemaphoreType.DMA((n,))],
    )(*items)


def gather_all(v, *, name, total):
    rows, W = v.shape

    def body(v_ref, o_ref, *scr):
        buf = scr[0] if total else o_ref
        send_sems, recv_sems = scr[-2], scr[-1]
        x, y, c = _place()
        me = 4 * x + 2 * y + c
        flips = [(k >> 2 & 1, k >> 1 & 1, k & 1) for k in range(1, 8)]
        peers = [((1 - x) if fx else x, (1 - y) if fy else y, (1 - c) if fc else c) for fx, fy, fc in flips]
        out = []
        for k, peer in enumerate(peers):
            cp = pltpu.make_async_remote_copy(src_ref=v_ref, dst_ref=buf.at[me], send_sem=send_sems.at[k],
                                              recv_sem=recv_sems.at[k], device_id=peer, device_id_type=MESH)
            cp.start()
            out.append(cp)
        buf[me] = v_ref[...]
        for k, (px, py, pc) in enumerate(peers):
            pltpu.make_async_remote_copy(src_ref=v_ref, dst_ref=buf.at[4 * px + 2 * py + pc], send_sem=send_sems.at[k],
                                         recv_sem=recv_sems.at[k], device_id=(px, py, pc), device_id_type=MESH).wait_recv()
        for cp in out:
            cp.wait_send()
        if total:
            acc = buf[0]
            for d in range(1, 8):
                acc = acc + buf[d]
            o_ref[...] = acc

    vm = pl.BlockSpec(memory_space=pltpu.VMEM)
    return pl.pallas_call(
        body, name=name, in_specs=[vm], out_specs=vm,
        out_shape=jax.ShapeDtypeStruct((rows, W) if total else (8, rows, W), F32),
        scratch_shapes=([pltpu.VMEM((8, rows, W), F32)] if total else [])
        + [pltpu.SemaphoreType.DMA((7,)), pltpu.SemaphoreType.DMA((7,))],
    )(v)


def pair_sum(g, r, *, name):
    _, R, W = g.shape
    Rh = R // 2
    tr = _pick(Rh, (512, 256, 128, 64, 32, 16))
    nb = Rh // tr

    def body(g_ref, r_ref, o_ref):
        o_ref[...] = (g_ref[...].astype(F32) + r_ref[...].astype(F32)).astype(o_ref.dtype)

    return pl.pallas_call(
        body, name=name, grid=(4, nb),
        in_specs=[pl.BlockSpec((1, tr, W), lambda s, i: (s, lax.axis_index("c") * nb + i, 0)),
                  pl.BlockSpec((1, tr, W), lambda s, i: (s, i, 0))],
        out_specs=pl.BlockSpec((1, tr, W), lambda s, i: (s, i, 0)),
        out_shape=jax.ShapeDtypeStruct((4, Rh, W), g.dtype), compiler_params=_params("parallel", "parallel"),
    )(g, r)


def chip_sum(g, r1, r2, *, name):
    _, R, W = g.shape
    Rh = R // 2
    tr = _pick(Rh, (512, 256, 128, 64, 32, 16))
    nb = Rh // tr

    def body(g_ref, r1_ref, a_ref, b_ref, c_ref, o_ref):
        acc = g_ref[0].astype(F32) + r1_ref[0].astype(F32)
        for ref in (a_ref, b_ref, c_ref):
            acc = acc + ref[0].astype(F32)
        o_ref[...] = acc

    shard = lambda: 2 * lax.axis_index("x") + lax.axis_index("y")
    half = lambda i: lax.axis_index("c") * nb + i
    return pl.pallas_call(
        body, name=name, grid=(nb,),
        in_specs=[pl.BlockSpec((1, tr, W), lambda i: (shard(), half(i), 0)),
                  pl.BlockSpec((1, tr, W), lambda i: (shard(), i, 0))]
        + [pl.BlockSpec((1, tr, W), lambda i, j=j: (j, i, 0)) for j in range(3)],
        out_specs=pl.BlockSpec((tr, W), lambda i: (half(i), 0)),
        out_shape=jax.ShapeDtypeStruct((R, W), F32), compiler_params=_params("parallel"),
    )(g, r1, r2, r2, r2)


BIG = (("w_in", False), ("pool_w", False), ("w_attn_br", False), ("w_pool_br", False), ("w_ssm_br", True),
       ("w_out", True), ("w_gate_up", False), ("w_down", True))


def _join(piece, axis):
    t = jnp.moveaxis(piece, 0, axis)
    shp = t.shape
    return t.reshape(shp[:axis] + (shp[axis] * shp[axis + 1],) + shp[axis + 2:])


def _seg_bounds():
    aw, kw, _, pw, di, cc, h, gd = C.in_widths
    o = [0, aw + 2 * kw]
    for wdt in (pw, di, cc, h, gd):
        o.append(o[-1] + wdt)
    return o


IN_PAD = 640


def _in_segments(blocks):
    w_in = jnp.concatenate([blocks[s] for s in range(4)], axis=1)
    o = _seg_bounds()
    segs = [w_in[:, o[i]:o[i + 1]] for i in range(6)]
    segs[4] = jnp.pad(segs[4], ((0, 0), (0, 128 - C.H)))
    width = sum(t.shape[1] for t in segs)
    pad = -width % IN_PAD
    return segs, jnp.concatenate(segs + [jnp.zeros((w_in.shape[0], pad), w_in.dtype)], axis=1)


def _layer_fwd(x, p, tag):
    nm = lambda s: f"{s}_{tag}"
    H = C.H
    h = rms_fwd(x, p["ln1_w"], name=nm("rms1"))
    wq, wu, wz, wx, wd, wg = p["in_segs"]
    qkv = matmul(h, wq, name=nm("mm_qkv"), out_dtype=MXU)
    u = matmul(h, wu, name=nm("mm_u"))
    z = matmul(h, wz, name=nm("mm_z"))
    xbc = matmul(h, wx, name=nm("mm_xbc"))
    dtp = matmul(h, wd, name=nm("mm_dt"))
    gl = matmul(h, wg, name=nm("mm_gate"), out_dtype=MXU)
    att = attn_fwd(qkv, p["attn_sink"], name=nm("attn_fwd"))
    pool, mixed = pool_fwd(u, p["pool_w"], p["pool_scale"], name=nm("pool_fwd"))
    xa = conv_fwd(xbc, p["conv_w"], p["conv_b"], name=nm("conv_fwd"))
    raw = dtp[:, :H]
    dt_t, acs_t = ssd_prep(raw.T, p["dt_bias"].T, p["a_log"].T, name=nm("ssd_prep"))
    dt, acs = dt_t.T, acs_t.T
    y, hprev = ssd_fwd(xa, dt, acs, acs_t, p["d_skip"], name=nm("ssd_fwd"))
    ssm = gnorm_fwd(y, z, p["ssm_norm_w"], name=nm("gnorm_fwd"))
    abr = matmul(att, name=nm("mm_abr"), out_dtype=MXU, **p["w_attn_br"])
    pbr = matmul(pool, name=nm("mm_pbr"), out_dtype=MXU, **p["w_pool_br"])
    sbr = matmul(ssm, name=nm("mm_sbr"), out_dtype=MXU, **p["w_ssm_br"])
    merged = merge_fwd(gl, abr, pbr, sbr, name=nm("merge_fwd"))
    xm = matmul(merged, add=x, name=nm("mm_out"), **p["w_out"])
    h2 = rms_fwd(xm, p["ln2_w"], name=nm("rms2"))
    gu = matmul(h2, name=nm("mm_gu"), out_dtype=MXU, **p["w_gate_up"])
    act = swiglu_fwd(gu, name=nm("swiglu_fwd"))
    xo = matmul(act, add=xm, name=nm("mm_down"), **p["w_down"])
    saved = dict(x=x, h=h, qkv=qkv, z=z, xbc=xbc, raw=raw, gl=gl, att=att, pool=pool, mixed=mixed, xa=xa, dt=dt, acs=acs,
                 acs_t=acs_t, y=y, hprev=hprev, ssm=ssm, abr=abr, pbr=pbr, sbr=sbr, merged=merged, xm=xm, h2=h2, gu=gu,
                 act=act)
    return xo, saved


def _layer_bwd(dxo, dxo_m, p, s, tag):
    nm = lambda t: f"{t}_{tag}"
    H = C.H
    g = {}

    def rows4(t):
        return t.reshape(4, t.shape[0] // 4, t.shape[1])

    g["w_down"] = rows4(matmul(s["act"], dxo_m, ta=True, out_dtype=MXU, name=nm("mmg_down")))
    dact = matmul(dxo_m, tb=True, out_dtype=MXU, name=nm("mmb_down"), **p["w_down"])
    dgu = swiglu_bwd(s["gu"], dact, name=nm("swiglu_bwd"))
    g["w_gate_up"] = matmul(s["h2"], dgu, ta=True, out_dtype=MXU, out_cols=True, name=nm("mmg_gu"))
    dh2 = matmul(dgu, tb=True, name=nm("mmb_gu"), **p["w_gate_up"])
    dxm, dxm_m, g["ln2_w"] = rms_bwd(s["xm"], p["ln2_w"], dh2, dxo, name=nm("rms2_bwd"))
    g["w_out"] = rows4(matmul(s["merged"], dxm_m, ta=True, out_dtype=MXU, name=nm("mmg_out")))
    dmerged = matmul(dxm_m, tb=True, out_dtype=MXU, name=nm("mmb_out"), **p["w_out"])
    dabr, dpbr, dsbr, dgl = merge_bwd(s["gl"], s["abr"], s["pbr"], s["sbr"], dmerged, name=nm("merge_bwd"))
    g["w_attn_br"] = matmul(s["att"], dabr, ta=True, out_dtype=MXU, out_cols=True, name=nm("mmg_abr"))
    g["w_pool_br"] = matmul(s["pool"], dpbr, ta=True, out_dtype=MXU, out_cols=True, name=nm("mmg_pbr"))
    g["w_ssm_br"] = rows4(matmul(s["ssm"], dsbr, ta=True, out_dtype=MXU, name=nm("mmg_sbr")))
    datt = matmul(dabr, tb=True, out_dtype=MXU, name=nm("mmb_abr"), **p["w_attn_br"])
    dpool = matmul(dpbr, tb=True, out_dtype=MXU, name=nm("mmb_pbr"), **p["w_pool_br"])
    dssm = matmul(dsbr, tb=True, out_dtype=MXU, name=nm("mmb_sbr"), **p["w_ssm_br"])
    dq, dk, dv, g["attn_sink"] = attn_bwd(s["qkv"], p["attn_sink"], datt, name=nm("attn_bwd"))
    du, dpw, g["pool_scale"] = pool_bwd(dpool, s["mixed"], p["pool_w"], p["pool_scale"], name=nm("pool_bwd"))
    pg = dpw.shape[1] // 4
    g["pool_w"] = jnp.moveaxis(dpw.reshape(4, 4, pg, dpw.shape[2]), 1, 0).reshape(4, 4 * pg, dpw.shape[2]).astype(MXU)
    dy, dz, g["ssm_norm_w"] = gnorm_bwd(s["y"], s["z"], p["ssm_norm_w"], dssm, name=nm("gnorm_bwd"))
    dxa, ddt, dacs, g["d_skip"] = ssd_bwd(s["xa"], s["dt"], s["acs"], s["acs_t"], p["d_skip"], s["hprev"], dy,
                                          name=nm("ssd_bwd"))
    draw, g["dt_bias"], g["a_log"] = ssd_post(ddt, dacs, s["dt"], s["raw"], p["dt_bias"], p["a_log"], name=nm("ssd_post"))
    dxbc, g["conv_w"], g["conv_b"] = conv_bwd(s["xbc"], p["conv_w"], p["conv_b"], dxa, name=nm("conv_bwd"))
    return g, dict(dq=dq, dk=dk, dv=dv, du=du, dz=dz, dxbc=dxbc, draw=draw, dgl=dgl, dxm=dxm)


def _layer_bwd_in(c, p, s, tag):
    nm = lambda t: f"{t}_{tag}"
    H = C.H
    dq, dk, dv, du, dz, dxbc, draw, dgl, dxm = (c[k] for k in ("dq", "dk", "dv", "du", "dz", "dxbc", "draw", "dgl", "dxm"))
    w_all = p["in_all"]
    ddtp = jnp.pad(draw, ((0, 0), (0, 128 - H))).astype(MXU)
    parts = [dq, dk, dv, du, dz, dxbc, ddtp, dgl]
    used = sum(t.shape[1] for t in parts)
    dproj = jnp.concatenate(parts + [jnp.zeros((dq.shape[0], w_all.shape[1] - used), MXU)], axis=1)
    dh = matmul(dproj, w_all, tb=True, name=nm("mmb_in"))
    g_all = matmul(s["h"], dproj, ta=True, out_dtype=MXU, name=nm("mmg_in"))
    o = _seg_bounds()
    dt0 = o[4]
    g_in = jnp.concatenate([g_all[:, :dt0 + H], g_all[:, dt0 + 128:dt0 + 128 + (o[6] - o[5])]], axis=1)
    nc = g_in.shape[1] // 4
    g_w_in = jnp.stack([g_in[:, k * nc:(k + 1) * nc] for k in range(4)])
    dx, dx_m, g_ln1 = rms_bwd(s["x"], p["ln1_w"], dh, dxm, name=nm("rms1_bwd"))
    return dx, dx_m, g_w_in, g_ln1


SMALL = ("ln1_w", "attn_sink", "conv_w", "conv_b", "dt_bias", "a_log", "d_skip", "ssm_norm_w", "pool_scale", "ln2_w")
WEIGHTS = ("ln1_w", "w_in", "attn_sink", "conv_w", "conv_b", "dt_bias", "a_log", "d_skip", "ssm_norm_w", "pool_w",
           "pool_scale", "w_attn_br", "w_pool_br", "w_ssm_br", "w_out", "ln2_w", "w_gate_up", "w_down", "final_w")


def _step(x, loss_target, w, m, v):
    depth = C.depth
    xi, yi, ci = _place()
    shard = 2 * xi + yi

    names = [n for n, _ in BIG]
    nbig = len(names)
    cw_cols = w["conv_w"].shape[2]

    def shards(i):
        taps = w["conv_w"][i].reshape(CONV_TAP_ROWS, -1)
        return [w[n][i].astype(MXU).reshape(-1, w[n].shape[-1]) for n in names] + [taps]

    def params(i, blocks):
        full = dict(zip(names, blocks))
        p = {n: w[n][i][None] for n in SMALL if n != "conv_w"}
        p["conv_w"] = _join(blocks[-1].reshape(4, CONV_K, cw_cols), 1)
        p["in_segs"], p["in_all"] = _in_segments(full["w_in"])
        pw = full["pool_w"]
        p["pool_w"] = _join(pw.reshape(4, 4, pw.shape[1] // 4, pw.shape[2]), 1)
        for n, rw in BIG[2:]:
            t = full[n]
            p[n] = dict(b=t.reshape(4 * t.shape[1], t.shape[2])) if rw else dict(b=t, b_cols=True)
        return p

    xs = x[0]
    layers, saved = [], []
    blocks = gather_layer(shards(0), name="gather_l0")
    for i in range(depth):
        nxt = None
        if i + 1 < depth:
            src = shards(i + 1)
            nxt = exchange_start(src, [jax.ShapeDtypeStruct((4,) + a.shape, a.dtype) for a in src], gather_plan,
                                 7 * len(src), name=f"gather_l{i + 1}_start")
            xs = xs + nxt[4][0, 0]
        layers.append(params(i, blocks))
        xs, sv = _layer_fwd(xs, layers[i], f"l{i}")
        saved.append(sv)
        if nxt is not None:
            blocks = exchange_wait(nxt[0], nxt[1], nxt[2], nxt[3], gather_plan, xs, name=f"gather_l{i + 1}_wait")
    loss_part, dx, dx_m, g_final = final_loss(xs, w["final_w"][None], loss_target[0], name="final_loss")

    def pair_sums(i, ns, gs):
        sb = swap_halves(gs, name=f"swap_halves_{ns[0]}_l{i}")
        return sb, [pair_sum(g, r, name=f"pair_sum_{n}_l{i}") for n, g, r in zip(ns, gs, sb)]

    def scatter_start(chip, tag):
        return exchange_start(chip, [jax.ShapeDtypeStruct((3,) + a.shape[1:], a.dtype) for a in chip], scatter_plan,
                              3 * len(chip), name=f"scatter_{tag}_start")

    def scatter_wait(pend, after, tag):
        return exchange_wait(pend[0], pend[1], pend[2], pend[3], scatter_plan, after, name=f"scatter_{tag}_wait")

    grads = [None] * depth
    pieces = {}
    rest = names[1:]
    pend_in = None
    for i in reversed(range(depth)):
        if pend_in is not None:
            dx_m = dx_m + pend_in[0][4][0, 0].astype(dx_m.dtype)
        grads[i], cot = _layer_bwd(dx, dx_m, layers[i], saved[i], f"l{i}")
        g_rest = [grads[i][n] for n in rest]
        sib_rest, chip_rest = pair_sums(i, rest, g_rest)
        pend_rest = scatter_start(chip_rest, f"rest_l{i}")
        cot["draw"] = cot["draw"] + pend_rest[4][0, 0]
        dx, dx_m, g_in, grads[i]["ln1_w"] = _layer_bwd_in(cot, layers[i], saved[i], f"l{i}")
        if pend_in is not None:
            pend, g_up, sib_up = pend_in
            pieces[(i + 1, "w_in")] = (g_up, sib_up[0], scatter_wait(pend, dx, f"in_l{i + 1}")[0])
        for n, g, r, o in zip(rest, g_rest, sib_rest, scatter_wait(pend_rest, dx, f"rest_l{i}")):
            pieces[(i, n)] = (g, r, o)
        sib_in, chip_in = pair_sums(i, ["w_in"], [g_in])
        if i > 0:
            pend_in = (scatter_start(chip_in, f"in_l{i}"), g_in, sib_in)
        else:
            pieces[(i, "w_in")] = (g_in, sib_in[0], scatter_chips(chip_in, name=f"scatter_in_l{i}")[0])
    keys = [(i, n) for i in range(depth) for n in names]
    mine = [chip_sum(*pieces[k], name=f"chip_sum_{k[1]}_l{k[0]}") for k in keys]
    reduced = dict(zip(keys, join_halves(mine, name="join_halves")))
    gout = {}

    small = [jnp.stack([grads[i][n].reshape(-1) for i in range(depth)]).reshape(1, -1) for n in SMALL]
    small += [g_final.reshape(1, -1), loss_part.reshape(1, -1)]
    tot = gather_all(_pack_small(small), name="sum_small", total=True)
    parts = _unpack_small(tot, [t.shape[1] for t in small])
    for n, t in zip(SMALL, parts):
        if n == "conv_w":
            cols = w[n].shape[2]
            gout[n] = lax.dynamic_slice_in_dim(t.reshape(depth, CONV_K, -1), shard * cols, cols, axis=2)
        else:
            gout[n] = t.reshape(w[n].shape)
    gout["final_w"] = parts[-2].reshape(w["final_w"].shape)
    loss = parts[-1].reshape(())

    upd = {}
    for n in WEIGHTS:
        if n in gout:
            upd[n] = (gout[n].reshape(w[n].shape),) + adamw(w[n], gout[n].reshape(w[n].shape), m[n], v[n], name=f"adamw_{n}")
        else:
            upd[n] = adamw_layers(w[n], [reduced[(i, n)] for i in range(depth)], m[n], v[n], name=f"adamw_{n}")
    return (loss, dx[None], *[upd[n][0] for n in WEIGHTS], *[upd[n][1] for n in WEIGHTS],
            *[upd[n][2] for n in WEIGHTS], *[upd[n][3] for n in WEIGHTS])


def _pack_small(parts):
    flat = jnp.concatenate(parts, axis=1)
    rows = -(-flat.shape[1] // PACK_W)
    rows = -(-rows // 8) * 8
    return jnp.pad(flat, ((0, 0), (0, rows * PACK_W - flat.shape[1]))).reshape(rows, PACK_W)


def _unpack_small(buf, sizes):
    flat = buf.reshape(-1)
    out, off = [], 0
    for n in sizes:
        out.append(flat[off:off + n])
        off += n
    return out


def kernel(x, ln1_w, w_in, attn_sink, conv_w, conv_b, dt_bias, a_log, d_skip, ssm_norm_w, pool_w, pool_scale, w_attn_br, w_pool_br, w_ssm_br, w_out, ln2_w, w_gate_up, w_down, final_w, loss_target, m_ln1_w, m_w_in, m_attn_sink, m_conv_w, m_conv_b, m_dt_bias, m_a_log, m_d_skip, m_ssm_norm_w, m_pool_w, m_pool_scale, m_w_attn_br, m_w_pool_br, m_w_ssm_br, m_w_out, m_ln2_w, m_w_gate_up, m_w_down, m_final_w, v_ln1_w, v_w_in, v_attn_sink, v_conv_w, v_conv_b, v_dt_bias, v_a_log, v_d_skip, v_ssm_norm_w, v_pool_w, v_pool_scale, v_w_attn_br, v_w_pool_br, v_w_ssm_br, v_w_out, v_ln2_w, v_w_gate_up, v_w_down, v_final_w):
    w = dict(ln1_w=ln1_w, w_in=w_in, attn_sink=attn_sink, conv_w=conv_w, conv_b=conv_b, dt_bias=dt_bias, a_log=a_log,
             d_skip=d_skip, ssm_norm_w=ssm_norm_w, pool_w=pool_w, pool_scale=pool_scale, w_attn_br=w_attn_br,
             w_pool_br=w_pool_br, w_ssm_br=w_ssm_br, w_out=w_out, ln2_w=ln2_w, w_gate_up=w_gate_up, w_down=w_down,
             final_w=final_w)
    m = dict(ln1_w=m_ln1_w, w_in=m_w_in, attn_sink=m_attn_sink, conv_w=m_conv_w, conv_b=m_conv_b, dt_bias=m_dt_bias,
             a_log=m_a_log, d_skip=m_d_skip, ssm_norm_w=m_ssm_norm_w, pool_w=m_pool_w, pool_scale=m_pool_scale,
             w_attn_br=m_w_attn_br, w_pool_br=m_w_pool_br, w_ssm_br=m_w_ssm_br, w_out=m_w_out, ln2_w=m_ln2_w,
             w_gate_up=m_w_gate_up, w_down=m_w_down, final_w=m_final_w)
    v = dict(ln1_w=v_ln1_w, w_in=v_w_in, attn_sink=v_attn_sink, conv_w=v_conv_w, conv_b=v_conv_b, dt_bias=v_dt_bias,
             a_log=v_a_log, d_skip=v_d_skip, ssm_norm_w=v_ssm_norm_w, pool_w=v_pool_w, pool_scale=v_pool_scale,
             w_attn_br=v_w_attn_br, w_pool_br=v_w_pool_br, w_ssm_br=v_w_ssm_br, w_out=v_w_out, ln2_w=v_ln2_w,
             w_gate_up=v_w_gate_up, w_down=v_w_down, final_w=v_final_w)
    return _step(x, loss_target, w, m, v)
```

```python
import functools

import jax
import jax.numpy as jnp
from jax import lax
from jax.experimental import pallas as pl
from jax.experimental.pallas import tpu as pltpu

F32 = jnp.float32
MXU = jnp.bfloat16
VMEM_LIMIT = 56 * 1024 * 1024
EPS = 1e-6
NEG = -1e30

ADAM_LR, ADAM_B1, ADAM_B2, ADAM_EPS, ADAM_WD, ADAM_STEP = 0.001, 0.9, 0.999, 1e-08, 0.01, 10


class Cfg:
    def __init__(self, d_model=2048, seq=8192, depth=2, q_heads=16, kv_heads=4, head_dim=64,
                 ssm_head_dim=64, ssm_groups=4, d_state=128):
        self.D, self.T, self.depth = d_model, seq, depth
        self.hd, self.qh, self.kvh = head_dim, q_heads, kv_heads
        self.AW, self.KW = q_heads * head_dim, kv_heads * head_dim
        self.blk = 128
        self.PW = d_model // 2
        self.PG = self.PW // 4
        self.DI = d_model
        self.P = ssm_head_dim
        self.H = self.DI // self.P
        self.G = ssm_groups
        self.HG = self.H // self.G
        self.N = d_state
        self.CC = self.DI + 2 * self.G * self.N
        self.F = -(-8 * d_model // (3 * 256)) * 256
        self.in_widths = (self.AW, self.KW, self.KW, self.PW, self.DI, self.CC, self.H, 3 * d_model)
        self.in_cols = sum(self.in_widths)


C = Cfg()
POOL_WINDOWS = (2, 4, 8, 16)


def _pick(n, cands):
    for c in cands:
        if n % c == 0:
            return c
    return n


def _params(*sem):
    return pltpu.CompilerParams(dimension_semantics=sem, vmem_limit_bytes=VMEM_LIMIT)


def _sigmoid(x):
    return 1.0 / (1.0 + jnp.exp(-x))


def _dot(a, b, dims):
    return lax.dot_general(a.astype(MXU), b.astype(MXU), (dims, ((), ())), preferred_element_type=F32)


def _dot_nn(a, b):
    return _dot(a, b, ((1,), (0,)))


def _dot_nt(a, b):
    return _dot(a, b, ((1,), (1,)))


def _dot_tn(a, b):
    return _dot(a, b, ((0,), (0,)))


MM_VMEM = 40 * 1024 * 1024
LANES = 128


def _tile(n, cap):
    best = None
    for d in range(LANES, min(n, cap) + 1, LANES):
        if n % d == 0:
            best = d
    return n if best is None else best


def _divisors(n, cap):
    ds = [d for d in range(LANES, min(n, cap) + 1, LANES) if n % d == 0]
    return ds or [n]


HBM_RATE, MXU_RATE, ACC_RATE, STEP_COST = 3.0e12, 0.9e15, 1.2e13, 0.35e-6
MXU_WIDTH = 256


def _choose_tiles(M, N, K, n_unit, k_unit, sa, sb, so, sadd, ta):
    best = None
    for tm in sorted({_tile(M, 1024), _tile(M, 512)}):
        for tk in _divisors(k_unit, 4096):
            for tn in _divisors(n_unit, 2048):
                nk = K // tk
                vmem = 2 * (tm * tk * sa + tk * tn * sb + tm * tn * (so + sadd)) + (tm * tn * 4 if nk > 1 else 0)
                if vmem > MM_VMEM:
                    continue
                steps = (M // tm) * (N // tn) * nk
                a_bytes = M * K * sa * (N // tn if (nk > 1 or ta) else 1)
                b_bytes = K * N * sb * (M // tm)
                hbm = (a_bytes + b_bytes + M * N * (so + sadd)) / HBM_RATE
                fill = (tn / (-(-tn // MXU_WIDTH) * MXU_WIDTH)) * (tk / (-(-tk // MXU_WIDTH) * MXU_WIDTH))
                mxu = 2.0 * M * N * K / (MXU_RATE * fill)
                acc = steps * tm * tn * 8 / ACC_RATE if nk > 1 else 0.0
                cost = max(hbm, mxu) + acc + steps * STEP_COST
                if best is None or cost < best[0]:
                    best = (cost, tm, tn, tk)
    assert best is not None, (M, N, K)
    return best[1:]


def matmul(a, b, *, name, ta=False, tb=False, out_dtype=F32, add=None, layer=None, b_cols=False, out_cols=False):
    M, K = (a.shape[1], a.shape[0]) if ta else a.shape
    rows, cols = b.shape[-2], (4 if b_cols else 1) * b.shape[-1]
    N, bk = (rows, cols) if tb else (cols, rows)
    assert K == bk, (a.shape, b.shape)
    n_unit = N // 4 if (out_cols or (b_cols and not tb)) else N
    k_unit = K // 4 if (b_cols and tb) else K
    sa, sb, so = a.dtype.itemsize, b.dtype.itemsize, jnp.dtype(out_dtype).itemsize
    sadd = add.dtype.itemsize if add is not None else 0
    tm, tn, tk = _choose_tiles(M, N, K, n_unit, k_unit, sa, sb, so, sadd, ta)
    nk = K // tk
    a_spec = pl.BlockSpec((tk, tm), lambda i, j, k: (k, i)) if ta else pl.BlockSpec((tm, tk), lambda i, j, k: (i, k))
    if b_cols and tb:
        per = k_unit // tk
        b_spec = pl.BlockSpec((None, tn, tk), lambda i, j, k: (k // per, j, k % per))
    elif b_cols:
        per = n_unit // tn
        b_spec = pl.BlockSpec((None, tk, tn), lambda i, j, k: (j // per, k, j % per))
    elif layer is not None:
        b_spec = (pl.BlockSpec((None, tn, tk), lambda i, j, k: (layer, j, k)) if tb
                  else pl.BlockSpec((None, tk, tn), lambda i, j, k: (layer, k, j)))
    else:
        b_spec = pl.BlockSpec((tn, tk), lambda i, j, k: (j, k)) if tb else pl.BlockSpec((tk, tn), lambda i, j, k: (k, j))
    if out_cols:
        per = n_unit // tn
        o_spec = pl.BlockSpec((None, tm, tn), lambda i, j, k: (j // per, i, j % per))
        out_shape = jax.ShapeDtypeStruct((4, M, N // 4), out_dtype)
    else:
        o_spec = pl.BlockSpec((tm, tn), lambda i, j, k: (i, j))
        out_shape = jax.ShapeDtypeStruct((M, N), out_dtype)
    dims = ((0 if ta else 1,), (1 if tb else 0,))
    has_add = add is not None

    def body(*refs):
        a_ref, b_ref = refs[0], refs[1]
        add_ref = refs[2] if has_add else None
        o_ref = refs[3] if has_add else refs[2]
        part = _dot(a_ref[...], b_ref[...], dims)

        def finish(total):
            if has_add:
                total = total + add_ref[...].astype(F32)
            o_ref[...] = total.astype(o_ref.dtype)

        if nk == 1:
            finish(part)
        else:
            acc_ref = refs[-1]
            k = pl.program_id(2)

            @pl.when(k == 0)
            def _():
                acc_ref[...] = part

            @pl.when(k > 0)
            def _():
                acc_ref[...] += part

            @pl.when(k == nk - 1)
            def _():
                finish(acc_ref[...])

    in_specs = [a_spec, b_spec] + ([o_spec] if has_add else [])
    args = (a, b) + ((add,) if has_add else ())
    return pl.pallas_call(
        body, name=name, grid=(M // tm, N // tn, nk), in_specs=in_specs, out_specs=o_spec, out_shape=out_shape,
        scratch_shapes=[pltpu.VMEM((tm, tn), F32)] if nk > 1 else [],
        compiler_params=_params("parallel", "parallel", "arbitrary"),
    )(*args)


def _row_tile(t):
    return _pick(t, (256, 128, 64, 32, 16, 8))


def rms_fwd(x, w, *, name):
    T, D = x.shape
    tr = _row_tile(T)

    def body(x_ref, w_ref, o_ref):
        xv = x_ref[...]
        r = lax.rsqrt(jnp.mean(xv * xv, axis=-1, keepdims=True) + EPS)
        o_ref[...] = (xv * r * w_ref[...]).astype(o_ref.dtype)

    row = pl.BlockSpec((tr, D), lambda i: (i, 0))
    return pl.pallas_call(
        body, name=name, grid=(T // tr,), in_specs=[row, pl.BlockSpec((1, D), lambda i: (0, 0))], out_specs=row,
        out_shape=jax.ShapeDtypeStruct((T, D), MXU), compiler_params=_params("parallel"),
    )(x, w)


def rms_bwd(x, w, dh, dres, *, name):
    T, D = x.shape
    tr = _row_tile(T)

    def body(x_ref, w_ref, dh_ref, dres_ref, dx_ref, dxm_ref, dw_ref):
        xv = x_ref[...]
        dhv = dh_ref[...].astype(F32)
        r = lax.rsqrt(jnp.mean(xv * xv, axis=-1, keepdims=True) + EPS)
        g = dhv * w_ref[...]
        dot = jnp.mean(g * xv, axis=-1, keepdims=True)
        dx = dres_ref[...] + r * g - xv * (r * r * r * dot)
        dx_ref[...] = dx
        dxm_ref[...] = dx.astype(dxm_ref.dtype)
        part = jnp.sum(dhv * xv * r, axis=0, keepdims=True)

        @pl.when(pl.program_id(0) == 0)
        def _():
            dw_ref[...] = part

        @pl.when(pl.program_id(0) > 0)
        def _():
            dw_ref[...] += part

    row = pl.BlockSpec((tr, D), lambda i: (i, 0))
    vec = pl.BlockSpec((1, D), lambda i: (0, 0))
    return pl.pallas_call(
        body, name=name, grid=(T // tr,), in_specs=[row, vec, row, row], out_specs=[row, row, vec],
        out_shape=[jax.ShapeDtypeStruct((T, D), F32), jax.ShapeDtypeStruct((T, D), MXU),
                   jax.ShapeDtypeStruct((1, D), F32)],
        compiler_params=_params("arbitrary"),
    )(x, w, dh, dres)


def final_loss(x, w, target, *, name):
    T, D = x.shape
    tr = _row_tile(T)

    def body(x_ref, w_ref, t_ref, loss_ref, dx_ref, dxm_ref, dw_ref):
        xv = x_ref[...]
        wv = w_ref[...]
        r = lax.rsqrt(jnp.mean(xv * xv, axis=-1, keepdims=True) + EPS)
        err = xv * r * wv - t_ref[...]
        lpart = 0.5 * jnp.sum(jnp.mean(err * err, axis=-1, keepdims=True), axis=0, keepdims=True)
        dy = err * (1.0 / D)
        g = dy * wv
        dot = jnp.mean(g * xv, axis=-1, keepdims=True)
        dx = r * g - xv * (r * r * r * dot)
        dx_ref[...] = dx
        dxm_ref[...] = dx.astype(dxm_ref.dtype)
        part = jnp.sum(dy * xv * r, axis=0, keepdims=True)

        @pl.when(pl.program_id(0) == 0)
        def _():
            dw_ref[...] = part
            loss_ref[...] = lpart

        @pl.when(pl.program_id(0) > 0)
        def _():
            dw_ref[...] += part
            loss_ref[...] += lpart

    row = pl.BlockSpec((tr, D), lambda i: (i, 0))
    vec = pl.BlockSpec((1, D), lambda i: (0, 0))
    one = pl.BlockSpec((1, 1), lambda i: (0, 0))
    return pl.pallas_call(
        body, name=name, grid=(T // tr,), in_specs=[row, vec, row], out_specs=[one, row, row, vec],
        out_shape=[jax.ShapeDtypeStruct((1, 1), F32), jax.ShapeDtypeStruct((T, D), F32),
                   jax.ShapeDtypeStruct((T, D), MXU), jax.ShapeDtypeStruct((1, D), F32)],
        compiler_params=_params("arbitrary"),
    )(x, w, target)


def swiglu_fwd(gu, *, name):
    T, F2 = gu.shape
    F = F2 // 2
    tr = _row_tile(T)

    def body(g_ref, u_ref, o_ref):
        g = g_ref[...].astype(F32)
        o_ref[...] = (g * _sigmoid(g) * u_ref[...].astype(F32)).astype(o_ref.dtype)

    return pl.pallas_call(
        body, name=name, grid=(T // tr,),
        in_specs=[pl.BlockSpec((tr, F), lambda i: (i, 0)), pl.BlockSpec((tr, F), lambda i: (i, 1))],
        out_specs=pl.BlockSpec((tr, F), lambda i: (i, 0)),
        out_shape=jax.ShapeDtypeStruct((T, F), MXU), compiler_params=_params("parallel"),
    )(gu, gu)


def swiglu_bwd(gu, dact, *, name):
    T, F2 = gu.shape
    F = F2 // 2
    tr = _row_tile(T)

    def body(g_ref, u_ref, d_ref, o_ref):
        g = g_ref[...].astype(F32)
        d = d_ref[...].astype(F32)
        s = _sigmoid(g)
        o_ref[:, :F] = (d * u_ref[...].astype(F32) * s * (1.0 + g * (1.0 - s))).astype(o_ref.dtype)
        o_ref[:, F:] = (d * g * s).astype(o_ref.dtype)

    lo = pl.BlockSpec((tr, F), lambda i: (i, 0))
    hi = pl.BlockSpec((tr, F), lambda i: (i, 1))
    return pl.pallas_call(
        body, name=name, grid=(T // tr,), in_specs=[lo, hi, lo], out_specs=pl.BlockSpec((tr, F2), lambda i: (i, 0)),
        out_shape=jax.ShapeDtypeStruct((T, F2), MXU), compiler_params=_params("parallel"),
    )(gu, gu, dact)


def merge_fwd(gl, abr, pbr, sbr, *, name):
    T, D = abr.shape
    tr = _row_tile(T)

    def body(g0, g1, g2, a_ref, p_ref, s_ref, o_ref):
        m = sum(_sigmoid(g[...].astype(F32)) * b[...].astype(F32) for g, b in ((g0, a_ref), (g1, p_ref), (g2, s_ref)))
        o_ref[...] = m.astype(o_ref.dtype)

    row = pl.BlockSpec((tr, D), lambda i: (i, 0))
    gs = [pl.BlockSpec((tr, D), lambda i, j=j: (i, j)) for j in range(3)]
    return pl.pallas_call(
        body, name=name, grid=(T // tr,), in_specs=gs + [row, row, row], out_specs=row,
        out_shape=jax.ShapeDtypeStruct((T, D), MXU), compiler_params=_params("parallel"),
    )(gl, gl, gl, abr, pbr, sbr)


def merge_bwd(gl, abr, pbr, sbr, dm, *, name):
    T, D = abr.shape
    tr = _row_tile(T)

    def body(g0, g1, g2, a_ref, p_ref, s_ref, dm_ref, da_ref, dp_ref, ds_ref, dg_ref):
        d = dm_ref[...].astype(F32)
        for j, (g_ref, b_ref, db_ref) in enumerate(((g0, a_ref, da_ref), (g1, p_ref, dp_ref), (g2, s_ref, ds_ref))):
            s = _sigmoid(g_ref[...].astype(F32))
            db_ref[...] = (d * s).astype(db_ref.dtype)
            dg_ref[:, j * D:(j + 1) * D] = (d * b_ref[...].astype(F32) * s * (1.0 - s)).astype(dg_ref.dtype)

    row = pl.BlockSpec((tr, D), lambda i: (i, 0))
    gs = [pl.BlockSpec((tr, D), lambda i, j=j: (i, j)) for j in range(3)]
    return pl.pallas_call(
        body, name=name, grid=(T // tr,), in_specs=gs + [row] * 4,
        out_specs=[row] * 3 + [pl.BlockSpec((tr, 3 * D), lambda i: (i, 0))],
        out_shape=[jax.ShapeDtypeStruct((T, D), MXU)] * 3 + [jax.ShapeDtypeStruct((T, 3 * D), MXU)],
        compiler_params=_params("parallel"),
    )(gl, gl, gl, abr, pbr, sbr, dm)


def gnorm_fwd(y, z, w, *, name):
    T, DI = y.shape
    gw = DI // C.G
    tr = _row_tile(T)

    def body(y_ref, z_ref, w_ref, o_ref):
        for g in range(C.G):
            sl = slice(g * gw, (g + 1) * gw)
            zz = z_ref[:, sl]
            v = y_ref[:, sl] * (zz * _sigmoid(zz))
            r = lax.rsqrt(jnp.mean(v * v, axis=-1, keepdims=True) + EPS)
            o_ref[:, sl] = (v * r * w_ref[:, sl]).astype(o_ref.dtype)

    row = pl.BlockSpec((tr, DI), lambda i: (i, 0))
    return pl.pallas_call(
        body, name=name, grid=(T // tr,), in_specs=[row, row, pl.BlockSpec((1, DI), lambda i: (0, 0))],
        out_specs=row, out_shape=jax.ShapeDtypeStruct((T, DI), MXU), compiler_params=_params("parallel"),
    )(y, z, w)


def gnorm_bwd(y, z, w, do, *, name):
    T, DI = y.shape
    gw = DI // C.G
    tr = _row_tile(T)

    def body(y_ref, z_ref, w_ref, do_ref, dy_ref, dz_ref, dw_ref):
        first = pl.program_id(0) == 0
        for g in range(C.G):
            sl = slice(g * gw, (g + 1) * gw)
            zz = z_ref[:, sl]
            yy = y_ref[:, sl]
            s = _sigmoid(zz)
            sz = zz * s
            v = yy * sz
            r = lax.rsqrt(jnp.mean(v * v, axis=-1, keepdims=True) + EPS)
            dov = do_ref[:, sl].astype(F32)
            gg = dov * w_ref[:, sl]
            dot = jnp.mean(gg * v, axis=-1, keepdims=True)
            dv = r * gg - v * (r * r * r * dot)
            dy_ref[:, sl] = dv * sz
            dz_ref[:, sl] = (dv * yy * s * (1.0 + zz * (1.0 - s))).astype(dz_ref.dtype)
            part = jnp.sum(dov * v * r, axis=0, keepdims=True)

            @pl.when(first)
            def _():
                dw_ref[:, sl] = part

            @pl.when(jnp.logical_not(first))
            def _():
                dw_ref[:, sl] += part

    row = pl.BlockSpec((tr, DI), lambda i: (i, 0))
    vec = pl.BlockSpec((1, DI), lambda i: (0, 0))
    return pl.pallas_call(
        body, name=name, grid=(T // tr,), in_specs=[row, row, vec, row], out_specs=[row, row, vec],
        out_shape=[jax.ShapeDtypeStruct((T, DI), F32), jax.ShapeDtypeStruct((T, DI), MXU),
                   jax.ShapeDtypeStruct((1, DI), F32)],
        compiler_params=_params("arbitrary"),
    )(y, z, w, do)


def adamw(w, g, m, v, *, name):
    shape = w.shape
    cols = shape[-1]
    rows = w.size // cols
    w2, g2, m2, v2 = (t.reshape(rows, cols) for t in (w, g, m, v))
    tr = rows if rows * cols * 4 <= (2 << 20) else _pick(rows, (512, 256, 128, 64, 32, 16, 8))
    while tr * cols * 4 > (2 << 20) and tr % 16 == 0:
        tr //= 2
    c1 = 1.0 - ADAM_B1 ** ADAM_STEP
    c2 = 1.0 - ADAM_B2 ** ADAM_STEP

    def body(w_ref, g_ref, m_ref, v_ref, d_ref, nm_ref, nv_ref):
        gv = g_ref[...]
        nm = ADAM_B1 * m_ref[...] + (1.0 - ADAM_B1) * gv
        nv = ADAM_B2 * v_ref[...] + (1.0 - ADAM_B2) * (gv * gv)
        d_ref[...] = -ADAM_LR * ((nm / c1) / (jnp.sqrt(nv / c2) + ADAM_EPS) + ADAM_WD * w_ref[...])
        nm_ref[...] = nm
        nv_ref[...] = nv

    row = pl.BlockSpec((tr, cols), lambda i: (i, 0))
    outs = pl.pallas_call(
        body, name=name, grid=(rows // tr,), in_specs=[row] * 4, out_specs=[row] * 3,
        out_shape=[jax.ShapeDtypeStruct((rows, cols), F32)] * 3, compiler_params=_params("parallel"),
    )(w2, g2, m2, v2)
    return tuple(o.reshape(shape) for o in outs)


def adamw_layers(w, g_layers, m, v, *, name):
    shape = w.shape
    L = shape[0]
    rows, cols = g_layers[0].shape
    w3, m3, v3 = (t.reshape(L, rows, cols) for t in (w, m, v))
    tr = _pick(rows, (512, 256, 128, 64, 32, 16, 8))
    while tr * cols * 4 > (2 << 20) and tr % 16 == 0:
        tr //= 2
    c1 = 1.0 - ADAM_B1 ** ADAM_STEP
    c2 = 1.0 - ADAM_B2 ** ADAM_STEP

    def body(*refs):
        w_ref, m_ref, v_ref = refs[0], refs[1], refs[2]
        g_refs = refs[3:3 + L]
        go_ref, d_ref, nm_ref, nv_ref = refs[3 + L:]
        layer = pl.program_id(0)
        gv = g_refs[0][...]
        for k in range(1, L):
            gv = jnp.where(layer == k, g_refs[k][...], gv)
        nm = ADAM_B1 * m_ref[...] + (1.0 - ADAM_B1) * gv
        nv = ADAM_B2 * v_ref[...] + (1.0 - ADAM_B2) * (gv * gv)
        d_ref[...] = -ADAM_LR * ((nm / c1) / (jnp.sqrt(nv / c2) + ADAM_EPS) + ADAM_WD * w_ref[...])
        go_ref[...] = gv
        nm_ref[...] = nm
        nv_ref[...] = nv

    blk = pl.BlockSpec((None, tr, cols), lambda l, i: (l, i, 0))
    gblks = [pl.BlockSpec((tr, cols), lambda l, i, k=k: (jnp.where(l == k, i, 0), 0)) for k in range(L)]
    outs = pl.pallas_call(
        body, name=name, grid=(L, rows // tr), in_specs=[blk] * 3 + gblks, out_specs=[blk] * 4,
        out_shape=[jax.ShapeDtypeStruct((L, rows, cols), F32)] * 4, compiler_params=_params("parallel", "parallel"),
    )(w3, m3, v3, *g_layers)
    return tuple(o.reshape(shape) for o in outs)


def _attn_masks(i, heads):
    B = C.blk
    row = lax.broadcasted_iota(jnp.int32, (heads * B, 2 * B), 0) & (B - 1)
    col = lax.broadcasted_iota(jnp.int32, (heads * B, 2 * B), 1)
    diff = row + B - col
    return (diff >= 0) & (diff < B) & ((col >= B) | (i > 0))


def _stack_heads(ref, kvh, ppk, lo):
    parts = []
    for pr in range(ppk):
        pair = kvh * ppk + pr
        qp = ref[:, pair * 128:(pair + 1) * 128].astype(F32)
        parts += [jnp.where(lo, qp, 0.0), jnp.where(lo, 0.0, qp)]
    return jnp.concatenate(parts, axis=0)


def _unstack_heads(ref, val, kvh, ppk, lo):
    B = C.blk
    for pr in range(ppk):
        pair = kvh * ppk + pr
        ref[:, pair * 128:(pair + 1) * 128] = jnp.where(lo, val[2 * pr * B:(2 * pr + 1) * B],
                                                        val[(2 * pr + 1) * B:(2 * pr + 2) * B]).astype(ref.dtype)


def _sink_scores(sink_ref, kvh, ppk):
    B = C.blk
    h0 = kvh * 2 * ppk
    return jnp.concatenate([jnp.full((B, 2 * B), sink_ref[0, h], F32) for h in range(h0, h0 + 2 * ppk)], axis=0)


def _kv2(prev_ref, cur_ref, m, lo):
    sl = slice(m * 128, (m + 1) * 128)
    slab = jnp.concatenate([prev_ref[:, sl], cur_ref[:, sl]], axis=0).astype(F32)
    slab = jnp.where(lax.broadcasted_iota(jnp.int32, (slab.shape[0], 1), 0) == 0, 0.0, slab)
    rolled = pltpu.roll(slab, 64, axis=1)
    return jnp.where(lo, slab, rolled), jnp.where(lo, rolled, slab)


def _attn_weights(q4, k2, mask, sink_scores):
    s = _dot_nt(q4, k2) * (C.hd ** -0.5)
    col0 = lax.broadcasted_iota(jnp.int32, (1, s.shape[1]), 1) == 0
    s = jnp.where(col0, sink_scores, jnp.where(mask, s, NEG))
    return jnp.exp(s - jnp.max(s, axis=-1, keepdims=True))


def _split_sums(v, sel):
    hi = v.astype(jnp.bfloat16)
    lo = (v - hi.astype(F32)).astype(jnp.bfloat16)
    sel2 = jnp.concatenate([sel, sel], axis=0).astype(jnp.bfloat16)
    return lax.dot_general(jnp.concatenate([hi, lo], axis=1), sel2, (((1,), (0,)), ((), ())), preferred_element_type=F32)


def _row_sums(p, exact=False):
    ones = jnp.ones((p.shape[1], 128), F32)
    return _split_sums(p, ones) if exact else _dot_nn(p, ones)


def attn_fwd(qkv, sink, *, name):
    T = qkv.shape[0]
    B, AW, KW = C.blk, C.AW, C.KW
    assert C.hd == 64 and KW % 128 == 0 and (C.qh // C.kvh) % 2 == 0
    nb = T // B
    kb = AW // KW
    ppk = C.qh // C.kvh // 2

    def body(q_ref, kc_ref, kp_ref, vc_ref, vp_ref, sink_ref, o_ref):
        i = pl.program_id(0)
        lo = lax.broadcasted_iota(jnp.int32, (1, 128), 1) < 64
        mask = _attn_masks(i, 2 * ppk)
        for m in range(KW // 128):
            k2s = _kv2(kp_ref, kc_ref, m, lo)
            v2s = _kv2(vp_ref, vc_ref, m, lo)
            for par in range(2):
                kvh = 2 * m + par
                q4 = _stack_heads(q_ref, kvh, ppk, lo)
                p = _attn_weights(q4, k2s[par], mask, _sink_scores(sink_ref, kvh, ppk))
                _unstack_heads(o_ref, _dot_nn(p, v2s[par]) / _row_sums(p), kvh, ppk, lo)

    prev = lambda i: jnp.maximum(i - 1, 0)
    return pl.pallas_call(
        body, name=name, grid=(nb,),
        in_specs=[pl.BlockSpec((B, AW), lambda i: (i, 0)),
                  pl.BlockSpec((B, KW), lambda i: (i, kb)), pl.BlockSpec((B, KW), lambda i: (prev(i), kb)),
                  pl.BlockSpec((B, KW), lambda i: (i, kb + 1)), pl.BlockSpec((B, KW), lambda i: (prev(i), kb + 1)),
                  pl.BlockSpec(memory_space=pltpu.SMEM)],
        out_specs=pl.BlockSpec((B, AW), lambda i: (i, 0)),
        out_shape=jax.ShapeDtypeStruct((T, AW), MXU), compiler_params=_params("parallel"),
    )(qkv, qkv, qkv, qkv, qkv, sink)


def attn_bwd(qkv, sink, dout, *, name):
    T = qkv.shape[0]
    B, AW, KW = C.blk, C.AW, C.KW
    nb = T // B
    kb = AW // KW
    ppk = C.qh // C.kvh // 2
    scale = C.hd ** -0.5

    def body(q_ref, kc_ref, kp_ref, vc_ref, vp_ref, sink_ref, do_ref, dq_ref, dk_ref, dv_ref, ds_ref, ck_ref, cv_ref):
        i = pl.program_id(0)

        @pl.when(i == 0)
        def _():
            ck_ref[...] = jnp.zeros_like(ck_ref)
            cv_ref[...] = jnp.zeros_like(cv_ref)
            ds_ref[...] = jnp.zeros_like(ds_ref)

        @pl.when(i < nb)
        def _():
            lane = lax.broadcasted_iota(jnp.int32, (1, 128), 1)
            lo = lane < 64
            hlane = lax.broadcasted_iota(jnp.int32, (1, C.qh), 1)
            mask = _attn_masks(i, 2 * ppk)
            dsink = jnp.zeros((1, C.qh), F32)
            for m in range(KW // 128):
                ksl = slice(m * 128, (m + 1) * 128)
                k2s = _kv2(kp_ref, kc_ref, m, lo)
                v2s = _kv2(vp_ref, vc_ref, m, lo)
                folded = []
                for par in range(2):
                    kvh = 2 * m + par
                    q4 = _stack_heads(q_ref, kvh, ppk, lo)
                    do4 = _stack_heads(do_ref, kvh, ppk, lo)
                    p = _attn_weights(q4, k2s[par], mask, _sink_scores(sink_ref, kvh, ppk))
                    inv = 1.0 / _row_sums(p)
                    p = p * jnp.concatenate([inv, inv], axis=1)
                    dp = _dot_nt(do4, v2s[par])
                    delta = _row_sums(p * dp, exact=True)
                    dsc = p * (dp - jnp.concatenate([delta, delta], axis=1))
                    _unstack_heads(dq_ref, _dot_nn(dsc, k2s[par]) * scale, kvh, ppk, lo)
                    dk2 = _dot_tn(dsc, q4) * scale
                    dv2 = _dot_tn(p, do4)
                    for k in range(2 * ppk):
                        dsh = jnp.sum(dsc[k * B:(k + 1) * B, :128], axis=0, keepdims=True)[:, :1]
                        dsink = dsink + jnp.where(hlane == kvh * 2 * ppk + k, dsh, 0.0)
                    folded.append((dk2 + pltpu.roll(dk2, 64, axis=1), dv2 + pltpu.roll(dv2, 64, axis=1)))
                row0 = lax.broadcasted_iota(jnp.int32, (2 * B, 1), 0) == 0
                dks = jnp.where(row0, 0.0, jnp.where(lo, folded[0][0], folded[1][0]))
                dvs = jnp.where(row0, 0.0, jnp.where(lo, folded[0][1], folded[1][1]))
                dk_ref[:, ksl] = (ck_ref[:, ksl] + dks[:B]).astype(dk_ref.dtype)
                dv_ref[:, ksl] = (cv_ref[:, ksl] + dvs[:B]).astype(dv_ref.dtype)
                ck_ref[:, ksl] = dks[B:]
                cv_ref[:, ksl] = dvs[B:]
            ds_ref[...] += dsink

        @pl.when(i == nb)
        def _():
            dk_ref[...] = ck_ref[...].astype(dk_ref.dtype)
            dv_ref[...] = cv_ref[...].astype(dv_ref.dtype)

    cur = lambda i: jnp.minimum(i, nb - 1)
    prev = lambda i: jnp.maximum(jnp.minimum(i, nb - 1) - 1, 0)
    out = lambda i: jnp.maximum(i - 1, 0)
    return pl.pallas_call(
        body, name=name, grid=(nb + 1,),
        in_specs=[pl.BlockSpec((B, AW), lambda i: (cur(i), 0)),
                  pl.BlockSpec((B, KW), lambda i: (cur(i), kb)), pl.BlockSpec((B, KW), lambda i: (prev(i), kb)),
                  pl.BlockSpec((B, KW), lambda i: (cur(i), kb + 1)), pl.BlockSpec((B, KW), lambda i: (prev(i), kb + 1)),
                  pl.BlockSpec(memory_space=pltpu.SMEM),
                  pl.BlockSpec((B, AW), lambda i: (cur(i), 0))],
        out_specs=[pl.BlockSpec((B, AW), lambda i: (cur(i), 0)),
                   pl.BlockSpec((B, KW), lambda i: (out(i), 0)), pl.BlockSpec((B, KW), lambda i: (out(i), 0)),
                   pl.BlockSpec((1, C.qh), lambda i: (0, 0))],
        out_shape=[jax.ShapeDtypeStruct((T, AW), MXU), jax.ShapeDtypeStruct((T, KW), MXU),
                   jax.ShapeDtypeStruct((T, KW), MXU), jax.ShapeDtypeStruct((1, C.qh), F32)],
        scratch_shapes=[pltpu.VMEM((B, KW), F32), pltpu.VMEM((B, KW), F32)],
        compiler_params=_params("arbitrary"),
    )(qkv, qkv, qkv, qkv, qkv, sink, dout)


POOL_HALO = 16


def _window_sum(e, w, n, forward):
    s, k = e, 1
    while k < w:
        s = s + pltpu.roll(s, (n - k) if forward else k, axis=0)
        k *= 2
    return s


def pool_fwd(u, pw, ps, *, name):
    T, PW = u.shape
    PG = C.PG
    tT = _pick(T, (256, 128))
    hb = tT // POOL_HALO

    def body(u_ref, uh_ref, pw_ref, ps_ref, o_ref, mx_ref):
        i = pl.program_id(0)
        halo = jnp.where(i > 0, uh_ref[...], 0.0)
        ext = jnp.concatenate([halo, u_ref[...]], axis=0)
        t = i * tT + lax.broadcasted_iota(jnp.int32, (tT, 1), 0)
        for g, w in enumerate(POOL_WINDOWS):
            sl = slice(g * PG, (g + 1) * PG)
            s = _window_sum(ext[:, sl], w, tT + POOL_HALO, False)[POOL_HALO:]
            cnt = jnp.minimum(t + 1, w).astype(F32)
            mixed = (s / cnt - u_ref[:, sl]).astype(MXU)
            mx_ref[:, sl] = mixed.astype(mx_ref.dtype)
            o_ref[:, sl] = (_dot_nn(mixed, pw_ref[g]) * ps_ref[:, sl]).astype(o_ref.dtype)

    row = pl.BlockSpec((tT, PW), lambda i: (i, 0))
    return pl.pallas_call(
        body, name=name, grid=(T // tT,),
        in_specs=[row, pl.BlockSpec((POOL_HALO, PW), lambda i: (jnp.maximum(i * hb - 1, 0), 0)),
                  pl.BlockSpec((4, PG, PG), lambda i: (0, 0, 0)), pl.BlockSpec((1, PW), lambda i: (0, 0))],
        out_specs=[row, row], out_shape=[jax.ShapeDtypeStruct((T, PW), MXU)] * 2,
        compiler_params=_params("parallel"),
    )(u, u, pw, ps)


def pool_bwd(dpool, mixed, pw, ps, *, name):
    T, PW = dpool.shape
    PG = C.PG
    tT = _pick(T, (256, 128))
    hb = tT // POOL_HALO
    n = T // tT
    rows = tT + POOL_HALO

    def body(dp_ref, dph_ref, mx_ref, pw_ref, ps_ref, du_ref, dpw_ref, dps_ref):
        i = pl.program_id(0)

        @pl.when(i == 0)
        def _():
            dpw_ref[...] = jnp.zeros_like(dpw_ref)
            dps_ref[...] = jnp.zeros_like(dps_ref)

        halo = jnp.where(i < n - 1, dph_ref[...].astype(F32), 0.0)
        dext = jnp.concatenate([dp_ref[...].astype(F32), halo], axis=0)
        t = i * tT + lax.broadcasted_iota(jnp.int32, (rows, 1), 0)
        for g, w in enumerate(POOL_WINDOWS):
            sl = slice(g * PG, (g + 1) * PG)
            dyg = dext[:, sl] * ps_ref[:, sl]
            dmix = _dot_nt(dyg, pw_ref[g])
            cnt = jnp.minimum(t + 1, w).astype(F32)
            s = _window_sum(dmix / cnt, w, rows, True)
            du_ref[:, sl] = (s[:tT] - dmix[:tT]).astype(du_ref.dtype)
            mb = mx_ref[:, sl]
            dpw_ref[g] += _dot_tn(mb, dyg[:tT])
            dps_ref[:, sl] += jnp.sum(dext[:tT, sl] * _dot_nn(mb, pw_ref[g]), axis=0, keepdims=True)

    row = pl.BlockSpec((tT, PW), lambda i: (i, 0))
    return pl.pallas_call(
        body, name=name, grid=(n,),
        in_specs=[row, pl.BlockSpec((POOL_HALO, PW), lambda i: (jnp.minimum((i + 1) * hb, T // POOL_HALO - 1), 0)), row,
                  pl.BlockSpec((4, PG, PG), lambda i: (0, 0, 0)), pl.BlockSpec((1, PW), lambda i: (0, 0))],
        out_specs=[row, pl.BlockSpec((4, PG, PG), lambda i: (0, 0, 0)), pl.BlockSpec((1, PW), lambda i: (0, 0))],
        out_shape=[jax.ShapeDtypeStruct((T, PW), MXU), jax.ShapeDtypeStruct((4, PG, PG), F32),
                   jax.ShapeDtypeStruct((1, PW), F32)],
        compiler_params=_params("arbitrary"),
    )(dpool, dpool, mixed, pw, ps)


CONV_HALO = 8
CONV_K = 4
CONV_TAP_ROWS = 16


def _conv(x, cw_ref, cb_ref, sl):
    acc = cb_ref[:, sl] + cw_ref[CONV_K - 1:CONV_K, sl] * x
    for k in range(CONV_K - 1):
        acc = acc + cw_ref[k:k + 1, sl] * pltpu.roll(x, CONV_K - 1 - k, axis=0)
    return acc


def conv_fwd(xbc, cw, cb, *, name):
    T, CC = xbc.shape
    tT = _pick(T, (256, 128))
    hb = tT // CONV_HALO
    cs = _pick(CC, (512, 256, 128))

    def body(x_ref, xh_ref, cw_ref, cb_ref, o_ref):
        i = pl.program_id(0)
        for c0 in range(0, CC, cs):
            sl = slice(c0, c0 + cs)
            ext = jnp.concatenate([jnp.where(i > 0, xh_ref[:, sl], 0.0), x_ref[:, sl]], axis=0)
            xc = _conv(ext, cw_ref, cb_ref, sl)[CONV_HALO:]
            o_ref[:, sl] = xc * _sigmoid(xc)

    row = pl.BlockSpec((tT, CC), lambda i: (i, 0))
    return pl.pallas_call(
        body, name=name, grid=(T // tT,),
        in_specs=[row, pl.BlockSpec((CONV_HALO, CC), lambda i: (jnp.maximum(i * hb - 1, 0), 0)),
                  pl.BlockSpec((CONV_K, CC), lambda i: (0, 0)), pl.BlockSpec((1, CC), lambda i: (0, 0))],
        out_specs=row, out_shape=jax.ShapeDtypeStruct((T, CC), F32), compiler_params=_params("parallel"),
    )(xbc, xbc, cw, cb)


def conv_bwd(xbc, cw, cb, dxa, *, name):
    T, CC = xbc.shape
    tT = _pick(T, (256, 128))
    hb = tT // CONV_HALO
    n = T // tT
    rows = tT + 2 * CONV_HALO
    cs = _pick(CC, (512, 256, 128))

    def body(x_ref, xp_ref, xn_ref, cw_ref, cb_ref, d_ref, dn_ref, dx_ref, dcw_ref, dcb_ref):
        i = pl.program_id(0)

        @pl.when(i == 0)
        def _():
            dcw_ref[...] = jnp.zeros_like(dcw_ref)
            dcb_ref[...] = jnp.zeros_like(dcb_ref)

        r = lax.broadcasted_iota(jnp.int32, (rows, 1), 0)
        own = (r >= CONV_HALO) & (r < tT + CONV_HALO)
        for c0 in range(0, CC, cs):
            sl = slice(c0, c0 + cs)
            x = jnp.concatenate([jnp.where(i > 0, xp_ref[:, sl], 0.0), x_ref[:, sl],
                                 jnp.where(i < n - 1, xn_ref[:, sl], 0.0)], axis=0)
            da = jnp.concatenate([jnp.zeros((CONV_HALO, cs), F32), d_ref[:, sl],
                                  jnp.where(i < n - 1, dn_ref[:, sl], 0.0)], axis=0)
            xc = _conv(x, cw_ref, cb_ref, sl)
            sg = _sigmoid(xc)
            dxc = da * sg * (1.0 + xc * (1.0 - sg))
            acc = cw_ref[CONV_K - 1:CONV_K, sl] * dxc
            for k in range(CONV_K - 1):
                acc = acc + cw_ref[k:k + 1, sl] * pltpu.roll(dxc, rows - (CONV_K - 1 - k), axis=0)
            dx_ref[:, sl] = acc[CONV_HALO:tT + CONV_HALO].astype(dx_ref.dtype)
            down = jnp.where(own, dxc, 0.0)
            dcb_ref[:, sl] += jnp.sum(down, axis=0, keepdims=True)
            dcw_ref[CONV_K - 1:CONV_K, sl] += jnp.sum(down * x, axis=0, keepdims=True)
            for k in range(CONV_K - 1):
                dcw_ref[k:k + 1, sl] += jnp.sum(down * pltpu.roll(x, CONV_K - 1 - k, axis=0), axis=0, keepdims=True)

    row = pl.BlockSpec((tT, CC), lambda i: (i, 0))
    prev = pl.BlockSpec((CONV_HALO, CC), lambda i: (jnp.maximum(i * hb - 1, 0), 0))
    nxt = pl.BlockSpec((CONV_HALO, CC), lambda i: (jnp.minimum((i + 1) * hb, T // CONV_HALO - 1), 0))
    return pl.pallas_call(
        body, name=name, grid=(n,),
        in_specs=[row, prev, nxt, pl.BlockSpec((CONV_K, CC), lambda i: (0, 0)), pl.BlockSpec((1, CC), lambda i: (0, 0)),
                  row, nxt],
        out_specs=[row, pl.BlockSpec((CONV_K, CC), lambda i: (0, 0)), pl.BlockSpec((1, CC), lambda i: (0, 0))],
        out_shape=[jax.ShapeDtypeStruct((T, CC), MXU), jax.ShapeDtypeStruct((CONV_K, CC), F32),
                   jax.ShapeDtypeStruct((1, CC), F32)],
        compiler_params=_params("arbitrary"),
    )(xbc, xbc, xbc, cw, cb, dxa, dxa)


def _softplus(x):
    return jnp.maximum(x, 0.0) + jnp.log(1.0 + jnp.exp(-jnp.abs(x)))


def _dot_exact(a, b):
    return lax.dot_general(a, b, (((1,), (0,)), ((), ())), precision=lax.Precision.HIGHEST, preferred_element_type=F32)


def ssd_prep(raw_t, bias, alog, *, name):
    H, T = raw_t.shape
    B = C.blk
    tc = _pick(T, (4 * B, 2 * B, B))

    def body(r_ref, b_ref, al_ref, dt_ref, acs_ref):
        dt = _softplus(r_ref[...] + b_ref[...])
        dt_ref[...] = dt
        dta = dt * (-jnp.exp(al_ref[...]))
        upper = (lax.broadcasted_iota(jnp.int32, (B, B), 0) <= lax.broadcasted_iota(jnp.int32, (B, B), 1)).astype(F32)
        for j in range(tc // B):
            acs_ref[:, j * B:(j + 1) * B] = _dot_exact(dta[:, j * B:(j + 1) * B], upper)

    blk = pl.BlockSpec((H, tc), lambda i: (0, i))
    vec = pl.BlockSpec((H, 1), lambda i: (0, 0))
    return pl.pallas_call(
        body, name=name, grid=(T // tc,), in_specs=[blk, vec, vec], out_specs=[blk, blk],
        out_shape=[jax.ShapeDtypeStruct((H, T), F32)] * 2, compiler_params=_params("parallel"),
    )(raw_t, bias, alog)


def _pair(lo, arr, h0, rows=slice(None)):
    return jnp.where(lo, arr[rows, h0:h0 + 1], arr[rows, h0 + 1:h0 + 2])


def _decay(acs, acs_t, h, causal):
    return jnp.exp(jnp.where(causal, acs[:, h:h + 1] - acs_t[h:h + 1, :], NEG))


def ssd_fwd(xa, dt, acs, acs_t, dskip, *, name):
    T = xa.shape[0]
    B, DI, G, N, HG, P, H = C.blk, C.DI, C.G, C.N, C.HG, C.P, C.H
    assert P == 64 and HG % 2 == 0
    nc = T // B
    W = HG * P

    def body(xa_ref, dt_ref, acs_ref, acst_ref, ds_ref, y_ref, hp_ref, h_scr):
        @pl.when(pl.program_id(0) == 0)
        def _():
            h_scr[...] = jnp.zeros_like(h_scr)

        lo = lax.broadcasted_iota(jnp.int32, (1, 128), 1) < 64
        causal = lax.broadcasted_iota(jnp.int32, (B, B), 0) >= lax.broadcasted_iota(jnp.int32, (B, B), 1)
        dt, acs, acs_t, dsk = dt_ref[...], acs_ref[...], acst_ref[...], ds_ref[...]
        for g in range(G):
            bg = xa_ref[:, DI + g * N:DI + (g + 1) * N]
            cg = xa_ref[:, DI + (G + g) * N:DI + (G + g + 1) * N]
            cb = _dot_nt(cg, bg)
            hg = h_scr[g]
            hp_ref[0, g * N:(g + 1) * N, :] = hg
            yoff = _dot_nn(cg, hg)
            xws, decs = [], []
            for j in range(HG // 2):
                h0 = g * HG + 2 * j
                xsl = slice(h0 * P, (h0 + 2) * P)
                xp = xa_ref[:, xsl]
                ap = _pair(lo, acs, h0)
                alast = _pair(lo, acs, h0, slice(B - 1, B))
                xdt = xp * _pair(lo, dt, h0)
                ys = [_dot_nn(cb * _decay(acs, acs_t, h0 + half, causal), xdt) for half in range(2)]
                y_ref[:, xsl] = (jnp.where(lo, ys[0], ys[1]) + yoff[:, 2 * j * P:(2 * j + 2) * P] * jnp.exp(ap)
                                 + _pair(lo, dsk, h0) * xp)
                xws.append(xdt * jnp.exp(alast - ap))
                decs.append(jnp.exp(alast))
            h_scr[g] = hg * jnp.concatenate(decs, axis=1) + _dot_tn(bg, jnp.concatenate(xws, axis=1))

    tok = lambda w: pl.BlockSpec((B, w), lambda c: (c, 0))
    return pl.pallas_call(
        body, name=name, grid=(nc,),
        in_specs=[tok(C.CC), tok(H), tok(H), pl.BlockSpec((H, B), lambda c: (0, c)), pl.BlockSpec((1, H), lambda c: (0, 0))],
        out_specs=[tok(DI), pl.BlockSpec((1, G * N, W), lambda c: (c, 0, 0))],
        out_shape=[jax.ShapeDtypeStruct((T, DI), F32), jax.ShapeDtypeStruct((nc, G * N, W), F32)],
        scratch_shapes=[pltpu.VMEM((G, N, W), F32)],
        compiler_params=_params("arbitrary"),
    )(xa, dt, acs, acs_t, dskip)


def ssd_bwd(xa, dt, acs, acs_t, dskip, hprev, dy, *, name):
    T = xa.shape[0]
    B, DI, G, N, HG, P, H = C.blk, C.DI, C.G, C.N, C.HG, C.P, C.H
    nc = T // B
    W = HG * P

    def body(xa_ref, dt_ref, acs_ref, acst_ref, ds_ref, hp_ref, dy_ref, dxa_ref, ddt_ref, dacs_ref, dd_ref, dh_scr):
        @pl.when(pl.program_id(0) == 0)
        def _():
            dh_scr[...] = jnp.zeros_like(dh_scr)
            dd_ref[...] = jnp.zeros_like(dd_ref)

        lo = lax.broadcasted_iota(jnp.int32, (1, 128), 1) < 64
        hi = jnp.logical_not(lo)
        causal = lax.broadcasted_iota(jnp.int32, (B, B), 0) >= lax.broadcasted_iota(jnp.int32, (B, B), 1)
        hlane = lax.broadcasted_iota(jnp.int32, (1, H), 1)
        hsub = lax.broadcasted_iota(jnp.int32, (B, 1), 0)
        lastrow = lax.broadcasted_iota(jnp.int32, (B, 1), 0) == B - 1
        dt, acs, acs_t, dsk = dt_ref[...], acs_ref[...], acst_ref[...], ds_ref[...]
        d_acs = jnp.zeros((B, H), F32)
        d_acs_t = jnp.zeros((B, B), F32)
        d_dt = jnp.zeros((B, H), F32)
        d_d = jnp.zeros((1, H), F32)

        lane_r = lax.broadcasted_iota(jnp.int32, (128, 128), 0)
        lane_c = lax.broadcasted_iota(jnp.int32, (128, 128), 1)
        same_half = ((lane_r < 64) == (lane_c < 64)).astype(F32)
        ones_sq = jnp.ones((128, 128), F32)

        def rsum(v):
            r = _split_sums(v, same_half)
            return r[:, 0:1], r[:, 64:65]

        for g in range(G):
            bsl = slice(DI + g * N, DI + (g + 1) * N)
            csl = slice(DI + (G + g) * N, DI + (G + g + 1) * N)
            bg, cg = xa_ref[:, bsl], xa_ref[:, csl]
            cb = _dot_nt(cg, bg)
            hg = hp_ref[0, g * N:(g + 1) * N, :]
            dhg = dh_scr[g]
            yoff = _dot_nn(cg, hg)
            bds = _dot_nn(bg, dhg)
            d_cb = jnp.zeros((B, B), F32)
            dyes, xws, decs = [], [], []
            for j in range(HG // 2):
                h0 = g * HG + 2 * j
                xsl = slice(h0 * P, (h0 + 2) * P)
                psl = slice(2 * j * P, (2 * j + 2) * P)
                xp, dyp = xa_ref[:, xsl], dy_ref[:, xsl]
                dtp = _pair(lo, dt, h0)
                ap = _pair(lo, acs, h0)
                alast = _pair(lo, acs, h0, slice(B - 1, B))
                ea, ew, el = jnp.exp(ap), jnp.exp(alast - ap), jnp.exp(alast)
                xdt = xp * dtp
                halves = []
                for half in range(2):
                    h = h0 + half
                    lm = _decay(acs, acs_t, h, causal)
                    m = cb * lm
                    d_m = _dot_nt(jnp.where(lo if half == 0 else hi, dyp, 0.0), xdt)
                    d_cb = d_cb + d_m * lm
                    wgt = d_m * m
                    d_acs = d_acs + jnp.where(hlane == h, _split_sums(wgt, ones_sq)[:, 0:1], 0.0)
                    w_hi = wgt.astype(jnp.bfloat16)
                    w_lo = (wgt - w_hi.astype(F32)).astype(jnp.bfloat16)
                    col = _dot_tn(jnp.ones((2 * B, 8), jnp.bfloat16), jnp.concatenate([w_hi, w_lo], axis=0))
                    d_acs_t = d_acs_t + jnp.where(hsub == h, col[0:1, :], 0.0)
                    halves.append(_dot_tn(m, dyp))
                bdp = bds[:, psl]
                dxdt = jnp.where(lo, halves[0], halves[1]) + ew * bdp
                dxa_ref[:, xsl] = dtp * dxdt + _pair(lo, dsk, h0) * dyp
                xw = xdt * ew
                terms_dt = rsum(dxdt * xp)
                terms_dd = rsum(dyp * xp)
                terms_off = rsum(dyp * (ea * yoff[:, psl]))
                terms_e = rsum(xw * bdp)
                terms_h = rsum(hg[:, psl] * dhg[:, psl])
                for half in range(2):
                    h = h0 + half
                    sel = hlane == h
                    d_dt = d_dt + jnp.where(sel, terms_dt[half], 0.0)
                    d_d = d_d + jnp.where(sel, jnp.sum(terms_dd[half], axis=0, keepdims=True), 0.0)
                    e_last = jnp.sum(jnp.where(lo if half == 0 else hi, el, 0.0), axis=1, keepdims=True) * (1.0 / P)
                    d_last = (jnp.sum(terms_e[half], axis=0, keepdims=True)
                              + e_last * jnp.sum(terms_h[half], axis=0, keepdims=True))
                    d_acs = d_acs + jnp.where(sel, terms_off[half] - terms_e[half] + jnp.where(lastrow, d_last, 0.0), 0.0)
                dyes.append(dyp * ea)
                xws.append(xw)
                decs.append(el)
            dye = jnp.concatenate(dyes, axis=1)
            xwc = jnp.concatenate(xws, axis=1)
            dxa_ref[:, csl] = _dot_nn(d_cb, bg) + _dot_nt(dye, hg)
            dxa_ref[:, bsl] = _dot_tn(d_cb, cg) + _dot_nt(xwc, dhg)
            dh_scr[g] = dhg * jnp.concatenate(decs, axis=1) + _dot_tn(cg, dye)
        ddt_ref[...] = d_dt
        dacs_ref[...] = d_acs - d_acs_t.T[:, :H]
        dd_ref[...] += d_d

    rev = lambda w: pl.BlockSpec((B, w), lambda c: (nc - 1 - c, 0))
    vec = pl.BlockSpec((1, H), lambda c: (0, 0))
    return pl.pallas_call(
        body, name=name, grid=(nc,),
        in_specs=[rev(C.CC), rev(H), rev(H), pl.BlockSpec((H, B), lambda c: (0, nc - 1 - c)), vec,
                  pl.BlockSpec((1, G * N, W), lambda c: (nc - 1 - c, 0, 0)), rev(DI)],
        out_specs=[rev(C.CC), rev(H), rev(H), vec],
        out_shape=[jax.ShapeDtypeStruct((T, C.CC), F32), jax.ShapeDtypeStruct((T, H), F32),
                   jax.ShapeDtypeStruct((T, H), F32), jax.ShapeDtypeStruct((1, H), F32)],
        scratch_shapes=[pltpu.VMEM((G, N, W), F32)],
        compiler_params=_params("arbitrary"),
    )(xa, dt, acs, acs_t, dskip, hprev, dy)


def ssd_post(ddt, dacs, dt, raw, bias, alog, *, name):
    T, H = ddt.shape
    B = C.blk
    tc = _pick(T, (4 * B, 2 * B, B))

    def body(ddt_ref, dacs_ref, dt_ref, raw_ref, b_ref, al_ref, draw_ref, db_ref, dal_ref):
        @pl.when(pl.program_id(0) == 0)
        def _():
            db_ref[...] = jnp.zeros_like(db_ref)
            dal_ref[...] = jnp.zeros_like(dal_ref)

        a = -jnp.exp(al_ref[...])
        lower = (lax.broadcasted_iota(jnp.int32, (B, B), 0) <= lax.broadcasted_iota(jnp.int32, (B, B), 1)).astype(F32)
        for j in range(tc // B):
            sl = slice(j * B, (j + 1) * B)
            rc = _dot_exact(lower, dacs_ref[sl, :])
            dtv = dt_ref[sl, :]
            draw = (ddt_ref[sl, :] + a * rc) * _sigmoid(raw_ref[sl, :] + b_ref[...])
            draw_ref[sl, :] = draw
            db_ref[...] += jnp.sum(draw, axis=0, keepdims=True)
            dal_ref[...] += jnp.sum(dtv * rc, axis=0, keepdims=True) * a

    blk = pl.BlockSpec((tc, H), lambda i: (i, 0))
    vec = pl.BlockSpec((1, H), lambda i: (0, 0))
    return pl.pallas_call(
        body, name=name, grid=(T // tc,), in_specs=[blk, blk, blk, blk, vec, vec], out_specs=[blk, vec, vec],
        out_shape=[jax.ShapeDtypeStruct((T, H), F32), jax.ShapeDtypeStruct((1, H), F32), jax.ShapeDtypeStruct((1, H), F32)],
        compiler_params=_params("arbitrary"),
    )(ddt, dacs, dt, raw, bias, alog)


MESH = pl.DeviceIdType.MESH
PACK_W = 1024
ANY = pl.BlockSpec(memory_space=pl.ANY)


def _place():
    return lax.axis_index("x"), lax.axis_index("y"), lax.axis_index("c")


def _other_chips(x, y):
    return [(1 - x, y), (x, 1 - y), (1 - x, 1 - y)]


def _remote(src, dst, send_sems, recv_sems, k, to):
    return pltpu.make_async_remote_copy(src_ref=src, dst_ref=dst, send_sem=send_sems.at[k], recv_sem=recv_sems.at[k],
                                        device_id=to, device_id_type=MESH)


def _half(c, rows):
    rh = rows // 2
    return pl.ds(pl.multiple_of(c * rh, 16 if rh % 16 == 0 else 8), rh)


def gather_layer(shards, *, name):
    n = len(shards)

    def body(*refs):
        ps, gs = refs[:n], refs[n:2 * n]
        send_sems, recv_sems = refs[2 * n:]
        x, y, c = _place()
        s = 2 * x + y
        sib = (x, y, 1 - c)
        chips = _other_chips(x, y)
        copy = functools.partial(_remote, send_sems=send_sems, recv_sems=recv_sems)
        started = []
        for w in range(n):
            started.append(copy(ps[w], gs[w].at[s], k=w, to=sib))
        for w in range(n):
            mine = _half(c, ps[w].shape[0])
            for j, (px, py) in enumerate(chips):
                started.append(copy(ps[w].at[mine], gs[w].at[s, mine], k=n + 3 * w + j, to=(px, py, c)))
        for cp in started:
            cp.start()
        passed = []
        for w in range(n):
            mine = _half(c, ps[w].shape[0])
            for j, (px, py) in enumerate(chips):
                there = gs[w].at[2 * px + py, mine]
                copy(there, there, k=n + 3 * w + j, to=(px, py, c)).wait_recv()
                fw = copy(there, there, k=4 * n + 3 * w + j, to=sib)
                fw.start()
                passed.append(fw)
        for w in range(n):
            copy(ps[w], gs[w].at[s], k=w, to=sib).wait_recv()
            theirs = _half(1 - c, ps[w].shape[0])
            for j, (px, py) in enumerate(chips):
                there = gs[w].at[2 * px + py, theirs]
                copy(there, there, k=4 * n + 3 * w + j, to=sib).wait_recv()
        for cp in started + passed:
            cp.wait_send()

    return pl.pallas_call(
        body, name=name, in_specs=[ANY] * n, out_specs=[ANY] * n,
        out_shape=[jax.ShapeDtypeStruct((4,) + p.shape, p.dtype) for p in shards],
        scratch_shapes=[pltpu.SemaphoreType.DMA((7 * n,)), pltpu.SemaphoreType.DMA((7 * n,))],
    )(*shards)


HBM = pl.BlockSpec(memory_space=pltpu.HBM)
SEMS = pl.BlockSpec(memory_space=pltpu.SEMAPHORE)


def gather_plan(x, y, c, ps, gs):
    s = 2 * x + y
    sends, lands = [], []
    for p, g in zip(ps, gs):
        rows = p.shape[0]
        sends.append((p, g.at[s], (x, y, 1 - c)))
        lands.append(g.at[s])
        for px, py in _other_chips(x, y):
            for pc in (c, 1 - c):
                sends.append((p.at[_half(c, rows)], g.at[s, _half(c, rows)], (px, py, pc)))
                lands.append(g.at[2 * px + py, _half(pc, rows)])
    return sends, lands


def scatter_plan(x, y, c, ps, gs):
    sends, lands = [], []
    for p, g in zip(ps, gs):
        for j, (px, py) in enumerate(_other_chips(x, y)):
            sends.append((p.at[2 * px + py], g.at[j], (px, py, c)))
            lands.append(g.at[j])
    return sends, lands


def _hbm(a):
    return pltpu.with_memory_space_constraint(a, pltpu.HBM)


def exchange_start(srcs, land_shapes, plan, ncopies, *, name):
    n = len(srcs)

    def body(*refs):
        ps, gs = refs[:n], refs[n:2 * n]
        send_sems, recv_sems = refs[2 * n], refs[2 * n + 1]
        token = refs[-1]
        sends, _ = plan(*_place(), ps, gs)
        for k, (src, dst, to) in enumerate(sends):
            _remote(src, dst, send_sems, recv_sems, k, to).start()
        token[...] = jnp.zeros_like(token)

    lands = [_hbm(lax.empty(s.shape, s.dtype)) for s in land_shapes]
    outs = pl.pallas_call(
        body, name=name,
        out_shape=[pltpu.SemaphoreType.DMA((ncopies,)), pltpu.SemaphoreType.DMA((ncopies,))]
        + [pltpu.HBM(a.shape, a.dtype) for a in srcs] + [pltpu.HBM(s.shape, s.dtype) for s in land_shapes]
        + [jax.ShapeDtypeStruct((8, 128), F32)],
        in_specs=[HBM] * (2 * n), out_specs=[SEMS, SEMS] + [HBM] * (2 * n) + [pl.BlockSpec(memory_space=pltpu.VMEM)],
        input_output_aliases={k: 2 + k for k in range(2 * n)},
        compiler_params=pltpu.CompilerParams(has_side_effects=pltpu.SideEffectType.DATAFLOW_SIDE_EFFECTING),
    )(*[_hbm(a) for a in srcs], *lands)
    return outs[0], outs[1], list(outs[2:2 + n]), list(outs[2 + n:2 + 2 * n]), outs[-1]


def exchange_wait(send_sems, recv_sems, srcs, lands, plan, after, *, name):
    n = len(srcs)

    def body(*refs):
        ps, gs = refs[:n], refs[n:2 * n]
        send_sems, recv_sems = refs[2 * n], refs[2 * n + 1]
        sends, arrivals = plan(*_place(), ps, gs)
        for k, ((src, dst, to), land) in enumerate(zip(sends, arrivals)):
            _remote(src, dst, send_sems, recv_sems, k, to).wait_send()
            _remote(land, land, send_sems, recv_sems, k, to).wait_recv()

    outs = pl.pallas_call(
        body, name=name,
        out_shape=[pltpu.HBM(a.shape, a.dtype) for a in srcs] + [pltpu.HBM(a.shape, a.dtype) for a in lands],
        in_specs=[HBM] * (2 * n) + [SEMS, SEMS, ANY], out_specs=[HBM] * (2 * n),
        input_output_aliases={k: k for k in range(2 * n)},
        compiler_params=pltpu.CompilerParams(has_side_effects=pltpu.SideEffectType.DATAFLOW_SIDE_EFFECTING),
    )(*srcs, *lands, send_sems, recv_sems, after)
    return list(outs[n:])


def swap_halves(items, *, name):
    n = len(items)

    def body(*refs):
        gs, rs = refs[:n], refs[n:2 * n]
        send_sems, recv_sems = refs[2 * n:]
        x, y, c = _place()
        cps = [_remote(g.at[:, _half(1 - c, g.shape[1])], r, send_sems, recv_sems, k, (x, y, 1 - c))
               for k, (g, r) in enumerate(zip(gs, rs))]
        for cp in cps:
            cp.start()
        for cp in cps:
            cp.wait()

    return pl.pallas_call(
        body, name=name, in_specs=[ANY] * n, out_specs=[ANY] * n,
        out_shape=[jax.ShapeDtypeStruct((4, g.shape[1] // 2, g.shape[2]), g.dtype) for g in items],
        scratch_shapes=[pltpu.SemaphoreType.DMA((n,)), pltpu.SemaphoreType.DMA((n,))],
    )(*items)


def join_halves(items, *, name):
    n = len(items)

    def body(*refs):
        rs, outs = refs[:n], refs[n:2 * n]
        send_sems, recv_sems = refs[2 * n:]
        x, y, c = _place()
        sib = (x, y, 1 - c)
        cps = []
        for k, (r, o) in enumerate(zip(rs, outs)):
            mine = _half(c, r.shape[0])
            cps.append(_remote(r.at[mine], o.at[mine], send_sems, recv_sems, k, sib))
            cps[-1].start()
        for k, (r, o) in enumerate(zip(rs, outs)):
            theirs = _half(1 - c, r.shape[0])
            _remote(r.at[theirs], o.at[theirs], send_sems, recv_sems, k, sib).wait_recv()
        for cp in cps:
            cp.wait_send()

    return pl.pallas_call(
        body, name=name, in_specs=[ANY] * n, out_specs=[ANY] * n,
        out_shape=[jax.ShapeDtypeStruct(r.shape, r.dtype) for r in items],
        input_output_aliases={k: k for k in range(n)},
        scratch_shapes=[pltpu.SemaphoreType.DMA((n,)), pltpu.SemaphoreType.DMA((n,))],
    )(*items)


def gather_all(v, *, name, total):
    rows, W = v.shape

    def body(v_ref, o_ref, *scr):
        buf = scr[0] if total else o_ref
        send_sems, recv_sems = scr[-2], scr[-1]
        x, y, c = _place()
        me = 4 * x + 2 * y + c
        flips = [(k >> 2 & 1, k >> 1 & 1, k & 1) for k in range(1, 8)]
        peers = [((1 - x) if fx else x, (1 - y) if fy else y, (1 - c) if fc else c) for fx, fy, fc in flips]
        out = []
        for k, peer in enumerate(peers):
            cp = pltpu.make_async_remote_copy(src_ref=v_ref, dst_ref=buf.at[me], send_sem=send_sems.at[k],
                                              recv_sem=recv_sems.at[k], device_id=peer, device_id_type=MESH)
            cp.start()
            out.append(cp)
        buf[me] = v_ref[...]
        for k, (px, py, pc) in enumerate(peers):
            pltpu.make_async_remote_copy(src_ref=v_ref, dst_ref=buf.at[4 * px + 2 * py + pc], send_sem=send_sems.at[k],
                                         recv_sem=recv_sems.at[k], device_id=(px, py, pc), device_id_type=MESH).wait_recv()
        for cp in out:
            cp.wait_send()
        if total:
            acc = buf[0]
            for d in range(1, 8):
                acc = acc + buf[d]
            o_ref[...] = acc

    vm = pl.BlockSpec(memory_space=pltpu.VMEM)
    return pl.pallas_call(
        body, name=name, in_specs=[vm], out_specs=vm,
        out_shape=jax.ShapeDtypeStruct((rows, W) if total else (8, rows, W), F32),
        scratch_shapes=([pltpu.VMEM((8, rows, W), F32)] if total else [])
        + [pltpu.SemaphoreType.DMA((7,)), pltpu.SemaphoreType.DMA((7,))],
    )(v)


def pair_sum(g, r, *, name):
    _, R, W = g.shape
    Rh = R // 2
    tr = _pick(Rh, (512, 256, 128, 64, 32, 16))
    nb = Rh // tr

    def body(g_ref, r_ref, o_ref):
        o_ref[...] = (g_ref[...].astype(F32) + r_ref[...].astype(F32)).astype(o_ref.dtype)

    return pl.pallas_call(
        body, name=name, grid=(4, nb),
        in_specs=[pl.BlockSpec((1, tr, W), lambda s, i: (s, lax.axis_index("c") * nb + i, 0)),
                  pl.BlockSpec((1, tr, W), lambda s, i: (s, i, 0))],
        out_specs=pl.BlockSpec((1, tr, W), lambda s, i: (s, i, 0)),
        out_shape=jax.ShapeDtypeStruct((4, Rh, W), g.dtype), compiler_params=_params("parallel", "parallel"),
    )(g, r)


def chip_sum(g, r1, r2, *, name):
    _, R, W = g.shape
    Rh = R // 2
    tr = _pick(Rh, (512, 256, 128, 64, 32, 16))
    nb = Rh // tr

    def body(g_ref, r1_ref, a_ref, b_ref, c_ref, o_ref):
        acc = g_ref[0].astype(F32) + r1_ref[0].astype(F32)
        for ref in (a_ref, b_ref, c_ref):
            acc = acc + ref[0].astype(F32)
        o_ref[...] = acc

    shard = lambda: 2 * lax.axis_index("x") + lax.axis_index("y")
    half = lambda i: lax.axis_index("c") * nb + i
    return pl.pallas_call(
        body, name=name, grid=(nb,),
        in_specs=[pl.BlockSpec((1, tr, W), lambda i: (shard(), half(i), 0)),
                  pl.BlockSpec((1, tr, W), lambda i: (shard(), i, 0))]
        + [pl.BlockSpec((1, tr, W), lambda i, j=j: (j, i, 0)) for j in range(3)],
        out_specs=pl.BlockSpec((tr, W), lambda i: (half(i), 0)),
        out_shape=jax.ShapeDtypeStruct((R, W), F32), compiler_params=_params("parallel"),
    )(g, r1, r2, r2, r2)


BIG = (("w_in", False), ("pool_w", False), ("w_attn_br", False), ("w_pool_br", False), ("w_ssm_br", True),
       ("w_out", True), ("w_gate_up", False), ("w_down", True))


def _join(piece, axis):
    t = jnp.moveaxis(piece, 0, axis)
    shp = t.shape
    return t.reshape(shp[:axis] + (shp[axis] * shp[axis + 1],) + shp[axis + 2:])


def _seg_bounds():
    aw, kw, _, pw, di, cc, h, gd = C.in_widths
    o = [0, aw + 2 * kw]
    for wdt in (pw, di, cc, h, gd):
        o.append(o[-1] + wdt)
    return o


IN_PAD = 640


def _in_segments(blocks):
    w_in = jnp.concatenate([blocks[s] for s in range(4)], axis=1)
    o = _seg_bounds()
    segs = [w_in[:, o[i]:o[i + 1]] for i in range(6)]
    segs[4] = jnp.pad(segs[4], ((0, 0), (0, 128 - C.H)))
    width = sum(t.shape[1] for t in segs)
    pad = -width % IN_PAD
    return segs, jnp.concatenate(segs + [jnp.zeros((w_in.shape[0], pad), w_in.dtype)], axis=1)


def _layer_fwd(x, p, tag):
    nm = lambda s: f"{s}_{tag}"
    H = C.H
    h = rms_fwd(x, p["ln1_w"], name=nm("rms1"))
    wq, wu, wz, wx, wd, wg = p["in_segs"]
    qkv = matmul(h, wq, name=nm("mm_qkv"), out_dtype=MXU)
    u = matmul(h, wu, name=nm("mm_u"))
    z = matmul(h, wz, name=nm("mm_z"))
    xbc = matmul(h, wx, name=nm("mm_xbc"))
    dtp = matmul(h, wd, name=nm("mm_dt"))
    gl = matmul(h, wg, name=nm("mm_gate"), out_dtype=MXU)
    att = attn_fwd(qkv, p["attn_sink"], name=nm("attn_fwd"))
    pool, mixed = pool_fwd(u, p["pool_w"], p["pool_scale"], name=nm("pool_fwd"))
    xa = conv_fwd(xbc, p["conv_w"], p["conv_b"], name=nm("conv_fwd"))
    raw = dtp[:, :H]
    dt_t, acs_t = ssd_prep(raw.T, p["dt_bias"].T, p["a_log"].T, name=nm("ssd_prep"))
    dt, acs = dt_t.T, acs_t.T
    y, hprev = ssd_fwd(xa, dt, acs, acs_t, p["d_skip"], name=nm("ssd_fwd"))
    ssm = gnorm_fwd(y, z, p["ssm_norm_w"], name=nm("gnorm_fwd"))
    abr = matmul(att, name=nm("mm_abr"), out_dtype=MXU, **p["w_attn_br"])
    pbr = matmul(pool, name=nm("mm_pbr"), out_dtype=MXU, **p["w_pool_br"])
    sbr = matmul(ssm, name=nm("mm_sbr"), out_dtype=MXU, **p["w_ssm_br"])
    merged = merge_fwd(gl, abr, pbr, sbr, name=nm("merge_fwd"))
    xm = matmul(merged, add=x, name=nm("mm_out"), **p["w_out"])
    h2 = rms_fwd(xm, p["ln2_w"], name=nm("rms2"))
    gu = matmul(h2, name=nm("mm_gu"), out_dtype=MXU, **p["w_gate_up"])
    act = swiglu_fwd(gu, name=nm("swiglu_fwd"))
    xo = matmul(act, add=xm, name=nm("mm_down"), **p["w_down"])
    saved = dict(x=x, h=h, qkv=qkv, z=z, xbc=xbc, raw=raw, gl=gl, att=att, pool=pool, mixed=mixed, xa=xa, dt=dt, acs=acs,
                 acs_t=acs_t, y=y, hprev=hprev, ssm=ssm, abr=abr, pbr=pbr, sbr=sbr, merged=merged, xm=xm, h2=h2, gu=gu,
                 act=act)
    return xo, saved


def _layer_bwd(dxo, dxo_m, p, s, tag):
    nm = lambda t: f"{t}_{tag}"
    H = C.H
    g = {}

    def rows4(t):
        return t.reshape(4, t.shape[0] // 4, t.shape[1])

    g["w_down"] = rows4(matmul(s["act"], dxo_m, ta=True, out_dtype=MXU, name=nm("mmg_down")))
    dact = matmul(dxo_m, tb=True, out_dtype=MXU, name=nm("mmb_down"), **p["w_down"])
    dgu = swiglu_bwd(s["gu"], dact, name=nm("swiglu_bwd"))
    g["w_gate_up"] = matmul(s["h2"], dgu, ta=True, out_dtype=MXU, out_cols=True, name=nm("mmg_gu"))
    dh2 = matmul(dgu, tb=True, name=nm("mmb_gu"), **p["w_gate_up"])
    dxm, dxm_m, g["ln2_w"] = rms_bwd(s["xm"], p["ln2_w"], dh2, dxo, name=nm("rms2_bwd"))
    g["w_out"] = rows4(matmul(s["merged"], dxm_m, ta=True, out_dtype=MXU, name=nm("mmg_out")))
    dmerged = matmul(dxm_m, tb=True, out_dtype=MXU, name=nm("mmb_out"), **p["w_out"])
    dabr, dpbr, dsbr, dgl = merge_bwd(s["gl"], s["abr"], s["pbr"], s["sbr"], dmerged, name=nm("merge_bwd"))
    g["w_attn_br"] = matmul(s["att"], dabr, ta=True, out_dtype=MXU, out_cols=True, name=nm("mmg_abr"))
    g["w_pool_br"] = matmul(s["pool"], dpbr, ta=True, out_dtype=MXU, out_cols=True, name=nm("mmg_pbr"))
    g["w_ssm_br"] = rows4(matmul(s["ssm"], dsbr, ta=True, out_dtype=MXU, name=nm("mmg_sbr")))
    datt = matmul(dabr, tb=True, out_dtype=MXU, name=nm("mmb_abr"), **p["w_attn_br"])
    dpool = matmul(dpbr, tb=True, out_dtype=MXU, name=nm("mmb_pbr"), **p["w_pool_br"])
    dssm = matmul(dsbr, tb=True, out_dtype=MXU, name=nm("mmb_sbr"), **p["w_ssm_br"])
    dq, dk, dv, g["attn_sink"] = attn_bwd(s["qkv"], p["attn_sink"], datt, name=nm("attn_bwd"))
    du, dpw, g["pool_scale"] = pool_bwd(dpool, s["mixed"], p["pool_w"], p["pool_scale"], name=nm("pool_bwd"))
    pg = dpw.shape[1] // 4
    g["pool_w"] = jnp.moveaxis(dpw.reshape(4, 4, pg, dpw.shape[2]), 1, 0).reshape(4, 4 * pg, dpw.shape[2]).astype(MXU)
    dy, dz, g["ssm_norm_w"] = gnorm_bwd(s["y"], s["z"], p["ssm_norm_w"], dssm, name=nm("gnorm_bwd"))
    dxa, ddt, dacs, g["d_skip"] = ssd_bwd(s["xa"], s["dt"], s["acs"], s["acs_t"], p["d_skip"], s["hprev"], dy,
                                          name=nm("ssd_bwd"))
    draw, g["dt_bias"], g["a_log"] = ssd_post(ddt, dacs, s["dt"], s["raw"], p["dt_bias"], p["a_log"], name=nm("ssd_post"))
    dxbc, g["conv_w"], g["conv_b"] = conv_bwd(s["xbc"], p["conv_w"], p["conv_b"], dxa, name=nm("conv_bwd"))
    return g, dict(dq=dq, dk=dk, dv=dv, du=du, dz=dz, dxbc=dxbc, draw=draw, dgl=dgl, dxm=dxm)


def _layer_bwd_in(c, p, s, tag):
    nm = lambda t: f"{t}_{tag}"
    H = C.H
    dq, dk, dv, du, dz, dxbc, draw, dgl, dxm = (c[k] for k in ("dq", "dk", "dv", "du", "dz", "dxbc", "draw", "dgl", "dxm"))
    w_all = p["in_all"]
    ddtp = jnp.pad(draw, ((0, 0), (0, 128 - H))).astype(MXU)
    parts = [dq, dk, dv, du, dz, dxbc, ddtp, dgl]
    used = sum(t.shape[1] for t in parts)
    dproj = jnp.concatenate(parts + [jnp.zeros((dq.shape[0], w_all.shape[1] - used), MXU)], axis=1)
    dh = matmul(dproj, w_all, tb=True, name=nm("mmb_in"))
    g_all = matmul(s["h"], dproj, ta=True, out_dtype=MXU, name=nm("mmg_in"))
    o = _seg_bounds()
    dt0 = o[4]
    g_in = jnp.concatenate([g_all[:, :dt0 + H], g_all[:, dt0 + 128:dt0 + 128 + (o[6] - o[5])]], axis=1)
    nc = g_in.shape[1] // 4
    g_w_in = jnp.stack([g_in[:, k * nc:(k + 1) * nc] for k in range(4)])
    dx, dx_m, g_ln1 = rms_bwd(s["x"], p["ln1_w"], dh, dxm, name=nm("rms1_bwd"))
    return dx, dx_m, g_w_in, g_ln1


SMALL = ("ln1_w", "attn_sink", "conv_w", "conv_b", "dt_bias", "a_log", "d_skip", "ssm_norm_w", "pool_scale", "ln2_w")
WEIGHTS = ("ln1_w", "w_in", "attn_sink", "conv_w", "conv_b", "dt_bias", "a_log", "d_skip", "ssm_norm_w", "pool_w",
           "pool_scale", "w_attn_br", "w_pool_br", "w_ssm_br", "w_out", "ln2_w", "w_gate_up", "w_down", "final_w")


def _step(x, loss_target, w, m, v):
    depth = C.depth
    xi, yi, ci = _place()
    shard = 2 * xi + yi

    names = [n for n, _ in BIG]
    nbig = len(names)
    cw_cols = w["conv_w"].shape[2]

    def shards(i):
        taps = w["conv_w"][i].reshape(CONV_TAP_ROWS, -1)
        return [w[n][i].astype(MXU).reshape(-1, w[n].shape[-1]) for n in names] + [taps]

    def params(i, blocks):
        full = dict(zip(names, blocks))
        p = {n: w[n][i][None] for n in SMALL if n != "conv_w"}
        p["conv_w"] = _join(blocks[-1].reshape(4, CONV_K, cw_cols), 1)
        p["in_segs"], p["in_all"] = _in_segments(full["w_in"])
        pw = full["pool_w"]
        p["pool_w"] = _join(pw.reshape(4, 4, pw.shape[1] // 4, pw.shape[2]), 1)
        for n, rw in BIG[2:]:
            t = full[n]
            p[n] = dict(b=t.reshape(4 * t.shape[1], t.shape[2])) if rw else dict(b=t, b_cols=True)
        return p

    xs = x[0]
    layers, saved = [], []
    blocks = gather_layer(shards(0), name="gather_l0")
    for i in range(depth):
        nxt = None
        if i + 1 < depth:
            src = shards(i + 1)
            nxt = exchange_start(src, [jax.ShapeDtypeStruct((4,) + a.shape, a.dtype) for a in src], gather_plan,
                                 7 * len(src), name=f"gather_l{i + 1}_start")
            xs = xs + nxt[4][0, 0]
        layers.append(params(i, blocks))
        xs, sv = _layer_fwd(xs, layers[i], f"l{i}")
        saved.append(sv)
        if nxt is not None:
            blocks = exchange_wait(nxt[0], nxt[1], nxt[2], nxt[3], gather_plan, xs, name=f"gather_l{i + 1}_wait")
    loss_part, dx, dx_m, g_final = final_loss(xs, w["final_w"][None], loss_target[0], name="final_loss")

    def pair_sums(i, ns, gs):
        sb = swap_halves(gs, name=f"swap_halves_{ns[0]}_l{i}")
        return sb, [pair_sum(g, r, name=f"pair_sum_{n}_l{i}") for n, g, r in zip(ns, gs, sb)]

    def scatter_start(chip, tag):
        return exchange_start(chip, [jax.ShapeDtypeStruct((3,) + a.shape[1:], a.dtype) for a in chip], scatter_plan,
                              3 * len(chip), name=f"scatter_{tag}_start")

    def scatter_wait(pend, after, tag):
        return exchange_wait(pend[0], pend[1], pend[2], pend[3], scatter_plan, after, name=f"scatter_{tag}_wait")

    grads = [None] * depth
    pieces = {}
    rest = names[1:]
    pend_in = None
    for i in reversed(range(depth)):
        if pend_in is not None:
            dx_m = dx_m + pend_in[0][4][0, 0].astype(dx_m.dtype)
        grads[i], cot = _layer_bwd(dx, dx_m, layers[i], saved[i], f"l{i}")
        g_rest = [grads[i][n] for n in rest]
        sib_rest, chip_rest = pair_sums(i, rest, g_rest)
        pend_rest = scatter_start(chip_rest, f"rest_l{i}")
        cot["draw"] = cot["draw"] + pend_rest[4][0, 0]
        dx, dx_m, g_in, grads[i]["ln1_w"] = _layer_bwd_in(cot, layers[i], saved[i], f"l{i}")
        if pend_in is not None:
            pend, g_up, sib_up = pend_in
            pieces[(i + 1, "w_in")] = (g_up, sib_up[0], scatter_wait(pend, dx, f"in_l{i + 1}")[0])
        for n, g, r, o in zip(rest, g_rest, sib_rest, scatter_wait(pend_rest, dx, f"rest_l{i}")):
            pieces[(i, n)] = (g, r, o)
        sib_in, chip_in = pair_sums(i, ["w_in"], [g_in])
        pend_in = (scatter_start(chip_in, f"in_l{i}"), g_in, sib_in)
    keys = [(i, n) for i in range(depth) for n in names if (i, n) != (0, "w_in")]
    g_first, r_first, o_first = pieces[keys[0]]
    pieces[keys[0]] = (g_first + pend_in[0][4][0, 0].astype(g_first.dtype), r_first, o_first)
    mine = [chip_sum(*pieces[k], name=f"chip_sum_{k[1]}_l{k[0]}") for k in keys]
    reduced = dict(zip(keys, join_halves(mine, name="join_halves")))
    gout = {}

    small = [jnp.stack([grads[i][n].reshape(-1) for i in range(depth)]).reshape(1, -1) for n in SMALL]
    small += [g_final.reshape(1, -1), loss_part.reshape(1, -1)]
    tot = gather_all(_pack_small(small), name="sum_small", total=True)
    parts = _unpack_small(tot, [t.shape[1] for t in small])
    for n, t in zip(SMALL, parts):
        if n == "conv_w":
            cols = w[n].shape[2]
            gout[n] = lax.dynamic_slice_in_dim(t.reshape(depth, CONV_K, -1), shard * cols, cols, axis=2)
        else:
            gout[n] = t.reshape(w[n].shape)
    gout["final_w"] = parts[-2].reshape(w["final_w"].shape)
    loss = parts[-1].reshape(())

    upd = {}
    for n in [t for t in WEIGHTS if t != "w_in"] + ["w_in"]:
        if n == "w_in":
            pend, g_in, sib_in = pend_in
            landed = scatter_wait(pend, upd["w_down"][1], "in_l0")[0]
            mine_in = chip_sum(g_in, sib_in[0], landed, name="chip_sum_w_in_l0")
            reduced[(0, n)] = join_halves([mine_in], name="join_halves_w_in_l0")[0]
        if n in gout:
            upd[n] = (gout[n].reshape(w[n].shape),) + adamw(w[n], gout[n].reshape(w[n].shape), m[n], v[n], name=f"adamw_{n}")
        else:
            upd[n] = adamw_layers(w[n], [reduced[(i, n)] for i in range(depth)], m[n], v[n], name=f"adamw_{n}")
    return (loss, dx[None], *[upd[n][0] for n in WEIGHTS], *[upd[n][1] for n in WEIGHTS],
            *[upd[n][2] for n in WEIGHTS], *[upd[n][3] for n in WEIGHTS])


def _pack_small(parts):
    flat = jnp.concatenate(parts, axis=1)
    rows = -(-flat.shape[1] // PACK_W)
    rows = -(-rows // 8) * 8
    return jnp.pad(flat, ((0, 0), (0, rows * PACK_W - flat.shape[1]))).reshape(rows, PACK_W)


def _unpack_small(buf, sizes):
    flat = buf.reshape(-1)
    out, off = [], 0
    for n in sizes:
        out.append(flat[off:off + n])
        off += n
    return out


def kernel(x, ln1_w, w_in, attn_sink, conv_w, conv_b, dt_bias, a_log, d_skip, ssm_norm_w, pool_w, pool_scale, w_attn_br, w_pool_br, w_ssm_br, w_out, ln2_w, w_gate_up, w_down, final_w, loss_target, m_ln1_w, m_w_in, m_attn_sink, m_conv_w, m_conv_b, m_dt_bias, m_a_log, m_d_skip, m_ssm_norm_w, m_pool_w, m_pool_scale, m_w_attn_br, m_w_pool_br, m_w_ssm_br, m_w_out, m_ln2_w, m_w_gate_up, m_w_down, m_final_w, v_ln1_w, v_w_in, v_attn_sink, v_conv_w, v_conv_b, v_dt_bias, v_a_log, v_d_skip, v_ssm_norm_w, v_pool_w, v_pool_scale, v_w_attn_br, v_w_pool_br, v_w_ssm_br, v_w_out, v_ln2_w, v_w_gate_up, v_w_down, v_final_w):
    w = dict(ln1_w=ln1_w, w_in=w_in, attn_sink=attn_sink, conv_w=conv_w, conv_b=conv_b, dt_bias=dt_bias, a_log=a_log,
             d_skip=d_skip, ssm_norm_w=ssm_norm_w, pool_w=pool_w, pool_scale=pool_scale, w_attn_br=w_attn_br,
             w_pool_br=w_pool_br, w_ssm_br=w_ssm_br, w_out=w_out, ln2_w=ln2_w, w_gate_up=w_gate_up, w_down=w_down,
             final_w=final_w)
    m = dict(ln1_w=m_ln1_w, w_in=m_w_in, attn_sink=m_attn_sink, conv_w=m_conv_w, conv_b=m_conv_b, dt_bias=m_dt_bias,
             a_log=m_a_log, d_skip=m_d_skip, ssm_norm_w=m_ssm_norm_w, pool_w=m_pool_w, pool_scale=m_pool_scale,
             w_attn_br=m_w_attn_br, w_pool_br=m_w_pool_br, w_ssm_br=m_w_ssm_br, w_out=m_w_out, ln2_w=m_ln2_w,
             w_gate_up=m_w_gate_up, w_down=m_w_down, final_w=m_final_w)
    v = dict(ln1_w=v_ln1_w, w_in=v_w_in, attn_sink=v_attn_sink, conv_w=v_conv_w, conv_b=v_conv_b, dt_bias=v_dt_bias,
             a_log=v_a_log, d_skip=v_d_skip, ssm_norm_w=v_ssm_norm_w, pool_w=v_pool_w, pool_scale=v_pool_scale,
             w_attn_br=v_w_attn_br, w_pool_br=v_w_pool_br, w_ssm_br=v_w_ssm_br, w_out=v_w_out, ln2_w=v_ln2_w,
             w_gate_up=v_w_gate_up, w_down=v_w_down, final_w=v_final_w)
    return _step(x, loss_target, w, m, v)
```

```python
import functools

import jax
import jax.numpy as jnp
from jax import lax
from jax.experimental import pallas as pl
from jax.experimental.pallas import tpu as pltpu

F32 = jnp.float32
MXU = jnp.bfloat16
VMEM_LIMIT = 56 * 1024 * 1024
EPS = 1e-6
NEG = -1e30

ADAM_LR, ADAM_B1, ADAM_B2, ADAM_EPS, ADAM_WD, ADAM_STEP = 0.001, 0.9, 0.999, 1e-08, 0.01, 10


class Cfg:
    def __init__(self, d_model=2048, seq=8192, depth=2, q_heads=16, kv_heads=4, head_dim=64,
                 ssm_head_dim=64, ssm_groups=4, d_state=128):
        self.D, self.T, self.depth = d_model, seq, depth
        self.hd, self.qh, self.kvh = head_dim, q_heads, kv_heads
        self.AW, self.KW = q_heads * head_dim, kv_heads * head_dim
        self.blk = 128
        self.PW = d_model // 2
        self.PG = self.PW // 4
        self.DI = d_model
        self.P = ssm_head_dim
        self.H = self.DI // self.P
        self.G = ssm_groups
        self.HG = self.H // self.G
        self.N = d_state
        self.CC = self.DI + 2 * self.G * self.N
        self.F = -(-8 * d_model // (3 * 256)) * 256
        self.in_widths = (self.AW, self.KW, self.KW, self.PW, self.DI, self.CC, self.H, 3 * d_model)
        self.in_cols = sum(self.in_widths)


C = Cfg()
POOL_WINDOWS = (2, 4, 8, 16)


def _pick(n, cands):
    for c in cands:
        if n % c == 0:
            return c
    return n


def _params(*sem):
    return pltpu.CompilerParams(dimension_semantics=sem, vmem_limit_bytes=VMEM_LIMIT)


def _sigmoid(x):
    return 1.0 / (1.0 + jnp.exp(-x))


def _dot(a, b, dims):
    return lax.dot_general(a.astype(MXU), b.astype(MXU), (dims, ((), ())), preferred_element_type=F32)


def _dot_nn(a, b):
    return _dot(a, b, ((1,), (0,)))


def _dot_nt(a, b):
    return _dot(a, b, ((1,), (1,)))


def _dot_tn(a, b):
    return _dot(a, b, ((0,), (0,)))


MM_VMEM = 40 * 1024 * 1024
LANES = 128


def _tile(n, cap):
    best = None
    for d in range(LANES, min(n, cap) + 1, LANES):
        if n % d == 0:
            best = d
    return n if best is None else best


def _divisors(n, cap):
    ds = [d for d in range(LANES, min(n, cap) + 1, LANES) if n % d == 0]
    return ds or [n]


HBM_RATE, MXU_RATE, ACC_RATE, STEP_COST = 3.0e12, 0.9e15, 1.2e13, 0.35e-6
MXU_WIDTH = 256


def _choose_tiles(M, N, K, n_unit, k_unit, sa, sb, so, sadd, ta):
    best = None
    for tm in sorted({_tile(M, 1024), _tile(M, 512)}):
        for tk in _divisors(k_unit, 4096):
            for tn in _divisors(n_unit, 2048):
                nk = K // tk
                vmem = 2 * (tm * tk * sa + tk * tn * sb + tm * tn * (so + sadd)) + (tm * tn * 4 if nk > 1 else 0)
                if vmem > MM_VMEM:
                    continue
                steps = (M // tm) * (N // tn) * nk
                a_bytes = M * K * sa * (N // tn if (nk > 1 or ta) else 1)
                b_bytes = K * N * sb * (M // tm)
                hbm = (a_bytes + b_bytes + M * N * (so + sadd)) / HBM_RATE
                fill = (tn / (-(-tn // MXU_WIDTH) * MXU_WIDTH)) * (tk / (-(-tk // MXU_WIDTH) * MXU_WIDTH))
                mxu = 2.0 * M * N * K / (MXU_RATE * fill)
                acc = steps * tm * tn * 8 / ACC_RATE if nk > 1 else 0.0
                cost = max(hbm, mxu) + acc + steps * STEP_COST
                if best is None or cost < best[0]:
                    best = (cost, tm, tn, tk)
    assert best is not None, (M, N, K)
    return best[1:]


def matmul(a, b, *, name, ta=False, tb=False, out_dtype=F32, add=None, layer=None, b_cols=False, out_cols=False):
    M, K = (a.shape[1], a.shape[0]) if ta else a.shape
    rows, cols = b.shape[-2], (4 if b_cols else 1) * b.shape[-1]
    N, bk = (rows, cols) if tb else (cols, rows)
    assert K == bk, (a.shape, b.shape)
    n_unit = N // 4 if (out_cols or (b_cols and not tb)) else N
    k_unit = K // 4 if (b_cols and tb) else K
    sa, sb, so = a.dtype.itemsize, b.dtype.itemsize, jnp.dtype(out_dtype).itemsize
    sadd = add.dtype.itemsize if add is not None else 0
    tm, tn, tk = _choose_tiles(M, N, K, n_unit, k_unit, sa, sb, so, sadd, ta)
    nk = K // tk
    a_spec = pl.BlockSpec((tk, tm), lambda i, j, k: (k, i)) if ta else pl.BlockSpec((tm, tk), lambda i, j, k: (i, k))
    if b_cols and tb:
        per = k_unit // tk
        b_spec = pl.BlockSpec((None, tn, tk), lambda i, j, k: (k // per, j, k % per))
    elif b_cols:
        per = n_unit // tn
        b_spec = pl.BlockSpec((None, tk, tn), lambda i, j, k: (j // per, k, j % per))
    elif layer is not None:
        b_spec = (pl.BlockSpec((None, tn, tk), lambda i, j, k: (layer, j, k)) if tb
                  else pl.BlockSpec((None, tk, tn), lambda i, j, k: (layer, k, j)))
    else:
        b_spec = pl.BlockSpec((tn, tk), lambda i, j, k: (j, k)) if tb else pl.BlockSpec((tk, tn), lambda i, j, k: (k, j))
    if out_cols:
        per = n_unit // tn
        o_spec = pl.BlockSpec((None, tm, tn), lambda i, j, k: (j // per, i, j % per))
        out_shape = jax.ShapeDtypeStruct((4, M, N // 4), out_dtype)
    else:
        o_spec = pl.BlockSpec((tm, tn), lambda i, j, k: (i, j))
        out_shape = jax.ShapeDtypeStruct((M, N), out_dtype)
    dims = ((0 if ta else 1,), (1 if tb else 0,))
    has_add = add is not None

    def body(*refs):
        a_ref, b_ref = refs[0], refs[1]
        add_ref = refs[2] if has_add else None
        o_ref = refs[3] if has_add else refs[2]
        part = _dot(a_ref[...], b_ref[...], dims)

        def finish(total):
            if has_add:
                total = total + add_ref[...].astype(F32)
            o_ref[...] = total.astype(o_ref.dtype)

        if nk == 1:
            finish(part)
        else:
            acc_ref = refs[-1]
            k = pl.program_id(2)

            @pl.when(k == 0)
            def _():
                acc_ref[...] = part

            @pl.when(k > 0)
            def _():
                acc_ref[...] += part

            @pl.when(k == nk - 1)
            def _():
                finish(acc_ref[...])

    in_specs = [a_spec, b_spec] + ([o_spec] if has_add else [])
    args = (a, b) + ((add,) if has_add else ())
    return pl.pallas_call(
        body, name=name, grid=(M // tm, N // tn, nk), in_specs=in_specs, out_specs=o_spec, out_shape=out_shape,
        scratch_shapes=[pltpu.VMEM((tm, tn), F32)] if nk > 1 else [],
        compiler_params=_params("parallel", "parallel", "arbitrary"),
    )(*args)


def _row_tile(t):
    return _pick(t, (256, 128, 64, 32, 16, 8))


def rms_fwd(x, w, *, name):
    T, D = x.shape
    tr = _row_tile(T)

    def body(x_ref, w_ref, o_ref):
        xv = x_ref[...]
        r = lax.rsqrt(jnp.mean(xv * xv, axis=-1, keepdims=True) + EPS)
        o_ref[...] = (xv * r * w_ref[...]).astype(o_ref.dtype)

    row = pl.BlockSpec((tr, D), lambda i: (i, 0))
    return pl.pallas_call(
        body, name=name, grid=(T // tr,), in_specs=[row, pl.BlockSpec((1, D), lambda i: (0, 0))], out_specs=row,
        out_shape=jax.ShapeDtypeStruct((T, D), MXU), compiler_params=_params("parallel"),
    )(x, w)


def rms_bwd(x, w, dh, dres, *, name):
    T, D = x.shape
    tr = _row_tile(T)

    def body(x_ref, w_ref, dh_ref, dres_ref, dx_ref, dxm_ref, dw_ref):
        xv = x_ref[...]
        dhv = dh_ref[...].astype(F32)
        r = lax.rsqrt(jnp.mean(xv * xv, axis=-1, keepdims=True) + EPS)
        g = dhv * w_ref[...]
        dot = jnp.mean(g * xv, axis=-1, keepdims=True)
        dx = dres_ref[...] + r * g - xv * (r * r * r * dot)
        dx_ref[...] = dx
        dxm_ref[...] = dx.astype(dxm_ref.dtype)
        part = jnp.sum(dhv * xv * r, axis=0, keepdims=True)

        @pl.when(pl.program_id(0) == 0)
        def _():
            dw_ref[...] = part

        @pl.when(pl.program_id(0) > 0)
        def _():
            dw_ref[...] += part

    row = pl.BlockSpec((tr, D), lambda i: (i, 0))
    vec = pl.BlockSpec((1, D), lambda i: (0, 0))
    return pl.pallas_call(
        body, name=name, grid=(T // tr,), in_specs=[row, vec, row, row], out_specs=[row, row, vec],
        out_shape=[jax.ShapeDtypeStruct((T, D), F32), jax.ShapeDtypeStruct((T, D), MXU),
                   jax.ShapeDtypeStruct((1, D), F32)],
        compiler_params=_params("arbitrary"),
    )(x, w, dh, dres)


def final_loss(x, w, target, *, name):
    T, D = x.shape
    tr = _row_tile(T)

    def body(x_ref, w_ref, t_ref, loss_ref, dx_ref, dxm_ref, dw_ref):
        xv = x_ref[...]
        wv = w_ref[...]
        r = lax.rsqrt(jnp.mean(xv * xv, axis=-1, keepdims=True) + EPS)
        err = xv * r * wv - t_ref[...]
        lpart = 0.5 * jnp.sum(jnp.mean(err * err, axis=-1, keepdims=True), axis=0, keepdims=True)
        dy = err * (1.0 / D)
        g = dy * wv
        dot = jnp.mean(g * xv, axis=-1, keepdims=True)
        dx = r * g - xv * (r * r * r * dot)
        dx_ref[...] = dx
        dxm_ref[...] = dx.astype(dxm_ref.dtype)
        part = jnp.sum(dy * xv * r, axis=0, keepdims=True)

        @pl.when(pl.program_id(0) == 0)
        def _():
            dw_ref[...] = part
            loss_ref[...] = lpart

        @pl.when(pl.program_id(0) > 0)
        def _():
            dw_ref[...] += part
            loss_ref[...] += lpart

    row = pl.BlockSpec((tr, D), lambda i: (i, 0))
    vec = pl.BlockSpec((1, D), lambda i: (0, 0))
    one = pl.BlockSpec((1, 1), lambda i: (0, 0))
    return pl.pallas_call(
        body, name=name, grid=(T // tr,), in_specs=[row, vec, row], out_specs=[one, row, row, vec],
        out_shape=[jax.ShapeDtypeStruct((1, 1), F32), jax.ShapeDtypeStruct((T, D), F32),
                   jax.ShapeDtypeStruct((T, D), MXU), jax.ShapeDtypeStruct((1, D), F32)],
        compiler_params=_params("arbitrary"),
    )(x, w, target)


def swiglu_fwd(gu, *, name):
    T, F2 = gu.shape
    F = F2 // 2
    tr = _row_tile(T)

    def body(g_ref, u_ref, o_ref):
        g = g_ref[...].astype(F32)
        o_ref[...] = (g * _sigmoid(g) * u_ref[...].astype(F32)).astype(o_ref.dtype)

    return pl.pallas_call(
        body, name=name, grid=(T // tr,),
        in_specs=[pl.BlockSpec((tr, F), lambda i: (i, 0)), pl.BlockSpec((tr, F), lambda i: (i, 1))],
        out_specs=pl.BlockSpec((tr, F), lambda i: (i, 0)),
        out_shape=jax.ShapeDtypeStruct((T, F), MXU), compiler_params=_params("parallel"),
    )(gu, gu)


def swiglu_bwd(gu, dact, *, name):
    T, F2 = gu.shape
    F = F2 // 2
    tr = _row_tile(T)

    def body(g_ref, u_ref, d_ref, o_ref):
        g = g_ref[...].astype(F32)
        d = d_ref[...].astype(F32)
        s = _sigmoid(g)
        o_ref[:, :F] = (d * u_ref[...].astype(F32) * s * (1.0 + g * (1.0 - s))).astype(o_ref.dtype)
        o_ref[:, F:] = (d * g * s).astype(o_ref.dtype)

    lo = pl.BlockSpec((tr, F), lambda i: (i, 0))
    hi = pl.BlockSpec((tr, F), lambda i: (i, 1))
    return pl.pallas_call(
        body, name=name, grid=(T // tr,), in_specs=[lo, hi, lo], out_specs=pl.BlockSpec((tr, F2), lambda i: (i, 0)),
        out_shape=jax.ShapeDtypeStruct((T, F2), MXU), compiler_params=_params("parallel"),
    )(gu, gu, dact)


def merge_fwd(gl, abr, pbr, sbr, *, name):
    T, D = abr.shape
    tr = _row_tile(T)

    def body(g0, g1, g2, a_ref, p_ref, s_ref, o_ref):
        m = sum(_sigmoid(g[...].astype(F32)) * b[...].astype(F32) for g, b in ((g0, a_ref), (g1, p_ref), (g2, s_ref)))
        o_ref[...] = m.astype(o_ref.dtype)

    row = pl.BlockSpec((tr, D), lambda i: (i, 0))
    gs = [pl.BlockSpec((tr, D), lambda i, j=j: (i, j)) for j in range(3)]
    return pl.pallas_call(
        body, name=name, grid=(T // tr,), in_specs=gs + [row, row, row], out_specs=row,
        out_shape=jax.ShapeDtypeStruct((T, D), MXU), compiler_params=_params("parallel"),
    )(gl, gl, gl, abr, pbr, sbr)


def merge_bwd(gl, abr, pbr, sbr, dm, *, name):
    T, D = abr.shape
    tr = _row_tile(T)

    def body(g0, g1, g2, a_ref, p_ref, s_ref, dm_ref, da_ref, dp_ref, ds_ref, dg_ref):
        d = dm_ref[...].astype(F32)
        for j, (g_ref, b_ref, db_ref) in enumerate(((g0, a_ref, da_ref), (g1, p_ref, dp_ref), (g2, s_ref, ds_ref))):
            s = _sigmoid(g_ref[...].astype(F32))
            db_ref[...] = (d * s).astype(db_ref.dtype)
            dg_ref[:, j * D:(j + 1) * D] = (d * b_ref[...].astype(F32) * s * (1.0 - s)).astype(dg_ref.dtype)

    row = pl.BlockSpec((tr, D), lambda i: (i, 0))
    gs = [pl.BlockSpec((tr, D), lambda i, j=j: (i, j)) for j in range(3)]
    return pl.pallas_call(
        body, name=name, grid=(T // tr,), in_specs=gs + [row] * 4,
        out_specs=[row] * 3 + [pl.BlockSpec((tr, 3 * D), lambda i: (i, 0))],
        out_shape=[jax.ShapeDtypeStruct((T, D), MXU)] * 3 + [jax.ShapeDtypeStruct((T, 3 * D), MXU)],
        compiler_params=_params("parallel"),
    )(gl, gl, gl, abr, pbr, sbr, dm)


def gnorm_fwd(y, z, w, *, name):
    T, DI = y.shape
    gw = DI // C.G
    tr = _row_tile(T)

    def body(y_ref, z_ref, w_ref, o_ref):
        for g in range(C.G):
            sl = slice(g * gw, (g + 1) * gw)
            zz = z_ref[:, sl]
            v = y_ref[:, sl] * (zz * _sigmoid(zz))
            r = lax.rsqrt(jnp.mean(v * v, axis=-1, keepdims=True) + EPS)
            o_ref[:, sl] = (v * r * w_ref[:, sl]).astype(o_ref.dtype)

    row = pl.BlockSpec((tr, DI), lambda i: (i, 0))
    return pl.pallas_call(
        body, name=name, grid=(T // tr,), in_specs=[row, row, pl.BlockSpec((1, DI), lambda i: (0, 0))],
        out_specs=row, out_shape=jax.ShapeDtypeStruct((T, DI), MXU), compiler_params=_params("parallel"),
    )(y, z, w)


def gnorm_bwd(y, z, w, do, *, name):
    T, DI = y.shape
    gw = DI // C.G
    tr = _row_tile(T)

    def body(y_ref, z_ref, w_ref, do_ref, dy_ref, dz_ref, dw_ref):
        first = pl.program_id(0) == 0
        for g in range(C.G):
            sl = slice(g * gw, (g + 1) * gw)
            zz = z_ref[:, sl]
            yy = y_ref[:, sl]
            s = _sigmoid(zz)
            sz = zz * s
            v = yy * sz
            r = lax.rsqrt(jnp.mean(v * v, axis=-1, keepdims=True) + EPS)
            dov = do_ref[:, sl].astype(F32)
            gg = dov * w_ref[:, sl]
            dot = jnp.mean(gg * v, axis=-1, keepdims=True)
            dv = r * gg - v * (r * r * r * dot)
            dy_ref[:, sl] = dv * sz
            dz_ref[:, sl] = (dv * yy * s * (1.0 + zz * (1.0 - s))).astype(dz_ref.dtype)
            part = jnp.sum(dov * v * r, axis=0, keepdims=True)

            @pl.when(first)
            def _():
                dw_ref[:, sl] = part

            @pl.when(jnp.logical_not(first))
            def _():
                dw_ref[:, sl] += part

    row = pl.BlockSpec((tr, DI), lambda i: (i, 0))
    vec = pl.BlockSpec((1, DI), lambda i: (0, 0))
    return pl.pallas_call(
        body, name=name, grid=(T // tr,), in_specs=[row, row, vec, row], out_specs=[row, row, vec],
        out_shape=[jax.ShapeDtypeStruct((T, DI), F32), jax.ShapeDtypeStruct((T, DI), MXU),
                   jax.ShapeDtypeStruct((1, DI), F32)],
        compiler_params=_params("arbitrary"),
    )(y, z, w, do)


def adamw(w, g, m, v, *, name):
    shape = w.shape
    cols = shape[-1]
    rows = w.size // cols
    w2, g2, m2, v2 = (t.reshape(rows, cols) for t in (w, g, m, v))
    tr = rows if rows * cols * 4 <= (2 << 20) else _pick(rows, (512, 256, 128, 64, 32, 16, 8))
    while tr * cols * 4 > (2 << 20) and tr % 16 == 0:
        tr //= 2
    c1 = 1.0 - ADAM_B1 ** ADAM_STEP
    c2 = 1.0 - ADAM_B2 ** ADAM_STEP

    def body(w_ref, g_ref, m_ref, v_ref, d_ref, nm_ref, nv_ref):
        gv = g_ref[...]
        nm = ADAM_B1 * m_ref[...] + (1.0 - ADAM_B1) * gv
        nv = ADAM_B2 * v_ref[...] + (1.0 - ADAM_B2) * (gv * gv)
        d_ref[...] = -ADAM_LR * ((nm / c1) / (jnp.sqrt(nv / c2) + ADAM_EPS) + ADAM_WD * w_ref[...])
        nm_ref[...] = nm
        nv_ref[...] = nv

    row = pl.BlockSpec((tr, cols), lambda i: (i, 0))
    outs = pl.pallas_call(
        body, name=name, grid=(rows // tr,), in_specs=[row] * 4, out_specs=[row] * 3,
        out_shape=[jax.ShapeDtypeStruct((rows, cols), F32)] * 3, compiler_params=_params("parallel"),
    )(w2, g2, m2, v2)
    return tuple(o.reshape(shape) for o in outs)


def adamw_layers(w, g_layers, m, v, *, name):
    shape = w.shape
    L = shape[0]
    rows, cols = g_layers[0].shape
    w3, m3, v3 = (t.reshape(L, rows, cols) for t in (w, m, v))
    tr = _pick(rows, (512, 256, 128, 64, 32, 16, 8))
    while tr * cols * 4 > (2 << 20) and tr % 16 == 0:
        tr //= 2
    c1 = 1.0 - ADAM_B1 ** ADAM_STEP
    c2 = 1.0 - ADAM_B2 ** ADAM_STEP

    def body(*refs):
        w_ref, m_ref, v_ref = refs[0], refs[1], refs[2]
        g_refs = refs[3:3 + L]
        go_ref, d_ref, nm_ref, nv_ref = refs[3 + L:]
        layer = pl.program_id(0)
        gv = g_refs[0][...]
        for k in range(1, L):
            gv = jnp.where(layer == k, g_refs[k][...], gv)
        nm = ADAM_B1 * m_ref[...] + (1.0 - ADAM_B1) * gv
        nv = ADAM_B2 * v_ref[...] + (1.0 - ADAM_B2) * (gv * gv)
        d_ref[...] = -ADAM_LR * ((nm / c1) / (jnp.sqrt(nv / c2) + ADAM_EPS) + ADAM_WD * w_ref[...])
        go_ref[...] = gv
        nm_ref[...] = nm
        nv_ref[...] = nv

    blk = pl.BlockSpec((None, tr, cols), lambda l, i: (l, i, 0))
    gblks = [pl.BlockSpec((tr, cols), lambda l, i, k=k: (jnp.where(l == k, i, 0), 0)) for k in range(L)]
    outs = pl.pallas_call(
        body, name=name, grid=(L, rows // tr), in_specs=[blk] * 3 + gblks, out_specs=[blk] * 4,
        out_shape=[jax.ShapeDtypeStruct((L, rows, cols), F32)] * 4, compiler_params=_params("parallel", "parallel"),
    )(w3, m3, v3, *g_layers)
    return tuple(o.reshape(shape) for o in outs)


def _attn_masks(i, heads):
    B = C.blk
    row = lax.broadcasted_iota(jnp.int32, (heads * B, 2 * B), 0) & (B - 1)
    col = lax.broadcasted_iota(jnp.int32, (heads * B, 2 * B), 1)
    diff = row + B - col
    return (diff >= 0) & (diff < B) & ((col >= B) | (i > 0))


def _stack_heads(ref, kvh, ppk, lo):
    parts = []
    for pr in range(ppk):
        pair = kvh * ppk + pr
        qp = ref[:, pair * 128:(pair + 1) * 128].astype(F32)
        parts += [jnp.where(lo, qp, 0.0), jnp.where(lo, 0.0, qp)]
    return jnp.concatenate(parts, axis=0)


def _unstack_heads(ref, val, kvh, ppk, lo):
    B = C.blk
    for pr in range(ppk):
        pair = kvh * ppk + pr
        ref[:, pair * 128:(pair + 1) * 128] = jnp.where(lo, val[2 * pr * B:(2 * pr + 1) * B],
                                                        val[(2 * pr + 1) * B:(2 * pr + 2) * B]).astype(ref.dtype)


def _sink_scores(sink_ref, kvh, ppk):
    B = C.blk
    h0 = kvh * 2 * ppk
    return jnp.concatenate([jnp.full((B, 2 * B), sink_ref[0, h], F32) for h in range(h0, h0 + 2 * ppk)], axis=0)


def _kv2(prev_ref, cur_ref, m, lo):
    sl = slice(m * 128, (m + 1) * 128)
    slab = jnp.concatenate([prev_ref[:, sl], cur_ref[:, sl]], axis=0).astype(F32)
    slab = jnp.where(lax.broadcasted_iota(jnp.int32, (slab.shape[0], 1), 0) == 0, 0.0, slab)
    rolled = pltpu.roll(slab, 64, axis=1)
    return jnp.where(lo, slab, rolled), jnp.where(lo, rolled, slab)


def _attn_weights(q4, k2, mask, sink_scores):
    s = _dot_nt(q4, k2) * (C.hd ** -0.5)
    col0 = lax.broadcasted_iota(jnp.int32, (1, s.shape[1]), 1) == 0
    s = jnp.where(col0, sink_scores, jnp.where(mask, s, NEG))
    return jnp.exp(s - jnp.max(s, axis=-1, keepdims=True))


def _split_sums(v, sel):
    hi = v.astype(jnp.bfloat16)
    lo = (v - hi.astype(F32)).astype(jnp.bfloat16)
    sel2 = jnp.concatenate([sel, sel], axis=0).astype(jnp.bfloat16)
    return lax.dot_general(jnp.concatenate([hi, lo], axis=1), sel2, (((1,), (0,)), ((), ())), preferred_element_type=F32)


def _row_sums(p, exact=False):
    ones = jnp.ones((p.shape[1], 128), F32)
    return _split_sums(p, ones) if exact else _dot_nn(p, ones)


def attn_fwd(qkv, sink, *, name):
    T = qkv.shape[0]
    B, AW, KW = C.blk, C.AW, C.KW
    assert C.hd == 64 and KW % 128 == 0 and (C.qh // C.kvh) % 2 == 0
    nb = T // B
    kb = AW // KW
    ppk = C.qh // C.kvh // 2

    def body(q_ref, kc_ref, kp_ref, vc_ref, vp_ref, sink_ref, o_ref):
        i = pl.program_id(0)
        lo = lax.broadcasted_iota(jnp.int32, (1, 128), 1) < 64
        mask = _attn_masks(i, 2 * ppk)
        for m in range(KW // 128):
            k2s = _kv2(kp_ref, kc_ref, m, lo)
            v2s = _kv2(vp_ref, vc_ref, m, lo)
            for par in range(2):
                kvh = 2 * m + par
                q4 = _stack_heads(q_ref, kvh, ppk, lo)
                p = _attn_weights(q4, k2s[par], mask, _sink_scores(sink_ref, kvh, ppk))
                _unstack_heads(o_ref, _dot_nn(p, v2s[par]) / _row_sums(p), kvh, ppk, lo)

    prev = lambda i: jnp.maximum(i - 1, 0)
    return pl.pallas_call(
        body, name=name, grid=(nb,),
        in_specs=[pl.BlockSpec((B, AW), lambda i: (i, 0)),
                  pl.BlockSpec((B, KW), lambda i: (i, kb)), pl.BlockSpec((B, KW), lambda i: (prev(i), kb)),
                  pl.BlockSpec((B, KW), lambda i: (i, kb + 1)), pl.BlockSpec((B, KW), lambda i: (prev(i), kb + 1)),
                  pl.BlockSpec(memory_space=pltpu.SMEM)],
        out_specs=pl.BlockSpec((B, AW), lambda i: (i, 0)),
        out_shape=jax.ShapeDtypeStruct((T, AW), MXU), compiler_params=_params("parallel"),
    )(qkv, qkv, qkv, qkv, qkv, sink)


def attn_bwd(qkv, sink, dout, *, name):
    T = qkv.shape[0]
    B, AW, KW = C.blk, C.AW, C.KW
    nb = T // B
    kb = AW // KW
    ppk = C.qh // C.kvh // 2
    scale = C.hd ** -0.5

    def body(q_ref, kc_ref, kp_ref, vc_ref, vp_ref, sink_ref, do_ref, dq_ref, dk_ref, dv_ref, ds_ref, ck_ref, cv_ref):
        i = pl.program_id(0)

        @pl.when(i == 0)
        def _():
            ck_ref[...] = jnp.zeros_like(ck_ref)
            cv_ref[...] = jnp.zeros_like(cv_ref)
            ds_ref[...] = jnp.zeros_like(ds_ref)

        @pl.when(i < nb)
        def _():
            lane = lax.broadcasted_iota(jnp.int32, (1, 128), 1)
            lo = lane < 64
            hlane = lax.broadcasted_iota(jnp.int32, (1, C.qh), 1)
            mask = _attn_masks(i, 2 * ppk)
            dsink = jnp.zeros((1, C.qh), F32)
            for m in range(KW // 128):
                ksl = slice(m * 128, (m + 1) * 128)
                k2s = _kv2(kp_ref, kc_ref, m, lo)
                v2s = _kv2(vp_ref, vc_ref, m, lo)
                folded = []
                for par in range(2):
                    kvh = 2 * m + par
                    q4 = _stack_heads(q_ref, kvh, ppk, lo)
                    do4 = _stack_heads(do_ref, kvh, ppk, lo)
                    p = _attn_weights(q4, k2s[par], mask, _sink_scores(sink_ref, kvh, ppk))
                    inv = 1.0 / _row_sums(p)
                    p = p * jnp.concatenate([inv, inv], axis=1)
                    dp = _dot_nt(do4, v2s[par])
                    delta = _row_sums(p * dp, exact=True)
                    dsc = p * (dp - jnp.concatenate([delta, delta], axis=1))
                    _unstack_heads(dq_ref, _dot_nn(dsc, k2s[par]) * scale, kvh, ppk, lo)
                    dk2 = _dot_tn(dsc, q4) * scale
                    dv2 = _dot_tn(p, do4)
                    for k in range(2 * ppk):
                        dsh = jnp.sum(dsc[k * B:(k + 1) * B, :128], axis=0, keepdims=True)[:, :1]
                        dsink = dsink + jnp.where(hlane == kvh * 2 * ppk + k, dsh, 0.0)
                    folded.append((dk2 + pltpu.roll(dk2, 64, axis=1), dv2 + pltpu.roll(dv2, 64, axis=1)))
                row0 = lax.broadcasted_iota(jnp.int32, (2 * B, 1), 0) == 0
                dks = jnp.where(row0, 0.0, jnp.where(lo, folded[0][0], folded[1][0]))
                dvs = jnp.where(row0, 0.0, jnp.where(lo, folded[0][1], folded[1][1]))
                dk_ref[:, ksl] = (ck_ref[:, ksl] + dks[:B]).astype(dk_ref.dtype)
                dv_ref[:, ksl] = (cv_ref[:, ksl] + dvs[:B]).astype(dv_ref.dtype)
                ck_ref[:, ksl] = dks[B:]
                cv_ref[:, ksl] = dvs[B:]
            ds_ref[...] += dsink

        @pl.when(i == nb)
        def _():
            dk_ref[...] = ck_ref[...].astype(dk_ref.dtype)
            dv_ref[...] = cv_ref[...].astype(dv_ref.dtype)

    cur = lambda i: jnp.minimum(i, nb - 1)
    prev = lambda i: jnp.maximum(jnp.minimum(i, nb - 1) - 1, 0)
    out = lambda i: jnp.maximum(i - 1, 0)
    return pl.pallas_call(
        body, name=name, grid=(nb + 1,),
        in_specs=[pl.BlockSpec((B, AW), lambda i: (cur(i), 0)),
                  pl.BlockSpec((B, KW), lambda i: (cur(i), kb)), pl.BlockSpec((B, KW), lambda i: (prev(i), kb)),
                  pl.BlockSpec((B, KW), lambda i: (cur(i), kb + 1)), pl.BlockSpec((B, KW), lambda i: (prev(i), kb + 1)),
                  pl.BlockSpec(memory_space=pltpu.SMEM),
                  pl.BlockSpec((B, AW), lambda i: (cur(i), 0))],
        out_specs=[pl.BlockSpec((B, AW), lambda i: (cur(i), 0)),
                   pl.BlockSpec((B, KW), lambda i: (out(i), 0)), pl.BlockSpec((B, KW), lambda i: (out(i), 0)),
                   pl.BlockSpec((1, C.qh), lambda i: (0, 0))],
        out_shape=[jax.ShapeDtypeStruct((T, AW), MXU), jax.ShapeDtypeStruct((T, KW), MXU),
                   jax.ShapeDtypeStruct((T, KW), MXU), jax.ShapeDtypeStruct((1, C.qh), F32)],
        scratch_shapes=[pltpu.VMEM((B, KW), F32), pltpu.VMEM((B, KW), F32)],
        compiler_params=_params("arbitrary"),
    )(qkv, qkv, qkv, qkv, qkv, sink, dout)


POOL_HALO = 16


def _window_sum(e, w, n, forward):
    s, k = e, 1
    while k < w:
        s = s + pltpu.roll(s, (n - k) if forward else k, axis=0)
        k *= 2
    return s


def pool_fwd(u, pw, ps, *, name):
    T, PW = u.shape
    PG = C.PG
    tT = _pick(T, (256, 128))
    hb = tT // POOL_HALO

    def body(u_ref, uh_ref, pw_ref, ps_ref, o_ref, mx_ref):
        i = pl.program_id(0)
        halo = jnp.where(i > 0, uh_ref[...], 0.0)
        ext = jnp.concatenate([halo, u_ref[...]], axis=0)
        t = i * tT + lax.broadcasted_iota(jnp.int32, (tT, 1), 0)
        for g, w in enumerate(POOL_WINDOWS):
            sl = slice(g * PG, (g + 1) * PG)
            s = _window_sum(ext[:, sl], w, tT + POOL_HALO, False)[POOL_HALO:]
            cnt = jnp.minimum(t + 1, w).astype(F32)
            mixed = (s / cnt - u_ref[:, sl]).astype(MXU)
            mx_ref[:, sl] = mixed.astype(mx_ref.dtype)
            o_ref[:, sl] = (_dot_nn(mixed, pw_ref[g]) * ps_ref[:, sl]).astype(o_ref.dtype)

    row = pl.BlockSpec((tT, PW), lambda i: (i, 0))
    return pl.pallas_call(
        body, name=name, grid=(T // tT,),
        in_specs=[row, pl.BlockSpec((POOL_HALO, PW), lambda i: (jnp.maximum(i * hb - 1, 0), 0)),
                  pl.BlockSpec((4, PG, PG), lambda i: (0, 0, 0)), pl.BlockSpec((1, PW), lambda i: (0, 0))],
        out_specs=[row, row], out_shape=[jax.ShapeDtypeStruct((T, PW), MXU)] * 2,
        compiler_params=_params("parallel"),
    )(u, u, pw, ps)


def pool_bwd(dpool, mixed, pw, ps, *, name):
    T, PW = dpool.shape
    PG = C.PG
    tT = _pick(T, (256, 128))
    hb = tT // POOL_HALO
    n = T // tT
    rows = tT + POOL_HALO

    def body(dp_ref, dph_ref, mx_ref, pw_ref, ps_ref, du_ref, dpw_ref, dps_ref):
        i = pl.program_id(0)

        @pl.when(i == 0)
        def _():
            dpw_ref[...] = jnp.zeros_like(dpw_ref)
            dps_ref[...] = jnp.zeros_like(dps_ref)

        halo = jnp.where(i < n - 1, dph_ref[...].astype(F32), 0.0)
        dext = jnp.concatenate([dp_ref[...].astype(F32), halo], axis=0)
        t = i * tT + lax.broadcasted_iota(jnp.int32, (rows, 1), 0)
        for g, w in enumerate(POOL_WINDOWS):
            sl = slice(g * PG, (g + 1) * PG)
            dyg = dext[:, sl] * ps_ref[:, sl]
            dmix = _dot_nt(dyg, pw_ref[g])
            cnt = jnp.minimum(t + 1, w).astype(F32)
            s = _window_sum(dmix / cnt, w, rows, True)
            du_ref[:, sl] = (s[:tT] - dmix[:tT]).astype(du_ref.dtype)
            mb = mx_ref[:, sl]
            dpw_ref[g] += _dot_tn(mb, dyg[:tT])
            dps_ref[:, sl] += jnp.sum(dext[:tT, sl] * _dot_nn(mb, pw_ref[g]), axis=0, keepdims=True)

    row = pl.BlockSpec((tT, PW), lambda i: (i, 0))
    return pl.pallas_call(
        body, name=name, grid=(n,),
        in_specs=[row, pl.BlockSpec((POOL_HALO, PW), lambda i: (jnp.minimum((i + 1) * hb, T // POOL_HALO - 1), 0)), row,
                  pl.BlockSpec((4, PG, PG), lambda i: (0, 0, 0)), pl.BlockSpec((1, PW), lambda i: (0, 0))],
        out_specs=[row, pl.BlockSpec((4, PG, PG), lambda i: (0, 0, 0)), pl.BlockSpec((1, PW), lambda i: (0, 0))],
        out_shape=[jax.ShapeDtypeStruct((T, PW), MXU), jax.ShapeDtypeStruct((4, PG, PG), F32),
                   jax.ShapeDtypeStruct((1, PW), F32)],
        compiler_params=_params("arbitrary"),
    )(dpool, dpool, mixed, pw, ps)


CONV_HALO = 8
CONV_K = 4
CONV_TAP_ROWS = 16


def _conv(x, cw_ref, cb_ref, sl):
    acc = cb_ref[:, sl] + cw_ref[CONV_K - 1:CONV_K, sl] * x
    for k in range(CONV_K - 1):
        acc = acc + cw_ref[k:k + 1, sl] * pltpu.roll(x, CONV_K - 1 - k, axis=0)
    return acc


def conv_fwd(xbc, cw, cb, *, name):
    T, CC = xbc.shape
    tT = _pick(T, (256, 128))
    hb = tT // CONV_HALO
    cs = _pick(CC, (512, 256, 128))

    def body(x_ref, xh_ref, cw_ref, cb_ref, o_ref):
        i = pl.program_id(0)
        for c0 in range(0, CC, cs):
            sl = slice(c0, c0 + cs)
            ext = jnp.concatenate([jnp.where(i > 0, xh_ref[:, sl], 0.0), x_ref[:, sl]], axis=0)
            xc = _conv(ext, cw_ref, cb_ref, sl)[CONV_HALO:]
            o_ref[:, sl] = xc * _sigmoid(xc)

    row = pl.BlockSpec((tT, CC), lambda i: (i, 0))
    return pl.pallas_call(
        body, name=name, grid=(T // tT,),
        in_specs=[row, pl.BlockSpec((CONV_HALO, CC), lambda i: (jnp.maximum(i * hb - 1, 0), 0)),
                  pl.BlockSpec((CONV_K, CC), lambda i: (0, 0)), pl.BlockSpec((1, CC), lambda i: (0, 0))],
        out_specs=row, out_shape=jax.ShapeDtypeStruct((T, CC), F32), compiler_params=_params("parallel"),
    )(xbc, xbc, cw, cb)


def conv_bwd(xbc, cw, cb, dxa, *, name):
    T, CC = xbc.shape
    tT = _pick(T, (256, 128))
    hb = tT // CONV_HALO
    n = T // tT
    rows = tT + 2 * CONV_HALO
    cs = _pick(CC, (512, 256, 128))

    def body(x_ref, xp_ref, xn_ref, cw_ref, cb_ref, d_ref, dn_ref, dx_ref, dcw_ref, dcb_ref):
        i = pl.program_id(0)

        @pl.when(i == 0)
        def _():
            dcw_ref[...] = jnp.zeros_like(dcw_ref)
            dcb_ref[...] = jnp.zeros_like(dcb_ref)

        r = lax.broadcasted_iota(jnp.int32, (rows, 1), 0)
        own = (r >= CONV_HALO) & (r < tT + CONV_HALO)
        for c0 in range(0, CC, cs):
            sl = slice(c0, c0 + cs)
            x = jnp.concatenate([jnp.where(i > 0, xp_ref[:, sl], 0.0), x_ref[:, sl],
                                 jnp.where(i < n - 1, xn_ref[:, sl], 0.0)], axis=0)
            da = jnp.concatenate([jnp.zeros((CONV_HALO, cs), F32), d_ref[:, sl],
                                  jnp.where(i < n - 1, dn_ref[:, sl], 0.0)], axis=0)
            xc = _conv(x, cw_ref, cb_ref, sl)
            sg = _sigmoid(xc)
            dxc = da * sg * (1.0 + xc * (1.0 - sg))
            acc = cw_ref[CONV_K - 1:CONV_K, sl] * dxc
            for k in range(CONV_K - 1):
                acc = acc + cw_ref[k:k + 1, sl] * pltpu.roll(dxc, rows - (CONV_K - 1 - k), axis=0)
            dx_ref[:, sl] = acc[CONV_HALO:tT + CONV_HALO].astype(dx_ref.dtype)
            down = jnp.where(own, dxc, 0.0)
            dcb_ref[:, sl] += jnp.sum(down, axis=0, keepdims=True)
            dcw_ref[CONV_K - 1:CONV_K, sl] += jnp.sum(down * x, axis=0, keepdims=True)
            for k in range(CONV_K - 1):
                dcw_ref[k:k + 1, sl] += jnp.sum(down * pltpu.roll(x, CONV_K - 1 - k, axis=0), axis=0, keepdims=True)

    row = pl.BlockSpec((tT, CC), lambda i: (i, 0))
    prev = pl.BlockSpec((CONV_HALO, CC), lambda i: (jnp.maximum(i * hb - 1, 0), 0))
    nxt = pl.BlockSpec((CONV_HALO, CC), lambda i: (jnp.minimum((i + 1) * hb, T // CONV_HALO - 1), 0))
    return pl.pallas_call(
        body, name=name, grid=(n,),
        in_specs=[row, prev, nxt, pl.BlockSpec((CONV_K, CC), lambda i: (0, 0)), pl.BlockSpec((1, CC), lambda i: (0, 0)),
                  row, nxt],
        out_specs=[row, pl.BlockSpec((CONV_K, CC), lambda i: (0, 0)), pl.BlockSpec((1, CC), lambda i: (0, 0))],
        out_shape=[jax.ShapeDtypeStruct((T, CC), MXU), jax.ShapeDtypeStruct((CONV_K, CC), F32),
                   jax.ShapeDtypeStruct((1, CC), F32)],
        compiler_params=_params("arbitrary"),
    )(xbc, xbc, xbc, cw, cb, dxa, dxa)


def _softplus(x):
    return jnp.maximum(x, 0.0) + jnp.log(1.0 + jnp.exp(-jnp.abs(x)))


def _dot_exact(a, b):
    return lax.dot_general(a, b, (((1,), (0,)), ((), ())), precision=lax.Precision.HIGHEST, preferred_element_type=F32)


def ssd_prep(raw_t, bias, alog, *, name):
    H, T = raw_t.shape
    B = C.blk
    tc = _pick(T, (4 * B, 2 * B, B))

    def body(r_ref, b_ref, al_ref, dt_ref, acs_ref):
        dt = _softplus(r_ref[...] + b_ref[...])
        dt_ref[...] = dt
        dta = dt * (-jnp.exp(al_ref[...]))
        upper = (lax.broadcasted_iota(jnp.int32, (B, B), 0) <= lax.broadcasted_iota(jnp.int32, (B, B), 1)).astype(F32)
        for j in range(tc // B):
            acs_ref[:, j * B:(j + 1) * B] = _dot_exact(dta[:, j * B:(j + 1) * B], upper)

    blk = pl.BlockSpec((H, tc), lambda i: (0, i))
    vec = pl.BlockSpec((H, 1), lambda i: (0, 0))
    return pl.pallas_call(
        body, name=name, grid=(T // tc,), in_specs=[blk, vec, vec], out_specs=[blk, blk],
        out_shape=[jax.ShapeDtypeStruct((H, T), F32)] * 2, compiler_params=_params("parallel"),
    )(raw_t, bias, alog)


def _pair(lo, arr, h0, rows=slice(None)):
    return jnp.where(lo, arr[rows, h0:h0 + 1], arr[rows, h0 + 1:h0 + 2])


def _decay(acs, acs_t, h, causal):
    return jnp.exp(jnp.where(causal, acs[:, h:h + 1] - acs_t[h:h + 1, :], NEG))


def ssd_fwd(xa, dt, acs, acs_t, dskip, *, name):
    T = xa.shape[0]
    B, DI, G, N, HG, P, H = C.blk, C.DI, C.G, C.N, C.HG, C.P, C.H
    assert P == 64 and HG % 2 == 0
    nc = T // B
    W = HG * P

    def body(xa_ref, dt_ref, acs_ref, acst_ref, ds_ref, y_ref, hp_ref, h_scr):
        @pl.when(pl.program_id(0) == 0)
        def _():
            h_scr[...] = jnp.zeros_like(h_scr)

        lo = lax.broadcasted_iota(jnp.int32, (1, 128), 1) < 64
        causal = lax.broadcasted_iota(jnp.int32, (B, B), 0) >= lax.broadcasted_iota(jnp.int32, (B, B), 1)
        dt, acs, acs_t, dsk = dt_ref[...], acs_ref[...], acst_ref[...], ds_ref[...]
        for g in range(G):
            bg = xa_ref[:, DI + g * N:DI + (g + 1) * N]
            cg = xa_ref[:, DI + (G + g) * N:DI + (G + g + 1) * N]
            cb = _dot_nt(cg, bg)
            hg = h_scr[g]
            hp_ref[0, g * N:(g + 1) * N, :] = hg
            yoff = _dot_nn(cg, hg)
            xws, decs = [], []
            for j in range(HG // 2):
                h0 = g * HG + 2 * j
                xsl = slice(h0 * P, (h0 + 2) * P)
                xp = xa_ref[:, xsl]
                ap = _pair(lo, acs, h0)
                alast = _pair(lo, acs, h0, slice(B - 1, B))
                xdt = xp * _pair(lo, dt, h0)
                ys = [_dot_nn(cb * _decay(acs, acs_t, h0 + half, causal), xdt) for half in range(2)]
                y_ref[:, xsl] = (jnp.where(lo, ys[0], ys[1]) + yoff[:, 2 * j * P:(2 * j + 2) * P] * jnp.exp(ap)
                                 + _pair(lo, dsk, h0) * xp)
                xws.append(xdt * jnp.exp(alast - ap))
                decs.append(jnp.exp(alast))
            h_scr[g] = hg * jnp.concatenate(decs, axis=1) + _dot_tn(bg, jnp.concatenate(xws, axis=1))

    tok = lambda w: pl.BlockSpec((B, w), lambda c: (c, 0))
    return pl.pallas_call(
        body, name=name, grid=(nc,),
        in_specs=[tok(C.CC), tok(H), tok(H), pl.BlockSpec((H, B), lambda c: (0, c)), pl.BlockSpec((1, H), lambda c: (0, 0))],
        out_specs=[tok(DI), pl.BlockSpec((1, G * N, W), lambda c: (c, 0, 0))],
        out_shape=[jax.ShapeDtypeStruct((T, DI), F32), jax.ShapeDtypeStruct((nc, G * N, W), F32)],
        scratch_shapes=[pltpu.VMEM((G, N, W), F32)],
        compiler_params=_params("arbitrary"),
    )(xa, dt, acs, acs_t, dskip)


def ssd_bwd(xa, dt, acs, acs_t, dskip, hprev, dy, *, name):
    T = xa.shape[0]
    B, DI, G, N, HG, P, H = C.blk, C.DI, C.G, C.N, C.HG, C.P, C.H
    nc = T // B
    W = HG * P

    def body(xa_ref, dt_ref, acs_ref, acst_ref, ds_ref, hp_ref, dy_ref, dxa_ref, ddt_ref, dacs_ref, dd_ref, dh_scr):
        @pl.when(pl.program_id(0) == 0)
        def _():
            dh_scr[...] = jnp.zeros_like(dh_scr)
            dd_ref[...] = jnp.zeros_like(dd_ref)

        lo = lax.broadcasted_iota(jnp.int32, (1, 128), 1) < 64
        hi = jnp.logical_not(lo)
        causal = lax.broadcasted_iota(jnp.int32, (B, B), 0) >= lax.broadcasted_iota(jnp.int32, (B, B), 1)
        hlane = lax.broadcasted_iota(jnp.int32, (1, H), 1)
        hsub = lax.broadcasted_iota(jnp.int32, (B, 1), 0)
        lastrow = lax.broadcasted_iota(jnp.int32, (B, 1), 0) == B - 1
        dt, acs, acs_t, dsk = dt_ref[...], acs_ref[...], acst_ref[...], ds_ref[...]
        d_acs = jnp.zeros((B, H), F32)
        d_acs_t = jnp.zeros((B, B), F32)
        d_dt = jnp.zeros((B, H), F32)
        d_d = jnp.zeros((1, H), F32)

        lane_r = lax.broadcasted_iota(jnp.int32, (128, 128), 0)
        lane_c = lax.broadcasted_iota(jnp.int32, (128, 128), 1)
        same_half = ((lane_r < 64) == (lane_c < 64)).astype(F32)
        ones_sq = jnp.ones((128, 128), F32)

        def rsum(v):
            r = _split_sums(v, same_half)
            return r[:, 0:1], r[:, 64:65]

        for g in range(G):
            bsl = slice(DI + g * N, DI + (g + 1) * N)
            csl = slice(DI + (G + g) * N, DI + (G + g + 1) * N)
            bg, cg = xa_ref[:, bsl], xa_ref[:, csl]
            cb = _dot_nt(cg, bg)
            hg = hp_ref[0, g * N:(g + 1) * N, :]
            dhg = dh_scr[g]
            yoff = _dot_nn(cg, hg)
            bds = _dot_nn(bg, dhg)
            d_cb = jnp.zeros((B, B), F32)
            dyes, xws, decs = [], [], []
            for j in range(HG // 2):
                h0 = g * HG + 2 * j
                xsl = slice(h0 * P, (h0 + 2) * P)
                psl = slice(2 * j * P, (2 * j + 2) * P)
                xp, dyp = xa_ref[:, xsl], dy_ref[:, xsl]
                dtp = _pair(lo, dt, h0)
                ap = _pair(lo, acs, h0)
                alast = _pair(lo, acs, h0, slice(B - 1, B))
                ea, ew, el = jnp.exp(ap), jnp.exp(alast - ap), jnp.exp(alast)
                xdt = xp * dtp
                halves = []
                for half in range(2):
                    h = h0 + half
                    lm = _decay(acs, acs_t, h, causal)
                    m = cb * lm
                    d_m = _dot_nt(jnp.where(lo if half == 0 else hi, dyp, 0.0), xdt)
                    d_cb = d_cb + d_m * lm
                    wgt = d_m * m
                    d_acs = d_acs + jnp.where(hlane == h, _split_sums(wgt, ones_sq)[:, 0:1], 0.0)
                    w_hi = wgt.astype(jnp.bfloat16)
                    w_lo = (wgt - w_hi.astype(F32)).astype(jnp.bfloat16)
                    col = _dot_tn(jnp.ones((2 * B, 8), jnp.bfloat16), jnp.concatenate([w_hi, w_lo], axis=0))
                    d_acs_t = d_acs_t + jnp.where(hsub == h, col[0:1, :], 0.0)
                    halves.append(_dot_tn(m, dyp))
                bdp = bds[:, psl]
                dxdt = jnp.where(lo, halves[0], halves[1]) + ew * bdp
                dxa_ref[:, xsl] = dtp * dxdt + _pair(lo, dsk, h0) * dyp
                xw = xdt * ew
                terms_dt = rsum(dxdt * xp)
                terms_dd = rsum(dyp * xp)
                terms_off = rsum(dyp * (ea * yoff[:, psl]))
                terms_e = rsum(xw * bdp)
                terms_h = rsum(hg[:, psl] * dhg[:, psl])
                for half in range(2):
                    h = h0 + half
                    sel = hlane == h
                    d_dt = d_dt + jnp.where(sel, terms_dt[half], 0.0)
                    d_d = d_d + jnp.where(sel, jnp.sum(terms_dd[half], axis=0, keepdims=True), 0.0)
                    e_last = jnp.sum(jnp.where(lo if half == 0 else hi, el, 0.0), axis=1, keepdims=True) * (1.0 / P)
                    d_last = (jnp.sum(terms_e[half], axis=0, keepdims=True)
                              + e_last * jnp.sum(terms_h[half], axis=0, keepdims=True))
                    d_acs = d_acs + jnp.where(sel, terms_off[half] - terms_e[half] + jnp.where(lastrow, d_last, 0.0), 0.0)
                dyes.append(dyp * ea)
                xws.append(xw)
                decs.append(el)
            dye = jnp.concatenate(dyes, axis=1)
            xwc = jnp.concatenate(xws, axis=1)
            dxa_ref[:, csl] = _dot_nn(d_cb, bg) + _dot_nt(dye, hg)
            dxa_ref[:, bsl] = _dot_tn(d_cb, cg) + _dot_nt(xwc, dhg)
            dh_scr[g] = dhg * jnp.concatenate(decs, axis=1) + _dot_tn(cg, dye)
        ddt_ref[...] = d_dt
        dacs_ref[...] = d_acs - d_acs_t.T[:, :H]
        dd_ref[...] += d_d

    rev = lambda w: pl.BlockSpec((B, w), lambda c: (nc - 1 - c, 0))
    vec = pl.BlockSpec((1, H), lambda c: (0, 0))
    return pl.pallas_call(
        body, name=name, grid=(nc,),
        in_specs=[rev(C.CC), rev(H), rev(H), pl.BlockSpec((H, B), lambda c: (0, nc - 1 - c)), vec,
                  pl.BlockSpec((1, G * N, W), lambda c: (nc - 1 - c, 0, 0)), rev(DI)],
        out_specs=[rev(C.CC), rev(H), rev(H), vec],
        out_shape=[jax.ShapeDtypeStruct((T, C.CC), F32), jax.ShapeDtypeStruct((T, H), F32),
                   jax.ShapeDtypeStruct((T, H), F32), jax.ShapeDtypeStruct((1, H), F32)],
        scratch_shapes=[pltpu.VMEM((G, N, W), F32)],
        compiler_params=_params("arbitrary"),
    )(xa, dt, acs, acs_t, dskip, hprev, dy)


def ssd_post(ddt, dacs, dt, raw, bias, alog, *, name):
    T, H = ddt.shape
    B = C.blk
    tc = _pick(T, (4 * B, 2 * B, B))

    def body(ddt_ref, dacs_ref, dt_ref, raw_ref, b_ref, al_ref, draw_ref, db_ref, dal_ref):
        @pl.when(pl.program_id(0) == 0)
        def _():
            db_ref[...] = jnp.zeros_like(db_ref)
            dal_ref[...] = jnp.zeros_like(dal_ref)

        a = -jnp.exp(al_ref[...])
        lower = (lax.broadcasted_iota(jnp.int32, (B, B), 0) <= lax.broadcasted_iota(jnp.int32, (B, B), 1)).astype(F32)
        for j in range(tc // B):
            sl = slice(j * B, (j + 1) * B)
            rc = _dot_exact(lower, dacs_ref[sl, :])
            dtv = dt_ref[sl, :]
            draw = (ddt_ref[sl, :] + a * rc) * _sigmoid(raw_ref[sl, :] + b_ref[...])
            draw_ref[sl, :] = draw
            db_ref[...] += jnp.sum(draw, axis=0, keepdims=True)
            dal_ref[...] += jnp.sum(dtv * rc, axis=0, keepdims=True) * a

    blk = pl.BlockSpec((tc, H), lambda i: (i, 0))
    vec = pl.BlockSpec((1, H), lambda i: (0, 0))
    return pl.pallas_call(
        body, name=name, grid=(T // tc,), in_specs=[blk, blk, blk, blk, vec, vec], out_specs=[blk, vec, vec],
        out_shape=[jax.ShapeDtypeStruct((T, H), F32), jax.ShapeDtypeStruct((1, H), F32), jax.ShapeDtypeStruct((1, H), F32)],
        compiler_params=_params("arbitrary"),
    )(ddt, dacs, dt, raw, bias, alog)


MESH = pl.DeviceIdType.MESH
PACK_W = 1024
ANY = pl.BlockSpec(memory_space=pl.ANY)


def _place():
    return lax.axis_index("x"), lax.axis_index("y"), lax.axis_index("c")


def _other_chips(x, y):
    return [(1 - x, y), (x, 1 - y), (1 - x, 1 - y)]


def _remote(src, dst, send_sems, recv_sems, k, to):
    return pltpu.make_async_remote_copy(src_ref=src, dst_ref=dst, send_sem=send_sems.at[k], recv_sem=recv_sems.at[k],
                                        device_id=to, device_id_type=MESH)


def _half(c, rows):
    rh = rows // 2
    return pl.ds(pl.multiple_of(c * rh, 16 if rh % 16 == 0 else 8), rh)


def gather_layer(shards, *, name):
    n = len(shards)

    def body(*refs):
        ps, gs = refs[:n], refs[n:2 * n]
        send_sems, recv_sems = refs[2 * n:]
        x, y, c = _place()
        s = 2 * x + y
        sib = (x, y, 1 - c)
        chips = _other_chips(x, y)
        copy = functools.partial(_remote, send_sems=send_sems, recv_sems=recv_sems)
        started = []
        for w in range(n):
            started.append(copy(ps[w], gs[w].at[s], k=w, to=sib))
        for w in range(n):
            mine = _half(c, ps[w].shape[0])
            for j, (px, py) in enumerate(chips):
                started.append(copy(ps[w].at[mine], gs[w].at[s, mine], k=n + 3 * w + j, to=(px, py, c)))
        for cp in started:
            cp.start()
        passed = []
        for w in range(n):
            mine = _half(c, ps[w].shape[0])
            for j, (px, py) in enumerate(chips):
                there = gs[w].at[2 * px + py, mine]
                copy(there, there, k=n + 3 * w + j, to=(px, py, c)).wait_recv()
                fw = copy(there, there, k=4 * n + 3 * w + j, to=sib)
                fw.start()
                passed.append(fw)
        for w in range(n):
            copy(ps[w], gs[w].at[s], k=w, to=sib).wait_recv()
            theirs = _half(1 - c, ps[w].shape[0])
            for j, (px, py) in enumerate(chips):
                there = gs[w].at[2 * px + py, theirs]
                copy(there, there, k=4 * n + 3 * w + j, to=sib).wait_recv()
        for cp in started + passed:
            cp.wait_send()

    return pl.pallas_call(
        body, name=name, in_specs=[ANY] * n, out_specs=[ANY] * n,
        out_shape=[jax.ShapeDtypeStruct((4,) + p.shape, p.dtype) for p in shards],
        scratch_shapes=[pltpu.SemaphoreType.DMA((7 * n,)), pltpu.SemaphoreType.DMA((7 * n,))],
    )(*shards)


HBM = pl.BlockSpec(memory_space=pltpu.HBM)
SEMS = pl.BlockSpec(memory_space=pltpu.SEMAPHORE)


def gather_plan(x, y, c, ps, gs):
    s = 2 * x + y
    sends, lands = [], []
    for p, g in zip(ps, gs):
        rows = p.shape[0]
        sends.append((p, g.at[s], (x, y, 1 - c)))
        lands.append(g.at[s])
        for px, py in _other_chips(x, y):
            for pc in (c, 1 - c):
                sends.append((p.at[_half(c, rows)], g.at[s, _half(c, rows)], (px, py, pc)))
                lands.append(g.at[2 * px + py, _half(pc, rows)])
    return sends, lands


def scatter_plan(x, y, c, ps, gs):
    sends, lands = [], []
    for p, g in zip(ps, gs):
        for j, (px, py) in enumerate(_other_chips(x, y)):
            sends.append((p.at[2 * px + py], g.at[j], (px, py, c)))
            lands.append(g.at[j])
    return sends, lands


def _hbm(a):
    return pltpu.with_memory_space_constraint(a, pltpu.HBM)


def exchange_start(srcs, land_shapes, plan, ncopies, *, name):
    n = len(srcs)

    def body(*refs):
        ps, gs = refs[:n], refs[n:2 * n]
        send_sems, recv_sems = refs[2 * n], refs[2 * n + 1]
        token = refs[-1]
        sends, _ = plan(*_place(), ps, gs)
        for k, (src, dst, to) in enumerate(sends):
            _remote(src, dst, send_sems, recv_sems, k, to).start()
        token[...] = jnp.zeros_like(token)

    lands = [_hbm(lax.empty(s.shape, s.dtype)) for s in land_shapes]
    outs = pl.pallas_call(
        body, name=name,
        out_shape=[pltpu.SemaphoreType.DMA((ncopies,)), pltpu.SemaphoreType.DMA((ncopies,))]
        + [pltpu.HBM(a.shape, a.dtype) for a in srcs] + [pltpu.HBM(s.shape, s.dtype) for s in land_shapes]
        + [jax.ShapeDtypeStruct((8, 128), F32)],
        in_specs=[HBM] * (2 * n), out_specs=[SEMS, SEMS] + [HBM] * (2 * n) + [pl.BlockSpec(memory_space=pltpu.VMEM)],
        input_output_aliases={k: 2 + k for k in range(2 * n)},
        compiler_params=pltpu.CompilerParams(has_side_effects=pltpu.SideEffectType.DATAFLOW_SIDE_EFFECTING),
    )(*[_hbm(a) for a in srcs], *lands)
    return outs[0], outs[1], list(outs[2:2 + n]), list(outs[2 + n:2 + 2 * n]), outs[-1]


def exchange_wait(send_sems, recv_sems, srcs, lands, plan, after, *, name, items=None):
    n = len(srcs)
    items = list(range(n)) if items is None else items

    def body(*refs):
        ps, gs = refs[:n], refs[n:2 * n]
        send_sems, recv_sems = refs[2 * n], refs[2 * n + 1]
        sends, arrivals = plan(*_place(), ps, gs)
        per_item = len(sends) // n
        for j, ((src, dst, to), land) in enumerate(zip(sends, arrivals)):
            k = per_item * items[j // per_item] + j % per_item
            _remote(src, dst, send_sems, recv_sems, k, to).wait_send()
            _remote(land, land, send_sems, recv_sems, k, to).wait_recv()

    outs = pl.pallas_call(
        body, name=name,
        out_shape=[pltpu.HBM(a.shape, a.dtype) for a in srcs] + [pltpu.HBM(a.shape, a.dtype) for a in lands],
        in_specs=[HBM] * (2 * n) + [SEMS, SEMS, ANY], out_specs=[HBM] * (2 * n),
        input_output_aliases={k: k for k in range(2 * n)},
        compiler_params=pltpu.CompilerParams(has_side_effects=pltpu.SideEffectType.DATAFLOW_SIDE_EFFECTING),
    )(*srcs, *lands, send_sems, recv_sems, after)
    return list(outs[n:])


def swap_halves(items, *, name):
    n = len(items)

    def body(*refs):
        gs, rs = refs[:n], refs[n:2 * n]
        send_sems, recv_sems = refs[2 * n:]
        x, y, c = _place()
        cps = [_remote(g.at[:, _half(1 - c, g.shape[1])], r, send_sems, recv_sems, k, (x, y, 1 - c))
               for k, (g, r) in enumerate(zip(gs, rs))]
        for cp in cps:
            cp.start()
        for cp in cps:
            cp.wait()

    return pl.pallas_call(
        body, name=name, in_specs=[ANY] * n, out_specs=[ANY] * n,
        out_shape=[jax.ShapeDtypeStruct((4, g.shape[1] // 2, g.shape[2]), g.dtype) for g in items],
        scratch_shapes=[pltpu.SemaphoreType.DMA((n,)), pltpu.SemaphoreType.DMA((n,))],
    )(*items)


def join_halves(items, *, name):
    n = len(items)

    def body(*refs):
        rs, outs = refs[:n], refs[n:2 * n]
        send_sems, recv_sems = refs[2 * n:]
        x, y, c = _place()
        sib = (x, y, 1 - c)
        cps = []
        for k, (r, o) in enumerate(zip(rs, outs)):
            mine = _half(c, r.shape[0])
            cps.append(_remote(r.at[mine], o.at[mine], send_sems, recv_sems, k, sib))
            cps[-1].start()
        for k, (r, o) in enumerate(zip(rs, outs)):
            theirs = _half(1 - c, r.shape[0])
            _remote(r.at[theirs], o.at[theirs], send_sems, recv_sems, k, sib).wait_recv()
        for cp in cps:
            cp.wait_send()

    return pl.pallas_call(
        body, name=name, in_specs=[ANY] * n, out_specs=[ANY] * n,
        out_shape=[jax.ShapeDtypeStruct(r.shape, r.dtype) for r in items],
        input_output_aliases={k: k for k in range(n)},
        scratch_shapes=[pltpu.SemaphoreType.DMA((n,)), pltpu.SemaphoreType.DMA((n,))],
    )(*items)


def gather_all(v, *, name, total):
    rows, W = v.shape

    def body(v_ref, o_ref, *scr):
        buf = scr[0] if total else o_ref
        send_sems, recv_sems = scr[-2], scr[-1]
        x, y, c = _place()
        me = 4 * x + 2 * y + c
        flips = [(k >> 2 & 1, k >> 1 & 1, k & 1) for k in range(1, 8)]
        peers = [((1 - x) if fx else x, (1 - y) if fy else y, (1 - c) if fc else c) for fx, fy, fc in flips]
        out = []
        for k, peer in enumerate(peers):
            cp = pltpu.make_async_remote_copy(src_ref=v_ref, dst_ref=buf.at[me], send_sem=send_sems.at[k],
                                              recv_sem=recv_sems.at[k], device_id=peer, device_id_type=MESH)
            cp.start()
            out.append(cp)
        buf[me] = v_ref[...]
        for k, (px, py, pc) in enumerate(peers):
            pltpu.make_async_remote_copy(src_ref=v_ref, dst_ref=buf.at[4 * px + 2 * py + pc], send_sem=send_sems.at[k],
                                         recv_sem=recv_sems.at[k], device_id=(px, py, pc), device_id_type=MESH).wait_recv()
        for cp in out:
            cp.wait_send()
        if total:
            acc = buf[0]
            for d in range(1, 8):
                acc = acc + buf[d]
            o_ref[...] = acc

    vm = pl.BlockSpec(memory_space=pltpu.VMEM)
    return pl.pallas_call(
        body, name=name, in_specs=[vm], out_specs=vm,
        out_shape=jax.ShapeDtypeStruct((rows, W) if total else (8, rows, W), F32),
        scratch_shapes=([pltpu.VMEM((8, rows, W), F32)] if total else [])
        + [pltpu.SemaphoreType.DMA((7,)), pltpu.SemaphoreType.DMA((7,))],
    )(v)


def pair_sum(g, r, *, name):
    _, R, W = g.shape
    Rh = R // 2
    tr = _pick(Rh, (512, 256, 128, 64, 32, 16))
    nb = Rh // tr

    def body(g_ref, r_ref, o_ref):
        o_ref[...] = (g_ref[...].astype(F32) + r_ref[...].astype(F32)).astype(o_ref.dtype)

    return pl.pallas_call(
        body, name=name, grid=(4, nb),
        in_specs=[pl.BlockSpec((1, tr, W), lambda s, i: (s, lax.axis_index("c") * nb + i, 0)),
                  pl.BlockSpec((1, tr, W), lambda s, i: (s, i, 0))],
        out_specs=pl.BlockSpec((1, tr, W), lambda s, i: (s, i, 0)),
        out_shape=jax.ShapeDtypeStruct((4, Rh, W), g.dtype), compiler_params=_params("parallel", "parallel"),
    )(g, r)


def chip_sum(g, r1, r2, *, name):
    _, R, W = g.shape
    Rh = R // 2
    tr = _pick(Rh, (512, 256, 128, 64, 32, 16))
    nb = Rh // tr

    def body(g_ref, r1_ref, a_ref, b_ref, c_ref, o_ref):
        acc = g_ref[0].astype(F32) + r1_ref[0].astype(F32)
        for ref in (a_ref, b_ref, c_ref):
            acc = acc + ref[0].astype(F32)
        o_ref[...] = acc

    shard = lambda: 2 * lax.axis_index("x") + lax.axis_index("y")
    half = lambda i: lax.axis_index("c") * nb + i
    return pl.pallas_call(
        body, name=name, grid=(nb,),
        in_specs=[pl.BlockSpec((1, tr, W), lambda i: (shard(), half(i), 0)),
                  pl.BlockSpec((1, tr, W), lambda i: (shard(), i, 0))]
        + [pl.BlockSpec((1, tr, W), lambda i, j=j: (j, i, 0)) for j in range(3)],
        out_specs=pl.BlockSpec((tr, W), lambda i: (half(i), 0)),
        out_shape=jax.ShapeDtypeStruct((R, W), F32), compiler_params=_params("parallel"),
    )(g, r1, r2, r2, r2)


BIG = (("w_in", False), ("pool_w", False), ("w_attn_br", False), ("w_pool_br", False), ("w_ssm_br", True),
       ("w_out", True), ("w_gate_up", False), ("w_down", True))


def _join(piece, axis):
    t = jnp.moveaxis(piece, 0, axis)
    shp = t.shape
    return t.reshape(shp[:axis] + (shp[axis] * shp[axis + 1],) + shp[axis + 2:])


def _seg_bounds():
    aw, kw, _, pw, di, cc, h, gd = C.in_widths
    o = [0, aw + 2 * kw]
    for wdt in (pw, di, cc, h, gd):
        o.append(o[-1] + wdt)
    return o


IN_PAD = 640


def _in_segments(blocks):
    w_in = jnp.concatenate([blocks[s] for s in range(4)], axis=1)
    o = _seg_bounds()
    segs = [w_in[:, o[i]:o[i + 1]] for i in range(6)]
    segs[4] = jnp.pad(segs[4], ((0, 0), (0, 128 - C.H)))
    width = sum(t.shape[1] for t in segs)
    pad = -width % IN_PAD
    return segs, jnp.concatenate(segs + [jnp.zeros((w_in.shape[0], pad), w_in.dtype)], axis=1)


def _layer_fwd_mix(x, p, tag):
    nm = lambda s: f"{s}_{tag}"
    H = C.H
    h = rms_fwd(x, p["ln1_w"], name=nm("rms1"))
    wq, wu, wz, wx, wd, wg = p["in_segs"]
    qkv = matmul(h, wq, name=nm("mm_qkv"), out_dtype=MXU)
    u = matmul(h, wu, name=nm("mm_u"))
    z = matmul(h, wz, name=nm("mm_z"))
    xbc = matmul(h, wx, name=nm("mm_xbc"))
    dtp = matmul(h, wd, name=nm("mm_dt"))
    gl = matmul(h, wg, name=nm("mm_gate"), out_dtype=MXU)
    att = attn_fwd(qkv, p["attn_sink"], name=nm("attn_fwd"))
    pool, mixed = pool_fwd(u, p["pool_w"], p["pool_scale"], name=nm("pool_fwd"))
    xa = conv_fwd(xbc, p["conv_w"], p["conv_b"], name=nm("conv_fwd"))
    raw = dtp[:, :H]
    dt_t, acs_t = ssd_prep(raw.T, p["dt_bias"].T, p["a_log"].T, name=nm("ssd_prep"))
    dt, acs = dt_t.T, acs_t.T
    y, hprev = ssd_fwd(xa, dt, acs, acs_t, p["d_skip"], name=nm("ssd_fwd"))
    ssm = gnorm_fwd(y, z, p["ssm_norm_w"], name=nm("gnorm_fwd"))
    return dict(x=x, h=h, qkv=qkv, z=z, xbc=xbc, raw=raw, gl=gl, att=att, pool=pool, mixed=mixed, xa=xa, dt=dt, acs=acs,
                acs_t=acs_t, y=y, hprev=hprev, ssm=ssm)


def _layer_fwd_out(s, p, tag):
    nm = lambda t: f"{t}_{tag}"
    x, gl, att, pool, ssm = s["x"], s["gl"], s["att"], s["pool"], s["ssm"]
    abr = matmul(att, name=nm("mm_abr"), out_dtype=MXU, **p["w_attn_br"])
    pbr = matmul(pool, name=nm("mm_pbr"), out_dtype=MXU, **p["w_pool_br"])
    sbr = matmul(ssm, name=nm("mm_sbr"), out_dtype=MXU, **p["w_ssm_br"])
    merged = merge_fwd(gl, abr, pbr, sbr, name=nm("merge_fwd"))
    xm = matmul(merged, add=x, name=nm("mm_out"), **p["w_out"])
    h2 = rms_fwd(xm, p["ln2_w"], name=nm("rms2"))
    gu = matmul(h2, name=nm("mm_gu"), out_dtype=MXU, **p["w_gate_up"])
    act = swiglu_fwd(gu, name=nm("swiglu_fwd"))
    xo = matmul(act, add=xm, name=nm("mm_down"), **p["w_down"])
    return xo, dict(s, abr=abr, pbr=pbr, sbr=sbr, merged=merged, xm=xm, h2=h2, gu=gu, act=act)


def _layer_bwd(dxo, dxo_m, p, s, tag):
    nm = lambda t: f"{t}_{tag}"
    H = C.H
    g = {}

    def rows4(t):
        return t.reshape(4, t.shape[0] // 4, t.shape[1])

    g["w_down"] = rows4(matmul(s["act"], dxo_m, ta=True, out_dtype=MXU, name=nm("mmg_down")))
    dact = matmul(dxo_m, tb=True, out_dtype=MXU, name=nm("mmb_down"), **p["w_down"])
    dgu = swiglu_bwd(s["gu"], dact, name=nm("swiglu_bwd"))
    g["w_gate_up"] = matmul(s["h2"], dgu, ta=True, out_dtype=MXU, out_cols=True, name=nm("mmg_gu"))
    dh2 = matmul(dgu, tb=True, name=nm("mmb_gu"), **p["w_gate_up"])
    dxm, dxm_m, g["ln2_w"] = rms_bwd(s["xm"], p["ln2_w"], dh2, dxo, name=nm("rms2_bwd"))
    g["w_out"] = rows4(matmul(s["merged"], dxm_m, ta=True, out_dtype=MXU, name=nm("mmg_out")))
    dmerged = matmul(dxm_m, tb=True, out_dtype=MXU, name=nm("mmb_out"), **p["w_out"])
    dabr, dpbr, dsbr, dgl = merge_bwd(s["gl"], s["abr"], s["pbr"], s["sbr"], dmerged, name=nm("merge_bwd"))
    g["w_attn_br"] = matmul(s["att"], dabr, ta=True, out_dtype=MXU, out_cols=True, name=nm("mmg_abr"))
    g["w_pool_br"] = matmul(s["pool"], dpbr, ta=True, out_dtype=MXU, out_cols=True, name=nm("mmg_pbr"))
    g["w_ssm_br"] = rows4(matmul(s["ssm"], dsbr, ta=True, out_dtype=MXU, name=nm("mmg_sbr")))
    datt = matmul(dabr, tb=True, out_dtype=MXU, name=nm("mmb_abr"), **p["w_attn_br"])
    dpool = matmul(dpbr, tb=True, out_dtype=MXU, name=nm("mmb_pbr"), **p["w_pool_br"])
    dssm = matmul(dsbr, tb=True, out_dtype=MXU, name=nm("mmb_sbr"), **p["w_ssm_br"])
    dq, dk, dv, g["attn_sink"] = attn_bwd(s["qkv"], p["attn_sink"], datt, name=nm("attn_bwd"))
    du, dpw, g["pool_scale"] = pool_bwd(dpool, s["mixed"], p["pool_w"], p["pool_scale"], name=nm("pool_bwd"))
    pg = dpw.shape[1] // 4
    g["pool_w"] = jnp.moveaxis(dpw.reshape(4, 4, pg, dpw.shape[2]), 1, 0).reshape(4, 4 * pg, dpw.shape[2]).astype(MXU)
    dy, dz, g["ssm_norm_w"] = gnorm_bwd(s["y"], s["z"], p["ssm_norm_w"], dssm, name=nm("gnorm_bwd"))
    dxa, ddt, dacs, g["d_skip"] = ssd_bwd(s["xa"], s["dt"], s["acs"], s["acs_t"], p["d_skip"], s["hprev"], dy,
                                          name=nm("ssd_bwd"))
    draw, g["dt_bias"], g["a_log"] = ssd_post(ddt, dacs, s["dt"], s["raw"], p["dt_bias"], p["a_log"], name=nm("ssd_post"))
    dxbc, g["conv_w"], g["conv_b"] = conv_bwd(s["xbc"], p["conv_w"], p["conv_b"], dxa, name=nm("conv_bwd"))
    return g, dict(dq=dq, dk=dk, dv=dv, du=du, dz=dz, dxbc=dxbc, draw=draw, dgl=dgl, dxm=dxm)


def _layer_bwd_in(c, p, s, tag):
    nm = lambda t: f"{t}_{tag}"
    H = C.H
    dq, dk, dv, du, dz, dxbc, draw, dgl, dxm = (c[k] for k in ("dq", "dk", "dv", "du", "dz", "dxbc", "draw", "dgl", "dxm"))
    w_all = p["in_all"]
    ddtp = jnp.pad(draw, ((0, 0), (0, 128 - H))).astype(MXU)
    parts = [dq, dk, dv, du, dz, dxbc, ddtp, dgl]
    used = sum(t.shape[1] for t in parts)
    dproj = jnp.concatenate(parts + [jnp.zeros((dq.shape[0], w_all.shape[1] - used), MXU)], axis=1)
    dh = matmul(dproj, w_all, tb=True, name=nm("mmb_in"))
    g_all = matmul(s["h"], dproj, ta=True, out_dtype=MXU, name=nm("mmg_in"))
    o = _seg_bounds()
    dt0 = o[4]
    g_in = jnp.concatenate([g_all[:, :dt0 + H], g_all[:, dt0 + 128:dt0 + 128 + (o[6] - o[5])]], axis=1)
    nc = g_in.shape[1] // 4
    g_w_in = jnp.stack([g_in[:, k * nc:(k + 1) * nc] for k in range(4)])
    dx, dx_m, g_ln1 = rms_bwd(s["x"], p["ln1_w"], dh, dxm, name=nm("rms1_bwd"))
    return dx, dx_m, g_w_in, g_ln1


SMALL = ("ln1_w", "attn_sink", "conv_w", "conv_b", "dt_bias", "a_log", "d_skip", "ssm_norm_w", "pool_scale", "ln2_w")
WEIGHTS = ("ln1_w", "w_in", "attn_sink", "conv_w", "conv_b", "dt_bias", "a_log", "d_skip", "ssm_norm_w", "pool_w",
           "pool_scale", "w_attn_br", "w_pool_br", "w_ssm_br", "w_out", "ln2_w", "w_gate_up", "w_down", "final_w")


def _step(x, loss_target, w, m, v):
    depth = C.depth
    xi, yi, ci = _place()
    shard = 2 * xi + yi

    names = [n for n, _ in BIG]
    nbig = len(names)
    cw_cols = w["conv_w"].shape[2]

    def block2d(n, i):
        return w[n][i].astype(MXU).reshape(-1, w[n].shape[-1])

    def mix_shards(i):
        return [block2d("w_in", i), block2d("pool_w", i), w["conv_w"][i].reshape(CONV_TAP_ROWS, -1)]

    def out_shards(i):
        return [block2d(n, i) for n, _ in BIG[2:]]

    def mix_params(i, blocks):
        p = {n: w[n][i][None] for n in SMALL if n != "conv_w"}
        p["in_segs"], p["in_all"] = _in_segments(blocks[0])
        pw = blocks[1]
        p["pool_w"] = _join(pw.reshape(4, 4, pw.shape[1] // 4, pw.shape[2]), 1)
        p["conv_w"] = _join(blocks[2].reshape(4, CONV_K, cw_cols), 1)
        return p

    def out_params(blocks):
        return {n: dict(b=t.reshape(4 * t.shape[1], t.shape[2])) if rw else dict(b=t, b_cols=True)
                for (n, rw), t in zip(BIG[2:], blocks)}

    def gather_start(src, tag):
        return exchange_start(src, [jax.ShapeDtypeStruct((4,) + a.shape, a.dtype) for a in src], gather_plan,
                              7 * len(src), name=f"gather_{tag}_start")

    xs = x[0]
    layers, saved = [], []
    n_mix = len(mix_shards(0))
    mix_blocks = gather_layer(mix_shards(0), name="gather_mix_l0")
    pend_out = gather_start(out_shards(0), "out_l0")
    xs = xs + pend_out[4][0, 0]
    for i in range(depth):
        nxt = None
        if i + 1 < depth:
            nxt = gather_start(mix_shards(i + 1) + out_shards(i + 1), f"l{i + 1}")
            xs = xs + nxt[4][0, 0]
        p = mix_params(i, mix_blocks)
        half = _layer_fwd_mix(xs, p, f"l{i}")
        ss, rs, srcs, lands, _ = pend_out
        items = None if i == 0 else list(range(n_mix, len(srcs) + n_mix))
        p.update(out_params(exchange_wait(ss, rs, srcs, lands, gather_plan, half["ssm"], items=items,
                                          name=f"gather_out_l{i}_wait")))
        layers.append(p)
        xs, sv = _layer_fwd_out(half, p, f"l{i}")
        saved.append(sv)
        if nxt is not None:
            ss, rs, srcs, lands, _ = nxt
            mix_blocks = exchange_wait(ss, rs, srcs[:n_mix], lands[:n_mix], gather_plan, xs, items=list(range(n_mix)),
                                       name=f"gather_mix_l{i + 1}_wait")
            pend_out = (ss, rs, srcs[n_mix:], lands[n_mix:], None)
    loss_part, dx, dx_m, g_final = final_loss(xs, w["final_w"][None], loss_target[0], name="final_loss")

    def pair_sums(i, ns, gs):
        sb = swap_halves(gs, name=f"swap_halves_{ns[0]}_l{i}")
        return sb, [pair_sum(g, r, name=f"pair_sum_{n}_l{i}") for n, g, r in zip(ns, gs, sb)]

    def scatter_start(chip, tag):
        return exchange_start(chip, [jax.ShapeDtypeStruct((3,) + a.shape[1:], a.dtype) for a in chip], scatter_plan,
                              3 * len(chip), name=f"scatter_{tag}_start")

    def scatter_wait(pend, after, tag):
        return exchange_wait(pend[0], pend[1], pend[2], pend[3], scatter_plan, after, name=f"scatter_{tag}_wait")

    grads = [None] * depth
    pieces = {}
    rest = names[1:]
    pend_in = None
    for i in reversed(range(depth)):
        if pend_in is not None:
            dx_m = dx_m + pend_in[0][4][0, 0].astype(dx_m.dtype)
        grads[i], cot = _layer_bwd(dx, dx_m, layers[i], saved[i], f"l{i}")
        g_rest = [grads[i][n] for n in rest]
        sib_rest, chip_rest = pair_sums(i, rest, g_rest)
        pend_rest = scatter_start(chip_rest, f"rest_l{i}")
        cot["draw"] = cot["draw"] + pend_rest[4][0, 0]
        dx, dx_m, g_in, grads[i]["ln1_w"] = _layer_bwd_in(cot, layers[i], saved[i], f"l{i}")
        if pend_in is not None:
            pend, g_up, sib_up = pend_in
            pieces[(i + 1, "w_in")] = (g_up, sib_up[0], scatter_wait(pend, dx, f"in_l{i + 1}")[0])
        for n, g, r, o in zip(rest, g_rest, sib_rest, scatter_wait(pend_rest, dx, f"rest_l{i}")):
            pieces[(i, n)] = (g, r, o)
        sib_in, chip_in = pair_sums(i, ["w_in"], [g_in])
        pend_in = (scatter_start(chip_in, f"in_l{i}"), g_in, sib_in)
    keys = [(i, n) for i in range(depth) for n in names if (i, n) != (0, "w_in")]
    g_first, r_first, o_first = pieces[keys[0]]
    pieces[keys[0]] = (g_first + pend_in[0][4][0, 0].astype(g_first.dtype), r_first, o_first)
    mine = [chip_sum(*pieces[k], name=f"chip_sum_{k[1]}_l{k[0]}") for k in keys]
    reduced = dict(zip(keys, join_halves(mine, name="join_halves")))
    gout = {}

    small = [jnp.stack([grads[i][n].reshape(-1) for i in range(depth)]).reshape(1, -1) for n in SMALL]
    small += [g_final.reshape(1, -1), loss_part.reshape(1, -1)]
    tot = gather_all(_pack_small(small), name="sum_small", total=True)
    parts = _unpack_small(tot, [t.shape[1] for t in small])
    for n, t in zip(SMALL, parts):
        if n == "conv_w":
            cols = w[n].shape[2]
            gout[n] = lax.dynamic_slice_in_dim(t.reshape(depth, CONV_K, -1), shard * cols, cols, axis=2)
        else:
            gout[n] = t.reshape(w[n].shape)
    gout["final_w"] = parts[-2].reshape(w["final_w"].shape)
    loss = parts[-1].reshape(())

    upd = {}
    for n in [t for t in WEIGHTS if t != "w_in"] + ["w_in"]:
        if n == "w_in":
            pend, g_in, sib_in = pend_in
            landed = scatter_wait(pend, upd["w_down"][1], "in_l0")[0]
            mine_in = chip_sum(g_in, sib_in[0], landed, name="chip_sum_w_in_l0")
            reduced[(0, n)] = join_halves([mine_in], name="join_halves_w_in_l0")[0]
        if n in gout:
            upd[n] = (gout[n].reshape(w[n].shape),) + adamw(w[n], gout[n].reshape(w[n].shape), m[n], v[n], name=f"adamw_{n}")
        else:
            upd[n] = adamw_layers(w[n], [reduced[(i, n)] for i in range(depth)], m[n], v[n], name=f"adamw_{n}")
    return (loss, dx[None], *[upd[n][0] for n in WEIGHTS], *[upd[n][1] for n in WEIGHTS],
            *[upd[n][2] for n in WEIGHTS], *[upd[n][3] for n in WEIGHTS])


def _pack_small(parts):
    flat = jnp.concatenate(parts, axis=1)
    rows = -(-flat.shape[1] // PACK_W)
    rows = -(-rows // 8) * 8
    return jnp.pad(flat, ((0, 0), (0, rows * PACK_W - flat.shape[1]))).reshape(rows, PACK_W)


def _unpack_small(buf, sizes):
    flat = buf.reshape(-1)
    out, off = [], 0
    for n in sizes:
        out.append(flat[off:off + n])
        off += n
    return out


def kernel(x, ln1_w, w_in, attn_sink, conv_w, conv_b, dt_bias, a_log, d_skip, ssm_norm_w, pool_w, pool_scale, w_attn_br, w_pool_br, w_ssm_br, w_out, ln2_w, w_gate_up, w_down, final_w, loss_target, m_ln1_w, m_w_in, m_attn_sink, m_conv_w, m_conv_b, m_dt_bias, m_a_log, m_d_skip, m_ssm_norm_w, m_pool_w, m_pool_scale, m_w_attn_br, m_w_pool_br, m_w_ssm_br, m_w_out, m_ln2_w, m_w_gate_up, m_w_down, m_final_w, v_ln1_w, v_w_in, v_attn_sink, v_conv_w, v_conv_b, v_dt_bias, v_a_log, v_d_skip, v_ssm_norm_w, v_pool_w, v_pool_scale, v_w_attn_br, v_w_pool_br, v_w_ssm_br, v_w_out, v_ln2_w, v_w_gate_up, v_w_down, v_final_w):
    w = dict(ln1_w=ln1_w, w_in=w_in, attn_sink=attn_sink, conv_w=conv_w, conv_b=conv_b, dt_bias=dt_bias, a_log=a_log,
             d_skip=d_skip, ssm_norm_w=ssm_norm_w, pool_w=pool_w, pool_scale=pool_scale, w_attn_br=w_attn_br,
             w_pool_br=w_pool_br, w_ssm_br=w_ssm_br, w_out=w_out, ln2_w=ln2_w, w_gate_up=w_gate_up, w_down=w_down,
             final_w=final_w)
    m = dict(ln1_w=m_ln1_w, w_in=m_w_in, attn_sink=m_attn_sink, conv_w=m_conv_w, conv_b=m_conv_b, dt_bias=m_dt_bias,
             a_log=m_a_log, d_skip=m_d_skip, ssm_norm_w=m_ssm_norm_w, pool_w=m_pool_w, pool_scale=m_pool_scale,
             w_attn_br=m_w_attn_br, w_pool_br=m_w_pool_br, w_ssm_br=m_w_ssm_br, w_out=m_w_out, ln2_w=m_ln2_w,
             w_gate_up=m_w_gate_up, w_down=m_w_down, final_w=m_final_w)
    v = dict(ln1_w=v_ln1_w, w_in=v_w_in, attn_sink=v_attn_sink, conv_w=v_conv_w, conv_b=v_conv_b, dt_bias=v_dt_bias,
             a_log=v_a_log, d_skip=v_d_skip, ssm_norm_w=v_ssm_norm_w, pool_w=v_pool_w, pool_scale=v_pool_scale,
             w_attn_br=v_w_attn_br, w_pool_br=v_w_pool_br, w_ssm_br=v_w_ssm_br, w_out=v_w_out, ln2_w=v_ln2_w,
             w_gate_up=v_w_gate_up, w_down=v_w_down, final_w=v_final_w)
    return _step(x, loss_target, w, m, v)
```

```python
import functools

import jax
import jax.numpy as jnp
from jax import lax
from jax.experimental import pallas as pl
from jax.experimental.pallas import tpu as pltpu

F32 = jnp.float32
MXU = jnp.bfloat16
VMEM_LIMIT = 56 * 1024 * 1024
EPS = 1e-6
NEG = -1e30

ADAM_LR, ADAM_B1, ADAM_B2, ADAM_EPS, ADAM_WD, ADAM_STEP = 0.001, 0.9, 0.999, 1e-08, 0.01, 10


class Cfg:
    def __init__(self, d_model=2048, seq=8192, depth=2, q_heads=16, kv_heads=4, head_dim=64,
                 ssm_head_dim=64, ssm_groups=4, d_state=128):
        self.D, self.T, self.depth = d_model, seq, depth
        self.hd, self.qh, self.kvh = head_dim, q_heads, kv_heads
        self.AW, self.KW = q_heads * head_dim, kv_heads * head_dim
        self.blk = 128
        self.PW = d_model // 2
        self.PG = self.PW // 4
        self.DI = d_model
        self.P = ssm_head_dim
        self.H = self.DI // self.P
        self.G = ssm_groups
        self.HG = self.H // self.G
        self.N = d_state
        self.CC = self.DI + 2 * self.G * self.N
        self.F = -(-8 * d_model // (3 * 256)) * 256
        self.in_widths = (self.AW, self.KW, self.KW, self.PW, self.DI, self.CC, self.H, 3 * d_model)
        self.in_cols = sum(self.in_widths)


C = Cfg()
POOL_WINDOWS = (2, 4, 8, 16)


def _pick(n, cands):
    for c in cands:
        if n % c == 0:
            return c
    return n


def _params(*sem):
    return pltpu.CompilerParams(dimension_semantics=sem, vmem_limit_bytes=VMEM_LIMIT)


def _sigmoid(x):
    return 1.0 / (1.0 + jnp.exp(-x))


def _dot(a, b, dims):
    return lax.dot_general(a.astype(MXU), b.astype(MXU), (dims, ((), ())), preferred_element_type=F32)


def _dot_nn(a, b):
    return _dot(a, b, ((1,), (0,)))


def _dot_nt(a, b):
    return _dot(a, b, ((1,), (1,)))


def _dot_tn(a, b):
    return _dot(a, b, ((0,), (0,)))


MM_VMEM = 40 * 1024 * 1024
LANES = 128


def _tile(n, cap):
    best = None
    for d in range(LANES, min(n, cap) + 1, LANES):
        if n % d == 0:
            best = d
    return n if best is None else best


def _divisors(n, cap):
    ds = [d for d in range(LANES, min(n, cap) + 1, LANES) if n % d == 0]
    return ds or [n]


HBM_RATE, MXU_RATE, ACC_RATE, STEP_COST = 3.0e12, 0.9e15, 1.2e13, 0.35e-6
MXU_WIDTH = 256


def _choose_tiles(M, N, K, n_unit, k_unit, sa, sb, so, sadd, ta):
    best = None
    for tm in sorted({_tile(M, 1024), _tile(M, 512)}):
        for tk in _divisors(k_unit, 4096):
            for tn in _divisors(n_unit, 2048):
                nk = K // tk
                vmem = 2 * (tm * tk * sa + tk * tn * sb + tm * tn * (so + sadd)) + (tm * tn * 4 if nk > 1 else 0)
                if vmem > MM_VMEM:
                    continue
                steps = (M // tm) * (N // tn) * nk
                a_bytes = M * K * sa * (N // tn if (nk > 1 or ta) else 1)
                b_bytes = K * N * sb * (M // tm)
                hbm = (a_bytes + b_bytes + M * N * (so + sadd)) / HBM_RATE
                fill = (tn / (-(-tn // MXU_WIDTH) * MXU_WIDTH)) * (tk / (-(-tk // MXU_WIDTH) * MXU_WIDTH))
                mxu = 2.0 * M * N * K / (MXU_RATE * fill)
                acc = steps * tm * tn * 8 / ACC_RATE if nk > 1 else 0.0
                cost = max(hbm, mxu) + acc + steps * STEP_COST
                if best is None or cost < best[0]:
                    best = (cost, tm, tn, tk)
    assert best is not None, (M, N, K)
    return best[1:]


def matmul(a, b, *, name, ta=False, tb=False, out_dtype=F32, add=None, layer=None, b_cols=False, out_cols=False):
    M, K = (a.shape[1], a.shape[0]) if ta else a.shape
    rows, cols = b.shape[-2], (4 if b_cols else 1) * b.shape[-1]
    N, bk = (rows, cols) if tb else (cols, rows)
    assert K == bk, (a.shape, b.shape)
    n_unit = N // 4 if (out_cols or (b_cols and not tb)) else N
    k_unit = K // 4 if (b_cols and tb) else K
    sa, sb, so = a.dtype.itemsize, b.dtype.itemsize, jnp.dtype(out_dtype).itemsize
    sadd = add.dtype.itemsize if add is not None else 0
    tm, tn, tk = _choose_tiles(M, N, K, n_unit, k_unit, sa, sb, so, sadd, ta)
    nk = K // tk
    a_spec = pl.BlockSpec((tk, tm), lambda i, j, k: (k, i)) if ta else pl.BlockSpec((tm, tk), lambda i, j, k: (i, k))
    if b_cols and tb:
        per = k_unit // tk
        b_spec = pl.BlockSpec((None, tn, tk), lambda i, j, k: (k // per, j, k % per))
    elif b_cols:
        per = n_unit // tn
        b_spec = pl.BlockSpec((None, tk, tn), lambda i, j, k: (j // per, k, j % per))
    elif layer is not None:
        b_spec = (pl.BlockSpec((None, tn, tk), lambda i, j, k: (layer, j, k)) if tb
                  else pl.BlockSpec((None, tk, tn), lambda i, j, k: (layer, k, j)))
    else:
        b_spec = pl.BlockSpec((tn, tk), lambda i, j, k: (j, k)) if tb else pl.BlockSpec((tk, tn), lambda i, j, k: (k, j))
    if out_cols:
        per = n_unit // tn
        o_spec = pl.BlockSpec((None, tm, tn), lambda i, j, k: (j // per, i, j % per))
        out_shape = jax.ShapeDtypeStruct((4, M, N // 4), out_dtype)
    else:
        o_spec = pl.BlockSpec((tm, tn), lambda i, j, k: (i, j))
        out_shape = jax.ShapeDtypeStruct((M, N), out_dtype)
    dims = ((0 if ta else 1,), (1 if tb else 0,))
    has_add = add is not None

    def body(*refs):
        a_ref, b_ref = refs[0], refs[1]
        add_ref = refs[2] if has_add else None
        o_ref = refs[3] if has_add else refs[2]
        part = _dot(a_ref[...], b_ref[...], dims)

        def finish(total):
            if has_add:
                total = total + add_ref[...].astype(F32)
            o_ref[...] = total.astype(o_ref.dtype)

        if nk == 1:
            finish(part)
        else:
            acc_ref = refs[-1]
            k = pl.program_id(2)

            @pl.when(k == 0)
            def _():
                acc_ref[...] = part

            @pl.when(k > 0)
            def _():
                acc_ref[...] += part

            @pl.when(k == nk - 1)
            def _():
                finish(acc_ref[...])

    in_specs = [a_spec, b_spec] + ([o_spec] if has_add else [])
    args = (a, b) + ((add,) if has_add else ())
    return pl.pallas_call(
        body, name=name, grid=(M // tm, N // tn, nk), in_specs=in_specs, out_specs=o_spec, out_shape=out_shape,
        scratch_shapes=[pltpu.VMEM((tm, tn), F32)] if nk > 1 else [],
        compiler_params=_params("parallel", "parallel", "arbitrary"),
    )(*args)


def _row_tile(t):
    return _pick(t, (256, 128, 64, 32, 16, 8))


def rms_fwd(x, w, *, name):
    T, D = x.shape
    tr = _row_tile(T)

    def body(x_ref, w_ref, o_ref):
        xv = x_ref[...]
        r = lax.rsqrt(jnp.mean(xv * xv, axis=-1, keepdims=True) + EPS)
        o_ref[...] = (xv * r * w_ref[...]).astype(o_ref.dtype)

    row = pl.BlockSpec((tr, D), lambda i: (i, 0))
    return pl.pallas_call(
        body, name=name, grid=(T // tr,), in_specs=[row, pl.BlockSpec((1, D), lambda i: (0, 0))], out_specs=row,
        out_shape=jax.ShapeDtypeStruct((T, D), MXU), compiler_params=_params("parallel"),
    )(x, w)


def rms_bwd(x, w, dh, dres, *, name):
    T, D = x.shape
    tr = _row_tile(T)

    def body(x_ref, w_ref, dh_ref, dres_ref, dx_ref, dxm_ref, dw_ref):
        xv = x_ref[...]
        dhv = dh_ref[...].astype(F32)
        r = lax.rsqrt(jnp.mean(xv * xv, axis=-1, keepdims=True) + EPS)
        g = dhv * w_ref[...]
        dot = jnp.mean(g * xv, axis=-1, keepdims=True)
        dx = dres_ref[...] + r * g - xv * (r * r * r * dot)
        dx_ref[...] = dx
        dxm_ref[...] = dx.astype(dxm_ref.dtype)
        part = jnp.sum(dhv * xv * r, axis=0, keepdims=True)

        @pl.when(pl.program_id(0) == 0)
        def _():
            dw_ref[...] = part

        @pl.when(pl.program_id(0) > 0)
        def _():
            dw_ref[...] += part

    row = pl.BlockSpec((tr, D), lambda i: (i, 0))
    vec = pl.BlockSpec((1, D), lambda i: (0, 0))
    return pl.pallas_call(
        body, name=name, grid=(T // tr,), in_specs=[row, vec, row, row], out_specs=[row, row, vec],
        out_shape=[jax.ShapeDtypeStruct((T, D), F32), jax.ShapeDtypeStruct((T, D), MXU),
                   jax.ShapeDtypeStruct((1, D), F32)],
        compiler_params=_params("arbitrary"),
    )(x, w, dh, dres)


def final_loss(x, w, target, *, name):
    T, D = x.shape
    tr = _row_tile(T)

    def body(x_ref, w_ref, t_ref, loss_ref, dx_ref, dxm_ref, dw_ref):
        xv = x_ref[...]
        wv = w_ref[...]
        r = lax.rsqrt(jnp.mean(xv * xv, axis=-1, keepdims=True) + EPS)
        err = xv * r * wv - t_ref[...]
        lpart = 0.5 * jnp.sum(jnp.mean(err * err, axis=-1, keepdims=True), axis=0, keepdims=True)
        dy = err * (1.0 / D)
        g = dy * wv
        dot = jnp.mean(g * xv, axis=-1, keepdims=True)
        dx = r * g - xv * (r * r * r * dot)
        dx_ref[...] = dx
        dxm_ref[...] = dx.astype(dxm_ref.dtype)
        part = jnp.sum(dy * xv * r, axis=0, keepdims=True)

        @pl.when(pl.program_id(0) == 0)
        def _():
            dw_ref[...] = part
            loss_ref[...] = lpart

        @pl.when(pl.program_id(0) > 0)
        def _():
            dw_ref[...] += part
            loss_ref[...] += lpart

    row = pl.BlockSpec((tr, D), lambda i: (i, 0))
    vec = pl.BlockSpec((1, D), lambda i: (0, 0))
    one = pl.BlockSpec((1, 1), lambda i: (0, 0))
    return pl.pallas_call(
        body, name=name, grid=(T // tr,), in_specs=[row, vec, row], out_specs=[one, row, row, vec],
        out_shape=[jax.ShapeDtypeStruct((1, 1), F32), jax.ShapeDtypeStruct((T, D), F32),
                   jax.ShapeDtypeStruct((T, D), MXU), jax.ShapeDtypeStruct((1, D), F32)],
        compiler_params=_params("arbitrary"),
    )(x, w, target)


def swiglu_fwd(gu, *, name):
    T, F2 = gu.shape
    F = F2 // 2
    tr = _row_tile(T)

    def body(g_ref, u_ref, o_ref):
        g = g_ref[...].astype(F32)
        o_ref[...] = (g * _sigmoid(g) * u_ref[...].astype(F32)).astype(o_ref.dtype)

    return pl.pallas_call(
        body, name=name, grid=(T // tr,),
        in_specs=[pl.BlockSpec((tr, F), lambda i: (i, 0)), pl.BlockSpec((tr, F), lambda i: (i, 1))],
        out_specs=pl.BlockSpec((tr, F), lambda i: (i, 0)),
        out_shape=jax.ShapeDtypeStruct((T, F), MXU), compiler_params=_params("parallel"),
    )(gu, gu)


def swiglu_bwd(gu, dact, *, name):
    T, F2 = gu.shape
    F = F2 // 2
    tr = _row_tile(T)

    def body(g_ref, u_ref, d_ref, o_ref):
        g = g_ref[...].astype(F32)
        d = d_ref[...].astype(F32)
        s = _sigmoid(g)
        o_ref[:, :F] = (d * u_ref[...].astype(F32) * s * (1.0 + g * (1.0 - s))).astype(o_ref.dtype)
        o_ref[:, F:] = (d * g * s).astype(o_ref.dtype)

    lo = pl.BlockSpec((tr, F), lambda i: (i, 0))
    hi = pl.BlockSpec((tr, F), lambda i: (i, 1))
    return pl.pallas_call(
        body, name=name, grid=(T // tr,), in_specs=[lo, hi, lo], out_specs=pl.BlockSpec((tr, F2), lambda i: (i, 0)),
        out_shape=jax.ShapeDtypeStruct((T, F2), MXU), compiler_params=_params("parallel"),
    )(gu, gu, dact)


def merge_fwd(gl, abr, pbr, sbr, *, name):
    T, D = abr.shape
    tr = _row_tile(T)

    def body(g0, g1, g2, a_ref, p_ref, s_ref, o_ref):
        m = sum(_sigmoid(g[...].astype(F32)) * b[...].astype(F32) for g, b in ((g0, a_ref), (g1, p_ref), (g2, s_ref)))
        o_ref[...] = m.astype(o_ref.dtype)

    row = pl.BlockSpec((tr, D), lambda i: (i, 0))
    gs = [pl.BlockSpec((tr, D), lambda i, j=j: (i, j)) for j in range(3)]
    return pl.pallas_call(
        body, name=name, grid=(T // tr,), in_specs=gs + [row, row, row], out_specs=row,
        out_shape=jax.ShapeDtypeStruct((T, D), MXU), compiler_params=_params("parallel"),
    )(gl, gl, gl, abr, pbr, sbr)


def merge_bwd(gl, abr, pbr, sbr, dm, *, name):
    T, D = abr.shape
    tr = _row_tile(T)

    def body(g0, g1, g2, a_ref, p_ref, s_ref, dm_ref, da_ref, dp_ref, ds_ref, dg_ref):
        d = dm_ref[...].astype(F32)
        for j, (g_ref, b_ref, db_ref) in enumerate(((g0, a_ref, da_ref), (g1, p_ref, dp_ref), (g2, s_ref, ds_ref))):
            s = _sigmoid(g_ref[...].astype(F32))
            db_ref[...] = (d * s).astype(db_ref.dtype)
            dg_ref[:, j * D:(j + 1) * D] = (d * b_ref[...].astype(F32) * s * (1.0 - s)).astype(dg_ref.dtype)

    row = pl.BlockSpec((tr, D), lambda i: (i, 0))
    gs = [pl.BlockSpec((tr, D), lambda i, j=j: (i, j)) for j in range(3)]
    return pl.pallas_call(
        body, name=name, grid=(T // tr,), in_specs=gs + [row] * 4,
        out_specs=[row] * 3 + [pl.BlockSpec((tr, 3 * D), lambda i: (i, 0))],
        out_shape=[jax.ShapeDtypeStruct((T, D), MXU)] * 3 + [jax.ShapeDtypeStruct((T, 3 * D), MXU)],
        compiler_params=_params("parallel"),
    )(gl, gl, gl, abr, pbr, sbr, dm)


def gnorm_fwd(y, z, w, *, name):
    T, DI = y.shape
    gw = DI // C.G
    tr = _row_tile(T)

    def body(y_ref, z_ref, w_ref, o_ref):
        for g in range(C.G):
            sl = slice(g * gw, (g + 1) * gw)
            zz = z_ref[:, sl]
            v = y_ref[:, sl] * (zz * _sigmoid(zz))
            r = lax.rsqrt(jnp.mean(v * v, axis=-1, keepdims=True) + EPS)
            o_ref[:, sl] = (v * r * w_ref[:, sl]).astype(o_ref.dtype)

    row = pl.BlockSpec((tr, DI), lambda i: (i, 0))
    return pl.pallas_call(
        body, name=name, grid=(T // tr,), in_specs=[row, row, pl.BlockSpec((1, DI), lambda i: (0, 0))],
        out_specs=row, out_shape=jax.ShapeDtypeStruct((T, DI), MXU), compiler_params=_params("parallel"),
    )(y, z, w)


def gnorm_bwd(y, z, w, do, *, name):
    T, DI = y.shape
    gw = DI // C.G
    tr = _row_tile(T)

    def body(y_ref, z_ref, w_ref, do_ref, dy_ref, dz_ref, dw_ref):
        first = pl.program_id(0) == 0
        for g in range(C.G):
            sl = slice(g * gw, (g + 1) * gw)
            zz = z_ref[:, sl]
            yy = y_ref[:, sl]
            s = _sigmoid(zz)
            sz = zz * s
            v = yy * sz
            r = lax.rsqrt(jnp.mean(v * v, axis=-1, keepdims=True) + EPS)
            dov = do_ref[:, sl].astype(F32)
            gg = dov * w_ref[:, sl]
            dot = jnp.mean(gg * v, axis=-1, keepdims=True)
            dv = r * gg - v * (r * r * r * dot)
            dy_ref[:, sl] = dv * sz
            dz_ref[:, sl] = (dv * yy * s * (1.0 + zz * (1.0 - s))).astype(dz_ref.dtype)
            part = jnp.sum(dov * v * r, axis=0, keepdims=True)

            @pl.when(first)
            def _():
                dw_ref[:, sl] = part

            @pl.when(jnp.logical_not(first))
            def _():
                dw_ref[:, sl] += part

    row = pl.BlockSpec((tr, DI), lambda i: (i, 0))
    vec = pl.BlockSpec((1, DI), lambda i: (0, 0))
    return pl.pallas_call(
        body, name=name, grid=(T // tr,), in_specs=[row, row, vec, row], out_specs=[row, row, vec],
        out_shape=[jax.ShapeDtypeStruct((T, DI), F32), jax.ShapeDtypeStruct((T, DI), MXU),
                   jax.ShapeDtypeStruct((1, DI), F32)],
        compiler_params=_params("arbitrary"),
    )(y, z, w, do)


def adamw(w, g, m, v, *, name):
    shape = w.shape
    cols = shape[-1]
    rows = w.size // cols
    w2, g2, m2, v2 = (t.reshape(rows, cols) for t in (w, g, m, v))
    tr = rows if rows * cols * 4 <= (2 << 20) else _pick(rows, (512, 256, 128, 64, 32, 16, 8))
    while tr * cols * 4 > (2 << 20) and tr % 16 == 0:
        tr //= 2
    c1 = 1.0 - ADAM_B1 ** ADAM_STEP
    c2 = 1.0 - ADAM_B2 ** ADAM_STEP

    def body(w_ref, g_ref, m_ref, v_ref, d_ref, nm_ref, nv_ref):
        gv = g_ref[...]
        nm = ADAM_B1 * m_ref[...] + (1.0 - ADAM_B1) * gv
        nv = ADAM_B2 * v_ref[...] + (1.0 - ADAM_B2) * (gv * gv)
        d_ref[...] = -ADAM_LR * ((nm / c1) / (jnp.sqrt(nv / c2) + ADAM_EPS) + ADAM_WD * w_ref[...])
        nm_ref[...] = nm
        nv_ref[...] = nv

    row = pl.BlockSpec((tr, cols), lambda i: (i, 0))
    outs = pl.pallas_call(
        body, name=name, grid=(rows // tr,), in_specs=[row] * 4, out_specs=[row] * 3,
        out_shape=[jax.ShapeDtypeStruct((rows, cols), F32)] * 3, compiler_params=_params("parallel"),
    )(w2, g2, m2, v2)
    return tuple(o.reshape(shape) for o in outs)


def adamw_layers(w, g_layers, m, v, *, name):
    shape = w.shape
    L = shape[0]
    rows, cols = g_layers[0].shape
    w3, m3, v3 = (t.reshape(L, rows, cols) for t in (w, m, v))
    tr = _pick(rows, (512, 256, 128, 64, 32, 16, 8))
    while tr * cols * 4 > (2 << 20) and tr % 16 == 0:
        tr //= 2
    c1 = 1.0 - ADAM_B1 ** ADAM_STEP
    c2 = 1.0 - ADAM_B2 ** ADAM_STEP

    def body(*refs):
        w_ref, m_ref, v_ref = refs[0], refs[1], refs[2]
        g_refs = refs[3:3 + L]
        go_ref, d_ref, nm_ref, nv_ref = refs[3 + L:]
        layer = pl.program_id(0)
        gv = g_refs[0][...]
        for k in range(1, L):
            gv = jnp.where(layer == k, g_refs[k][...], gv)
        nm = ADAM_B1 * m_ref[...] + (1.0 - ADAM_B1) * gv
        nv = ADAM_B2 * v_ref[...] + (1.0 - ADAM_B2) * (gv * gv)
        d_ref[...] = -ADAM_LR * ((nm / c1) / (jnp.sqrt(nv / c2) + ADAM_EPS) + ADAM_WD * w_ref[...])
        go_ref[...] = gv
        nm_ref[...] = nm
        nv_ref[...] = nv

    blk = pl.BlockSpec((None, tr, cols), lambda l, i: (l, i, 0))
    gblks = [pl.BlockSpec((tr, cols), lambda l, i, k=k: (jnp.where(l == k, i, 0), 0)) for k in range(L)]
    outs = pl.pallas_call(
        body, name=name, grid=(L, rows // tr), in_specs=[blk] * 3 + gblks, out_specs=[blk] * 4,
        out_shape=[jax.ShapeDtypeStruct((L, rows, cols), F32)] * 4, compiler_params=_params("parallel", "parallel"),
    )(w3, m3, v3, *g_layers)
    return tuple(o.reshape(shape) for o in outs)


def _attn_masks(i, heads):
    B = C.blk
    row = lax.broadcasted_iota(jnp.int32, (heads * B, 2 * B), 0) & (B - 1)
    col = lax.broadcasted_iota(jnp.int32, (heads * B, 2 * B), 1)
    diff = row + B - col
    return (diff >= 0) & (diff < B) & ((col >= B) | (i > 0))


def _stack_heads(ref, kvh, ppk, lo):
    parts = []
    for pr in range(ppk):
        pair = kvh * ppk + pr
        qp = ref[:, pair * 128:(pair + 1) * 128].astype(F32)
        parts += [jnp.where(lo, qp, 0.0), jnp.where(lo, 0.0, qp)]
    return jnp.concatenate(parts, axis=0)


def _unstack_heads(ref, val, kvh, ppk, lo):
    B = C.blk
    for pr in range(ppk):
        pair = kvh * ppk + pr
        ref[:, pair * 128:(pair + 1) * 128] = jnp.where(lo, val[2 * pr * B:(2 * pr + 1) * B],
                                                        val[(2 * pr + 1) * B:(2 * pr + 2) * B]).astype(ref.dtype)


def _sink_scores(sink_ref, kvh, ppk):
    B = C.blk
    h0 = kvh * 2 * ppk
    return jnp.concatenate([jnp.full((B, 2 * B), sink_ref[0, h], F32) for h in range(h0, h0 + 2 * ppk)], axis=0)


def _kv2(prev_ref, cur_ref, m, lo):
    sl = slice(m * 128, (m + 1) * 128)
    slab = jnp.concatenate([prev_ref[:, sl], cur_ref[:, sl]], axis=0).astype(F32)
    slab = jnp.where(lax.broadcasted_iota(jnp.int32, (slab.shape[0], 1), 0) == 0, 0.0, slab)
    rolled = pltpu.roll(slab, 64, axis=1)
    return jnp.where(lo, slab, rolled), jnp.where(lo, rolled, slab)


def _attn_weights(q4, k2, mask, sink_scores):
    s = _dot_nt(q4, k2) * (C.hd ** -0.5)
    col0 = lax.broadcasted_iota(jnp.int32, (1, s.shape[1]), 1) == 0
    s = jnp.where(col0, sink_scores, jnp.where(mask, s, NEG))
    return jnp.exp(s - jnp.max(s, axis=-1, keepdims=True))


def _split_sums(v, sel):
    hi = v.astype(jnp.bfloat16)
    lo = (v - hi.astype(F32)).astype(jnp.bfloat16)
    sel2 = jnp.concatenate([sel, sel], axis=0).astype(jnp.bfloat16)
    return lax.dot_general(jnp.concatenate([hi, lo], axis=1), sel2, (((1,), (0,)), ((), ())), preferred_element_type=F32)


def _row_sums(p, exact=False):
    ones = jnp.ones((p.shape[1], 128), F32)
    return _split_sums(p, ones) if exact else _dot_nn(p, ones)


def attn_fwd(qkv, sink, *, name):
    T = qkv.shape[0]
    B, AW, KW = C.blk, C.AW, C.KW
    assert C.hd == 64 and KW % 128 == 0 and (C.qh // C.kvh) % 2 == 0
    nb = T // B
    kb = AW // KW
    ppk = C.qh // C.kvh // 2

    def body(q_ref, kc_ref, kp_ref, vc_ref, vp_ref, sink_ref, o_ref):
        i = pl.program_id(0)
        lo = lax.broadcasted_iota(jnp.int32, (1, 128), 1) < 64
        mask = _attn_masks(i, 2 * ppk)
        for m in range(KW // 128):
            k2s = _kv2(kp_ref, kc_ref, m, lo)
            v2s = _kv2(vp_ref, vc_ref, m, lo)
            for par in range(2):
                kvh = 2 * m + par
                q4 = _stack_heads(q_ref, kvh, ppk, lo)
                p = _attn_weights(q4, k2s[par], mask, _sink_scores(sink_ref, kvh, ppk))
                _unstack_heads(o_ref, _dot_nn(p, v2s[par]) / _row_sums(p), kvh, ppk, lo)

    prev = lambda i: jnp.maximum(i - 1, 0)
    return pl.pallas_call(
        body, name=name, grid=(nb,),
        in_specs=[pl.BlockSpec((B, AW), lambda i: (i, 0)),
                  pl.BlockSpec((B, KW), lambda i: (i, kb)), pl.BlockSpec((B, KW), lambda i: (prev(i), kb)),
                  pl.BlockSpec((B, KW), lambda i: (i, kb + 1)), pl.BlockSpec((B, KW), lambda i: (prev(i), kb + 1)),
                  pl.BlockSpec(memory_space=pltpu.SMEM)],
        out_specs=pl.BlockSpec((B, AW), lambda i: (i, 0)),
        out_shape=jax.ShapeDtypeStruct((T, AW), MXU), compiler_params=_params("parallel"),
    )(qkv, qkv, qkv, qkv, qkv, sink)


def attn_bwd(qkv, sink, dout, *, name):
    T = qkv.shape[0]
    B, AW, KW = C.blk, C.AW, C.KW
    nb = T // B
    kb = AW // KW
    ppk = C.qh // C.kvh // 2
    scale = C.hd ** -0.5

    def body(q_ref, kc_ref, kp_ref, vc_ref, vp_ref, sink_ref, do_ref, dq_ref, dk_ref, dv_ref, ds_ref, ck_ref, cv_ref):
        i = pl.program_id(0)

        @pl.when(i == 0)
        def _():
            ck_ref[...] = jnp.zeros_like(ck_ref)
            cv_ref[...] = jnp.zeros_like(cv_ref)
            ds_ref[...] = jnp.zeros_like(ds_ref)

        @pl.when(i < nb)
        def _():
            lane = lax.broadcasted_iota(jnp.int32, (1, 128), 1)
            lo = lane < 64
            hlane = lax.broadcasted_iota(jnp.int32, (1, C.qh), 1)
            mask = _attn_masks(i, 2 * ppk)
            dsink = jnp.zeros((1, C.qh), F32)
            for m in range(KW // 128):
                ksl = slice(m * 128, (m + 1) * 128)
                k2s = _kv2(kp_ref, kc_ref, m, lo)
                v2s = _kv2(vp_ref, vc_ref, m, lo)
                folded = []
                for par in range(2):
                    kvh = 2 * m + par
                    q4 = _stack_heads(q_ref, kvh, ppk, lo)
                    do4 = _stack_heads(do_ref, kvh, ppk, lo)
                    p = _attn_weights(q4, k2s[par], mask, _sink_scores(sink_ref, kvh, ppk))
                    inv = 1.0 / _row_sums(p)
                    p = p * jnp.concatenate([inv, inv], axis=1)
                    dp = _dot_nt(do4, v2s[par])
                    delta = _row_sums(p * dp, exact=True)
                    dsc = p * (dp - jnp.concatenate([delta, delta], axis=1))
                    _unstack_heads(dq_ref, _dot_nn(dsc, k2s[par]) * scale, kvh, ppk, lo)
                    dk2 = _dot_tn(dsc, q4) * scale
                    dv2 = _dot_tn(p, do4)
                    for k in range(2 * ppk):
                        dsh = jnp.sum(dsc[k * B:(k + 1) * B, :128], axis=0, keepdims=True)[:, :1]
                        dsink = dsink + jnp.where(hlane == kvh * 2 * ppk + k, dsh, 0.0)
                    folded.append((dk2 + pltpu.roll(dk2, 64, axis=1), dv2 + pltpu.roll(dv2, 64, axis=1)))
                row0 = lax.broadcasted_iota(jnp.int32, (2 * B, 1), 0) == 0
                dks = jnp.where(row0, 0.0, jnp.where(lo, folded[0][0], folded[1][0]))
                dvs = jnp.where(row0, 0.0, jnp.where(lo, folded[0][1], folded[1][1]))
                dk_ref[:, ksl] = (ck_ref[:, ksl] + dks[:B]).astype(dk_ref.dtype)
                dv_ref[:, ksl] = (cv_ref[:, ksl] + dvs[:B]).astype(dv_ref.dtype)
                ck_ref[:, ksl] = dks[B:]
                cv_ref[:, ksl] = dvs[B:]
            ds_ref[...] += dsink

        @pl.when(i == nb)
        def _():
            dk_ref[...] = ck_ref[...].astype(dk_ref.dtype)
            dv_ref[...] = cv_ref[...].astype(dv_ref.dtype)

    cur = lambda i: jnp.minimum(i, nb - 1)
    prev = lambda i: jnp.maximum(jnp.minimum(i, nb - 1) - 1, 0)
    out = lambda i: jnp.maximum(i - 1, 0)
    return pl.pallas_call(
        body, name=name, grid=(nb + 1,),
        in_specs=[pl.BlockSpec((B, AW), lambda i: (cur(i), 0)),
                  pl.BlockSpec((B, KW), lambda i: (cur(i), kb)), pl.BlockSpec((B, KW), lambda i: (prev(i), kb)),
                  pl.BlockSpec((B, KW), lambda i: (cur(i), kb + 1)), pl.BlockSpec((B, KW), lambda i: (prev(i), kb + 1)),
                  pl.BlockSpec(memory_space=pltpu.SMEM),
                  pl.BlockSpec((B, AW), lambda i: (cur(i), 0))],
        out_specs=[pl.BlockSpec((B, AW), lambda i: (cur(i), 0)),
                   pl.BlockSpec((B, KW), lambda i: (out(i), 0)), pl.BlockSpec((B, KW), lambda i: (out(i), 0)),
                   pl.BlockSpec((1, C.qh), lambda i: (0, 0))],
        out_shape=[jax.ShapeDtypeStruct((T, AW), MXU), jax.ShapeDtypeStruct((T, KW), MXU),
                   jax.ShapeDtypeStruct((T, KW), MXU), jax.ShapeDtypeStruct((1, C.qh), F32)],
        scratch_shapes=[pltpu.VMEM((B, KW), F32), pltpu.VMEM((B, KW), F32)],
        compiler_params=_params("arbitrary"),
    )(qkv, qkv, qkv, qkv, qkv, sink, dout)


POOL_HALO = 16


def _window_sum(e, w, n, forward):
    s, k = e, 1
    while k < w:
        s = s + pltpu.roll(s, (n - k) if forward else k, axis=0)
        k *= 2
    return s


def pool_fwd(u, pw, ps, *, name):
    T, PW = u.shape
    PG = C.PG
    tT = _pick(T, (256, 128))
    hb = tT // POOL_HALO

    def body(u_ref, uh_ref, pw_ref, ps_ref, o_ref, mx_ref):
        i = pl.program_id(0)
        halo = jnp.where(i > 0, uh_ref[...], 0.0)
        ext = jnp.concatenate([halo, u_ref[...]], axis=0)
        t = i * tT + lax.broadcasted_iota(jnp.int32, (tT, 1), 0)
        for g, w in enumerate(POOL_WINDOWS):
            sl = slice(g * PG, (g + 1) * PG)
            s = _window_sum(ext[:, sl], w, tT + POOL_HALO, False)[POOL_HALO:]
            cnt = jnp.minimum(t + 1, w).astype(F32)
            mixed = (s / cnt - u_ref[:, sl]).astype(MXU)
            mx_ref[:, sl] = mixed.astype(mx_ref.dtype)
            o_ref[:, sl] = (_dot_nn(mixed, pw_ref[g]) * ps_ref[:, sl]).astype(o_ref.dtype)

    row = pl.BlockSpec((tT, PW), lambda i: (i, 0))
    return pl.pallas_call(
        body, name=name, grid=(T // tT,),
        in_specs=[row, pl.BlockSpec((POOL_HALO, PW), lambda i: (jnp.maximum(i * hb - 1, 0), 0)),
                  pl.BlockSpec((4, PG, PG), lambda i: (0, 0, 0)), pl.BlockSpec((1, PW), lambda i: (0, 0))],
        out_specs=[row, row], out_shape=[jax.ShapeDtypeStruct((T, PW), MXU)] * 2,
        compiler_params=_params("parallel"),
    )(u, u, pw, ps)


def pool_bwd(dpool, mixed, pw, ps, *, name):
    T, PW = dpool.shape
    PG = C.PG
    tT = _pick(T, (256, 128))
    hb = tT // POOL_HALO
    n = T // tT
    rows = tT + POOL_HALO

    def body(dp_ref, dph_ref, mx_ref, pw_ref, ps_ref, du_ref, dpw_ref, dps_ref):
        i = pl.program_id(0)

        @pl.when(i == 0)
        def _():
            dpw_ref[...] = jnp.zeros_like(dpw_ref)
            dps_ref[...] = jnp.zeros_like(dps_ref)

        halo = jnp.where(i < n - 1, dph_ref[...].astype(F32), 0.0)
        dext = jnp.concatenate([dp_ref[...].astype(F32), halo], axis=0)
        t = i * tT + lax.broadcasted_iota(jnp.int32, (rows, 1), 0)
        for g, w in enumerate(POOL_WINDOWS):
            sl = slice(g * PG, (g + 1) * PG)
            dyg = dext[:, sl] * ps_ref[:, sl]
            dmix = _dot_nt(dyg, pw_ref[g])
            cnt = jnp.minimum(t + 1, w).astype(F32)
            s = _window_sum(dmix / cnt, w, rows, True)
            du_ref[:, sl] = (s[:tT] - dmix[:tT]).astype(du_ref.dtype)
            mb = mx_ref[:, sl]
            dpw_ref[g] += _dot_tn(mb, dyg[:tT])
            dps_ref[:, sl] += jnp.sum(dext[:tT, sl] * _dot_nn(mb, pw_ref[g]), axis=0, keepdims=True)

    row = pl.BlockSpec((tT, PW), lambda i: (i, 0))
    return pl.pallas_call(
        body, name=name, grid=(n,),
        in_specs=[row, pl.BlockSpec((POOL_HALO, PW), lambda i: (jnp.minimum((i + 1) * hb, T // POOL_HALO - 1), 0)), row,
                  pl.BlockSpec((4, PG, PG), lambda i: (0, 0, 0)), pl.BlockSpec((1, PW), lambda i: (0, 0))],
        out_specs=[row, pl.BlockSpec((4, PG, PG), lambda i: (0, 0, 0)), pl.BlockSpec((1, PW), lambda i: (0, 0))],
        out_shape=[jax.ShapeDtypeStruct((T, PW), MXU), jax.ShapeDtypeStruct((4, PG, PG), F32),
                   jax.ShapeDtypeStruct((1, PW), F32)],
        compiler_params=_params("arbitrary"),
    )(dpool, dpool, mixed, pw, ps)


CONV_HALO = 8
CONV_K = 4
CONV_TAP_ROWS = 16


def _conv(x, cw_ref, cb_ref, sl):
    acc = cb_ref[:, sl] + cw_ref[CONV_K - 1:CONV_K, sl] * x
    for k in range(CONV_K - 1):
        acc = acc + cw_ref[k:k + 1, sl] * pltpu.roll(x, CONV_K - 1 - k, axis=0)
    return acc


def conv_fwd(xbc, cw, cb, *, name):
    T, CC = xbc.shape
    tT = _pick(T, (256, 128))
    hb = tT // CONV_HALO
    cs = _pick(CC, (512, 256, 128))

    def body(x_ref, xh_ref, cw_ref, cb_ref, o_ref):
        i = pl.program_id(0)
        for c0 in range(0, CC, cs):
            sl = slice(c0, c0 + cs)
            ext = jnp.concatenate([jnp.where(i > 0, xh_ref[:, sl], 0.0), x_ref[:, sl]], axis=0)
            xc = _conv(ext, cw_ref, cb_ref, sl)[CONV_HALO:]
            o_ref[:, sl] = xc * _sigmoid(xc)

    row = pl.BlockSpec((tT, CC), lambda i: (i, 0))
    return pl.pallas_call(
        body, name=name, grid=(T // tT,),
        in_specs=[row, pl.BlockSpec((CONV_HALO, CC), lambda i: (jnp.maximum(i * hb - 1, 0), 0)),
                  pl.BlockSpec((CONV_K, CC), lambda i: (0, 0)), pl.BlockSpec((1, CC), lambda i: (0, 0))],
        out_specs=row, out_shape=jax.ShapeDtypeStruct((T, CC), F32), compiler_params=_params("parallel"),
    )(xbc, xbc, cw, cb)


def conv_bwd(xbc, cw, cb, dxa, *, name):
    T, CC = xbc.shape
    tT = _pick(T, (256, 128))
    hb = tT // CONV_HALO
    n = T // tT
    rows = tT + 2 * CONV_HALO
    cs = _pick(CC, (512, 256, 128))

    def body(x_ref, xp_ref, xn_ref, cw_ref, cb_ref, d_ref, dn_ref, dx_ref, dcw_ref, dcb_ref):
        i = pl.program_id(0)

        @pl.when(i == 0)
        def _():
            dcw_ref[...] = jnp.zeros_like(dcw_ref)
            dcb_ref[...] = jnp.zeros_like(dcb_ref)

        r = lax.broadcasted_iota(jnp.int32, (rows, 1), 0)
        own = (r >= CONV_HALO) & (r < tT + CONV_HALO)
        for c0 in range(0, CC, cs):
            sl = slice(c0, c0 + cs)
            x = jnp.concatenate([jnp.where(i > 0, xp_ref[:, sl], 0.0), x_ref[:, sl],
                                 jnp.where(i < n - 1, xn_ref[:, sl], 0.0)], axis=0)
            da = jnp.concatenate([jnp.zeros((CONV_HALO, cs), F32), d_ref[:, sl],
                                  jnp.where(i < n - 1, dn_ref[:, sl], 0.0)], axis=0)
            xc = _conv(x, cw_ref, cb_ref, sl)
            sg = _sigmoid(xc)
            dxc = da * sg * (1.0 + xc * (1.0 - sg))
            acc = cw_ref[CONV_K - 1:CONV_K, sl] * dxc
            for k in range(CONV_K - 1):
                acc = acc + cw_ref[k:k + 1, sl] * pltpu.roll(dxc, rows - (CONV_K - 1 - k), axis=0)
            dx_ref[:, sl] = acc[CONV_HALO:tT + CONV_HALO].astype(dx_ref.dtype)
            down = jnp.where(own, dxc, 0.0)
            dcb_ref[:, sl] += jnp.sum(down, axis=0, keepdims=True)
            dcw_ref[CONV_K - 1:CONV_K, sl] += jnp.sum(down * x, axis=0, keepdims=True)
            for k in range(CONV_K - 1):
                dcw_ref[k:k + 1, sl] += jnp.sum(down * pltpu.roll(x, CONV_K - 1 - k, axis=0), axis=0, keepdims=True)

    row = pl.BlockSpec((tT, CC), lambda i: (i, 0))
    prev = pl.BlockSpec((CONV_HALO, CC), lambda i: (jnp.maximum(i * hb - 1, 0), 0))
    nxt = pl.BlockSpec((CONV_HALO, CC), lambda i: (jnp.minimum((i + 1) * hb, T // CONV_HALO - 1), 0))
    return pl.pallas_call(
        body, name=name, grid=(n,),
        in_specs=[row, prev, nxt, pl.BlockSpec((CONV_K, CC), lambda i: (0, 0)), pl.BlockSpec((1, CC), lambda i: (0, 0)),
                  row, nxt],
        out_specs=[row, pl.BlockSpec((CONV_K, CC), lambda i: (0, 0)), pl.BlockSpec((1, CC), lambda i: (0, 0))],
        out_shape=[jax.ShapeDtypeStruct((T, CC), MXU), jax.ShapeDtypeStruct((CONV_K, CC), F32),
                   jax.ShapeDtypeStruct((1, CC), F32)],
        compiler_params=_params("arbitrary"),
    )(xbc, xbc, xbc, cw, cb, dxa, dxa)


def _softplus(x):
    return jnp.maximum(x, 0.0) + jnp.log(1.0 + jnp.exp(-jnp.abs(x)))


def _dot_exact(a, b):
    return lax.dot_general(a, b, (((1,), (0,)), ((), ())), precision=lax.Precision.HIGHEST, preferred_element_type=F32)


def ssd_prep(raw_t, bias, alog, *, name):
    H, T = raw_t.shape
    B = C.blk
    tc = _pick(T, (4 * B, 2 * B, B))

    def body(r_ref, b_ref, al_ref, dt_ref, acs_ref):
        dt = _softplus(r_ref[...] + b_ref[...])
        dt_ref[...] = dt
        dta = dt * (-jnp.exp(al_ref[...]))
        upper = (lax.broadcasted_iota(jnp.int32, (B, B), 0) <= lax.broadcasted_iota(jnp.int32, (B, B), 1)).astype(F32)
        for j in range(tc // B):
            acs_ref[:, j * B:(j + 1) * B] = _dot_exact(dta[:, j * B:(j + 1) * B], upper)

    blk = pl.BlockSpec((H, tc), lambda i: (0, i))
    vec = pl.BlockSpec((H, 1), lambda i: (0, 0))
    return pl.pallas_call(
        body, name=name, grid=(T // tc,), in_specs=[blk, vec, vec], out_specs=[blk, blk],
        out_shape=[jax.ShapeDtypeStruct((H, T), F32)] * 2, compiler_params=_params("parallel"),
    )(raw_t, bias, alog)


def _pair(lo, arr, h0, rows=slice(None)):
    return jnp.where(lo, arr[rows, h0:h0 + 1], arr[rows, h0 + 1:h0 + 2])


def _decay(acs, acs_t, h, causal):
    return jnp.exp(jnp.where(causal, acs[:, h:h + 1] - acs_t[h:h + 1, :], NEG))


def ssd_fwd(xa, dt, acs, acs_t, dskip, *, name):
    T = xa.shape[0]
    B, DI, G, N, HG, P, H = C.blk, C.DI, C.G, C.N, C.HG, C.P, C.H
    assert P == 64 and HG % 2 == 0
    nc = T // B
    W = HG * P

    def body(xa_ref, dt_ref, acs_ref, acst_ref, ds_ref, y_ref, hp_ref, h_scr):
        @pl.when(pl.program_id(0) == 0)
        def _():
            h_scr[...] = jnp.zeros_like(h_scr)

        lo = lax.broadcasted_iota(jnp.int32, (1, 128), 1) < 64
        causal = lax.broadcasted_iota(jnp.int32, (B, B), 0) >= lax.broadcasted_iota(jnp.int32, (B, B), 1)
        dt, acs, acs_t, dsk = dt_ref[...], acs_ref[...], acst_ref[...], ds_ref[...]
        for g in range(G):
            bg = xa_ref[:, DI + g * N:DI + (g + 1) * N]
            cg = xa_ref[:, DI + (G + g) * N:DI + (G + g + 1) * N]
            cb = _dot_nt(cg, bg)
            hg = h_scr[g]
            hp_ref[0, g * N:(g + 1) * N, :] = hg
            yoff = _dot_nn(cg, hg)
            xws, decs = [], []
            for j in range(HG // 2):
                h0 = g * HG + 2 * j
                xsl = slice(h0 * P, (h0 + 2) * P)
                xp = xa_ref[:, xsl]
                ap = _pair(lo, acs, h0)
                alast = _pair(lo, acs, h0, slice(B - 1, B))
                xdt = xp * _pair(lo, dt, h0)
                ys = [_dot_nn(cb * _decay(acs, acs_t, h0 + half, causal), xdt) for half in range(2)]
                y_ref[:, xsl] = (jnp.where(lo, ys[0], ys[1]) + yoff[:, 2 * j * P:(2 * j + 2) * P] * jnp.exp(ap)
                                 + _pair(lo, dsk, h0) * xp)
                xws.append(xdt * jnp.exp(alast - ap))
                decs.append(jnp.exp(alast))
            h_scr[g] = hg * jnp.concatenate(decs, axis=1) + _dot_tn(bg, jnp.concatenate(xws, axis=1))

    tok = lambda w: pl.BlockSpec((B, w), lambda c: (c, 0))
    return pl.pallas_call(
        body, name=name, grid=(nc,),
        in_specs=[tok(C.CC), tok(H), tok(H), pl.BlockSpec((H, B), lambda c: (0, c)), pl.BlockSpec((1, H), lambda c: (0, 0))],
        out_specs=[tok(DI), pl.BlockSpec((1, G * N, W), lambda c: (c, 0, 0))],
        out_shape=[jax.ShapeDtypeStruct((T, DI), F32), jax.ShapeDtypeStruct((nc, G * N, W), F32)],
        scratch_shapes=[pltpu.VMEM((G, N, W), F32)],
        compiler_params=_params("arbitrary"),
    )(xa, dt, acs, acs_t, dskip)


def ssd_bwd(xa, dt, acs, acs_t, dskip, hprev, dy, *, name):
    T = xa.shape[0]
    B, DI, G, N, HG, P, H = C.blk, C.DI, C.G, C.N, C.HG, C.P, C.H
    nc = T // B
    W = HG * P

    def body(xa_ref, dt_ref, acs_ref, acst_ref, ds_ref, hp_ref, dy_ref, dxa_ref, ddt_ref, dacs_ref, dd_ref, dh_scr):
        @pl.when(pl.program_id(0) == 0)
        def _():
            dh_scr[...] = jnp.zeros_like(dh_scr)
            dd_ref[...] = jnp.zeros_like(dd_ref)

        lo = lax.broadcasted_iota(jnp.int32, (1, 128), 1) < 64
        hi = jnp.logical_not(lo)
        causal = lax.broadcasted_iota(jnp.int32, (B, B), 0) >= lax.broadcasted_iota(jnp.int32, (B, B), 1)
        hlane = lax.broadcasted_iota(jnp.int32, (1, H), 1)
        hsub = lax.broadcasted_iota(jnp.int32, (B, 1), 0)
        lastrow = lax.broadcasted_iota(jnp.int32, (B, 1), 0) == B - 1
        dt, acs, acs_t, dsk = dt_ref[...], acs_ref[...], acst_ref[...], ds_ref[...]
        d_acs = jnp.zeros((B, H), F32)
        d_acs_t = jnp.zeros((B, B), F32)
        d_dt = jnp.zeros((B, H), F32)
        d_d = jnp.zeros((1, H), F32)

        lane_r = lax.broadcasted_iota(jnp.int32, (128, 128), 0)
        lane_c = lax.broadcasted_iota(jnp.int32, (128, 128), 1)
        same_half = ((lane_r < 64) == (lane_c < 64)).astype(F32)
        ones_sq = jnp.ones((128, 128), F32)

        def rsum(v):
            r = _split_sums(v, same_half)
            return r[:, 0:1], r[:, 64:65]

        for g in range(G):
            bsl = slice(DI + g * N, DI + (g + 1) * N)
            csl = slice(DI + (G + g) * N, DI + (G + g + 1) * N)
            bg, cg = xa_ref[:, bsl], xa_ref[:, csl]
            cb = _dot_nt(cg, bg)
            hg = hp_ref[0, g * N:(g + 1) * N, :]
            dhg = dh_scr[g]
            yoff = _dot_nn(cg, hg)
            bds = _dot_nn(bg, dhg)
            d_cb = jnp.zeros((B, B), F32)
            dyes, xws, decs = [], [], []
            for j in range(HG // 2):
                h0 = g * HG + 2 * j
                xsl = slice(h0 * P, (h0 + 2) * P)
                psl = slice(2 * j * P, (2 * j + 2) * P)
                xp, dyp = xa_ref[:, xsl], dy_ref[:, xsl]
                dtp = _pair(lo, dt, h0)
                ap = _pair(lo, acs, h0)
                alast = _pair(lo, acs, h0, slice(B - 1, B))
                ea, ew, el = jnp.exp(ap), jnp.exp(alast - ap), jnp.exp(alast)
                xdt = xp * dtp
                halves = []
                for half in range(2):
                    h = h0 + half
                    lm = _decay(acs, acs_t, h, causal)
                    m = cb * lm
                    d_m = _dot_nt(jnp.where(lo if half == 0 else hi, dyp, 0.0), xdt)
                    d_cb = d_cb + d_m * lm
                    wgt = d_m * m
                    d_acs = d_acs + jnp.where(hlane == h, _split_sums(wgt, ones_sq)[:, 0:1], 0.0)
                    w_hi = wgt.astype(jnp.bfloat16)
                    w_lo = (wgt - w_hi.astype(F32)).astype(jnp.bfloat16)
                    col = _dot_tn(jnp.ones((2 * B, 8), jnp.bfloat16), jnp.concatenate([w_hi, w_lo], axis=0))
                    d_acs_t = d_acs_t + jnp.where(hsub == h, col[0:1, :], 0.0)
                    halves.append(_dot_tn(m, dyp))
                bdp = bds[:, psl]
                dxdt = jnp.where(lo, halves[0], halves[1]) + ew * bdp
                dxa_ref[:, xsl] = dtp * dxdt + _pair(lo, dsk, h0) * dyp
                xw = xdt * ew
                terms_dt = rsum(dxdt * xp)
                terms_dd = rsum(dyp * xp)
                terms_off = rsum(dyp * (ea * yoff[:, psl]))
                terms_e = rsum(xw * bdp)
                terms_h = rsum(hg[:, psl] * dhg[:, psl])
                for half in range(2):
                    h = h0 + half
                    sel = hlane == h
                    d_dt = d_dt + jnp.where(sel, terms_dt[half], 0.0)
                    d_d = d_d + jnp.where(sel, jnp.sum(terms_dd[half], axis=0, keepdims=True), 0.0)
                    e_last = jnp.sum(jnp.where(lo if half == 0 else hi, el, 0.0), axis=1, keepdims=True) * (1.0 / P)
                    d_last = (jnp.sum(terms_e[half], axis=0, keepdims=True)
                              + e_last * jnp.sum(terms_h[half], axis=0, keepdims=True))
                    d_acs = d_acs + jnp.where(sel, terms_off[half] - terms_e[half] + jnp.where(lastrow, d_last, 0.0), 0.0)
                dyes.append(dyp * ea)
                xws.append(xw)
                decs.append(el)
            dye = jnp.concatenate(dyes, axis=1)
            xwc = jnp.concatenate(xws, axis=1)
            dxa_ref[:, csl] = _dot_nn(d_cb, bg) + _dot_nt(dye, hg)
            dxa_ref[:, bsl] = _dot_tn(d_cb, cg) + _dot_nt(xwc, dhg)
            dh_scr[g] = dhg * jnp.concatenate(decs, axis=1) + _dot_tn(cg, dye)
        ddt_ref[...] = d_dt
        dacs_ref[...] = d_acs - d_acs_t.T[:, :H]
        dd_ref[...] += d_d

    rev = lambda w: pl.BlockSpec((B, w), lambda c: (nc - 1 - c, 0))
    vec = pl.BlockSpec((1, H), lambda c: (0, 0))
    return pl.pallas_call(
        body, name=name, grid=(nc,),
        in_specs=[rev(C.CC), rev(H), rev(H), pl.BlockSpec((H, B), lambda c: (0, nc - 1 - c)), vec,
                  pl.BlockSpec((1, G * N, W), lambda c: (nc - 1 - c, 0, 0)), rev(DI)],
        out_specs=[rev(C.CC), rev(H), rev(H), vec],
        out_shape=[jax.ShapeDtypeStruct((T, C.CC), F32), jax.ShapeDtypeStruct((T, H), F32),
                   jax.ShapeDtypeStruct((T, H), F32), jax.ShapeDtypeStruct((1, H), F32)],
        scratch_shapes=[pltpu.VMEM((G, N, W), F32)],
        compiler_params=_params("arbitrary"),
    )(xa, dt, acs, acs_t, dskip, hprev, dy)


def ssd_post(ddt, dacs, dt, raw, bias, alog, *, name):
    T, H = ddt.shape
    B = C.blk
    tc = _pick(T, (4 * B, 2 * B, B))

    def body(ddt_ref, dacs_ref, dt_ref, raw_ref, b_ref, al_ref, draw_ref, db_ref, dal_ref):
        @pl.when(pl.program_id(0) == 0)
        def _():
            db_ref[...] = jnp.zeros_like(db_ref)
            dal_ref[...] = jnp.zeros_like(dal_ref)

        a = -jnp.exp(al_ref[...])
        lower = (lax.broadcasted_iota(jnp.int32, (B, B), 0) <= lax.broadcasted_iota(jnp.int32, (B, B), 1)).astype(F32)
        for j in range(tc // B):
            sl = slice(j * B, (j + 1) * B)
            rc = _dot_exact(lower, dacs_ref[sl, :])
            dtv = dt_ref[sl, :]
            draw = (ddt_ref[sl, :] + a * rc) * _sigmoid(raw_ref[sl, :] + b_ref[...])
            draw_ref[sl, :] = draw
            db_ref[...] += jnp.sum(draw, axis=0, keepdims=True)
            dal_ref[...] += jnp.sum(dtv * rc, axis=0, keepdims=True) * a

    blk = pl.BlockSpec((tc, H), lambda i: (i, 0))
    vec = pl.BlockSpec((1, H), lambda i: (0, 0))
    return pl.pallas_call(
        body, name=name, grid=(T // tc,), in_specs=[blk, blk, blk, blk, vec, vec], out_specs=[blk, vec, vec],
        out_shape=[jax.ShapeDtypeStruct((T, H), F32), jax.ShapeDtypeStruct((1, H), F32), jax.ShapeDtypeStruct((1, H), F32)],
        compiler_params=_params("arbitrary"),
    )(ddt, dacs, dt, raw, bias, alog)


MESH = pl.DeviceIdType.MESH
PACK_W = 1024
ANY = pl.BlockSpec(memory_space=pl.ANY)


def _place():
    return lax.axis_index("x"), lax.axis_index("y"), lax.axis_index("c")


def _other_chips(x, y):
    return [(1 - x, y), (x, 1 - y), (1 - x, 1 - y)]


def _remote(src, dst, send_sems, recv_sems, k, to):
    return pltpu.make_async_remote_copy(src_ref=src, dst_ref=dst, send_sem=send_sems.at[k], recv_sem=recv_sems.at[k],
                                        device_id=to, device_id_type=MESH)


def _half(c, rows):
    rh = rows // 2
    return pl.ds(pl.multiple_of(c * rh, 16 if rh % 16 == 0 else 8), rh)


def gather_layer(shards, *, name):
    n = len(shards)

    def body(*refs):
        ps, gs = refs[:n], refs[n:2 * n]
        send_sems, recv_sems = refs[2 * n:]
        x, y, c = _place()
        s = 2 * x + y
        sib = (x, y, 1 - c)
        chips = _other_chips(x, y)
        copy = functools.partial(_remote, send_sems=send_sems, recv_sems=recv_sems)
        started = []
        for w in range(n):
            started.append(copy(ps[w], gs[w].at[s], k=w, to=sib))
        for w in range(n):
            mine = _half(c, ps[w].shape[0])
            for j, (px, py) in enumerate(chips):
                started.append(copy(ps[w].at[mine], gs[w].at[s, mine], k=n + 3 * w + j, to=(px, py, c)))
        for cp in started:
            cp.start()
        passed = []
        for w in range(n):
            mine = _half(c, ps[w].shape[0])
            for j, (px, py) in enumerate(chips):
                there = gs[w].at[2 * px + py, mine]
                copy(there, there, k=n + 3 * w + j, to=(px, py, c)).wait_recv()
                fw = copy(there, there, k=4 * n + 3 * w + j, to=sib)
                fw.start()
                passed.append(fw)
        for w in range(n):
            copy(ps[w], gs[w].at[s], k=w, to=sib).wait_recv()
            theirs = _half(1 - c, ps[w].shape[0])
            for j, (px, py) in enumerate(chips):
                there = gs[w].at[2 * px + py, theirs]
                copy(there, there, k=4 * n + 3 * w + j, to=sib).wait_recv()
        for cp in started + passed:
            cp.wait_send()

    return pl.pallas_call(
        body, name=name, in_specs=[ANY] * n, out_specs=[ANY] * n,
        out_shape=[jax.ShapeDtypeStruct((4,) + p.shape, p.dtype) for p in shards],
        scratch_shapes=[pltpu.SemaphoreType.DMA((7 * n,)), pltpu.SemaphoreType.DMA((7 * n,))],
    )(*shards)


HBM = pl.BlockSpec(memory_space=pltpu.HBM)
SEMS = pl.BlockSpec(memory_space=pltpu.SEMAPHORE)


def gather_plan(x, y, c, ps, gs):
    s = 2 * x + y
    sends, lands = [], []
    for p, g in zip(ps, gs):
        rows = p.shape[0]
        sends.append((p, g.at[s], (x, y, 1 - c)))
        lands.append(g.at[s])
        for px, py in _other_chips(x, y):
            for pc in (c, 1 - c):
                sends.append((p.at[_half(c, rows)], g.at[s, _half(c, rows)], (px, py, pc)))
                lands.append(g.at[2 * px + py, _half(pc, rows)])
    return sends, lands


def scatter_plan(x, y, c, ps, gs):
    sends, lands = [], []
    for p, g in zip(ps, gs):
        for j, (px, py) in enumerate(_other_chips(x, y)):
            sends.append((p.at[2 * px + py], g.at[j], (px, py, c)))
            lands.append(g.at[j])
    return sends, lands


def _hbm(a):
    return pltpu.with_memory_space_constraint(a, pltpu.HBM)


def exchange_start(srcs, land_shapes, plan, ncopies, *, name):
    n = len(srcs)

    def body(*refs):
        ps, gs = refs[:n], refs[n:2 * n]
        send_sems, recv_sems = refs[2 * n], refs[2 * n + 1]
        token = refs[-1]
        sends, _ = plan(*_place(), ps, gs)
        for k, (src, dst, to) in enumerate(sends):
            _remote(src, dst, send_sems, recv_sems, k, to).start()
        token[...] = jnp.zeros_like(token)

    lands = [_hbm(lax.empty(s.shape, s.dtype)) for s in land_shapes]
    outs = pl.pallas_call(
        body, name=name,
        out_shape=[pltpu.SemaphoreType.DMA((ncopies,)), pltpu.SemaphoreType.DMA((ncopies,))]
        + [pltpu.HBM(a.shape, a.dtype) for a in srcs] + [pltpu.HBM(s.shape, s.dtype) for s in land_shapes]
        + [jax.ShapeDtypeStruct((8, 128), F32)],
        in_specs=[HBM] * (2 * n), out_specs=[SEMS, SEMS] + [HBM] * (2 * n) + [pl.BlockSpec(memory_space=pltpu.VMEM)],
        input_output_aliases={k: 2 + k for k in range(2 * n)},
        compiler_params=pltpu.CompilerParams(has_side_effects=pltpu.SideEffectType.DATAFLOW_SIDE_EFFECTING),
    )(*[_hbm(a) for a in srcs], *lands)
    return outs[0], outs[1], list(outs[2:2 + n]), list(outs[2 + n:2 + 2 * n]), outs[-1]


def exchange_wait(send_sems, recv_sems, srcs, lands, plan, after, *, name, items=None):
    n = len(srcs)
    items = list(range(n)) if items is None else items

    def body(*refs):
        ps, gs = refs[:n], refs[n:2 * n]
        send_sems, recv_sems = refs[2 * n], refs[2 * n + 1]
        sends, arrivals = plan(*_place(), ps, gs)
        per_item = len(sends) // n
        for j, ((src, dst, to), land) in enumerate(zip(sends, arrivals)):
            k = per_item * items[j // per_item] + j % per_item
            _remote(src, dst, send_sems, recv_sems, k, to).wait_send()
            _remote(land, land, send_sems, recv_sems, k, to).wait_recv()

    outs = pl.pallas_call(
        body, name=name,
        out_shape=[pltpu.HBM(a.shape, a.dtype) for a in srcs] + [pltpu.HBM(a.shape, a.dtype) for a in lands],
        in_specs=[HBM] * (2 * n) + [SEMS, SEMS, ANY], out_specs=[HBM] * (2 * n),
        input_output_aliases={k: k for k in range(2 * n)},
        compiler_params=pltpu.CompilerParams(has_side_effects=pltpu.SideEffectType.DATAFLOW_SIDE_EFFECTING),
    )(*srcs, *lands, send_sems, recv_sems, after)
    return list(outs[n:])


def swap_halves(items, *, name):
    n = len(items)

    def body(*refs):
        gs, rs = refs[:n], refs[n:2 * n]
        send_sems, recv_sems = refs[2 * n:]
        x, y, c = _place()
        cps = [_remote(g.at[:, _half(1 - c, g.shape[1])], r, send_sems, recv_sems, k, (x, y, 1 - c))
               for k, (g, r) in enumerate(zip(gs, rs))]
        for cp in cps:
            cp.start()
        for cp in cps:
            cp.wait()

    return pl.pallas_call(
        body, name=name, in_specs=[ANY] * n, out_specs=[ANY] * n,
        out_shape=[jax.ShapeDtypeStruct((4, g.shape[1] // 2, g.shape[2]), g.dtype) for g in items],
        scratch_shapes=[pltpu.SemaphoreType.DMA((n,)), pltpu.SemaphoreType.DMA((n,))],
    )(*items)


def join_halves(items, *, name):
    n = len(items)

    def body(*refs):
        rs, outs = refs[:n], refs[n:2 * n]
        send_sems, recv_sems = refs[2 * n:]
        x, y, c = _place()
        sib = (x, y, 1 - c)
        cps = []
        for k, (r, o) in enumerate(zip(rs, outs)):
            mine = _half(c, r.shape[0])
            cps.append(_remote(r.at[mine], o.at[mine], send_sems, recv_sems, k, sib))
            cps[-1].start()
        for k, (r, o) in enumerate(zip(rs, outs)):
            theirs = _half(1 - c, r.shape[0])
            _remote(r.at[theirs], o.at[theirs], send_sems, recv_sems, k, sib).wait_recv()
        for cp in cps:
            cp.wait_send()

    return pl.pallas_call(
        body, name=name, in_specs=[ANY] * n, out_specs=[ANY] * n,
        out_shape=[jax.ShapeDtypeStruct(r.shape, r.dtype) for r in items],
        input_output_aliases={k: k for k in range(n)},
        scratch_shapes=[pltpu.SemaphoreType.DMA((n,)), pltpu.SemaphoreType.DMA((n,))],
    )(*items)


def gather_all(v, *, name, total):
    rows, W = v.shape

    def body(v_ref, o_ref, *scr):
        buf = scr[0] if total else o_ref
        send_sems, recv_sems = scr[-2], scr[-1]
        x, y, c = _place()
        me = 4 * x + 2 * y + c
        flips = [(k >> 2 & 1, k >> 1 & 1, k & 1) for k in range(1, 8)]
        peers = [((1 - x) if fx else x, (1 - y) if fy else y, (1 - c) if fc else c) for fx, fy, fc in flips]
        out = []
        for k, peer in enumerate(peers):
            cp = pltpu.make_async_remote_copy(src_ref=v_ref, dst_ref=buf.at[me], send_sem=send_sems.at[k],
                                              recv_sem=recv_sems.at[k], device_id=peer, device_id_type=MESH)
            cp.start()
            out.append(cp)
        buf[me] = v_ref[...]
        for k, (px, py, pc) in enumerate(peers):
            pltpu.make_async_remote_copy(src_ref=v_ref, dst_ref=buf.at[4 * px + 2 * py + pc], send_sem=send_sems.at[k],
                                         recv_sem=recv_sems.at[k], device_id=(px, py, pc), device_id_type=MESH).wait_recv()
        for cp in out:
            cp.wait_send()
        if total:
            acc = buf[0]
            for d in range(1, 8):
                acc = acc + buf[d]
            o_ref[...] = acc

    vm = pl.BlockSpec(memory_space=pltpu.VMEM)
    return pl.pallas_call(
        body, name=name, in_specs=[vm], out_specs=vm,
        out_shape=jax.ShapeDtypeStruct((rows, W) if total else (8, rows, W), F32),
        scratch_shapes=([pltpu.VMEM((8, rows, W), F32)] if total else [])
        + [pltpu.SemaphoreType.DMA((7,)), pltpu.SemaphoreType.DMA((7,))],
    )(v)


def pair_sum(g, r, *, name):
    _, R, W = g.shape
    Rh = R // 2
    tr = _pick(Rh, (512, 256, 128, 64, 32, 16))
    nb = Rh // tr

    def body(g_ref, r_ref, o_ref):
        o_ref[...] = (g_ref[...].astype(F32) + r_ref[...].astype(F32)).astype(o_ref.dtype)

    return pl.pallas_call(
        body, name=name, grid=(4, nb),
        in_specs=[pl.BlockSpec((1, tr, W), lambda s, i: (s, lax.axis_index("c") * nb + i, 0)),
                  pl.BlockSpec((1, tr, W), lambda s, i: (s, i, 0))],
        out_specs=pl.BlockSpec((1, tr, W), lambda s, i: (s, i, 0)),
        out_shape=jax.ShapeDtypeStruct((4, Rh, W), g.dtype), compiler_params=_params("parallel", "parallel"),
    )(g, r)


def chip_sum(g, r1, r2, *, name):
    _, R, W = g.shape
    Rh = R // 2
    tr = _pick(Rh, (512, 256, 128, 64, 32, 16))
    nb = Rh // tr

    def body(g_ref, r1_ref, a_ref, b_ref, c_ref, o_ref):
        acc = g_ref[0].astype(F32) + r1_ref[0].astype(F32)
        for ref in (a_ref, b_ref, c_ref):
            acc = acc + ref[0].astype(F32)
        o_ref[...] = acc

    shard = lambda: 2 * lax.axis_index("x") + lax.axis_index("y")
    half = lambda i: lax.axis_index("c") * nb + i
    return pl.pallas_call(
        body, name=name, grid=(nb,),
        in_specs=[pl.BlockSpec((1, tr, W), lambda i: (shard(), half(i), 0)),
                  pl.BlockSpec((1, tr, W), lambda i: (shard(), i, 0))]
        + [pl.BlockSpec((1, tr, W), lambda i, j=j: (j, i, 0)) for j in range(3)],
        out_specs=pl.BlockSpec((tr, W), lambda i: (half(i), 0)),
        out_shape=jax.ShapeDtypeStruct((R, W), F32), compiler_params=_params("parallel"),
    )(g, r1, r2, r2, r2)


BIG = (("w_in", False), ("pool_w", False), ("w_attn_br", False), ("w_pool_br", False), ("w_ssm_br", True),
       ("w_out", True), ("w_gate_up", False), ("w_down", True))


def _join(piece, axis):
    t = jnp.moveaxis(piece, 0, axis)
    shp = t.shape
    return t.reshape(shp[:axis] + (shp[axis] * shp[axis + 1],) + shp[axis + 2:])


def _seg_bounds():
    aw, kw, _, pw, di, cc, h, gd = C.in_widths
    o = [0, aw + 2 * kw]
    for wdt in (pw, di, cc, h, gd):
        o.append(o[-1] + wdt)
    return o


IN_PAD = 640


def _in_segments(blocks):
    w_in = jnp.concatenate([blocks[s] for s in range(4)], axis=1)
    o = _seg_bounds()
    segs = [w_in[:, o[i]:o[i + 1]] for i in range(6)]
    segs[4] = jnp.pad(segs[4], ((0, 0), (0, 128 - C.H)))
    width = sum(t.shape[1] for t in segs)
    pad = -width % IN_PAD
    return segs, jnp.concatenate(segs + [jnp.zeros((w_in.shape[0], pad), w_in.dtype)], axis=1)


def _layer_fwd_mix(x, p, tag):
    nm = lambda s: f"{s}_{tag}"
    H = C.H
    h = rms_fwd(x, p["ln1_w"], name=nm("rms1"))
    wq, wu, wz, wx, wd, wg = p["in_segs"]
    qkv = matmul(h, wq, name=nm("mm_qkv"), out_dtype=MXU)
    u = matmul(h, wu, name=nm("mm_u"))
    z = matmul(h, wz, name=nm("mm_z"))
    xbc = matmul(h, wx, name=nm("mm_xbc"))
    dtp = matmul(h, wd, name=nm("mm_dt"))
    gl = matmul(h, wg, name=nm("mm_gate"), out_dtype=MXU)
    att = attn_fwd(qkv, p["attn_sink"], name=nm("attn_fwd"))
    pool, mixed = pool_fwd(u, p["pool_w"], p["pool_scale"], name=nm("pool_fwd"))
    xa = conv_fwd(xbc, p["conv_w"], p["conv_b"], name=nm("conv_fwd"))
    raw = dtp[:, :H]
    dt_t, acs_t = ssd_prep(raw.T, p["dt_bias"].T, p["a_log"].T, name=nm("ssd_prep"))
    dt, acs = dt_t.T, acs_t.T
    y, hprev = ssd_fwd(xa, dt, acs, acs_t, p["d_skip"], name=nm("ssd_fwd"))
    ssm = gnorm_fwd(y, z, p["ssm_norm_w"], name=nm("gnorm_fwd"))
    return dict(x=x, h=h, qkv=qkv, z=z, xbc=xbc, raw=raw, gl=gl, att=att, pool=pool, mixed=mixed, xa=xa, dt=dt, acs=acs,
                acs_t=acs_t, y=y, hprev=hprev, ssm=ssm)


def _layer_fwd_out(s, p, tag):
    nm = lambda t: f"{t}_{tag}"
    x, gl, att, pool, ssm = s["x"], s["gl"], s["att"], s["pool"], s["ssm"]
    abr = matmul(att, name=nm("mm_abr"), out_dtype=MXU, **p["w_attn_br"])
    pbr = matmul(pool, name=nm("mm_pbr"), out_dtype=MXU, **p["w_pool_br"])
    sbr = matmul(ssm, name=nm("mm_sbr"), out_dtype=MXU, **p["w_ssm_br"])
    merged = merge_fwd(gl, abr, pbr, sbr, name=nm("merge_fwd"))
    xm = matmul(merged, add=x, name=nm("mm_out"), **p["w_out"])
    h2 = rms_fwd(xm, p["ln2_w"], name=nm("rms2"))
    gu = matmul(h2, name=nm("mm_gu"), out_dtype=MXU, **p["w_gate_up"])
    act = swiglu_fwd(gu, name=nm("swiglu_fwd"))
    xo = matmul(act, add=xm, name=nm("mm_down"), **p["w_down"])
    return xo, dict(s, abr=abr, pbr=pbr, sbr=sbr, merged=merged, xm=xm, h2=h2, gu=gu, act=act)


def _layer_bwd(dxo, dxo_m, p, s, tag):
    nm = lambda t: f"{t}_{tag}"
    H = C.H
    g = {}

    def rows4(t):
        return t.reshape(4, t.shape[0] // 4, t.shape[1])

    g["w_down"] = rows4(matmul(s["act"], dxo_m, ta=True, out_dtype=MXU, name=nm("mmg_down")))
    dact = matmul(dxo_m, tb=True, out_dtype=MXU, name=nm("mmb_down"), **p["w_down"])
    dgu = swiglu_bwd(s["gu"], dact, name=nm("swiglu_bwd"))
    g["w_gate_up"] = matmul(s["h2"], dgu, ta=True, out_dtype=MXU, out_cols=True, name=nm("mmg_gu"))
    dh2 = matmul(dgu, tb=True, name=nm("mmb_gu"), **p["w_gate_up"])
    dxm, dxm_m, g["ln2_w"] = rms_bwd(s["xm"], p["ln2_w"], dh2, dxo, name=nm("rms2_bwd"))
    g["w_out"] = rows4(matmul(s["merged"], dxm_m, ta=True, out_dtype=MXU, name=nm("mmg_out")))
    dmerged = matmul(dxm_m, tb=True, out_dtype=MXU, name=nm("mmb_out"), **p["w_out"])
    dabr, dpbr, dsbr, dgl = merge_bwd(s["gl"], s["abr"], s["pbr"], s["sbr"], dmerged, name=nm("merge_bwd"))
    g["w_attn_br"] = matmul(s["att"], dabr, ta=True, out_dtype=MXU, out_cols=True, name=nm("mmg_abr"))
    g["w_pool_br"] = matmul(s["pool"], dpbr, ta=True, out_dtype=MXU, out_cols=True, name=nm("mmg_pbr"))
    g["w_ssm_br"] = rows4(matmul(s["ssm"], dsbr, ta=True, out_dtype=MXU, name=nm("mmg_sbr")))
    datt = matmul(dabr, tb=True, out_dtype=MXU, name=nm("mmb_abr"), **p["w_attn_br"])
    dpool = matmul(dpbr, tb=True, out_dtype=MXU, name=nm("mmb_pbr"), **p["w_pool_br"])
    dssm = matmul(dsbr, tb=True, out_dtype=MXU, name=nm("mmb_sbr"), **p["w_ssm_br"])
    dq, dk, dv, g["attn_sink"] = attn_bwd(s["qkv"], p["attn_sink"], datt, name=nm("attn_bwd"))
    du, dpw, g["pool_scale"] = pool_bwd(dpool, s["mixed"], p["pool_w"], p["pool_scale"], name=nm("pool_bwd"))
    pg = dpw.shape[1] // 4
    g["pool_w"] = jnp.moveaxis(dpw.reshape(4, 4, pg, dpw.shape[2]), 1, 0).reshape(4, 4 * pg, dpw.shape[2]).astype(MXU)
    dy, dz, g["ssm_norm_w"] = gnorm_bwd(s["y"], s["z"], p["ssm_norm_w"], dssm, name=nm("gnorm_bwd"))
    dxa, ddt, dacs, g["d_skip"] = ssd_bwd(s["xa"], s["dt"], s["acs"], s["acs_t"], p["d_skip"], s["hprev"], dy,
                                          name=nm("ssd_bwd"))
    draw, g["dt_bias"], g["a_log"] = ssd_post(ddt, dacs, s["dt"], s["raw"], p["dt_bias"], p["a_log"], name=nm("ssd_post"))
    dxbc, g["conv_w"], g["conv_b"] = conv_bwd(s["xbc"], p["conv_w"], p["conv_b"], dxa, name=nm("conv_bwd"))
    return g, dict(dq=dq, dk=dk, dv=dv, du=du, dz=dz, dxbc=dxbc, draw=draw, dgl=dgl, dxm=dxm)


def _layer_bwd_in(c, p, s, tag):
    nm = lambda t: f"{t}_{tag}"
    H = C.H
    dq, dk, dv, du, dz, dxbc, draw, dgl, dxm = (c[k] for k in ("dq", "dk", "dv", "du", "dz", "dxbc", "draw", "dgl", "dxm"))
    w_all = p["in_all"]
    ddtp = jnp.pad(draw, ((0, 0), (0, 128 - H))).astype(MXU)
    parts = [dq, dk, dv, du, dz, dxbc, ddtp, dgl]
    used = sum(t.shape[1] for t in parts)
    dproj = jnp.concatenate(parts + [jnp.zeros((dq.shape[0], w_all.shape[1] - used), MXU)], axis=1)
    dh = matmul(dproj, w_all, tb=True, name=nm("mmb_in"))
    g_all = matmul(s["h"], dproj, ta=True, out_dtype=MXU, name=nm("mmg_in"))
    o = _seg_bounds()
    dt0 = o[4]
    g_in = jnp.concatenate([g_all[:, :dt0 + H], g_all[:, dt0 + 128:dt0 + 128 + (o[6] - o[5])]], axis=1)
    nc = g_in.shape[1] // 4
    g_w_in = jnp.stack([g_in[:, k * nc:(k + 1) * nc] for k in range(4)])
    dx, dx_m, g_ln1 = rms_bwd(s["x"], p["ln1_w"], dh, dxm, name=nm("rms1_bwd"))
    return dx, dx_m, g_w_in, g_ln1


SMALL = ("ln1_w", "attn_sink", "conv_w", "conv_b", "dt_bias", "a_log", "d_skip", "ssm_norm_w", "pool_scale", "ln2_w")
WEIGHTS = ("ln1_w", "w_in", "attn_sink", "conv_w", "conv_b", "dt_bias", "a_log", "d_skip", "ssm_norm_w", "pool_w",
           "pool_scale", "w_attn_br", "w_pool_br", "w_ssm_br", "w_out", "ln2_w", "w_gate_up", "w_down", "final_w")


def _step(x, loss_target, w, m, v):
    depth = C.depth
    xi, yi, ci = _place()
    shard = 2 * xi + yi

    names = [n for n, _ in BIG]
    nbig = len(names)
    cw_cols = w["conv_w"].shape[2]

    def block2d(n, i):
        return w[n][i].astype(MXU).reshape(-1, w[n].shape[-1])

    def mix_shards(i):
        return [block2d("w_in", i), block2d("pool_w", i), w["conv_w"][i].reshape(CONV_TAP_ROWS, -1)]

    def out_shards(i):
        return [block2d(n, i) for n, _ in BIG[2:]]

    def mix_params(i, blocks):
        p = {n: w[n][i][None] for n in SMALL if n != "conv_w"}
        p["in_segs"], p["in_all"] = _in_segments(blocks[0])
        pw = blocks[1]
        p["pool_w"] = _join(pw.reshape(4, 4, pw.shape[1] // 4, pw.shape[2]), 1)
        p["conv_w"] = _join(blocks[2].reshape(4, CONV_K, cw_cols), 1)
        return p

    def out_params(blocks):
        return {n: dict(b=t.reshape(4 * t.shape[1], t.shape[2])) if rw else dict(b=t, b_cols=True)
                for (n, rw), t in zip(BIG[2:], blocks)}

    def gather_start(src, tag):
        return exchange_start(src, [jax.ShapeDtypeStruct((4,) + a.shape, a.dtype) for a in src], gather_plan,
                              7 * len(src), name=f"gather_{tag}_start")

    xs = x[0]
    layers, saved = [], []
    n_mix = len(mix_shards(0))
    mix_blocks = gather_layer(mix_shards(0), name="gather_mix_l0")

    def after(src, scalar):
        return [src[0] + jnp.minimum(jnp.abs(scalar), 0.0).astype(src[0].dtype)] + src[1:]

    pend_out = gather_start(after(out_shards(0), mix_blocks[2][0, 0, 0]), "out_l0")
    xs = xs + pend_out[4][0, 0]
    for i in range(depth):
        nxt = None
        if i + 1 < depth:
            prev = pend_out[4][0, 0] if i == 0 else xs[0, 0]
            nxt = gather_start(after(mix_shards(i + 1) + out_shards(i + 1), prev), f"l{i + 1}")
            xs = xs + nxt[4][0, 0]
        p = mix_params(i, mix_blocks)
        half = _layer_fwd_mix(xs, p, f"l{i}")
        ss, rs, srcs, lands, _ = pend_out
        items = None if i == 0 else list(range(n_mix, len(srcs) + n_mix))
        p.update(out_params(exchange_wait(ss, rs, srcs, lands, gather_plan, half["ssm"], items=items,
                                          name=f"gather_out_l{i}_wait")))
        layers.append(p)
        xs, sv = _layer_fwd_out(half, p, f"l{i}")
        saved.append(sv)
        if nxt is not None:
            ss, rs, srcs, lands, _ = nxt
            mix_blocks = exchange_wait(ss, rs, srcs[:n_mix], lands[:n_mix], gather_plan, xs, items=list(range(n_mix)),
                                       name=f"gather_mix_l{i + 1}_wait")
            pend_out = (ss, rs, srcs[n_mix:], lands[n_mix:], None)
    loss_part, dx, dx_m, g_final = final_loss(xs, w["final_w"][None], loss_target[0], name="final_loss")

    def pair_sums(i, ns, gs):
        sb = swap_halves(gs, name=f"swap_halves_{ns[0]}_l{i}")
        return sb, [pair_sum(g, r, name=f"pair_sum_{n}_l{i}") for n, g, r in zip(ns, gs, sb)]

    def scatter_start(chip, tag):
        return exchange_start(chip, [jax.ShapeDtypeStruct((3,) + a.shape[1:], a.dtype) for a in chip], scatter_plan,
                              3 * len(chip), name=f"scatter_{tag}_start")

    def scatter_wait(pend, after, tag):
        return exchange_wait(pend[0], pend[1], pend[2], pend[3], scatter_plan, after, name=f"scatter_{tag}_wait")

    grads = [None] * depth
    pieces = {}
    rest = names[1:]
    pend_in = None
    for i in reversed(range(depth)):
        if pend_in is not None:
            dx_m = dx_m + pend_in[0][4][0, 0].astype(dx_m.dtype)
        grads[i], cot = _layer_bwd(dx, dx_m, layers[i], saved[i], f"l{i}")
        g_rest = [grads[i][n] for n in rest]
        sib_rest, chip_rest = pair_sums(i, rest, g_rest)
        pend_rest = scatter_start(chip_rest, f"rest_l{i}")
        cot["draw"] = cot["draw"] + pend_rest[4][0, 0]
        dx, dx_m, g_in, grads[i]["ln1_w"] = _layer_bwd_in(cot, layers[i], saved[i], f"l{i}")
        if pend_in is not None:
            pend, g_up, sib_up = pend_in
            pieces[(i + 1, "w_in")] = (g_up, sib_up[0], scatter_wait(pend, dx, f"in_l{i + 1}")[0])
        for n, g, r, o in zip(rest, g_rest, sib_rest, scatter_wait(pend_rest, dx, f"rest_l{i}")):
            pieces[(i, n)] = (g, r, o)
        sib_in, chip_in = pair_sums(i, ["w_in"], [g_in])
        pend_in = (scatter_start(chip_in, f"in_l{i}"), g_in, sib_in)
    keys = [(i, n) for i in range(depth) for n in names if (i, n) != (0, "w_in")]
    g_first, r_first, o_first = pieces[keys[0]]
    pieces[keys[0]] = (g_first + pend_in[0][4][0, 0].astype(g_first.dtype), r_first, o_first)
    mine = [chip_sum(*pieces[k], name=f"chip_sum_{k[1]}_l{k[0]}") for k in keys]
    reduced = dict(zip(keys, join_halves(mine, name="join_halves")))
    gout = {}

    small = [jnp.stack([grads[i][n].reshape(-1) for i in range(depth)]).reshape(1, -1) for n in SMALL]
    small += [g_final.reshape(1, -1), loss_part.reshape(1, -1)]
    tot = gather_all(_pack_small(small), name="sum_small", total=True)
    parts = _unpack_small(tot, [t.shape[1] for t in small])
    for n, t in zip(SMALL, parts):
        if n == "conv_w":
            cols = w[n].shape[2]
            gout[n] = lax.dynamic_slice_in_dim(t.reshape(depth, CONV_K, -1), shard * cols, cols, axis=2)
        else:
            gout[n] = t.reshape(w[n].shape)
    gout["final_w"] = parts[-2].reshape(w["final_w"].shape)
    loss = parts[-1].reshape(())

    upd = {}
    for n in [t for t in WEIGHTS if t != "w_in"] + ["w_in"]:
        if n == "w_in":
            pend, g_in, sib_in = pend_in
            landed = scatter_wait(pend, upd["w_down"][1], "in_l0")[0]
            mine_in = chip_sum(g_in, sib_in[0], landed, name="chip_sum_w_in_l0")
            reduced[(0, n)] = join_halves([mine_in], name="join_halves_w_in_l0")[0]
        if n in gout:
            upd[n] = (gout[n].reshape(w[n].shape),) + adamw(w[n], gout[n].reshape(w[n].shape), m[n], v[n], name=f"adamw_{n}")
        else:
            upd[n] = adamw_layers(w[n], [reduced[(i, n)] for i in range(depth)], m[n], v[n], name=f"adamw_{n}")
    return (loss, dx[None], *[upd[n][0] for n in WEIGHTS], *[upd[n][1] for n in WEIGHTS],
            *[upd[n][2] for n in WEIGHTS], *[upd[n][3] for n in WEIGHTS])


def _pack_small(parts):
    flat = jnp.concatenate(parts, axis=1)
    rows = -(-flat.shape[1] // PACK_W)
    rows = -(-rows // 8) * 8
    return jnp.pad(flat, ((0, 0), (0, rows * PACK_W - flat.shape[1]))).reshape(rows, PACK_W)


def _unpack_small(buf, sizes):
    flat = buf.reshape(-1)
    out, off = [], 0
    for n in sizes:
        out.append(flat[off:off + n])
        off += n
    return out


def kernel(x, ln1_w, w_in, attn_sink, conv_w, conv_b, dt_bias, a_log, d_skip, ssm_norm_w, pool_w, pool_scale, w_attn_br, w_pool_br, w_ssm_br, w_out, ln2_w, w_gate_up, w_down, final_w, loss_target, m_ln1_w, m_w_in, m_attn_sink, m_conv_w, m_conv_b, m_dt_bias, m_a_log, m_d_skip, m_ssm_norm_w, m_pool_w, m_pool_scale, m_w_attn_br, m_w_pool_br, m_w_ssm_br, m_w_out, m_ln2_w, m_w_gate_up, m_w_down, m_final_w, v_ln1_w, v_w_in, v_attn_sink, v_conv_w, v_conv_b, v_dt_bias, v_a_log, v_d_skip, v_ssm_norm_w, v_pool_w, v_pool_scale, v_w_attn_br, v_w_pool_br, v_w_ssm_br, v_w_out, v_ln2_w, v_w_gate_up, v_w_down, v_final_w):
    w = dict(ln1_w=ln1_w, w_in=w_in, attn_sink=attn_sink, conv_w=conv_w, conv_b=conv_b, dt_bias=dt_bias, a_log=a_log,
             d_skip=d_skip, ssm_norm_w=ssm_norm_w, pool_w=pool_w, pool_scale=pool_scale, w_attn_br=w_attn_br,
             w_pool_br=w_pool_br, w_ssm_br=w_ssm_br, w_out=w_out, ln2_w=ln2_w, w_gate_up=w_gate_up, w_down=w_down,
             final_w=final_w)
    m = dict(ln1_w=m_ln1_w, w_in=m_w_in, attn_sink=m_attn_sink, conv_w=m_conv_w, conv_b=m_conv_b, dt_bias=m_dt_bias,
             a_log=m_a_log, d_skip=m_d_skip, ssm_norm_w=m_ssm_norm_w, pool_w=m_pool_w, pool_scale=m_pool_scale,
             w_attn_br=m_w_attn_br, w_pool_br=m_w_pool_br, w_ssm_br=m_w_ssm_br, w_out=m_w_out, ln2_w=m_ln2_w,
             w_gate_up=m_w_gate_up, w_down=m_w_down, final_w=m_final_w)
    v = dict(ln1_w=v_ln1_w, w_in=v_w_in, attn_sink=v_attn_sink, conv_w=v_conv_w, conv_b=v_conv_b, dt_bias=v_dt_bias,
             a_log=v_a_log, d_skip=v_d_skip, ssm_norm_w=v_ssm_norm_w, pool_w=v_pool_w, pool_scale=v_pool_scale,
             w_attn_br=v_w_attn_br, w_pool_br=v_w_pool_br, w_ssm_br=v_w_ssm_br, w_out=v_w_out, ln2_w=v_ln2_w,
             w_gate_up=v_w_gate_up, w_down=v_w_down, final_w=v_final_w)
    return _step(x, loss_target, w, m, v)
```

```python
import functools

import jax
import jax.numpy as jnp
from jax import lax
from jax.experimental import pallas as pl
from jax.experimental.pallas import tpu as pltpu

F32 = jnp.float32
MXU = jnp.bfloat16
VMEM_LIMIT = 56 * 1024 * 1024
EPS = 1e-6
NEG = -1e30

ADAM_LR, ADAM_B1, ADAM_B2, ADAM_EPS, ADAM_WD, ADAM_STEP = 0.001, 0.9, 0.999, 1e-08, 0.01, 10


class Cfg:
    def __init__(self, d_model=2048, seq=8192, depth=2, q_heads=16, kv_heads=4, head_dim=64,
                 ssm_head_dim=64, ssm_groups=4, d_state=128):
        self.D, self.T, self.depth = d_model, seq, depth
        self.hd, self.qh, self.kvh = head_dim, q_heads, kv_heads
        self.AW, self.KW = q_heads * head_dim, kv_heads * head_dim
        self.blk = 128
        self.PW = d_model // 2
        self.PG = self.PW // 4
        self.DI = d_model
        self.P = ssm_head_dim
        self.H = self.DI // self.P
        self.G = ssm_groups
        self.HG = self.H // self.G
        self.N = d_state
        self.CC = self.DI + 2 * self.G * self.N
        self.F = -(-8 * d_model // (3 * 256)) * 256
        self.in_widths = (self.AW, self.KW, self.KW, self.PW, self.DI, self.CC, self.H, 3 * d_model)
        self.in_cols = sum(self.in_widths)


C = Cfg()
POOL_WINDOWS = (2, 4, 8, 16)


def _pick(n, cands):
    for c in cands:
        if n % c == 0:
            return c
    return n


def _params(*sem):
    return pltpu.CompilerParams(dimension_semantics=sem, vmem_limit_bytes=VMEM_LIMIT)


def _sigmoid(x):
    return 1.0 / (1.0 + jnp.exp(-x))


def _dot(a, b, dims):
    return lax.dot_general(a.astype(MXU), b.astype(MXU), (dims, ((), ())), preferred_element_type=F32)


def _dot_nn(a, b):
    return _dot(a, b, ((1,), (0,)))


def _dot_nt(a, b):
    return _dot(a, b, ((1,), (1,)))


def _dot_tn(a, b):
    return _dot(a, b, ((0,), (0,)))


MM_VMEM = 40 * 1024 * 1024
LANES = 128


def _tile(n, cap):
    best = None
    for d in range(LANES, min(n, cap) + 1, LANES):
        if n % d == 0:
            best = d
    return n if best is None else best


def _divisors(n, cap):
    ds = [d for d in range(LANES, min(n, cap) + 1, LANES) if n % d == 0]
    return ds or [n]


HBM_RATE, MXU_RATE, ACC_RATE, STEP_COST = 3.0e12, 0.9e15, 1.2e13, 0.35e-6
MXU_WIDTH = 256


def _choose_tiles(M, N, K, n_unit, k_unit, sa, sb, so, sadd, ta):
    best = None
    for tm in sorted({_tile(M, 1024), _tile(M, 512)}):
        for tk in _divisors(k_unit, 4096):
            for tn in _divisors(n_unit, 2048):
                nk = K // tk
                vmem = 2 * (tm * tk * sa + tk * tn * sb + tm * tn * (so + sadd)) + (tm * tn * 4 if nk > 1 else 0)
                if vmem > MM_VMEM:
                    continue
                steps = (M // tm) * (N // tn) * nk
                a_bytes = M * K * sa * (N // tn if (nk > 1 or ta) else 1)
                b_bytes = K * N * sb * (M // tm)
                hbm = (a_bytes + b_bytes + M * N * (so + sadd)) / HBM_RATE
                fill = (tn / (-(-tn // MXU_WIDTH) * MXU_WIDTH)) * (tk / (-(-tk // MXU_WIDTH) * MXU_WIDTH))
                mxu = 2.0 * M * N * K / (MXU_RATE * fill)
                acc = steps * tm * tn * 8 / ACC_RATE if nk > 1 else 0.0
                cost = max(hbm, mxu) + acc + steps * STEP_COST
                if best is None or cost < best[0]:
                    best = (cost, tm, tn, tk)
    assert best is not None, (M, N, K)
    return best[1:]


def matmul(a, b, *, name, ta=False, tb=False, out_dtype=F32, add=None, layer=None, b_cols=False, out_cols=False):
    M, K = (a.shape[1], a.shape[0]) if ta else a.shape
    rows, cols = b.shape[-2], (4 if b_cols else 1) * b.shape[-1]
    N, bk = (rows, cols) if tb else (cols, rows)
    assert K == bk, (a.shape, b.shape)
    n_unit = N // 4 if (out_cols or (b_cols and not tb)) else N
    k_unit = K // 4 if (b_cols and tb) else K
    sa, sb, so = a.dtype.itemsize, b.dtype.itemsize, jnp.dtype(out_dtype).itemsize
    sadd = add.dtype.itemsize if add is not None else 0
    tm, tn, tk = _choose_tiles(M, N, K, n_unit, k_unit, sa, sb, so, sadd, ta)
    nk = K // tk
    a_spec = pl.BlockSpec((tk, tm), lambda i, j, k: (k, i)) if ta else pl.BlockSpec((tm, tk), lambda i, j, k: (i, k))
    if b_cols and tb:
        per = k_unit // tk
        b_spec = pl.BlockSpec((None, tn, tk), lambda i, j, k: (k // per, j, k % per))
    elif b_cols:
        per = n_unit // tn
        b_spec = pl.BlockSpec((None, tk, tn), lambda i, j, k: (j // per, k, j % per))
    elif layer is not None:
        b_spec = (pl.BlockSpec((None, tn, tk), lambda i, j, k: (layer, j, k)) if tb
                  else pl.BlockSpec((None, tk, tn), lambda i, j, k: (layer, k, j)))
    else:
        b_spec = pl.BlockSpec((tn, tk), lambda i, j, k: (j, k)) if tb else pl.BlockSpec((tk, tn), lambda i, j, k: (k, j))
    if out_cols:
        per = n_unit // tn
        o_spec = pl.BlockSpec((None, tm, tn), lambda i, j, k: (j // per, i, j % per))
        out_shape = jax.ShapeDtypeStruct((4, M, N // 4), out_dtype)
    else:
        o_spec = pl.BlockSpec((tm, tn), lambda i, j, k: (i, j))
        out_shape = jax.ShapeDtypeStruct((M, N), out_dtype)
    dims = ((0 if ta else 1,), (1 if tb else 0,))
    has_add = add is not None

    def body(*refs):
        a_ref, b_ref = refs[0], refs[1]
        add_ref = refs[2] if has_add else None
        o_ref = refs[3] if has_add else refs[2]
        part = _dot(a_ref[...], b_ref[...], dims)

        def finish(total):
            if has_add:
                total = total + add_ref[...].astype(F32)
            o_ref[...] = total.astype(o_ref.dtype)

        if nk == 1:
            finish(part)
        else:
            acc_ref = refs[-1]
            k = pl.program_id(2)

            @pl.when(k == 0)
            def _():
                acc_ref[...] = part

            @pl.when(k > 0)
            def _():
                acc_ref[...] += part

            @pl.when(k == nk - 1)
            def _():
                finish(acc_ref[...])

    in_specs = [a_spec, b_spec] + ([o_spec] if has_add else [])
    args = (a, b) + ((add,) if has_add else ())
    return pl.pallas_call(
        body, name=name, grid=(M // tm, N // tn, nk), in_specs=in_specs, out_specs=o_spec, out_shape=out_shape,
        scratch_shapes=[pltpu.VMEM((tm, tn), F32)] if nk > 1 else [],
        compiler_params=_params("parallel", "parallel", "arbitrary"),
    )(*args)


def _row_tile(t):
    return _pick(t, (256, 128, 64, 32, 16, 8))


def rms_fwd(x, w, *, name):
    T, D = x.shape
    tr = _row_tile(T)

    def body(x_ref, w_ref, o_ref):
        xv = x_ref[...]
        r = lax.rsqrt(jnp.mean(xv * xv, axis=-1, keepdims=True) + EPS)
        o_ref[...] = (xv * r * w_ref[...]).astype(o_ref.dtype)

    row = pl.BlockSpec((tr, D), lambda i: (i, 0))
    return pl.pallas_call(
        body, name=name, grid=(T // tr,), in_specs=[row, pl.BlockSpec((1, D), lambda i: (0, 0))], out_specs=row,
        out_shape=jax.ShapeDtypeStruct((T, D), MXU), compiler_params=_params("parallel"),
    )(x, w)


def rms_bwd(x, w, dh, dres, *, name):
    T, D = x.shape
    tr = _row_tile(T)

    def body(x_ref, w_ref, dh_ref, dres_ref, dx_ref, dxm_ref, dw_ref):
        xv = x_ref[...]
        dhv = dh_ref[...].astype(F32)
        r = lax.rsqrt(jnp.mean(xv * xv, axis=-1, keepdims=True) + EPS)
        g = dhv * w_ref[...]
        dot = jnp.mean(g * xv, axis=-1, keepdims=True)
        dx = dres_ref[...] + r * g - xv * (r * r * r * dot)
        dx_ref[...] = dx
        dxm_ref[...] = dx.astype(dxm_ref.dtype)
        part = jnp.sum(dhv * xv * r, axis=0, keepdims=True)

        @pl.when(pl.program_id(0) == 0)
        def _():
            dw_ref[...] = part

        @pl.when(pl.program_id(0) > 0)
        def _():
            dw_ref[...] += part

    row = pl.BlockSpec((tr, D), lambda i: (i, 0))
    vec = pl.BlockSpec((1, D), lambda i: (0, 0))
    return pl.pallas_call(
        body, name=name, grid=(T // tr,), in_specs=[row, vec, row, row], out_specs=[row, row, vec],
        out_shape=[jax.ShapeDtypeStruct((T, D), F32), jax.ShapeDtypeStruct((T, D), MXU),
                   jax.ShapeDtypeStruct((1, D), F32)],
        compiler_params=_params("arbitrary"),
    )(x, w, dh, dres)


def final_loss(x, w, target, *, name):
    T, D = x.shape
    tr = _row_tile(T)

    def body(x_ref, w_ref, t_ref, loss_ref, dx_ref, dxm_ref, dw_ref):
        xv = x_ref[...]
        wv = w_ref[...]
        r = lax.rsqrt(jnp.mean(xv * xv, axis=-1, keepdims=True) + EPS)
        err = xv * r * wv - t_ref[...]
        lpart = 0.5 * jnp.sum(jnp.mean(err * err, axis=-1, keepdims=True), axis=0, keepdims=True)
        dy = err * (1.0 / D)
        g = dy * wv
        dot = jnp.mean(g * xv, axis=-1, keepdims=True)
        dx = r * g - xv * (r * r * r * dot)
        dx_ref[...] = dx
        dxm_ref[...] = dx.astype(dxm_ref.dtype)
        part = jnp.sum(dy * xv * r, axis=0, keepdims=True)

        @pl.when(pl.program_id(0) == 0)
        def _():
            dw_ref[...] = part
            loss_ref[...] = lpart

        @pl.when(pl.program_id(0) > 0)
        def _():
            dw_ref[...] += part
            loss_ref[...] += lpart

    row = pl.BlockSpec((tr, D), lambda i: (i, 0))
    vec = pl.BlockSpec((1, D), lambda i: (0, 0))
    one = pl.BlockSpec((1, 1), lambda i: (0, 0))
    return pl.pallas_call(
        body, name=name, grid=(T // tr,), in_specs=[row, vec, row], out_specs=[one, row, row, vec],
        out_shape=[jax.ShapeDtypeStruct((1, 1), F32), jax.ShapeDtypeStruct((T, D), F32),
                   jax.ShapeDtypeStruct((T, D), MXU), jax.ShapeDtypeStruct((1, D), F32)],
        compiler_params=_params("arbitrary"),
    )(x, w, target)


def swiglu_fwd(gu, *, name):
    T, F2 = gu.shape
    F = F2 // 2
    tr = _row_tile(T)

    def body(g_ref, u_ref, o_ref):
        g = g_ref[...].astype(F32)
        o_ref[...] = (g * _sigmoid(g) * u_ref[...].astype(F32)).astype(o_ref.dtype)

    return pl.pallas_call(
        body, name=name, grid=(T // tr,),
        in_specs=[pl.BlockSpec((tr, F), lambda i: (i, 0)), pl.BlockSpec((tr, F), lambda i: (i, 1))],
        out_specs=pl.BlockSpec((tr, F), lambda i: (i, 0)),
        out_shape=jax.ShapeDtypeStruct((T, F), MXU), compiler_params=_params("parallel"),
    )(gu, gu)


def swiglu_bwd(gu, dact, *, name):
    T, F2 = gu.shape
    F = F2 // 2
    tr = _row_tile(T)

    def body(g_ref, u_ref, d_ref, o_ref):
        g = g_ref[...].astype(F32)
        d = d_ref[...].astype(F32)
        s = _sigmoid(g)
        o_ref[:, :F] = (d * u_ref[...].astype(F32) * s * (1.0 + g * (1.0 - s))).astype(o_ref.dtype)
        o_ref[:, F:] = (d * g * s).astype(o_ref.dtype)

    lo = pl.BlockSpec((tr, F), lambda i: (i, 0))
    hi = pl.BlockSpec((tr, F), lambda i: (i, 1))
    return pl.pallas_call(
        body, name=name, grid=(T // tr,), in_specs=[lo, hi, lo], out_specs=pl.BlockSpec((tr, F2), lambda i: (i, 0)),
        out_shape=jax.ShapeDtypeStruct((T, F2), MXU), compiler_params=_params("parallel"),
    )(gu, gu, dact)


def merge_fwd(gl, abr, pbr, sbr, *, name):
    T, D = abr.shape
    tr = _row_tile(T)

    def body(g0, g1, g2, a_ref, p_ref, s_ref, o_ref):
        m = sum(_sigmoid(g[...].astype(F32)) * b[...].astype(F32) for g, b in ((g0, a_ref), (g1, p_ref), (g2, s_ref)))
        o_ref[...] = m.astype(o_ref.dtype)

    row = pl.BlockSpec((tr, D), lambda i: (i, 0))
    gs = [pl.BlockSpec((tr, D), lambda i, j=j: (i, j)) for j in range(3)]
    return pl.pallas_call(
        body, name=name, grid=(T // tr,), in_specs=gs + [row, row, row], out_specs=row,
        out_shape=jax.ShapeDtypeStruct((T, D), MXU), compiler_params=_params("parallel"),
    )(gl, gl, gl, abr, pbr, sbr)


def merge_bwd(gl, abr, pbr, sbr, dm, *, name):
    T, D = abr.shape
    tr = _row_tile(T)

    def body(g0, g1, g2, a_ref, p_ref, s_ref, dm_ref, da_ref, dp_ref, ds_ref, dg_ref):
        d = dm_ref[...].astype(F32)
        for j, (g_ref, b_ref, db_ref) in enumerate(((g0, a_ref, da_ref), (g1, p_ref, dp_ref), (g2, s_ref, ds_ref))):
            s = _sigmoid(g_ref[...].astype(F32))
            db_ref[...] = (d * s).astype(db_ref.dtype)
            dg_ref[:, j * D:(j + 1) * D] = (d * b_ref[...].astype(F32) * s * (1.0 - s)).astype(dg_ref.dtype)

    row = pl.BlockSpec((tr, D), lambda i: (i, 0))
    gs = [pl.BlockSpec((tr, D), lambda i, j=j: (i, j)) for j in range(3)]
    return pl.pallas_call(
        body, name=name, grid=(T // tr,), in_specs=gs + [row] * 4,
        out_specs=[row] * 3 + [pl.BlockSpec((tr, 3 * D), lambda i: (i, 0))],
        out_shape=[jax.ShapeDtypeStruct((T, D), MXU)] * 3 + [jax.ShapeDtypeStruct((T, 3 * D), MXU)],
        compiler_params=_params("parallel"),
    )(gl, gl, gl, abr, pbr, sbr, dm)


def gnorm_fwd(y, z, w, *, name):
    T, DI = y.shape
    gw = DI // C.G
    tr = _row_tile(T)

    def body(y_ref, z_ref, w_ref, o_ref):
        for g in range(C.G):
            sl = slice(g * gw, (g + 1) * gw)
            zz = z_ref[:, sl]
            v = y_ref[:, sl] * (zz * _sigmoid(zz))
            r = lax.rsqrt(jnp.mean(v * v, axis=-1, keepdims=True) + EPS)
            o_ref[:, sl] = (v * r * w_ref[:, sl]).astype(o_ref.dtype)

    row = pl.BlockSpec((tr, DI), lambda i: (i, 0))
    return pl.pallas_call(
        body, name=name, grid=(T // tr,), in_specs=[row, row, pl.BlockSpec((1, DI), lambda i: (0, 0))],
        out_specs=row, out_shape=jax.ShapeDtypeStruct((T, DI), MXU), compiler_params=_params("parallel"),
    )(y, z, w)


def gnorm_bwd(y, z, w, do, *, name):
    T, DI = y.shape
    gw = DI // C.G
    tr = _row_tile(T)

    def body(y_ref, z_ref, w_ref, do_ref, dy_ref, dz_ref, dw_ref):
        first = pl.program_id(0) == 0
        for g in range(C.G):
            sl = slice(g * gw, (g + 1) * gw)
            zz = z_ref[:, sl]
            yy = y_ref[:, sl]
            s = _sigmoid(zz)
            sz = zz * s
            v = yy * sz
            r = lax.rsqrt(jnp.mean(v * v, axis=-1, keepdims=True) + EPS)
            dov = do_ref[:, sl].astype(F32)
            gg = dov * w_ref[:, sl]
            dot = jnp.mean(gg * v, axis=-1, keepdims=True)
            dv = r * gg - v * (r * r * r * dot)
            dy_ref[:, sl] = dv * sz
            dz_ref[:, sl] = (dv * yy * s * (1.0 + zz * (1.0 - s))).astype(dz_ref.dtype)
            part = jnp.sum(dov * v * r, axis=0, keepdims=True)

            @pl.when(first)
            def _():
                dw_ref[:, sl] = part

            @pl.when(jnp.logical_not(first))
            def _():
                dw_ref[:, sl] += part

    row = pl.BlockSpec((tr, DI), lambda i: (i, 0))
    vec = pl.BlockSpec((1, DI), lambda i: (0, 0))
    return pl.pallas_call(
        body, name=name, grid=(T // tr,), in_specs=[row, row, vec, row], out_specs=[row, row, vec],
        out_shape=[jax.ShapeDtypeStruct((T, DI), F32), jax.ShapeDtypeStruct((T, DI), MXU),
                   jax.ShapeDtypeStruct((1, DI), F32)],
        compiler_params=_params("arbitrary"),
    )(y, z, w, do)


def adamw(w, g, m, v, *, name):
    shape = w.shape
    cols = shape[-1]
    rows = w.size // cols
    w2, g2, m2, v2 = (t.reshape(rows, cols) for t in (w, g, m, v))
    tr = rows if rows * cols * 4 <= (2 << 20) else _pick(rows, (512, 256, 128, 64, 32, 16, 8))
    while tr * cols * 4 > (2 << 20) and tr % 16 == 0:
        tr //= 2
    c1 = 1.0 - ADAM_B1 ** ADAM_STEP
    c2 = 1.0 - ADAM_B2 ** ADAM_STEP

    def body(w_ref, g_ref, m_ref, v_ref, d_ref, nm_ref, nv_ref):
        gv = g_ref[...]
        nm = ADAM_B1 * m_ref[...] + (1.0 - ADAM_B1) * gv
        nv = ADAM_B2 * v_ref[...] + (1.0 - ADAM_B2) * (gv * gv)
        d_ref[...] = -ADAM_LR * ((nm / c1) / (jnp.sqrt(nv / c2) + ADAM_EPS) + ADAM_WD * w_ref[...])
        nm_ref[...] = nm
        nv_ref[...] = nv

    row = pl.BlockSpec((tr, cols), lambda i: (i, 0))
    outs = pl.pallas_call(
        body, name=name, grid=(rows // tr,), in_specs=[row] * 4, out_specs=[row] * 3,
        out_shape=[jax.ShapeDtypeStruct((rows, cols), F32)] * 3, compiler_params=_params("parallel"),
    )(w2, g2, m2, v2)
    return tuple(o.reshape(shape) for o in outs)


def adamw_layers(w, g_layers, m, v, *, name):
    shape = w.shape
    L = shape[0]
    rows, cols = g_layers[0].shape
    w3, m3, v3 = (t.reshape(L, rows, cols) for t in (w, m, v))
    tr = _pick(rows, (512, 256, 128, 64, 32, 16, 8))
    while tr * cols * 4 > (2 << 20) and tr % 16 == 0:
        tr //= 2
    c1 = 1.0 - ADAM_B1 ** ADAM_STEP
    c2 = 1.0 - ADAM_B2 ** ADAM_STEP

    def body(*refs):
        w_ref, m_ref, v_ref = refs[0], refs[1], refs[2]
        g_refs = refs[3:3 + L]
        go_ref, d_ref, nm_ref, nv_ref = refs[3 + L:]
        layer = pl.program_id(0)
        gv = g_refs[0][...]
        for k in range(1, L):
            gv = jnp.where(layer == k, g_refs[k][...], gv)
        nm = ADAM_B1 * m_ref[...] + (1.0 - ADAM_B1) * gv
        nv = ADAM_B2 * v_ref[...] + (1.0 - ADAM_B2) * (gv * gv)
        d_ref[...] = -ADAM_LR * ((nm / c1) / (jnp.sqrt(nv / c2) + ADAM_EPS) + ADAM_WD * w_ref[...])
        go_ref[...] = gv
        nm_ref[...] = nm
        nv_ref[...] = nv

    blk = pl.BlockSpec((None, tr, cols), lambda l, i: (l, i, 0))
    gblks = [pl.BlockSpec((tr, cols), lambda l, i, k=k: (jnp.where(l == k, i, 0), 0)) for k in range(L)]
    outs = pl.pallas_call(
        body, name=name, grid=(L, rows // tr), in_specs=[blk] * 3 + gblks, out_specs=[blk] * 4,
        out_shape=[jax.ShapeDtypeStruct((L, rows, cols), F32)] * 4, compiler_params=_params("parallel", "parallel"),
    )(w3, m3, v3, *g_layers)
    return tuple(o.reshape(shape) for o in outs)


def _attn_masks(i, heads):
    B = C.blk
    row = lax.broadcasted_iota(jnp.int32, (heads * B, 2 * B), 0) & (B - 1)
    col = lax.broadcasted_iota(jnp.int32, (heads * B, 2 * B), 1)
    diff = row + B - col
    return (diff >= 0) & (diff < B) & ((col >= B) | (i > 0))


def _stack_heads(ref, kvh, ppk, lo):
    parts = []
    for pr in range(ppk):
        pair = kvh * ppk + pr
        qp = ref[:, pair * 128:(pair + 1) * 128].astype(F32)
        parts += [jnp.where(lo, qp, 0.0), jnp.where(lo, 0.0, qp)]
    return jnp.concatenate(parts, axis=0)


def _unstack_heads(ref, val, kvh, ppk, lo):
    B = C.blk
    for pr in range(ppk):
        pair = kvh * ppk + pr
        ref[:, pair * 128:(pair + 1) * 128] = jnp.where(lo, val[2 * pr * B:(2 * pr + 1) * B],
                                                        val[(2 * pr + 1) * B:(2 * pr + 2) * B]).astype(ref.dtype)


def _sink_scores(sink_ref, kvh, ppk):
    B = C.blk
    h0 = kvh * 2 * ppk
    return jnp.concatenate([jnp.full((B, 2 * B), sink_ref[0, h], F32) for h in range(h0, h0 + 2 * ppk)], axis=0)


def _kv2(prev_ref, cur_ref, m, lo):
    sl = slice(m * 128, (m + 1) * 128)
    slab = jnp.concatenate([prev_ref[:, sl], cur_ref[:, sl]], axis=0).astype(F32)
    slab = jnp.where(lax.broadcasted_iota(jnp.int32, (slab.shape[0], 1), 0) == 0, 0.0, slab)
    rolled = pltpu.roll(slab, 64, axis=1)
    return jnp.where(lo, slab, rolled), jnp.where(lo, rolled, slab)


def _attn_weights(q4, k2, mask, sink_scores):
    s = _dot_nt(q4, k2) * (C.hd ** -0.5)
    col0 = lax.broadcasted_iota(jnp.int32, (1, s.shape[1]), 1) == 0
    s = jnp.where(col0, sink_scores, jnp.where(mask, s, NEG))
    return jnp.exp(s - jnp.max(s, axis=-1, keepdims=True))


def _split_sums(v, sel):
    hi = v.astype(jnp.bfloat16)
    lo = (v - hi.astype(F32)).astype(jnp.bfloat16)
    sel2 = jnp.concatenate([sel, sel], axis=0).astype(jnp.bfloat16)
    return lax.dot_general(jnp.concatenate([hi, lo], axis=1), sel2, (((1,), (0,)), ((), ())), preferred_element_type=F32)


def _row_sums(p, exact=False):
    ones = jnp.ones((p.shape[1], 128), F32)
    return _split_sums(p, ones) if exact else _dot_nn(p, ones)


def attn_fwd(qkv, sink, *, name):
    T = qkv.shape[0]
    B, AW, KW = C.blk, C.AW, C.KW
    assert C.hd == 64 and KW % 128 == 0 and (C.qh // C.kvh) % 2 == 0
    nb = T // B
    kb = AW // KW
    ppk = C.qh // C.kvh // 2

    def body(q_ref, kc_ref, kp_ref, vc_ref, vp_ref, sink_ref, o_ref):
        i = pl.program_id(0)
        lo = lax.broadcasted_iota(jnp.int32, (1, 128), 1) < 64
        mask = _attn_masks(i, 2 * ppk)
        for m in range(KW // 128):
            k2s = _kv2(kp_ref, kc_ref, m, lo)
            v2s = _kv2(vp_ref, vc_ref, m, lo)
            for par in range(2):
                kvh = 2 * m + par
                q4 = _stack_heads(q_ref, kvh, ppk, lo)
                p = _attn_weights(q4, k2s[par], mask, _sink_scores(sink_ref, kvh, ppk))
                _unstack_heads(o_ref, _dot_nn(p, v2s[par]) / _row_sums(p), kvh, ppk, lo)

    prev = lambda i: jnp.maximum(i - 1, 0)
    return pl.pallas_call(
        body, name=name, grid=(nb,),
        in_specs=[pl.BlockSpec((B, AW), lambda i: (i, 0)),
                  pl.BlockSpec((B, KW), lambda i: (i, kb)), pl.BlockSpec((B, KW), lambda i: (prev(i), kb)),
                  pl.BlockSpec((B, KW), lambda i: (i, kb + 1)), pl.BlockSpec((B, KW), lambda i: (prev(i), kb + 1)),
                  pl.BlockSpec(memory_space=pltpu.SMEM)],
        out_specs=pl.BlockSpec((B, AW), lambda i: (i, 0)),
        out_shape=jax.ShapeDtypeStruct((T, AW), MXU), compiler_params=_params("parallel"),
    )(qkv, qkv, qkv, qkv, qkv, sink)


def attn_bwd(qkv, sink, dout, *, name):
    T = qkv.shape[0]
    B, AW, KW = C.blk, C.AW, C.KW
    nb = T // B
    kb = AW // KW
    ppk = C.qh // C.kvh // 2
    scale = C.hd ** -0.5

    def body(q_ref, kc_ref, kp_ref, vc_ref, vp_ref, sink_ref, do_ref, dq_ref, dk_ref, dv_ref, ds_ref, ck_ref, cv_ref):
        i = pl.program_id(0)

        @pl.when(i == 0)
        def _():
            ck_ref[...] = jnp.zeros_like(ck_ref)
            cv_ref[...] = jnp.zeros_like(cv_ref)
            ds_ref[...] = jnp.zeros_like(ds_ref)

        @pl.when(i < nb)
        def _():
            lane = lax.broadcasted_iota(jnp.int32, (1, 128), 1)
            lo = lane < 64
            hlane = lax.broadcasted_iota(jnp.int32, (1, C.qh), 1)
            mask = _attn_masks(i, 2 * ppk)
            dsink = jnp.zeros((1, C.qh), F32)
            for m in range(KW // 128):
                ksl = slice(m * 128, (m + 1) * 128)
                k2s = _kv2(kp_ref, kc_ref, m, lo)
                v2s = _kv2(vp_ref, vc_ref, m, lo)
                folded = []
                for par in range(2):
                    kvh = 2 * m + par
                    q4 = _stack_heads(q_ref, kvh, ppk, lo)
                    do4 = _stack_heads(do_ref, kvh, ppk, lo)
                    p = _attn_weights(q4, k2s[par], mask, _sink_scores(sink_ref, kvh, ppk))
                    inv = 1.0 / _row_sums(p)
                    p = p * jnp.concatenate([inv, inv], axis=1)
                    dp = _dot_nt(do4, v2s[par])
                    delta = _row_sums(p * dp, exact=True)
                    dsc = p * (dp - jnp.concatenate([delta, delta], axis=1))
                    _unstack_heads(dq_ref, _dot_nn(dsc, k2s[par]) * scale, kvh, ppk, lo)
                    dk2 = _dot_tn(dsc, q4) * scale
                    dv2 = _dot_tn(p, do4)
                    for k in range(2 * ppk):
                        dsh = jnp.sum(dsc[k * B:(k + 1) * B, :128], axis=0, keepdims=True)[:, :1]
                        dsink = dsink + jnp.where(hlane == kvh * 2 * ppk + k, dsh, 0.0)
                    folded.append((dk2 + pltpu.roll(dk2, 64, axis=1), dv2 + pltpu.roll(dv2, 64, axis=1)))
                row0 = lax.broadcasted_iota(jnp.int32, (2 * B, 1), 0) == 0
                dks = jnp.where(row0, 0.0, jnp.where(lo, folded[0][0], folded[1][0]))
                dvs = jnp.where(row0, 0.0, jnp.where(lo, folded[0][1], folded[1][1]))
                dk_ref[:, ksl] = (ck_ref[:, ksl] + dks[:B]).astype(dk_ref.dtype)
                dv_ref[:, ksl] = (cv_ref[:, ksl] + dvs[:B]).astype(dv_ref.dtype)
                ck_ref[:, ksl] = dks[B:]
                cv_ref[:, ksl] = dvs[B:]
            ds_ref[...] += dsink

        @pl.when(i == nb)
        def _():
            dk_ref[...] = ck_ref[...].astype(dk_ref.dtype)
            dv_ref[...] = cv_ref[...].astype(dv_ref.dtype)

    cur = lambda i: jnp.minimum(i, nb - 1)
    prev = lambda i: jnp.maximum(jnp.minimum(i, nb - 1) - 1, 0)
    out = lambda i: jnp.maximum(i - 1, 0)
    return pl.pallas_call(
        body, name=name, grid=(nb + 1,),
        in_specs=[pl.BlockSpec((B, AW), lambda i: (cur(i), 0)),
                  pl.BlockSpec((B, KW), lambda i: (cur(i), kb)), pl.BlockSpec((B, KW), lambda i: (prev(i), kb)),
                  pl.BlockSpec((B, KW), lambda i: (cur(i), kb + 1)), pl.BlockSpec((B, KW), lambda i: (prev(i), kb + 1)),
                  pl.BlockSpec(memory_space=pltpu.SMEM),
                  pl.BlockSpec((B, AW), lambda i: (cur(i), 0))],
        out_specs=[pl.BlockSpec((B, AW), lambda i: (cur(i), 0)),
                   pl.BlockSpec((B, KW), lambda i: (out(i), 0)), pl.BlockSpec((B, KW), lambda i: (out(i), 0)),
                   pl.BlockSpec((1, C.qh), lambda i: (0, 0))],
        out_shape=[jax.ShapeDtypeStruct((T, AW), MXU), jax.ShapeDtypeStruct((T, KW), MXU),
                   jax.ShapeDtypeStruct((T, KW), MXU), jax.ShapeDtypeStruct((1, C.qh), F32)],
        scratch_shapes=[pltpu.VMEM((B, KW), F32), pltpu.VMEM((B, KW), F32)],
        compiler_params=_params("arbitrary"),
    )(qkv, qkv, qkv, qkv, qkv, sink, dout)


POOL_HALO = 16


def _window_sum(e, w, n, forward):
    s, k = e, 1
    while k < w:
        s = s + pltpu.roll(s, (n - k) if forward else k, axis=0)
        k *= 2
    return s


def pool_fwd(u, pw, ps, *, name):
    T, PW = u.shape
    PG = C.PG
    tT = _pick(T, (256, 128))
    hb = tT // POOL_HALO

    def body(u_ref, uh_ref, pw_ref, ps_ref, o_ref, mx_ref):
        i = pl.program_id(0)
        halo = jnp.where(i > 0, uh_ref[...], 0.0)
        ext = jnp.concatenate([halo, u_ref[...]], axis=0)
        t = i * tT + lax.broadcasted_iota(jnp.int32, (tT, 1), 0)
        for g, w in enumerate(POOL_WINDOWS):
            sl = slice(g * PG, (g + 1) * PG)
            s = _window_sum(ext[:, sl], w, tT + POOL_HALO, False)[POOL_HALO:]
            cnt = jnp.minimum(t + 1, w).astype(F32)
            mixed = (s / cnt - u_ref[:, sl]).astype(MXU)
            mx_ref[:, sl] = mixed.astype(mx_ref.dtype)
            o_ref[:, sl] = (_dot_nn(mixed, pw_ref[g]) * ps_ref[:, sl]).astype(o_ref.dtype)

    row = pl.BlockSpec((tT, PW), lambda i: (i, 0))
    return pl.pallas_call(
        body, name=name, grid=(T // tT,),
        in_specs=[row, pl.BlockSpec((POOL_HALO, PW), lambda i: (jnp.maximum(i * hb - 1, 0), 0)),
                  pl.BlockSpec((4, PG, PG), lambda i: (0, 0, 0)), pl.BlockSpec((1, PW), lambda i: (0, 0))],
        out_specs=[row, row], out_shape=[jax.ShapeDtypeStruct((T, PW), MXU)] * 2,
        compiler_params=_params("parallel"),
    )(u, u, pw, ps)


def pool_bwd(dpool, mixed, pw, ps, *, name):
    T, PW = dpool.shape
    PG = C.PG
    tT = _pick(T, (256, 128))
    hb = tT // POOL_HALO
    n = T // tT
    rows = tT + POOL_HALO

    def body(dp_ref, dph_ref, mx_ref, pw_ref, ps_ref, du_ref, dpw_ref, dps_ref):
        i = pl.program_id(0)

        @pl.when(i == 0)
        def _():
            dpw_ref[...] = jnp.zeros_like(dpw_ref)
            dps_ref[...] = jnp.zeros_like(dps_ref)

        halo = jnp.where(i < n - 1, dph_ref[...].astype(F32), 0.0)
        dext = jnp.concatenate([dp_ref[...].astype(F32), halo], axis=0)
        t = i * tT + lax.broadcasted_iota(jnp.int32, (rows, 1), 0)
        for g, w in enumerate(POOL_WINDOWS):
            sl = slice(g * PG, (g + 1) * PG)
            dyg = dext[:, sl] * ps_ref[:, sl]
            dmix = _dot_nt(dyg, pw_ref[g])
            cnt = jnp.minimum(t + 1, w).astype(F32)
            s = _window_sum(dmix / cnt, w, rows, True)
            du_ref[:, sl] = (s[:tT] - dmix[:tT]).astype(du_ref.dtype)
            mb = mx_ref[:, sl]
            dpw_ref[g] += _dot_tn(mb, dyg[:tT])
            dps_ref[:, sl] += jnp.sum(dext[:tT, sl] * _dot_nn(mb, pw_ref[g]), axis=0, keepdims=True)

    row = pl.BlockSpec((tT, PW), lambda i: (i, 0))
    return pl.pallas_call(
        body, name=name, grid=(n,),
        in_specs=[row, pl.BlockSpec((POOL_HALO, PW), lambda i: (jnp.minimum((i + 1) * hb, T // POOL_HALO - 1), 0)), row,
                  pl.BlockSpec((4, PG, PG), lambda i: (0, 0, 0)), pl.BlockSpec((1, PW), lambda i: (0, 0))],
        out_specs=[row, pl.BlockSpec((4, PG, PG), lambda i: (0, 0, 0)), pl.BlockSpec((1, PW), lambda i: (0, 0))],
        out_shape=[jax.ShapeDtypeStruct((T, PW), MXU), jax.ShapeDtypeStruct((4, PG, PG), F32),
                   jax.ShapeDtypeStruct((1, PW), F32)],
        compiler_params=_params("arbitrary"),
    )(dpool, dpool, mixed, pw, ps)


CONV_HALO = 8
CONV_K = 4
CONV_TAP_ROWS = 16


def _conv(x, cw_ref, cb_ref, sl):
    acc = cb_ref[:, sl] + cw_ref[CONV_K - 1:CONV_K, sl] * x
    for k in range(CONV_K - 1):
        acc = acc + cw_ref[k:k + 1, sl] * pltpu.roll(x, CONV_K - 1 - k, axis=0)
    return acc


def conv_fwd(xbc, cw, cb, *, name):
    T, CC = xbc.shape
    tT = _pick(T, (256, 128))
    hb = tT // CONV_HALO
    cs = _pick(CC, (512, 256, 128))

    def body(x_ref, xh_ref, cw_ref, cb_ref, o_ref):
        i = pl.program_id(0)
        for c0 in range(0, CC, cs):
            sl = slice(c0, c0 + cs)
            ext = jnp.concatenate([jnp.where(i > 0, xh_ref[:, sl], 0.0), x_ref[:, sl]], axis=0)
            xc = _conv(ext, cw_ref, cb_ref, sl)[CONV_HALO:]
            o_ref[:, sl] = xc * _sigmoid(xc)

    row = pl.BlockSpec((tT, CC), lambda i: (i, 0))
    return pl.pallas_call(
        body, name=name, grid=(T // tT,),
        in_specs=[row, pl.BlockSpec((CONV_HALO, CC), lambda i: (jnp.maximum(i * hb - 1, 0), 0)),
                  pl.BlockSpec((CONV_K, CC), lambda i: (0, 0)), pl.BlockSpec((1, CC), lambda i: (0, 0))],
        out_specs=row, out_shape=jax.ShapeDtypeStruct((T, CC), F32), compiler_params=_params("parallel"),
    )(xbc, xbc, cw, cb)


def conv_bwd(xbc, cw, cb, dxa, *, name):
    T, CC = xbc.shape
    tT = _pick(T, (256, 128))
    hb = tT // CONV_HALO
    n = T // tT
    rows = tT + 2 * CONV_HALO
    cs = _pick(CC, (512, 256, 128))

    def body(x_ref, xp_ref, xn_ref, cw_ref, cb_ref, d_ref, dn_ref, dx_ref, dcw_ref, dcb_ref):
        i = pl.program_id(0)

        @pl.when(i == 0)
        def _():
            dcw_ref[...] = jnp.zeros_like(dcw_ref)
            dcb_ref[...] = jnp.zeros_like(dcb_ref)

        r = lax.broadcasted_iota(jnp.int32, (rows, 1), 0)
        own = (r >= CONV_HALO) & (r < tT + CONV_HALO)
        for c0 in range(0, CC, cs):
            sl = slice(c0, c0 + cs)
            x = jnp.concatenate([jnp.where(i > 0, xp_ref[:, sl], 0.0), x_ref[:, sl],
                                 jnp.where(i < n - 1, xn_ref[:, sl], 0.0)], axis=0)
            da = jnp.concatenate([jnp.zeros((CONV_HALO, cs), F32), d_ref[:, sl],
                                  jnp.where(i < n - 1, dn_ref[:, sl], 0.0)], axis=0)
            xc = _conv(x, cw_ref, cb_ref, sl)
            sg = _sigmoid(xc)
            dxc = da * sg * (1.0 + xc * (1.0 - sg))
            acc = cw_ref[CONV_K - 1:CONV_K, sl] * dxc
            for k in range(CONV_K - 1):
                acc = acc + cw_ref[k:k + 1, sl] * pltpu.roll(dxc, rows - (CONV_K - 1 - k), axis=0)
            dx_ref[:, sl] = acc[CONV_HALO:tT + CONV_HALO].astype(dx_ref.dtype)
            down = jnp.where(own, dxc, 0.0)
            dcb_ref[:, sl] += jnp.sum(down, axis=0, keepdims=True)
            dcw_ref[CONV_K - 1:CONV_K, sl] += jnp.sum(down * x, axis=0, keepdims=True)
            for k in range(CONV_K - 1):
                dcw_ref[k:k + 1, sl] += jnp.sum(down * pltpu.roll(x, CONV_K - 1 - k, axis=0), axis=0, keepdims=True)

    row = pl.BlockSpec((tT, CC), lambda i: (i, 0))
    prev = pl.BlockSpec((CONV_HALO, CC), lambda i: (jnp.maximum(i * hb - 1, 0), 0))
    nxt = pl.BlockSpec((CONV_HALO, CC), lambda i: (jnp.minimum((i + 1) * hb, T // CONV_HALO - 1), 0))
    return pl.pallas_call(
        body, name=name, grid=(n,),
        in_specs=[row, prev, nxt, pl.BlockSpec((CONV_K, CC), lambda i: (0, 0)), pl.BlockSpec((1, CC), lambda i: (0, 0)),
                  row, nxt],
        out_specs=[row, pl.BlockSpec((CONV_K, CC), lambda i: (0, 0)), pl.BlockSpec((1, CC), lambda i: (0, 0))],
        out_shape=[jax.ShapeDtypeStruct((T, CC), MXU), jax.ShapeDtypeStruct((CONV_K, CC), F32),
                   jax.ShapeDtypeStruct((1, CC), F32)],
        compiler_params=_params("arbitrary"),
    )(xbc, xbc, xbc, cw, cb, dxa, dxa)


def _softplus(x):
    return jnp.maximum(x, 0.0) + jnp.log(1.0 + jnp.exp(-jnp.abs(x)))


def _dot_exact(a, b):
    return lax.dot_general(a, b, (((1,), (0,)), ((), ())), precision=lax.Precision.HIGHEST, preferred_element_type=F32)


def ssd_prep(raw_t, bias, alog, *, name):
    H, T = raw_t.shape
    B = C.blk
    tc = _pick(T, (4 * B, 2 * B, B))

    def body(r_ref, b_ref, al_ref, dt_ref, acs_ref):
        dt = _softplus(r_ref[...] + b_ref[...])
        dt_ref[...] = dt
        dta = dt * (-jnp.exp(al_ref[...]))
        upper = (lax.broadcasted_iota(jnp.int32, (B, B), 0) <= lax.broadcasted_iota(jnp.int32, (B, B), 1)).astype(F32)
        for j in range(tc // B):
            acs_ref[:, j * B:(j + 1) * B] = _dot_exact(dta[:, j * B:(j + 1) * B], upper)

    blk = pl.BlockSpec((H, tc), lambda i: (0, i))
    vec = pl.BlockSpec((H, 1), lambda i: (0, 0))
    return pl.pallas_call(
        body, name=name, grid=(T // tc,), in_specs=[blk, vec, vec], out_specs=[blk, blk],
        out_shape=[jax.ShapeDtypeStruct((H, T), F32)] * 2, compiler_params=_params("parallel"),
    )(raw_t, bias, alog)


def _pair(lo, arr, h0, rows=slice(None)):
    return jnp.where(lo, arr[rows, h0:h0 + 1], arr[rows, h0 + 1:h0 + 2])


def _decay(acs, acs_t, h, causal):
    return jnp.exp(jnp.where(causal, acs[:, h:h + 1] - acs_t[h:h + 1, :], NEG))


def ssd_fwd(xa, dt, acs, acs_t, dskip, *, name):
    T = xa.shape[0]
    B, DI, G, N, HG, P, H = C.blk, C.DI, C.G, C.N, C.HG, C.P, C.H
    assert P == 64 and HG % 2 == 0
    nc = T // B
    W = HG * P

    def body(xa_ref, dt_ref, acs_ref, acst_ref, ds_ref, y_ref, hp_ref, h_scr):
        @pl.when(pl.program_id(0) == 0)
        def _():
            h_scr[...] = jnp.zeros_like(h_scr)

        lo = lax.broadcasted_iota(jnp.int32, (1, 128), 1) < 64
        causal = lax.broadcasted_iota(jnp.int32, (B, B), 0) >= lax.broadcasted_iota(jnp.int32, (B, B), 1)
        dt, acs, acs_t, dsk = dt_ref[...], acs_ref[...], acst_ref[...], ds_ref[...]
        for g in range(G):
            bg = xa_ref[:, DI + g * N:DI + (g + 1) * N]
            cg = xa_ref[:, DI + (G + g) * N:DI + (G + g + 1) * N]
            cb = _dot_nt(cg, bg)
            hg = h_scr[g]
            hp_ref[0, g * N:(g + 1) * N, :] = hg
            yoff = _dot_nn(cg, hg)
            xws, decs = [], []
            for j in range(HG // 2):
                h0 = g * HG + 2 * j
                xsl = slice(h0 * P, (h0 + 2) * P)
                xp = xa_ref[:, xsl]
                ap = _pair(lo, acs, h0)
                alast = _pair(lo, acs, h0, slice(B - 1, B))
                xdt = xp * _pair(lo, dt, h0)
                ys = [_dot_nn(cb * _decay(acs, acs_t, h0 + half, causal), xdt) for half in range(2)]
                y_ref[:, xsl] = (jnp.where(lo, ys[0], ys[1]) + yoff[:, 2 * j * P:(2 * j + 2) * P] * jnp.exp(ap)
                                 + _pair(lo, dsk, h0) * xp)
                xws.append(xdt * jnp.exp(alast - ap))
                decs.append(jnp.exp(alast))
            h_scr[g] = hg * jnp.concatenate(decs, axis=1) + _dot_tn(bg, jnp.concatenate(xws, axis=1))

    tok = lambda w: pl.BlockSpec((B, w), lambda c: (c, 0))
    return pl.pallas_call(
        body, name=name, grid=(nc,),
        in_specs=[tok(C.CC), tok(H), tok(H), pl.BlockSpec((H, B), lambda c: (0, c)), pl.BlockSpec((1, H), lambda c: (0, 0))],
        out_specs=[tok(DI), pl.BlockSpec((1, G * N, W), lambda c: (c, 0, 0))],
        out_shape=[jax.ShapeDtypeStruct((T, DI), F32), jax.ShapeDtypeStruct((nc, G * N, W), F32)],
        scratch_shapes=[pltpu.VMEM((G, N, W), F32)],
        compiler_params=_params("arbitrary"),
    )(xa, dt, acs, acs_t, dskip)


def ssd_bwd(xa, dt, acs, acs_t, dskip, hprev, dy, *, name):
    T = xa.shape[0]
    B, DI, G, N, HG, P, H = C.blk, C.DI, C.G, C.N, C.HG, C.P, C.H
    nc = T // B
    W = HG * P

    def body(xa_ref, dt_ref, acs_ref, acst_ref, ds_ref, hp_ref, dy_ref, dxa_ref, ddt_ref, dacs_ref, dd_ref, dh_scr):
        @pl.when(pl.program_id(0) == 0)
        def _():
            dh_scr[...] = jnp.zeros_like(dh_scr)
            dd_ref[...] = jnp.zeros_like(dd_ref)

        lo = lax.broadcasted_iota(jnp.int32, (1, 128), 1) < 64
        hi = jnp.logical_not(lo)
        causal = lax.broadcasted_iota(jnp.int32, (B, B), 0) >= lax.broadcasted_iota(jnp.int32, (B, B), 1)
        hlane = lax.broadcasted_iota(jnp.int32, (1, H), 1)
        hsub = lax.broadcasted_iota(jnp.int32, (B, 1), 0)
        lastrow = lax.broadcasted_iota(jnp.int32, (B, 1), 0) == B - 1
        dt, acs, acs_t, dsk = dt_ref[...], acs_ref[...], acst_ref[...], ds_ref[...]
        d_acs = jnp.zeros((B, H), F32)
        d_acs_t = jnp.zeros((B, B), F32)
        d_dt = jnp.zeros((B, H), F32)
        d_d = jnp.zeros((1, H), F32)

        lane_r = lax.broadcasted_iota(jnp.int32, (128, 128), 0)
        lane_c = lax.broadcasted_iota(jnp.int32, (128, 128), 1)
        same_half = ((lane_r < 64) == (lane_c < 64)).astype(F32)
        ones_sq = jnp.ones((128, 128), F32)

        def rsum(v):
            r = _split_sums(v, same_half)
            return r[:, 0:1], r[:, 64:65]

        for g in range(G):
            bsl = slice(DI + g * N, DI + (g + 1) * N)
            csl = slice(DI + (G + g) * N, DI + (G + g + 1) * N)
            bg, cg = xa_ref[:, bsl], xa_ref[:, csl]
            cb = _dot_nt(cg, bg)
            hg = hp_ref[0, g * N:(g + 1) * N, :]
            dhg = dh_scr[g]
            yoff = _dot_nn(cg, hg)
            bds = _dot_nn(bg, dhg)
            d_cb = jnp.zeros((B, B), F32)
            dyes, xws, decs = [], [], []
            for j in range(HG // 2):
                h0 = g * HG + 2 * j
                xsl = slice(h0 * P, (h0 + 2) * P)
                psl = slice(2 * j * P, (2 * j + 2) * P)
                xp, dyp = xa_ref[:, xsl], dy_ref[:, xsl]
                dtp = _pair(lo, dt, h0)
                ap = _pair(lo, acs, h0)
                alast = _pair(lo, acs, h0, slice(B - 1, B))
                ea, ew, el = jnp.exp(ap), jnp.exp(alast - ap), jnp.exp(alast)
                xdt = xp * dtp
                halves = []
                for half in range(2):
                    h = h0 + half
                    lm = _decay(acs, acs_t, h, causal)
                    m = cb * lm
                    d_m = _dot_nt(jnp.where(lo if half == 0 else hi, dyp, 0.0), xdt)
                    d_cb = d_cb + d_m * lm
                    wgt = d_m * m
                    d_acs = d_acs + jnp.where(hlane == h, _split_sums(wgt, ones_sq)[:, 0:1], 0.0)
                    w_hi = wgt.astype(jnp.bfloat16)
                    w_lo = (wgt - w_hi.astype(F32)).astype(jnp.bfloat16)
                    col = _dot_tn(jnp.ones((2 * B, 8), jnp.bfloat16), jnp.concatenate([w_hi, w_lo], axis=0))
                    d_acs_t = d_acs_t + jnp.where(hsub == h, col[0:1, :], 0.0)
                    halves.append(_dot_tn(m, dyp))
                bdp = bds[:, psl]
                dxdt = jnp.where(lo, halves[0], halves[1]) + ew * bdp
                dxa_ref[:, xsl] = dtp * dxdt + _pair(lo, dsk, h0) * dyp
                xw = xdt * ew
                terms_dt = rsum(dxdt * xp)
                terms_dd = rsum(dyp * xp)
                terms_off = rsum(dyp * (ea * yoff[:, psl]))
                terms_e = rsum(xw * bdp)
                terms_h = rsum(hg[:, psl] * dhg[:, psl])
                for half in range(2):
                    h = h0 + half
                    sel = hlane == h
                    d_dt = d_dt + jnp.where(sel, terms_dt[half], 0.0)
                    d_d = d_d + jnp.where(sel, jnp.sum(terms_dd[half], axis=0, keepdims=True), 0.0)
                    e_last = jnp.sum(jnp.where(lo if half == 0 else hi, el, 0.0), axis=1, keepdims=True) * (1.0 / P)
                    d_last = (jnp.sum(terms_e[half], axis=0, keepdims=True)
                              + e_last * jnp.sum(terms_h[half], axis=0, keepdims=True))
                    d_acs = d_acs + jnp.where(sel, terms_off[half] - terms_e[half] + jnp.where(lastrow, d_last, 0.0), 0.0)
                dyes.append(dyp * ea)
                xws.append(xw)
                decs.append(el)
            dye = jnp.concatenate(dyes, axis=1)
            xwc = jnp.concatenate(xws, axis=1)
            dxa_ref[:, csl] = _dot_nn(d_cb, bg) + _dot_nt(dye, hg)
            dxa_ref[:, bsl] = _dot_tn(d_cb, cg) + _dot_nt(xwc, dhg)
            dh_scr[g] = dhg * jnp.concatenate(decs, axis=1) + _dot_tn(cg, dye)
        ddt_ref[...] = d_dt
        dacs_ref[...] = d_acs - d_acs_t.T[:, :H]
        dd_ref[...] += d_d

    rev = lambda w: pl.BlockSpec((B, w), lambda c: (nc - 1 - c, 0))
    vec = pl.BlockSpec((1, H), lambda c: (0, 0))
    return pl.pallas_call(
        body, name=name, grid=(nc,),
        in_specs=[rev(C.CC), rev(H), rev(H), pl.BlockSpec((H, B), lambda c: (0, nc - 1 - c)), vec,
                  pl.BlockSpec((1, G * N, W), lambda c: (nc - 1 - c, 0, 0)), rev(DI)],
        out_specs=[rev(C.CC), rev(H), rev(H), vec],
        out_shape=[jax.ShapeDtypeStruct((T, C.CC), F32), jax.ShapeDtypeStruct((T, H), F32),
                   jax.ShapeDtypeStruct((T, H), F32), jax.ShapeDtypeStruct((1, H), F32)],
        scratch_shapes=[pltpu.VMEM((G, N, W), F32)],
        compiler_params=_params("arbitrary"),
    )(xa, dt, acs, acs_t, dskip, hprev, dy)


def ssd_post(ddt, dacs, dt, raw, bias, alog, *, name):
    T, H = ddt.shape
    B = C.blk
    tc = _pick(T, (4 * B, 2 * B, B))

    def body(ddt_ref, dacs_ref, dt_ref, raw_ref, b_ref, al_ref, draw_ref, db_ref, dal_ref):
        @pl.when(pl.program_id(0) == 0)
        def _():
            db_ref[...] = jnp.zeros_like(db_ref)
            dal_ref[...] = jnp.zeros_like(dal_ref)

        a = -jnp.exp(al_ref[...])
        lower = (lax.broadcasted_iota(jnp.int32, (B, B), 0) <= lax.broadcasted_iota(jnp.int32, (B, B), 1)).astype(F32)
        for j in range(tc // B):
            sl = slice(j * B, (j + 1) * B)
            rc = _dot_exact(lower, dacs_ref[sl, :])
            dtv = dt_ref[sl, :]
            draw = (ddt_ref[sl, :] + a * rc) * _sigmoid(raw_ref[sl, :] + b_ref[...])
            draw_ref[sl, :] = draw
            db_ref[...] += jnp.sum(draw, axis=0, keepdims=True)
            dal_ref[...] += jnp.sum(dtv * rc, axis=0, keepdims=True) * a

    blk = pl.BlockSpec((tc, H), lambda i: (i, 0))
    vec = pl.BlockSpec((1, H), lambda i: (0, 0))
    return pl.pallas_call(
        body, name=name, grid=(T // tc,), in_specs=[blk, blk, blk, blk, vec, vec], out_specs=[blk, vec, vec],
        out_shape=[jax.ShapeDtypeStruct((T, H), F32), jax.ShapeDtypeStruct((1, H), F32), jax.ShapeDtypeStruct((1, H), F32)],
        compiler_params=_params("arbitrary"),
    )(ddt, dacs, dt, raw, bias, alog)


MESH = pl.DeviceIdType.MESH
PACK_W = 1024
ANY = pl.BlockSpec(memory_space=pl.ANY)


def _place():
    return lax.axis_index("x"), lax.axis_index("y"), lax.axis_index("c")


def _other_chips(x, y):
    return [(1 - x, y), (x, 1 - y), (1 - x, 1 - y)]


def _remote(src, dst, send_sems, recv_sems, k, to):
    return pltpu.make_async_remote_copy(src_ref=src, dst_ref=dst, send_sem=send_sems.at[k], recv_sem=recv_sems.at[k],
                                        device_id=to, device_id_type=MESH)


def _half(c, rows):
    rh = rows // 2
    return pl.ds(pl.multiple_of(c * rh, 16 if rh % 16 == 0 else 8), rh)


def gather_layer(shards, *, name):
    n = len(shards)

    def body(*refs):
        ps, gs = refs[:n], refs[n:2 * n]
        send_sems, recv_sems = refs[2 * n:]
        x, y, c = _place()
        s = 2 * x + y
        sib = (x, y, 1 - c)
        chips = _other_chips(x, y)
        copy = functools.partial(_remote, send_sems=send_sems, recv_sems=recv_sems)
        started = []
        for w in range(n):
            started.append(copy(ps[w], gs[w].at[s], k=w, to=sib))
        for w in range(n):
            mine = _half(c, ps[w].shape[0])
            for j, (px, py) in enumerate(chips):
                started.append(copy(ps[w].at[mine], gs[w].at[s, mine], k=n + 3 * w + j, to=(px, py, c)))
        for cp in started:
            cp.start()
        passed = []
        for w in range(n):
            mine = _half(c, ps[w].shape[0])
            for j, (px, py) in enumerate(chips):
                there = gs[w].at[2 * px + py, mine]
                copy(there, there, k=n + 3 * w + j, to=(px, py, c)).wait_recv()
                fw = copy(there, there, k=4 * n + 3 * w + j, to=sib)
                fw.start()
                passed.append(fw)
        for w in range(n):
            copy(ps[w], gs[w].at[s], k=w, to=sib).wait_recv()
            theirs = _half(1 - c, ps[w].shape[0])
            for j, (px, py) in enumerate(chips):
                there = gs[w].at[2 * px + py, theirs]
                copy(there, there, k=4 * n + 3 * w + j, to=sib).wait_recv()
        for cp in started + passed:
            cp.wait_send()

    return pl.pallas_call(
        body, name=name, in_specs=[ANY] * n, out_specs=[ANY] * n,
        out_shape=[jax.ShapeDtypeStruct((4,) + p.shape, p.dtype) for p in shards],
        scratch_shapes=[pltpu.SemaphoreType.DMA((7 * n,)), pltpu.SemaphoreType.DMA((7 * n,))],
    )(*shards)


HBM = pl.BlockSpec(memory_space=pltpu.HBM)
SEMS = pl.BlockSpec(memory_space=pltpu.SEMAPHORE)


def gather_plan(x, y, c, ps, gs):
    s = 2 * x + y
    sends, lands = [], []
    for p, g in zip(ps, gs):
        rows = p.shape[0]
        sends.append((p, g.at[s], (x, y, 1 - c)))
        lands.append(g.at[s])
        for px, py in _other_chips(x, y):
            for pc in (c, 1 - c):
                sends.append((p.at[_half(c, rows)], g.at[s, _half(c, rows)], (px, py, pc)))
                lands.append(g.at[2 * px + py, _half(pc, rows)])
    return sends, lands


def scatter_plan(x, y, c, ps, gs):
    sends, lands = [], []
    for p, g in zip(ps, gs):
        for j, (px, py) in enumerate(_other_chips(x, y)):
            sends.append((p.at[2 * px + py], g.at[j], (px, py, c)))
            lands.append(g.at[j])
    return sends, lands


def _hbm(a):
    return pltpu.with_memory_space_constraint(a, pltpu.HBM)


def exchange_start(srcs, land_shapes, plan, ncopies, *, name):
    n = len(srcs)

    def body(*refs):
        ps, gs = refs[:n], refs[n:2 * n]
        send_sems, recv_sems = refs[2 * n], refs[2 * n + 1]
        token = refs[-1]
        sends, _ = plan(*_place(), ps, gs)
        for k, (src, dst, to) in enumerate(sends):
            _remote(src, dst, send_sems, recv_sems, k, to).start()
        token[...] = jnp.zeros_like(token)

    lands = [_hbm(lax.empty(s.shape, s.dtype)) for s in land_shapes]
    outs = pl.pallas_call(
        body, name=name,
        out_shape=[pltpu.SemaphoreType.DMA((ncopies,)), pltpu.SemaphoreType.DMA((ncopies,))]
        + [pltpu.HBM(a.shape, a.dtype) for a in srcs] + [pltpu.HBM(s.shape, s.dtype) for s in land_shapes]
        + [jax.ShapeDtypeStruct((8, 128), F32)],
        in_specs=[HBM] * (2 * n), out_specs=[SEMS, SEMS] + [HBM] * (2 * n) + [pl.BlockSpec(memory_space=pltpu.VMEM)],
        input_output_aliases={k: 2 + k for k in range(2 * n)},
        compiler_params=pltpu.CompilerParams(has_side_effects=pltpu.SideEffectType.DATAFLOW_SIDE_EFFECTING),
    )(*[_hbm(a) for a in srcs], *lands)
    return outs[0], outs[1], list(outs[2:2 + n]), list(outs[2 + n:2 + 2 * n]), outs[-1]


def exchange_wait(send_sems, recv_sems, srcs, lands, plan, after, *, name, items=None):
    n = len(srcs)
    items = list(range(n)) if items is None else items

    def body(*refs):
        ps, gs = refs[:n], refs[n:2 * n]
        send_sems, recv_sems = refs[2 * n], refs[2 * n + 1]
        sends, arrivals = plan(*_place(), ps, gs)
        per_item = len(sends) // n
        for j, ((src, dst, to), land) in enumerate(zip(sends, arrivals)):
            k = per_item * items[j // per_item] + j % per_item
            _remote(src, dst, send_sems, recv_sems, k, to).wait_send()
            _remote(land, land, send_sems, recv_sems, k, to).wait_recv()

    outs = pl.pallas_call(
        body, name=name,
        out_shape=[pltpu.HBM(a.shape, a.dtype) for a in srcs] + [pltpu.HBM(a.shape, a.dtype) for a in lands],
        in_specs=[HBM] * (2 * n) + [SEMS, SEMS, ANY], out_specs=[HBM] * (2 * n),
        input_output_aliases={k: k for k in range(2 * n)},
        compiler_params=pltpu.CompilerParams(has_side_effects=pltpu.SideEffectType.DATAFLOW_SIDE_EFFECTING),
    )(*srcs, *lands, send_sems, recv_sems, after)
    return list(outs[n:])


def swap_halves(items, *, name):
    n = len(items)

    def body(*refs):
        gs, rs = refs[:n], refs[n:2 * n]
        send_sems, recv_sems = refs[2 * n:]
        x, y, c = _place()
        cps = [_remote(g.at[:, _half(1 - c, g.shape[1])], r, send_sems, recv_sems, k, (x, y, 1 - c))
               for k, (g, r) in enumerate(zip(gs, rs))]
        for cp in cps:
            cp.start()
        for cp in cps:
            cp.wait()

    return pl.pallas_call(
        body, name=name, in_specs=[ANY] * n, out_specs=[ANY] * n,
        out_shape=[jax.ShapeDtypeStruct((4, g.shape[1] // 2, g.shape[2]), g.dtype) for g in items],
        scratch_shapes=[pltpu.SemaphoreType.DMA((n,)), pltpu.SemaphoreType.DMA((n,))],
    )(*items)


def join_halves(items, *, name):
    n = len(items)

    def body(*refs):
        rs, outs = refs[:n], refs[n:2 * n]
        send_sems, recv_sems = refs[2 * n:]
        x, y, c = _place()
        sib = (x, y, 1 - c)
        cps = []
        for k, (r, o) in enumerate(zip(rs, outs)):
            mine = _half(c, r.shape[0])
            cps.append(_remote(r.at[mine], o.at[mine], send_sems, recv_sems, k, sib))
            cps[-1].start()
        for k, (r, o) in enumerate(zip(rs, outs)):
            theirs = _half(1 - c, r.shape[0])
            _remote(r.at[theirs], o.at[theirs], send_sems, recv_sems, k, sib).wait_recv()
        for cp in cps:
            cp.wait_send()

    return pl.pallas_call(
        body, name=name, in_specs=[ANY] * n, out_specs=[ANY] * n,
        out_shape=[jax.ShapeDtypeStruct(r.shape, r.dtype) for r in items],
        input_output_aliases={k: k for k in range(n)},
        scratch_shapes=[pltpu.SemaphoreType.DMA((n,)), pltpu.SemaphoreType.DMA((n,))],
    )(*items)


def gather_all(v, *, name, total):
    rows, W = v.shape

    def body(v_ref, o_ref, *scr):
        buf = scr[0] if total else o_ref
        send_sems, recv_sems = scr[-2], scr[-1]
        x, y, c = _place()
        me = 4 * x + 2 * y + c
        flips = [(k >> 2 & 1, k >> 1 & 1, k & 1) for k in range(1, 8)]
        peers = [((1 - x) if fx else x, (1 - y) if fy else y, (1 - c) if fc else c) for fx, fy, fc in flips]
        out = []
        for k, peer in enumerate(peers):
            cp = pltpu.make_async_remote_copy(src_ref=v_ref, dst_ref=buf.at[me], send_sem=send_sems.at[k],
                                              recv_sem=recv_sems.at[k], device_id=peer, device_id_type=MESH)
            cp.start()
            out.append(cp)
        buf[me] = v_ref[...]
        for k, (px, py, pc) in enumerate(peers):
            pltpu.make_async_remote_copy(src_ref=v_ref, dst_ref=buf.at[4 * px + 2 * py + pc], send_sem=send_sems.at[k],
                                         recv_sem=recv_sems.at[k], device_id=(px, py, pc), device_id_type=MESH).wait_recv()
        for cp in out:
            cp.wait_send()
        if total:
            acc = buf[0]
            for d in range(1, 8):
                acc = acc + buf[d]
            o_ref[...] = acc

    vm = pl.BlockSpec(memory_space=pltpu.VMEM)
    return pl.pallas_call(
        body, name=name, in_specs=[vm], out_specs=vm,
        out_shape=jax.ShapeDtypeStruct((rows, W) if total else (8, rows, W), F32),
        scratch_shapes=([pltpu.VMEM((8, rows, W), F32)] if total else [])
        + [pltpu.SemaphoreType.DMA((7,)), pltpu.SemaphoreType.DMA((7,))],
    )(v)


def pair_sum(g, r, *, name):
    _, R, W = g.shape
    Rh = R // 2
    tr = _pick(Rh, (512, 256, 128, 64, 32, 16))
    nb = Rh // tr

    def body(g_ref, r_ref, o_ref):
        o_ref[...] = (g_ref[...].astype(F32) + r_ref[...].astype(F32)).astype(o_ref.dtype)

    return pl.pallas_call(
        body, name=name, grid=(4, nb),
        in_specs=[pl.BlockSpec((1, tr, W), lambda s, i: (s, lax.axis_index("c") * nb + i, 0)),
                  pl.BlockSpec((1, tr, W), lambda s, i: (s, i, 0))],
        out_specs=pl.BlockSpec((1, tr, W), lambda s, i: (s, i, 0)),
        out_shape=jax.ShapeDtypeStruct((4, Rh, W), g.dtype), compiler_params=_params("parallel", "parallel"),
    )(g, r)


def chip_sum(g, r1, r2, *, name):
    _, R, W = g.shape
    Rh = R // 2
    tr = _pick(Rh, (512, 256, 128, 64, 32, 16))
    nb = Rh // tr

    def body(g_ref, r1_ref, a_ref, b_ref, c_ref, o_ref):
        acc = g_ref[0].astype(F32) + r1_ref[0].astype(F32)
        for ref in (a_ref, b_ref, c_ref):
            acc = acc + ref[0].astype(F32)
        o_ref[...] = acc

    shard = lambda: 2 * lax.axis_index("x") + lax.axis_index("y")
    half = lambda i: lax.axis_index("c") * nb + i
    return pl.pallas_call(
        body, name=name, grid=(nb,),
        in_specs=[pl.BlockSpec((1, tr, W), lambda i: (shard(), half(i), 0)),
                  pl.BlockSpec((1, tr, W), lambda i: (shard(), i, 0))]
        + [pl.BlockSpec((1, tr, W), lambda i, j=j: (j, i, 0)) for j in range(3)],
        out_specs=pl.BlockSpec((tr, W), lambda i: (half(i), 0)),
        out_shape=jax.ShapeDtypeStruct((R, W), F32), compiler_params=_params("parallel"),
    )(g, r1, r2, r2, r2)


BIG = (("w_in", False), ("pool_w", False), ("w_attn_br", False), ("w_pool_br", False), ("w_ssm_br", True),
       ("w_out", True), ("w_gate_up", False), ("w_down", True))


def _join(piece, axis):
    t = jnp.moveaxis(piece, 0, axis)
    shp = t.shape
    return t.reshape(shp[:axis] + (shp[axis] * shp[axis + 1],) + shp[axis + 2:])


def _seg_bounds():
    aw, kw, _, pw, di, cc, h, gd = C.in_widths
    o = [0, aw + 2 * kw]
    for wdt in (pw, di, cc, h, gd):
        o.append(o[-1] + wdt)
    return o


IN_PAD = 640


def _in_segments(blocks):
    w_in = jnp.concatenate([blocks[s] for s in range(4)], axis=1)
    o = _seg_bounds()
    segs = [w_in[:, o[i]:o[i + 1]] for i in range(6)]
    segs[4] = jnp.pad(segs[4], ((0, 0), (0, 128 - C.H)))
    width = sum(t.shape[1] for t in segs)
    pad = -width % IN_PAD
    return segs, jnp.concatenate(segs + [jnp.zeros((w_in.shape[0], pad), w_in.dtype)], axis=1)


def _layer_fwd_mix(x, p, tag):
    nm = lambda s: f"{s}_{tag}"
    H = C.H
    h = rms_fwd(x, p["ln1_w"], name=nm("rms1"))
    wq, wu, wz, wx, wd, wg = p["in_segs"]
    qkv = matmul(h, wq, name=nm("mm_qkv"), out_dtype=MXU)
    u = matmul(h, wu, name=nm("mm_u"))
    z = matmul(h, wz, name=nm("mm_z"))
    xbc = matmul(h, wx, name=nm("mm_xbc"))
    dtp = matmul(h, wd, name=nm("mm_dt"))
    gl = matmul(h, wg, name=nm("mm_gate"), out_dtype=MXU)
    att = attn_fwd(qkv, p["attn_sink"], name=nm("attn_fwd"))
    pool, mixed = pool_fwd(u, p["pool_w"], p["pool_scale"], name=nm("pool_fwd"))
    xa = conv_fwd(xbc, p["conv_w"], p["conv_b"], name=nm("conv_fwd"))
    raw = dtp[:, :H]
    dt_t, acs_t = ssd_prep(raw.T, p["dt_bias"].T, p["a_log"].T, name=nm("ssd_prep"))
    dt, acs = dt_t.T, acs_t.T
    y, hprev = ssd_fwd(xa, dt, acs, acs_t, p["d_skip"], name=nm("ssd_fwd"))
    ssm = gnorm_fwd(y, z, p["ssm_norm_w"], name=nm("gnorm_fwd"))
    return dict(x=x, h=h, qkv=qkv, z=z, xbc=xbc, raw=raw, gl=gl, att=att, pool=pool, mixed=mixed, xa=xa, dt=dt, acs=acs,
                acs_t=acs_t, y=y, hprev=hprev, ssm=ssm)


def _layer_fwd_out(s, p, tag):
    nm = lambda t: f"{t}_{tag}"
    x, gl, att, pool, ssm = s["x"], s["gl"], s["att"], s["pool"], s["ssm"]
    abr = matmul(att, name=nm("mm_abr"), out_dtype=MXU, **p["w_attn_br"])
    pbr = matmul(pool, name=nm("mm_pbr"), out_dtype=MXU, **p["w_pool_br"])
    sbr = matmul(ssm, name=nm("mm_sbr"), out_dtype=MXU, **p["w_ssm_br"])
    merged = merge_fwd(gl, abr, pbr, sbr, name=nm("merge_fwd"))
    xm = matmul(merged, add=x, name=nm("mm_out"), **p["w_out"])
    h2 = rms_fwd(xm, p["ln2_w"], name=nm("rms2"))
    gu = matmul(h2, name=nm("mm_gu"), out_dtype=MXU, **p["w_gate_up"])
    act = swiglu_fwd(gu, name=nm("swiglu_fwd"))
    xo = matmul(act, add=xm, name=nm("mm_down"), **p["w_down"])
    return xo, dict(s, abr=abr, pbr=pbr, sbr=sbr, merged=merged, xm=xm, h2=h2, gu=gu, act=act)


def _layer_bwd(dxo, dxo_m, p, s, tag):
    nm = lambda t: f"{t}_{tag}"
    H = C.H
    g = {}

    def rows4(t):
        return t.reshape(4, t.shape[0] // 4, t.shape[1])

    g["w_down"] = rows4(matmul(s["act"], dxo_m, ta=True, out_dtype=MXU, name=nm("mmg_down")))
    dact = matmul(dxo_m, tb=True, out_dtype=MXU, name=nm("mmb_down"), **p["w_down"])
    dgu = swiglu_bwd(s["gu"], dact, name=nm("swiglu_bwd"))
    g["w_gate_up"] = matmul(s["h2"], dgu, ta=True, out_dtype=MXU, out_cols=True, name=nm("mmg_gu"))
    dh2 = matmul(dgu, tb=True, name=nm("mmb_gu"), **p["w_gate_up"])
    dxm, dxm_m, g["ln2_w"] = rms_bwd(s["xm"], p["ln2_w"], dh2, dxo, name=nm("rms2_bwd"))
    g["w_out"] = rows4(matmul(s["merged"], dxm_m, ta=True, out_dtype=MXU, name=nm("mmg_out")))
    dmerged = matmul(dxm_m, tb=True, out_dtype=MXU, name=nm("mmb_out"), **p["w_out"])
    dabr, dpbr, dsbr, dgl = merge_bwd(s["gl"], s["abr"], s["pbr"], s["sbr"], dmerged, name=nm("merge_bwd"))
    g["w_attn_br"] = matmul(s["att"], dabr, ta=True, out_dtype=MXU, out_cols=True, name=nm("mmg_abr"))
    g["w_pool_br"] = matmul(s["pool"], dpbr, ta=True, out_dtype=MXU, out_cols=True, name=nm("mmg_pbr"))
    g["w_ssm_br"] = rows4(matmul(s["ssm"], dsbr, ta=True, out_dtype=MXU, name=nm("mmg_sbr")))
    datt = matmul(dabr, tb=True, out_dtype=MXU, name=nm("mmb_abr"), **p["w_attn_br"])
    dpool = matmul(dpbr, tb=True, out_dtype=MXU, name=nm("mmb_pbr"), **p["w_pool_br"])
    dssm = matmul(dsbr, tb=True, out_dtype=MXU, name=nm("mmb_sbr"), **p["w_ssm_br"])
    dq, dk, dv, g["attn_sink"] = attn_bwd(s["qkv"], p["attn_sink"], datt, name=nm("attn_bwd"))
    du, dpw, g["pool_scale"] = pool_bwd(dpool, s["mixed"], p["pool_w"], p["pool_scale"], name=nm("pool_bwd"))
    pg = dpw.shape[1] // 4
    g["pool_w"] = jnp.moveaxis(dpw.reshape(4, 4, pg, dpw.shape[2]), 1, 0).reshape(4, 4 * pg, dpw.shape[2]).astype(MXU)
    dy, dz, g["ssm_norm_w"] = gnorm_bwd(s["y"], s["z"], p["ssm_norm_w"], dssm, name=nm("gnorm_bwd"))
    dxa, ddt, dacs, g["d_skip"] = ssd_bwd(s["xa"], s["dt"], s["acs"], s["acs_t"], p["d_skip"], s["hprev"], dy,
                                          name=nm("ssd_bwd"))
    draw, g["dt_bias"], g["a_log"] = ssd_post(ddt, dacs, s["dt"], s["raw"], p["dt_bias"], p["a_log"], name=nm("ssd_post"))
    dxbc, g["conv_w"], g["conv_b"] = conv_bwd(s["xbc"], p["conv_w"], p["conv_b"], dxa, name=nm("conv_bwd"))
    return g, dict(dq=dq, dk=dk, dv=dv, du=du, dz=dz, dxbc=dxbc, draw=draw, dgl=dgl, dxm=dxm)


def _layer_bwd_in(c, p, s, tag):
    nm = lambda t: f"{t}_{tag}"
    H = C.H
    dq, dk, dv, du, dz, dxbc, draw, dgl, dxm = (c[k] for k in ("dq", "dk", "dv", "du", "dz", "dxbc", "draw", "dgl", "dxm"))
    w_all = p["in_all"]
    ddtp = jnp.pad(draw, ((0, 0), (0, 128 - H))).astype(MXU)
    parts = [dq, dk, dv, du, dz, dxbc, ddtp, dgl]
    used = sum(t.shape[1] for t in parts)
    dproj = jnp.concatenate(parts + [jnp.zeros((dq.shape[0], w_all.shape[1] - used), MXU)], axis=1)
    dh = matmul(dproj, w_all, tb=True, name=nm("mmb_in"))
    g_all = matmul(s["h"], dproj, ta=True, out_dtype=MXU, name=nm("mmg_in"))
    o = _seg_bounds()
    dt0 = o[4]
    g_in = jnp.concatenate([g_all[:, :dt0 + H], g_all[:, dt0 + 128:dt0 + 128 + (o[6] - o[5])]], axis=1)
    nc = g_in.shape[1] // 4
    g_w_in = jnp.stack([g_in[:, k * nc:(k + 1) * nc] for k in range(4)])
    dx, dx_m, g_ln1 = rms_bwd(s["x"], p["ln1_w"], dh, dxm, name=nm("rms1_bwd"))
    return dx, dx_m, g_w_in, g_ln1


SMALL = ("ln1_w", "attn_sink", "conv_w", "conv_b", "dt_bias", "a_log", "d_skip", "ssm_norm_w", "pool_scale", "ln2_w")
WEIGHTS = ("ln1_w", "w_in", "attn_sink", "conv_w", "conv_b", "dt_bias", "a_log", "d_skip", "ssm_norm_w", "pool_w",
           "pool_scale", "w_attn_br", "w_pool_br", "w_ssm_br", "w_out", "ln2_w", "w_gate_up", "w_down", "final_w")


def _step(x, loss_target, w, m, v):
    depth = C.depth
    xi, yi, ci = _place()
    shard = 2 * xi + yi

    names = [n for n, _ in BIG]
    nbig = len(names)
    cw_cols = w["conv_w"].shape[2]

    def block2d(n, i):
        return w[n][i].astype(MXU).reshape(-1, w[n].shape[-1])

    def mix_shards(i):
        return [block2d("w_in", i), block2d("pool_w", i), w["conv_w"][i].reshape(CONV_TAP_ROWS, -1)]

    def out_shards(i):
        return [block2d(n, i) for n, _ in BIG[2:]]

    def mix_params(i, blocks):
        p = {n: w[n][i][None] for n in SMALL if n != "conv_w"}
        p["in_segs"], p["in_all"] = _in_segments(blocks[0])
        pw = blocks[1]
        p["pool_w"] = _join(pw.reshape(4, 4, pw.shape[1] // 4, pw.shape[2]), 1)
        p["conv_w"] = _join(blocks[2].reshape(4, CONV_K, cw_cols), 1)
        return p

    def out_params(blocks):
        return {n: dict(b=t.reshape(4 * t.shape[1], t.shape[2])) if rw else dict(b=t, b_cols=True)
                for (n, rw), t in zip(BIG[2:], blocks)}

    def gather_start(src, tag):
        return exchange_start(src, [jax.ShapeDtypeStruct((4,) + a.shape, a.dtype) for a in src], gather_plan,
                              7 * len(src), name=f"gather_{tag}_start")

    xs = x[0]
    layers, saved = [], []
    n_mix = len(mix_shards(0))
    mix_blocks = gather_layer(mix_shards(0), name="gather_mix_l0")

    def after(src, scalar):
        return [src[0] + jnp.minimum(jnp.abs(scalar), 0.0).astype(src[0].dtype)] + src[1:]

    pend_out = gather_start(after(out_shards(0), mix_blocks[2][0, 0, 0]), "out_l0")
    started = pend_out[4][0, 0]
    for i in range(depth):
        nxt = None
        if i + 1 < depth:
            prev = pend_out[4][0, 0] if i == 0 else xs[0, 0]
            nxt = gather_start(after(mix_shards(i + 1) + out_shards(i + 1), prev), f"l{i + 1}")
            started = started + nxt[4][0, 0]
        p = mix_params(i, mix_blocks)
        p["ln1_w"] = p["ln1_w"] + started
        started = jnp.zeros((), F32)
        half = _layer_fwd_mix(xs, p, f"l{i}")
        ss, rs, srcs, lands, _ = pend_out
        items = None if i == 0 else list(range(n_mix, len(srcs) + n_mix))
        p.update(out_params(exchange_wait(ss, rs, srcs, lands, gather_plan, half["ssm"], items=items,
                                          name=f"gather_out_l{i}_wait")))
        layers.append(p)
        xs, sv = _layer_fwd_out(half, p, f"l{i}")
        saved.append(sv)
        if nxt is not None:
            ss, rs, srcs, lands, _ = nxt
            mix_blocks = exchange_wait(ss, rs, srcs[:n_mix], lands[:n_mix], gather_plan, xs, items=list(range(n_mix)),
                                       name=f"gather_mix_l{i + 1}_wait")
            pend_out = (ss, rs, srcs[n_mix:], lands[n_mix:], None)
    loss_part, dx, dx_m, g_final = final_loss(xs, w["final_w"][None], loss_target[0], name="final_loss")

    def pair_sums(i, ns, gs):
        sb = swap_halves(gs, name=f"swap_halves_{ns[0]}_l{i}")
        return sb, [pair_sum(g, r, name=f"pair_sum_{n}_l{i}") for n, g, r in zip(ns, gs, sb)]

    def scatter_start(chip, tag):
        return exchange_start(chip, [jax.ShapeDtypeStruct((3,) + a.shape[1:], a.dtype) for a in chip], scatter_plan,
                              3 * len(chip), name=f"scatter_{tag}_start")

    def scatter_wait(pend, after, tag):
        return exchange_wait(pend[0], pend[1], pend[2], pend[3], scatter_plan, after, name=f"scatter_{tag}_wait")

    grads = [None] * depth
    pieces = {}
    rest = names[1:]
    pend_in = None
    for i in reversed(range(depth)):
        if pend_in is not None:
            dx_m = dx_m + pend_in[0][4][0, 0].astype(dx_m.dtype)
        grads[i], cot = _layer_bwd(dx, dx_m, layers[i], saved[i], f"l{i}")
        g_rest = [grads[i][n] for n in rest]
        sib_rest, chip_rest = pair_sums(i, rest, g_rest)
        pend_rest = scatter_start(chip_rest, f"rest_l{i}")
        cot["draw"] = cot["draw"] + pend_rest[4][0, 0]
        dx, dx_m, g_in, grads[i]["ln1_w"] = _layer_bwd_in(cot, layers[i], saved[i], f"l{i}")
        if pend_in is not None:
            pend, g_up, sib_up = pend_in
            pieces[(i + 1, "w_in")] = (g_up, sib_up[0], scatter_wait(pend, dx, f"in_l{i + 1}")[0])
        for n, g, r, o in zip(rest, g_rest, sib_rest, scatter_wait(pend_rest, dx, f"rest_l{i}")):
            pieces[(i, n)] = (g, r, o)
        sib_in, chip_in = pair_sums(i, ["w_in"], [g_in])
        pend_in = (scatter_start(chip_in, f"in_l{i}"), g_in, sib_in)
    keys = [(i, n) for i in range(depth) for n in names if (i, n) != (0, "w_in")]
    g_first, r_first, o_first = pieces[keys[0]]
    pieces[keys[0]] = (g_first + pend_in[0][4][0, 0].astype(g_first.dtype), r_first, o_first)
    mine = [chip_sum(*pieces[k], name=f"chip_sum_{k[1]}_l{k[0]}") for k in keys]
    reduced = dict(zip(keys, join_halves(mine, name="join_halves")))
    gout = {}

    small = [jnp.stack([grads[i][n].reshape(-1) for i in range(depth)]).reshape(1, -1) for n in SMALL]
    small += [g_final.reshape(1, -1), loss_part.reshape(1, -1)]
    tot = gather_all(_pack_small(small), name="sum_small", total=True)
    parts = _unpack_small(tot, [t.shape[1] for t in small])
    for n, t in zip(SMALL, parts):
        if n == "conv_w":
            cols = w[n].shape[2]
            gout[n] = lax.dynamic_slice_in_dim(t.reshape(depth, CONV_K, -1), shard * cols, cols, axis=2)
        else:
            gout[n] = t.reshape(w[n].shape)
    gout["final_w"] = parts[-2].reshape(w["final_w"].shape)
    loss = parts[-1].reshape(())

    upd = {}
    for n in [t for t in WEIGHTS if t != "w_in"] + ["w_in"]:
        if n == "w_in":
            pend, g_in, sib_in = pend_in
            landed = scatter_wait(pend, upd["w_down"][1], "in_l0")[0]
            mine_in = chip_sum(g_in, sib_in[0], landed, name="chip_sum_w_in_l0")
            reduced[(0, n)] = join_halves([mine_in], name="join_halves_w_in_l0")[0]
        if n in gout:
            upd[n] = (gout[n].reshape(w[n].shape),) + adamw(w[n], gout[n].reshape(w[n].shape), m[n], v[n], name=f"adamw_{n}")
        else:
            upd[n] = adamw_layers(w[n], [reduced[(i, n)] for i in range(depth)], m[n], v[n], name=f"adamw_{n}")
    return (loss, dx[None], *[upd[n][0] for n in WEIGHTS], *[upd[n][1] for n in WEIGHTS],
            *[upd[n][2] for n in WEIGHTS], *[upd[n][3] for n in WEIGHTS])


def _pack_small(parts):
    flat = jnp.concatenate(parts, axis=1)
    rows = -(-flat.shape[1] // PACK_W)
    rows = -(-rows // 8) * 8
    return jnp.pad(flat, ((0, 0), (0, rows * PACK_W - flat.shape[1]))).reshape(rows, PACK_W)


def _unpack_small(buf, sizes):
    flat = buf.reshape(-1)
    out, off = [], 0
    for n in sizes:
        out.append(flat[off:off + n])
        off += n
    return out


def kernel(x, ln1_w, w_in, attn_sink, conv_w, conv_b, dt_bias, a_log, d_skip, ssm_norm_w, pool_w, pool_scale, w_attn_br, w_pool_br, w_ssm_br, w_out, ln2_w, w_gate_up, w_down, final_w, loss_target, m_ln1_w, m_w_in, m_attn_sink, m_conv_w, m_conv_b, m_dt_bias, m_a_log, m_d_skip, m_ssm_norm_w, m_pool_w, m_pool_scale, m_w_attn_br, m_w_pool_br, m_w_ssm_br, m_w_out, m_ln2_w, m_w_gate_up, m_w_down, m_final_w, v_ln1_w, v_w_in, v_attn_sink, v_conv_w, v_conv_b, v_dt_bias, v_a_log, v_d_skip, v_ssm_norm_w, v_pool_w, v_pool_scale, v_w_attn_br, v_w_pool_br, v_w_ssm_br, v_w_out, v_ln2_w, v_w_gate_up, v_w_down, v_final_w):
    w = dict(ln1_w=ln1_w, w_in=w_in, attn_sink=attn_sink, conv_w=conv_w, conv_b=conv_b, dt_bias=dt_bias, a_log=a_log,
             d_skip=d_skip, ssm_norm_w=ssm_norm_w, pool_w=pool_w, pool_scale=pool_scale, w_attn_br=w_attn_br,
             w_pool_br=w_pool_br, w_ssm_br=w_ssm_br, w_out=w_out, ln2_w=ln2_w, w_gate_up=w_gate_up, w_down=w_down,
             final_w=final_w)
    m = dict(ln1_w=m_ln1_w, w_in=m_w_in, attn_sink=m_attn_sink, conv_w=m_conv_w, conv_b=m_conv_b, dt_bias=m_dt_bias,
             a_log=m_a_log, d_skip=m_d_skip, ssm_norm_w=m_ssm_norm_w, pool_w=m_pool_w, pool_scale=m_pool_scale,
             w_attn_br=m_w_attn_br, w_pool_br=m_w_pool_br, w_ssm_br=m_w_ssm_br, w_out=m_w_out, ln2_w=m_ln2_w,
             w_gate_up=m_w_gate_up, w_down=m_w_down, final_w=m_final_w)
    v = dict(ln1_w=v_ln1_w, w_in=v_w_in, attn_sink=v_attn_sink, conv_w=v_conv_w, conv_b=v_conv_b, dt_bias=v_dt_bias,
             a_log=v_a_log, d_skip=v_d_skip, ssm_norm_w=v_ssm_norm_w, pool_w=v_pool_w, pool_scale=v_pool_scale,
             w_attn_br=v_w_attn_br, w_pool_br=v_w_pool_br, w_ssm_br=v_w_ssm_br, w_out=v_w_out, ln2_w=v_ln2_w,
             w_gate_up=v_w_gate_up, w_down=v_w_down, final_w=v_final_w)
    return _step(x, loss_target, w, m, v)
```

```python
import functools

import jax
import jax.numpy as jnp
from jax import lax
from jax.experimental import pallas as pl
from jax.experimental.pallas import tpu as pltpu

F32 = jnp.float32
MXU = jnp.bfloat16
VMEM_LIMIT = 56 * 1024 * 1024
EPS = 1e-6
NEG = -1e30

ADAM_LR, ADAM_B1, ADAM_B2, ADAM_EPS, ADAM_WD, ADAM_STEP = 0.001, 0.9, 0.999, 1e-08, 0.01, 10


class Cfg:
    def __init__(self, d_model=2048, seq=8192, depth=2, q_heads=16, kv_heads=4, head_dim=64,
                 ssm_head_dim=64, ssm_groups=4, d_state=128):
        self.D, self.T, self.depth = d_model, seq, depth
        self.hd, self.qh, self.kvh = head_dim, q_heads, kv_heads
        self.AW, self.KW = q_heads * head_dim, kv_heads * head_dim
        self.blk = 128
        self.PW = d_model // 2
        self.PG = self.PW // 4
        self.DI = d_model
        self.P = ssm_head_dim
        self.H = self.DI // self.P
        self.G = ssm_groups
        self.HG = self.H // self.G
        self.N = d_state
        self.CC = self.DI + 2 * self.G * self.N
        self.F = -(-8 * d_model // (3 * 256)) * 256
        self.in_widths = (self.AW, self.KW, self.KW, self.PW, self.DI, self.CC, self.H, 3 * d_model)
        self.in_cols = sum(self.in_widths)


C = Cfg()
POOL_WINDOWS = (2, 4, 8, 16)


def _pick(n, cands):
    for c in cands:
        if n % c == 0:
            return c
    return n


def _params(*sem):
    return pltpu.CompilerParams(dimension_semantics=sem, vmem_limit_bytes=VMEM_LIMIT)


def _sigmoid(x):
    return 0.5 * jnp.tanh(0.5 * x) + 0.5


def _dot(a, b, dims):
    return lax.dot_general(a.astype(MXU), b.astype(MXU), (dims, ((), ())), preferred_element_type=F32)


def _dot_nn(a, b):
    return _dot(a, b, ((1,), (0,)))


def _dot_nt(a, b):
    return _dot(a, b, ((1,), (1,)))


def _dot_tn(a, b):
    return _dot(a, b, ((0,), (0,)))


MM_VMEM = 40 * 1024 * 1024
LANES = 128


def _tile(n, cap):
    best = None
    for d in range(LANES, min(n, cap) + 1, LANES):
        if n % d == 0:
            best = d
    return n if best is None else best


def _divisors(n, cap):
    ds = [d for d in range(LANES, min(n, cap) + 1, LANES) if n % d == 0]
    return ds or [n]


HBM_RATE, MXU_RATE, ACC_RATE, STEP_COST = 3.0e12, 0.9e15, 1.2e13, 0.35e-6
MXU_WIDTH = 256


def _choose_tiles(M, N, K, n_unit, k_unit, sa, sb, so, sadd, ta):
    best = None
    for tm in sorted({_tile(M, 1024), _tile(M, 512)}):
        for tk in _divisors(k_unit, 4096):
            for tn in _divisors(n_unit, 2048):
                nk = K // tk
                vmem = 2 * (tm * tk * sa + tk * tn * sb + tm * tn * (so + sadd)) + (tm * tn * 4 if nk > 1 else 0)
                if vmem > MM_VMEM:
                    continue
                steps = (M // tm) * (N // tn) * nk
                a_bytes = M * K * sa * (N // tn if (nk > 1 or ta) else 1)
                b_bytes = K * N * sb * (M // tm)
                hbm = (a_bytes + b_bytes + M * N * (so + sadd)) / HBM_RATE
                fill = (tn / (-(-tn // MXU_WIDTH) * MXU_WIDTH)) * (tk / (-(-tk // MXU_WIDTH) * MXU_WIDTH))
                mxu = 2.0 * M * N * K / (MXU_RATE * fill)
                acc = steps * tm * tn * 8 / ACC_RATE if nk > 1 else 0.0
                cost = max(hbm, mxu) + acc + steps * STEP_COST
                if best is None or cost < best[0]:
                    best = (cost, tm, tn, tk)
    assert best is not None, (M, N, K)
    return best[1:]


def matmul(a, b, *, name, ta=False, tb=False, out_dtype=F32, add=None, layer=None, b_cols=False, out_cols=False):
    M, K = (a.shape[1], a.shape[0]) if ta else a.shape
    rows, cols = b.shape[-2], (4 if b_cols else 1) * b.shape[-1]
    N, bk = (rows, cols) if tb else (cols, rows)
    assert K == bk, (a.shape, b.shape)
    n_unit = N // 4 if (out_cols or (b_cols and not tb)) else N
    k_unit = K // 4 if (b_cols and tb) else K
    sa, sb, so = a.dtype.itemsize, b.dtype.itemsize, jnp.dtype(out_dtype).itemsize
    sadd = add.dtype.itemsize if add is not None else 0
    tm, tn, tk = _choose_tiles(M, N, K, n_unit, k_unit, sa, sb, so, sadd, ta)
    nk = K // tk
    a_spec = pl.BlockSpec((tk, tm), lambda i, j, k: (k, i)) if ta else pl.BlockSpec((tm, tk), lambda i, j, k: (i, k))
    if b_cols and tb:
        per = k_unit // tk
        b_spec = pl.BlockSpec((None, tn, tk), lambda i, j, k: (k // per, j, k % per))
    elif b_cols:
        per = n_unit // tn
        b_spec = pl.BlockSpec((None, tk, tn), lambda i, j, k: (j // per, k, j % per))
    elif layer is not None:
        b_spec = (pl.BlockSpec((None, tn, tk), lambda i, j, k: (layer, j, k)) if tb
                  else pl.BlockSpec((None, tk, tn), lambda i, j, k: (layer, k, j)))
    else:
        b_spec = pl.BlockSpec((tn, tk), lambda i, j, k: (j, k)) if tb else pl.BlockSpec((tk, tn), lambda i, j, k: (k, j))
    if out_cols:
        per = n_unit // tn
        o_spec = pl.BlockSpec((None, tm, tn), lambda i, j, k: (j // per, i, j % per))
        out_shape = jax.ShapeDtypeStruct((4, M, N // 4), out_dtype)
    else:
        o_spec = pl.BlockSpec((tm, tn), lambda i, j, k: (i, j))
        out_shape = jax.ShapeDtypeStruct((M, N), out_dtype)
    dims = ((0 if ta else 1,), (1 if tb else 0,))
    has_add = add is not None

    def body(*refs):
        a_ref, b_ref = refs[0], refs[1]
        add_ref = refs[2] if has_add else None
        o_ref = refs[3] if has_add else refs[2]
        part = _dot(a_ref[...], b_ref[...], dims)

        def finish(total):
            if has_add:
                total = total + add_ref[...].astype(F32)
            o_ref[...] = total.astype(o_ref.dtype)

        if nk == 1:
            finish(part)
        else:
            acc_ref = refs[-1]
            k = pl.program_id(2)

            @pl.when(k == 0)
            def _():
                acc_ref[...] = part

            @pl.when(k > 0)
            def _():
                acc_ref[...] += part

            @pl.when(k == nk - 1)
            def _():
                finish(acc_ref[...])

    in_specs = [a_spec, b_spec] + ([o_spec] if has_add else [])
    args = (a, b) + ((add,) if has_add else ())
    return pl.pallas_call(
        body, name=name, grid=(M // tm, N // tn, nk), in_specs=in_specs, out_specs=o_spec, out_shape=out_shape,
        scratch_shapes=[pltpu.VMEM((tm, tn), F32)] if nk > 1 else [],
        compiler_params=_params("parallel", "parallel", "arbitrary"),
    )(*args)


def _row_tile(t):
    return _pick(t, (256, 128, 64, 32, 16, 8))


def rms_fwd(x, w, *, name):
    T, D = x.shape
    tr = _row_tile(T)

    def body(x_ref, w_ref, o_ref):
        xv = x_ref[...]
        r = lax.rsqrt(jnp.mean(xv * xv, axis=-1, keepdims=True) + EPS)
        o_ref[...] = (xv * r * w_ref[...]).astype(o_ref.dtype)

    row = pl.BlockSpec((tr, D), lambda i: (i, 0))
    return pl.pallas_call(
        body, name=name, grid=(T // tr,), in_specs=[row, pl.BlockSpec((1, D), lambda i: (0, 0))], out_specs=row,
        out_shape=jax.ShapeDtypeStruct((T, D), MXU), compiler_params=_params("parallel"),
    )(x, w)


def rms_bwd(x, w, dh, dres, *, name):
    T, D = x.shape
    tr = _row_tile(T)

    def body(x_ref, w_ref, dh_ref, dres_ref, dx_ref, dxm_ref, dw_ref):
        xv = x_ref[...]
        dhv = dh_ref[...].astype(F32)
        r = lax.rsqrt(jnp.mean(xv * xv, axis=-1, keepdims=True) + EPS)
        g = dhv * w_ref[...]
        dot = jnp.mean(g * xv, axis=-1, keepdims=True)
        dx = dres_ref[...] + r * g - xv * (r * r * r * dot)
        dx_ref[...] = dx
        dxm_ref[...] = dx.astype(dxm_ref.dtype)
        part = jnp.sum(dhv * xv * r, axis=0, keepdims=True)

        @pl.when(pl.program_id(0) == 0)
        def _():
            dw_ref[...] = part

        @pl.when(pl.program_id(0) > 0)
        def _():
            dw_ref[...] += part

    row = pl.BlockSpec((tr, D), lambda i: (i, 0))
    vec = pl.BlockSpec((1, D), lambda i: (0, 0))
    return pl.pallas_call(
        body, name=name, grid=(T // tr,), in_specs=[row, vec, row, row], out_specs=[row, row, vec],
        out_shape=[jax.ShapeDtypeStruct((T, D), F32), jax.ShapeDtypeStruct((T, D), MXU),
                   jax.ShapeDtypeStruct((1, D), F32)],
        compiler_params=_params("arbitrary"),
    )(x, w, dh, dres)


def final_loss(x, w, target, *, name):
    T, D = x.shape
    tr = _row_tile(T)

    def body(x_ref, w_ref, t_ref, loss_ref, dx_ref, dxm_ref, dw_ref):
        xv = x_ref[...]
        wv = w_ref[...]
        r = lax.rsqrt(jnp.mean(xv * xv, axis=-1, keepdims=True) + EPS)
        err = xv * r * wv - t_ref[...]
        lpart = 0.5 * jnp.sum(jnp.mean(err * err, axis=-1, keepdims=True), axis=0, keepdims=True)
        dy = err * (1.0 / D)
        g = dy * wv
        dot = jnp.mean(g * xv, axis=-1, keepdims=True)
        dx = r * g - xv * (r * r * r * dot)
        dx_ref[...] = dx
        dxm_ref[...] = dx.astype(dxm_ref.dtype)
        part = jnp.sum(dy * xv * r, axis=0, keepdims=True)

        @pl.when(pl.program_id(0) == 0)
        def _():
            dw_ref[...] = part
            loss_ref[...] = lpart

        @pl.when(pl.program_id(0) > 0)
        def _():
            dw_ref[...] += part
            loss_ref[...] += lpart

    row = pl.BlockSpec((tr, D), lambda i: (i, 0))
    vec = pl.BlockSpec((1, D), lambda i: (0, 0))
    one = pl.BlockSpec((1, 1), lambda i: (0, 0))
    return pl.pallas_call(
        body, name=name, grid=(T // tr,), in_specs=[row, vec, row], out_specs=[one, row, row, vec],
        out_shape=[jax.ShapeDtypeStruct((1, 1), F32), jax.ShapeDtypeStruct((T, D), F32),
                   jax.ShapeDtypeStruct((T, D), MXU), jax.ShapeDtypeStruct((1, D), F32)],
        compiler_params=_params("arbitrary"),
    )(x, w, target)


def swiglu_fwd(gu, *, name):
    T, F2 = gu.shape
    F = F2 // 2
    tr = _row_tile(T)

    def body(g_ref, u_ref, o_ref):
        g = g_ref[...].astype(F32)
        o_ref[...] = (g * _sigmoid(g) * u_ref[...].astype(F32)).astype(o_ref.dtype)

    return pl.pallas_call(
        body, name=name, grid=(T // tr,),
        in_specs=[pl.BlockSpec((tr, F), lambda i: (i, 0)), pl.BlockSpec((tr, F), lambda i: (i, 1))],
        out_specs=pl.BlockSpec((tr, F), lambda i: (i, 0)),
        out_shape=jax.ShapeDtypeStruct((T, F), MXU), compiler_params=_params("parallel"),
    )(gu, gu)


def swiglu_bwd(gu, dact, *, name):
    T, F2 = gu.shape
    F = F2 // 2
    tr = _row_tile(T)

    def body(g_ref, u_ref, d_ref, o_ref):
        g = g_ref[...].astype(F32)
        d = d_ref[...].astype(F32)
        s = _sigmoid(g)
        o_ref[:, :F] = (d * u_ref[...].astype(F32) * s * (1.0 + g * (1.0 - s))).astype(o_ref.dtype)
        o_ref[:, F:] = (d * g * s).astype(o_ref.dtype)

    lo = pl.BlockSpec((tr, F), lambda i: (i, 0))
    hi = pl.BlockSpec((tr, F), lambda i: (i, 1))
    return pl.pallas_call(
        body, name=name, grid=(T // tr,), in_specs=[lo, hi, lo], out_specs=pl.BlockSpec((tr, F2), lambda i: (i, 0)),
        out_shape=jax.ShapeDtypeStruct((T, F2), MXU), compiler_params=_params("parallel"),
    )(gu, gu, dact)


def merge_fwd(gl, abr, pbr, sbr, *, name):
    T, D = abr.shape
    tr = _row_tile(T)

    def body(g0, g1, g2, a_ref, p_ref, s_ref, o_ref):
        m = sum(_sigmoid(g[...].astype(F32)) * b[...].astype(F32) for g, b in ((g0, a_ref), (g1, p_ref), (g2, s_ref)))
        o_ref[...] = m.astype(o_ref.dtype)

    row = pl.BlockSpec((tr, D), lambda i: (i, 0))
    gs = [pl.BlockSpec((tr, D), lambda i, j=j: (i, j)) for j in range(3)]
    return pl.pallas_call(
        body, name=name, grid=(T // tr,), in_specs=gs + [row, row, row], out_specs=row,
        out_shape=jax.ShapeDtypeStruct((T, D), MXU), compiler_params=_params("parallel"),
    )(gl, gl, gl, abr, pbr, sbr)


def merge_bwd(gl, abr, pbr, sbr, dm, *, name):
    T, D = abr.shape
    tr = _row_tile(T)

    def body(g0, g1, g2, a_ref, p_ref, s_ref, dm_ref, da_ref, dp_ref, ds_ref, dg_ref):
        d = dm_ref[...].astype(F32)
        for j, (g_ref, b_ref, db_ref) in enumerate(((g0, a_ref, da_ref), (g1, p_ref, dp_ref), (g2, s_ref, ds_ref))):
            s = _sigmoid(g_ref[...].astype(F32))
            db_ref[...] = (d * s).astype(db_ref.dtype)
            dg_ref[:, j * D:(j + 1) * D] = (d * b_ref[...].astype(F32) * s * (1.0 - s)).astype(dg_ref.dtype)

    row = pl.BlockSpec((tr, D), lambda i: (i, 0))
    gs = [pl.BlockSpec((tr, D), lambda i, j=j: (i, j)) for j in range(3)]
    return pl.pallas_call(
        body, name=name, grid=(T // tr,), in_specs=gs + [row] * 4,
        out_specs=[row] * 3 + [pl.BlockSpec((tr, 3 * D), lambda i: (i, 0))],
        out_shape=[jax.ShapeDtypeStruct((T, D), MXU)] * 3 + [jax.ShapeDtypeStruct((T, 3 * D), MXU)],
        compiler_params=_params("parallel"),
    )(gl, gl, gl, abr, pbr, sbr, dm)


def gnorm_fwd(y, z, w, *, name):
    T, DI = y.shape
    gw = DI // C.G
    tr = _row_tile(T)

    def body(y_ref, z_ref, w_ref, o_ref):
        for g in range(C.G):
            sl = slice(g * gw, (g + 1) * gw)
            zz = z_ref[:, sl]
            v = y_ref[:, sl] * (zz * _sigmoid(zz))
            r = lax.rsqrt(jnp.mean(v * v, axis=-1, keepdims=True) + EPS)
            o_ref[:, sl] = (v * r * w_ref[:, sl]).astype(o_ref.dtype)

    row = pl.BlockSpec((tr, DI), lambda i: (i, 0))
    return pl.pallas_call(
        body, name=name, grid=(T // tr,), in_specs=[row, row, pl.BlockSpec((1, DI), lambda i: (0, 0))],
        out_specs=row, out_shape=jax.ShapeDtypeStruct((T, DI), MXU), compiler_params=_params("parallel"),
    )(y, z, w)


def gnorm_bwd(y, z, w, do, *, name):
    T, DI = y.shape
    gw = DI // C.G
    tr = _row_tile(T)

    def body(y_ref, z_ref, w_ref, do_ref, dy_ref, dz_ref, dw_ref):
        first = pl.program_id(0) == 0
        for g in range(C.G):
            sl = slice(g * gw, (g + 1) * gw)
            zz = z_ref[:, sl]
            yy = y_ref[:, sl]
            s = _sigmoid(zz)
            sz = zz * s
            v = yy * sz
            r = lax.rsqrt(jnp.mean(v * v, axis=-1, keepdims=True) + EPS)
            dov = do_ref[:, sl].astype(F32)
            gg = dov * w_ref[:, sl]
            dot = jnp.mean(gg * v, axis=-1, keepdims=True)
            dv = r * gg - v * (r * r * r * dot)
            dy_ref[:, sl] = dv * sz
            dz_ref[:, sl] = (dv * yy * s * (1.0 + zz * (1.0 - s))).astype(dz_ref.dtype)
            part = jnp.sum(dov * v * r, axis=0, keepdims=True)

            @pl.when(first)
            def _():
                dw_ref[:, sl] = part

            @pl.when(jnp.logical_not(first))
            def _():
                dw_ref[:, sl] += part

    row = pl.BlockSpec((tr, DI), lambda i: (i, 0))
    vec = pl.BlockSpec((1, DI), lambda i: (0, 0))
    return pl.pallas_call(
        body, name=name, grid=(T // tr,), in_specs=[row, row, vec, row], out_specs=[row, row, vec],
        out_shape=[jax.ShapeDtypeStruct((T, DI), F32), jax.ShapeDtypeStruct((T, DI), MXU),
                   jax.ShapeDtypeStruct((1, DI), F32)],
        compiler_params=_params("arbitrary"),
    )(y, z, w, do)


def adamw(w, g, m, v, *, name):
    shape = w.shape
    cols = shape[-1]
    rows = w.size // cols
    w2, g2, m2, v2 = (t.reshape(rows, cols) for t in (w, g, m, v))
    tr = rows if rows * cols * 4 <= (2 << 20) else _pick(rows, (512, 256, 128, 64, 32, 16, 8))
    while tr * cols * 4 > (2 << 20) and tr % 16 == 0:
        tr //= 2
    c1 = 1.0 - ADAM_B1 ** ADAM_STEP
    c2 = 1.0 - ADAM_B2 ** ADAM_STEP

    def body(w_ref, g_ref, m_ref, v_ref, d_ref, nm_ref, nv_ref):
        gv = g_ref[...]
        nm = ADAM_B1 * m_ref[...] + (1.0 - ADAM_B1) * gv
        nv = ADAM_B2 * v_ref[...] + (1.0 - ADAM_B2) * (gv * gv)
        d_ref[...] = -ADAM_LR * ((nm / c1) / (jnp.sqrt(nv / c2) + ADAM_EPS) + ADAM_WD * w_ref[...])
        nm_ref[...] = nm
        nv_ref[...] = nv

    row = pl.BlockSpec((tr, cols), lambda i: (i, 0))
    outs = pl.pallas_call(
        body, name=name, grid=(rows // tr,), in_specs=[row] * 4, out_specs=[row] * 3,
        out_shape=[jax.ShapeDtypeStruct((rows, cols), F32)] * 3, compiler_params=_params("parallel"),
    )(w2, g2, m2, v2)
    return tuple(o.reshape(shape) for o in outs)


def adamw_layers(w, g_layers, m, v, *, name):
    shape = w.shape
    L = shape[0]
    rows, cols = g_layers[0].shape
    w3, m3, v3 = (t.reshape(L, rows, cols) for t in (w, m, v))
    tr = _pick(rows, (512, 256, 128, 64, 32, 16, 8))
    while tr * cols * 4 > (2 << 20) and tr % 16 == 0:
        tr //= 2
    c1 = 1.0 - ADAM_B1 ** ADAM_STEP
    c2 = 1.0 - ADAM_B2 ** ADAM_STEP

    def body(*refs):
        w_ref, m_ref, v_ref = refs[0], refs[1], refs[2]
        g_refs = refs[3:3 + L]
        go_ref, d_ref, nm_ref, nv_ref = refs[3 + L:]
        layer = pl.program_id(0)
        gv = g_refs[0][...]
        for k in range(1, L):
            gv = jnp.where(layer == k, g_refs[k][...], gv)
        nm = ADAM_B1 * m_ref[...] + (1.0 - ADAM_B1) * gv
        nv = ADAM_B2 * v_ref[...] + (1.0 - ADAM_B2) * (gv * gv)
        d_ref[...] = -ADAM_LR * ((nm / c1) / (jnp.sqrt(nv / c2) + ADAM_EPS) + ADAM_WD * w_ref[...])
        go_ref[...] = gv
        nm_ref[...] = nm
        nv_ref[...] = nv

    blk = pl.BlockSpec((None, tr, cols), lambda l, i: (l, i, 0))
    gblks = [pl.BlockSpec((tr, cols), lambda l, i, k=k: (jnp.where(l == k, i, 0), 0)) for k in range(L)]
    outs = pl.pallas_call(
        body, name=name, grid=(L, rows // tr), in_specs=[blk] * 3 + gblks, out_specs=[blk] * 4,
        out_shape=[jax.ShapeDtypeStruct((L, rows, cols), F32)] * 4, compiler_params=_params("parallel", "parallel"),
    )(w3, m3, v3, *g_layers)
    return tuple(o.reshape(shape) for o in outs)


def _attn_masks(i, heads):
    B = C.blk
    row = lax.broadcasted_iota(jnp.int32, (heads * B, 2 * B), 0) & (B - 1)
    col = lax.broadcasted_iota(jnp.int32, (heads * B, 2 * B), 1)
    diff = row + B - col
    return (diff >= 0) & (diff < B) & ((col >= B) | (i > 0))


def _stack_heads(ref, kvh, ppk, lo):
    parts = []
    for pr in range(ppk):
        pair = kvh * ppk + pr
        qp = ref[:, pair * 128:(pair + 1) * 128].astype(F32)
        parts += [jnp.where(lo, qp, 0.0), jnp.where(lo, 0.0, qp)]
    return jnp.concatenate(parts, axis=0)


def _unstack_heads(ref, val, kvh, ppk, lo):
    B = C.blk
    for pr in range(ppk):
        pair = kvh * ppk + pr
        ref[:, pair * 128:(pair + 1) * 128] = jnp.where(lo, val[2 * pr * B:(2 * pr + 1) * B],
                                                        val[(2 * pr + 1) * B:(2 * pr + 2) * B]).astype(ref.dtype)


def _sink_scores(sink_ref, kvh, ppk):
    B = C.blk
    h0 = kvh * 2 * ppk
    return jnp.concatenate([jnp.full((B, 2 * B), sink_ref[0, h], F32) for h in range(h0, h0 + 2 * ppk)], axis=0)


def _kv2(prev_ref, cur_ref, m, lo):
    sl = slice(m * 128, (m + 1) * 128)
    slab = jnp.concatenate([prev_ref[:, sl], cur_ref[:, sl]], axis=0).astype(F32)
    slab = jnp.where(lax.broadcasted_iota(jnp.int32, (slab.shape[0], 1), 0) == 0, 0.0, slab)
    rolled = pltpu.roll(slab, 64, axis=1)
    return jnp.where(lo, slab, rolled), jnp.where(lo, rolled, slab)


def _attn_weights(q4, k2, mask, sink_scores):
    s = _dot_nt(q4, k2) * (C.hd ** -0.5)
    col0 = lax.broadcasted_iota(jnp.int32, (1, s.shape[1]), 1) == 0
    s = jnp.where(col0, sink_scores, jnp.where(mask, s, NEG))
    return jnp.exp(s - jnp.max(s, axis=-1, keepdims=True))


def _split_sums(v, sel):
    hi = v.astype(jnp.bfloat16)
    lo = (v - hi.astype(F32)).astype(jnp.bfloat16)
    sel2 = jnp.concatenate([sel, sel], axis=0).astype(jnp.bfloat16)
    return lax.dot_general(jnp.concatenate([hi, lo], axis=1), sel2, (((1,), (0,)), ((), ())), preferred_element_type=F32)


def _row_sums(p, exact=False):
    ones = jnp.ones((p.shape[1], 128), F32)
    return _split_sums(p, ones) if exact else _dot_nn(p, ones)


def attn_fwd(qkv, sink, *, name):
    T = qkv.shape[0]
    B, AW, KW = C.blk, C.AW, C.KW
    assert C.hd == 64 and KW % 128 == 0 and (C.qh // C.kvh) % 2 == 0
    nb = T // B
    kb = AW // KW
    ppk = C.qh // C.kvh // 2

    def body(q_ref, kc_ref, kp_ref, vc_ref, vp_ref, sink_ref, o_ref):
        i = pl.program_id(0)
        lo = lax.broadcasted_iota(jnp.int32, (1, 128), 1) < 64
        mask = _attn_masks(i, 2 * ppk)
        for m in range(KW // 128):
            k2s = _kv2(kp_ref, kc_ref, m, lo)
            v2s = _kv2(vp_ref, vc_ref, m, lo)
            for par in range(2):
                kvh = 2 * m + par
                q4 = _stack_heads(q_ref, kvh, ppk, lo)
                p = _attn_weights(q4, k2s[par], mask, _sink_scores(sink_ref, kvh, ppk))
                _unstack_heads(o_ref, _dot_nn(p, v2s[par]) / _row_sums(p), kvh, ppk, lo)

    prev = lambda i: jnp.maximum(i - 1, 0)
    return pl.pallas_call(
        body, name=name, grid=(nb,),
        in_specs=[pl.BlockSpec((B, AW), lambda i: (i, 0)),
                  pl.BlockSpec((B, KW), lambda i: (i, kb)), pl.BlockSpec((B, KW), lambda i: (prev(i), kb)),
                  pl.BlockSpec((B, KW), lambda i: (i, kb + 1)), pl.BlockSpec((B, KW), lambda i: (prev(i), kb + 1)),
                  pl.BlockSpec(memory_space=pltpu.SMEM)],
        out_specs=pl.BlockSpec((B, AW), lambda i: (i, 0)),
        out_shape=jax.ShapeDtypeStruct((T, AW), MXU), compiler_params=_params("parallel"),
    )(qkv, qkv, qkv, qkv, qkv, sink)


def attn_bwd(qkv, sink, dout, *, name):
    T = qkv.shape[0]
    B, AW, KW = C.blk, C.AW, C.KW
    nb = T // B
    kb = AW // KW
    ppk = C.qh // C.kvh // 2
    scale = C.hd ** -0.5

    def body(q_ref, kc_ref, kp_ref, vc_ref, vp_ref, sink_ref, do_ref, dq_ref, dk_ref, dv_ref, ds_ref, ck_ref, cv_ref):
        i = pl.program_id(0)

        @pl.when(i == 0)
        def _():
            ck_ref[...] = jnp.zeros_like(ck_ref)
            cv_ref[...] = jnp.zeros_like(cv_ref)
            ds_ref[...] = jnp.zeros_like(ds_ref)

        @pl.when(i < nb)
        def _():
            lane = lax.broadcasted_iota(jnp.int32, (1, 128), 1)
            lo = lane < 64
            hlane = lax.broadcasted_iota(jnp.int32, (1, C.qh), 1)
            mask = _attn_masks(i, 2 * ppk)
            dsink = jnp.zeros((1, C.qh), F32)
            for m in range(KW // 128):
                ksl = slice(m * 128, (m + 1) * 128)
                k2s = _kv2(kp_ref, kc_ref, m, lo)
                v2s = _kv2(vp_ref, vc_ref, m, lo)
                folded = []
                for par in range(2):
                    kvh = 2 * m + par
                    q4 = _stack_heads(q_ref, kvh, ppk, lo)
                    do4 = _stack_heads(do_ref, kvh, ppk, lo)
                    p = _attn_weights(q4, k2s[par], mask, _sink_scores(sink_ref, kvh, ppk))
                    inv = 1.0 / _row_sums(p)
                    p = p * jnp.concatenate([inv, inv], axis=1)
                    dp = _dot_nt(do4, v2s[par])
                    delta = _row_sums(p * dp, exact=True)
                    dsc = p * (dp - jnp.concatenate([delta, delta], axis=1))
                    _unstack_heads(dq_ref, _dot_nn(dsc, k2s[par]) * scale, kvh, ppk, lo)
                    dk2 = _dot_tn(dsc, q4) * scale
                    dv2 = _dot_tn(p, do4)
                    for k in range(2 * ppk):
                        dsh = jnp.sum(dsc[k * B:(k + 1) * B, :128], axis=0, keepdims=True)[:, :1]
                        dsink = dsink + jnp.where(hlane == kvh * 2 * ppk + k, dsh, 0.0)
                    folded.append((dk2 + pltpu.roll(dk2, 64, axis=1), dv2 + pltpu.roll(dv2, 64, axis=1)))
                row0 = lax.broadcasted_iota(jnp.int32, (2 * B, 1), 0) == 0
                dks = jnp.where(row0, 0.0, jnp.where(lo, folded[0][0], folded[1][0]))
                dvs = jnp.where(row0, 0.0, jnp.where(lo, folded[0][1], folded[1][1]))
                dk_ref[:, ksl] = (ck_ref[:, ksl] + dks[:B]).astype(dk_ref.dtype)
                dv_ref[:, ksl] = (cv_ref[:, ksl] + dvs[:B]).astype(dv_ref.dtype)
                ck_ref[:, ksl] = dks[B:]
                cv_ref[:, ksl] = dvs[B:]
            ds_ref[...] += dsink

        @pl.when(i == nb)
        def _():
            dk_ref[...] = ck_ref[...].astype(dk_ref.dtype)
            dv_ref[...] = cv_ref[...].astype(dv_ref.dtype)

    cur = lambda i: jnp.minimum(i, nb - 1)
    prev = lambda i: jnp.maximum(jnp.minimum(i, nb - 1) - 1, 0)
    out = lambda i: jnp.maximum(i - 1, 0)
    return pl.pallas_call(
        body, name=name, grid=(nb + 1,),
        in_specs=[pl.BlockSpec((B, AW), lambda i: (cur(i), 0)),
                  pl.BlockSpec((B, KW), lambda i: (cur(i), kb)), pl.BlockSpec((B, KW), lambda i: (prev(i), kb)),
                  pl.BlockSpec((B, KW), lambda i: (cur(i), kb + 1)), pl.BlockSpec((B, KW), lambda i: (prev(i), kb + 1)),
                  pl.BlockSpec(memory_space=pltpu.SMEM),
                  pl.BlockSpec((B, AW), lambda i: (cur(i), 0))],
        out_specs=[pl.BlockSpec((B, AW), lambda i: (cur(i), 0)),
                   pl.BlockSpec((B, KW), lambda i: (out(i), 0)), pl.BlockSpec((B, KW), lambda i: (out(i), 0)),
                   pl.BlockSpec((1, C.qh), lambda i: (0, 0))],
        out_shape=[jax.ShapeDtypeStruct((T, AW), MXU), jax.ShapeDtypeStruct((T, KW), MXU),
                   jax.ShapeDtypeStruct((T, KW), MXU), jax.ShapeDtypeStruct((1, C.qh), F32)],
        scratch_shapes=[pltpu.VMEM((B, KW), F32), pltpu.VMEM((B, KW), F32)],
        compiler_params=_params("arbitrary"),
    )(qkv, qkv, qkv, qkv, qkv, sink, dout)


POOL_HALO = 16


def _window_sum(e, w, n, forward):
    s, k = e, 1
    while k < w:
        s = s + pltpu.roll(s, (n - k) if forward else k, axis=0)
        k *= 2
    return s


def pool_fwd(u, pw, ps, *, name):
    T, PW = u.shape
    PG = C.PG
    tT = _pick(T, (256, 128))
    hb = tT // POOL_HALO

    def body(u_ref, uh_ref, pw_ref, ps_ref, o_ref, mx_ref):
        i = pl.program_id(0)
        halo = jnp.where(i > 0, uh_ref[...], 0.0)
        ext = jnp.concatenate([halo, u_ref[...]], axis=0)
        t = i * tT + lax.broadcasted_iota(jnp.int32, (tT, 1), 0)
        for g, w in enumerate(POOL_WINDOWS):
            sl = slice(g * PG, (g + 1) * PG)
            s = _window_sum(ext[:, sl], w, tT + POOL_HALO, False)[POOL_HALO:]
            cnt = jnp.minimum(t + 1, w).astype(F32)
            mixed = (s / cnt - u_ref[:, sl]).astype(MXU)
            mx_ref[:, sl] = mixed.astype(mx_ref.dtype)
            o_ref[:, sl] = (_dot_nn(mixed, pw_ref[g]) * ps_ref[:, sl]).astype(o_ref.dtype)

    row = pl.BlockSpec((tT, PW), lambda i: (i, 0))
    return pl.pallas_call(
        body, name=name, grid=(T // tT,),
        in_specs=[row, pl.BlockSpec((POOL_HALO, PW), lambda i: (jnp.maximum(i * hb - 1, 0), 0)),
                  pl.BlockSpec((4, PG, PG), lambda i: (0, 0, 0)), pl.BlockSpec((1, PW), lambda i: (0, 0))],
        out_specs=[row, row], out_shape=[jax.ShapeDtypeStruct((T, PW), MXU)] * 2,
        compiler_params=_params("parallel"),
    )(u, u, pw, ps)


def pool_bwd(dpool, mixed, pw, ps, *, name):
    T, PW = dpool.shape
    PG = C.PG
    tT = _pick(T, (256, 128))
    hb = tT // POOL_HALO
    n = T // tT
    rows = tT + POOL_HALO

    def body(dp_ref, dph_ref, mx_ref, pw_ref, ps_ref, du_ref, dpw_ref, dps_ref):
        i = pl.program_id(0)

        @pl.when(i == 0)
        def _():
            dpw_ref[...] = jnp.zeros_like(dpw_ref)
            dps_ref[...] = jnp.zeros_like(dps_ref)

        halo = jnp.where(i < n - 1, dph_ref[...].astype(F32), 0.0)
        dext = jnp.concatenate([dp_ref[...].astype(F32), halo], axis=0)
        t = i * tT + lax.broadcasted_iota(jnp.int32, (rows, 1), 0)
        for g, w in enumerate(POOL_WINDOWS):
            sl = slice(g * PG, (g + 1) * PG)
            dyg = dext[:, sl] * ps_ref[:, sl]
            dmix = _dot_nt(dyg, pw_ref[g])
            cnt = jnp.minimum(t + 1, w).astype(F32)
            s = _window_sum(dmix / cnt, w, rows, True)
            du_ref[:, sl] = (s[:tT] - dmix[:tT]).astype(du_ref.dtype)
            mb = mx_ref[:, sl]
            dpw_ref[g] += _dot_tn(mb, dyg[:tT])
            dps_ref[:, sl] += jnp.sum(dext[:tT, sl] * _dot_nn(mb, pw_ref[g]), axis=0, keepdims=True)

    row = pl.BlockSpec((tT, PW), lambda i: (i, 0))
    return pl.pallas_call(
        body, name=name, grid=(n,),
        in_specs=[row, pl.BlockSpec((POOL_HALO, PW), lambda i: (jnp.minimum((i + 1) * hb, T // POOL_HALO - 1), 0)), row,
                  pl.BlockSpec((4, PG, PG), lambda i: (0, 0, 0)), pl.BlockSpec((1, PW), lambda i: (0, 0))],
        out_specs=[row, pl.BlockSpec((4, PG, PG), lambda i: (0, 0, 0)), pl.BlockSpec((1, PW), lambda i: (0, 0))],
        out_shape=[jax.ShapeDtypeStruct((T, PW), MXU), jax.ShapeDtypeStruct((4, PG, PG), F32),
                   jax.ShapeDtypeStruct((1, PW), F32)],
        compiler_params=_params("arbitrary"),
    )(dpool, dpool, mixed, pw, ps)


CONV_HALO = 8
CONV_K = 4
CONV_TAP_ROWS = 16


def _conv(x, cw_ref, cb_ref, sl):
    acc = cb_ref[:, sl] + cw_ref[CONV_K - 1:CONV_K, sl] * x
    for k in range(CONV_K - 1):
        acc = acc + cw_ref[k:k + 1, sl] * pltpu.roll(x, CONV_K - 1 - k, axis=0)
    return acc


def conv_fwd(xbc, cw, cb, *, name):
    T, CC = xbc.shape
    tT = _pick(T, (256, 128))
    hb = tT // CONV_HALO
    cs = _pick(CC, (512, 256, 128))

    def body(x_ref, xh_ref, cw_ref, cb_ref, o_ref):
        i = pl.program_id(0)
        for c0 in range(0, CC, cs):
            sl = slice(c0, c0 + cs)
            ext = jnp.concatenate([jnp.where(i > 0, xh_ref[:, sl], 0.0), x_ref[:, sl]], axis=0)
            xc = _conv(ext, cw_ref, cb_ref, sl)[CONV_HALO:]
            o_ref[:, sl] = xc * _sigmoid(xc)

    row = pl.BlockSpec((tT, CC), lambda i: (i, 0))
    return pl.pallas_call(
        body, name=name, grid=(T // tT,),
        in_specs=[row, pl.BlockSpec((CONV_HALO, CC), lambda i: (jnp.maximum(i * hb - 1, 0), 0)),
                  pl.BlockSpec((CONV_K, CC), lambda i: (0, 0)), pl.BlockSpec((1, CC), lambda i: (0, 0))],
        out_specs=row, out_shape=jax.ShapeDtypeStruct((T, CC), F32), compiler_params=_params("parallel"),
    )(xbc, xbc, cw, cb)


def conv_bwd(xbc, cw, cb, dxa, *, name):
    T, CC = xbc.shape
    tT = _pick(T, (256, 128))
    hb = tT // CONV_HALO
    n = T // tT
    rows = tT + 2 * CONV_HALO
    cs = _pick(CC, (512, 256, 128))

    def body(x_ref, xp_ref, xn_ref, cw_ref, cb_ref, d_ref, dn_ref, dx_ref, dcw_ref, dcb_ref):
        i = pl.program_id(0)

        @pl.when(i == 0)
        def _():
            dcw_ref[...] = jnp.zeros_like(dcw_ref)
            dcb_ref[...] = jnp.zeros_like(dcb_ref)

        r = lax.broadcasted_iota(jnp.int32, (rows, 1), 0)
        own = (r >= CONV_HALO) & (r < tT + CONV_HALO)
        for c0 in range(0, CC, cs):
            sl = slice(c0, c0 + cs)
            x = jnp.concatenate([jnp.where(i > 0, xp_ref[:, sl], 0.0), x_ref[:, sl],
                                 jnp.where(i < n - 1, xn_ref[:, sl], 0.0)], axis=0)
            da = jnp.concatenate([jnp.zeros((CONV_HALO, cs), F32), d_ref[:, sl],
                                  jnp.where(i < n - 1, dn_ref[:, sl], 0.0)], axis=0)
            xc = _conv(x, cw_ref, cb_ref, sl)
            sg = _sigmoid(xc)
            dxc = da * sg * (1.0 + xc * (1.0 - sg))
            acc = cw_ref[CONV_K - 1:CONV_K, sl] * dxc
            for k in range(CONV_K - 1):
                acc = acc + cw_ref[k:k + 1, sl] * pltpu.roll(dxc, rows - (CONV_K - 1 - k), axis=0)
            dx_ref[:, sl] = acc[CONV_HALO:tT + CONV_HALO].astype(dx_ref.dtype)
            down = jnp.where(own, dxc, 0.0)
            dcb_ref[:, sl] += jnp.sum(down, axis=0, keepdims=True)
            dcw_ref[CONV_K - 1:CONV_K, sl] += jnp.sum(down * x, axis=0, keepdims=True)
            for k in range(CONV_K - 1):
                dcw_ref[k:k + 1, sl] += jnp.sum(down * pltpu.roll(x, CONV_K - 1 - k, axis=0), axis=0, keepdims=True)

    row = pl.BlockSpec((tT, CC), lambda i: (i, 0))
    prev = pl.BlockSpec((CONV_HALO, CC), lambda i: (jnp.maximum(i * hb - 1, 0), 0))
    nxt = pl.BlockSpec((CONV_HALO, CC), lambda i: (jnp.minimum((i + 1) * hb, T // CONV_HALO - 1), 0))
    return pl.pallas_call(
        body, name=name, grid=(n,),
        in_specs=[row, prev, nxt, pl.BlockSpec((CONV_K, CC), lambda i: (0, 0)), pl.BlockSpec((1, CC), lambda i: (0, 0)),
                  row, nxt],
        out_specs=[row, pl.BlockSpec((CONV_K, CC), lambda i: (0, 0)), pl.BlockSpec((1, CC), lambda i: (0, 0))],
        out_shape=[jax.ShapeDtypeStruct((T, CC), MXU), jax.ShapeDtypeStruct((CONV_K, CC), F32),
                   jax.ShapeDtypeStruct((1, CC), F32)],
        compiler_params=_params("arbitrary"),
    )(xbc, xbc, xbc, cw, cb, dxa, dxa)


def _softplus(x):
    return jnp.maximum(x, 0.0) + jnp.log(1.0 + jnp.exp(-jnp.abs(x)))


def _dot_exact(a, b):
    return lax.dot_general(a, b, (((1,), (0,)), ((), ())), precision=lax.Precision.HIGHEST, preferred_element_type=F32)


def ssd_prep(raw_t, bias, alog, *, name):
    H, T = raw_t.shape
    B = C.blk
    tc = _pick(T, (4 * B, 2 * B, B))

    def body(r_ref, b_ref, al_ref, dt_ref, acs_ref):
        dt = _softplus(r_ref[...] + b_ref[...])
        dt_ref[...] = dt
        dta = dt * (-jnp.exp(al_ref[...]))
        upper = (lax.broadcasted_iota(jnp.int32, (B, B), 0) <= lax.broadcasted_iota(jnp.int32, (B, B), 1)).astype(F32)
        for j in range(tc // B):
            acs_ref[:, j * B:(j + 1) * B] = _dot_exact(dta[:, j * B:(j + 1) * B], upper)

    blk = pl.BlockSpec((H, tc), lambda i: (0, i))
    vec = pl.BlockSpec((H, 1), lambda i: (0, 0))
    return pl.pallas_call(
        body, name=name, grid=(T // tc,), in_specs=[blk, vec, vec], out_specs=[blk, blk],
        out_shape=[jax.ShapeDtypeStruct((H, T), F32)] * 2, compiler_params=_params("parallel"),
    )(raw_t, bias, alog)


def _pair(lo, arr, h0, rows=slice(None)):
    return jnp.where(lo, arr[rows, h0:h0 + 1], arr[rows, h0 + 1:h0 + 2])


def _decay(acs, acs_t, h, causal):
    return jnp.exp(jnp.where(causal, acs[:, h:h + 1] - acs_t[h:h + 1, :], NEG))


def ssd_fwd(xa, dt, acs, acs_t, dskip, *, name):
    T = xa.shape[0]
    B, DI, G, N, HG, P, H = C.blk, C.DI, C.G, C.N, C.HG, C.P, C.H
    assert P == 64 and HG % 2 == 0
    nc = T // B
    W = HG * P

    def body(xa_ref, dt_ref, acs_ref, acst_ref, ds_ref, y_ref, hp_ref, h_scr):
        @pl.when(pl.program_id(0) == 0)
        def _():
            h_scr[...] = jnp.zeros_like(h_scr)

        lo = lax.broadcasted_iota(jnp.int32, (1, 128), 1) < 64
        causal = lax.broadcasted_iota(jnp.int32, (B, B), 0) >= lax.broadcasted_iota(jnp.int32, (B, B), 1)
        dt, acs, acs_t, dsk = dt_ref[...], acs_ref[...], acst_ref[...], ds_ref[...]
        for g in range(G):
            bg = xa_ref[:, DI + g * N:DI + (g + 1) * N]
            cg = xa_ref[:, DI + (G + g) * N:DI + (G + g + 1) * N]
            cb = _dot_nt(cg, bg)
            hg = h_scr[g]
            hp_ref[0, g * N:(g + 1) * N, :] = hg
            yoff = _dot_nn(cg, hg)
            xws, decs = [], []
            for j in range(HG // 2):
                h0 = g * HG + 2 * j
                xsl = slice(h0 * P, (h0 + 2) * P)
                xp = xa_ref[:, xsl]
                ap = _pair(lo, acs, h0)
                alast = _pair(lo, acs, h0, slice(B - 1, B))
                xdt = xp * _pair(lo, dt, h0)
                ys = [_dot_nn(cb * _decay(acs, acs_t, h0 + half, causal), xdt) for half in range(2)]
                y_ref[:, xsl] = (jnp.where(lo, ys[0], ys[1]) + yoff[:, 2 * j * P:(2 * j + 2) * P] * jnp.exp(ap)
                                 + _pair(lo, dsk, h0) * xp)
                xws.append(xdt * jnp.exp(alast - ap))
                decs.append(jnp.exp(alast))
            h_scr[g] = hg * jnp.concatenate(decs, axis=1) + _dot_tn(bg, jnp.concatenate(xws, axis=1))

    tok = lambda w: pl.BlockSpec((B, w), lambda c: (c, 0))
    return pl.pallas_call(
        body, name=name, grid=(nc,),
        in_specs=[tok(C.CC), tok(H), tok(H), pl.BlockSpec((H, B), lambda c: (0, c)), pl.BlockSpec((1, H), lambda c: (0, 0))],
        out_specs=[tok(DI), pl.BlockSpec((1, G * N, W), lambda c: (c, 0, 0))],
        out_shape=[jax.ShapeDtypeStruct((T, DI), F32), jax.ShapeDtypeStruct((nc, G * N, W), F32)],
        scratch_shapes=[pltpu.VMEM((G, N, W), F32)],
        compiler_params=_params("arbitrary"),
    )(xa, dt, acs, acs_t, dskip)


def ssd_bwd(xa, dt, acs, acs_t, dskip, hprev, dy, *, name):
    T = xa.shape[0]
    B, DI, G, N, HG, P, H = C.blk, C.DI, C.G, C.N, C.HG, C.P, C.H
    nc = T // B
    W = HG * P

    def body(xa_ref, dt_ref, acs_ref, acst_ref, ds_ref, hp_ref, dy_ref, dxa_ref, ddt_ref, dacs_ref, dd_ref, dh_scr):
        @pl.when(pl.program_id(0) == 0)
        def _():
            dh_scr[...] = jnp.zeros_like(dh_scr)
            dd_ref[...] = jnp.zeros_like(dd_ref)

        lo = lax.broadcasted_iota(jnp.int32, (1, 128), 1) < 64
        hi = jnp.logical_not(lo)
        causal = lax.broadcasted_iota(jnp.int32, (B, B), 0) >= lax.broadcasted_iota(jnp.int32, (B, B), 1)
        hlane = lax.broadcasted_iota(jnp.int32, (1, H), 1)
        hsub = lax.broadcasted_iota(jnp.int32, (B, 1), 0)
        lastrow = lax.broadcasted_iota(jnp.int32, (B, 1), 0) == B - 1
        dt, acs, acs_t, dsk = dt_ref[...], acs_ref[...], acst_ref[...], ds_ref[...]
        d_acs = jnp.zeros((B, H), F32)
        d_acs_t = jnp.zeros((B, B), F32)
        d_dt = jnp.zeros((B, H), F32)
        d_d = jnp.zeros((1, H), F32)

        lane_r = lax.broadcasted_iota(jnp.int32, (128, 128), 0)
        lane_c = lax.broadcasted_iota(jnp.int32, (128, 128), 1)
        same_half = ((lane_r < 64) == (lane_c < 64)).astype(F32)
        ones_sq = jnp.ones((128, 128), F32)

        def rsum(v):
            r = _split_sums(v, same_half)
            return r[:, 0:1], r[:, 64:65]

        for g in range(G):
            bsl = slice(DI + g * N, DI + (g + 1) * N)
            csl = slice(DI + (G + g) * N, DI + (G + g + 1) * N)
            bg, cg = xa_ref[:, bsl], xa_ref[:, csl]
            cb = _dot_nt(cg, bg)
            hg = hp_ref[0, g * N:(g + 1) * N, :]
            dhg = dh_scr[g]
            yoff = _dot_nn(cg, hg)
            bds = _dot_nn(bg, dhg)
            d_cb = jnp.zeros((B, B), F32)
            dyes, xws, decs = [], [], []
            for j in range(HG // 2):
                h0 = g * HG + 2 * j
                xsl = slice(h0 * P, (h0 + 2) * P)
                psl = slice(2 * j * P, (2 * j + 2) * P)
                xp, dyp = xa_ref[:, xsl], dy_ref[:, xsl]
                dtp = _pair(lo, dt, h0)
                ap = _pair(lo, acs, h0)
                alast = _pair(lo, acs, h0, slice(B - 1, B))
                ea, ew, el = jnp.exp(ap), jnp.exp(alast - ap), jnp.exp(alast)
                xdt = xp * dtp
                halves = []
                for half in range(2):
                    h = h0 + half
                    lm = _decay(acs, acs_t, h, causal)
                    m = cb * lm
                    d_m = _dot_nt(jnp.where(lo if half == 0 else hi, dyp, 0.0), xdt)
                    d_cb = d_cb + d_m * lm
                    wgt = d_m * m
                    d_acs = d_acs + jnp.where(hlane == h, _split_sums(wgt, ones_sq)[:, 0:1], 0.0)
                    w_hi = wgt.astype(jnp.bfloat16)
                    w_lo = (wgt - w_hi.astype(F32)).astype(jnp.bfloat16)
                    col = _dot_tn(jnp.ones((2 * B, 8), jnp.bfloat16), jnp.concatenate([w_hi, w_lo], axis=0))
                    d_acs_t = d_acs_t + jnp.where(hsub == h, col[0:1, :], 0.0)
                    halves.append(_dot_tn(m, dyp))
                bdp = bds[:, psl]
                dxdt = jnp.where(lo, halves[0], halves[1]) + ew * bdp
                dxa_ref[:, xsl] = dtp * dxdt + _pair(lo, dsk, h0) * dyp
                xw = xdt * ew
                terms_dt = rsum(dxdt * xp)
                terms_dd = rsum(dyp * xp)
                terms_off = rsum(dyp * (ea * yoff[:, psl]))
                terms_e = rsum(xw * bdp)
                terms_h = rsum(hg[:, psl] * dhg[:, psl])
                for half in range(2):
                    h = h0 + half
                    sel = hlane == h
                    d_dt = d_dt + jnp.where(sel, terms_dt[half], 0.0)
                    d_d = d_d + jnp.where(sel, jnp.sum(terms_dd[half], axis=0, keepdims=True), 0.0)
                    e_last = jnp.sum(jnp.where(lo if half == 0 else hi, el, 0.0), axis=1, keepdims=True) * (1.0 / P)
                    d_last = (jnp.sum(terms_e[half], axis=0, keepdims=True)
                              + e_last * jnp.sum(terms_h[half], axis=0, keepdims=True))
                    d_acs = d_acs + jnp.where(sel, terms_off[half] - terms_e[half] + jnp.where(lastrow, d_last, 0.0), 0.0)
                dyes.append(dyp * ea)
                xws.append(xw)
                decs.append(el)
            dye = jnp.concatenate(dyes, axis=1)
            xwc = jnp.concatenate(xws, axis=1)
            dxa_ref[:, csl] = _dot_nn(d_cb, bg) + _dot_nt(dye, hg)
            dxa_ref[:, bsl] = _dot_tn(d_cb, cg) + _dot_nt(xwc, dhg)
            dh_scr[g] = dhg * jnp.concatenate(decs, axis=1) + _dot_tn(cg, dye)
        ddt_ref[...] = d_dt
        dacs_ref[...] = d_acs - d_acs_t.T[:, :H]
        dd_ref[...] += d_d

    rev = lambda w: pl.BlockSpec((B, w), lambda c: (nc - 1 - c, 0))
    vec = pl.BlockSpec((1, H), lambda c: (0, 0))
    return pl.pallas_call(
        body, name=name, grid=(nc,),
        in_specs=[rev(C.CC), rev(H), rev(H), pl.BlockSpec((H, B), lambda c: (0, nc - 1 - c)), vec,
                  pl.BlockSpec((1, G * N, W), lambda c: (nc - 1 - c, 0, 0)), rev(DI)],
        out_specs=[rev(C.CC), rev(H), rev(H), vec],
        out_shape=[jax.ShapeDtypeStruct((T, C.CC), F32), jax.ShapeDtypeStruct((T, H), F32),
                   jax.ShapeDtypeStruct((T, H), F32), jax.ShapeDtypeStruct((1, H), F32)],
        scratch_shapes=[pltpu.VMEM((G, N, W), F32)],
        compiler_params=_params("arbitrary"),
    )(xa, dt, acs, acs_t, dskip, hprev, dy)


def ssd_post(ddt, dacs, dt, raw, bias, alog, *, name):
    T, H = ddt.shape
    B = C.blk
    tc = _pick(T, (4 * B, 2 * B, B))

    def body(ddt_ref, dacs_ref, dt_ref, raw_ref, b_ref, al_ref, draw_ref, db_ref, dal_ref):
        @pl.when(pl.program_id(0) == 0)
        def _():
            db_ref[...] = jnp.zeros_like(db_ref)
            dal_ref[...] = jnp.zeros_like(dal_ref)

        a = -jnp.exp(al_ref[...])
        lower = (lax.broadcasted_iota(jnp.int32, (B, B), 0) <= lax.broadcasted_iota(jnp.int32, (B, B), 1)).astype(F32)
        for j in range(tc // B):
            sl = slice(j * B, (j + 1) * B)
            rc = _dot_exact(lower, dacs_ref[sl, :])
            dtv = dt_ref[sl, :]
            draw = (ddt_ref[sl, :] + a * rc) * _sigmoid(raw_ref[sl, :] + b_ref[...])
            draw_ref[sl, :] = draw
            db_ref[...] += jnp.sum(draw, axis=0, keepdims=True)
            dal_ref[...] += jnp.sum(dtv * rc, axis=0, keepdims=True) * a

    blk = pl.BlockSpec((tc, H), lambda i: (i, 0))
    vec = pl.BlockSpec((1, H), lambda i: (0, 0))
    return pl.pallas_call(
        body, name=name, grid=(T // tc,), in_specs=[blk, blk, blk, blk, vec, vec], out_specs=[blk, vec, vec],
        out_shape=[jax.ShapeDtypeStruct((T, H), F32), jax.ShapeDtypeStruct((1, H), F32), jax.ShapeDtypeStruct((1, H), F32)],
        compiler_params=_params("arbitrary"),
    )(ddt, dacs, dt, raw, bias, alog)


MESH = pl.DeviceIdType.MESH
PACK_W = 1024
ANY = pl.BlockSpec(memory_space=pl.ANY)


def _place():
    return lax.axis_index("x"), lax.axis_index("y"), lax.axis_index("c")


def _other_chips(x, y):
    return [(1 - x, y), (x, 1 - y), (1 - x, 1 - y)]


def _remote(src, dst, send_sems, recv_sems, k, to):
    return pltpu.make_async_remote_copy(src_ref=src, dst_ref=dst, send_sem=send_sems.at[k], recv_sem=recv_sems.at[k],
                                        device_id=to, device_id_type=MESH)


def _half(c, rows):
    rh = rows // 2
    return pl.ds(pl.multiple_of(c * rh, 16 if rh % 16 == 0 else 8), rh)


def gather_layer(shards, *, name):
    n = len(shards)

    def body(*refs):
        ps, gs = refs[:n], refs[n:2 * n]
        send_sems, recv_sems = refs[2 * n:]
        x, y, c = _place()
        s = 2 * x + y
        sib = (x, y, 1 - c)
        chips = _other_chips(x, y)
        copy = functools.partial(_remote, send_sems=send_sems, recv_sems=recv_sems)
        started = []
        for w in range(n):
            started.append(copy(ps[w], gs[w].at[s], k=w, to=sib))
        for w in range(n):
            mine = _half(c, ps[w].shape[0])
            for j, (px, py) in enumerate(chips):
                started.append(copy(ps[w].at[mine], gs[w].at[s, mine], k=n + 3 * w + j, to=(px, py, c)))
        for cp in started:
            cp.start()
        passed = []
        for w in range(n):
            mine = _half(c, ps[w].shape[0])
            for j, (px, py) in enumerate(chips):
                there = gs[w].at[2 * px + py, mine]
                copy(there, there, k=n + 3 * w + j, to=(px, py, c)).wait_recv()
                fw = copy(there, there, k=4 * n + 3 * w + j, to=sib)
                fw.start()
                passed.append(fw)
        for w in range(n):
            copy(ps[w], gs[w].at[s], k=w, to=sib).wait_recv()
            theirs = _half(1 - c, ps[w].shape[0])
            for j, (px, py) in enumerate(chips):
                there = gs[w].at[2 * px + py, theirs]
                copy(there, there, k=4 * n + 3 * w + j, to=sib).wait_recv()
        for cp in started + passed:
            cp.wait_send()

    return pl.pallas_call(
        body, name=name, in_specs=[ANY] * n, out_specs=[ANY] * n,
        out_shape=[jax.ShapeDtypeStruct((4,) + p.shape, p.dtype) for p in shards],
        scratch_shapes=[pltpu.SemaphoreType.DMA((7 * n,)), pltpu.SemaphoreType.DMA((7 * n,))],
    )(*shards)


HBM = pl.BlockSpec(memory_space=pltpu.HBM)
SEMS = pl.BlockSpec(memory_space=pltpu.SEMAPHORE)


def gather_plan(x, y, c, ps, gs):
    s = 2 * x + y
    sends, lands = [], []
    for p, g in zip(ps, gs):
        rows = p.shape[0]
        sends.append((p, g.at[s], (x, y, 1 - c)))
        lands.append(g.at[s])
        for px, py in _other_chips(x, y):
            for pc in (c, 1 - c):
                sends.append((p.at[_half(c, rows)], g.at[s, _half(c, rows)], (px, py, pc)))
                lands.append(g.at[2 * px + py, _half(pc, rows)])
    return sends, lands


def scatter_plan(x, y, c, ps, gs):
    sends, lands = [], []
    for p, g in zip(ps, gs):
        for j, (px, py) in enumerate(_other_chips(x, y)):
            sends.append((p.at[2 * px + py], g.at[j], (px, py, c)))
            lands.append(g.at[j])
    return sends, lands


def _hbm(a):
    return pltpu.with_memory_space_constraint(a, pltpu.HBM)


def exchange_start(srcs, land_shapes, plan, ncopies, *, name):
    n = len(srcs)

    def body(*refs):
        ps, gs = refs[:n], refs[n:2 * n]
        send_sems, recv_sems = refs[2 * n], refs[2 * n + 1]
        token = refs[-1]
        sends, _ = plan(*_place(), ps, gs)
        for k, (src, dst, to) in enumerate(sends):
            _remote(src, dst, send_sems, recv_sems, k, to).start()
        token[...] = jnp.zeros_like(token)

    lands = [_hbm(lax.empty(s.shape, s.dtype)) for s in land_shapes]
    outs = pl.pallas_call(
        body, name=name,
        out_shape=[pltpu.SemaphoreType.DMA((ncopies,)), pltpu.SemaphoreType.DMA((ncopies,))]
        + [pltpu.HBM(a.shape, a.dtype) for a in srcs] + [pltpu.HBM(s.shape, s.dtype) for s in land_shapes]
        + [jax.ShapeDtypeStruct((8, 128), F32)],
        in_specs=[HBM] * (2 * n), out_specs=[SEMS, SEMS] + [HBM] * (2 * n) + [pl.BlockSpec(memory_space=pltpu.VMEM)],
        input_output_aliases={k: 2 + k for k in range(2 * n)},
        compiler_params=pltpu.CompilerParams(has_side_effects=pltpu.SideEffectType.DATAFLOW_SIDE_EFFECTING),
    )(*[_hbm(a) for a in srcs], *lands)
    return outs[0], outs[1], list(outs[2:2 + n]), list(outs[2 + n:2 + 2 * n]), outs[-1]


def exchange_wait(send_sems, recv_sems, srcs, lands, plan, after, *, name, items=None):
    n = len(srcs)
    items = list(range(n)) if items is None else items

    def body(*refs):
        ps, gs = refs[:n], refs[n:2 * n]
        send_sems, recv_sems = refs[2 * n], refs[2 * n + 1]
        sends, arrivals = plan(*_place(), ps, gs)
        per_item = len(sends) // n
        for j, ((src, dst, to), land) in enumerate(zip(sends, arrivals)):
            k = per_item * items[j // per_item] + j % per_item
            _remote(src, dst, send_sems, recv_sems, k, to).wait_send()
            _remote(land, land, send_sems, recv_sems, k, to).wait_recv()

    outs = pl.pallas_call(
        body, name=name,
        out_shape=[pltpu.HBM(a.shape, a.dtype) for a in srcs] + [pltpu.HBM(a.shape, a.dtype) for a in lands],
        in_specs=[HBM] * (2 * n) + [SEMS, SEMS, ANY], out_specs=[HBM] * (2 * n),
        input_output_aliases={k: k for k in range(2 * n)},
        compiler_params=pltpu.CompilerParams(has_side_effects=pltpu.SideEffectType.DATAFLOW_SIDE_EFFECTING),
    )(*srcs, *lands, send_sems, recv_sems, after)
    return list(outs[n:])


def swap_halves(items, *, name):
    n = len(items)

    def body(*refs):
        gs, rs = refs[:n], refs[n:2 * n]
        send_sems, recv_sems = refs[2 * n:]
        x, y, c = _place()
        cps = [_remote(g.at[:, _half(1 - c, g.shape[1])], r, send_sems, recv_sems, k, (x, y, 1 - c))
               for k, (g, r) in enumerate(zip(gs, rs))]
        for cp in cps:
            cp.start()
        for cp in cps:
            cp.wait()

    return pl.pallas_call(
        body, name=name, in_specs=[ANY] * n, out_specs=[ANY] * n,
        out_shape=[jax.ShapeDtypeStruct((4, g.shape[1] // 2, g.shape[2]), g.dtype) for g in items],
        scratch_shapes=[pltpu.SemaphoreType.DMA((n,)), pltpu.SemaphoreType.DMA((n,))],
    )(*items)


def join_halves(items, *, name):
    n = len(items)

    def body(*refs):
        rs, outs = refs[:n], refs[n:2 * n]
        send_sems, recv_sems = refs[2 * n:]
        x, y, c = _place()
        sib = (x, y, 1 - c)
        cps = []
        for k, (r, o) in enumerate(zip(rs, outs)):
            mine = _half(c, r.shape[0])
            cps.append(_remote(r.at[mine], o.at[mine], send_sems, recv_sems, k, sib))
            cps[-1].start()
        for k, (r, o) in enumerate(zip(rs, outs)):
            theirs = _half(1 - c, r.shape[0])
            _remote(r.at[theirs], o.at[theirs], send_sems, recv_sems, k, sib).wait_recv()
        for cp in cps:
            cp.wait_send()

    return pl.pallas_call(
        body, name=name, in_specs=[ANY] * n, out_specs=[ANY] * n,
        out_shape=[jax.ShapeDtypeStruct(r.shape, r.dtype) for r in items],
        input_output_aliases={k: k for k in range(n)},
        scratch_shapes=[pltpu.SemaphoreType.DMA((n,)), pltpu.SemaphoreType.DMA((n,))],
    )(*items)


def gather_all(v, *, name, total):
    rows, W = v.shape

    def body(v_ref, o_ref, *scr):
        buf = scr[0] if total else o_ref
        send_sems, recv_sems = scr[-2], scr[-1]
        x, y, c = _place()
        me = 4 * x + 2 * y + c
        flips = [(k >> 2 & 1, k >> 1 & 1, k & 1) for k in range(1, 8)]
        peers = [((1 - x) if fx else x, (1 - y) if fy else y, (1 - c) if fc else c) for fx, fy, fc in flips]
        out = []
        for k, peer in enumerate(peers):
            cp = pltpu.make_async_remote_copy(src_ref=v_ref, dst_ref=buf.at[me], send_sem=send_sems.at[k],
                                              recv_sem=recv_sems.at[k], device_id=peer, device_id_type=MESH)
            cp.start()
            out.append(cp)
        buf[me] = v_ref[...]
        for k, (px, py, pc) in enumerate(peers):
            pltpu.make_async_remote_copy(src_ref=v_ref, dst_ref=buf.at[4 * px + 2 * py + pc], send_sem=send_sems.at[k],
                                         recv_sem=recv_sems.at[k], device_id=(px, py, pc), device_id_type=MESH).wait_recv()
        for cp in out:
            cp.wait_send()
        if total:
            acc = buf[0]
            for d in range(1, 8):
                acc = acc + buf[d]
            o_ref[...] = acc

    vm = pl.BlockSpec(memory_space=pltpu.VMEM)
    return pl.pallas_call(
        body, name=name, in_specs=[vm], out_specs=vm,
        out_shape=jax.ShapeDtypeStruct((rows, W) if total else (8, rows, W), F32),
        scratch_shapes=([pltpu.VMEM((8, rows, W), F32)] if total else [])
        + [pltpu.SemaphoreType.DMA((7,)), pltpu.SemaphoreType.DMA((7,))],
    )(v)


def pair_sum(g, r, *, name):
    _, R, W = g.shape
    Rh = R // 2
    tr = _pick(Rh, (512, 256, 128, 64, 32, 16))
    nb = Rh // tr

    def body(g_ref, r_ref, o_ref):
        o_ref[...] = (g_ref[...].astype(F32) + r_ref[...].astype(F32)).astype(o_ref.dtype)

    return pl.pallas_call(
        body, name=name, grid=(4, nb),
        in_specs=[pl.BlockSpec((1, tr, W), lambda s, i: (s, lax.axis_index("c") * nb + i, 0)),
                  pl.BlockSpec((1, tr, W), lambda s, i: (s, i, 0))],
        out_specs=pl.BlockSpec((1, tr, W), lambda s, i: (s, i, 0)),
        out_shape=jax.ShapeDtypeStruct((4, Rh, W), g.dtype), compiler_params=_params("parallel", "parallel"),
    )(g, r)


def chip_sum(g, r1, r2, *, name):
    _, R, W = g.shape
    Rh = R // 2
    tr = _pick(Rh, (512, 256, 128, 64, 32, 16))
    nb = Rh // tr

    def body(g_ref, r1_ref, a_ref, b_ref, c_ref, o_ref):
        acc = g_ref[0].astype(F32) + r1_ref[0].astype(F32)
        for ref in (a_ref, b_ref, c_ref):
            acc = acc + ref[0].astype(F32)
        o_ref[...] = acc

    shard = lambda: 2 * lax.axis_index("x") + lax.axis_index("y")
    half = lambda i: lax.axis_index("c") * nb + i
    return pl.pallas_call(
        body, name=name, grid=(nb,),
        in_specs=[pl.BlockSpec((1, tr, W), lambda i: (shard(), half(i), 0)),
                  pl.BlockSpec((1, tr, W), lambda i: (shard(), i, 0))]
        + [pl.BlockSpec((1, tr, W), lambda i, j=j: (j, i, 0)) for j in range(3)],
        out_specs=pl.BlockSpec((tr, W), lambda i: (half(i), 0)),
        out_shape=jax.ShapeDtypeStruct((R, W), F32), compiler_params=_params("parallel"),
    )(g, r1, r2, r2, r2)


BIG = (("w_in", False), ("pool_w", False), ("w_attn_br", False), ("w_pool_br", False), ("w_ssm_br", True),
       ("w_out", True), ("w_gate_up", False), ("w_down", True))


def _join(piece, axis):
    t = jnp.moveaxis(piece, 0, axis)
    shp = t.shape
    return t.reshape(shp[:axis] + (shp[axis] * shp[axis + 1],) + shp[axis + 2:])


def _seg_bounds():
    aw, kw, _, pw, di, cc, h, gd = C.in_widths
    o = [0, aw + 2 * kw]
    for wdt in (pw, di, cc, h, gd):
        o.append(o[-1] + wdt)
    return o


IN_PAD = 640


def _in_segments(blocks):
    w_in = jnp.concatenate([blocks[s] for s in range(4)], axis=1)
    o = _seg_bounds()
    segs = [w_in[:, o[i]:o[i + 1]] for i in range(6)]
    segs[4] = jnp.pad(segs[4], ((0, 0), (0, 128 - C.H)))
    width = sum(t.shape[1] for t in segs)
    pad = -width % IN_PAD
    return segs, jnp.concatenate(segs + [jnp.zeros((w_in.shape[0], pad), w_in.dtype)], axis=1)


def _layer_fwd_mix(x, p, tag):
    nm = lambda s: f"{s}_{tag}"
    H = C.H
    h = rms_fwd(x, p["ln1_w"], name=nm("rms1"))
    wq, wu, wz, wx, wd, wg = p["in_segs"]
    qkv = matmul(h, wq, name=nm("mm_qkv"), out_dtype=MXU)
    u = matmul(h, wu, name=nm("mm_u"))
    z = matmul(h, wz, name=nm("mm_z"))
    xbc = matmul(h, wx, name=nm("mm_xbc"))
    dtp = matmul(h, wd, name=nm("mm_dt"))
    gl = matmul(h, wg, name=nm("mm_gate"), out_dtype=MXU)
    att = attn_fwd(qkv, p["attn_sink"], name=nm("attn_fwd"))
    pool, mixed = pool_fwd(u, p["pool_w"], p["pool_scale"], name=nm("pool_fwd"))
    xa = conv_fwd(xbc, p["conv_w"], p["conv_b"], name=nm("conv_fwd"))
    raw = dtp[:, :H]
    dt_t, acs_t = ssd_prep(raw.T, p["dt_bias"].T, p["a_log"].T, name=nm("ssd_prep"))
    dt, acs = dt_t.T, acs_t.T
    y, hprev = ssd_fwd(xa, dt, acs, acs_t, p["d_skip"], name=nm("ssd_fwd"))
    ssm = gnorm_fwd(y, z, p["ssm_norm_w"], name=nm("gnorm_fwd"))
    return dict(x=x, h=h, qkv=qkv, z=z, xbc=xbc, raw=raw, gl=gl, att=att, pool=pool, mixed=mixed, xa=xa, dt=dt, acs=acs,
                acs_t=acs_t, y=y, hprev=hprev, ssm=ssm)


def _layer_fwd_out(s, p, tag):
    nm = lambda t: f"{t}_{tag}"
    x, gl, att, pool, ssm = s["x"], s["gl"], s["att"], s["pool"], s["ssm"]
    abr = matmul(att, name=nm("mm_abr"), out_dtype=MXU, **p["w_attn_br"])
    pbr = matmul(pool, name=nm("mm_pbr"), out_dtype=MXU, **p["w_pool_br"])
    sbr = matmul(ssm, name=nm("mm_sbr"), out_dtype=MXU, **p["w_ssm_br"])
    merged = merge_fwd(gl, abr, pbr, sbr, name=nm("merge_fwd"))
    xm = matmul(merged, add=x, name=nm("mm_out"), **p["w_out"])
    h2 = rms_fwd(xm, p["ln2_w"], name=nm("rms2"))
    gu = matmul(h2, name=nm("mm_gu"), out_dtype=MXU, **p["w_gate_up"])
    act = swiglu_fwd(gu, name=nm("swiglu_fwd"))
    xo = matmul(act, add=xm, name=nm("mm_down"), **p["w_down"])
    return xo, dict(s, abr=abr, pbr=pbr, sbr=sbr, merged=merged, xm=xm, h2=h2, gu=gu, act=act)


def _layer_bwd(dxo, dxo_m, p, s, tag):
    nm = lambda t: f"{t}_{tag}"
    H = C.H
    g = {}

    def rows4(t):
        return t.reshape(4, t.shape[0] // 4, t.shape[1])

    g["w_down"] = rows4(matmul(s["act"], dxo_m, ta=True, out_dtype=MXU, name=nm("mmg_down")))
    dact = matmul(dxo_m, tb=True, out_dtype=MXU, name=nm("mmb_down"), **p["w_down"])
    dgu = swiglu_bwd(s["gu"], dact, name=nm("swiglu_bwd"))
    g["w_gate_up"] = matmul(s["h2"], dgu, ta=True, out_dtype=MXU, out_cols=True, name=nm("mmg_gu"))
    dh2 = matmul(dgu, tb=True, name=nm("mmb_gu"), **p["w_gate_up"])
    dxm, dxm_m, g["ln2_w"] = rms_bwd(s["xm"], p["ln2_w"], dh2, dxo, name=nm("rms2_bwd"))
    g["w_out"] = rows4(matmul(s["merged"], dxm_m, ta=True, out_dtype=MXU, name=nm("mmg_out")))
    dmerged = matmul(dxm_m, tb=True, out_dtype=MXU, name=nm("mmb_out"), **p["w_out"])
    dabr, dpbr, dsbr, dgl = merge_bwd(s["gl"], s["abr"], s["pbr"], s["sbr"], dmerged, name=nm("merge_bwd"))
    g["w_attn_br"] = matmul(s["att"], dabr, ta=True, out_dtype=MXU, out_cols=True, name=nm("mmg_abr"))
    g["w_pool_br"] = matmul(s["pool"], dpbr, ta=True, out_dtype=MXU, out_cols=True, name=nm("mmg_pbr"))
    g["w_ssm_br"] = rows4(matmul(s["ssm"], dsbr, ta=True, out_dtype=MXU, name=nm("mmg_sbr")))
    datt = matmul(dabr, tb=True, out_dtype=MXU, name=nm("mmb_abr"), **p["w_attn_br"])
    dpool = matmul(dpbr, tb=True, out_dtype=MXU, name=nm("mmb_pbr"), **p["w_pool_br"])
    dssm = matmul(dsbr, tb=True, out_dtype=MXU, name=nm("mmb_sbr"), **p["w_ssm_br"])
    dq, dk, dv, g["attn_sink"] = attn_bwd(s["qkv"], p["attn_sink"], datt, name=nm("attn_bwd"))
    du, dpw, g["pool_scale"] = pool_bwd(dpool, s["mixed"], p["pool_w"], p["pool_scale"], name=nm("pool_bwd"))
    pg = dpw.shape[1] // 4
    g["pool_w"] = jnp.moveaxis(dpw.reshape(4, 4, pg, dpw.shape[2]), 1, 0).reshape(4, 4 * pg, dpw.shape[2]).astype(MXU)
    dy, dz, g["ssm_norm_w"] = gnorm_bwd(s["y"], s["z"], p["ssm_norm_w"], dssm, name=nm("gnorm_bwd"))
    dxa, ddt, dacs, g["d_skip"] = ssd_bwd(s["xa"], s["dt"], s["acs"], s["acs_t"], p["d_skip"], s["hprev"], dy,
                                          name=nm("ssd_bwd"))
    draw, g["dt_bias"], g["a_log"] = ssd_post(ddt, dacs, s["dt"], s["raw"], p["dt_bias"], p["a_log"], name=nm("ssd_post"))
    dxbc, g["conv_w"], g["conv_b"] = conv_bwd(s["xbc"], p["conv_w"], p["conv_b"], dxa, name=nm("conv_bwd"))
    return g, dict(dq=dq, dk=dk, dv=dv, du=du, dz=dz, dxbc=dxbc, draw=draw, dgl=dgl, dxm=dxm)


def _layer_bwd_in(c, p, s, tag):
    nm = lambda t: f"{t}_{tag}"
    H = C.H
    dq, dk, dv, du, dz, dxbc, draw, dgl, dxm = (c[k] for k in ("dq", "dk", "dv", "du", "dz", "dxbc", "draw", "dgl", "dxm"))
    w_all = p["in_all"]
    ddtp = jnp.pad(draw, ((0, 0), (0, 128 - H))).astype(MXU)
    parts = [dq, dk, dv, du, dz, dxbc, ddtp, dgl]
    used = sum(t.shape[1] for t in parts)
    dproj = jnp.concatenate(parts + [jnp.zeros((dq.shape[0], w_all.shape[1] - used), MXU)], axis=1)
    dh = matmul(dproj, w_all, tb=True, name=nm("mmb_in"))
    g_all = matmul(s["h"], dproj, ta=True, out_dtype=MXU, name=nm("mmg_in"))
    o = _seg_bounds()
    dt0 = o[4]
    g_in = jnp.concatenate([g_all[:, :dt0 + H], g_all[:, dt0 + 128:dt0 + 128 + (o[6] - o[5])]], axis=1)
    nc = g_in.shape[1] // 4
    g_w_in = jnp.stack([g_in[:, k * nc:(k + 1) * nc] for k in range(4)])
    dx, dx_m, g_ln1 = rms_bwd(s["x"], p["ln1_w"], dh, dxm, name=nm("rms1_bwd"))
    return dx, dx_m, g_w_in, g_ln1


SMALL = ("ln1_w", "attn_sink", "conv_w", "conv_b", "dt_bias", "a_log", "d_skip", "ssm_norm_w", "pool_scale", "ln2_w")
WEIGHTS = ("ln1_w", "w_in", "attn_sink", "conv_w", "conv_b", "dt_bias", "a_log", "d_skip", "ssm_norm_w", "pool_w",
           "pool_scale", "w_attn_br", "w_pool_br", "w_ssm_br", "w_out", "ln2_w", "w_gate_up", "w_down", "final_w")


def _step(x, loss_target, w, m, v):
    depth = C.depth
    xi, yi, ci = _place()
    shard = 2 * xi + yi

    names = [n for n, _ in BIG]
    nbig = len(names)
    cw_cols = w["conv_w"].shape[2]

    def block2d(n, i):
        return w[n][i].astype(MXU).reshape(-1, w[n].shape[-1])

    def mix_shards(i):
        return [block2d("w_in", i), block2d("pool_w", i), w["conv_w"][i].reshape(CONV_TAP_ROWS, -1)]

    def out_shards(i):
        return [block2d(n, i) for n, _ in BIG[2:]]

    def mix_params(i, blocks):
        p = {n: w[n][i][None] for n in SMALL if n != "conv_w"}
        p["in_segs"], p["in_all"] = _in_segments(blocks[0])
        pw = blocks[1]
        p["pool_w"] = _join(pw.reshape(4, 4, pw.shape[1] // 4, pw.shape[2]), 1)
        p["conv_w"] = _join(blocks[2].reshape(4, CONV_K, cw_cols), 1)
        return p

    def out_params(blocks):
        return {n: dict(b=t.reshape(4 * t.shape[1], t.shape[2])) if rw else dict(b=t, b_cols=True)
                for (n, rw), t in zip(BIG[2:], blocks)}

    def gather_start(src, tag):
        return exchange_start(src, [jax.ShapeDtypeStruct((4,) + a.shape, a.dtype) for a in src], gather_plan,
                              7 * len(src), name=f"gather_{tag}_start")

    xs = x[0]
    layers, saved = [], []
    n_mix = len(mix_shards(0))
    mix_blocks = gather_layer(mix_shards(0), name="gather_mix_l0")

    def after(src, scalar):
        return [src[0] + jnp.minimum(jnp.abs(scalar), 0.0).astype(src[0].dtype)] + src[1:]

    pend_out = gather_start(after(out_shards(0), mix_blocks[2][0, 0, 0]), "out_l0")
    started = pend_out[4][0, 0]
    for i in range(depth):
        nxt = None
        if i + 1 < depth:
            prev = pend_out[4][0, 0] if i == 0 else xs[0, 0]
            nxt = gather_start(after(mix_shards(i + 1) + out_shards(i + 1), prev), f"l{i + 1}")
            started = started + nxt[4][0, 0]
        p = mix_params(i, mix_blocks)
        p["ln1_w"] = p["ln1_w"] + started
        started = jnp.zeros((), F32)
        half = _layer_fwd_mix(xs, p, f"l{i}")
        ss, rs, srcs, lands, _ = pend_out
        items = None if i == 0 else list(range(n_mix, len(srcs) + n_mix))
        p.update(out_params(exchange_wait(ss, rs, srcs, lands, gather_plan, half["ssm"], items=items,
                                          name=f"gather_out_l{i}_wait")))
        layers.append(p)
        xs, sv = _layer_fwd_out(half, p, f"l{i}")
        saved.append(sv)
        if nxt is not None:
            ss, rs, srcs, lands, _ = nxt
            mix_blocks = exchange_wait(ss, rs, srcs[:n_mix], lands[:n_mix], gather_plan, xs, items=list(range(n_mix)),
                                       name=f"gather_mix_l{i + 1}_wait")
            pend_out = (ss, rs, srcs[n_mix:], lands[n_mix:], None)
    loss_part, dx, dx_m, g_final = final_loss(xs, w["final_w"][None], loss_target[0], name="final_loss")

    def pair_sums(i, ns, gs):
        sb = swap_halves(gs, name=f"swap_halves_{ns[0]}_l{i}")
        return sb, [pair_sum(g, r, name=f"pair_sum_{n}_l{i}") for n, g, r in zip(ns, gs, sb)]

    def scatter_start(chip, tag):
        return exchange_start(chip, [jax.ShapeDtypeStruct((3,) + a.shape[1:], a.dtype) for a in chip], scatter_plan,
                              3 * len(chip), name=f"scatter_{tag}_start")

    def scatter_wait(pend, after, tag):
        return exchange_wait(pend[0], pend[1], pend[2], pend[3], scatter_plan, after, name=f"scatter_{tag}_wait")

    grads = [None] * depth
    pieces = {}
    rest = names[1:]
    pend_in = None
    for i in reversed(range(depth)):
        if pend_in is not None:
            dx_m = dx_m + pend_in[0][4][0, 0].astype(dx_m.dtype)
        grads[i], cot = _layer_bwd(dx, dx_m, layers[i], saved[i], f"l{i}")
        g_rest = [grads[i][n] for n in rest]
        sib_rest, chip_rest = pair_sums(i, rest, g_rest)
        pend_rest = scatter_start(chip_rest, f"rest_l{i}")
        cot["draw"] = cot["draw"] + pend_rest[4][0, 0]
        dx, dx_m, g_in, grads[i]["ln1_w"] = _layer_bwd_in(cot, layers[i], saved[i], f"l{i}")
        if pend_in is not None:
            pend, g_up, sib_up = pend_in
            pieces[(i + 1, "w_in")] = (g_up, sib_up[0], scatter_wait(pend, dx, f"in_l{i + 1}")[0])
        for n, g, r, o in zip(rest, g_rest, sib_rest, scatter_wait(pend_rest, dx, f"rest_l{i}")):
            pieces[(i, n)] = (g, r, o)
        sib_in, chip_in = pair_sums(i, ["w_in"], [g_in])
        pend_in = (scatter_start(chip_in, f"in_l{i}"), g_in, sib_in)
    keys = [(i, n) for i in range(depth) for n in names if (i, n) != (0, "w_in")]
    g_first, r_first, o_first = pieces[keys[0]]
    pieces[keys[0]] = (g_first + pend_in[0][4][0, 0].astype(g_first.dtype), r_first, o_first)
    mine = [chip_sum(*pieces[k], name=f"chip_sum_{k[1]}_l{k[0]}") for k in keys]
    reduced = dict(zip(keys, join_halves(mine, name="join_halves")))
    gout = {}

    small = [jnp.stack([grads[i][n].reshape(-1) for i in range(depth)]).reshape(1, -1) for n in SMALL]
    small += [g_final.reshape(1, -1), loss_part.reshape(1, -1)]
    tot = gather_all(_pack_small(small), name="sum_small", total=True)
    parts = _unpack_small(tot, [t.shape[1] for t in small])
    for n, t in zip(SMALL, parts):
        if n == "conv_w":
            cols = w[n].shape[2]
            gout[n] = lax.dynamic_slice_in_dim(t.reshape(depth, CONV_K, -1), shard * cols, cols, axis=2)
        else:
            gout[n] = t.reshape(w[n].shape)
    gout["final_w"] = parts[-2].reshape(w["final_w"].shape)
    loss = parts[-1].reshape(())

    upd = {}
    for n in [t for t in WEIGHTS if t != "w_in"] + ["w_in"]:
        if n == "w_in":
            pend, g_in, sib_in = pend_in
            landed = scatter_wait(pend, upd["w_down"][1], "in_l0")[0]
            mine_in = chip_sum(g_in, sib_in[0], landed, name="chip_sum_w_in_l0")
            reduced[(0, n)] = join_halves([mine_in], name="join_halves_w_in_l0")[0]
        if n in gout:
            upd[n] = (gout[n].reshape(w[n].shape),) + adamw(w[n], gout[n].reshape(w[n].shape), m[n], v[n], name=f"adamw_{n}")
        else:
            upd[n] = adamw_layers(w[n], [reduced[(i, n)] for i in range(depth)], m[n], v[n], name=f"adamw_{n}")
    return (loss, dx[None], *[upd[n][0] for n in WEIGHTS], *[upd[n][1] for n in WEIGHTS],
            *[upd[n][2] for n in WEIGHTS], *[upd[n][3] for n in WEIGHTS])


def _pack_small(parts):
    flat = jnp.concatenate(parts, axis=1)
    rows = -(-flat.shape[1] // PACK_W)
    rows = -(-rows // 8) * 8
    return jnp.pad(flat, ((0, 0), (0, rows * PACK_W - flat.shape[1]))).reshape(rows, PACK_W)


def _unpack_small(buf, sizes):
    flat = buf.reshape(-1)
    out, off = [], 0
    for n in sizes:
        out.append(flat[off:off + n])
        off += n
    return out


def kernel(x, ln1_w, w_in, attn_sink, conv_w, conv_b, dt_bias, a_log, d_skip, ssm_norm_w, pool_w, pool_scale, w_attn_br, w_pool_br, w_ssm_br, w_out, ln2_w, w_gate_up, w_down, final_w, loss_target, m_ln1_w, m_w_in, m_attn_sink, m_conv_w, m_conv_b, m_dt_bias, m_a_log, m_d_skip, m_ssm_norm_w, m_pool_w, m_pool_scale, m_w_attn_br, m_w_pool_br, m_w_ssm_br, m_w_out, m_ln2_w, m_w_gate_up, m_w_down, m_final_w, v_ln1_w, v_w_in, v_attn_sink, v_conv_w, v_conv_b, v_dt_bias, v_a_log, v_d_skip, v_ssm_norm_w, v_pool_w, v_pool_scale, v_w_attn_br, v_w_pool_br, v_w_ssm_br, v_w_out, v_ln2_w, v_w_gate_up, v_w_down, v_final_w):
    w = dict(ln1_w=ln1_w, w_in=w_in, attn_sink=attn_sink, conv_w=conv_w, conv_b=conv_b, dt_bias=dt_bias, a_log=a_log,
             d_skip=d_skip, ssm_norm_w=ssm_norm_w, pool_w=pool_w, pool_scale=pool_scale, w_attn_br=w_attn_br,
             w_pool_br=w_pool_br, w_ssm_br=w_ssm_br, w_out=w_out, ln2_w=ln2_w, w_gate_up=w_gate_up, w_down=w_down,
             final_w=final_w)
    m = dict(ln1_w=m_ln1_w, w_in=m_w_in, attn_sink=m_attn_sink, conv_w=m_conv_w, conv_b=m_conv_b, dt_bias=m_dt_bias,
             a_log=m_a_log, d_skip=m_d_skip, ssm_norm_w=m_ssm_norm_w, pool_w=m_pool_w, pool_scale=m_pool_scale,
             w_attn_br=m_w_attn_br, w_pool_br=m_w_pool_br, w_ssm_br=m_w_ssm_br, w_out=m_w_out, ln2_w=m_ln2_w,
             w_gate_up=m_w_gate_up, w_down=m_w_down, final_w=m_final_w)
    v = dict(ln1_w=v_ln1_w, w_in=v_w_in, attn_sink=v_attn_sink, conv_w=v_conv_w, conv_b=v_conv_b, dt_bias=v_dt_bias,
             a_log=v_a_log, d_skip=v_d_skip, ssm_norm_w=v_ssm_norm_w, pool_w=v_pool_w, pool_scale=v_pool_scale,
             w_attn_br=v_w_attn_br, w_pool_br=v_w_pool_br, w_ssm_br=v_w_ssm_br, w_out=v_w_out, ln2_w=v_ln2_w,
             w_gate_up=v_w_gate_up, w_down=v_w_down, final_w=v_final_w)
    return _step(x, loss_target, w, m, v)
```
